```python
import math
import jax
import jax.numpy as jnp
from jax import lax
import numpy as np

D_MODEL = 1024
BATCH = 8
SEQ = 2048
DEPTH = 4
DEC_BATCH = 32
DEC_SEQ = 1
PAST_LEN = 8192
PAGE_SIZE = 128

N_MIXERS = 4
PLE_DIM = 256
D_FF = 2816
ALPHA = (2 * DEPTH) ** 0.25
BETA = (8 * DEPTH) ** -0.25
LN_EPS = 1e-5

CHUNK = 128
GM_WIDTH = 2 * D_MODEL
GM_GROUPS = 8
GM_GROUP_DIM = GM_WIDTH // GM_GROUPS

SSM_D_INNER = 2 * D_MODEL
SSM_HEAD_DIM = 64
SSM_HEADS = SSM_D_INNER // SSM_HEAD_DIM
SSM_GROUPS = 4
SSM_HPG = SSM_HEADS // SSM_GROUPS
SSM_STATE = 128
SSM_CONV = 4
SSM_CONV_DIM = SSM_D_INNER + 2 * SSM_GROUPS * SSM_STATE
SSM_IN_DIM = SSM_D_INNER + SSM_CONV_DIM + SSM_HEADS
SSM_CHUNK = 128

ATT_HEADS = 16
ATT_KV_HEADS = 4
ATT_HEAD_DIM = D_MODEL // ATT_HEADS
ROPE_DIM = ATT_HEAD_DIM // 4
ROPE_THETA = 500000.0
IDX_HEADS = 8
IDX_DIM = 64
IDX_ROPE_DIM = IDX_DIM // 4
TOPK_MAX = 256
Q_BLOCK = 128
ATT_Q_DIM = ATT_HEADS * ATT_HEAD_DIM
ATT_KV_DIM = ATT_KV_HEADS * ATT_HEAD_DIM
ATT_IN_SPLITS = (ATT_Q_DIM, ATT_Q_DIM + ATT_KV_DIM, ATT_Q_DIM + 2 * ATT_KV_DIM,
                 ATT_Q_DIM + 2 * ATT_KV_DIM + IDX_HEADS * IDX_DIM,
                 ATT_Q_DIM + 2 * ATT_KV_DIM + IDX_HEADS * IDX_DIM + IDX_DIM)
ATT_IN_DIM = ATT_IN_SPLITS[-1] + IDX_HEADS

RW_HEAD = 64
RW_HEADS = D_MODEL // RW_HEAD
RW_DECAY_LORA = 64
RW_AAA_LORA = 64
RW_GATE_LORA = 128
RW_GN_EPS = 64e-5

kernel_name = 'hybrid_interleaved_decoder_step'


def layer_norm(x, g, b):
    xf = x.astype(jnp.float32)
    mu = jnp.mean(xf, -1, keepdims=True)
    var = jnp.mean(jnp.square(xf - mu), -1, keepdims=True)
    return ((xf - mu) * lax.rsqrt(var + LN_EPS)).astype(x.dtype) * g + b


def post_norm(x, f, g, b):
    return layer_norm(ALPHA * x + f, g, b)


def swiglu(x, w_up, w_down):
    a, b = jnp.split(x @ w_up, 2, axis=-1)
    return (jax.nn.silu(a) * b) @ w_down


def ple_add(x, p, w_p, w_g, b_g):
    return x + jax.nn.sigmoid(x @ w_g + b_g) * (p @ w_p)


def rope_partial(x, pos, rot_dim):
    half = rot_dim // 2
    inv = ROPE_THETA ** (-jnp.arange(half, dtype=jnp.float32) / half)
    ang = pos.astype(jnp.float32)[:, None] * inv[None, :]
    cos = jnp.cos(ang)[:, None, :]
    sin = jnp.sin(ang)[:, None, :]
    xf = x[..., :rot_dim].astype(jnp.float32)
    x1, x2 = xf[..., :half], xf[..., half:]
    rot = jnp.concatenate([x1 * cos - x2 * sin, x2 * cos + x1 * sin], axis=-1).astype(x.dtype)
    return jnp.concatenate([rot, x[..., rot_dim:]], axis=-1)


def gather_rows(rows, idx):
    return jax.vmap(lambda r, i: r[i])(rows, idx)


def gmlp_mixer(x, w_in, ln_g, ln_b, ws, bs, w_out):
    B_, T, _ = x.shape
    u, v = jnp.split(jax.nn.gelu(x @ w_in), 2, axis=-1)
    v = layer_norm(v, ln_g, ln_b)
    l = min(T, CHUNK)
    c = T // l
    mask = jnp.tril(jnp.ones((l, l), dtype=bool))
    w = jnp.where(mask, ws[:, :l, :l], 0.0)
    vc = v.reshape(B_, c, l, GM_GROUPS, GM_GROUP_DIM)
    mixed = jnp.einsum('gts,bcsgd->bctgd', w, vc) + jnp.transpose(bs[:, :l])[:, :, None]
    return (u * mixed.reshape(B_, T, GM_WIDTH)) @ w_out, v


def ssd_chunked(xs, dt, a, bm, cm, h0):
    B_, T = xs.shape[:2]
    l = min(T, SSM_CHUNK)
    c = T // l
    blk = lambda t: t.reshape((B_, c, l) + t.shape[2:])
    xdt = blk(xs.astype(jnp.float32) * dt[..., None])
    bc, cc, acum = blk(bm), blk(cm), jnp.cumsum(blk(a), axis=2)
    at = jnp.moveaxis(acum, 2, -1)
    causal = jnp.tril(jnp.ones((l, l), dtype=bool))
    seg = jnp.exp(jnp.where(causal, at[..., :, None] - at[..., None, :], -jnp.inf))
    cb = jnp.einsum('bctgn,bcsgn->bcgts', cc, bc)
    y_diag = jnp.einsum('bcgts,bcgets,bcsgep->bctgep', cb, seg, xdt)
    states = jnp.einsum('bclgn,bclge,bclgep->bcgepn', bc, jnp.exp(acum[:, :, -1:] - acum), xdt)
    chunk_decay = jnp.exp(acum[:, :, -1])

    def step(h, inp):
        dec, st = inp
        return h * dec[..., None, None] + st, h

    h_last, h_in = lax.scan(step, h0, (jnp.moveaxis(chunk_decay, 1, 0), jnp.moveaxis(states, 1, 0)))
    y_off = jnp.einsum('bctgn,bcgepn,bctge->bctgep', cc, jnp.moveaxis(h_in, 0, 1), jnp.exp(acum))
    return (y_diag + y_off).reshape(B_, T, SSM_GROUPS, SSM_HPG, SSM_HEAD_DIM), h_last


def mamba2_mixer(x, conv_state, ssm_state, w_in, conv_w, conv_b, dt_bias, a_log, d_skip, norm_g, w_out):
    B_, T, _ = x.shape
    z, xbc, dt = jnp.split(x @ w_in, [SSM_D_INNER, SSM_D_INNER + SSM_CONV_DIM], axis=-1)
    xbc_ext = jnp.concatenate([conv_state, xbc], axis=1)
    conv = conv_b
    for j in range(SSM_CONV):
        conv = conv + xbc_ext[:, j:j + T] * conv_w[j]
    xbc = jax.nn.silu(conv)
    xs, bm, cm = jnp.split(xbc, [SSM_D_INNER, SSM_D_INNER + SSM_GROUPS * SSM_STATE], axis=-1)
    xs = xs.reshape(B_, T, SSM_GROUPS, SSM_HPG, SSM_HEAD_DIM)
    bm = bm.reshape(B_, T, SSM_GROUPS, SSM_STATE)
    cm = cm.reshape(B_, T, SSM_GROUPS, SSM_STATE)
    dt = jax.nn.softplus((dt + dt_bias).astype(jnp.float32)).reshape(B_, T, SSM_GROUPS, SSM_HPG)
    a_neg = -jnp.exp(a_log.astype(jnp.float32)).reshape(SSM_GROUPS, SSM_HPG)
    h0 = ssm_state.astype(jnp.float32).reshape(B_, SSM_GROUPS, SSM_HPG, SSM_HEAD_DIM, SSM_STATE)
    y, h_last = ssd_chunked(xs, dt, dt * a_neg, bm, cm, h0)
    y = y.astype(x.dtype) + xs * d_skip.reshape(SSM_GROUPS, SSM_HPG, 1)
    yg = (y.reshape(B_, T, SSM_D_INNER) * jax.nn.silu(z)).reshape(B_, T, SSM_GROUPS, -1).astype(jnp.float32)
    yg = (yg * lax.rsqrt(jnp.mean(jnp.square(yg), -1, keepdims=True) + LN_EPS)).astype(x.dtype)
    yg = yg.reshape(B_, T, SSM_D_INNER) * norm_g
    new_ssm = h_last.reshape(B_, SSM_HEADS, SSM_HEAD_DIM, SSM_STATE).astype(ssm_state.dtype)
    return yg @ w_out, xbc_ext[:, T:], new_ssm


def dsa_project(x, pos, w_in, kn_g, kn_b):
    B_, T, _ = x.shape
    q, k, v, iq, ik, iw = jnp.split(x @ w_in, list(ATT_IN_SPLITS), axis=-1)
    q = rope_partial(q.reshape(B_, T, ATT_HEADS, ATT_HEAD_DIM), pos, ROPE_DIM)
    k = rope_partial(k.reshape(B_, T, ATT_KV_HEADS, ATT_HEAD_DIM), pos, ROPE_DIM)
    v = v.reshape(B_, T, ATT_KV_HEADS, ATT_HEAD_DIM)
    iq = rope_partial(iq.reshape(B_, T, IDX_HEADS, IDX_DIM), pos, IDX_ROPE_DIM)
    ik = rope_partial(layer_norm(ik, kn_g, kn_b)[:, :, None, :], pos, IDX_ROPE_DIM)[:, :, 0, :]
    iw = iw * (IDX_HEADS ** -0.5 * IDX_DIM ** -0.5)
    return q, k, v, iq, ik, iw


def dsa_select(iq, iw, ik, qpos, topk):
    s = jnp.einsum('bqhd,bsd->bqhs', iq, ik)
    score = jnp.einsum('bqh,bqhs->bqs', iw, jax.nn.relu(s)).astype(jnp.float32)
    adm = jnp.arange(ik.shape[1])[None, :] <= qpos[:, None]
    score = jnp.where(adm[None], score, -jnp.inf)
    _, idx = lax.top_k(score, topk)
    return idx, idx <= qpos[None, :, None]


def sparse_attend(q, k_sel, v_sel, valid):
    B_, Q = q.shape[:2]
    qg = q.reshape(B_, Q, ATT_KV_HEADS, ATT_HEADS // ATT_KV_HEADS, ATT_HEAD_DIM)
    s = jnp.einsum('bqhgd,bqkhd->bqhgk', qg, k_sel).astype(jnp.float32) * (ATT_HEAD_DIM ** -0.5)
    s = jnp.where(valid[:, :, None, None, :], s, -jnp.inf)
    p = jax.nn.softmax(s, axis=-1).astype(v_sel.dtype)
    o = jnp.einsum('bqhgk,bqkhd->bqhgd', p, v_sel)
    return o.reshape(B_, Q, ATT_Q_DIM)


def dsa_prompt(x, w_in, kn_g, kn_b, w_out):
    B_, T, _ = x.shape
    q, k, v, iq, ik, iw = dsa_project(x, jnp.arange(T), w_in, kn_g, kn_b)
    topk = min(TOPK_MAX, T // 4)

    def block(bi):
        t0 = bi * Q_BLOCK
        sl = lambda t: lax.dynamic_slice_in_dim(t, t0, Q_BLOCK, axis=1)
        qpos = t0 + jnp.arange(Q_BLOCK)
        idx, valid = dsa_select(sl(iq), sl(iw), ik, qpos, topk)
        return sparse_attend(sl(q), gather_rows(k, idx), gather_rows(v, idx), valid)

    o = lax.map(block, jnp.arange(T // Q_BLOCK))
    o = jnp.moveaxis(o, 0, 1).reshape(B_, T, ATT_Q_DIM)
    return o @ w_out, k, v, ik


def dsa_sample(x, cache_k, cache_v, cache_idx_k, page_table, w_in, kn_g, kn_b, w_out):
    B_, T, _ = x.shape
    past = page_table.shape[1] * cache_k.shape[1]
    pos = past + jnp.arange(T)
    q, k, v, iq, ik, iw = dsa_project(x, pos, w_in, kn_g, kn_b)
    ik_all = jnp.concatenate([cache_idx_k[page_table].reshape(B_, past, IDX_DIM), ik], axis=1)
    idx, valid = dsa_select(iq, iw, ik_all, pos, min(TOPK_MAX, (past + T) // 4))
    past_idx = jnp.minimum(idx, past - 1)
    phys = jnp.take_along_axis(page_table, (past_idx // PAGE_SIZE).reshape(B_, -1), axis=1).reshape(idx.shape)
    off = past_idx % PAGE_SIZE
    new_idx = jnp.clip(idx - past, 0, T - 1)
    is_new = (idx >= past)[..., None, None]
    k_sel = jnp.where(is_new, gather_rows(k, new_idx), cache_k[phys, off])
    v_sel = jnp.where(is_new, gather_rows(v, new_idx), cache_v[phys, off])
    o = sparse_attend(q, k_sel, v_sel, valid)
    return o @ w_out, k, v, ik


def rwkv7_mixer(x, shift, wkv, mu, w_r, w_k, w_v, w_o, w0, w1, w2, a0, a1, a2, g1, g2,
                k_k, k_a, r_k, gn_g, gn_b):
    B_, T, _ = x.shape
    x_prev = jnp.concatenate([shift[:, None, :], x[:, :-1]], axis=1)
    xm = x[None] + (x_prev - x)[None] * mu[:, None, None, :]
    xr, xw, xk, xv, xa, xg = xm
    r = xr @ w_r
    w_log = -jax.nn.softplus(-(w0 + jnp.tanh(xw @ w1) @ w2)) - 0.5
    k = xk @ w_k
    v = xv @ w_v
    a = jax.nn.sigmoid(a0 + (xa @ a1) @ a2)
    g = jax.nn.sigmoid(xg @ g1) @ g2
    heads = lambda t: t.reshape(B_, T, RW_HEADS, RW_HEAD)
    kk = heads(k * k_k).astype(jnp.float32)
    kk = kk / jnp.maximum(jnp.sqrt(jnp.sum(kk * kk, -1, keepdims=True)), 1e-12)
    k = k * (1.0 + (a - 1.0) * k_a)
    decay = jnp.exp(-jnp.exp(w_log.astype(jnp.float32)))
    r, k, v, a, decay = heads(r), heads(k), heads(v), heads(a), heads(decay)
    seq = tuple(jnp.moveaxis(t.astype(jnp.float32), 1, 0) for t in (r, decay, k, v, kk, kk * a))

    def step(s, inp):
        r_t, d_t, k_t, v_t, kk_t, b_t = inp
        sa = jnp.einsum('bhij,bhj->bhi', s, kk_t)
        s = s * d_t[:, :, None, :] - sa[..., None] * b_t[:, :, None, :] + v_t[..., None] * k_t[:, :, None, :]
        return s, jnp.einsum('bhij,bhj->bhi', s, r_t)

    s_last, y = lax.scan(step, wkv.astype(jnp.float32), seq)
    y = jnp.moveaxis(y, 0, 1)
    mu_y = jnp.mean(y, -1, keepdims=True)
    var_y = jnp.mean(jnp.square(y - mu_y), -1, keepdims=True)
    yn = ((y - mu_y) * lax.rsqrt(var_y + RW_GN_EPS)).reshape(B_, T, D_MODEL).astype(x.dtype) * gn_g + gn_b
    bonus = (jnp.sum(r * k * r_k, -1, keepdims=True) * v).reshape(B_, T, D_MODEL)
    return ((yn + bonus) * g) @ w_o, x[:, -1], s_last.astype(wkv.dtype)


def setup_inputs(seed: int = 0) -> dict:
    key = jax.random.key(seed)
    keys = jax.random.split(key, 96)
    counter = [0]

    def nk():
        counter[0] += 1
        return keys[counter[0] - 1]

    f32 = jnp.float32

    def nrm(shape, scale=1.0):
        return jax.random.normal(nk(), shape, f32) * scale

    def gain(shape):
        return 1.0 + nrm(shape, 0.02)

    D = D_MODEL
    n_pages = PAST_LEN // PAGE_SIZE
    n_pool = (5 * DEC_BATCH * n_pages) // 4
    page_table = jax.random.permutation(nk(), n_pool)[: DEC_BATCH * n_pages].reshape(DEC_BATCH, n_pages).astype(jnp.int32)

    ssm_w_in = nrm((D, SSM_IN_DIM), D ** -0.5)
    ssm_w_in = ssm_w_in.at[:, -SSM_HEADS:].multiply(0.1)
    dt0 = jnp.exp(jax.random.uniform(nk(), (SSM_HEADS,), f32, math.log(1e-3), math.log(1e-1)))

    return {
        'x_prompt': nrm((BATCH, SEQ, D)),
        'x_sample': nrm((DEC_BATCH, DEC_SEQ, D)),
        'state_ssm_conv': nrm((DEC_BATCH, SSM_CONV - 1, SSM_CONV_DIM)),
        'state_ssm': nrm((DEC_BATCH, SSM_HEADS, SSM_HEAD_DIM, SSM_STATE), 0.1),
        'cache_k': nrm((n_pool, PAGE_SIZE, ATT_KV_HEADS, ATT_HEAD_DIM)),
        'cache_v': nrm((n_pool, PAGE_SIZE, ATT_KV_HEADS, ATT_HEAD_DIM)),
        'cache_idx_k': nrm((n_pool, PAGE_SIZE, IDX_DIM)),
        'state_rwkv_shift': nrm((DEC_BATCH, D)),
        'state_rwkv_wkv': nrm((DEC_BATCH, RW_HEADS, RW_HEAD, RW_HEAD), 0.1),
        'page_table': page_table,
        'p_prompt': nrm((DEPTH, BATCH, SEQ, PLE_DIM)),
        'p_sample': nrm((DEPTH, DEC_BATCH, DEC_SEQ, PLE_DIM)),
        'ln_g': gain((DEPTH, 3, D)),
        'ln_b': nrm((DEPTH, 3, D), 0.02),
        'ffn_w_up': nrm((DEPTH, 2, D, 2 * D_FF), D ** -0.5),
        'ffn_w_down': nrm((DEPTH, 2, D_FF, D), BETA * D_FF ** -0.5),
        'ple_w_p': nrm((DEPTH, PLE_DIM, D), PLE_DIM ** -0.5),
        'ple_w_g': nrm((DEPTH, D, D), D ** -0.5),
        'ple_b_g': nrm((DEPTH, D), 0.02),
        'gm_w_in': nrm((D, 2 * GM_WIDTH), D ** -0.5),
        'gm_ln_g': gain((GM_WIDTH,)),
        'gm_ln_b': nrm((GM_WIDTH,), 0.02),
        'gm_ws': nrm((GM_GROUPS, CHUNK, CHUNK), CHUNK ** -0.5),
        'gm_bs': 1.0 + nrm((GM_GROUPS, CHUNK), 0.1),
        'gm_w_out': nrm((GM_WIDTH, D), BETA * GM_WIDTH ** -0.5),
        'ssm_w_in': ssm_w_in,
        'ssm_conv_w': nrm((SSM_CONV, SSM_CONV_DIM), SSM_CONV ** -0.5),
        'ssm_conv_b': nrm((SSM_CONV_DIM,), 0.02),
        'ssm_dt_bias': dt0 + jnp.log(-jnp.expm1(-dt0)),
        'ssm_a_log': jnp.log(jax.random.uniform(nk(), (SSM_HEADS,), f32, 1.0, 16.0)),
        'ssm_d': 1.0 + nrm((SSM_HEADS,), 0.1),
        'ssm_norm_g': gain((SSM_D_INNER,)),
        'ssm_w_out': nrm((SSM_D_INNER, D), BETA * SSM_D_INNER ** -0.5),
        'att_w_in': nrm((D, ATT_IN_DIM), D ** -0.5),
        'att_kn_g': gain((IDX_DIM,)),
        'att_kn_b': nrm((IDX_DIM,), 0.02),
        'att_w_out': nrm((ATT_Q_DIM, D), BETA * ATT_Q_DIM ** -0.5),
        'rw_mu': jax.random.uniform(nk(), (6, D), f32),
        'rw_w_r': nrm((D, D), D ** -0.5),
        'rw_w_k': nrm((D, D), D ** -0.5),
        'rw_w_v': nrm((D, D), D ** -0.5),
        'rw_w_o': nrm((D, D), BETA * D ** -0.5),
        'rw_w0': jax.random.uniform(nk(), (D,), f32, -6.0, 1.0),
        'rw_w1': nrm((D, RW_DECAY_LORA), D ** -0.5),
        'rw_w2': nrm((RW_DECAY_LORA, D), 0.5 * RW_DECAY_LORA ** -0.5),
        'rw_a0': nrm((D,), 0.1),
        'rw_a1': nrm((D, RW_AAA_LORA), D ** -0.5),
        'rw_a2': nrm((RW_AAA_LORA, D), 0.5 * RW_AAA_LORA ** -0.5),
        'rw_g1': nrm((D, RW_GATE_LORA), D ** -0.5),
        'rw_g2': nrm((RW_GATE_LORA, D), RW_GATE_LORA ** -0.5),
        'rw_k_k': 0.85 + nrm((D,), 0.02),
        'rw_k_a': 1.0 + nrm((D,), 0.02),
        'rw_r_k': nrm((RW_HEADS, RW_HEAD), 0.1),
        'rw_gn_g': gain((D,)),
        'rw_gn_b': nrm((D,), 0.02),
    }


def reference(x_prompt, x_sample, state_ssm_conv, state_ssm, cache_k, cache_v, cache_idx_k,
              state_rwkv_shift, state_rwkv_wkv, page_table, p_prompt, p_sample,
              ln_g, ln_b, ffn_w_up, ffn_w_down, ple_w_p, ple_w_g, ple_b_g,
              gm_w_in, gm_ln_g, gm_ln_b, gm_ws, gm_bs, gm_w_out,
              ssm_w_in, ssm_conv_w, ssm_conv_b, ssm_dt_bias, ssm_a_log, ssm_d, ssm_norm_g, ssm_w_out,
              att_w_in, att_kn_g, att_kn_b, att_w_out,
              rw_mu, rw_w_r, rw_w_k, rw_w_v, rw_w_o, rw_w0, rw_w1, rw_w2, rw_a0, rw_a1, rw_a2,
              rw_g1, rw_g2, rw_k_k, rw_k_a, rw_r_k, rw_gn_g, rw_gn_b):
    B_ = x_prompt.shape[0]

    def ffn_sub(x, i, j):
        return post_norm(x, 0.5 * swiglu(x, ffn_w_up[i, j], ffn_w_down[i, j]), ln_g[i, 2 * j], ln_b[i, 2 * j])

    yp, ys = x_prompt, x_sample
    for i in range(DEPTH):
        yp, ys = ffn_sub(yp, i, 0), ffn_sub(ys, i, 0)
        m = i % N_MIXERS
        if m == 0:
            hp, _ = gmlp_mixer(yp, gm_w_in, gm_ln_g, gm_ln_b, gm_ws, gm_bs, gm_w_out)
            hs, gm_v_s = gmlp_mixer(ys, gm_w_in, gm_ln_g, gm_ln_b, gm_ws, gm_bs, gm_w_out)
        elif m == 1:
            ssm_args = (ssm_w_in, ssm_conv_w, ssm_conv_b, ssm_dt_bias, ssm_a_log, ssm_d, ssm_norm_g, ssm_w_out)
            hp, conv_p, ssm_p = mamba2_mixer(
                yp, jnp.zeros((B_, SSM_CONV - 1, SSM_CONV_DIM), yp.dtype),
                jnp.zeros((B_, SSM_HEADS, SSM_HEAD_DIM, SSM_STATE), yp.dtype), *ssm_args)
            hs, conv_s, ssm_s = mamba2_mixer(ys, state_ssm_conv, state_ssm, *ssm_args)
        elif m == 2:
            hp, k_p, v_p, ik_p = dsa_prompt(yp, att_w_in, att_kn_g, att_kn_b, att_w_out)
            hs, k_s, v_s, ik_s = dsa_sample(ys, cache_k, cache_v, cache_idx_k, page_table,
                                            att_w_in, att_kn_g, att_kn_b, att_w_out)
        else:
            rw_args = (rw_mu, rw_w_r, rw_w_k, rw_w_v, rw_w_o, rw_w0, rw_w1, rw_w2, rw_a0, rw_a1, rw_a2,
                       rw_g1, rw_g2, rw_k_k, rw_k_a, rw_r_k, rw_gn_g, rw_gn_b)
            hp, sh_p, wkv_p = rwkv7_mixer(
                yp, jnp.zeros((B_, D_MODEL), yp.dtype),
                jnp.zeros((B_, RW_HEADS, RW_HEAD, RW_HEAD), yp.dtype), *rw_args)
            hs, sh_s, wkv_s = rwkv7_mixer(ys, state_rwkv_shift, state_rwkv_wkv, *rw_args)
        yp = post_norm(yp, hp, ln_g[i, 1], ln_b[i, 1])
        ys = post_norm(ys, hs, ln_g[i, 1], ln_b[i, 1])
        yp, ys = ffn_sub(yp, i, 1), ffn_sub(ys, i, 1)
        yp = ple_add(yp, p_prompt[i], ple_w_p[i], ple_w_g[i], ple_b_g[i])
        ys = ple_add(ys, p_sample[i], ple_w_p[i], ple_w_g[i], ple_b_g[i])

    return (yp, ys, gm_v_s, conv_p, ssm_p, conv_s, ssm_s, k_p, v_p, ik_p, k_s, v_s, ik_s,
            sh_p, wkv_p, sh_s, wkv_s)
```

```python
import functools
import math

import jax
import jax.numpy as jnp
from jax import lax
from jax.experimental import pallas as pl
from jax.experimental.pallas import tpu as pltpu

D_MODEL = 1024
DEPTH = 4
N_MIXERS = 4
PLE_DIM = 256
D_FF = 2816
ALPHA = (2 * DEPTH) ** 0.25
LN_EPS = 1e-5

CHUNK = 128
GM_WIDTH = 2 * D_MODEL
GM_GROUPS = 8
GM_GROUP_DIM = GM_WIDTH // GM_GROUPS

SSM_D_INNER = 2 * D_MODEL
SSM_HEAD_DIM = 64
SSM_HEADS = SSM_D_INNER // SSM_HEAD_DIM
SSM_GROUPS = 4
SSM_HPG = SSM_HEADS // SSM_GROUPS
SSM_STATE = 128
SSM_CONV = 4
SSM_CONV_DIM = SSM_D_INNER + 2 * SSM_GROUPS * SSM_STATE
SSM_CHUNK = 128

ATT_HEADS = 16
ATT_KV_HEADS = 4
ATT_HEAD_DIM = D_MODEL // ATT_HEADS
ROPE_DIM = ATT_HEAD_DIM // 4
ROPE_THETA = 500000.0
IDX_HEADS = 8
IDX_DIM = 64
IDX_ROPE_DIM = IDX_DIM // 4
TOPK_MAX = 256
Q_BLOCK = 128
ATT_Q_DIM = ATT_HEADS * ATT_HEAD_DIM
ATT_KV_DIM = ATT_KV_HEADS * ATT_HEAD_DIM
ATT_IN_SPLITS = (ATT_Q_DIM, ATT_Q_DIM + ATT_KV_DIM, ATT_Q_DIM + 2 * ATT_KV_DIM,
                 ATT_Q_DIM + 2 * ATT_KV_DIM + IDX_HEADS * IDX_DIM,
                 ATT_Q_DIM + 2 * ATT_KV_DIM + IDX_HEADS * IDX_DIM + IDX_DIM)

RW_HEAD = 64
RW_HEADS = D_MODEL // RW_HEAD
RW_GN_EPS = 64e-5

V7X_VMEM_LIMIT_BYTES = 52 * 1024 * 1024
FF_TILE = D_FF // 2
ROW_TILE = 512


def _row_tile(m):
    return ROW_TILE if m % ROW_TILE == 0 else m


def _ln_rows(y, g, b):
    mu = jnp.mean(y, axis=-1, keepdims=True)
    yc = y - mu
    var = jnp.mean(yc * yc, axis=-1, keepdims=True)
    return yc * lax.rsqrt(var + LN_EPS) * g + b


def _ffn_kernel(x_ref, wa_ref, wb_ref, wd_ref, g_ref, b_ref, o_ref, acc_ref):
    f = pl.program_id(1)

    @pl.when(f == 0)
    def _():
        acc_ref[...] = jnp.zeros_like(acc_ref)

    xb = x_ref[...].astype(jnp.bfloat16)
    a = jnp.dot(xb, wa_ref[...], preferred_element_type=jnp.float32)
    b = jnp.dot(xb, wb_ref[...], preferred_element_type=jnp.float32)
    h = (a * jax.nn.sigmoid(a) * b).astype(jnp.bfloat16)
    acc_ref[...] += jnp.dot(h, wd_ref[...], preferred_element_type=jnp.float32)

    @pl.when(f == pl.num_programs(1) - 1)
    def _():
        y = ALPHA * x_ref[...] + 0.5 * acc_ref[...]
        o_ref[...] = _ln_rows(y, g_ref[...], b_ref[...])


def _ffn_sub(x2d, w_up, w_down, g, b):
    m = x2d.shape[0]
    tm = _row_tile(m)
    nf = D_FF // FF_TILE
    return pl.pallas_call(
        _ffn_kernel,
        grid=(m // tm, nf),
        in_specs=[
            pl.BlockSpec((tm, D_MODEL), lambda i, f: (i, 0)),
            pl.BlockSpec((D_MODEL, FF_TILE), lambda i, f: (0, f)),
            pl.BlockSpec((D_MODEL, FF_TILE), lambda i, f: (0, f + nf)),
            pl.BlockSpec((FF_TILE, D_MODEL), lambda i, f: (f, 0)),
            pl.BlockSpec((1, D_MODEL), lambda i, f: (0, 0)),
            pl.BlockSpec((1, D_MODEL), lambda i, f: (0, 0)),
        ],
        out_specs=pl.BlockSpec((tm, D_MODEL), lambda i, f: (i, 0)),
        out_shape=jax.ShapeDtypeStruct((m, D_MODEL), jnp.float32),
        scratch_shapes=[pltpu.VMEM((tm, D_MODEL), jnp.float32)],
        compiler_params=pltpu.CompilerParams(
            dimension_semantics=("parallel", "arbitrary"),
            vmem_limit_bytes=V7X_VMEM_LIMIT_BYTES),
        name="ffn_sub",
    )(x2d, w_up, w_up, w_down, g.reshape(1, D_MODEL), b.reshape(1, D_MODEL))


def _ple_kernel(x_ref, p_ref, wp_ref, wg_ref, bg_ref, o_ref):
    x = x_ref[...]
    gate = jax.nn.sigmoid(
        jnp.dot(x.astype(jnp.bfloat16), wg_ref[...], preferred_element_type=jnp.float32) + bg_ref[...])
    emb = jnp.dot(p_ref[...].astype(jnp.bfloat16), wp_ref[...], preferred_element_type=jnp.float32)
    o_ref[...] = x + gate * emb


def _ple_add(x2d, p2d, w_p, w_g, b_g):
    m = x2d.shape[0]
    tm = _row_tile(m)
    return pl.pallas_call(
        _ple_kernel,
        grid=(m // tm,),
        in_specs=[
            pl.BlockSpec((tm, D_MODEL), lambda i: (i, 0)),
            pl.BlockSpec((tm, PLE_DIM), lambda i: (i, 0)),
            pl.BlockSpec((PLE_DIM, D_MODEL), lambda i: (0, 0)),
            pl.BlockSpec((D_MODEL, D_MODEL), lambda i: (0, 0)),
            pl.BlockSpec((1, D_MODEL), lambda i: (0, 0)),
        ],
        out_specs=pl.BlockSpec((tm, D_MODEL), lambda i: (i, 0)),
        out_shape=jax.ShapeDtypeStruct((m, D_MODEL), jnp.float32),
        compiler_params=pltpu.CompilerParams(
            dimension_semantics=("parallel",),
            vmem_limit_bytes=V7X_VMEM_LIMIT_BYTES),
        name="ple_add",
    )(x2d, p2d, w_p, w_g, b_g.reshape(1, D_MODEL))


def _proj_ln_kernel(x_ref, h_ref, w_ref, g_ref, b_ref, o_ref):
    y = ALPHA * x_ref[...] + jnp.dot(h_ref[...].astype(jnp.bfloat16), w_ref[...],
                                     preferred_element_type=jnp.float32)
    o_ref[...] = _ln_rows(y, g_ref[...], b_ref[...])


def _proj_post_norm(x2d, h2d, w_out, g, b):
    m = x2d.shape[0]
    k = h2d.shape[1]
    tm = _row_tile(m)
    return pl.pallas_call(
        _proj_ln_kernel,
        grid=(m // tm,),
        in_specs=[
            pl.BlockSpec((tm, D_MODEL), lambda i: (i, 0)),
            pl.BlockSpec((tm, k), lambda i: (i, 0)),
            pl.BlockSpec((k, D_MODEL), lambda i: (0, 0)),
            pl.BlockSpec((1, D_MODEL), lambda i: (0, 0)),
            pl.BlockSpec((1, D_MODEL), lambda i: (0, 0)),
        ],
        out_specs=pl.BlockSpec((tm, D_MODEL), lambda i: (i, 0)),
        out_shape=jax.ShapeDtypeStruct((m, D_MODEL), jnp.float32),
        compiler_params=pltpu.CompilerParams(
            dimension_semantics=("parallel",),
            vmem_limit_bytes=V7X_VMEM_LIMIT_BYTES),
        name="proj_post_norm",
    )(x2d, h2d, w_out, g.reshape(1, D_MODEL), b.reshape(1, D_MODEL))


def _layer_norm(x, g, b):
    xf = x.astype(jnp.float32)
    mu = jnp.mean(xf, -1, keepdims=True)
    var = jnp.mean(jnp.square(xf - mu), -1, keepdims=True)
    return ((xf - mu) * lax.rsqrt(var + LN_EPS)).astype(x.dtype) * g + b


def _rope_partial(x, pos, rot_dim):
    half = rot_dim // 2
    inv = ROPE_THETA ** (-jnp.arange(half, dtype=jnp.float32) / half)
    ang = pos.astype(jnp.float32)[:, None] * inv[None, :]
    cos = jnp.cos(ang)[:, None, :]
    sin = jnp.sin(ang)[:, None, :]
    xf = x[..., :rot_dim].astype(jnp.float32)
    x1, x2 = xf[..., :half], xf[..., half:]
    rot = jnp.concatenate([x1 * cos - x2 * sin, x2 * cos + x1 * sin], axis=-1).astype(x.dtype)
    return jnp.concatenate([rot, x[..., rot_dim:]], axis=-1)


def _gather_rows(rows, idx):
    return jax.vmap(lambda r, i: r[i])(rows, idx)


def _gmlp_mixer(x, w_in, ln_g, ln_b, ws, bs):
    B_, T, _ = x.shape
    u, v = jnp.split(jax.nn.gelu(x @ w_in), 2, axis=-1)
    v = _layer_norm(v, ln_g, ln_b)
    l = min(T, CHUNK)
    c = T // l
    mask = jnp.tril(jnp.ones((l, l), dtype=bool))
    w = jnp.where(mask, ws[:, :l, :l], 0.0)
    vc = v.reshape(B_, c, l, GM_GROUPS, GM_GROUP_DIM)
    mixed = jnp.einsum('gts,bcsgd->bctgd', w, vc) + jnp.transpose(bs[:, :l])[:, :, None]
    return u * mixed.reshape(B_, T, GM_WIDTH), v


def _ssd_chunked(xs, dt, a, bm, cm, h0):
    B_, T = xs.shape[:2]
    l = min(T, SSM_CHUNK)
    c = T // l
    blk = lambda t: t.reshape((B_, c, l) + t.shape[2:])
    xdt = blk(xs.astype(jnp.float32) * dt[..., None])
    bc, cc, acum = blk(bm), blk(cm), jnp.cumsum(blk(a), axis=2)
    at = jnp.moveaxis(acum, 2, -1)
    causal = jnp.tril(jnp.ones((l, l), dtype=bool))
    seg = jnp.exp(jnp.where(causal, at[..., :, None] - at[..., None, :], -jnp.inf))
    cb = jnp.einsum('bctgn,bcsgn->bcgts', cc, bc)
    y_diag = jnp.einsum('bcgts,bcgets,bcsgep->bctgep', cb, seg, xdt)
    states = jnp.einsum('bclgn,bclge,bclgep->bcgepn', bc, jnp.exp(acum[:, :, -1:] - acum), xdt)
    chunk_decay = jnp.exp(acum[:, :, -1])

    def step(h, inp):
        dec, st = inp
        return h * dec[..., None, None] + st, h

    h_last, h_in = lax.scan(step, h0, (jnp.moveaxis(chunk_decay, 1, 0), jnp.moveaxis(states, 1, 0)))
    y_off = jnp.einsum('bctgn,bcgepn,bctge->bctgep', cc, jnp.moveaxis(h_in, 0, 1), jnp.exp(acum))
    return (y_diag + y_off).reshape(B_, T, SSM_GROUPS, SSM_HPG, SSM_HEAD_DIM), h_last


def _mamba2_mixer(x, conv_state, ssm_state, w_in, conv_w, conv_b, dt_bias, a_log, d_skip, norm_g):
    B_, T, _ = x.shape
    z, xbc, dt = jnp.split(x @ w_in, [SSM_D_INNER, SSM_D_INNER + SSM_CONV_DIM], axis=-1)
    xbc_ext = jnp.concatenate([conv_state, xbc], axis=1)
    conv = conv_b
    for j in range(SSM_CONV):
        conv = conv + xbc_ext[:, j:j + T] * conv_w[j]
    xbc = jax.nn.silu(conv)
    xs, bm, cm = jnp.split(xbc, [SSM_D_INNER, SSM_D_INNER + SSM_GROUPS * SSM_STATE], axis=-1)
    xs = xs.reshape(B_, T, SSM_GROUPS, SSM_HPG, SSM_HEAD_DIM)
    bm = bm.reshape(B_, T, SSM_GROUPS, SSM_STATE)
    cm = cm.reshape(B_, T, SSM_GROUPS, SSM_STATE)
    dt = jax.nn.softplus((dt + dt_bias).astype(jnp.float32)).reshape(B_, T, SSM_GROUPS, SSM_HPG)
    a_neg = -jnp.exp(a_log.astype(jnp.float32)).reshape(SSM_GROUPS, SSM_HPG)
    h0 = ssm_state.astype(jnp.float32).reshape(B_, SSM_GROUPS, SSM_HPG, SSM_HEAD_DIM, SSM_STATE)
    y, h_last = _ssd_chunked(xs, dt, dt * a_neg, bm, cm, h0)
    y = y.astype(x.dtype) + xs * d_skip.reshape(SSM_GROUPS, SSM_HPG, 1)
    yg = (y.reshape(B_, T, SSM_D_INNER) * jax.nn.silu(z)).reshape(B_, T, SSM_GROUPS, -1).astype(jnp.float32)
    yg = (yg * lax.rsqrt(jnp.mean(jnp.square(yg), -1, keepdims=True) + LN_EPS)).astype(x.dtype)
    yg = yg.reshape(B_, T, SSM_D_INNER) * norm_g
    new_ssm = h_last.reshape(B_, SSM_HEADS, SSM_HEAD_DIM, SSM_STATE).astype(ssm_state.dtype)
    return yg, xbc_ext[:, T:], new_ssm


def _dsa_project(x, pos, w_in, kn_g, kn_b):
    B_, T, _ = x.shape
    q, k, v, iq, ik, iw = jnp.split(x @ w_in, list(ATT_IN_SPLITS), axis=-1)
    q = _rope_partial(q.reshape(B_, T, ATT_HEADS, ATT_HEAD_DIM), pos, ROPE_DIM)
    k = _rope_partial(k.reshape(B_, T, ATT_KV_HEADS, ATT_HEAD_DIM), pos, ROPE_DIM)
    v = v.reshape(B_, T, ATT_KV_HEADS, ATT_HEAD_DIM)
    iq = _rope_partial(iq.reshape(B_, T, IDX_HEADS, IDX_DIM), pos, IDX_ROPE_DIM)
    ik = _rope_partial(_layer_norm(ik, kn_g, kn_b)[:, :, None, :], pos, IDX_ROPE_DIM)[:, :, 0, :]
    iw = iw * (IDX_HEADS ** -0.5 * IDX_DIM ** -0.5)
    return q, k, v, iq, ik, iw


def _dsa_select(iq, iw, ik, qpos, topk):
    s = jnp.einsum('bqhd,bsd->bqhs', iq, ik)
    score = jnp.einsum('bqh,bqhs->bqs', iw, jax.nn.relu(s)).astype(jnp.float32)
    adm = jnp.arange(ik.shape[1])[None, :] <= qpos[:, None]
    score = jnp.where(adm[None], score, -jnp.inf)
    _, idx = lax.top_k(score, topk)
    return idx, idx <= qpos[None, :, None]


def _sparse_attend(q, k_sel, v_sel, valid):
    B_, Q = q.shape[:2]
    qg = q.reshape(B_, Q, ATT_KV_HEADS, ATT_HEADS // ATT_KV_HEADS, ATT_HEAD_DIM)
    s = jnp.einsum('bqhgd,bqkhd->bqhgk', qg, k_sel).astype(jnp.float32) * (ATT_HEAD_DIM ** -0.5)
    s = jnp.where(valid[:, :, None, None, :], s, -jnp.inf)
    p = jax.nn.softmax(s, axis=-1).astype(v_sel.dtype)
    o = jnp.einsum('bqhgk,bqkhd->bqhgd', p, v_sel)
    return o.reshape(B_, Q, ATT_Q_DIM)


def _dsa_prompt(x, w_in, kn_g, kn_b):
    B_, T, _ = x.shape
    q, k, v, iq, ik, iw = _dsa_project(x, jnp.arange(T), w_in, kn_g, kn_b)
    topk = min(TOPK_MAX, T // 4)

    def block(bi):
        t0 = bi * Q_BLOCK
        sl = lambda t: lax.dynamic_slice_in_dim(t, t0, Q_BLOCK, axis=1)
        qpos = t0 + jnp.arange(Q_BLOCK)
        idx, valid = _dsa_select(sl(iq), sl(iw), ik, qpos, topk)
        return _sparse_attend(sl(q), _gather_rows(k, idx), _gather_rows(v, idx), valid)

    o = lax.map(block, jnp.arange(T // Q_BLOCK))
    o = jnp.moveaxis(o, 0, 1).reshape(B_, T, ATT_Q_DIM)
    return o, k, v, ik


def _dsa_sample(x, cache_k, cache_v, cache_idx_k, page_table, w_in, kn_g, kn_b):
    B_, T, _ = x.shape
    page = cache_k.shape[1]
    past = page_table.shape[1] * page
    pos = past + jnp.arange(T)
    q, k, v, iq, ik, iw = _dsa_project(x, pos, w_in, kn_g, kn_b)
    ik_all = jnp.concatenate([cache_idx_k[page_table].reshape(B_, past, IDX_DIM), ik], axis=1)
    idx, valid = _dsa_select(iq, iw, ik_all, pos, min(TOPK_MAX, (past + T) // 4))
    past_idx = jnp.minimum(idx, past - 1)
    phys = jnp.take_along_axis(page_table, (past_idx // page).reshape(B_, -1), axis=1).reshape(idx.shape)
    off = past_idx % page
    new_idx = jnp.clip(idx - past, 0, T - 1)
    is_new = (idx >= past)[..., None, None]
    k_sel = jnp.where(is_new, _gather_rows(k, new_idx), cache_k[phys, off])
    v_sel = jnp.where(is_new, _gather_rows(v, new_idx), cache_v[phys, off])
    o = _sparse_attend(q, k_sel, v_sel, valid)
    return o, k, v, ik


def _rwkv7_mixer(x, shift, wkv, mu, w_r, w_k, w_v, w0, w1, w2, a0, a1, a2, g1, g2,
                 k_k, k_a, r_k, gn_g, gn_b):
    B_, T, _ = x.shape
    x_prev = jnp.concatenate([shift[:, None, :], x[:, :-1]], axis=1)
    xm = x[None] + (x_prev - x)[None] * mu[:, None, None, :]
    xr, xw, xk, xv, xa, xg = xm
    r = xr @ w_r
    w_log = -jax.nn.softplus(-(w0 + jnp.tanh(xw @ w1) @ w2)) - 0.5
    k = xk @ w_k
    v = xv @ w_v
    a = jax.nn.sigmoid(a0 + (xa @ a1) @ a2)
    g = jax.nn.sigmoid(xg @ g1) @ g2
    heads = lambda t: t.reshape(B_, T, RW_HEADS, RW_HEAD)
    kk = heads(k * k_k).astype(jnp.float32)
    kk = kk / jnp.maximum(jnp.sqrt(jnp.sum(kk * kk, -1, keepdims=True)), 1e-12)
    k = k * (1.0 + (a - 1.0) * k_a)
    decay = jnp.exp(-jnp.exp(w_log.astype(jnp.float32)))
    r, k, v, a, decay = heads(r), heads(k), heads(v), heads(a), heads(decay)
    seq = tuple(jnp.moveaxis(t.astype(jnp.float32), 1, 0) for t in (r, decay, k, v, kk, kk * a))

    def step(s, inp):
        r_t, d_t, k_t, v_t, kk_t, b_t = inp
        sa = jnp.einsum('bhij,bhj->bhi', s, kk_t)
        s = s * d_t[:, :, None, :] - sa[..., None] * b_t[:, :, None, :] + v_t[..., None] * k_t[:, :, None, :]
        return s, jnp.einsum('bhij,bhj->bhi', s, r_t)

    s_last, y = lax.scan(step, wkv.astype(jnp.float32), seq)
    y = jnp.moveaxis(y, 0, 1)
    mu_y = jnp.mean(y, -1, keepdims=True)
    var_y = jnp.mean(jnp.square(y - mu_y), -1, keepdims=True)
    yn = ((y - mu_y) * lax.rsqrt(var_y + RW_GN_EPS)).reshape(B_, T, D_MODEL).astype(x.dtype) * gn_g + gn_b
    bonus = (jnp.sum(r * k * r_k, -1, keepdims=True) * v).reshape(B_, T, D_MODEL)
    return (yn + bonus) * g, x[:, -1], s_last.astype(wkv.dtype)


def kernel(x_prompt, x_sample, state_ssm_conv, state_ssm, cache_k, cache_v, cache_idx_k, state_rwkv_shift, state_rwkv_wkv, page_table, p_prompt, p_sample, ln_g, ln_b, ffn_w_up, ffn_w_down, ple_w_p, ple_w_g, ple_b_g, gm_w_in, gm_ln_g, gm_ln_b, gm_ws, gm_bs, gm_w_out, ssm_w_in, ssm_conv_w, ssm_conv_b, ssm_dt_bias, ssm_a_log, ssm_d, ssm_norm_g, ssm_w_out, att_w_in, att_kn_g, att_kn_b, att_w_out, rw_mu, rw_w_r, rw_w_k, rw_w_v, rw_w_o, rw_w0, rw_w1, rw_w2, rw_a0, rw_a1, rw_a2, rw_g1, rw_g2, rw_k_k, rw_k_a, rw_r_k, rw_gn_g, rw_gn_b):
    bp, tp, _ = x_prompt.shape
    bs_, ts, _ = x_sample.shape
    bf = lambda w: w.astype(jnp.bfloat16)
    w_up_bf, w_down_bf = bf(ffn_w_up), bf(ffn_w_down)
    ple_wp_bf, ple_wg_bf = bf(ple_w_p), bf(ple_w_g)

    yp = x_prompt.reshape(bp * tp, D_MODEL)
    ys = x_sample.reshape(bs_ * ts, D_MODEL)
    r3p = lambda t: t.reshape(bp, tp, -1)
    r3s = lambda t: t.reshape(bs_, ts, -1)
    f2 = lambda t: t.reshape(-1, t.shape[-1])

    for i in range(DEPTH):
        yp = _ffn_sub(yp, w_up_bf[i, 0], w_down_bf[i, 0], ln_g[i, 0], ln_b[i, 0])
        ys = _ffn_sub(ys, w_up_bf[i, 0], w_down_bf[i, 0], ln_g[i, 0], ln_b[i, 0])
        m = i % N_MIXERS
        if m == 0:
            hp, _ = _gmlp_mixer(r3p(yp), gm_w_in, gm_ln_g, gm_ln_b, gm_ws, gm_bs)
            hs, gm_v_s = _gmlp_mixer(r3s(ys), gm_w_in, gm_ln_g, gm_ln_b, gm_ws, gm_bs)
            w_out = bf(gm_w_out)
        elif m == 1:
            ssm_args = (ssm_w_in, ssm_conv_w, ssm_conv_b, ssm_dt_bias, ssm_a_log, ssm_d, ssm_norm_g)
            hp, conv_p, ssm_p = _mamba2_mixer(
                r3p(yp), jnp.zeros((bp, SSM_CONV - 1, SSM_CONV_DIM), yp.dtype),
                jnp.zeros((bp, SSM_HEADS, SSM_HEAD_DIM, SSM_STATE), yp.dtype), *ssm_args)
            hs, conv_s, ssm_s = _mamba2_mixer(r3s(ys), state_ssm_conv, state_ssm, *ssm_args)
            w_out = bf(ssm_w_out)
        elif m == 2:
            hp, k_p, v_p, ik_p = _dsa_prompt(r3p(yp), att_w_in, att_kn_g, att_kn_b)
            hs, k_s, v_s, ik_s = _dsa_sample(r3s(ys), cache_k, cache_v, cache_idx_k, page_table,
                                             att_w_in, att_kn_g, att_kn_b)
            w_out = bf(att_w_out)
        else:
            rw_args = (rw_mu, rw_w_r, rw_w_k, rw_w_v, rw_w0, rw_w1, rw_w2, rw_a0, rw_a1, rw_a2,
                       rw_g1, rw_g2, rw_k_k, rw_k_a, rw_r_k, rw_gn_g, rw_gn_b)
            hp, sh_p, wkv_p = _rwkv7_mixer(
                r3p(yp), jnp.zeros((bp, D_MODEL), yp.dtype),
                jnp.zeros((bp, RW_HEADS, RW_HEAD, RW_HEAD), yp.dtype), *rw_args)
            hs, sh_s, wkv_s = _rwkv7_mixer(r3s(ys), state_rwkv_shift, state_rwkv_wkv, *rw_args)
            w_out = bf(rw_w_o)
        yp = _proj_post_norm(yp, f2(hp), w_out, ln_g[i, 1], ln_b[i, 1])
        ys = _proj_post_norm(ys, f2(hs), w_out, ln_g[i, 1], ln_b[i, 1])
        yp = _ffn_sub(yp, w_up_bf[i, 1], w_down_bf[i, 1], ln_g[i, 2], ln_b[i, 2])
        ys = _ffn_sub(ys, w_up_bf[i, 1], w_down_bf[i, 1], ln_g[i, 2], ln_b[i, 2])
        yp = _ple_add(yp, f2(p_prompt[i]), ple_wp_bf[i], ple_wg_bf[i], ple_b_g[i])
        ys = _ple_add(ys, f2(p_sample[i]), ple_wp_bf[i], ple_wg_bf[i], ple_b_g[i])

    return (r3p(yp), r3s(ys), gm_v_s, conv_p, ssm_p, conv_s, ssm_s, k_p, v_p, ik_p, k_s, v_s, ik_s,
            sh_p, wkv_p, sh_s, wkv_s)
```

```python
import functools
import math

import jax
import jax.numpy as jnp
from jax import lax
from jax.experimental import pallas as pl
from jax.experimental.pallas import tpu as pltpu

D_MODEL = 1024
DEPTH = 4
N_MIXERS = 4
PLE_DIM = 256
D_FF = 2816
ALPHA = (2 * DEPTH) ** 0.25
LN_EPS = 1e-5

CHUNK = 128
GM_WIDTH = 2 * D_MODEL
GM_GROUPS = 8
GM_GROUP_DIM = GM_WIDTH // GM_GROUPS

SSM_D_INNER = 2 * D_MODEL
SSM_HEAD_DIM = 64
SSM_HEADS = SSM_D_INNER // SSM_HEAD_DIM
SSM_GROUPS = 4
SSM_HPG = SSM_HEADS // SSM_GROUPS
SSM_STATE = 128
SSM_CONV = 4
SSM_CONV_DIM = SSM_D_INNER + 2 * SSM_GROUPS * SSM_STATE
SSM_CHUNK = 128

ATT_HEADS = 16
ATT_KV_HEADS = 4
ATT_HEAD_DIM = D_MODEL // ATT_HEADS
ROPE_DIM = ATT_HEAD_DIM // 4
ROPE_THETA = 500000.0
IDX_HEADS = 8
IDX_DIM = 64
IDX_ROPE_DIM = IDX_DIM // 4
TOPK_MAX = 256
Q_BLOCK = 128
ATT_Q_DIM = ATT_HEADS * ATT_HEAD_DIM
ATT_KV_DIM = ATT_KV_HEADS * ATT_HEAD_DIM
ATT_IN_SPLITS = (ATT_Q_DIM, ATT_Q_DIM + ATT_KV_DIM, ATT_Q_DIM + 2 * ATT_KV_DIM,
                 ATT_Q_DIM + 2 * ATT_KV_DIM + IDX_HEADS * IDX_DIM,
                 ATT_Q_DIM + 2 * ATT_KV_DIM + IDX_HEADS * IDX_DIM + IDX_DIM)

RW_HEAD = 64
RW_HEADS = D_MODEL // RW_HEAD
RW_GN_EPS = 64e-5

V7X_VMEM_LIMIT_BYTES = 52 * 1024 * 1024
FF_TILE = D_FF // 2
ROW_TILE = 512


def _row_tile(m):
    return ROW_TILE if m % ROW_TILE == 0 else m


def _ln_rows(y, g, b):
    mu = jnp.mean(y, axis=-1, keepdims=True)
    yc = y - mu
    var = jnp.mean(yc * yc, axis=-1, keepdims=True)
    return yc * lax.rsqrt(var + LN_EPS) * g + b


def _ffn_kernel(x_ref, wa_ref, wb_ref, wd_ref, g_ref, b_ref, o_ref, acc_ref):
    f = pl.program_id(1)

    @pl.when(f == 0)
    def _():
        acc_ref[...] = jnp.zeros_like(acc_ref)

    xb = x_ref[...].astype(jnp.bfloat16)
    a = jnp.dot(xb, wa_ref[...], preferred_element_type=jnp.float32)
    b = jnp.dot(xb, wb_ref[...], preferred_element_type=jnp.float32)
    h = (a * jax.nn.sigmoid(a) * b).astype(jnp.bfloat16)
    acc_ref[...] += jnp.dot(h, wd_ref[...], preferred_element_type=jnp.float32)

    @pl.when(f == pl.num_programs(1) - 1)
    def _():
        y = ALPHA * x_ref[...] + 0.5 * acc_ref[...]
        o_ref[...] = _ln_rows(y, g_ref[...], b_ref[...])


def _ffn_sub(x2d, w_up, w_down, g, b):
    m = x2d.shape[0]
    tm = _row_tile(m)
    nf = D_FF // FF_TILE
    return pl.pallas_call(
        _ffn_kernel,
        grid=(m // tm, nf),
        in_specs=[
            pl.BlockSpec((tm, D_MODEL), lambda i, f: (i, 0)),
            pl.BlockSpec((D_MODEL, FF_TILE), lambda i, f: (0, f)),
            pl.BlockSpec((D_MODEL, FF_TILE), lambda i, f: (0, f + nf)),
            pl.BlockSpec((FF_TILE, D_MODEL), lambda i, f: (f, 0)),
            pl.BlockSpec((1, D_MODEL), lambda i, f: (0, 0)),
            pl.BlockSpec((1, D_MODEL), lambda i, f: (0, 0)),
        ],
        out_specs=pl.BlockSpec((tm, D_MODEL), lambda i, f: (i, 0)),
        out_shape=jax.ShapeDtypeStruct((m, D_MODEL), jnp.float32),
        scratch_shapes=[pltpu.VMEM((tm, D_MODEL), jnp.float32)],
        compiler_params=pltpu.CompilerParams(
            dimension_semantics=("parallel", "arbitrary"),
            vmem_limit_bytes=V7X_VMEM_LIMIT_BYTES),
        name="ffn_sub",
    )(x2d, w_up, w_up, w_down, g.reshape(1, D_MODEL), b.reshape(1, D_MODEL))


def _ple_kernel(x_ref, p_ref, wp_ref, wg_ref, bg_ref, o_ref):
    x = x_ref[...]
    gate = jax.nn.sigmoid(
        jnp.dot(x.astype(jnp.bfloat16), wg_ref[...], preferred_element_type=jnp.float32) + bg_ref[...])
    emb = jnp.dot(p_ref[...].astype(jnp.bfloat16), wp_ref[...], preferred_element_type=jnp.float32)
    o_ref[...] = x + gate * emb


def _ple_add(x2d, p2d, w_p, w_g, b_g):
    m = x2d.shape[0]
    tm = _row_tile(m)
    return pl.pallas_call(
        _ple_kernel,
        grid=(m // tm,),
        in_specs=[
            pl.BlockSpec((tm, D_MODEL), lambda i: (i, 0)),
            pl.BlockSpec((tm, PLE_DIM), lambda i: (i, 0)),
            pl.BlockSpec((PLE_DIM, D_MODEL), lambda i: (0, 0)),
            pl.BlockSpec((D_MODEL, D_MODEL), lambda i: (0, 0)),
            pl.BlockSpec((1, D_MODEL), lambda i: (0, 0)),
        ],
        out_specs=pl.BlockSpec((tm, D_MODEL), lambda i: (i, 0)),
        out_shape=jax.ShapeDtypeStruct((m, D_MODEL), jnp.float32),
        compiler_params=pltpu.CompilerParams(
            dimension_semantics=("parallel",),
            vmem_limit_bytes=V7X_VMEM_LIMIT_BYTES),
        name="ple_add",
    )(x2d, p2d, w_p, w_g, b_g.reshape(1, D_MODEL))


def _proj_ln_kernel(x_ref, h_ref, w_ref, g_ref, b_ref, o_ref):
    y = ALPHA * x_ref[...] + jnp.dot(h_ref[...].astype(jnp.bfloat16), w_ref[...],
                                     preferred_element_type=jnp.float32)
    o_ref[...] = _ln_rows(y, g_ref[...], b_ref[...])


def _proj_post_norm(x2d, h2d, w_out, g, b):
    m = x2d.shape[0]
    k = h2d.shape[1]
    tm = _row_tile(m)
    return pl.pallas_call(
        _proj_ln_kernel,
        grid=(m // tm,),
        in_specs=[
            pl.BlockSpec((tm, D_MODEL), lambda i: (i, 0)),
            pl.BlockSpec((tm, k), lambda i: (i, 0)),
            pl.BlockSpec((k, D_MODEL), lambda i: (0, 0)),
            pl.BlockSpec((1, D_MODEL), lambda i: (0, 0)),
            pl.BlockSpec((1, D_MODEL), lambda i: (0, 0)),
        ],
        out_specs=pl.BlockSpec((tm, D_MODEL), lambda i: (i, 0)),
        out_shape=jax.ShapeDtypeStruct((m, D_MODEL), jnp.float32),
        compiler_params=pltpu.CompilerParams(
            dimension_semantics=("parallel",),
            vmem_limit_bytes=V7X_VMEM_LIMIT_BYTES),
        name="proj_post_norm",
    )(x2d, h2d, w_out, g.reshape(1, D_MODEL), b.reshape(1, D_MODEL))


MXU_DTYPE = jnp.bfloat16
KEY_GROUP = 512
INT32_MIN = -2 ** 31
MASK_NEG = -1e30


def _rope_lane_tables(pos, rot_dim, head_dim):
    half = rot_dim // 2
    inv = ROPE_THETA ** (-jnp.arange(half, dtype=jnp.float32) / half)
    ang = pos.astype(jnp.float32)[:, None] * inv[None, :]
    cos, sin = jnp.cos(ang), jnp.sin(ang)
    n = pos.shape[0]
    rest = head_dim - rot_dim
    c = jnp.concatenate([cos, cos, jnp.ones((n, rest), jnp.float32)], axis=1)
    s1 = jnp.concatenate([-sin, jnp.zeros((n, half + rest), jnp.float32)], axis=1)
    s2 = jnp.concatenate([jnp.zeros((n, half), jnp.float32), sin, jnp.zeros((n, rest), jnp.float32)], axis=1)
    reps = 128 // head_dim
    tile = lambda t: jnp.tile(t, (1, reps))
    return tile(c), tile(s1), tile(s2), cos.T, sin.T


def _rope_lanes(t, c, s1, s2, half):
    n = t.shape[1]
    reps = n // 128
    tl = lambda a: jnp.concatenate([a] * reps, axis=1)
    return t * tl(c) + pltpu.roll(t, n - half, 1) * tl(s1) + pltpu.roll(t, half, 1) * tl(s2)


def _rope_rows(t, cT, sT, head_dim, half):
    pieces = []
    for h in range(t.shape[0] // head_dim):
        x1 = t[h * head_dim:h * head_dim + half]
        x2 = t[h * head_dim + half:h * head_dim + 2 * half]
        pieces += [x1 * cT - x2 * sT, x2 * cT + x1 * sT, t[h * head_dim + 2 * half:(h + 1) * head_dim]]
    return jnp.concatenate(pieces, axis=0)


def _dsa_proj_kernel(x_ref, wq_ref, wiq_ref, wv_ref, wvx_ref, wiw_ref, wkT_ref, wikT_ref,
                     c_ref, s1_ref, s2_ref, cT_ref, sT_ref, kng_ref, knb_ref, one_ref,
                     q_ref, iq_ref, v_ref, vx_ref, iw_ref, kT_ref, kTb_ref, ikT_ref, ikTb_ref):
    xb = x_ref[...].astype(MXU_DTYPE)
    c, s1, s2 = c_ref[...], s1_ref[...], s2_ref[...]
    cT, sT = cT_ref[...], sT_ref[...]
    dot = lambda a, b: jnp.dot(a, b, preferred_element_type=jnp.float32)
    dot_t = lambda w, a: lax.dot_general(w, a, (((1,), (1,)), ((), ())), preferred_element_type=jnp.float32)

    q = _rope_lanes(dot(xb, wq_ref[...]), c, s1, s2, ROPE_DIM // 2)
    q_ref[...] = (q * (ATT_HEAD_DIM ** -0.5)).astype(q_ref.dtype)
    iq = _rope_lanes(dot(xb, wiq_ref[...]), c, s1, s2, IDX_ROPE_DIM // 2)
    iq_ref[...] = iq.astype(iq_ref.dtype)
    v_ref[...] = dot(xb, wv_ref[...])
    vx_ref[...] = (dot(xb, wvx_ref[...]) + one_ref[...]).astype(vx_ref.dtype)
    iw_ref[...] = dot(xb, wiw_ref[...]) * (IDX_HEADS ** -0.5 * IDX_DIM ** -0.5)

    kT = _rope_rows(dot_t(wkT_ref[...], xb), cT, sT, ATT_HEAD_DIM, ROPE_DIM // 2)
    kT_ref[0, 0] = kT
    kTb_ref[0, 0] = kT.astype(kTb_ref.dtype)
    ikT = dot_t(wikT_ref[...], xb)
    mu = jnp.mean(ikT, axis=0, keepdims=True)
    ikc = ikT - mu
    var = jnp.mean(ikc * ikc, axis=0, keepdims=True)
    ikT = ikc * lax.rsqrt(var + LN_EPS) * kng_ref[...] + knb_ref[...]
    ikT = _rope_rows(ikT, cT, sT, IDX_DIM, IDX_ROPE_DIM // 2)
    ikT_ref[0, 0] = ikT
    ikTb_ref[0, 0] = ikT.astype(ikTb_ref.dtype)


def _dsa_project(x3d, pos, w_in, kn_g, kn_b):
    b_, t_, _ = x3d.shape
    tk = KEY_GROUP if t_ % KEY_GROUP == 0 else t_
    ng = t_ // tk
    m = b_ * t_
    w_q, w_k, w_v, w_iq, w_ik, w_iw = jnp.split(w_in, list(ATT_IN_SPLITS), axis=1)
    cast = lambda w: w.astype(MXU_DTYPE)
    w_vx = jnp.pad(w_v.reshape(D_MODEL, ATT_KV_HEADS, ATT_HEAD_DIM),
                   ((0, 0), (0, 0), (0, 128 - ATT_HEAD_DIM))).reshape(D_MODEL, ATT_KV_HEADS * 128)
    one_col = jnp.tile((jnp.arange(128) == ATT_HEAD_DIM).astype(jnp.float32), ATT_KV_HEADS)[None, :]
    w_iw_pad = jnp.pad(w_iw, ((0, 0), (0, 128 - IDX_HEADS)))
    c, s1, s2, cT, sT = _rope_lane_tables(pos, ROPE_DIM, ATT_HEAD_DIM)
    full = lambda shape: pl.BlockSpec(shape, lambda b, i: (0,) * len(shape))
    rows = lambda n: pl.BlockSpec((tk, n), lambda b, i: (b * ng + i, 0))
    ptab = lambda n: pl.BlockSpec((tk, n), lambda b, i: (i, 0))
    grp = lambda n: pl.BlockSpec((1, 1, n, tk), lambda b, i: (b, i, 0, 0))
    sds = jax.ShapeDtypeStruct
    return pl.pallas_call(
        _dsa_proj_kernel,
        grid=(b_, ng),
        in_specs=[rows(D_MODEL), full((D_MODEL, ATT_Q_DIM)), full((D_MODEL, IDX_HEADS * IDX_DIM)),
                  full((D_MODEL, ATT_KV_DIM)), full((D_MODEL, ATT_KV_HEADS * 128)), full((D_MODEL, 128)),
                  full((ATT_KV_DIM, D_MODEL)), full((IDX_DIM, D_MODEL)),
                  ptab(128), ptab(128), ptab(128),
                  pl.BlockSpec((ROPE_DIM // 2, tk), lambda b, i: (0, i)),
                  pl.BlockSpec((ROPE_DIM // 2, tk), lambda b, i: (0, i)),
                  full((IDX_DIM, 1)), full((IDX_DIM, 1)), full((1, ATT_KV_HEADS * 128))],
        out_specs=[rows(ATT_Q_DIM), rows(IDX_HEADS * IDX_DIM), rows(ATT_KV_DIM), rows(ATT_KV_HEADS * 128),
                   rows(128), grp(ATT_KV_DIM), grp(ATT_KV_DIM), grp(IDX_DIM), grp(IDX_DIM)],
        out_shape=[sds((m, ATT_Q_DIM), MXU_DTYPE), sds((m, IDX_HEADS * IDX_DIM), MXU_DTYPE),
                   sds((m, ATT_KV_DIM), jnp.float32), sds((m, ATT_KV_HEADS * 128), MXU_DTYPE),
                   sds((m, 128), jnp.float32),
                   sds((b_, ng, ATT_KV_DIM, tk), jnp.float32), sds((b_, ng, ATT_KV_DIM, tk), MXU_DTYPE),
                   sds((b_, ng, IDX_DIM, tk), jnp.float32), sds((b_, ng, IDX_DIM, tk), MXU_DTYPE)],
        compiler_params=pltpu.CompilerParams(
            dimension_semantics=("parallel", "parallel"),
            vmem_limit_bytes=V7X_VMEM_LIMIT_BYTES),
        name="dsa_project",
    )(x3d.reshape(m, D_MODEL), cast(w_q), cast(w_iq), cast(w_v), cast(w_vx), cast(w_iw_pad),
      cast(w_k.T), cast(w_ik.T), c, s1, s2, cT, sT, kn_g.reshape(IDX_DIM, 1), kn_b.reshape(IDX_DIM, 1), one_col)


def _untranspose_groups(tg):
    b_, g_, r_, tk = tg.shape
    return jnp.transpose(tg, (0, 1, 3, 2)).reshape(b_, g_ * tk, r_)


def _dsa_attend_kernel(iq_ref, iw_ref, ikT_ref, q_ref, kT_ref, vx_ref, o_ref, key_ref, m_ref, acc_ref, *,
                       topk, col_bits):
    j = pl.program_id(1)
    tq = iq_ref.shape[0]
    tk = key_ref.shape[2]
    n_groups = (j * tq + tq + tk - 1) // tk
    row = j * tq + lax.broadcasted_iota(jnp.int32, (tq, tk), 0)
    col0 = lax.broadcasted_iota(jnp.int32, (tq, tk), 1)
    dot = lambda a, b: jnp.dot(a, b, preferred_element_type=jnp.float32)

    def score_body(g, carry):
        ikT = ikT_ref[0, g]
        sc = jnp.zeros((tq, tk), jnp.float32)
        for h in range(IDX_HEADS):
            s = dot(iq_ref[:, h * IDX_DIM:(h + 1) * IDX_DIM], ikT)
            sc = sc + iw_ref[:, h:h + 1] * jnp.maximum(s, 0.0)
        bits = pltpu.bitcast(sc, jnp.int32)
        key = jnp.where(bits >= 0, bits, bits ^ jnp.int32(0x7FFFFFFF))
        key_ref[g] = jnp.where(col0 + g * tk <= row, key, jnp.int32(INT32_MIN))
        return carry

    lax.fori_loop(0, n_groups, score_body, 0)

    def bit_body(i, thr):
        cand = thr ^ lax.shift_left(jnp.int32(1), jnp.int32(31) - i)

        def count_body(g, cnt):
            hit = jnp.where(key_ref[g] >= cand, 1.0, 0.0)
            for l in range(tk // 128):
                cnt = cnt + hit[:, l * 128:(l + 1) * 128]
            return cnt

        cnt = lax.fori_loop(0, n_groups, count_body, jnp.zeros((tq, 128), jnp.float32))
        total = jnp.sum(cnt, axis=1, keepdims=True)
        return jnp.where(total >= float(topk), cand, thr)

    thr = lax.fori_loop(0, 32, bit_body, jnp.full((tq, 1), INT32_MIN, jnp.int32))

    def lane_fold(hit, cnt):
        for l in range(tk // 128):
            cnt = cnt + hit[:, l * 128:(l + 1) * 128]
        return cnt

    def above_body(g, cnt):
        return lane_fold(jnp.where(key_ref[g] > thr, 1.0, 0.0), cnt)

    n_above = jnp.sum(lax.fori_loop(0, n_groups, above_body, jnp.zeros((tq, 128), jnp.float32)),
                      axis=1, keepdims=True)
    need = float(topk) - n_above

    def col_body(i, last):
        cand = last | lax.shift_left(jnp.int32(1), jnp.int32(col_bits - 1) - i)

        def tie_body(g, cnt):
            hit = jnp.where((key_ref[g] == thr) & (col0 + g * tk < cand), 1.0, 0.0)
            return lane_fold(hit, cnt)

        ties = jnp.sum(lax.fori_loop(0, n_groups, tie_body, jnp.zeros((tq, 128), jnp.float32)),
                       axis=1, keepdims=True)
        return jnp.where(ties < need, cand, last)

    last_tie = lax.fori_loop(0, col_bits, col_body, jnp.zeros((tq, 1), jnp.int32))

    m_ref[...] = jnp.full(m_ref.shape, MASK_NEG, jnp.float32)
    acc_ref[...] = jnp.zeros(acc_ref.shape, jnp.float32)
    gsz = ATT_HEADS // ATT_KV_HEADS

    def attend_body(g, carry):
        key = key_ref[g]
        col = col0 + g * tk
        keep = (key > thr) | ((key == thr) & (col <= last_tie))
        bias = jnp.where(keep & (col <= row), 0.0, MASK_NEG)
        start = pl.multiple_of(g * tk, tk)
        for h in range(ATT_HEADS):
            kv = h // gsz
            s = dot(q_ref[:, h * ATT_HEAD_DIM:(h + 1) * ATT_HEAD_DIM],
                    kT_ref[0, g, kv * ATT_HEAD_DIM:(kv + 1) * ATT_HEAD_DIM, :]) + bias
            m_old = m_ref[h]
            m_new = jnp.maximum(m_old, jnp.max(s, axis=1, keepdims=True))
            p = jnp.exp(s - m_new).astype(vx_ref.dtype)
            pv = dot(p, vx_ref[0, pl.ds(start, tk), kv * 128:(kv + 1) * 128])
            acc_ref[h] = jnp.exp(m_old - m_new) * acc_ref[h] + pv
            m_ref[h] = m_new
        return carry

    lax.fori_loop(0, n_groups, attend_body, 0)

    for h in range(ATT_HEADS):
        a = acc_ref[h]
        o_ref[:, h * ATT_HEAD_DIM:(h + 1) * ATT_HEAD_DIM] = (
            a[:, :ATT_HEAD_DIM] / a[:, ATT_HEAD_DIM:ATT_HEAD_DIM + 1]).astype(o_ref.dtype)


def _dsa_attend(b_, t_, q, iq, iw, ikTb, kTb, vx):
    ng, tk = kTb.shape[1], kTb.shape[3]
    tq = Q_BLOCK
    nq = t_ // tq
    rows = lambda n: pl.BlockSpec((tq, n), lambda b, j: (b * nq + j, 0))
    return pl.pallas_call(
        functools.partial(_dsa_attend_kernel, topk=min(TOPK_MAX, t_ // 4), col_bits=max(1, (t_ - 1).bit_length())),
        grid=(b_, nq),
        in_specs=[rows(IDX_HEADS * IDX_DIM), rows(128),
                  pl.BlockSpec((1, ng, IDX_DIM, tk), lambda b, j: (b, 0, 0, 0)),
                  rows(ATT_Q_DIM),
                  pl.BlockSpec((1, ng, ATT_KV_DIM, tk), lambda b, j: (b, 0, 0, 0)),
                  pl.BlockSpec((1, t_, ATT_KV_HEADS * 128), lambda b, j: (b, 0, 0))],
        out_specs=rows(ATT_Q_DIM),
        out_shape=jax.ShapeDtypeStruct((b_ * t_, ATT_Q_DIM), MXU_DTYPE),
        scratch_shapes=[pltpu.VMEM((ng, tq, tk), jnp.int32),
                        pltpu.VMEM((ATT_HEADS, tq, 1), jnp.float32),
                        pltpu.VMEM((ATT_HEADS, tq, 128), jnp.float32)],
        compiler_params=pltpu.CompilerParams(
            dimension_semantics=("parallel", "arbitrary"),
            vmem_limit_bytes=V7X_VMEM_LIMIT_BYTES),
        name="dsa_attend",
    )(iq, iw, ikTb, q, kTb, vx.reshape(b_, t_, ATT_KV_HEADS * 128))


def _dsa_prompt_pallas(x3d, w_in, kn_g, kn_b):
    b_, t_, _ = x3d.shape
    q, iq, v, vx, iw, kT, kTb, ikT, ikTb = _dsa_project(x3d, jnp.arange(t_), w_in, kn_g, kn_b)
    o = _dsa_attend(b_, t_, q, iq, iw, ikTb, kTb, vx)
    k = _untranspose_groups(kT).reshape(b_, t_, ATT_KV_HEADS, ATT_HEAD_DIM)
    ik = _untranspose_groups(ikT)
    return o, k, v.reshape(b_, t_, ATT_KV_HEADS, ATT_HEAD_DIM), ik


RW_ROW_TILE = 256


def _rwkv_proj_kernel(x_ref, xp_ref, mu_ref, wr_ref, wk_ref, wv_ref, w1_ref, w2_ref, a1_ref, a2_ref,
                      g1_ref, g2_ref, w0_ref, a0_ref, r_ref, d_ref, k_ref, v_ref, a_ref, g_ref):
    x = x_ref[...]
    dx = xp_ref[...] - x
    mix = lambda c: (x + dx * mu_ref[c:c + 1, :]).astype(MXU_DTYPE)
    dot = lambda a, b: jnp.dot(a.astype(MXU_DTYPE), b, preferred_element_type=jnp.float32)
    r_ref[...] = dot(mix(0), wr_ref[...])
    lora_w = dot(jnp.tanh(dot(mix(1), w1_ref[...])), w2_ref[...])
    w_log = -jax.nn.softplus(-(w0_ref[...] + lora_w)) - 0.5
    d_ref[...] = jnp.exp(-jnp.exp(w_log))
    k_ref[...] = dot(mix(2), wk_ref[...])
    v_ref[...] = dot(mix(3), wv_ref[...])
    a_ref[...] = jax.nn.sigmoid(a0_ref[...] + dot(dot(mix(4), a1_ref[...]), a2_ref[...]))
    g_ref[...] = dot(jax.nn.sigmoid(dot(mix(5), g1_ref[...])), g2_ref[...])


def _rwkv_project(x2d, xprev2d, mu, w_r, w_k, w_v, w0, w1, w2, a0, a1, a2, g1, g2):
    m = x2d.shape[0]
    tm = RW_ROW_TILE if m % RW_ROW_TILE == 0 else m
    cast = lambda w: w.astype(MXU_DTYPE)
    full = lambda a: pl.BlockSpec(a.shape, lambda i: (0,) * a.ndim)
    rows = pl.BlockSpec((tm, D_MODEL), lambda i: (i, 0))
    consts = [mu, cast(w_r), cast(w_k), cast(w_v), cast(w1), cast(w2), cast(a1), cast(a2), cast(g1), cast(g2),
              w0.reshape(1, D_MODEL), a0.reshape(1, D_MODEL)]
    return pl.pallas_call(
        _rwkv_proj_kernel,
        grid=(m // tm,),
        in_specs=[rows, rows] + [full(a) for a in consts],
        out_specs=[rows] * 6,
        out_shape=[jax.ShapeDtypeStruct((m, D_MODEL), jnp.float32)] * 6,
        compiler_params=pltpu.CompilerParams(
            dimension_semantics=("parallel",),
            vmem_limit_bytes=V7X_VMEM_LIMIT_BYTES),
        name="rwkv_project",
    )(x2d, xprev2d, *consts)


RW_LANES = 128
RW_TIME_CHUNK = 32


def _rwkv_scan_kernel(r_ref, d_ref, k_ref, v_ref, a_ref, s0_ref, kk_ref, ka_ref, rk_ref, gg_ref, gb_ref,
                      z_ref, s_out_ref, s_ref, vec_ref):
    c = pl.program_id(1)
    n = RW_HEAD

    @pl.when(c == 0)
    def _():
        s_ref[...] = s0_ref[...]

    def step(t, carry):
        r, k, v, a = r_ref[t], k_ref[t], v_ref[t], a_ref[t]
        kkr = k * kk_ref[...]
        nrm = jnp.sqrt(jnp.sum(kkr * kkr, axis=0, keepdims=True))
        kk = kkr / jnp.maximum(nrm, 1e-12)
        kmod = k * (1.0 + (a - 1.0) * ka_ref[...])
        vec_ref[0] = kk
        vec_ref[1] = d_ref[t]
        vec_ref[2] = kk * a
        vec_ref[3] = kmod
        vec_ref[4] = r
        row = lambda q, j: vec_ref[q, j:j + 1, :]
        sa = jnp.zeros((n, RW_LANES), jnp.float32)
        for j in range(n):
            sa = sa + s_ref[j] * row(0, j)
        y = jnp.zeros((n, RW_LANES), jnp.float32)
        for j in range(n):
            sn = s_ref[j] * row(1, j) - sa * row(2, j) + v * row(3, j)
            s_ref[j] = sn
            y = y + sn * row(4, j)
        mu = jnp.mean(y, axis=0, keepdims=True)
        yc = y - mu
        var = jnp.mean(yc * yc, axis=0, keepdims=True)
        bonus = jnp.sum(r * kmod * rk_ref[...], axis=0, keepdims=True)
        z_ref[t] = yc * lax.rsqrt(var + RW_GN_EPS) * gg_ref[...] + gb_ref[...] + bonus * v
        return carry

    lax.fori_loop(0, r_ref.shape[0], step, 0)

    @pl.when(c == pl.num_programs(1) - 1)
    def _():
        s_out_ref[...] = s_ref[...]


def _rwkv_scan(rT, dT, kT, vT, aT, s0T, k_k, k_a, r_k, gn_g, gn_b):
    t_, n, bh = rT.shape
    tc = RW_TIME_CHUNK if t_ % RW_TIME_CHUNK == 0 else t_
    reps = RW_LANES // RW_HEADS
    table = lambda p: jnp.tile(p.reshape(RW_HEADS, n).T, (1, reps))
    seq = pl.BlockSpec((tc, n, RW_LANES), lambda l, c: (c, 0, l))
    state = pl.BlockSpec((n, n, RW_LANES), lambda l, c: (0, 0, l))
    tab = pl.BlockSpec((n, RW_LANES), lambda l, c: (0, 0))
    return pl.pallas_call(
        _rwkv_scan_kernel,
        grid=(bh // RW_LANES, t_ // tc),
        in_specs=[seq] * 5 + [state] + [tab] * 5,
        out_specs=[seq, state],
        out_shape=[jax.ShapeDtypeStruct((t_, n, bh), jnp.float32),
                   jax.ShapeDtypeStruct((n, n, bh), jnp.float32)],
        scratch_shapes=[pltpu.VMEM((n, n, RW_LANES), jnp.float32),
                        pltpu.VMEM((5, n, RW_LANES), jnp.float32)],
        compiler_params=pltpu.CompilerParams(
            dimension_semantics=("parallel", "arbitrary"),
            vmem_limit_bytes=V7X_VMEM_LIMIT_BYTES),
        name="rwkv_scan",
    )(rT, dT, kT, vT, aT, s0T, table(k_k), table(k_a), table(r_k), table(gn_g), table(gn_b))


def _rwkv7_mixer_pallas(x3d, shift, wkv, mu, w_r, w_k, w_v, w0, w1, w2, a0, a1, a2, g1, g2,
                        k_k, k_a, r_k, gn_g, gn_b):
    b_, t_, _ = x3d.shape
    m = b_ * t_
    x_prev = jnp.concatenate([shift[:, None, :], x3d[:, :-1]], axis=1)
    r, d, k, v, a, g = _rwkv_project(x3d.reshape(m, D_MODEL), x_prev.reshape(m, D_MODEL), mu,
                                     w_r, w_k, w_v, w0, w1, w2, a0, a1, a2, g1, g2)
    to_scan = lambda u: jnp.transpose(u.reshape(b_, t_, RW_HEADS, RW_HEAD), (1, 3, 0, 2)).reshape(
        t_, RW_HEAD, b_ * RW_HEADS)
    s0T = jnp.transpose(wkv.astype(jnp.float32), (3, 2, 0, 1)).reshape(RW_HEAD, RW_HEAD, b_ * RW_HEADS)
    zT, sT = _rwkv_scan(to_scan(r), to_scan(d), to_scan(k), to_scan(v), to_scan(a), s0T,
                        k_k, k_a, r_k, gn_g, gn_b)
    z = jnp.transpose(zT.reshape(t_, RW_HEAD, b_, RW_HEADS), (2, 0, 3, 1)).reshape(m, D_MODEL)
    s_new = jnp.transpose(sT.reshape(RW_HEAD, RW_HEAD, b_, RW_HEADS), (2, 3, 1, 0)).astype(wkv.dtype)
    return z, g, x3d[:, -1], s_new


def _proj_gate_ln_kernel(x_ref, h_ref, gate_ref, w_ref, g_ref, b_ref, o_ref):
    h = (h_ref[...] * gate_ref[...]).astype(MXU_DTYPE)
    y = ALPHA * x_ref[...] + jnp.dot(h, w_ref[...], preferred_element_type=jnp.float32)
    o_ref[...] = _ln_rows(y, g_ref[...], b_ref[...])


def _proj_gate_post_norm(x2d, h2d, gate2d, w_out, g, b):
    m = x2d.shape[0]
    tm = _row_tile(m)
    rows = pl.BlockSpec((tm, D_MODEL), lambda i: (i, 0))
    vec = pl.BlockSpec((1, D_MODEL), lambda i: (0, 0))
    return pl.pallas_call(
        _proj_gate_ln_kernel,
        grid=(m // tm,),
        in_specs=[rows, rows, rows, pl.BlockSpec((D_MODEL, D_MODEL), lambda i: (0, 0)), vec, vec],
        out_specs=rows,
        out_shape=jax.ShapeDtypeStruct((m, D_MODEL), jnp.float32),
        compiler_params=pltpu.CompilerParams(
            dimension_semantics=("parallel",),
            vmem_limit_bytes=V7X_VMEM_LIMIT_BYTES),
        name="proj_gate_post_norm",
    )(x2d, h2d, gate2d, w_out, g.reshape(1, D_MODEL), b.reshape(1, D_MODEL))


def _layer_norm(x, g, b):
    xf = x.astype(jnp.float32)
    mu = jnp.mean(xf, -1, keepdims=True)
    var = jnp.mean(jnp.square(xf - mu), -1, keepdims=True)
    return ((xf - mu) * lax.rsqrt(var + LN_EPS)).astype(x.dtype) * g + b


def _rope_partial(x, pos, rot_dim):
    half = rot_dim // 2
    inv = ROPE_THETA ** (-jnp.arange(half, dtype=jnp.float32) / half)
    ang = pos.astype(jnp.float32)[:, None] * inv[None, :]
    cos = jnp.cos(ang)[:, None, :]
    sin = jnp.sin(ang)[:, None, :]
    xf = x[..., :rot_dim].astype(jnp.float32)
    x1, x2 = xf[..., :half], xf[..., half:]
    rot = jnp.concatenate([x1 * cos - x2 * sin, x2 * cos + x1 * sin], axis=-1).astype(x.dtype)
    return jnp.concatenate([rot, x[..., rot_dim:]], axis=-1)


def _gather_rows(rows, idx):
    return jax.vmap(lambda r, i: r[i])(rows, idx)


def _gmlp_mixer(x, w_in, ln_g, ln_b, ws, bs):
    B_, T, _ = x.shape
    u, v = jnp.split(jax.nn.gelu(x @ w_in), 2, axis=-1)
    v = _layer_norm(v, ln_g, ln_b)
    l = min(T, CHUNK)
    c = T // l
    mask = jnp.tril(jnp.ones((l, l), dtype=bool))
    w = jnp.where(mask, ws[:, :l, :l], 0.0)
    vc = v.reshape(B_, c, l, GM_GROUPS, GM_GROUP_DIM)
    mixed = jnp.einsum('gts,bcsgd->bctgd', w, vc) + jnp.transpose(bs[:, :l])[:, :, None]
    return u * mixed.reshape(B_, T, GM_WIDTH), v


def _ssd_chunked(xs, dt, a, bm, cm, h0):
    B_, T = xs.shape[:2]
    l = min(T, SSM_CHUNK)
    c = T // l
    blk = lambda t: t.reshape((B_, c, l) + t.shape[2:])
    xdt = blk(xs.astype(jnp.float32) * dt[..., None])
    bc, cc, acum = blk(bm), blk(cm), jnp.cumsum(blk(a), axis=2)
    at = jnp.moveaxis(acum, 2, -1)
    causal = jnp.tril(jnp.ones((l, l), dtype=bool))
    seg = jnp.exp(jnp.where(causal, at[..., :, None] - at[..., None, :], -jnp.inf))
    cb = jnp.einsum('bctgn,bcsgn->bcgts', cc, bc)
    y_diag = jnp.einsum('bcgts,bcgets,bcsgep->bctgep', cb, seg, xdt)
    states = jnp.einsum('bclgn,bclge,bclgep->bcgepn', bc, jnp.exp(acum[:, :, -1:] - acum), xdt)
    chunk_decay = jnp.exp(acum[:, :, -1])

    def step(h, inp):
        dec, st = inp
        return h * dec[..., None, None] + st, h

    h_last, h_in = lax.scan(step, h0, (jnp.moveaxis(chunk_decay, 1, 0), jnp.moveaxis(states, 1, 0)))
    y_off = jnp.einsum('bctgn,bcgepn,bctge->bctgep', cc, jnp.moveaxis(h_in, 0, 1), jnp.exp(acum))
    return (y_diag + y_off).reshape(B_, T, SSM_GROUPS, SSM_HPG, SSM_HEAD_DIM), h_last


def _mamba2_mixer(x, conv_state, ssm_state, w_in, conv_w, conv_b, dt_bias, a_log, d_skip, norm_g):
    B_, T, _ = x.shape
    z, xbc, dt = jnp.split(x @ w_in, [SSM_D_INNER, SSM_D_INNER + SSM_CONV_DIM], axis=-1)
    xbc_ext = jnp.concatenate([conv_state, xbc], axis=1)
    conv = conv_b
    for j in range(SSM_CONV):
        conv = conv + xbc_ext[:, j:j + T] * conv_w[j]
    xbc = jax.nn.silu(conv)
    xs, bm, cm = jnp.split(xbc, [SSM_D_INNER, SSM_D_INNER + SSM_GROUPS * SSM_STATE], axis=-1)
    xs = xs.reshape(B_, T, SSM_GROUPS, SSM_HPG, SSM_HEAD_DIM)
    bm = bm.reshape(B_, T, SSM_GROUPS, SSM_STATE)
    cm = cm.reshape(B_, T, SSM_GROUPS, SSM_STATE)
    dt = jax.nn.softplus((dt + dt_bias).astype(jnp.float32)).reshape(B_, T, SSM_GROUPS, SSM_HPG)
    a_neg = -jnp.exp(a_log.astype(jnp.float32)).reshape(SSM_GROUPS, SSM_HPG)
    h0 = ssm_state.astype(jnp.float32).reshape(B_, SSM_GROUPS, SSM_HPG, SSM_HEAD_DIM, SSM_STATE)
    y, h_last = _ssd_chunked(xs, dt, dt * a_neg, bm, cm, h0)
    y = y.astype(x.dtype) + xs * d_skip.reshape(SSM_GROUPS, SSM_HPG, 1)
    yg = (y.reshape(B_, T, SSM_D_INNER) * jax.nn.silu(z)).reshape(B_, T, SSM_GROUPS, -1).astype(jnp.float32)
    yg = (yg * lax.rsqrt(jnp.mean(jnp.square(yg), -1, keepdims=True) + LN_EPS)).astype(x.dtype)
    yg = yg.reshape(B_, T, SSM_D_INNER) * norm_g
    new_ssm = h_last.reshape(B_, SSM_HEADS, SSM_HEAD_DIM, SSM_STATE).astype(ssm_state.dtype)
    return yg, xbc_ext[:, T:], new_ssm


def _dsa_project_jax(x, pos, w_in, kn_g, kn_b):
    B_, T, _ = x.shape
    q, k, v, iq, ik, iw = jnp.split(x @ w_in, list(ATT_IN_SPLITS), axis=-1)
    q = _rope_partial(q.reshape(B_, T, ATT_HEADS, ATT_HEAD_DIM), pos, ROPE_DIM)
    k = _rope_partial(k.reshape(B_, T, ATT_KV_HEADS, ATT_HEAD_DIM), pos, ROPE_DIM)
    v = v.reshape(B_, T, ATT_KV_HEADS, ATT_HEAD_DIM)
    iq = _rope_partial(iq.reshape(B_, T, IDX_HEADS, IDX_DIM), pos, IDX_ROPE_DIM)
    ik = _rope_partial(_layer_norm(ik, kn_g, kn_b)[:, :, None, :], pos, IDX_ROPE_DIM)[:, :, 0, :]
    iw = iw * (IDX_HEADS ** -0.5 * IDX_DIM ** -0.5)
    return q, k, v, iq, ik, iw


def _dsa_select(iq, iw, ik, qpos, topk):
    s = jnp.einsum('bqhd,bsd->bqhs', iq, ik)
    score = jnp.einsum('bqh,bqhs->bqs', iw, jax.nn.relu(s)).astype(jnp.float32)
    adm = jnp.arange(ik.shape[1])[None, :] <= qpos[:, None]
    score = jnp.where(adm[None], score, -jnp.inf)
    _, idx = lax.top_k(score, topk)
    return idx, idx <= qpos[None, :, None]


def _sparse_attend(q, k_sel, v_sel, valid):
    B_, Q = q.shape[:2]
    qg = q.reshape(B_, Q, ATT_KV_HEADS, ATT_HEADS // ATT_KV_HEADS, ATT_HEAD_DIM)
    s = jnp.einsum('bqhgd,bqkhd->bqhgk', qg, k_sel).astype(jnp.float32) * (ATT_HEAD_DIM ** -0.5)
    s = jnp.where(valid[:, :, None, None, :], s, -jnp.inf)
    p = jax.nn.softmax(s, axis=-1).astype(v_sel.dtype)
    o = jnp.einsum('bqhgk,bqkhd->bqhgd', p, v_sel)
    return o.reshape(B_, Q, ATT_Q_DIM)


def _dsa_prompt(x, w_in, kn_g, kn_b):
    B_, T, _ = x.shape
    q, k, v, iq, ik, iw = _dsa_project_jax(x, jnp.arange(T), w_in, kn_g, kn_b)
    topk = min(TOPK_MAX, T // 4)

    def block(bi):
        t0 = bi * Q_BLOCK
        sl = lambda t: lax.dynamic_slice_in_dim(t, t0, Q_BLOCK, axis=1)
        qpos = t0 + jnp.arange(Q_BLOCK)
        idx, valid = _dsa_select(sl(iq), sl(iw), ik, qpos, topk)
        return _sparse_attend(sl(q), _gather_rows(k, idx), _gather_rows(v, idx), valid)

    o = lax.map(block, jnp.arange(T // Q_BLOCK))
    o = jnp.moveaxis(o, 0, 1).reshape(B_, T, ATT_Q_DIM)
    return o, k, v, ik


def _dsa_sample(x, cache_k, cache_v, cache_idx_k, page_table, w_in, kn_g, kn_b):
    B_, T, _ = x.shape
    page = cache_k.shape[1]
    past = page_table.shape[1] * page
    pos = past + jnp.arange(T)
    q, k, v, iq, ik, iw = _dsa_project_jax(x, pos, w_in, kn_g, kn_b)
    ik_all = jnp.concatenate([cache_idx_k[page_table].reshape(B_, past, IDX_DIM), ik], axis=1)
    idx, valid = _dsa_select(iq, iw, ik_all, pos, min(TOPK_MAX, (past + T) // 4))
    past_idx = jnp.minimum(idx, past - 1)
    phys = jnp.take_along_axis(page_table, (past_idx // page).reshape(B_, -1), axis=1).reshape(idx.shape)
    off = past_idx % page
    new_idx = jnp.clip(idx - past, 0, T - 1)
    is_new = (idx >= past)[..., None, None]
    k_sel = jnp.where(is_new, _gather_rows(k, new_idx), cache_k[phys, off])
    v_sel = jnp.where(is_new, _gather_rows(v, new_idx), cache_v[phys, off])
    o = _sparse_attend(q, k_sel, v_sel, valid)
    return o, k, v, ik


def _rwkv7_mixer(x, shift, wkv, mu, w_r, w_k, w_v, w0, w1, w2, a0, a1, a2, g1, g2,
                 k_k, k_a, r_k, gn_g, gn_b):
    B_, T, _ = x.shape
    x_prev = jnp.concatenate([shift[:, None, :], x[:, :-1]], axis=1)
    xm = x[None] + (x_prev - x)[None] * mu[:, None, None, :]
    xr, xw, xk, xv, xa, xg = xm
    r = xr @ w_r
    w_log = -jax.nn.softplus(-(w0 + jnp.tanh(xw @ w1) @ w2)) - 0.5
    k = xk @ w_k
    v = xv @ w_v
    a = jax.nn.sigmoid(a0 + (xa @ a1) @ a2)
    g = jax.nn.sigmoid(xg @ g1) @ g2
    heads = lambda t: t.reshape(B_, T, RW_HEADS, RW_HEAD)
    kk = heads(k * k_k).astype(jnp.float32)
    kk = kk / jnp.maximum(jnp.sqrt(jnp.sum(kk * kk, -1, keepdims=True)), 1e-12)
    k = k * (1.0 + (a - 1.0) * k_a)
    decay = jnp.exp(-jnp.exp(w_log.astype(jnp.float32)))
    r, k, v, a, decay = heads(r), heads(k), heads(v), heads(a), heads(decay)
    seq = tuple(jnp.moveaxis(t.astype(jnp.float32), 1, 0) for t in (r, decay, k, v, kk, kk * a))

    def step(s, inp):
        r_t, d_t, k_t, v_t, kk_t, b_t = inp
        sa = jnp.einsum('bhij,bhj->bhi', s, kk_t)
        s = s * d_t[:, :, None, :] - sa[..., None] * b_t[:, :, None, :] + v_t[..., None] * k_t[:, :, None, :]
        return s, jnp.einsum('bhij,bhj->bhi', s, r_t)

    s_last, y = lax.scan(step, wkv.astype(jnp.float32), seq)
    y = jnp.moveaxis(y, 0, 1)
    mu_y = jnp.mean(y, -1, keepdims=True)
    var_y = jnp.mean(jnp.square(y - mu_y), -1, keepdims=True)
    yn = ((y - mu_y) * lax.rsqrt(var_y + RW_GN_EPS)).reshape(B_, T, D_MODEL).astype(x.dtype) * gn_g + gn_b
    bonus = (jnp.sum(r * k * r_k, -1, keepdims=True) * v).reshape(B_, T, D_MODEL)
    return (yn + bonus) * g, x[:, -1], s_last.astype(wkv.dtype)


def kernel(x_prompt, x_sample, state_ssm_conv, state_ssm, cache_k, cache_v, cache_idx_k, state_rwkv_shift, state_rwkv_wkv, page_table, p_prompt, p_sample, ln_g, ln_b, ffn_w_up, ffn_w_down, ple_w_p, ple_w_g, ple_b_g, gm_w_in, gm_ln_g, gm_ln_b, gm_ws, gm_bs, gm_w_out, ssm_w_in, ssm_conv_w, ssm_conv_b, ssm_dt_bias, ssm_a_log, ssm_d, ssm_norm_g, ssm_w_out, att_w_in, att_kn_g, att_kn_b, att_w_out, rw_mu, rw_w_r, rw_w_k, rw_w_v, rw_w_o, rw_w0, rw_w1, rw_w2, rw_a0, rw_a1, rw_a2, rw_g1, rw_g2, rw_k_k, rw_k_a, rw_r_k, rw_gn_g, rw_gn_b):
    bp, tp, _ = x_prompt.shape
    bs_, ts, _ = x_sample.shape
    bf = lambda w: w.astype(jnp.bfloat16)
    w_up_bf, w_down_bf = bf(ffn_w_up), bf(ffn_w_down)
    ple_wp_bf, ple_wg_bf = bf(ple_w_p), bf(ple_w_g)

    yp = x_prompt.reshape(bp * tp, D_MODEL)
    ys = x_sample.reshape(bs_ * ts, D_MODEL)
    r3p = lambda t: t.reshape(bp, tp, -1)
    r3s = lambda t: t.reshape(bs_, ts, -1)
    f2 = lambda t: t.reshape(-1, t.shape[-1])

    for i in range(DEPTH):
        yp = _ffn_sub(yp, w_up_bf[i, 0], w_down_bf[i, 0], ln_g[i, 0], ln_b[i, 0])
        ys = _ffn_sub(ys, w_up_bf[i, 0], w_down_bf[i, 0], ln_g[i, 0], ln_b[i, 0])
        m = i % N_MIXERS
        if m == 0:
            hp, _ = _gmlp_mixer(r3p(yp), gm_w_in, gm_ln_g, gm_ln_b, gm_ws, gm_bs)
            hs, gm_v_s = _gmlp_mixer(r3s(ys), gm_w_in, gm_ln_g, gm_ln_b, gm_ws, gm_bs)
            w_out = bf(gm_w_out)
        elif m == 1:
            ssm_args = (ssm_w_in, ssm_conv_w, ssm_conv_b, ssm_dt_bias, ssm_a_log, ssm_d, ssm_norm_g)
            hp, conv_p, ssm_p = _mamba2_mixer(
                r3p(yp), jnp.zeros((bp, SSM_CONV - 1, SSM_CONV_DIM), yp.dtype),
                jnp.zeros((bp, SSM_HEADS, SSM_HEAD_DIM, SSM_STATE), yp.dtype), *ssm_args)
            hs, conv_s, ssm_s = _mamba2_mixer(r3s(ys), state_ssm_conv, state_ssm, *ssm_args)
            w_out = bf(ssm_w_out)
        elif m == 2:
            hp, k_p, v_p, ik_p = _dsa_prompt_pallas(r3p(yp), att_w_in, att_kn_g, att_kn_b)
            hs, k_s, v_s, ik_s = _dsa_sample(r3s(ys), cache_k, cache_v, cache_idx_k, page_table,
                                             att_w_in, att_kn_g, att_kn_b)
            w_out = bf(att_w_out)
        else:
            rw_args = (rw_mu, rw_w_r, rw_w_k, rw_w_v, rw_w0, rw_w1, rw_w2, rw_a0, rw_a1, rw_a2,
                       rw_g1, rw_g2, rw_k_k, rw_k_a, rw_r_k, rw_gn_g, rw_gn_b)
            hp, gate_p, sh_p, wkv_p = _rwkv7_mixer_pallas(
                r3p(yp), jnp.zeros((bp, D_MODEL), yp.dtype),
                jnp.zeros((bp, RW_HEADS, RW_HEAD, RW_HEAD), yp.dtype), *rw_args)
            hs, gate_s, sh_s, wkv_s = _rwkv7_mixer_pallas(r3s(ys), state_rwkv_shift, state_rwkv_wkv, *rw_args)
            w_out = bf(rw_w_o)
        if m == 3:
            yp = _proj_gate_post_norm(yp, hp, gate_p, w_out, ln_g[i, 1], ln_b[i, 1])
            ys = _proj_gate_post_norm(ys, hs, gate_s, w_out, ln_g[i, 1], ln_b[i, 1])
        else:
            yp = _proj_post_norm(yp, f2(hp), w_out, ln_g[i, 1], ln_b[i, 1])
            ys = _proj_post_norm(ys, f2(hs), w_out, ln_g[i, 1], ln_b[i, 1])
        yp = _ffn_sub(yp, w_up_bf[i, 1], w_down_bf[i, 1], ln_g[i, 2], ln_b[i, 2])
        ys = _ffn_sub(ys, w_up_bf[i, 1], w_down_bf[i, 1], ln_g[i, 2], ln_b[i, 2])
        yp = _ple_add(yp, f2(p_prompt[i]), ple_wp_bf[i], ple_wg_bf[i], ple_b_g[i])
        ys = _ple_add(ys, f2(p_sample[i]), ple_wp_bf[i], ple_wg_bf[i], ple_b_g[i])

    return (r3p(yp), r3s(ys), gm_v_s, conv_p, ssm_p, conv_s, ssm_s, k_p, v_p, ik_p, k_s, v_s, ik_s,
            sh_p, wkv_p, sh_s, wkv_s)
```

```python
import functools
import math

import jax
import jax.numpy as jnp
from jax import lax
from jax.experimental import pallas as pl
from jax.experimental.pallas import tpu as pltpu

D_MODEL = 1024
DEPTH = 4
N_MIXERS = 4
PLE_DIM = 256
D_FF = 2816
ALPHA = (2 * DEPTH) ** 0.25
LN_EPS = 1e-5

CHUNK = 128
GM_WIDTH = 2 * D_MODEL
GM_GROUPS = 8
GM_GROUP_DIM = GM_WIDTH // GM_GROUPS

SSM_D_INNER = 2 * D_MODEL
SSM_HEAD_DIM = 64
SSM_HEADS = SSM_D_INNER // SSM_HEAD_DIM
SSM_GROUPS = 4
SSM_HPG = SSM_HEADS // SSM_GROUPS
SSM_STATE = 128
SSM_CONV = 4
SSM_CONV_DIM = SSM_D_INNER + 2 * SSM_GROUPS * SSM_STATE
SSM_CHUNK = 128

ATT_HEADS = 16
ATT_KV_HEADS = 4
ATT_HEAD_DIM = D_MODEL // ATT_HEADS
ROPE_DIM = ATT_HEAD_DIM // 4
ROPE_THETA = 500000.0
IDX_HEADS = 8
IDX_DIM = 64
IDX_ROPE_DIM = IDX_DIM // 4
TOPK_MAX = 256
Q_BLOCK = 128
ATT_Q_DIM = ATT_HEADS * ATT_HEAD_DIM
ATT_KV_DIM = ATT_KV_HEADS * ATT_HEAD_DIM
ATT_IN_SPLITS = (ATT_Q_DIM, ATT_Q_DIM + ATT_KV_DIM, ATT_Q_DIM + 2 * ATT_KV_DIM,
                 ATT_Q_DIM + 2 * ATT_KV_DIM + IDX_HEADS * IDX_DIM,
                 ATT_Q_DIM + 2 * ATT_KV_DIM + IDX_HEADS * IDX_DIM + IDX_DIM)

RW_HEAD = 64
RW_HEADS = D_MODEL // RW_HEAD
RW_GN_EPS = 64e-5

V7X_VMEM_LIMIT_BYTES = 52 * 1024 * 1024
FF_TILE = D_FF // 2
ROW_TILE = 512


def _row_tile(m):
    return ROW_TILE if m % ROW_TILE == 0 else m


def _ln_rows(y, g, b):
    mu = jnp.mean(y, axis=-1, keepdims=True)
    yc = y - mu
    var = jnp.mean(yc * yc, axis=-1, keepdims=True)
    return yc * lax.rsqrt(var + LN_EPS) * g + b


def _ffn_kernel(x_ref, wa_ref, wb_ref, wd_ref, g_ref, b_ref, o_ref, acc_ref):
    f = pl.program_id(1)

    @pl.when(f == 0)
    def _():
        acc_ref[...] = jnp.zeros_like(acc_ref)

    xb = x_ref[...].astype(jnp.bfloat16)
    a = jnp.dot(xb, wa_ref[...], preferred_element_type=jnp.float32)
    b = jnp.dot(xb, wb_ref[...], preferred_element_type=jnp.float32)
    h = (a * jax.nn.sigmoid(a) * b).astype(jnp.bfloat16)
    acc_ref[...] += jnp.dot(h, wd_ref[...], preferred_element_type=jnp.float32)

    @pl.when(f == pl.num_programs(1) - 1)
    def _():
        y = ALPHA * x_ref[...] + 0.5 * acc_ref[...]
        o_ref[...] = _ln_rows(y, g_ref[...], b_ref[...])


def _ffn_sub(x2d, w_up, w_down, g, b):
    m = x2d.shape[0]
    tm = _row_tile(m)
    nf = D_FF // FF_TILE
    return pl.pallas_call(
        _ffn_kernel,
        grid=(m // tm, nf),
        in_specs=[
            pl.BlockSpec((tm, D_MODEL), lambda i, f: (i, 0)),
            pl.BlockSpec((D_MODEL, FF_TILE), lambda i, f: (0, f)),
            pl.BlockSpec((D_MODEL, FF_TILE), lambda i, f: (0, f + nf)),
            pl.BlockSpec((FF_TILE, D_MODEL), lambda i, f: (f, 0)),
            pl.BlockSpec((1, D_MODEL), lambda i, f: (0, 0)),
            pl.BlockSpec((1, D_MODEL), lambda i, f: (0, 0)),
        ],
        out_specs=pl.BlockSpec((tm, D_MODEL), lambda i, f: (i, 0)),
        out_shape=jax.ShapeDtypeStruct((m, D_MODEL), jnp.float32),
        scratch_shapes=[pltpu.VMEM((tm, D_MODEL), jnp.float32)],
        compiler_params=pltpu.CompilerParams(
            dimension_semantics=("parallel", "arbitrary"),
            vmem_limit_bytes=V7X_VMEM_LIMIT_BYTES),
        name="ffn_sub",
    )(x2d, w_up, w_up, w_down, g.reshape(1, D_MODEL), b.reshape(1, D_MODEL))


def _ple_kernel(x_ref, p_ref, wp_ref, wg_ref, bg_ref, o_ref):
    x = x_ref[...]
    gate = jax.nn.sigmoid(
        jnp.dot(x.astype(jnp.bfloat16), wg_ref[...], preferred_element_type=jnp.float32) + bg_ref[...])
    emb = jnp.dot(p_ref[...].astype(jnp.bfloat16), wp_ref[...], preferred_element_type=jnp.float32)
    o_ref[...] = x + gate * emb


def _ple_add(x2d, p2d, w_p, w_g, b_g):
    m = x2d.shape[0]
    tm = _row_tile(m)
    return pl.pallas_call(
        _ple_kernel,
        grid=(m // tm,),
        in_specs=[
            pl.BlockSpec((tm, D_MODEL), lambda i: (i, 0)),
            pl.BlockSpec((tm, PLE_DIM), lambda i: (i, 0)),
            pl.BlockSpec((PLE_DIM, D_MODEL), lambda i: (0, 0)),
            pl.BlockSpec((D_MODEL, D_MODEL), lambda i: (0, 0)),
            pl.BlockSpec((1, D_MODEL), lambda i: (0, 0)),
        ],
        out_specs=pl.BlockSpec((tm, D_MODEL), lambda i: (i, 0)),
        out_shape=jax.ShapeDtypeStruct((m, D_MODEL), jnp.float32),
        compiler_params=pltpu.CompilerParams(
            dimension_semantics=("parallel",),
            vmem_limit_bytes=V7X_VMEM_LIMIT_BYTES),
        name="ple_add",
    )(x2d, p2d, w_p, w_g, b_g.reshape(1, D_MODEL))


def _proj_ln_kernel(x_ref, h_ref, w_ref, g_ref, b_ref, o_ref):
    y = ALPHA * x_ref[...] + jnp.dot(h_ref[...].astype(jnp.bfloat16), w_ref[...],
                                     preferred_element_type=jnp.float32)
    o_ref[...] = _ln_rows(y, g_ref[...], b_ref[...])


def _proj_post_norm(x2d, h2d, w_out, g, b):
    m = x2d.shape[0]
    k = h2d.shape[1]
    tm = _row_tile(m)
    return pl.pallas_call(
        _proj_ln_kernel,
        grid=(m // tm,),
        in_specs=[
            pl.BlockSpec((tm, D_MODEL), lambda i: (i, 0)),
            pl.BlockSpec((tm, k), lambda i: (i, 0)),
            pl.BlockSpec((k, D_MODEL), lambda i: (0, 0)),
            pl.BlockSpec((1, D_MODEL), lambda i: (0, 0)),
            pl.BlockSpec((1, D_MODEL), lambda i: (0, 0)),
        ],
        out_specs=pl.BlockSpec((tm, D_MODEL), lambda i: (i, 0)),
        out_shape=jax.ShapeDtypeStruct((m, D_MODEL), jnp.float32),
        compiler_params=pltpu.CompilerParams(
            dimension_semantics=("parallel",),
            vmem_limit_bytes=V7X_VMEM_LIMIT_BYTES),
        name="proj_post_norm",
    )(x2d, h2d, w_out, g.reshape(1, D_MODEL), b.reshape(1, D_MODEL))


MXU_DTYPE = jnp.bfloat16
KEY_GROUP = 512
INT32_MIN = -2 ** 31
MASK_NEG = -1e30


def _rope_lane_tables(pos, rot_dim, head_dim):
    half = rot_dim // 2
    inv = ROPE_THETA ** (-jnp.arange(half, dtype=jnp.float32) / half)
    ang = pos.astype(jnp.float32)[:, None] * inv[None, :]
    cos, sin = jnp.cos(ang), jnp.sin(ang)
    n = pos.shape[0]
    rest = head_dim - rot_dim
    c = jnp.concatenate([cos, cos, jnp.ones((n, rest), jnp.float32)], axis=1)
    s1 = jnp.concatenate([-sin, jnp.zeros((n, half + rest), jnp.float32)], axis=1)
    s2 = jnp.concatenate([jnp.zeros((n, half), jnp.float32), sin, jnp.zeros((n, rest), jnp.float32)], axis=1)
    reps = 128 // head_dim
    tile = lambda t: jnp.tile(t, (1, reps))
    return tile(c), tile(s1), tile(s2), cos.T, sin.T


def _rope_lanes(t, c, s1, s2, half):
    n = t.shape[1]
    reps = n // 128
    tl = lambda a: jnp.concatenate([a] * reps, axis=1)
    return t * tl(c) + pltpu.roll(t, n - half, 1) * tl(s1) + pltpu.roll(t, half, 1) * tl(s2)


def _rope_rows(t, cT, sT, head_dim, half):
    pieces = []
    for h in range(t.shape[0] // head_dim):
        x1 = t[h * head_dim:h * head_dim + half]
        x2 = t[h * head_dim + half:h * head_dim + 2 * half]
        pieces += [x1 * cT - x2 * sT, x2 * cT + x1 * sT, t[h * head_dim + 2 * half:(h + 1) * head_dim]]
    return jnp.concatenate(pieces, axis=0)


def _dsa_proj_kernel(x_ref, wq_ref, wiq_ref, wv_ref, wvx_ref, wiw_ref, wkT_ref, wikT_ref,
                     c_ref, s1_ref, s2_ref, cT_ref, sT_ref, kng_ref, knb_ref, one_ref,
                     q_ref, iq_ref, v_ref, vx_ref, iw_ref, kT_ref, kTb_ref, ikT_ref, ikTb_ref):
    xb = x_ref[...].astype(MXU_DTYPE)
    c, s1, s2 = c_ref[...], s1_ref[...], s2_ref[...]
    cT, sT = cT_ref[...], sT_ref[...]
    dot = lambda a, b: jnp.dot(a, b, preferred_element_type=jnp.float32)
    dot_t = lambda w, a: lax.dot_general(w, a, (((1,), (1,)), ((), ())), preferred_element_type=jnp.float32)

    q = _rope_lanes(dot(xb, wq_ref[...]), c, s1, s2, ROPE_DIM // 2)
    q_ref[...] = (q * (ATT_HEAD_DIM ** -0.5)).astype(q_ref.dtype)
    iq = _rope_lanes(dot(xb, wiq_ref[...]), c, s1, s2, IDX_ROPE_DIM // 2)
    iq_ref[...] = iq.astype(iq_ref.dtype)
    v_ref[...] = dot(xb, wv_ref[...])
    vx_ref[...] = (dot(xb, wvx_ref[...]) + one_ref[...]).astype(vx_ref.dtype)
    iw_ref[...] = dot(xb, wiw_ref[...]) * (IDX_HEADS ** -0.5 * IDX_DIM ** -0.5)

    kT = _rope_rows(dot_t(wkT_ref[...], xb), cT, sT, ATT_HEAD_DIM, ROPE_DIM // 2)
    kT_ref[0, 0] = kT
    kTb_ref[0, 0] = kT.astype(kTb_ref.dtype)
    ikT = dot_t(wikT_ref[...], xb)
    mu = jnp.mean(ikT, axis=0, keepdims=True)
    ikc = ikT - mu
    var = jnp.mean(ikc * ikc, axis=0, keepdims=True)
    ikT = ikc * lax.rsqrt(var + LN_EPS) * kng_ref[...] + knb_ref[...]
    ikT = _rope_rows(ikT, cT, sT, IDX_DIM, IDX_ROPE_DIM // 2)
    ikT_ref[0, 0] = ikT
    ikTb_ref[0, 0] = ikT.astype(ikTb_ref.dtype)


def _dsa_project(x3d, pos, w_in, kn_g, kn_b):
    b_, t_, _ = x3d.shape
    tk = KEY_GROUP if t_ % KEY_GROUP == 0 else t_
    ng = t_ // tk
    m = b_ * t_
    w_q, w_k, w_v, w_iq, w_ik, w_iw = jnp.split(w_in, list(ATT_IN_SPLITS), axis=1)
    cast = lambda w: w.astype(MXU_DTYPE)
    w_vx = jnp.pad(w_v.reshape(D_MODEL, ATT_KV_HEADS, ATT_HEAD_DIM),
                   ((0, 0), (0, 0), (0, 128 - ATT_HEAD_DIM))).reshape(D_MODEL, ATT_KV_HEADS * 128)
    one_col = jnp.tile((jnp.arange(128) == ATT_HEAD_DIM).astype(jnp.float32), ATT_KV_HEADS)[None, :]
    w_iw_pad = jnp.pad(w_iw, ((0, 0), (0, 128 - IDX_HEADS)))
    c, s1, s2, cT, sT = _rope_lane_tables(pos, ROPE_DIM, ATT_HEAD_DIM)
    full = lambda shape: pl.BlockSpec(shape, lambda b, i: (0,) * len(shape))
    rows = lambda n: pl.BlockSpec((tk, n), lambda b, i: (b * ng + i, 0))
    ptab = lambda n: pl.BlockSpec((tk, n), lambda b, i: (i, 0))
    grp = lambda n: pl.BlockSpec((1, 1, n, tk), lambda b, i: (b, i, 0, 0))
    sds = jax.ShapeDtypeStruct
    return pl.pallas_call(
        _dsa_proj_kernel,
        grid=(b_, ng),
        in_specs=[rows(D_MODEL), full((D_MODEL, ATT_Q_DIM)), full((D_MODEL, IDX_HEADS * IDX_DIM)),
                  full((D_MODEL, ATT_KV_DIM)), full((D_MODEL, ATT_KV_HEADS * 128)), full((D_MODEL, 128)),
                  full((ATT_KV_DIM, D_MODEL)), full((IDX_DIM, D_MODEL)),
                  ptab(128), ptab(128), ptab(128),
                  pl.BlockSpec((ROPE_DIM // 2, tk), lambda b, i: (0, i)),
                  pl.BlockSpec((ROPE_DIM // 2, tk), lambda b, i: (0, i)),
                  full((IDX_DIM, 1)), full((IDX_DIM, 1)), full((1, ATT_KV_HEADS * 128))],
        out_specs=[rows(ATT_Q_DIM), rows(IDX_HEADS * IDX_DIM), rows(ATT_KV_DIM), rows(ATT_KV_HEADS * 128),
                   rows(128), grp(ATT_KV_DIM), grp(ATT_KV_DIM), grp(IDX_DIM), grp(IDX_DIM)],
        out_shape=[sds((m, ATT_Q_DIM), MXU_DTYPE), sds((m, IDX_HEADS * IDX_DIM), MXU_DTYPE),
                   sds((m, ATT_KV_DIM), jnp.float32), sds((m, ATT_KV_HEADS * 128), MXU_DTYPE),
                   sds((m, 128), jnp.float32),
                   sds((b_, ng, ATT_KV_DIM, tk), jnp.float32), sds((b_, ng, ATT_KV_DIM, tk), MXU_DTYPE),
                   sds((b_, ng, IDX_DIM, tk), jnp.float32), sds((b_, ng, IDX_DIM, tk), MXU_DTYPE)],
        compiler_params=pltpu.CompilerParams(
            dimension_semantics=("parallel", "parallel"),
            vmem_limit_bytes=V7X_VMEM_LIMIT_BYTES),
        name="dsa_project",
    )(x3d.reshape(m, D_MODEL), cast(w_q), cast(w_iq), cast(w_v), cast(w_vx), cast(w_iw_pad),
      cast(w_k.T), cast(w_ik.T), c, s1, s2, cT, sT, kn_g.reshape(IDX_DIM, 1), kn_b.reshape(IDX_DIM, 1), one_col)


def _untranspose_groups(tg):
    b_, g_, r_, tk = tg.shape
    return jnp.transpose(tg, (0, 1, 3, 2)).reshape(b_, g_ * tk, r_)


def _dsa_attend_kernel(iq_ref, iw_ref, ikT_ref, q_ref, kT_ref, vx_ref, o_ref, key_ref, m_ref, acc_ref, *,
                       topk, col_bits):
    j = pl.program_id(1)
    tq = iq_ref.shape[0]
    tk = key_ref.shape[2]
    n_groups = (j * tq + tq + tk - 1) // tk
    row = j * tq + lax.broadcasted_iota(jnp.int32, (tq, tk), 0)
    col0 = lax.broadcasted_iota(jnp.int32, (tq, tk), 1)
    dot = lambda a, b: jnp.dot(a, b, preferred_element_type=jnp.float32)

    def score_body(g, carry):
        ikT = ikT_ref[0, g]
        sc = jnp.zeros((tq, tk), jnp.float32)
        for h in range(IDX_HEADS):
            s = dot(iq_ref[:, h * IDX_DIM:(h + 1) * IDX_DIM], ikT)
            sc = sc + iw_ref[:, h:h + 1] * jnp.maximum(s, 0.0)
        bits = pltpu.bitcast(sc, jnp.int32)
        key = jnp.where(bits >= 0, bits, bits ^ jnp.int32(0x7FFFFFFF))
        key_ref[g] = jnp.where(col0 + g * tk <= row, key, jnp.int32(INT32_MIN))
        return carry

    lax.fori_loop(0, n_groups, score_body, 0)

    def bit_body(i, thr):
        cand = thr ^ lax.shift_left(jnp.int32(1), jnp.int32(31) - i)

        def count_body(g, cnt):
            hit = jnp.where(key_ref[g] >= cand, 1.0, 0.0)
            for l in range(tk // 128):
                cnt = cnt + hit[:, l * 128:(l + 1) * 128]
            return cnt

        cnt = lax.fori_loop(0, n_groups, count_body, jnp.zeros((tq, 128), jnp.float32))
        total = jnp.sum(cnt, axis=1, keepdims=True)
        return jnp.where(total >= float(topk), cand, thr)

    thr = lax.fori_loop(0, 32, bit_body, jnp.full((tq, 1), INT32_MIN, jnp.int32))

    def lane_fold(hit, cnt):
        for l in range(tk // 128):
            cnt = cnt + hit[:, l * 128:(l + 1) * 128]
        return cnt

    def above_body(g, cnt):
        return lane_fold(jnp.where(key_ref[g] > thr, 1.0, 0.0), cnt)

    n_above = jnp.sum(lax.fori_loop(0, n_groups, above_body, jnp.zeros((tq, 128), jnp.float32)),
                      axis=1, keepdims=True)
    need = float(topk) - n_above

    def col_body(i, last):
        cand = last | lax.shift_left(jnp.int32(1), jnp.int32(col_bits - 1) - i)

        def tie_body(g, cnt):
            hit = jnp.where((key_ref[g] == thr) & (col0 + g * tk < cand), 1.0, 0.0)
            return lane_fold(hit, cnt)

        ties = jnp.sum(lax.fori_loop(0, n_groups, tie_body, jnp.zeros((tq, 128), jnp.float32)),
                       axis=1, keepdims=True)
        return jnp.where(ties < need, cand, last)

    last_tie = lax.fori_loop(0, col_bits, col_body, jnp.zeros((tq, 1), jnp.int32))

    m_ref[...] = jnp.full(m_ref.shape, MASK_NEG, jnp.float32)
    acc_ref[...] = jnp.zeros(acc_ref.shape, jnp.float32)
    gsz = ATT_HEADS // ATT_KV_HEADS

    def attend_body(g, carry):
        key = key_ref[g]
        col = col0 + g * tk
        keep = (key > thr) | ((key == thr) & (col <= last_tie))
        bias = jnp.where(keep & (col <= row), 0.0, MASK_NEG)
        start = pl.multiple_of(g * tk, tk)
        for h in range(ATT_HEADS):
            kv = h // gsz
            s = dot(q_ref[:, h * ATT_HEAD_DIM:(h + 1) * ATT_HEAD_DIM],
                    kT_ref[0, g, kv * ATT_HEAD_DIM:(kv + 1) * ATT_HEAD_DIM, :]) + bias
            m_old = m_ref[h]
            m_new = jnp.maximum(m_old, jnp.max(s, axis=1, keepdims=True))
            p = jnp.exp(s - m_new).astype(vx_ref.dtype)
            pv = dot(p, vx_ref[0, pl.ds(start, tk), kv * 128:(kv + 1) * 128])
            acc_ref[h] = jnp.exp(m_old - m_new) * acc_ref[h] + pv
            m_ref[h] = m_new
        return carry

    lax.fori_loop(0, n_groups, attend_body, 0)

    for h in range(ATT_HEADS):
        a = acc_ref[h]
        o_ref[:, h * ATT_HEAD_DIM:(h + 1) * ATT_HEAD_DIM] = (
            a[:, :ATT_HEAD_DIM] / a[:, ATT_HEAD_DIM:ATT_HEAD_DIM + 1]).astype(o_ref.dtype)


def _dsa_attend(b_, t_, q, iq, iw, ikTb, kTb, vx):
    ng, tk = kTb.shape[1], kTb.shape[3]
    tq = Q_BLOCK
    nq = t_ // tq
    rows = lambda n: pl.BlockSpec((tq, n), lambda b, j: (b * nq + j, 0))
    return pl.pallas_call(
        functools.partial(_dsa_attend_kernel, topk=min(TOPK_MAX, t_ // 4), col_bits=max(1, (t_ - 1).bit_length())),
        grid=(b_, nq),
        in_specs=[rows(IDX_HEADS * IDX_DIM), rows(128),
                  pl.BlockSpec((1, ng, IDX_DIM, tk), lambda b, j: (b, 0, 0, 0)),
                  rows(ATT_Q_DIM),
                  pl.BlockSpec((1, ng, ATT_KV_DIM, tk), lambda b, j: (b, 0, 0, 0)),
                  pl.BlockSpec((1, t_, ATT_KV_HEADS * 128), lambda b, j: (b, 0, 0))],
        out_specs=rows(ATT_Q_DIM),
        out_shape=jax.ShapeDtypeStruct((b_ * t_, ATT_Q_DIM), MXU_DTYPE),
        scratch_shapes=[pltpu.VMEM((ng, tq, tk), jnp.int32),
                        pltpu.VMEM((ATT_HEADS, tq, 1), jnp.float32),
                        pltpu.VMEM((ATT_HEADS, tq, 128), jnp.float32)],
        compiler_params=pltpu.CompilerParams(
            dimension_semantics=("parallel", "arbitrary"),
            vmem_limit_bytes=V7X_VMEM_LIMIT_BYTES),
        name="dsa_attend",
    )(iq, iw, ikTb, q, kTb, vx.reshape(b_, t_, ATT_KV_HEADS * 128))


def _dsa_prompt_pallas(x3d, w_in, kn_g, kn_b):
    b_, t_, _ = x3d.shape
    q, iq, v, vx, iw, kT, kTb, ikT, ikTb = _dsa_project(x3d, jnp.arange(t_), w_in, kn_g, kn_b)
    o = _dsa_attend(b_, t_, q, iq, iw, ikTb, kTb, vx)
    k = _untranspose_groups(kT).reshape(b_, t_, ATT_KV_HEADS, ATT_HEAD_DIM)
    ik = _untranspose_groups(ikT)
    return o, k, v.reshape(b_, t_, ATT_KV_HEADS, ATT_HEAD_DIM), ik


RW_ROW_TILE = 256


def _rwkv_proj_kernel(x_ref, xp_ref, mu_ref, wr_ref, wk_ref, wv_ref, w1_ref, w2_ref, a1_ref, a2_ref,
                      g1_ref, g2_ref, w0_ref, a0_ref, r_ref, d_ref, k_ref, v_ref, a_ref, g_ref):
    x = x_ref[...]
    dx = xp_ref[...] - x
    mix = lambda c: (x + dx * mu_ref[c:c + 1, :]).astype(MXU_DTYPE)
    dot = lambda a, b: jnp.dot(a.astype(MXU_DTYPE), b, preferred_element_type=jnp.float32)
    r_ref[...] = dot(mix(0), wr_ref[...])
    lora_w = dot(jnp.tanh(dot(mix(1), w1_ref[...])), w2_ref[...])
    w_log = -jax.nn.softplus(-(w0_ref[...] + lora_w)) - 0.5
    d_ref[...] = jnp.exp(-jnp.exp(w_log))
    k_ref[...] = dot(mix(2), wk_ref[...])
    v_ref[...] = dot(mix(3), wv_ref[...])
    a_ref[...] = jax.nn.sigmoid(a0_ref[...] + dot(dot(mix(4), a1_ref[...]), a2_ref[...]))
    g_ref[...] = dot(jax.nn.sigmoid(dot(mix(5), g1_ref[...])), g2_ref[...])


def _rwkv_project(x2d, xprev2d, mu, w_r, w_k, w_v, w0, w1, w2, a0, a1, a2, g1, g2):
    m = x2d.shape[0]
    tm = RW_ROW_TILE if m % RW_ROW_TILE == 0 else m
    cast = lambda w: w.astype(MXU_DTYPE)
    full = lambda a: pl.BlockSpec(a.shape, lambda i: (0,) * a.ndim)
    rows = pl.BlockSpec((tm, D_MODEL), lambda i: (i, 0))
    consts = [mu, cast(w_r), cast(w_k), cast(w_v), cast(w1), cast(w2), cast(a1), cast(a2), cast(g1), cast(g2),
              w0.reshape(1, D_MODEL), a0.reshape(1, D_MODEL)]
    return pl.pallas_call(
        _rwkv_proj_kernel,
        grid=(m // tm,),
        in_specs=[rows, rows] + [full(a) for a in consts],
        out_specs=[rows] * 6,
        out_shape=[jax.ShapeDtypeStruct((m, D_MODEL), jnp.float32)] * 6,
        compiler_params=pltpu.CompilerParams(
            dimension_semantics=("parallel",),
            vmem_limit_bytes=V7X_VMEM_LIMIT_BYTES),
        name="rwkv_project",
    )(x2d, xprev2d, *consts)


RW_LANES = 128
RW_TIME_CHUNK = 32


def _rwkv_scan_kernel(r_ref, d_ref, k_ref, v_ref, a_ref, s0_ref, kk_ref, ka_ref, rk_ref, gg_ref, gb_ref,
                      z_ref, s_out_ref, s_ref, vec_ref):
    c = pl.program_id(1)
    n = RW_HEAD

    @pl.when(c == 0)
    def _():
        s_ref[...] = s0_ref[...]

    def step(t, carry):
        r, k, v, a = r_ref[t], k_ref[t], v_ref[t], a_ref[t]
        kkr = k * kk_ref[...]
        nrm = jnp.sqrt(jnp.sum(kkr * kkr, axis=0, keepdims=True))
        kk = kkr / jnp.maximum(nrm, 1e-12)
        kmod = k * (1.0 + (a - 1.0) * ka_ref[...])
        vec_ref[0] = kk
        vec_ref[1] = d_ref[t]
        vec_ref[2] = kk * a
        vec_ref[3] = kmod
        vec_ref[4] = r
        row = lambda q, j: vec_ref[q, j:j + 1, :]
        sa = jnp.zeros((n, RW_LANES), jnp.float32)
        for j in range(n):
            sa = sa + s_ref[j] * row(0, j)
        y = jnp.zeros((n, RW_LANES), jnp.float32)
        for j in range(n):
            sn = s_ref[j] * row(1, j) - sa * row(2, j) + v * row(3, j)
            s_ref[j] = sn
            y = y + sn * row(4, j)
        mu = jnp.mean(y, axis=0, keepdims=True)
        yc = y - mu
        var = jnp.mean(yc * yc, axis=0, keepdims=True)
        bonus = jnp.sum(r * kmod * rk_ref[...], axis=0, keepdims=True)
        z_ref[t] = yc * lax.rsqrt(var + RW_GN_EPS) * gg_ref[...] + gb_ref[...] + bonus * v
        return carry

    lax.fori_loop(0, r_ref.shape[0], step, 0)

    @pl.when(c == pl.num_programs(1) - 1)
    def _():
        s_out_ref[...] = s_ref[...]


def _rwkv_scan(rT, dT, kT, vT, aT, s0T, k_k, k_a, r_k, gn_g, gn_b):
    t_, n, bh = rT.shape
    tc = RW_TIME_CHUNK if t_ % RW_TIME_CHUNK == 0 else t_
    reps = RW_LANES // RW_HEADS
    table = lambda p: jnp.tile(p.reshape(RW_HEADS, n).T, (1, reps))
    seq = pl.BlockSpec((tc, n, RW_LANES), lambda l, c: (c, 0, l))
    state = pl.BlockSpec((n, n, RW_LANES), lambda l, c: (0, 0, l))
    tab = pl.BlockSpec((n, RW_LANES), lambda l, c: (0, 0))
    return pl.pallas_call(
        _rwkv_scan_kernel,
        grid=(bh // RW_LANES, t_ // tc),
        in_specs=[seq] * 5 + [state] + [tab] * 5,
        out_specs=[seq, state],
        out_shape=[jax.ShapeDtypeStruct((t_, n, bh), jnp.float32),
                   jax.ShapeDtypeStruct((n, n, bh), jnp.float32)],
        scratch_shapes=[pltpu.VMEM((n, n, RW_LANES), jnp.float32),
                        pltpu.VMEM((5, n, RW_LANES), jnp.float32)],
        compiler_params=pltpu.CompilerParams(
            dimension_semantics=("parallel", "arbitrary"),
            vmem_limit_bytes=V7X_VMEM_LIMIT_BYTES),
        name="rwkv_scan",
    )(rT, dT, kT, vT, aT, s0T, table(k_k), table(k_a), table(r_k), table(gn_g), table(gn_b))


def _rwkv7_mixer_pallas(x3d, shift, wkv, mu, w_r, w_k, w_v, w0, w1, w2, a0, a1, a2, g1, g2,
                        k_k, k_a, r_k, gn_g, gn_b):
    b_, t_, _ = x3d.shape
    m = b_ * t_
    x_prev = jnp.concatenate([shift[:, None, :], x3d[:, :-1]], axis=1)
    r, d, k, v, a, g = _rwkv_project(x3d.reshape(m, D_MODEL), x_prev.reshape(m, D_MODEL), mu,
                                     w_r, w_k, w_v, w0, w1, w2, a0, a1, a2, g1, g2)
    to_scan = lambda u: jnp.transpose(u.reshape(b_, t_, RW_HEADS, RW_HEAD), (1, 3, 0, 2)).reshape(
        t_, RW_HEAD, b_ * RW_HEADS)
    s0T = jnp.transpose(wkv.astype(jnp.float32), (3, 2, 0, 1)).reshape(RW_HEAD, RW_HEAD, b_ * RW_HEADS)
    zT, sT = _rwkv_scan(to_scan(r), to_scan(d), to_scan(k), to_scan(v), to_scan(a), s0T,
                        k_k, k_a, r_k, gn_g, gn_b)
    z = jnp.transpose(zT.reshape(t_, RW_HEAD, b_, RW_HEADS), (2, 0, 3, 1)).reshape(m, D_MODEL)
    s_new = jnp.transpose(sT.reshape(RW_HEAD, RW_HEAD, b_, RW_HEADS), (2, 3, 1, 0)).astype(wkv.dtype)
    return z, g, x3d[:, -1], s_new


def _proj_gate_ln_kernel(x_ref, h_ref, gate_ref, w_ref, g_ref, b_ref, o_ref):
    h = (h_ref[...] * gate_ref[...]).astype(MXU_DTYPE)
    y = ALPHA * x_ref[...] + jnp.dot(h, w_ref[...], preferred_element_type=jnp.float32)
    o_ref[...] = _ln_rows(y, g_ref[...], b_ref[...])


def _proj_gate_post_norm(x2d, h2d, gate2d, w_out, g, b):
    m = x2d.shape[0]
    tm = _row_tile(m)
    rows = pl.BlockSpec((tm, D_MODEL), lambda i: (i, 0))
    vec = pl.BlockSpec((1, D_MODEL), lambda i: (0, 0))
    return pl.pallas_call(
        _proj_gate_ln_kernel,
        grid=(m // tm,),
        in_specs=[rows, rows, rows, pl.BlockSpec((D_MODEL, D_MODEL), lambda i: (0, 0)), vec, vec],
        out_specs=rows,
        out_shape=jax.ShapeDtypeStruct((m, D_MODEL), jnp.float32),
        compiler_params=pltpu.CompilerParams(
            dimension_semantics=("parallel",),
            vmem_limit_bytes=V7X_VMEM_LIMIT_BYTES),
        name="proj_gate_post_norm",
    )(x2d, h2d, gate2d, w_out, g.reshape(1, D_MODEL), b.reshape(1, D_MODEL))


GM_ROW_TILE = 256


def _gmlp_kernel(x_ref, win_ref, lng_ref, lnb_ref, mixw_ref, mixb_ref, wout_ref, g_ref, b_ref, *out_refs,
                 chunk_len, emit_v):
    x = x_ref[...]
    h = jax.nn.gelu(jnp.dot(x.astype(MXU_DTYPE), win_ref[...], preferred_element_type=jnp.float32))
    u = h[:, :GM_WIDTH]
    v = _ln_rows(h[:, GM_WIDTH:], lng_ref[...], lnb_ref[...])
    if emit_v:
        out_refs[1][...] = v
    if chunk_len == 1:
        gated = u * (v * mixw_ref[...] + mixb_ref[...])
    else:
        tm = x.shape[0]
        causal = (lax.broadcasted_iota(jnp.int32, (chunk_len, chunk_len), 0)
                  >= lax.broadcasted_iota(jnp.int32, (chunk_len, chunk_len), 1))
        vb = v.astype(MXU_DTYPE)
        cols = []
        for g in range(GM_GROUPS):
            w = jnp.where(causal, mixw_ref[g], 0.0).astype(MXU_DTYPE)
            bias = mixb_ref[:, g:g + 1]
            lanes = slice(g * GM_GROUP_DIM, (g + 1) * GM_GROUP_DIM)
            rows = [jnp.dot(w, vb[c * chunk_len:(c + 1) * chunk_len, lanes],
                            preferred_element_type=jnp.float32) + bias
                    for c in range(tm // chunk_len)]
            cols.append(jnp.concatenate(rows, axis=0))
        gated = u * jnp.concatenate(cols, axis=1)
    y = ALPHA * x + jnp.dot(gated.astype(MXU_DTYPE), wout_ref[...], preferred_element_type=jnp.float32)
    out_refs[0][...] = _ln_rows(y, g_ref[...], b_ref[...])


def _gmlp_block(x2d, seq_len, w_in, ln_g, ln_b, ws, bs, w_out, g, b, emit_v):
    m = x2d.shape[0]
    chunk_len = min(seq_len, CHUNK)
    if chunk_len == 1:
        tm = m
        mixw = jnp.repeat(ws[:, 0, 0], GM_GROUP_DIM)[None, :]
        mixb = jnp.repeat(bs[:, 0], GM_GROUP_DIM)[None, :]
    else:
        tm = GM_ROW_TILE
        mixw = ws[:, :chunk_len, :chunk_len]
        mixb = bs[:, :chunk_len].T
    full = lambda a: pl.BlockSpec(a.shape, lambda i: (0,) * a.ndim)
    rows = lambda n: pl.BlockSpec((tm, n), lambda i: (i, 0))
    consts = [w_in.astype(MXU_DTYPE), ln_g.reshape(1, GM_WIDTH), ln_b.reshape(1, GM_WIDTH), mixw, mixb,
              w_out.astype(MXU_DTYPE), g.reshape(1, D_MODEL), b.reshape(1, D_MODEL)]
    out_specs = [rows(D_MODEL)] + ([rows(GM_WIDTH)] if emit_v else [])
    out_shape = [jax.ShapeDtypeStruct((m, D_MODEL), jnp.float32)] + (
        [jax.ShapeDtypeStruct((m, GM_WIDTH), jnp.float32)] if emit_v else [])
    return pl.pallas_call(
        functools.partial(_gmlp_kernel, chunk_len=chunk_len, emit_v=emit_v),
        grid=(m // tm,),
        in_specs=[rows(D_MODEL)] + [full(a) for a in consts],
        out_specs=out_specs,
        out_shape=out_shape,
        compiler_params=pltpu.CompilerParams(
            dimension_semantics=("parallel",),
            vmem_limit_bytes=V7X_VMEM_LIMIT_BYTES),
        name="gmlp_block",
    )(x2d, *consts)


SSM_ROW_TILE = 256
SSM_BC_DIM = SSM_GROUPS * SSM_STATE
SSM_DT_LANES = 128
SUBLANES = 8


def _ssm_activate(xb, xbc, taps, wz_ref, wdt_ref, cw_ref, cb_ref, dtb_ref, z_ref, xs_ref, bm_ref, cm_ref, dt_ref):
    conv = cb_ref[...] + xbc * cw_ref[SSM_CONV - 1:SSM_CONV, :]
    for j in range(SSM_CONV - 1):
        conv = conv + taps[j] * cw_ref[j:j + 1, :]
    act = conv * jax.nn.sigmoid(conv)
    xs_ref[...] = act[:, :SSM_D_INNER]
    bm_ref[...] = act[:, SSM_D_INNER:SSM_D_INNER + SSM_BC_DIM].astype(bm_ref.dtype)
    cm_ref[...] = act[:, SSM_D_INNER + SSM_BC_DIM:].astype(cm_ref.dtype)
    z_ref[...] = jnp.dot(xb, wz_ref[...], preferred_element_type=jnp.float32)
    dt_ref[...] = jax.nn.softplus(jnp.dot(xb, wdt_ref[...], preferred_element_type=jnp.float32) + dtb_ref[...])


def _ssm_proj_seq_kernel(x_ref, halo_ref, cs_ref, wx_ref, wz_ref, wdt_ref, cw_ref, cb_ref, dtb_ref,
                         z_ref, xs_ref, bm_ref, cm_ref, dt_ref, tail_ref):
    i = pl.program_id(1)
    xb = x_ref[...].astype(MXU_DTYPE)
    xbc = jnp.dot(xb, wx_ref[...], preferred_element_type=jnp.float32)
    tm = xbc.shape[0]
    prev = jnp.dot(halo_ref[...].astype(MXU_DTYPE), wx_ref[...], preferred_element_type=jnp.float32)
    prev = jnp.where(i == 0, cs_ref[0], prev)
    row = lax.broadcasted_iota(jnp.int32, (tm, 1), 0)
    pad = jnp.zeros((tm - SUBLANES, xbc.shape[1]), jnp.float32)
    taps = []
    for j in range(SSM_CONV - 1):
        back = SSM_CONV - 1 - j
        head = jnp.concatenate([pltpu.roll(prev, back, 0), pad], axis=0)
        taps.append(jnp.where(row < back, head, pltpu.roll(xbc, back, 0)))
    _ssm_activate(xb, xbc, taps, wz_ref, wdt_ref, cw_ref, cb_ref, dtb_ref, z_ref, xs_ref, bm_ref, cm_ref, dt_ref)
    tail_ref[0] = xbc[tm - SUBLANES:, :]


def _ssm_proj_step_kernel(x_ref, st_ref, wx_ref, wz_ref, wdt_ref, cw_ref, cb_ref, dtb_ref,
                          z_ref, xs_ref, bm_ref, cm_ref, dt_ref, st_out_ref):
    xb = x_ref[...].astype(MXU_DTYPE)
    xbc = jnp.dot(xb, wx_ref[...], preferred_element_type=jnp.float32)
    taps = [st_ref[j] for j in range(SSM_CONV - 1)]
    _ssm_activate(xb, xbc, taps, wz_ref, wdt_ref, cw_ref, cb_ref, dtb_ref, z_ref, xs_ref, bm_ref, cm_ref, dt_ref)
    for j in range(SSM_CONV - 2):
        st_out_ref[j] = st_ref[j + 1]
    st_out_ref[SSM_CONV - 2] = xbc


def _ssm_project(x3d, conv_state, w_in, conv_w, conv_b, dt_bias):
    b_, t_, _ = x3d.shape
    m = b_ * t_
    w_z, w_x, w_dt = jnp.split(w_in, [SSM_D_INNER, SSM_D_INNER + SSM_CONV_DIM], axis=1)
    cast = lambda w: w.astype(MXU_DTYPE)
    consts = [cast(w_x), cast(w_z), cast(jnp.pad(w_dt, ((0, 0), (0, SSM_DT_LANES - SSM_HEADS)))),
              conv_w, conv_b.reshape(1, SSM_CONV_DIM),
              jnp.pad(dt_bias, (0, SSM_DT_LANES - SSM_HEADS)).reshape(1, SSM_DT_LANES)]
    sds = jax.ShapeDtypeStruct
    outs = [sds((m, SSM_D_INNER), jnp.float32), sds((m, SSM_D_INNER), jnp.float32),
            sds((m, SSM_BC_DIM), MXU_DTYPE), sds((m, SSM_BC_DIM), MXU_DTYPE), sds((m, SSM_DT_LANES), jnp.float32)]
    widths = [SSM_D_INNER, SSM_D_INNER, SSM_BC_DIM, SSM_BC_DIM, SSM_DT_LANES]
    params = dict(vmem_limit_bytes=V7X_VMEM_LIMIT_BYTES)
    x2d = x3d.reshape(m, D_MODEL)
    if t_ == 1:
        full = lambda a: pl.BlockSpec(a.shape, lambda i: (0,) * a.ndim)
        st = jnp.transpose(conv_state, (1, 0, 2))
        res = pl.pallas_call(
            _ssm_proj_step_kernel,
            grid=(1,),
            in_specs=[full(x2d), full(st)] + [full(a) for a in consts],
            out_specs=[pl.BlockSpec((m, w), lambda i: (0, 0)) for w in widths] + [full(st)],
            out_shape=outs + [sds(st.shape, jnp.float32)],
            compiler_params=pltpu.CompilerParams(dimension_semantics=("arbitrary",), **params),
            name="ssm_project_step",
        )(x2d, st, *consts)
        return list(res[:5]) + [jnp.transpose(res[5], (1, 0, 2))]
    tm = SSM_ROW_TILE
    nt = t_ // tm
    full = lambda a: pl.BlockSpec(a.shape, lambda b, i: (0,) * a.ndim)
    rows = lambda w: pl.BlockSpec((tm, w), lambda b, i: (b * nt + i, 0))
    halo = pl.BlockSpec((SUBLANES, D_MODEL), lambda b, i: (jnp.maximum((b * nt + i) * (tm // SUBLANES) - 1, 0), 0))
    cs8 = jnp.pad(conv_state, ((0, 0), (SUBLANES - (SSM_CONV - 1), 0), (0, 0)))
    tail = pl.BlockSpec((1, SUBLANES, SSM_CONV_DIM), lambda b, i: (b, 0, 0))
    res = pl.pallas_call(
        _ssm_proj_seq_kernel,
        grid=(b_, nt),
        in_specs=[rows(D_MODEL), halo, tail] + [full(a) for a in consts],
        out_specs=[rows(w) for w in widths] + [tail],
        out_shape=outs + [sds((b_, SUBLANES, SSM_CONV_DIM), jnp.float32)],
        compiler_params=pltpu.CompilerParams(dimension_semantics=("parallel", "arbitrary"), **params),
        name="ssm_project_seq",
    )(x2d, x2d, cs8, *consts)
    return list(res[:5]) + [res[5][:, SUBLANES - (SSM_CONV - 1):, :]]


def _ssm_gate_norm(y, xs, z, dskip, normg):
    yg = (y + xs * dskip) * (z * jax.nn.sigmoid(z))
    gw = SSM_D_INNER // SSM_GROUPS
    outs = []
    for g in range(SSM_GROUPS):
        part = yg[:, g * gw:(g + 1) * gw]
        ms = jnp.mean(part * part, axis=-1, keepdims=True)
        outs.append(part * lax.rsqrt(ms + LN_EPS))
    return jnp.concatenate(outs, axis=1) * normg


def _ssm_chunk_kernel(xs_ref, bm_ref, cm_ref, dt_ref, z_ref, aneg_ref, dskip_ref, normg_ref,
                      yg_ref, hT_out_ref, hT_ref, y_ref):
    c = pl.program_id(1)
    l = xs_ref.shape[0]

    @pl.when(c == 0)
    def _():
        hT_ref[...] = jnp.zeros_like(hT_ref)

    dt = dt_ref[...]
    a = dt * aneg_ref[...]
    r_i = lax.broadcasted_iota(jnp.int32, (l, l), 0)
    c_i = lax.broadcasted_iota(jnp.int32, (l, l), 1)
    causal = r_i >= c_i
    tril = jnp.where(causal, 1.0, 0.0)
    hi = lax.Precision.HIGHEST
    acum = jnp.dot(tril, a, precision=hi, preferred_element_type=jnp.float32)
    acum_t = jnp.dot(a.T, tril.T, precision=hi, preferred_element_type=jnp.float32)
    a_last = acum[l - 1:l, :]
    to_end = jnp.exp(a_last - acum)
    from_start = jnp.exp(acum)
    chunk_decay = jnp.exp(a_last)
    xs = xs_ref[...]
    for g in range(SSM_GROUPS):
        bm = bm_ref[:, g * SSM_STATE:(g + 1) * SSM_STATE]
        cm = cm_ref[:, g * SSM_STATE:(g + 1) * SSM_STATE]
        cb = lax.dot_general(cm, bm, (((1,), (1,)), ((), ())), preferred_element_type=jnp.float32)
        bm_t = bm.astype(jnp.float32).T.astype(MXU_DTYPE)
        for e in range(SSM_HPG):
            h = g * SSM_HPG + e
            seg = jnp.exp(jnp.where(causal, acum[:, h:h + 1] - acum_t[h:h + 1, :], -jnp.inf))
            xdt = xs[:, h * SSM_HEAD_DIM:(h + 1) * SSM_HEAD_DIM] * dt[:, h:h + 1]
            y_diag = jnp.dot((cb * seg).astype(MXU_DTYPE), xdt.astype(MXU_DTYPE),
                             preferred_element_type=jnp.float32)
            h_in = hT_ref[h]
            y_off = jnp.dot(cm, h_in.astype(MXU_DTYPE), preferred_element_type=jnp.float32) * from_start[:, h:h + 1]
            y_ref[:, h * SSM_HEAD_DIM:(h + 1) * SSM_HEAD_DIM] = y_diag + y_off
            st = jnp.dot(bm_t, (xdt * to_end[:, h:h + 1]).astype(MXU_DTYPE), preferred_element_type=jnp.float32)
            hT_ref[h] = h_in * chunk_decay[:, h:h + 1] + st
    yg_ref[...] = _ssm_gate_norm(y_ref[...], xs, z_ref[...], dskip_ref[...], normg_ref[...]).astype(yg_ref.dtype)

    @pl.when(c == pl.num_programs(1) - 1)
    def _():
        hT_out_ref[0] = hT_ref[...]


def _ssm_head_lanes(p):
    return jnp.pad(p.astype(jnp.float32), (0, SSM_DT_LANES - SSM_HEADS)).reshape(1, SSM_DT_LANES)


def _ssm_chunk_scan(b_, t_, xs, bm, cm, dt, z, a_log, d_skip, norm_g):
    l = SSM_CHUNK
    nc = t_ // l
    rows = lambda w: pl.BlockSpec((l, w), lambda b, c: (b * nc + c, 0))
    vec = lambda w: pl.BlockSpec((1, w), lambda b, c: (0, 0))
    aneg = _ssm_head_lanes(-jnp.exp(a_log.astype(jnp.float32)))
    dskip = jnp.repeat(d_skip, SSM_HEAD_DIM).reshape(1, SSM_D_INNER)
    yg, hT = pl.pallas_call(
        _ssm_chunk_kernel,
        grid=(b_, nc),
        in_specs=[rows(SSM_D_INNER), rows(SSM_BC_DIM), rows(SSM_BC_DIM), rows(SSM_DT_LANES), rows(SSM_D_INNER),
                  vec(SSM_DT_LANES), vec(SSM_D_INNER), vec(SSM_D_INNER)],
        out_specs=[rows(SSM_D_INNER),
                   pl.BlockSpec((1, SSM_HEADS, SSM_STATE, SSM_HEAD_DIM), lambda b, c: (b, 0, 0, 0))],
        out_shape=[jax.ShapeDtypeStruct((b_ * t_, SSM_D_INNER), MXU_DTYPE),
                   jax.ShapeDtypeStruct((b_, SSM_HEADS, SSM_STATE, SSM_HEAD_DIM), jnp.float32)],
        scratch_shapes=[pltpu.VMEM((SSM_HEADS, SSM_STATE, SSM_HEAD_DIM), jnp.float32),
                        pltpu.VMEM((l, SSM_D_INNER), jnp.float32)],
        compiler_params=pltpu.CompilerParams(
            dimension_semantics=("parallel", "arbitrary"),
            vmem_limit_bytes=V7X_VMEM_LIMIT_BYTES),
        name="ssm_chunk_scan",
    )(xs, bm, cm, dt, z, aneg, dskip, norm_g.reshape(1, SSM_D_INNER))
    return yg, jnp.transpose(hT, (0, 1, 3, 2))


def _ssm_step_kernel(h0_ref, xs_ref, dt_ref, an_ref, bm_ref, cm_ref, y_ref, h_ref):
    h0 = h0_ref[0]
    dt = dt_ref[0]
    decay = jnp.exp(dt * an_ref[...])
    xdt = xs_ref[0] * dt
    bm = bm_ref[0].astype(jnp.float32)
    cm = cm_ref[0].astype(jnp.float32)
    h_ref[0] = h0 * decay + xdt * bm
    cb = jnp.sum(cm * bm, axis=-1, keepdims=True)
    y_ref[0] = cb * xdt + jnp.sum(cm * h0, axis=-1, keepdims=True) * decay


def _ssm_step(state, xs, bm, cm, dt, a_log):
    b_ = state.shape[0]
    per_head = lambda u: jnp.repeat(u.reshape(b_, SSM_GROUPS, 1, SSM_STATE), SSM_HPG, axis=1)
    xs4 = xs.reshape(b_, SSM_HEADS, SSM_HEAD_DIM, 1)
    dt4 = dt[:, :SSM_HEADS].reshape(b_, SSM_HEADS, 1, 1)
    an = (-jnp.exp(a_log.astype(jnp.float32))).reshape(SSM_HEADS, 1, 1)
    blk = lambda a: pl.BlockSpec((1,) + a.shape[1:], lambda b: (b, 0, 0, 0))
    args = [state.astype(jnp.float32), xs4, dt4, an, per_head(bm), per_head(cm)]
    y4, h_new = pl.pallas_call(
        _ssm_step_kernel,
        grid=(b_,),
        in_specs=[blk(args[0]), blk(xs4), blk(dt4), pl.BlockSpec(an.shape, lambda b: (0, 0, 0)),
                  blk(args[4]), blk(args[5])],
        out_specs=[blk(xs4), blk(args[0])],
        out_shape=[jax.ShapeDtypeStruct(xs4.shape, jnp.float32), jax.ShapeDtypeStruct(state.shape, jnp.float32)],
        compiler_params=pltpu.CompilerParams(
            dimension_semantics=("parallel",),
            vmem_limit_bytes=V7X_VMEM_LIMIT_BYTES),
        name="ssm_step",
    )(*args)
    return y4.reshape(b_, SSM_D_INNER), h_new


def _ssm_gate_norm_kernel(y_ref, xs_ref, z_ref, dskip_ref, normg_ref, o_ref):
    o_ref[...] = _ssm_gate_norm(y_ref[...], xs_ref[...], z_ref[...], dskip_ref[...], normg_ref[...]).astype(o_ref.dtype)


def _ssm_gate_norm_rows(y, xs, z, d_skip, norm_g):
    full = lambda a: pl.BlockSpec(a.shape, lambda i: (0,) * a.ndim)
    args = [y, xs, z, jnp.repeat(d_skip, SSM_HEAD_DIM).reshape(1, SSM_D_INNER), norm_g.reshape(1, SSM_D_INNER)]
    return pl.pallas_call(
        _ssm_gate_norm_kernel,
        grid=(1,),
        in_specs=[full(a) for a in args],
        out_specs=full(y),
        out_shape=jax.ShapeDtypeStruct(y.shape, MXU_DTYPE),
        name="ssm_gate_norm",
    )(*args)


def _mamba2_mixer_pallas(x3d, conv_state, ssm_state, w_in, conv_w, conv_b, dt_bias, a_log, d_skip, norm_g):
    b_, t_, _ = x3d.shape
    z, xs, bm, cm, dt, conv_new = _ssm_project(x3d, conv_state, w_in, conv_w, conv_b, dt_bias)
    if t_ == 1:
        y, h_new = _ssm_step(ssm_state, xs, bm, cm, dt, a_log)
        yg = _ssm_gate_norm_rows(y, xs, z, d_skip, norm_g)
    else:
        yg, h_new = _ssm_chunk_scan(b_, t_, xs, bm, cm, dt, z, a_log, d_skip, norm_g)
    return yg, conv_new, h_new.astype(ssm_state.dtype)


PAGES_PER_STEP = 8


def _sortable_key(score):
    bits = pltpu.bitcast(score, jnp.int32)
    return jnp.where(bits >= 0, bits, bits ^ jnp.int32(0x7FFFFFFF))


def _dsa_decode_kernel(pt_ref, iq_ref, iw_ref, q_ref, ikn_ref, kn_ref, vn_ref, *rest,
                       topk, col_bits, n_steps, pages):
    idx_refs, k_refs, v_refs = rest[:pages], rest[pages:2 * pages], rest[2 * pages:3 * pages]
    o_ref, key_ref, sel_ref, m_ref, l_ref, acc_ref = rest[3 * pages:]
    s = pl.program_id(1)
    nk = key_ref.shape[2]
    nt = (((1,), (1,)), ((), ()))
    iq = iq_ref[0]
    iw = iw_ref[0]
    lane = lax.broadcasted_iota(jnp.int32, (1, nk), 1)

    def row_dot(a, row):
        return jnp.sum(a.astype(jnp.float32) * row.astype(jnp.float32), axis=1, keepdims=True)

    def index_score(ik):
        if ik.shape[0] == 1:
            sc = row_dot(iq, ik)
        else:
            sc = lax.dot_general(iq, ik, nt, preferred_element_type=jnp.float32)
        return jnp.sum(iw * jnp.maximum(sc, 0.0), axis=0, keepdims=True)

    def fold(hit):
        out = hit[:, 0:128]
        for l in range(1, nk // 128):
            out = out + hit[:, l * 128:(l + 1) * 128]
        return out

    @pl.when(s < n_steps)
    def _score():
        ik = jnp.concatenate([r[0] for r in idx_refs], axis=0).astype(MXU_DTYPE)
        key_ref[s] = _sortable_key(index_score(ik))

    @pl.when(s == n_steps - 1)
    def _select():
        key_new = _sortable_key(index_score(ikn_ref[0]))

        def count(pred_past, pred_new):
            cnt = jnp.zeros((1, 128), jnp.float32)
            for st in range(n_steps):
                cnt = cnt + fold(jnp.where(pred_past(key_ref[st], lane + st * nk), 1.0, 0.0))
            return jnp.sum(cnt, axis=1, keepdims=True) + jnp.where(pred_new(key_new), 1.0, 0.0)

        def bit_body(i, thr):
            cand = thr ^ lax.shift_left(jnp.int32(1), jnp.int32(31) - i)
            total = count(lambda k, c: k >= cand, lambda k: k >= cand)
            return jnp.where(total >= float(topk), cand, thr)

        thr = lax.fori_loop(0, 32, bit_body, jnp.full((1, 1), INT32_MIN, jnp.int32))
        need = float(topk) - count(lambda k, c: k > thr, lambda k: k > thr)

        def col_body(i, last):
            cand = last | lax.shift_left(jnp.int32(1), jnp.int32(col_bits - 1) - i)
            ties = count(lambda k, c: (k == thr) & (c < cand),
                         lambda k: (k == thr) & (jnp.int32(n_steps * nk) < cand))
            return jnp.where(ties < need, cand, last)

        last_tie = lax.fori_loop(0, col_bits, col_body, jnp.zeros((1, 1), jnp.int32))
        sel_ref[0] = jnp.broadcast_to(thr, sel_ref.shape[1:])
        sel_ref[1] = jnp.broadcast_to(last_tie, sel_ref.shape[1:])
        sel_ref[2] = jnp.broadcast_to(key_new, sel_ref.shape[1:])

    gsz = ATT_HEADS // ATT_KV_HEADS
    q = q_ref[0]
    q_wide = jnp.concatenate([q] * ATT_KV_HEADS, axis=1)
    head_i = lax.broadcasted_iota(jnp.int32, q_wide.shape, 0)
    col_i = lax.broadcasted_iota(jnp.int32, q_wide.shape, 1)
    own_group = (col_i // ATT_HEAD_DIM) == (head_i // gsz)
    q_blk = jnp.where(own_group, q_wide, jnp.zeros_like(q_wide))

    def keep_mask(key, col):
        thr, last_tie = sel_ref[0, 0:1, 0:1], sel_ref[1, 0:1, 0:1]
        return (key > thr) | ((key == thr) & (col <= last_tie))

    def online_update(logits, weighted_values):
        m_old = m_ref[...]
        m_new = jnp.maximum(m_old, jnp.max(logits, axis=1, keepdims=True))
        p = jnp.exp(logits - m_new)
        alpha = jnp.exp(m_old - m_new)
        l_ref[...] = alpha * l_ref[...] + jnp.sum(p, axis=1, keepdims=True)
        acc_ref[...] = alpha * acc_ref[...] + weighted_values(p.astype(MXU_DTYPE))
        m_ref[...] = m_new

    @pl.when(s == n_steps)
    def _init():
        m_ref[...] = jnp.full(m_ref.shape, MASK_NEG, jnp.float32)
        l_ref[...] = jnp.zeros(l_ref.shape, jnp.float32)
        acc_ref[...] = jnp.zeros(acc_ref.shape, jnp.float32)

    @pl.when(s >= n_steps)
    def _attend():
        st = s - n_steps
        bias = jnp.where(keep_mask(key_ref[st], lane + st * nk), 0.0, MASK_NEG)
        kk = jnp.concatenate([r[0] for r in k_refs], axis=0).astype(MXU_DTYPE)
        vv = jnp.concatenate([r[0] for r in v_refs], axis=0).astype(MXU_DTYPE)
        online_update(lax.dot_general(q_blk, kk, nt, preferred_element_type=jnp.float32) + bias,
                      lambda p: jnp.dot(p, vv, preferred_element_type=jnp.float32))

    @pl.when(s == 2 * n_steps - 1)
    def _finish():
        keep_new = keep_mask(sel_ref[2, 0:1, 0:1], jnp.int32(n_steps * nk))
        logit = row_dot(q_blk, kn_ref[0])
        v_row = vn_ref[0].astype(jnp.float32)
        online_update(logit + jnp.where(keep_new, 0.0, MASK_NEG), lambda p: p.astype(jnp.float32) * v_row)
        out = jnp.where(own_group, acc_ref[...] / l_ref[...], 0.0)
        o = out[:, 0:ATT_HEAD_DIM]
        for g in range(1, ATT_KV_HEADS):
            o = o + out[:, g * ATT_HEAD_DIM:(g + 1) * ATT_HEAD_DIM]
        o_ref[0] = o.astype(o_ref.dtype)


def _dsa_decode(q, iq, iw, ik_new, k_new, v_new, cache_k, cache_v, cache_idx_k, page_table):
    b_, n_pages = page_table.shape
    n_pool, page = cache_k.shape[0], cache_k.shape[1]
    pages = PAGES_PER_STEP
    n_steps = n_pages // pages
    past = n_pages * page
    ck = cache_k.reshape(n_pool, page, ATT_KV_DIM)
    cv = cache_v.reshape(n_pool, page, ATT_KV_DIM)
    per_seq = lambda a: pl.BlockSpec((1,) + a.shape[1:], lambda b, s, pt: (b,) + (0,) * (a.ndim - 1))

    def paged(width, j, attend_phase):
        def index(b, s, pt):
            grp = jnp.maximum(s - n_steps, 0) if attend_phase else jnp.minimum(s, n_steps - 1)
            return (pt[b, grp * pages + j], 0, 0)
        return pl.BlockSpec((1, page, width), index)

    small = [iq.reshape(b_, IDX_HEADS, IDX_DIM), iw[:, :IDX_HEADS].reshape(b_, IDX_HEADS, 1),
             q.reshape(b_, ATT_HEADS, ATT_HEAD_DIM), ik_new.astype(MXU_DTYPE).reshape(b_, 1, IDX_DIM),
             k_new.astype(MXU_DTYPE).reshape(b_, 1, ATT_KV_DIM), v_new.astype(MXU_DTYPE).reshape(b_, 1, ATT_KV_DIM)]
    grid_spec = pltpu.PrefetchScalarGridSpec(
        num_scalar_prefetch=1,
        grid=(b_, 2 * n_steps),
        in_specs=[per_seq(a) for a in small]
        + [paged(IDX_DIM, j, False) for j in range(pages)]
        + [paged(ATT_KV_DIM, j, True) for j in range(pages)]
        + [paged(ATT_KV_DIM, j, True) for j in range(pages)],
        out_specs=pl.BlockSpec((1, ATT_HEADS, ATT_HEAD_DIM), lambda b, s, pt: (b, 0, 0)),
        scratch_shapes=[pltpu.VMEM((n_steps, 1, pages * page), jnp.int32),
                        pltpu.VMEM((3, SUBLANES, 128), jnp.int32),
                        pltpu.VMEM((ATT_HEADS, 1), jnp.float32),
                        pltpu.VMEM((ATT_HEADS, 1), jnp.float32),
                        pltpu.VMEM((ATT_HEADS, ATT_KV_DIM), jnp.float32)])
    o = pl.pallas_call(
        functools.partial(_dsa_decode_kernel, topk=min(TOPK_MAX, (past + 1) // 4),
                          col_bits=max(1, past.bit_length()), n_steps=n_steps, pages=pages),
        grid_spec=grid_spec,
        out_shape=jax.ShapeDtypeStruct((b_, ATT_HEADS, ATT_HEAD_DIM), MXU_DTYPE),
        compiler_params=pltpu.CompilerParams(
            dimension_semantics=("parallel", "arbitrary"),
            vmem_limit_bytes=V7X_VMEM_LIMIT_BYTES),
        name="dsa_decode",
    )(page_table, *small, *([cache_idx_k] * pages), *([ck] * pages), *([cv] * pages))
    return o.reshape(b_, ATT_Q_DIM)


def _dsa_sample_pallas(x3d, cache_k, cache_v, cache_idx_k, page_table, w_in, kn_g, kn_b):
    b_, t_, _ = x3d.shape
    past = page_table.shape[1] * cache_k.shape[1]
    pos = jnp.full((b_,), past, jnp.int32)
    q, iq, v, _, iw, kT, _, ikT, _ = _dsa_project(x3d.reshape(1, b_, D_MODEL), pos, w_in, kn_g, kn_b)
    k = _untranspose_groups(kT)[0]
    ik = _untranspose_groups(ikT)[0]
    o = _dsa_decode(q, iq, iw, ik, k, v, cache_k, cache_v, cache_idx_k, page_table)
    kv4 = lambda u: u.reshape(b_, t_, ATT_KV_HEADS, ATT_HEAD_DIM)
    return o, kv4(k), kv4(v), ik.reshape(b_, t_, IDX_DIM)


def _layer_norm(x, g, b):
    xf = x.astype(jnp.float32)
    mu = jnp.mean(xf, -1, keepdims=True)
    var = jnp.mean(jnp.square(xf - mu), -1, keepdims=True)
    return ((xf - mu) * lax.rsqrt(var + LN_EPS)).astype(x.dtype) * g + b


def _rope_partial(x, pos, rot_dim):
    half = rot_dim // 2
    inv = ROPE_THETA ** (-jnp.arange(half, dtype=jnp.float32) / half)
    ang = pos.astype(jnp.float32)[:, None] * inv[None, :]
    cos = jnp.cos(ang)[:, None, :]
    sin = jnp.sin(ang)[:, None, :]
    xf = x[..., :rot_dim].astype(jnp.float32)
    x1, x2 = xf[..., :half], xf[..., half:]
    rot = jnp.concatenate([x1 * cos - x2 * sin, x2 * cos + x1 * sin], axis=-1).astype(x.dtype)
    return jnp.concatenate([rot, x[..., rot_dim:]], axis=-1)


def _gather_rows(rows, idx):
    return jax.vmap(lambda r, i: r[i])(rows, idx)


def _gmlp_mixer(x, w_in, ln_g, ln_b, ws, bs):
    B_, T, _ = x.shape
    u, v = jnp.split(jax.nn.gelu(x @ w_in), 2, axis=-1)
    v = _layer_norm(v, ln_g, ln_b)
    l = min(T, CHUNK)
    c = T // l
    mask = jnp.tril(jnp.ones((l, l), dtype=bool))
    w = jnp.where(mask, ws[:, :l, :l], 0.0)
    vc = v.reshape(B_, c, l, GM_GROUPS, GM_GROUP_DIM)
    mixed = jnp.einsum('gts,bcsgd->bctgd', w, vc) + jnp.transpose(bs[:, :l])[:, :, None]
    return u * mixed.reshape(B_, T, GM_WIDTH), v


def _ssd_chunked(xs, dt, a, bm, cm, h0):
    B_, T = xs.shape[:2]
    l = min(T, SSM_CHUNK)
    c = T // l
    blk = lambda t: t.reshape((B_, c, l) + t.shape[2:])
    xdt = blk(xs.astype(jnp.float32) * dt[..., None])
    bc, cc, acum = blk(bm), blk(cm), jnp.cumsum(blk(a), axis=2)
    at = jnp.moveaxis(acum, 2, -1)
    causal = jnp.tril(jnp.ones((l, l), dtype=bool))
    seg = jnp.exp(jnp.where(causal, at[..., :, None] - at[..., None, :], -jnp.inf))
    cb = jnp.einsum('bctgn,bcsgn->bcgts', cc, bc)
    y_diag = jnp.einsum('bcgts,bcgets,bcsgep->bctgep', cb, seg, xdt)
    states = jnp.einsum('bclgn,bclge,bclgep->bcgepn', bc, jnp.exp(acum[:, :, -1:] - acum), xdt)
    chunk_decay = jnp.exp(acum[:, :, -1])

    def step(h, inp):
        dec, st = inp
        return h * dec[..., None, None] + st, h

    h_last, h_in = lax.scan(step, h0, (jnp.moveaxis(chunk_decay, 1, 0), jnp.moveaxis(states, 1, 0)))
    y_off = jnp.einsum('bctgn,bcgepn,bctge->bctgep', cc, jnp.moveaxis(h_in, 0, 1), jnp.exp(acum))
    return (y_diag + y_off).reshape(B_, T, SSM_GROUPS, SSM_HPG, SSM_HEAD_DIM), h_last


def _mamba2_mixer(x, conv_state, ssm_state, w_in, conv_w, conv_b, dt_bias, a_log, d_skip, norm_g):
    B_, T, _ = x.shape
    z, xbc, dt = jnp.split(x @ w_in, [SSM_D_INNER, SSM_D_INNER + SSM_CONV_DIM], axis=-1)
    xbc_ext = jnp.concatenate([conv_state, xbc], axis=1)
    conv = conv_b
    for j in range(SSM_CONV):
        conv = conv + xbc_ext[:, j:j + T] * conv_w[j]
    xbc = jax.nn.silu(conv)
    xs, bm, cm = jnp.split(xbc, [SSM_D_INNER, SSM_D_INNER + SSM_GROUPS * SSM_STATE], axis=-1)
    xs = xs.reshape(B_, T, SSM_GROUPS, SSM_HPG, SSM_HEAD_DIM)
    bm = bm.reshape(B_, T, SSM_GROUPS, SSM_STATE)
    cm = cm.reshape(B_, T, SSM_GROUPS, SSM_STATE)
    dt = jax.nn.softplus((dt + dt_bias).astype(jnp.float32)).reshape(B_, T, SSM_GROUPS, SSM_HPG)
    a_neg = -jnp.exp(a_log.astype(jnp.float32)).reshape(SSM_GROUPS, SSM_HPG)
    h0 = ssm_state.astype(jnp.float32).reshape(B_, SSM_GROUPS, SSM_HPG, SSM_HEAD_DIM, SSM_STATE)
    y, h_last = _ssd_chunked(xs, dt, dt * a_neg, bm, cm, h0)
    y = y.astype(x.dtype) + xs * d_skip.reshape(SSM_GROUPS, SSM_HPG, 1)
    yg = (y.reshape(B_, T, SSM_D_INNER) * jax.nn.silu(z)).reshape(B_, T, SSM_GROUPS, -1).astype(jnp.float32)
    yg = (yg * lax.rsqrt(jnp.mean(jnp.square(yg), -1, keepdims=True) + LN_EPS)).astype(x.dtype)
    yg = yg.reshape(B_, T, SSM_D_INNER) * norm_g
    new_ssm = h_last.reshape(B_, SSM_HEADS, SSM_HEAD_DIM, SSM_STATE).astype(ssm_state.dtype)
    return yg, xbc_ext[:, T:], new_ssm


def _dsa_project_jax(x, pos, w_in, kn_g, kn_b):
    B_, T, _ = x.shape
    q, k, v, iq, ik, iw = jnp.split(x @ w_in, list(ATT_IN_SPLITS), axis=-1)
    q = _rope_partial(q.reshape(B_, T, ATT_HEADS, ATT_HEAD_DIM), pos, ROPE_DIM)
    k = _rope_partial(k.reshape(B_, T, ATT_KV_HEADS, ATT_HEAD_DIM), pos, ROPE_DIM)
    v = v.reshape(B_, T, ATT_KV_HEADS, ATT_HEAD_DIM)
    iq = _rope_partial(iq.reshape(B_, T, IDX_HEADS, IDX_DIM), pos, IDX_ROPE_DIM)
    ik = _rope_partial(_layer_norm(ik, kn_g, kn_b)[:, :, None, :], pos, IDX_ROPE_DIM)[:, :, 0, :]
    iw = iw * (IDX_HEADS ** -0.5 * IDX_DIM ** -0.5)
    return q, k, v, iq, ik, iw


def _dsa_select(iq, iw, ik, qpos, topk):
    s = jnp.einsum('bqhd,bsd->bqhs', iq, ik)
    score = jnp.einsum('bqh,bqhs->bqs', iw, jax.nn.relu(s)).astype(jnp.float32)
    adm = jnp.arange(ik.shape[1])[None, :] <= qpos[:, None]
    score = jnp.where(adm[None], score, -jnp.inf)
    _, idx = lax.top_k(score, topk)
    return idx, idx <= qpos[None, :, None]


def _sparse_attend(q, k_sel, v_sel, valid):
    B_, Q = q.shape[:2]
    qg = q.reshape(B_, Q, ATT_KV_HEADS, ATT_HEADS // ATT_KV_HEADS, ATT_HEAD_DIM)
    s = jnp.einsum('bqhgd,bqkhd->bqhgk', qg, k_sel).astype(jnp.float32) * (ATT_HEAD_DIM ** -0.5)
    s = jnp.where(valid[:, :, None, None, :], s, -jnp.inf)
    p = jax.nn.softmax(s, axis=-1).astype(v_sel.dtype)
    o = jnp.einsum('bqhgk,bqkhd->bqhgd', p, v_sel)
    return o.reshape(B_, Q, ATT_Q_DIM)


def _dsa_prompt(x, w_in, kn_g, kn_b):
    B_, T, _ = x.shape
    q, k, v, iq, ik, iw = _dsa_project_jax(x, jnp.arange(T), w_in, kn_g, kn_b)
    topk = min(TOPK_MAX, T // 4)

    def block(bi):
        t0 = bi * Q_BLOCK
        sl = lambda t: lax.dynamic_slice_in_dim(t, t0, Q_BLOCK, axis=1)
        qpos = t0 + jnp.arange(Q_BLOCK)
        idx, valid = _dsa_select(sl(iq), sl(iw), ik, qpos, topk)
        return _sparse_attend(sl(q), _gather_rows(k, idx), _gather_rows(v, idx), valid)

    o = lax.map(block, jnp.arange(T // Q_BLOCK))
    o = jnp.moveaxis(o, 0, 1).reshape(B_, T, ATT_Q_DIM)
    return o, k, v, ik


def _dsa_sample(x, cache_k, cache_v, cache_idx_k, page_table, w_in, kn_g, kn_b):
    B_, T, _ = x.shape
    page = cache_k.shape[1]
    past = page_table.shape[1] * page
    pos = past + jnp.arange(T)
    q, k, v, iq, ik, iw = _dsa_project_jax(x, pos, w_in, kn_g, kn_b)
    ik_all = jnp.concatenate([cache_idx_k[page_table].reshape(B_, past, IDX_DIM), ik], axis=1)
    idx, valid = _dsa_select(iq, iw, ik_all, pos, min(TOPK_MAX, (past + T) // 4))
    past_idx = jnp.minimum(idx, past - 1)
    phys = jnp.take_along_axis(page_table, (past_idx // page).reshape(B_, -1), axis=1).reshape(idx.shape)
    off = past_idx % page
    new_idx = jnp.clip(idx - past, 0, T - 1)
    is_new = (idx >= past)[..., None, None]
    k_sel = jnp.where(is_new, _gather_rows(k, new_idx), cache_k[phys, off])
    v_sel = jnp.where(is_new, _gather_rows(v, new_idx), cache_v[phys, off])
    o = _sparse_attend(q, k_sel, v_sel, valid)
    return o, k, v, ik


def _rwkv7_mixer(x, shift, wkv, mu, w_r, w_k, w_v, w0, w1, w2, a0, a1, a2, g1, g2,
                 k_k, k_a, r_k, gn_g, gn_b):
    B_, T, _ = x.shape
    x_prev = jnp.concatenate([shift[:, None, :], x[:, :-1]], axis=1)
    xm = x[None] + (x_prev - x)[None] * mu[:, None, None, :]
    xr, xw, xk, xv, xa, xg = xm
    r = xr @ w_r
    w_log = -jax.nn.softplus(-(w0 + jnp.tanh(xw @ w1) @ w2)) - 0.5
    k = xk @ w_k
    v = xv @ w_v
    a = jax.nn.sigmoid(a0 + (xa @ a1) @ a2)
    g = jax.nn.sigmoid(xg @ g1) @ g2
    heads = lambda t: t.reshape(B_, T, RW_HEADS, RW_HEAD)
    kk = heads(k * k_k).astype(jnp.float32)
    kk = kk / jnp.maximum(jnp.sqrt(jnp.sum(kk * kk, -1, keepdims=True)), 1e-12)
    k = k * (1.0 + (a - 1.0) * k_a)
    decay = jnp.exp(-jnp.exp(w_log.astype(jnp.float32)))
    r, k, v, a, decay = heads(r), heads(k), heads(v), heads(a), heads(decay)
    seq = tuple(jnp.moveaxis(t.astype(jnp.float32), 1, 0) for t in (r, decay, k, v, kk, kk * a))

    def step(s, inp):
        r_t, d_t, k_t, v_t, kk_t, b_t = inp
        sa = jnp.einsum('bhij,bhj->bhi', s, kk_t)
        s = s * d_t[:, :, None, :] - sa[..., None] * b_t[:, :, None, :] + v_t[..., None] * k_t[:, :, None, :]
        return s, jnp.einsum('bhij,bhj->bhi', s, r_t)

    s_last, y = lax.scan(step, wkv.astype(jnp.float32), seq)
    y = jnp.moveaxis(y, 0, 1)
    mu_y = jnp.mean(y, -1, keepdims=True)
    var_y = jnp.mean(jnp.square(y - mu_y), -1, keepdims=True)
    yn = ((y - mu_y) * lax.rsqrt(var_y + RW_GN_EPS)).reshape(B_, T, D_MODEL).astype(x.dtype) * gn_g + gn_b
    bonus = (jnp.sum(r * k * r_k, -1, keepdims=True) * v).reshape(B_, T, D_MODEL)
    return (yn + bonus) * g, x[:, -1], s_last.astype(wkv.dtype)


def kernel(x_prompt, x_sample, state_ssm_conv, state_ssm, cache_k, cache_v, cache_idx_k, state_rwkv_shift, state_rwkv_wkv, page_table, p_prompt, p_sample, ln_g, ln_b, ffn_w_up, ffn_w_down, ple_w_p, ple_w_g, ple_b_g, gm_w_in, gm_ln_g, gm_ln_b, gm_ws, gm_bs, gm_w_out, ssm_w_in, ssm_conv_w, ssm_conv_b, ssm_dt_bias, ssm_a_log, ssm_d, ssm_norm_g, ssm_w_out, att_w_in, att_kn_g, att_kn_b, att_w_out, rw_mu, rw_w_r, rw_w_k, rw_w_v, rw_w_o, rw_w0, rw_w1, rw_w2, rw_a0, rw_a1, rw_a2, rw_g1, rw_g2, rw_k_k, rw_k_a, rw_r_k, rw_gn_g, rw_gn_b):
    bp, tp, _ = x_prompt.shape
    bs_, ts, _ = x_sample.shape
    bf = lambda w: w.astype(jnp.bfloat16)
    w_up_bf, w_down_bf = bf(ffn_w_up), bf(ffn_w_down)
    ple_wp_bf, ple_wg_bf = bf(ple_w_p), bf(ple_w_g)

    yp = x_prompt.reshape(bp * tp, D_MODEL)
    ys = x_sample.reshape(bs_ * ts, D_MODEL)
    r3p = lambda t: t.reshape(bp, tp, -1)
    r3s = lambda t: t.reshape(bs_, ts, -1)
    f2 = lambda t: t.reshape(-1, t.shape[-1])

    for i in range(DEPTH):
        yp = _ffn_sub(yp, w_up_bf[i, 0], w_down_bf[i, 0], ln_g[i, 0], ln_b[i, 0])
        ys = _ffn_sub(ys, w_up_bf[i, 0], w_down_bf[i, 0], ln_g[i, 0], ln_b[i, 0])
        m = i % N_MIXERS
        if m == 0:
            gm_args = (gm_w_in, gm_ln_g, gm_ln_b, gm_ws, gm_bs, gm_w_out, ln_g[i, 1], ln_b[i, 1])
            yp, = _gmlp_block(yp, tp, *gm_args, False)
            ys, gm_v_s = _gmlp_block(ys, ts, *gm_args, True)
            gm_v_s = r3s(gm_v_s)
        elif m == 1:
            ssm_args = (ssm_w_in, ssm_conv_w, ssm_conv_b, ssm_dt_bias, ssm_a_log, ssm_d, ssm_norm_g)
            hp, conv_p, ssm_p = _mamba2_mixer_pallas(
                r3p(yp), jnp.zeros((bp, SSM_CONV - 1, SSM_CONV_DIM), yp.dtype),
                jnp.zeros((bp, SSM_HEADS, SSM_HEAD_DIM, SSM_STATE), yp.dtype), *ssm_args)
            hs, conv_s, ssm_s = _mamba2_mixer_pallas(r3s(ys), state_ssm_conv, state_ssm, *ssm_args)
            w_out = bf(ssm_w_out)
        elif m == 2:
            hp, k_p, v_p, ik_p = _dsa_prompt_pallas(r3p(yp), att_w_in, att_kn_g, att_kn_b)
            hs, k_s, v_s, ik_s = _dsa_sample_pallas(r3s(ys), cache_k, cache_v, cache_idx_k, page_table,
                                                    att_w_in, att_kn_g, att_kn_b)
            w_out = bf(att_w_out)
        else:
            rw_args = (rw_mu, rw_w_r, rw_w_k, rw_w_v, rw_w0, rw_w1, rw_w2, rw_a0, rw_a1, rw_a2,
                       rw_g1, rw_g2, rw_k_k, rw_k_a, rw_r_k, rw_gn_g, rw_gn_b)
            hp, gate_p, sh_p, wkv_p = _rwkv7_mixer_pallas(
                r3p(yp), jnp.zeros((bp, D_MODEL), yp.dtype),
                jnp.zeros((bp, RW_HEADS, RW_HEAD, RW_HEAD), yp.dtype), *rw_args)
            hs, gate_s, sh_s, wkv_s = _rwkv7_mixer_pallas(r3s(ys), state_rwkv_shift, state_rwkv_wkv, *rw_args)
            w_out = bf(rw_w_o)
        if m == 3:
            yp = _proj_gate_post_norm(yp, hp, gate_p, w_out, ln_g[i, 1], ln_b[i, 1])
            ys = _proj_gate_post_norm(ys, hs, gate_s, w_out, ln_g[i, 1], ln_b[i, 1])
        elif m != 0:
            yp = _proj_post_norm(yp, f2(hp), w_out, ln_g[i, 1], ln_b[i, 1])
            ys = _proj_post_norm(ys, f2(hs), w_out, ln_g[i, 1], ln_b[i, 1])
        yp = _ffn_sub(yp, w_up_bf[i, 1], w_down_bf[i, 1], ln_g[i, 2], ln_b[i, 2])
        ys = _ffn_sub(ys, w_up_bf[i, 1], w_down_bf[i, 1], ln_g[i, 2], ln_b[i, 2])
        yp = _ple_add(yp, f2(p_prompt[i]), ple_wp_bf[i], ple_wg_bf[i], ple_b_g[i])
        ys = _ple_add(ys, f2(p_sample[i]), ple_wp_bf[i], ple_wg_bf[i], ple_b_g[i])

    return (r3p(yp), r3s(ys), gm_v_s, conv_p, ssm_p, conv_s, ssm_s, k_p, v_p, ik_p, k_s, v_s, ik_s,
            sh_p, wkv_p, sh_s, wkv_s)
```

```python
import functools
import math

import jax
import jax.numpy as jnp
from jax import lax
from jax.experimental import pallas as pl
from jax.experimental.pallas import tpu as pltpu

D_MODEL = 1024
DEPTH = 4
N_MIXERS = 4
PLE_DIM = 256
D_FF = 2816
ALPHA = (2 * DEPTH) ** 0.25
LN_EPS = 1e-5

CHUNK = 128
GM_WIDTH = 2 * D_MODEL
GM_GROUPS = 8
GM_GROUP_DIM = GM_WIDTH // GM_GROUPS

SSM_D_INNER = 2 * D_MODEL
SSM_HEAD_DIM = 64
SSM_HEADS = SSM_D_INNER // SSM_HEAD_DIM
SSM_GROUPS = 4
SSM_HPG = SSM_HEADS // SSM_GROUPS
SSM_STATE = 128
SSM_CONV = 4
SSM_CONV_DIM = SSM_D_INNER + 2 * SSM_GROUPS * SSM_STATE
SSM_CHUNK = 128

ATT_HEADS = 16
ATT_KV_HEADS = 4
ATT_HEAD_DIM = D_MODEL // ATT_HEADS
ROPE_DIM = ATT_HEAD_DIM // 4
ROPE_THETA = 500000.0
IDX_HEADS = 8
IDX_DIM = 64
IDX_ROPE_DIM = IDX_DIM // 4
TOPK_MAX = 256
Q_BLOCK = 128
ATT_Q_DIM = ATT_HEADS * ATT_HEAD_DIM
ATT_KV_DIM = ATT_KV_HEADS * ATT_HEAD_DIM
ATT_IN_SPLITS = (ATT_Q_DIM, ATT_Q_DIM + ATT_KV_DIM, ATT_Q_DIM + 2 * ATT_KV_DIM,
                 ATT_Q_DIM + 2 * ATT_KV_DIM + IDX_HEADS * IDX_DIM,
                 ATT_Q_DIM + 2 * ATT_KV_DIM + IDX_HEADS * IDX_DIM + IDX_DIM)

RW_HEAD = 64
RW_HEADS = D_MODEL // RW_HEAD
RW_GN_EPS = 64e-5

V7X_VMEM_LIMIT_BYTES = 52 * 1024 * 1024
FF_TILE = D_FF // 2
ROW_TILE = 512


def _row_tile(m):
    return ROW_TILE if m % ROW_TILE == 0 else m


def _ln_rows(y, g, b):
    mu = jnp.mean(y, axis=-1, keepdims=True)
    yc = y - mu
    var = jnp.mean(yc * yc, axis=-1, keepdims=True)
    return yc * lax.rsqrt(var + LN_EPS) * g + b


def _ffn_kernel(x_ref, wa_ref, wb_ref, wd_ref, g_ref, b_ref, o_ref, acc_ref):
    f = pl.program_id(1)

    @pl.when(f == 0)
    def _():
        acc_ref[...] = jnp.zeros_like(acc_ref)

    xb = x_ref[...].astype(jnp.bfloat16)
    a = jnp.dot(xb, wa_ref[...], preferred_element_type=jnp.float32)
    b = jnp.dot(xb, wb_ref[...], preferred_element_type=jnp.float32)
    h = (a * jax.nn.sigmoid(a) * b).astype(jnp.bfloat16)
    acc_ref[...] += jnp.dot(h, wd_ref[...], preferred_element_type=jnp.float32)

    @pl.when(f == pl.num_programs(1) - 1)
    def _():
        y = ALPHA * x_ref[...] + 0.5 * acc_ref[...]
        o_ref[...] = _ln_rows(y, g_ref[...], b_ref[...])


def _ffn_sub(x2d, w_up, w_down, g, b):
    m = x2d.shape[0]
    tm = _row_tile(m)
    nf = D_FF // FF_TILE
    return pl.pallas_call(
        _ffn_kernel,
        grid=(m // tm, nf),
        in_specs=[
            pl.BlockSpec((tm, D_MODEL), lambda i, f: (i, 0)),
            pl.BlockSpec((D_MODEL, FF_TILE), lambda i, f: (0, f)),
            pl.BlockSpec((D_MODEL, FF_TILE), lambda i, f: (0, f + nf)),
            pl.BlockSpec((FF_TILE, D_MODEL), lambda i, f: (f, 0)),
            pl.BlockSpec((1, D_MODEL), lambda i, f: (0, 0)),
            pl.BlockSpec((1, D_MODEL), lambda i, f: (0, 0)),
        ],
        out_specs=pl.BlockSpec((tm, D_MODEL), lambda i, f: (i, 0)),
        out_shape=jax.ShapeDtypeStruct((m, D_MODEL), jnp.float32),
        scratch_shapes=[pltpu.VMEM((tm, D_MODEL), jnp.float32)],
        compiler_params=pltpu.CompilerParams(
            dimension_semantics=("parallel", "arbitrary"),
            vmem_limit_bytes=V7X_VMEM_LIMIT_BYTES),
        name="ffn_sub",
    )(x2d, w_up, w_up, w_down, g.reshape(1, D_MODEL), b.reshape(1, D_MODEL))


def _ple_kernel(x_ref, p_ref, wp_ref, wg_ref, bg_ref, o_ref):
    x = x_ref[...]
    gate = jax.nn.sigmoid(
        jnp.dot(x.astype(jnp.bfloat16), wg_ref[...], preferred_element_type=jnp.float32) + bg_ref[...])
    emb = jnp.dot(p_ref[...].astype(jnp.bfloat16), wp_ref[...], preferred_element_type=jnp.float32)
    o_ref[...] = x + gate * emb


def _ple_add(x2d, p2d, w_p, w_g, b_g):
    m = x2d.shape[0]
    tm = _row_tile(m)
    return pl.pallas_call(
        _ple_kernel,
        grid=(m // tm,),
        in_specs=[
            pl.BlockSpec((tm, D_MODEL), lambda i: (i, 0)),
            pl.BlockSpec((tm, PLE_DIM), lambda i: (i, 0)),
            pl.BlockSpec((PLE_DIM, D_MODEL), lambda i: (0, 0)),
            pl.BlockSpec((D_MODEL, D_MODEL), lambda i: (0, 0)),
            pl.BlockSpec((1, D_MODEL), lambda i: (0, 0)),
        ],
        out_specs=pl.BlockSpec((tm, D_MODEL), lambda i: (i, 0)),
        out_shape=jax.ShapeDtypeStruct((m, D_MODEL), jnp.float32),
        compiler_params=pltpu.CompilerParams(
            dimension_semantics=("parallel",),
            vmem_limit_bytes=V7X_VMEM_LIMIT_BYTES),
        name="ple_add",
    )(x2d, p2d, w_p, w_g, b_g.reshape(1, D_MODEL))


def _proj_ln_kernel(x_ref, h_ref, w_ref, g_ref, b_ref, o_ref):
    y = ALPHA * x_ref[...] + jnp.dot(h_ref[...].astype(jnp.bfloat16), w_ref[...],
                                     preferred_element_type=jnp.float32)
    o_ref[...] = _ln_rows(y, g_ref[...], b_ref[...])


def _proj_post_norm(x2d, h2d, w_out, g, b):
    m = x2d.shape[0]
    k = h2d.shape[1]
    tm = _row_tile(m)
    return pl.pallas_call(
        _proj_ln_kernel,
        grid=(m // tm,),
        in_specs=[
            pl.BlockSpec((tm, D_MODEL), lambda i: (i, 0)),
            pl.BlockSpec((tm, k), lambda i: (i, 0)),
            pl.BlockSpec((k, D_MODEL), lambda i: (0, 0)),
            pl.BlockSpec((1, D_MODEL), lambda i: (0, 0)),
            pl.BlockSpec((1, D_MODEL), lambda i: (0, 0)),
        ],
        out_specs=pl.BlockSpec((tm, D_MODEL), lambda i: (i, 0)),
        out_shape=jax.ShapeDtypeStruct((m, D_MODEL), jnp.float32),
        compiler_params=pltpu.CompilerParams(
            dimension_semantics=("parallel",),
            vmem_limit_bytes=V7X_VMEM_LIMIT_BYTES),
        name="proj_post_norm",
    )(x2d, h2d, w_out, g.reshape(1, D_MODEL), b.reshape(1, D_MODEL))


MXU_DTYPE = jnp.bfloat16
KEY_GROUP = 512
INT32_MIN = -2 ** 31
MASK_NEG = -1e30


def _rope_lane_tables(pos, rot_dim, head_dim):
    half = rot_dim // 2
    inv = ROPE_THETA ** (-jnp.arange(half, dtype=jnp.float32) / half)
    ang = pos.astype(jnp.float32)[:, None] * inv[None, :]
    cos, sin = jnp.cos(ang), jnp.sin(ang)
    n = pos.shape[0]
    rest = head_dim - rot_dim
    c = jnp.concatenate([cos, cos, jnp.ones((n, rest), jnp.float32)], axis=1)
    s1 = jnp.concatenate([-sin, jnp.zeros((n, half + rest), jnp.float32)], axis=1)
    s2 = jnp.concatenate([jnp.zeros((n, half), jnp.float32), sin, jnp.zeros((n, rest), jnp.float32)], axis=1)
    reps = 128 // head_dim
    tile = lambda t: jnp.tile(t, (1, reps))
    return tile(c), tile(s1), tile(s2), cos.T, sin.T


def _rope_lanes(t, c, s1, s2, half):
    n = t.shape[1]
    reps = n // 128
    tl = lambda a: jnp.concatenate([a] * reps, axis=1)
    return t * tl(c) + pltpu.roll(t, n - half, 1) * tl(s1) + pltpu.roll(t, half, 1) * tl(s2)


def _rope_rows(t, cT, sT, head_dim, half):
    pieces = []
    for h in range(t.shape[0] // head_dim):
        x1 = t[h * head_dim:h * head_dim + half]
        x2 = t[h * head_dim + half:h * head_dim + 2 * half]
        pieces += [x1 * cT - x2 * sT, x2 * cT + x1 * sT, t[h * head_dim + 2 * half:(h + 1) * head_dim]]
    return jnp.concatenate(pieces, axis=0)


def _dsa_proj_kernel(x_ref, wq_ref, wiq_ref, wv_ref, wvx_ref, wiw_ref, wkT_ref, wikT_ref,
                     c_ref, s1_ref, s2_ref, cT_ref, sT_ref, kng_ref, knb_ref, one_ref,
                     q_ref, iq_ref, v_ref, vx_ref, iw_ref, kT_ref, kTb_ref, ikT_ref, ikTb_ref):
    xb = x_ref[...].astype(MXU_DTYPE)
    c, s1, s2 = c_ref[...], s1_ref[...], s2_ref[...]
    cT, sT = cT_ref[...], sT_ref[...]
    dot = lambda a, b: jnp.dot(a, b, preferred_element_type=jnp.float32)
    dot_t = lambda w, a: lax.dot_general(w, a, (((1,), (1,)), ((), ())), preferred_element_type=jnp.float32)

    q = _rope_lanes(dot(xb, wq_ref[...]), c, s1, s2, ROPE_DIM // 2)
    q_ref[...] = (q * (ATT_HEAD_DIM ** -0.5)).astype(q_ref.dtype)
    iq = _rope_lanes(dot(xb, wiq_ref[...]), c, s1, s2, IDX_ROPE_DIM // 2)
    iq_ref[...] = iq.astype(iq_ref.dtype)
    v_ref[...] = dot(xb, wv_ref[...])
    vx_ref[...] = (dot(xb, wvx_ref[...]) + one_ref[...]).astype(vx_ref.dtype)
    iw_ref[...] = dot(xb, wiw_ref[...]) * (IDX_HEADS ** -0.5 * IDX_DIM ** -0.5)

    kT = _rope_rows(dot_t(wkT_ref[...], xb), cT, sT, ATT_HEAD_DIM, ROPE_DIM // 2)
    kT_ref[0, 0] = kT
    kTb_ref[0, 0] = kT.astype(kTb_ref.dtype)
    ikT = dot_t(wikT_ref[...], xb)
    mu = jnp.mean(ikT, axis=0, keepdims=True)
    ikc = ikT - mu
    var = jnp.mean(ikc * ikc, axis=0, keepdims=True)
    ikT = ikc * lax.rsqrt(var + LN_EPS) * kng_ref[...] + knb_ref[...]
    ikT = _rope_rows(ikT, cT, sT, IDX_DIM, IDX_ROPE_DIM // 2)
    ikT_ref[0, 0] = ikT
    ikTb_ref[0, 0] = ikT.astype(ikTb_ref.dtype)


def _dsa_project(x3d, pos, w_in, kn_g, kn_b):
    b_, t_, _ = x3d.shape
    tk = KEY_GROUP if t_ % KEY_GROUP == 0 else t_
    ng = t_ // tk
    m = b_ * t_
    w_q, w_k, w_v, w_iq, w_ik, w_iw = jnp.split(w_in, list(ATT_IN_SPLITS), axis=1)
    cast = lambda w: w.astype(MXU_DTYPE)
    w_vx = jnp.pad(w_v.reshape(D_MODEL, ATT_KV_HEADS, ATT_HEAD_DIM),
                   ((0, 0), (0, 0), (0, 128 - ATT_HEAD_DIM))).reshape(D_MODEL, ATT_KV_HEADS * 128)
    one_col = jnp.tile((jnp.arange(128) == ATT_HEAD_DIM).astype(jnp.float32), ATT_KV_HEADS)[None, :]
    w_iw_pad = jnp.pad(w_iw, ((0, 0), (0, 128 - IDX_HEADS)))
    c, s1, s2, cT, sT = _rope_lane_tables(pos, ROPE_DIM, ATT_HEAD_DIM)
    full = lambda shape: pl.BlockSpec(shape, lambda b, i: (0,) * len(shape))
    rows = lambda n: pl.BlockSpec((tk, n), lambda b, i: (b * ng + i, 0))
    ptab = lambda n: pl.BlockSpec((tk, n), lambda b, i: (i, 0))
    grp = lambda n: pl.BlockSpec((1, 1, n, tk), lambda b, i: (b, i, 0, 0))
    sds = jax.ShapeDtypeStruct
    return pl.pallas_call(
        _dsa_proj_kernel,
        grid=(b_, ng),
        in_specs=[rows(D_MODEL), full((D_MODEL, ATT_Q_DIM)), full((D_MODEL, IDX_HEADS * IDX_DIM)),
                  full((D_MODEL, ATT_KV_DIM)), full((D_MODEL, ATT_KV_HEADS * 128)), full((D_MODEL, 128)),
                  full((ATT_KV_DIM, D_MODEL)), full((IDX_DIM, D_MODEL)),
                  ptab(128), ptab(128), ptab(128),
                  pl.BlockSpec((ROPE_DIM // 2, tk), lambda b, i: (0, i)),
                  pl.BlockSpec((ROPE_DIM // 2, tk), lambda b, i: (0, i)),
                  full((IDX_DIM, 1)), full((IDX_DIM, 1)), full((1, ATT_KV_HEADS * 128))],
        out_specs=[rows(ATT_Q_DIM), rows(IDX_HEADS * IDX_DIM), rows(ATT_KV_DIM), rows(ATT_KV_HEADS * 128),
                   rows(128), grp(ATT_KV_DIM), grp(ATT_KV_DIM), grp(IDX_DIM), grp(IDX_DIM)],
        out_shape=[sds((m, ATT_Q_DIM), MXU_DTYPE), sds((m, IDX_HEADS * IDX_DIM), MXU_DTYPE),
                   sds((m, ATT_KV_DIM), jnp.float32), sds((m, ATT_KV_HEADS * 128), MXU_DTYPE),
                   sds((m, 128), jnp.float32),
                   sds((b_, ng, ATT_KV_DIM, tk), jnp.float32), sds((b_, ng, ATT_KV_DIM, tk), MXU_DTYPE),
                   sds((b_, ng, IDX_DIM, tk), jnp.float32), sds((b_, ng, IDX_DIM, tk), MXU_DTYPE)],
        compiler_params=pltpu.CompilerParams(
            dimension_semantics=("parallel", "parallel"),
            vmem_limit_bytes=V7X_VMEM_LIMIT_BYTES),
        name="dsa_project",
    )(x3d.reshape(m, D_MODEL), cast(w_q), cast(w_iq), cast(w_v), cast(w_vx), cast(w_iw_pad),
      cast(w_k.T), cast(w_ik.T), c, s1, s2, cT, sT, kn_g.reshape(IDX_DIM, 1), kn_b.reshape(IDX_DIM, 1), one_col)


def _untranspose_groups(tg):
    b_, g_, r_, tk = tg.shape
    return jnp.transpose(tg, (0, 1, 3, 2)).reshape(b_, g_ * tk, r_)


def _dsa_attend_kernel(iq_ref, iw_ref, ikT_ref, q_ref, kT_ref, vx_ref, o_ref, key_ref, m_ref, acc_ref, *,
                       topk, col_bits):
    j = pl.program_id(1)
    tq = iq_ref.shape[0]
    tk = key_ref.shape[2]
    n_groups = (j * tq + tq + tk - 1) // tk
    row = j * tq + lax.broadcasted_iota(jnp.int32, (tq, tk), 0)
    col0 = lax.broadcasted_iota(jnp.int32, (tq, tk), 1)
    dot = lambda a, b: jnp.dot(a, b, preferred_element_type=jnp.float32)

    def score_body(g, carry):
        ikT = ikT_ref[0, g]
        sc = jnp.zeros((tq, tk), jnp.float32)
        for h in range(IDX_HEADS):
            s = dot(iq_ref[:, h * IDX_DIM:(h + 1) * IDX_DIM], ikT)
            sc = sc + iw_ref[:, h:h + 1] * jnp.maximum(s, 0.0)
        bits = pltpu.bitcast(sc, jnp.int32)
        key = jnp.where(bits >= 0, bits, bits ^ jnp.int32(0x7FFFFFFF))
        key_ref[g] = jnp.where(col0 + g * tk <= row, key, jnp.int32(INT32_MIN))
        return carry

    lax.fori_loop(0, n_groups, score_body, 0)

    def bit_body(i, thr):
        cand = thr ^ lax.shift_left(jnp.int32(1), jnp.int32(31) - i)

        def count_body(g, cnt):
            hit = jnp.where(key_ref[g] >= cand, 1.0, 0.0)
            for l in range(tk // 128):
                cnt = cnt + hit[:, l * 128:(l + 1) * 128]
            return cnt

        cnt = lax.fori_loop(0, n_groups, count_body, jnp.zeros((tq, 128), jnp.float32))
        total = jnp.sum(cnt, axis=1, keepdims=True)
        return jnp.where(total >= float(topk), cand, thr)

    thr = lax.fori_loop(0, 32, bit_body, jnp.full((tq, 1), INT32_MIN, jnp.int32))

    def lane_fold(hit, cnt):
        for l in range(tk // 128):
            cnt = cnt + hit[:, l * 128:(l + 1) * 128]
        return cnt

    def above_body(g, cnt):
        return lane_fold(jnp.where(key_ref[g] > thr, 1.0, 0.0), cnt)

    n_above = jnp.sum(lax.fori_loop(0, n_groups, above_body, jnp.zeros((tq, 128), jnp.float32)),
                      axis=1, keepdims=True)
    need = float(topk) - n_above

    def col_body(i, last):
        cand = last | lax.shift_left(jnp.int32(1), jnp.int32(col_bits - 1) - i)

        def tie_body(g, cnt):
            hit = jnp.where((key_ref[g] == thr) & (col0 + g * tk < cand), 1.0, 0.0)
            return lane_fold(hit, cnt)

        ties = jnp.sum(lax.fori_loop(0, n_groups, tie_body, jnp.zeros((tq, 128), jnp.float32)),
                       axis=1, keepdims=True)
        return jnp.where(ties < need, cand, last)

    last_tie = lax.fori_loop(0, col_bits, col_body, jnp.zeros((tq, 1), jnp.int32))

    m_ref[...] = jnp.full(m_ref.shape, MASK_NEG, jnp.float32)
    acc_ref[...] = jnp.zeros(acc_ref.shape, jnp.float32)
    gsz = ATT_HEADS // ATT_KV_HEADS

    def attend_body(g, carry):
        key = key_ref[g]
        col = col0 + g * tk
        keep = (key > thr) | ((key == thr) & (col <= last_tie))
        bias = jnp.where(keep & (col <= row), 0.0, MASK_NEG)
        start = pl.multiple_of(g * tk, tk)
        for h in range(ATT_HEADS):
            kv = h // gsz
            s = dot(q_ref[:, h * ATT_HEAD_DIM:(h + 1) * ATT_HEAD_DIM],
                    kT_ref[0, g, kv * ATT_HEAD_DIM:(kv + 1) * ATT_HEAD_DIM, :]) + bias
            m_old = m_ref[h]
            m_new = jnp.maximum(m_old, jnp.max(s, axis=1, keepdims=True))
            p = jnp.exp(s - m_new).astype(vx_ref.dtype)
            pv = dot(p, vx_ref[0, pl.ds(start, tk), kv * 128:(kv + 1) * 128])
            acc_ref[h] = jnp.exp(m_old - m_new) * acc_ref[h] + pv
            m_ref[h] = m_new
        return carry

    lax.fori_loop(0, n_groups, attend_body, 0)

    for h in range(ATT_HEADS):
        a = acc_ref[h]
        o_ref[:, h * ATT_HEAD_DIM:(h + 1) * ATT_HEAD_DIM] = (
            a[:, :ATT_HEAD_DIM] / a[:, ATT_HEAD_DIM:ATT_HEAD_DIM + 1]).astype(o_ref.dtype)


def _dsa_attend(b_, t_, q, iq, iw, ikTb, kTb, vx):
    ng, tk = kTb.shape[1], kTb.shape[3]
    tq = Q_BLOCK
    nq = t_ // tq
    rows = lambda n: pl.BlockSpec((tq, n), lambda b, j: (b * nq + j, 0))
    return pl.pallas_call(
        functools.partial(_dsa_attend_kernel, topk=min(TOPK_MAX, t_ // 4), col_bits=max(1, (t_ - 1).bit_length())),
        grid=(b_, nq),
        in_specs=[rows(IDX_HEADS * IDX_DIM), rows(128),
                  pl.BlockSpec((1, ng, IDX_DIM, tk), lambda b, j: (b, 0, 0, 0)),
                  rows(ATT_Q_DIM),
                  pl.BlockSpec((1, ng, ATT_KV_DIM, tk), lambda b, j: (b, 0, 0, 0)),
                  pl.BlockSpec((1, t_, ATT_KV_HEADS * 128), lambda b, j: (b, 0, 0))],
        out_specs=rows(ATT_Q_DIM),
        out_shape=jax.ShapeDtypeStruct((b_ * t_, ATT_Q_DIM), MXU_DTYPE),
        scratch_shapes=[pltpu.VMEM((ng, tq, tk), jnp.int32),
                        pltpu.VMEM((ATT_HEADS, tq, 1), jnp.float32),
                        pltpu.VMEM((ATT_HEADS, tq, 128), jnp.float32)],
        compiler_params=pltpu.CompilerParams(
            dimension_semantics=("parallel", "arbitrary"),
            vmem_limit_bytes=V7X_VMEM_LIMIT_BYTES),
        name="dsa_attend",
    )(iq, iw, ikTb, q, kTb, vx.reshape(b_, t_, ATT_KV_HEADS * 128))


def _dsa_proj_q_lanes_kernel(x_ref, wqT_ref, wiqT_ref, wiwT_ref, wk_ref, wv_ref, wvxT_ref, wik_ref,
                             c_ref, s1_ref, s2_ref, cT_ref, sT_ref, kng_ref, knb_ref, onerow_ref,
                             qT_ref, iqT_ref, iwT_ref, k_ref, khd_ref, v_ref, vxT_ref, ik_ref, ikb_ref):
    xb = x_ref[...].astype(MXU_DTYPE)
    tm = xb.shape[0]
    c, s1, s2 = c_ref[...], s1_ref[...], s2_ref[...]
    cT, sT = cT_ref[...], sT_ref[...]
    dot = lambda a, b: jnp.dot(a, b, preferred_element_type=jnp.float32)
    dot_t = lambda w, a: lax.dot_general(w, a, (((1,), (1,)), ((), ())), preferred_element_type=jnp.float32)

    qT = _rope_rows(dot_t(wqT_ref[...], xb), cT, sT, ATT_HEAD_DIM, ROPE_DIM // 2) * (ATT_HEAD_DIM ** -0.5)
    iqT = _rope_rows(dot_t(wiqT_ref[...], xb), cT, sT, IDX_DIM, IDX_ROPE_DIM // 2)
    iwT = dot_t(wiwT_ref[...], xb) * (IDX_HEADS ** -0.5 * IDX_DIM ** -0.5)
    for t in range(tm // Q_BLOCK):
        lanes = slice(t * Q_BLOCK, (t + 1) * Q_BLOCK)
        qT_ref[0, t] = qT[:, lanes].astype(qT_ref.dtype)
        iqT_ref[0, t] = iqT[:, lanes].astype(iqT_ref.dtype)
        iwT_ref[0, t] = iwT[:, lanes]

    k = _rope_lanes(dot(xb, wk_ref[...]), c, s1, s2, ROPE_DIM // 2)
    k_ref[...] = k
    for g in range(ATT_KV_HEADS):
        khd_ref[g] = k[:, g * ATT_HEAD_DIM:(g + 1) * ATT_HEAD_DIM].astype(khd_ref.dtype)
    v_ref[...] = dot(xb, wv_ref[...])
    vxT_ref[0, 0] = (dot_t(wvxT_ref[...], xb) + onerow_ref[...]).astype(vxT_ref.dtype)

    ik = dot(xb, wik_ref[...])
    real = lax.broadcasted_iota(jnp.int32, ik.shape, 1) < IDX_DIM
    mu = jnp.sum(ik, axis=-1, keepdims=True) * (1.0 / IDX_DIM)
    ikc = jnp.where(real, ik - mu, 0.0)
    var = jnp.sum(ikc * ikc, axis=-1, keepdims=True) * (1.0 / IDX_DIM)
    ikn = _rope_lanes(ikc * lax.rsqrt(var + LN_EPS) * kng_ref[...] + knb_ref[...], c, s1, s2, IDX_ROPE_DIM // 2)
    ik_ref[...] = ikn[:, :IDX_DIM]
    ikb_ref[...] = ikn[:, :IDX_DIM].astype(ikb_ref.dtype)


def _dsa_project_q_lanes(x3d, pos, w_in, kn_g, kn_b):
    b_, t_, _ = x3d.shape
    tk = KEY_GROUP
    ng = t_ // tk
    nq = tk // Q_BLOCK
    m = b_ * t_
    w_q, w_k, w_v, w_iq, w_ik, w_iw = jnp.split(w_in, list(ATT_IN_SPLITS), axis=1)
    cast = lambda w: w.astype(MXU_DTYPE)
    w_vxT = jnp.pad(w_v.T.reshape(ATT_KV_HEADS, ATT_HEAD_DIM, D_MODEL),
                    ((0, 0), (0, 128 - ATT_HEAD_DIM), (0, 0))).reshape(ATT_KV_HEADS * 128, D_MODEL)
    one_row = jnp.tile((jnp.arange(128) == ATT_HEAD_DIM).astype(jnp.float32), ATT_KV_HEADS)[:, None]
    pad_lanes = lambda a: jnp.pad(a, ((0, 0), (0, 128 - a.shape[1])))
    c, s1, s2, cT, sT = _rope_lane_tables(pos, ROPE_DIM, ATT_HEAD_DIM)
    full = lambda shape: pl.BlockSpec(shape, lambda b, i: (0,) * len(shape))
    rows = lambda n: pl.BlockSpec((tk, n), lambda b, i: (b * ng + i, 0))
    ptab = lambda n: pl.BlockSpec((tk, n), lambda b, i: (i, 0))
    qtile = lambda n: pl.BlockSpec((1, nq, n, Q_BLOCK), lambda b, i: (b, i, 0, 0))
    sds = jax.ShapeDtypeStruct
    return pl.pallas_call(
        _dsa_proj_q_lanes_kernel,
        grid=(b_, ng),
        in_specs=[rows(D_MODEL), full((ATT_Q_DIM, D_MODEL)), full((IDX_HEADS * IDX_DIM, D_MODEL)),
                  full((IDX_HEADS, D_MODEL)), full((D_MODEL, ATT_KV_DIM)), full((D_MODEL, ATT_KV_DIM)),
                  full((ATT_KV_HEADS * 128, D_MODEL)), full((D_MODEL, 128)),
                  ptab(128), ptab(128), ptab(128),
                  pl.BlockSpec((ROPE_DIM // 2, tk), lambda b, i: (0, i)),
                  pl.BlockSpec((ROPE_DIM // 2, tk), lambda b, i: (0, i)),
                  full((1, 128)), full((1, 128)), full((ATT_KV_HEADS * 128, 1))],
        out_specs=[qtile(ATT_Q_DIM), qtile(IDX_HEADS * IDX_DIM), qtile(IDX_HEADS),
                   rows(ATT_KV_DIM), pl.BlockSpec((ATT_KV_HEADS, tk, ATT_HEAD_DIM), lambda b, i: (0, b * ng + i, 0)),
                   rows(ATT_KV_DIM), pl.BlockSpec((1, 1, ATT_KV_HEADS * 128, tk), lambda b, i: (b, i, 0, 0)),
                   rows(IDX_DIM), rows(IDX_DIM)],
        out_shape=[sds((b_, t_ // Q_BLOCK, ATT_Q_DIM, Q_BLOCK), MXU_DTYPE),
                   sds((b_, t_ // Q_BLOCK, IDX_HEADS * IDX_DIM, Q_BLOCK), MXU_DTYPE),
                   sds((b_, t_ // Q_BLOCK, IDX_HEADS, Q_BLOCK), jnp.float32),
                   sds((m, ATT_KV_DIM), jnp.float32), sds((ATT_KV_HEADS, m, ATT_HEAD_DIM), MXU_DTYPE),
                   sds((m, ATT_KV_DIM), jnp.float32), sds((b_, ng, ATT_KV_HEADS * 128, tk), MXU_DTYPE),
                   sds((m, IDX_DIM), jnp.float32), sds((m, IDX_DIM), MXU_DTYPE)],
        compiler_params=pltpu.CompilerParams(
            dimension_semantics=("parallel", "parallel"),
            vmem_limit_bytes=V7X_VMEM_LIMIT_BYTES),
        name="dsa_project_q_lanes",
    )(x3d.reshape(m, D_MODEL), cast(w_q.T), cast(w_iq.T), cast(w_iw.T), cast(w_k), cast(w_v), cast(w_vxT),
      cast(pad_lanes(w_ik)), c, s1, s2, cT, sT, pad_lanes(kn_g.reshape(1, IDX_DIM)),
      pad_lanes(kn_b.reshape(1, IDX_DIM)), one_row)


def _tree_sum(parts):
    while len(parts) > 1:
        parts = [parts[i] + parts[i + 1] for i in range(0, len(parts) - 1, 2)] + (
            [parts[-1]] if len(parts) % 2 else [])
    return parts[0]


def _dsa_attend_q_lanes_kernel(iqT_ref, iwT_ref, ik_ref, qT_ref, k_ref, vxT_ref, o_ref,
                               key_ref, bias_ref, m_ref, acc_ref, *, topk, col_bits):
    j = pl.program_id(1)
    tk, tq = key_ref.shape[1], key_ref.shape[2]
    n_groups = (j * tq + tq + tk - 1) // tk
    qpos = j * tq + lax.broadcasted_iota(jnp.int32, (tk, tq), 1)
    kpos0 = lax.broadcasted_iota(jnp.int32, (tk, tq), 0)
    dot = lambda a, b: jnp.dot(a, b, preferred_element_type=jnp.float32)

    def score_body(g, carry):
        start = pl.multiple_of(g * tk, tk)
        w_iq = jnp.concatenate([iqT_ref[0, 0, h * IDX_DIM:(h + 1) * IDX_DIM, :] for h in range(IDX_HEADS)], axis=1)
        s_all = dot(ik_ref[0, pl.ds(start, tk), :], w_iq)
        sc = _tree_sum([iwT_ref[0, 0, h:h + 1, :] * jnp.maximum(s_all[:, h * tq:(h + 1) * tq], 0.0)
                        for h in range(IDX_HEADS)])
        key_ref[g] = jnp.where(kpos0 + g * tk <= qpos, _sortable_key(sc), jnp.int32(INT32_MIN))
        return carry

    lax.fori_loop(0, n_groups, score_body, 0)

    def count_keys(pred):
        def body(g, part):
            hit = jnp.where(pred(key_ref[g], kpos0 + g * tk), 1.0, 0.0)
            return part + _tree_sum([hit[r * SUBLANES:(r + 1) * SUBLANES] for r in range(tk // SUBLANES)])
        part = lax.fori_loop(0, n_groups, body, jnp.zeros((SUBLANES, tq), jnp.float32))
        return jnp.sum(part, axis=0, keepdims=True)

    def bit_body(i, thr):
        cand = thr ^ lax.shift_left(jnp.int32(1), jnp.int32(31) - i)
        return jnp.where(count_keys(lambda k, kp: k >= cand) >= float(topk), cand, thr)

    thr = lax.fori_loop(0, 32, bit_body, jnp.full((1, tq), INT32_MIN, jnp.int32))

    need = float(topk) - count_keys(lambda k, kp: k > thr)

    def pos_body(i, last):
        cand = last | lax.shift_left(jnp.int32(1), jnp.int32(col_bits - 1) - i)
        return jnp.where(count_keys(lambda k, kp: (k == thr) & (kp < cand)) < need, cand, last)

    last_tie = lax.fori_loop(0, col_bits, pos_body, jnp.zeros((1, tq), jnp.int32))

    m_ref[...] = jnp.full(m_ref.shape, MASK_NEG, jnp.float32)
    acc_ref[...] = jnp.zeros(acc_ref.shape, jnp.float32)
    gsz = ATT_HEADS // ATT_KV_HEADS

    def attend_body(g, carry):
        start = pl.multiple_of(g * tk, tk)
        key = key_ref[g]
        kpos = kpos0 + g * tk
        keep = (key > thr) | ((key == thr) & (kpos <= last_tie))
        bias_ref[...] = jnp.where(keep & (kpos <= qpos), 0.0, MASK_NEG)
        for kv in range(ATT_KV_HEADS):
            w_q = jnp.concatenate([qT_ref[0, 0, (kv * gsz + i) * ATT_HEAD_DIM:(kv * gsz + i + 1) * ATT_HEAD_DIM, :]
                                   for i in range(gsz)], axis=1)
            s_all = dot(k_ref[kv, pl.ds(start, tk), :], w_q)
            vxT = vxT_ref[0, g, kv * 128:(kv + 1) * 128, :]
            for i in range(gsz):
                h = kv * gsz + i
                s = s_all[:, i * tq:(i + 1) * tq] + bias_ref[...]
                m_old = m_ref[h]
                m_new = jnp.maximum(m_old, jnp.max(s, axis=0, keepdims=True))
                p = jnp.exp(s - m_new).astype(vxT.dtype)
                acc_ref[h] = jnp.exp(m_old - m_new) * acc_ref[h] + dot(vxT, p)
                m_ref[h] = m_new
        return carry

    lax.fori_loop(0, n_groups, attend_body, 0)

    for h in range(ATT_HEADS):
        a = acc_ref[h]
        o = (a / a[ATT_HEAD_DIM:ATT_HEAD_DIM + 1, :]).T
        o_ref[:, h * ATT_HEAD_DIM:(h + 1) * ATT_HEAD_DIM] = o[:, :ATT_HEAD_DIM].astype(o_ref.dtype)


def _dsa_attend_q_lanes(b_, t_, qT, iqT, iwT, ikb, khd, vxT):
    ng, tk = vxT.shape[1], vxT.shape[3]
    tq = Q_BLOCK
    nq = t_ // tq
    qtile = lambda n: pl.BlockSpec((1, 1, n, tq), lambda b, j: (b, j, 0, 0))
    return pl.pallas_call(
        functools.partial(_dsa_attend_q_lanes_kernel, topk=min(TOPK_MAX, t_ // 4),
                          col_bits=max(1, (t_ - 1).bit_length())),
        grid=(b_, nq),
        in_specs=[qtile(IDX_HEADS * IDX_DIM), qtile(IDX_HEADS),
                  pl.BlockSpec((1, t_, IDX_DIM), lambda b, j: (b, 0, 0)),
                  qtile(ATT_Q_DIM),
                  pl.BlockSpec((ATT_KV_HEADS, t_, ATT_HEAD_DIM), lambda b, j: (0, b, 0)),
                  pl.BlockSpec((1, ng, ATT_KV_HEADS * 128, tk), lambda b, j: (b, 0, 0, 0))],
        out_specs=pl.BlockSpec((tq, ATT_Q_DIM), lambda b, j: (b * nq + j, 0)),
        out_shape=jax.ShapeDtypeStruct((b_ * t_, ATT_Q_DIM), MXU_DTYPE),
        scratch_shapes=[pltpu.VMEM((ng, tk, tq), jnp.int32),
                        pltpu.VMEM((tk, tq), jnp.float32),
                        pltpu.VMEM((ATT_HEADS, 1, tq), jnp.float32),
                        pltpu.VMEM((ATT_HEADS, 128, tq), jnp.float32)],
        compiler_params=pltpu.CompilerParams(
            dimension_semantics=("parallel", "arbitrary"),
            vmem_limit_bytes=V7X_VMEM_LIMIT_BYTES),
        name="dsa_attend_q_lanes",
    )(iqT, iwT, ikb.reshape(b_, t_, IDX_DIM), qT, khd, vxT)


def _dsa_prompt_pallas(x3d, w_in, kn_g, kn_b):
    b_, t_, _ = x3d.shape
    qT, iqT, iwT, k, khd, v, vxT, ik, ikb = _dsa_project_q_lanes(x3d, jnp.arange(t_), w_in, kn_g, kn_b)
    o = _dsa_attend_q_lanes(b_, t_, qT, iqT, iwT, ikb, khd, vxT)
    kv4 = lambda u: u.reshape(b_, t_, ATT_KV_HEADS, ATT_HEAD_DIM)
    return o, kv4(k), kv4(v), ik.reshape(b_, t_, IDX_DIM)


RW_ROW_TILE = 256


def _rwkv_proj_kernel(x_ref, xp_ref, mu_ref, wr_ref, wk_ref, wv_ref, w1_ref, w2_ref, a1_ref, a2_ref,
                      g1_ref, g2_ref, w0_ref, a0_ref, r_ref, d_ref, k_ref, v_ref, a_ref, g_ref):
    x = x_ref[...]
    dx = xp_ref[...] - x
    mix = lambda c: (x + dx * mu_ref[c:c + 1, :]).astype(MXU_DTYPE)
    dot = lambda a, b: jnp.dot(a.astype(MXU_DTYPE), b, preferred_element_type=jnp.float32)
    r_ref[...] = dot(mix(0), wr_ref[...])
    lora_w = dot(jnp.tanh(dot(mix(1), w1_ref[...])), w2_ref[...])
    w_log = -jax.nn.softplus(-(w0_ref[...] + lora_w)) - 0.5
    d_ref[...] = jnp.exp(-jnp.exp(w_log))
    k_ref[...] = dot(mix(2), wk_ref[...])
    v_ref[...] = dot(mix(3), wv_ref[...])
    a_ref[...] = jax.nn.sigmoid(a0_ref[...] + dot(dot(mix(4), a1_ref[...]), a2_ref[...]))
    g_ref[...] = dot(jax.nn.sigmoid(dot(mix(5), g1_ref[...])), g2_ref[...])


def _rwkv_project(x2d, xprev2d, mu, w_r, w_k, w_v, w0, w1, w2, a0, a1, a2, g1, g2):
    m = x2d.shape[0]
    tm = RW_ROW_TILE if m % RW_ROW_TILE == 0 else m
    cast = lambda w: w.astype(MXU_DTYPE)
    full = lambda a: pl.BlockSpec(a.shape, lambda i: (0,) * a.ndim)
    rows = pl.BlockSpec((tm, D_MODEL), lambda i: (i, 0))
    consts = [mu, cast(w_r), cast(w_k), cast(w_v), cast(w1), cast(w2), cast(a1), cast(a2), cast(g1), cast(g2),
              w0.reshape(1, D_MODEL), a0.reshape(1, D_MODEL)]
    return pl.pallas_call(
        _rwkv_proj_kernel,
        grid=(m // tm,),
        in_specs=[rows, rows] + [full(a) for a in consts],
        out_specs=[rows] * 6,
        out_shape=[jax.ShapeDtypeStruct((m, D_MODEL), jnp.float32)] * 6,
        compiler_params=pltpu.CompilerParams(
            dimension_semantics=("parallel",),
            vmem_limit_bytes=V7X_VMEM_LIMIT_BYTES),
        name="rwkv_project",
    )(x2d, xprev2d, *consts)


RW_LANES = 128
RW_TIME_CHUNK = 32


def _rwkv_scan_kernel(r_ref, d_ref, k_ref, v_ref, a_ref, s0_ref, kk_ref, ka_ref, rk_ref, gg_ref, gb_ref,
                      z_ref, s_out_ref, s_ref, vec_ref):
    c = pl.program_id(1)
    n = RW_HEAD

    @pl.when(c == 0)
    def _():
        s_ref[...] = s0_ref[...]

    def step(t, carry):
        r, k, v, a = r_ref[t], k_ref[t], v_ref[t], a_ref[t]
        kkr = k * kk_ref[...]
        nrm = jnp.sqrt(jnp.sum(kkr * kkr, axis=0, keepdims=True))
        kk = kkr / jnp.maximum(nrm, 1e-12)
        kmod = k * (1.0 + (a - 1.0) * ka_ref[...])
        vec_ref[0] = kk
        vec_ref[1] = d_ref[t]
        vec_ref[2] = kk * a
        vec_ref[3] = kmod
        vec_ref[4] = r
        row = lambda q, j: vec_ref[q, j:j + 1, :]
        sa = jnp.zeros((n, RW_LANES), jnp.float32)
        for j in range(n):
            sa = sa + s_ref[j] * row(0, j)
        y = jnp.zeros((n, RW_LANES), jnp.float32)
        for j in range(n):
            sn = s_ref[j] * row(1, j) - sa * row(2, j) + v * row(3, j)
            s_ref[j] = sn
            y = y + sn * row(4, j)
        mu = jnp.mean(y, axis=0, keepdims=True)
        yc = y - mu
        var = jnp.mean(yc * yc, axis=0, keepdims=True)
        bonus = jnp.sum(r * kmod * rk_ref[...], axis=0, keepdims=True)
        z_ref[t] = yc * lax.rsqrt(var + RW_GN_EPS) * gg_ref[...] + gb_ref[...] + bonus * v
        return carry

    lax.fori_loop(0, r_ref.shape[0], step, 0)

    @pl.when(c == pl.num_programs(1) - 1)
    def _():
        s_out_ref[...] = s_ref[...]


def _rwkv_scan(rT, dT, kT, vT, aT, s0T, k_k, k_a, r_k, gn_g, gn_b):
    t_, n, bh = rT.shape
    tc = RW_TIME_CHUNK if t_ % RW_TIME_CHUNK == 0 else t_
    reps = RW_LANES // RW_HEADS
    table = lambda p: jnp.tile(p.reshape(RW_HEADS, n).T, (1, reps))
    seq = pl.BlockSpec((tc, n, RW_LANES), lambda l, c: (c, 0, l))
    state = pl.BlockSpec((n, n, RW_LANES), lambda l, c: (0, 0, l))
    tab = pl.BlockSpec((n, RW_LANES), lambda l, c: (0, 0))
    return pl.pallas_call(
        _rwkv_scan_kernel,
        grid=(bh // RW_LANES, t_ // tc),
        in_specs=[seq] * 5 + [state] + [tab] * 5,
        out_specs=[seq, state],
        out_shape=[jax.ShapeDtypeStruct((t_, n, bh), jnp.float32),
                   jax.ShapeDtypeStruct((n, n, bh), jnp.float32)],
        scratch_shapes=[pltpu.VMEM((n, n, RW_LANES), jnp.float32),
                        pltpu.VMEM((5, n, RW_LANES), jnp.float32)],
        compiler_params=pltpu.CompilerParams(
            dimension_semantics=("parallel", "arbitrary"),
            vmem_limit_bytes=V7X_VMEM_LIMIT_BYTES),
        name="rwkv_scan",
    )(rT, dT, kT, vT, aT, s0T, table(k_k), table(k_a), table(r_k), table(gn_g), table(gn_b))


def _rwkv7_mixer_pallas(x3d, shift, wkv, mu, w_r, w_k, w_v, w0, w1, w2, a0, a1, a2, g1, g2,
                        k_k, k_a, r_k, gn_g, gn_b):
    b_, t_, _ = x3d.shape
    m = b_ * t_
    x_prev = jnp.concatenate([shift[:, None, :], x3d[:, :-1]], axis=1)
    r, d, k, v, a, g = _rwkv_project(x3d.reshape(m, D_MODEL), x_prev.reshape(m, D_MODEL), mu,
                                     w_r, w_k, w_v, w0, w1, w2, a0, a1, a2, g1, g2)
    to_scan = lambda u: jnp.transpose(u.reshape(b_, t_, RW_HEADS, RW_HEAD), (1, 3, 0, 2)).reshape(
        t_, RW_HEAD, b_ * RW_HEADS)
    s0T = jnp.transpose(wkv.astype(jnp.float32), (3, 2, 0, 1)).reshape(RW_HEAD, RW_HEAD, b_ * RW_HEADS)
    zT, sT = _rwkv_scan(to_scan(r), to_scan(d), to_scan(k), to_scan(v), to_scan(a), s0T,
                        k_k, k_a, r_k, gn_g, gn_b)
    z = jnp.transpose(zT.reshape(t_, RW_HEAD, b_, RW_HEADS), (2, 0, 3, 1)).reshape(m, D_MODEL)
    s_new = jnp.transpose(sT.reshape(RW_HEAD, RW_HEAD, b_, RW_HEADS), (2, 3, 1, 0)).astype(wkv.dtype)
    return z, g, x3d[:, -1], s_new


def _proj_gate_ln_kernel(x_ref, h_ref, gate_ref, w_ref, g_ref, b_ref, o_ref):
    h = (h_ref[...] * gate_ref[...]).astype(MXU_DTYPE)
    y = ALPHA * x_ref[...] + jnp.dot(h, w_ref[...], preferred_element_type=jnp.float32)
    o_ref[...] = _ln_rows(y, g_ref[...], b_ref[...])


def _proj_gate_post_norm(x2d, h2d, gate2d, w_out, g, b):
    m = x2d.shape[0]
    tm = _row_tile(m)
    rows = pl.BlockSpec((tm, D_MODEL), lambda i: (i, 0))
    vec = pl.BlockSpec((1, D_MODEL), lambda i: (0, 0))
    return pl.pallas_call(
        _proj_gate_ln_kernel,
        grid=(m // tm,),
        in_specs=[rows, rows, rows, pl.BlockSpec((D_MODEL, D_MODEL), lambda i: (0, 0)), vec, vec],
        out_specs=rows,
        out_shape=jax.ShapeDtypeStruct((m, D_MODEL), jnp.float32),
        compiler_params=pltpu.CompilerParams(
            dimension_semantics=("parallel",),
            vmem_limit_bytes=V7X_VMEM_LIMIT_BYTES),
        name="proj_gate_post_norm",
    )(x2d, h2d, gate2d, w_out, g.reshape(1, D_MODEL), b.reshape(1, D_MODEL))


GM_ROW_TILE = 256


def _gmlp_kernel(x_ref, win_ref, lng_ref, lnb_ref, mixw_ref, mixb_ref, wout_ref, g_ref, b_ref, *out_refs,
                 chunk_len, emit_v):
    x = x_ref[...]
    h = jax.nn.gelu(jnp.dot(x.astype(MXU_DTYPE), win_ref[...], preferred_element_type=jnp.float32))
    u = h[:, :GM_WIDTH]
    v = _ln_rows(h[:, GM_WIDTH:], lng_ref[...], lnb_ref[...])
    if emit_v:
        out_refs[1][...] = v
    if chunk_len == 1:
        gated = u * (v * mixw_ref[...] + mixb_ref[...])
    else:
        tm = x.shape[0]
        causal = (lax.broadcasted_iota(jnp.int32, (chunk_len, chunk_len), 0)
                  >= lax.broadcasted_iota(jnp.int32, (chunk_len, chunk_len), 1))
        vb = v.astype(MXU_DTYPE)
        cols = []
        for g in range(GM_GROUPS):
            w = jnp.where(causal, mixw_ref[g], 0.0).astype(MXU_DTYPE)
            bias = mixb_ref[:, g:g + 1]
            lanes = slice(g * GM_GROUP_DIM, (g + 1) * GM_GROUP_DIM)
            rows = [jnp.dot(w, vb[c * chunk_len:(c + 1) * chunk_len, lanes],
                            preferred_element_type=jnp.float32) + bias
                    for c in range(tm // chunk_len)]
            cols.append(jnp.concatenate(rows, axis=0))
        gated = u * jnp.concatenate(cols, axis=1)
    y = ALPHA * x + jnp.dot(gated.astype(MXU_DTYPE), wout_ref[...], preferred_element_type=jnp.float32)
    out_refs[0][...] = _ln_rows(y, g_ref[...], b_ref[...])


def _gmlp_block(x2d, seq_len, w_in, ln_g, ln_b, ws, bs, w_out, g, b, emit_v):
    m = x2d.shape[0]
    chunk_len = min(seq_len, CHUNK)
    if chunk_len == 1:
        tm = m
        mixw = jnp.repeat(ws[:, 0, 0], GM_GROUP_DIM)[None, :]
        mixb = jnp.repeat(bs[:, 0], GM_GROUP_DIM)[None, :]
    else:
        tm = GM_ROW_TILE
        mixw = ws[:, :chunk_len, :chunk_len]
        mixb = bs[:, :chunk_len].T
    full = lambda a: pl.BlockSpec(a.shape, lambda i: (0,) * a.ndim)
    rows = lambda n: pl.BlockSpec((tm, n), lambda i: (i, 0))
    consts = [w_in.astype(MXU_DTYPE), ln_g.reshape(1, GM_WIDTH), ln_b.reshape(1, GM_WIDTH), mixw, mixb,
              w_out.astype(MXU_DTYPE), g.reshape(1, D_MODEL), b.reshape(1, D_MODEL)]
    out_specs = [rows(D_MODEL)] + ([rows(GM_WIDTH)] if emit_v else [])
    out_shape = [jax.ShapeDtypeStruct((m, D_MODEL), jnp.float32)] + (
        [jax.ShapeDtypeStruct((m, GM_WIDTH), jnp.float32)] if emit_v else [])
    return pl.pallas_call(
        functools.partial(_gmlp_kernel, chunk_len=chunk_len, emit_v=emit_v),
        grid=(m // tm,),
        in_specs=[rows(D_MODEL)] + [full(a) for a in consts],
        out_specs=out_specs,
        out_shape=out_shape,
        compiler_params=pltpu.CompilerParams(
            dimension_semantics=("parallel",),
            vmem_limit_bytes=V7X_VMEM_LIMIT_BYTES),
        name="gmlp_block",
    )(x2d, *consts)


SSM_ROW_TILE = 256
SSM_BC_DIM = SSM_GROUPS * SSM_STATE
SSM_DT_LANES = 128
SUBLANES = 8


def _ssm_activate(xb, xbc, taps, wz_ref, wdt_ref, cw_ref, cb_ref, dtb_ref, z_ref, xs_ref, bm_ref, cm_ref, dt_ref):
    conv = cb_ref[...] + xbc * cw_ref[SSM_CONV - 1:SSM_CONV, :]
    for j in range(SSM_CONV - 1):
        conv = conv + taps[j] * cw_ref[j:j + 1, :]
    act = conv * jax.nn.sigmoid(conv)
    xs_ref[...] = act[:, :SSM_D_INNER]
    bm_ref[...] = act[:, SSM_D_INNER:SSM_D_INNER + SSM_BC_DIM].astype(bm_ref.dtype)
    cm_ref[...] = act[:, SSM_D_INNER + SSM_BC_DIM:].astype(cm_ref.dtype)
    z_ref[...] = jnp.dot(xb, wz_ref[...], preferred_element_type=jnp.float32)
    dt_ref[...] = jax.nn.softplus(jnp.dot(xb, wdt_ref[...], preferred_element_type=jnp.float32) + dtb_ref[...])


def _ssm_proj_seq_kernel(x_ref, halo_ref, cs_ref, wx_ref, wz_ref, wdt_ref, cw_ref, cb_ref, dtb_ref,
                         z_ref, xs_ref, bm_ref, cm_ref, dt_ref, tail_ref):
    i = pl.program_id(1)
    xb = x_ref[...].astype(MXU_DTYPE)
    xbc = jnp.dot(xb, wx_ref[...], preferred_element_type=jnp.float32)
    tm = xbc.shape[0]
    prev = jnp.dot(halo_ref[...].astype(MXU_DTYPE), wx_ref[...], preferred_element_type=jnp.float32)
    prev = jnp.where(i == 0, cs_ref[0], prev)
    row = lax.broadcasted_iota(jnp.int32, (tm, 1), 0)
    pad = jnp.zeros((tm - SUBLANES, xbc.shape[1]), jnp.float32)
    taps = []
    for j in range(SSM_CONV - 1):
        back = SSM_CONV - 1 - j
        head = jnp.concatenate([pltpu.roll(prev, back, 0), pad], axis=0)
        taps.append(jnp.where(row < back, head, pltpu.roll(xbc, back, 0)))
    _ssm_activate(xb, xbc, taps, wz_ref, wdt_ref, cw_ref, cb_ref, dtb_ref, z_ref, xs_ref, bm_ref, cm_ref, dt_ref)
    tail_ref[0] = xbc[tm - SUBLANES:, :]


def _ssm_proj_step_kernel(x_ref, st_ref, wx_ref, wz_ref, wdt_ref, cw_ref, cb_ref, dtb_ref,
                          z_ref, xs_ref, bm_ref, cm_ref, dt_ref, st_out_ref):
    xb = x_ref[...].astype(MXU_DTYPE)
    xbc = jnp.dot(xb, wx_ref[...], preferred_element_type=jnp.float32)
    taps = [st_ref[j] for j in range(SSM_CONV - 1)]
    _ssm_activate(xb, xbc, taps, wz_ref, wdt_ref, cw_ref, cb_ref, dtb_ref, z_ref, xs_ref, bm_ref, cm_ref, dt_ref)
    for j in range(SSM_CONV - 2):
        st_out_ref[j] = st_ref[j + 1]
    st_out_ref[SSM_CONV - 2] = xbc


def _ssm_project(x3d, conv_state, w_in, conv_w, conv_b, dt_bias):
    b_, t_, _ = x3d.shape
    m = b_ * t_
    w_z, w_x, w_dt = jnp.split(w_in, [SSM_D_INNER, SSM_D_INNER + SSM_CONV_DIM], axis=1)
    cast = lambda w: w.astype(MXU_DTYPE)
    consts = [cast(w_x), cast(w_z), cast(jnp.pad(w_dt, ((0, 0), (0, SSM_DT_LANES - SSM_HEADS)))),
              conv_w, conv_b.reshape(1, SSM_CONV_DIM),
              jnp.pad(dt_bias, (0, SSM_DT_LANES - SSM_HEADS)).reshape(1, SSM_DT_LANES)]
    sds = jax.ShapeDtypeStruct
    outs = [sds((m, SSM_D_INNER), jnp.float32), sds((m, SSM_D_INNER), jnp.float32),
            sds((m, SSM_BC_DIM), MXU_DTYPE), sds((m, SSM_BC_DIM), MXU_DTYPE), sds((m, SSM_DT_LANES), jnp.float32)]
    widths = [SSM_D_INNER, SSM_D_INNER, SSM_BC_DIM, SSM_BC_DIM, SSM_DT_LANES]
    params = dict(vmem_limit_bytes=V7X_VMEM_LIMIT_BYTES)
    x2d = x3d.reshape(m, D_MODEL)
    if t_ == 1:
        full = lambda a: pl.BlockSpec(a.shape, lambda i: (0,) * a.ndim)
        st = jnp.transpose(conv_state, (1, 0, 2))
        res = pl.pallas_call(
            _ssm_proj_step_kernel,
            grid=(1,),
            in_specs=[full(x2d), full(st)] + [full(a) for a in consts],
            out_specs=[pl.BlockSpec((m, w), lambda i: (0, 0)) for w in widths] + [full(st)],
            out_shape=outs + [sds(st.shape, jnp.float32)],
            compiler_params=pltpu.CompilerParams(dimension_semantics=("arbitrary",), **params),
            name="ssm_project_step",
        )(x2d, st, *consts)
        return list(res[:5]) + [jnp.transpose(res[5], (1, 0, 2))]
    tm = SSM_ROW_TILE
    nt = t_ // tm
    full = lambda a: pl.BlockSpec(a.shape, lambda b, i: (0,) * a.ndim)
    rows = lambda w: pl.BlockSpec((tm, w), lambda b, i: (b * nt + i, 0))
    halo = pl.BlockSpec((SUBLANES, D_MODEL), lambda b, i: (jnp.maximum((b * nt + i) * (tm // SUBLANES) - 1, 0), 0))
    cs8 = jnp.pad(conv_state, ((0, 0), (SUBLANES - (SSM_CONV - 1), 0), (0, 0)))
    tail = pl.BlockSpec((1, SUBLANES, SSM_CONV_DIM), lambda b, i: (b, 0, 0))
    res = pl.pallas_call(
        _ssm_proj_seq_kernel,
        grid=(b_, nt),
        in_specs=[rows(D_MODEL), halo, tail] + [full(a) for a in consts],
        out_specs=[rows(w) for w in widths] + [tail],
        out_shape=outs + [sds((b_, SUBLANES, SSM_CONV_DIM), jnp.float32)],
        compiler_params=pltpu.CompilerParams(dimension_semantics=("parallel", "arbitrary"), **params),
        name="ssm_project_seq",
    )(x2d, x2d, cs8, *consts)
    return list(res[:5]) + [res[5][:, SUBLANES - (SSM_CONV - 1):, :]]


def _ssm_gate_norm(y, xs, z, dskip, normg):
    yg = (y + xs * dskip) * (z * jax.nn.sigmoid(z))
    gw = SSM_D_INNER // SSM_GROUPS
    outs = []
    for g in range(SSM_GROUPS):
        part = yg[:, g * gw:(g + 1) * gw]
        ms = jnp.mean(part * part, axis=-1, keepdims=True)
        outs.append(part * lax.rsqrt(ms + LN_EPS))
    return jnp.concatenate(outs, axis=1) * normg


def _ssm_chunk_kernel(xs_ref, bm_ref, cm_ref, dt_ref, z_ref, aneg_ref, dskip_ref, normg_ref,
                      yg_ref, hT_out_ref, hT_ref, y_ref):
    c = pl.program_id(1)
    l = xs_ref.shape[0]

    @pl.when(c == 0)
    def _():
        hT_ref[...] = jnp.zeros_like(hT_ref)

    dt = dt_ref[...]
    a = dt * aneg_ref[...]
    r_i = lax.broadcasted_iota(jnp.int32, (l, l), 0)
    c_i = lax.broadcasted_iota(jnp.int32, (l, l), 1)
    causal = r_i >= c_i
    tril = jnp.where(causal, 1.0, 0.0)
    hi = lax.Precision.HIGHEST
    acum = jnp.dot(tril, a, precision=hi, preferred_element_type=jnp.float32)
    acum_t = jnp.dot(a.T, tril.T, precision=hi, preferred_element_type=jnp.float32)
    a_last = acum[l - 1:l, :]
    to_end = jnp.exp(a_last - acum)
    from_start = jnp.exp(acum)
    chunk_decay = jnp.exp(a_last)
    xs = xs_ref[...]
    for g in range(SSM_GROUPS):
        bm = bm_ref[:, g * SSM_STATE:(g + 1) * SSM_STATE]
        cm = cm_ref[:, g * SSM_STATE:(g + 1) * SSM_STATE]
        cb = lax.dot_general(cm, bm, (((1,), (1,)), ((), ())), preferred_element_type=jnp.float32)
        bm_t = bm.astype(jnp.float32).T.astype(MXU_DTYPE)
        for e in range(SSM_HPG):
            h = g * SSM_HPG + e
            seg = jnp.exp(jnp.where(causal, acum[:, h:h + 1] - acum_t[h:h + 1, :], -jnp.inf))
            xdt = xs[:, h * SSM_HEAD_DIM:(h + 1) * SSM_HEAD_DIM] * dt[:, h:h + 1]
            y_diag = jnp.dot((cb * seg).astype(MXU_DTYPE), xdt.astype(MXU_DTYPE),
                             preferred_element_type=jnp.float32)
            h_in = hT_ref[h]
            y_off = jnp.dot(cm, h_in.astype(MXU_DTYPE), preferred_element_type=jnp.float32) * from_start[:, h:h + 1]
            y_ref[:, h * SSM_HEAD_DIM:(h + 1) * SSM_HEAD_DIM] = y_diag + y_off
            st = jnp.dot(bm_t, (xdt * to_end[:, h:h + 1]).astype(MXU_DTYPE), preferred_element_type=jnp.float32)
            hT_ref[h] = h_in * chunk_decay[:, h:h + 1] + st
    yg_ref[...] = _ssm_gate_norm(y_ref[...], xs, z_ref[...], dskip_ref[...], normg_ref[...]).astype(yg_ref.dtype)

    @pl.when(c == pl.num_programs(1) - 1)
    def _():
        hT_out_ref[0] = hT_ref[...]


def _ssm_head_lanes(p):
    return jnp.pad(p.astype(jnp.float32), (0, SSM_DT_LANES - SSM_HEADS)).reshape(1, SSM_DT_LANES)


def _ssm_chunk_scan(b_, t_, xs, bm, cm, dt, z, a_log, d_skip, norm_g):
    l = SSM_CHUNK
    nc = t_ // l
    rows = lambda w: pl.BlockSpec((l, w), lambda b, c: (b * nc + c, 0))
    vec = lambda w: pl.BlockSpec((1, w), lambda b, c: (0, 0))
    aneg = _ssm_head_lanes(-jnp.exp(a_log.astype(jnp.float32)))
    dskip = jnp.repeat(d_skip, SSM_HEAD_DIM).reshape(1, SSM_D_INNER)
    yg, hT = pl.pallas_call(
        _ssm_chunk_kernel,
        grid=(b_, nc),
        in_specs=[rows(SSM_D_INNER), rows(SSM_BC_DIM), rows(SSM_BC_DIM), rows(SSM_DT_LANES), rows(SSM_D_INNER),
                  vec(SSM_DT_LANES), vec(SSM_D_INNER), vec(SSM_D_INNER)],
        out_specs=[rows(SSM_D_INNER),
                   pl.BlockSpec((1, SSM_HEADS, SSM_STATE, SSM_HEAD_DIM), lambda b, c: (b, 0, 0, 0))],
        out_shape=[jax.ShapeDtypeStruct((b_ * t_, SSM_D_INNER), MXU_DTYPE),
                   jax.ShapeDtypeStruct((b_, SSM_HEADS, SSM_STATE, SSM_HEAD_DIM), jnp.float32)],
        scratch_shapes=[pltpu.VMEM((SSM_HEADS, SSM_STATE, SSM_HEAD_DIM), jnp.float32),
                        pltpu.VMEM((l, SSM_D_INNER), jnp.float32)],
        compiler_params=pltpu.CompilerParams(
            dimension_semantics=("parallel", "arbitrary"),
            vmem_limit_bytes=V7X_VMEM_LIMIT_BYTES),
        name="ssm_chunk_scan",
    )(xs, bm, cm, dt, z, aneg, dskip, norm_g.reshape(1, SSM_D_INNER))
    return yg, jnp.transpose(hT, (0, 1, 3, 2))


def _ssm_step_kernel(h0_ref, xs_ref, dt_ref, an_ref, bm_ref, cm_ref, y_ref, h_ref):
    h0 = h0_ref[0]
    dt = dt_ref[0]
    decay = jnp.exp(dt * an_ref[...])
    xdt = xs_ref[0] * dt
    bm = bm_ref[0].astype(jnp.float32)
    cm = cm_ref[0].astype(jnp.float32)
    h_ref[0] = h0 * decay + xdt * bm
    cb = jnp.sum(cm * bm, axis=-1, keepdims=True)
    y_ref[0] = cb * xdt + jnp.sum(cm * h0, axis=-1, keepdims=True) * decay


def _ssm_step(state, xs, bm, cm, dt, a_log):
    b_ = state.shape[0]
    per_head = lambda u: jnp.repeat(u.reshape(b_, SSM_GROUPS, 1, SSM_STATE), SSM_HPG, axis=1)
    xs4 = xs.reshape(b_, SSM_HEADS, SSM_HEAD_DIM, 1)
    dt4 = dt[:, :SSM_HEADS].reshape(b_, SSM_HEADS, 1, 1)
    an = (-jnp.exp(a_log.astype(jnp.float32))).reshape(SSM_HEADS, 1, 1)
    blk = lambda a: pl.BlockSpec((1,) + a.shape[1:], lambda b: (b, 0, 0, 0))
    args = [state.astype(jnp.float32), xs4, dt4, an, per_head(bm), per_head(cm)]
    y4, h_new = pl.pallas_call(
        _ssm_step_kernel,
        grid=(b_,),
        in_specs=[blk(args[0]), blk(xs4), blk(dt4), pl.BlockSpec(an.shape, lambda b: (0, 0, 0)),
                  blk(args[4]), blk(args[5])],
        out_specs=[blk(xs4), blk(args[0])],
        out_shape=[jax.ShapeDtypeStruct(xs4.shape, jnp.float32), jax.ShapeDtypeStruct(state.shape, jnp.float32)],
        compiler_params=pltpu.CompilerParams(
            dimension_semantics=("parallel",),
            vmem_limit_bytes=V7X_VMEM_LIMIT_BYTES),
        name="ssm_step",
    )(*args)
    return y4.reshape(b_, SSM_D_INNER), h_new


def _ssm_gate_norm_kernel(y_ref, xs_ref, z_ref, dskip_ref, normg_ref, o_ref):
    o_ref[...] = _ssm_gate_norm(y_ref[...], xs_ref[...], z_ref[...], dskip_ref[...], normg_ref[...]).astype(o_ref.dtype)


def _ssm_gate_norm_rows(y, xs, z, d_skip, norm_g):
    full = lambda a: pl.BlockSpec(a.shape, lambda i: (0,) * a.ndim)
    args = [y, xs, z, jnp.repeat(d_skip, SSM_HEAD_DIM).reshape(1, SSM_D_INNER), norm_g.reshape(1, SSM_D_INNER)]
    return pl.pallas_call(
        _ssm_gate_norm_kernel,
        grid=(1,),
        in_specs=[full(a) for a in args],
        out_specs=full(y),
        out_shape=jax.ShapeDtypeStruct(y.shape, MXU_DTYPE),
        name="ssm_gate_norm",
    )(*args)


def _mamba2_mixer_pallas(x3d, conv_state, ssm_state, w_in, conv_w, conv_b, dt_bias, a_log, d_skip, norm_g):
    b_, t_, _ = x3d.shape
    z, xs, bm, cm, dt, conv_new = _ssm_project(x3d, conv_state, w_in, conv_w, conv_b, dt_bias)
    if t_ == 1:
        y, h_new = _ssm_step(ssm_state, xs, bm, cm, dt, a_log)
        yg = _ssm_gate_norm_rows(y, xs, z, d_skip, norm_g)
    else:
        yg, h_new = _ssm_chunk_scan(b_, t_, xs, bm, cm, dt, z, a_log, d_skip, norm_g)
    return yg, conv_new, h_new.astype(ssm_state.dtype)


PAGES_PER_STEP = 8


def _sortable_key(score):
    bits = pltpu.bitcast(score, jnp.int32)
    return jnp.where(bits >= 0, bits, bits ^ jnp.int32(0x7FFFFFFF))


def _dsa_decode_kernel(pt_ref, iq_ref, iw_ref, q_ref, ikn_ref, kn_ref, vn_ref, *rest,
                       topk, col_bits, n_steps, pages):
    idx_refs, k_refs, v_refs = rest[:pages], rest[pages:2 * pages], rest[2 * pages:3 * pages]
    o_ref, key_ref, sel_ref, m_ref, l_ref, acc_ref = rest[3 * pages:]
    s = pl.program_id(1)
    nk = key_ref.shape[2]
    nt = (((1,), (1,)), ((), ()))
    iq = iq_ref[0]
    iw = iw_ref[0]
    lane = lax.broadcasted_iota(jnp.int32, (1, nk), 1)

    def row_dot(a, row):
        return jnp.sum(a.astype(jnp.float32) * row.astype(jnp.float32), axis=1, keepdims=True)

    def index_score(ik):
        if ik.shape[0] == 1:
            sc = row_dot(iq, ik)
        else:
            sc = lax.dot_general(iq, ik, nt, preferred_element_type=jnp.float32)
        return jnp.sum(iw * jnp.maximum(sc, 0.0), axis=0, keepdims=True)

    def fold(hit):
        out = hit[:, 0:128]
        for l in range(1, nk // 128):
            out = out + hit[:, l * 128:(l + 1) * 128]
        return out

    @pl.when(s < n_steps)
    def _score():
        ik = jnp.concatenate([r[0] for r in idx_refs], axis=0).astype(MXU_DTYPE)
        key_ref[s] = _sortable_key(index_score(ik))

    @pl.when(s == n_steps - 1)
    def _select():
        key_new = _sortable_key(index_score(ikn_ref[0]))

        def count(pred_past, pred_new):
            cnt = jnp.zeros((1, 128), jnp.float32)
            for st in range(n_steps):
                cnt = cnt + fold(jnp.where(pred_past(key_ref[st], lane + st * nk), 1.0, 0.0))
            return jnp.sum(cnt, axis=1, keepdims=True) + jnp.where(pred_new(key_new), 1.0, 0.0)

        def bit_body(i, thr):
            cand = thr ^ lax.shift_left(jnp.int32(1), jnp.int32(31) - i)
            total = count(lambda k, c: k >= cand, lambda k: k >= cand)
            return jnp.where(total >= float(topk), cand, thr)

        thr = lax.fori_loop(0, 32, bit_body, jnp.full((1, 1), INT32_MIN, jnp.int32))
        need = float(topk) - count(lambda k, c: k > thr, lambda k: k > thr)

        def col_body(i, last):
            cand = last | lax.shift_left(jnp.int32(1), jnp.int32(col_bits - 1) - i)
            ties = count(lambda k, c: (k == thr) & (c < cand),
                         lambda k: (k == thr) & (jnp.int32(n_steps * nk) < cand))
            return jnp.where(ties < need, cand, last)

        last_tie = lax.fori_loop(0, col_bits, col_body, jnp.zeros((1, 1), jnp.int32))
        sel_ref[0] = jnp.broadcast_to(thr, sel_ref.shape[1:])
        sel_ref[1] = jnp.broadcast_to(last_tie, sel_ref.shape[1:])
        sel_ref[2] = jnp.broadcast_to(key_new, sel_ref.shape[1:])

    gsz = ATT_HEADS // ATT_KV_HEADS
    q = q_ref[0]
    q_wide = jnp.concatenate([q] * ATT_KV_HEADS, axis=1)
    head_i = lax.broadcasted_iota(jnp.int32, q_wide.shape, 0)
    col_i = lax.broadcasted_iota(jnp.int32, q_wide.shape, 1)
    own_group = (col_i // ATT_HEAD_DIM) == (head_i // gsz)
    q_blk = jnp.where(own_group, q_wide, jnp.zeros_like(q_wide))

    def keep_mask(key, col):
        thr, last_tie = sel_ref[0, 0:1, 0:1], sel_ref[1, 0:1, 0:1]
        return (key > thr) | ((key == thr) & (col <= last_tie))

    def online_update(logits, weighted_values):
        m_old = m_ref[...]
        m_new = jnp.maximum(m_old, jnp.max(logits, axis=1, keepdims=True))
        p = jnp.exp(logits - m_new)
        alpha = jnp.exp(m_old - m_new)
        l_ref[...] = alpha * l_ref[...] + jnp.sum(p, axis=1, keepdims=True)
        acc_ref[...] = alpha * acc_ref[...] + weighted_values(p.astype(MXU_DTYPE))
        m_ref[...] = m_new

    @pl.when(s == n_steps)
    def _init():
        m_ref[...] = jnp.full(m_ref.shape, MASK_NEG, jnp.float32)
        l_ref[...] = jnp.zeros(l_ref.shape, jnp.float32)
        acc_ref[...] = jnp.zeros(acc_ref.shape, jnp.float32)

    @pl.when(s >= n_steps)
    def _attend():
        st = s - n_steps
        bias = jnp.where(keep_mask(key_ref[st], lane + st * nk), 0.0, MASK_NEG)
        kk = jnp.concatenate([r[0] for r in k_refs], axis=0).astype(MXU_DTYPE)
        vv = jnp.concatenate([r[0] for r in v_refs], axis=0).astype(MXU_DTYPE)
        online_update(lax.dot_general(q_blk, kk, nt, preferred_element_type=jnp.float32) + bias,
                      lambda p: jnp.dot(p, vv, preferred_element_type=jnp.float32))

    @pl.when(s == 2 * n_steps - 1)
    def _finish():
        keep_new = keep_mask(sel_ref[2, 0:1, 0:1], jnp.int32(n_steps * nk))
        logit = row_dot(q_blk, kn_ref[0])
        v_row = vn_ref[0].astype(jnp.float32)
        online_update(logit + jnp.where(keep_new, 0.0, MASK_NEG), lambda p: p.astype(jnp.float32) * v_row)
        out = jnp.where(own_group, acc_ref[...] / l_ref[...], 0.0)
        o = out[:, 0:ATT_HEAD_DIM]
        for g in range(1, ATT_KV_HEADS):
            o = o + out[:, g * ATT_HEAD_DIM:(g + 1) * ATT_HEAD_DIM]
        o_ref[0] = o.astype(o_ref.dtype)


def _dsa_decode(q, iq, iw, ik_new, k_new, v_new, cache_k, cache_v, cache_idx_k, page_table):
    b_, n_pages = page_table.shape
    n_pool, page = cache_k.shape[0], cache_k.shape[1]
    pages = PAGES_PER_STEP
    n_steps = n_pages // pages
    past = n_pages * page
    ck = cache_k.reshape(n_pool, page, ATT_KV_DIM)
    cv = cache_v.reshape(n_pool, page, ATT_KV_DIM)
    per_seq = lambda a: pl.BlockSpec((1,) + a.shape[1:], lambda b, s, pt: (b,) + (0,) * (a.ndim - 1))

    def paged(width, j, attend_phase):
        def index(b, s, pt):
            grp = jnp.maximum(s - n_steps, 0) if attend_phase else jnp.minimum(s, n_steps - 1)
            return (pt[b, grp * pages + j], 0, 0)
        return pl.BlockSpec((1, page, width), index)

    small = [iq.reshape(b_, IDX_HEADS, IDX_DIM), iw[:, :IDX_HEADS].reshape(b_, IDX_HEADS, 1),
             q.reshape(b_, ATT_HEADS, ATT_HEAD_DIM), ik_new.astype(MXU_DTYPE).reshape(b_, 1, IDX_DIM),
             k_new.astype(MXU_DTYPE).reshape(b_, 1, ATT_KV_DIM), v_new.astype(MXU_DTYPE).reshape(b_, 1, ATT_KV_DIM)]
    grid_spec = pltpu.PrefetchScalarGridSpec(
        num_scalar_prefetch=1,
        grid=(b_, 2 * n_steps),
        in_specs=[per_seq(a) for a in small]
        + [paged(IDX_DIM, j, False) for j in range(pages)]
        + [paged(ATT_KV_DIM, j, True) for j in range(pages)]
        + [paged(ATT_KV_DIM, j, True) for j in range(pages)],
        out_specs=pl.BlockSpec((1, ATT_HEADS, ATT_HEAD_DIM), lambda b, s, pt: (b, 0, 0)),
        scratch_shapes=[pltpu.VMEM((n_steps, 1, pages * page), jnp.int32),
                        pltpu.VMEM((3, SUBLANES, 128), jnp.int32),
                        pltpu.VMEM((ATT_HEADS, 1), jnp.float32),
                        pltpu.VMEM((ATT_HEADS, 1), jnp.float32),
                        pltpu.VMEM((ATT_HEADS, ATT_KV_DIM), jnp.float32)])
    o = pl.pallas_call(
        functools.partial(_dsa_decode_kernel, topk=min(TOPK_MAX, (past + 1) // 4),
                          col_bits=max(1, past.bit_length()), n_steps=n_steps, pages=pages),
        grid_spec=grid_spec,
        out_shape=jax.ShapeDtypeStruct((b_, ATT_HEADS, ATT_HEAD_DIM), MXU_DTYPE),
        compiler_params=pltpu.CompilerParams(
            dimension_semantics=("parallel", "arbitrary"),
            vmem_limit_bytes=V7X_VMEM_LIMIT_BYTES),
        name="dsa_decode",
    )(page_table, *small, *([cache_idx_k] * pages), *([ck] * pages), *([cv] * pages))
    return o.reshape(b_, ATT_Q_DIM)


def _dsa_sample_pallas(x3d, cache_k, cache_v, cache_idx_k, page_table, w_in, kn_g, kn_b):
    b_, t_, _ = x3d.shape
    past = page_table.shape[1] * cache_k.shape[1]
    pos = jnp.full((b_,), past, jnp.int32)
    q, iq, v, _, iw, kT, _, ikT, _ = _dsa_project(x3d.reshape(1, b_, D_MODEL), pos, w_in, kn_g, kn_b)
    k = _untranspose_groups(kT)[0]
    ik = _untranspose_groups(ikT)[0]
    o = _dsa_decode(q, iq, iw, ik, k, v, cache_k, cache_v, cache_idx_k, page_table)
    kv4 = lambda u: u.reshape(b_, t_, ATT_KV_HEADS, ATT_HEAD_DIM)
    return o, kv4(k), kv4(v), ik.reshape(b_, t_, IDX_DIM)


def _layer_norm(x, g, b):
    xf = x.astype(jnp.float32)
    mu = jnp.mean(xf, -1, keepdims=True)
    var = jnp.mean(jnp.square(xf - mu), -1, keepdims=True)
    return ((xf - mu) * lax.rsqrt(var + LN_EPS)).astype(x.dtype) * g + b


def _rope_partial(x, pos, rot_dim):
    half = rot_dim // 2
    inv = ROPE_THETA ** (-jnp.arange(half, dtype=jnp.float32) / half)
    ang = pos.astype(jnp.float32)[:, None] * inv[None, :]
    cos = jnp.cos(ang)[:, None, :]
    sin = jnp.sin(ang)[:, None, :]
    xf = x[..., :rot_dim].astype(jnp.float32)
    x1, x2 = xf[..., :half], xf[..., half:]
    rot = jnp.concatenate([x1 * cos - x2 * sin, x2 * cos + x1 * sin], axis=-1).astype(x.dtype)
    return jnp.concatenate([rot, x[..., rot_dim:]], axis=-1)


def _gather_rows(rows, idx):
    return jax.vmap(lambda r, i: r[i])(rows, idx)


def _gmlp_mixer(x, w_in, ln_g, ln_b, ws, bs):
    B_, T, _ = x.shape
    u, v = jnp.split(jax.nn.gelu(x @ w_in), 2, axis=-1)
    v = _layer_norm(v, ln_g, ln_b)
    l = min(T, CHUNK)
    c = T // l
    mask = jnp.tril(jnp.ones((l, l), dtype=bool))
    w = jnp.where(mask, ws[:, :l, :l], 0.0)
    vc = v.reshape(B_, c, l, GM_GROUPS, GM_GROUP_DIM)
    mixed = jnp.einsum('gts,bcsgd->bctgd', w, vc) + jnp.transpose(bs[:, :l])[:, :, None]
    return u * mixed.reshape(B_, T, GM_WIDTH), v


def _ssd_chunked(xs, dt, a, bm, cm, h0):
    B_, T = xs.shape[:2]
    l = min(T, SSM_CHUNK)
    c = T // l
    blk = lambda t: t.reshape((B_, c, l) + t.shape[2:])
    xdt = blk(xs.astype(jnp.float32) * dt[..., None])
    bc, cc, acum = blk(bm), blk(cm), jnp.cumsum(blk(a), axis=2)
    at = jnp.moveaxis(acum, 2, -1)
    causal = jnp.tril(jnp.ones((l, l), dtype=bool))
    seg = jnp.exp(jnp.where(causal, at[..., :, None] - at[..., None, :], -jnp.inf))
    cb = jnp.einsum('bctgn,bcsgn->bcgts', cc, bc)
    y_diag = jnp.einsum('bcgts,bcgets,bcsgep->bctgep', cb, seg, xdt)
    states = jnp.einsum('bclgn,bclge,bclgep->bcgepn', bc, jnp.exp(acum[:, :, -1:] - acum), xdt)
    chunk_decay = jnp.exp(acum[:, :, -1])

    def step(h, inp):
        dec, st = inp
        return h * dec[..., None, None] + st, h

    h_last, h_in = lax.scan(step, h0, (jnp.moveaxis(chunk_decay, 1, 0), jnp.moveaxis(states, 1, 0)))
    y_off = jnp.einsum('bctgn,bcgepn,bctge->bctgep', cc, jnp.moveaxis(h_in, 0, 1), jnp.exp(acum))
    return (y_diag + y_off).reshape(B_, T, SSM_GROUPS, SSM_HPG, SSM_HEAD_DIM), h_last


def _mamba2_mixer(x, conv_state, ssm_state, w_in, conv_w, conv_b, dt_bias, a_log, d_skip, norm_g):
    B_, T, _ = x.shape
    z, xbc, dt = jnp.split(x @ w_in, [SSM_D_INNER, SSM_D_INNER + SSM_CONV_DIM], axis=-1)
    xbc_ext = jnp.concatenate([conv_state, xbc], axis=1)
    conv = conv_b
    for j in range(SSM_CONV):
        conv = conv + xbc_ext[:, j:j + T] * conv_w[j]
    xbc = jax.nn.silu(conv)
    xs, bm, cm = jnp.split(xbc, [SSM_D_INNER, SSM_D_INNER + SSM_GROUPS * SSM_STATE], axis=-1)
    xs = xs.reshape(B_, T, SSM_GROUPS, SSM_HPG, SSM_HEAD_DIM)
    bm = bm.reshape(B_, T, SSM_GROUPS, SSM_STATE)
    cm = cm.reshape(B_, T, SSM_GROUPS, SSM_STATE)
    dt = jax.nn.softplus((dt + dt_bias).astype(jnp.float32)).reshape(B_, T, SSM_GROUPS, SSM_HPG)
    a_neg = -jnp.exp(a_log.astype(jnp.float32)).reshape(SSM_GROUPS, SSM_HPG)
    h0 = ssm_state.astype(jnp.float32).reshape(B_, SSM_GROUPS, SSM_HPG, SSM_HEAD_DIM, SSM_STATE)
    y, h_last = _ssd_chunked(xs, dt, dt * a_neg, bm, cm, h0)
    y = y.astype(x.dtype) + xs * d_skip.reshape(SSM_GROUPS, SSM_HPG, 1)
    yg = (y.reshape(B_, T, SSM_D_INNER) * jax.nn.silu(z)).reshape(B_, T, SSM_GROUPS, -1).astype(jnp.float32)
    yg = (yg * lax.rsqrt(jnp.mean(jnp.square(yg), -1, keepdims=True) + LN_EPS)).astype(x.dtype)
    yg = yg.reshape(B_, T, SSM_D_INNER) * norm_g
    new_ssm = h_last.reshape(B_, SSM_HEADS, SSM_HEAD_DIM, SSM_STATE).astype(ssm_state.dtype)
    return yg, xbc_ext[:, T:], new_ssm


def _dsa_project_jax(x, pos, w_in, kn_g, kn_b):
    B_, T, _ = x.shape
    q, k, v, iq, ik, iw = jnp.split(x @ w_in, list(ATT_IN_SPLITS), axis=-1)
    q = _rope_partial(q.reshape(B_, T, ATT_HEADS, ATT_HEAD_DIM), pos, ROPE_DIM)
    k = _rope_partial(k.reshape(B_, T, ATT_KV_HEADS, ATT_HEAD_DIM), pos, ROPE_DIM)
    v = v.reshape(B_, T, ATT_KV_HEADS, ATT_HEAD_DIM)
    iq = _rope_partial(iq.reshape(B_, T, IDX_HEADS, IDX_DIM), pos, IDX_ROPE_DIM)
    ik = _rope_partial(_layer_norm(ik, kn_g, kn_b)[:, :, None, :], pos, IDX_ROPE_DIM)[:, :, 0, :]
    iw = iw * (IDX_HEADS ** -0.5 * IDX_DIM ** -0.5)
    return q, k, v, iq, ik, iw


def _dsa_select(iq, iw, ik, qpos, topk):
    s = jnp.einsum('bqhd,bsd->bqhs', iq, ik)
    score = jnp.einsum('bqh,bqhs->bqs', iw, jax.nn.relu(s)).astype(jnp.float32)
    adm = jnp.arange(ik.shape[1])[None, :] <= qpos[:, None]
    score = jnp.where(adm[None], score, -jnp.inf)
    _, idx = lax.top_k(score, topk)
    return idx, idx <= qpos[None, :, None]


def _sparse_attend(q, k_sel, v_sel, valid):
    B_, Q = q.shape[:2]
    qg = q.reshape(B_, Q, ATT_KV_HEADS, ATT_HEADS // ATT_KV_HEADS, ATT_HEAD_DIM)
    s = jnp.einsum('bqhgd,bqkhd->bqhgk', qg, k_sel).astype(jnp.float32) * (ATT_HEAD_DIM ** -0.5)
    s = jnp.where(valid[:, :, None, None, :], s, -jnp.inf)
    p = jax.nn.softmax(s, axis=-1).astype(v_sel.dtype)
    o = jnp.einsum('bqhgk,bqkhd->bqhgd', p, v_sel)
    return o.reshape(B_, Q, ATT_Q_DIM)


def _dsa_prompt(x, w_in, kn_g, kn_b):
    B_, T, _ = x.shape
    q, k, v, iq, ik, iw = _dsa_project_jax(x, jnp.arange(T), w_in, kn_g, kn_b)
    topk = min(TOPK_MAX, T // 4)

    def block(bi):
        t0 = bi * Q_BLOCK
        sl = lambda t: lax.dynamic_slice_in_dim(t, t0, Q_BLOCK, axis=1)
        qpos = t0 + jnp.arange(Q_BLOCK)
        idx, valid = _dsa_select(sl(iq), sl(iw), ik, qpos, topk)
        return _sparse_attend(sl(q), _gather_rows(k, idx), _gather_rows(v, idx), valid)

    o = lax.map(block, jnp.arange(T // Q_BLOCK))
    o = jnp.moveaxis(o, 0, 1).reshape(B_, T, ATT_Q_DIM)
    return o, k, v, ik


def _dsa_sample(x, cache_k, cache_v, cache_idx_k, page_table, w_in, kn_g, kn_b):
    B_, T, _ = x.shape
    page = cache_k.shape[1]
    past = page_table.shape[1] * page
    pos = past + jnp.arange(T)
    q, k, v, iq, ik, iw = _dsa_project_jax(x, pos, w_in, kn_g, kn_b)
    ik_all = jnp.concatenate([cache_idx_k[page_table].reshape(B_, past, IDX_DIM), ik], axis=1)
    idx, valid = _dsa_select(iq, iw, ik_all, pos, min(TOPK_MAX, (past + T) // 4))
    past_idx = jnp.minimum(idx, past - 1)
    phys = jnp.take_along_axis(page_table, (past_idx // page).reshape(B_, -1), axis=1).reshape(idx.shape)
    off = past_idx % page
    new_idx = jnp.clip(idx - past, 0, T - 1)
    is_new = (idx >= past)[..., None, None]
    k_sel = jnp.where(is_new, _gather_rows(k, new_idx), cache_k[phys, off])
    v_sel = jnp.where(is_new, _gather_rows(v, new_idx), cache_v[phys, off])
    o = _sparse_attend(q, k_sel, v_sel, valid)
    return o, k, v, ik


def _rwkv7_mixer(x, shift, wkv, mu, w_r, w_k, w_v, w0, w1, w2, a0, a1, a2, g1, g2,
                 k_k, k_a, r_k, gn_g, gn_b):
    B_, T, _ = x.shape
    x_prev = jnp.concatenate([shift[:, None, :], x[:, :-1]], axis=1)
    xm = x[None] + (x_prev - x)[None] * mu[:, None, None, :]
    xr, xw, xk, xv, xa, xg = xm
    r = xr @ w_r
    w_log = -jax.nn.softplus(-(w0 + jnp.tanh(xw @ w1) @ w2)) - 0.5
    k = xk @ w_k
    v = xv @ w_v
    a = jax.nn.sigmoid(a0 + (xa @ a1) @ a2)
    g = jax.nn.sigmoid(xg @ g1) @ g2
    heads = lambda t: t.reshape(B_, T, RW_HEADS, RW_HEAD)
    kk = heads(k * k_k).astype(jnp.float32)
    kk = kk / jnp.maximum(jnp.sqrt(jnp.sum(kk * kk, -1, keepdims=True)), 1e-12)
    k = k * (1.0 + (a - 1.0) * k_a)
    decay = jnp.exp(-jnp.exp(w_log.astype(jnp.float32)))
    r, k, v, a, decay = heads(r), heads(k), heads(v), heads(a), heads(decay)
    seq = tuple(jnp.moveaxis(t.astype(jnp.float32), 1, 0) for t in (r, decay, k, v, kk, kk * a))

    def step(s, inp):
        r_t, d_t, k_t, v_t, kk_t, b_t = inp
        sa = jnp.einsum('bhij,bhj->bhi', s, kk_t)
        s = s * d_t[:, :, None, :] - sa[..., None] * b_t[:, :, None, :] + v_t[..., None] * k_t[:, :, None, :]
        return s, jnp.einsum('bhij,bhj->bhi', s, r_t)

    s_last, y = lax.scan(step, wkv.astype(jnp.float32), seq)
    y = jnp.moveaxis(y, 0, 1)
    mu_y = jnp.mean(y, -1, keepdims=True)
    var_y = jnp.mean(jnp.square(y - mu_y), -1, keepdims=True)
    yn = ((y - mu_y) * lax.rsqrt(var_y + RW_GN_EPS)).reshape(B_, T, D_MODEL).astype(x.dtype) * gn_g + gn_b
    bonus = (jnp.sum(r * k * r_k, -1, keepdims=True) * v).reshape(B_, T, D_MODEL)
    return (yn + bonus) * g, x[:, -1], s_last.astype(wkv.dtype)


def kernel(x_prompt, x_sample, state_ssm_conv, state_ssm, cache_k, cache_v, cache_idx_k, state_rwkv_shift, state_rwkv_wkv, page_table, p_prompt, p_sample, ln_g, ln_b, ffn_w_up, ffn_w_down, ple_w_p, ple_w_g, ple_b_g, gm_w_in, gm_ln_g, gm_ln_b, gm_ws, gm_bs, gm_w_out, ssm_w_in, ssm_conv_w, ssm_conv_b, ssm_dt_bias, ssm_a_log, ssm_d, ssm_norm_g, ssm_w_out, att_w_in, att_kn_g, att_kn_b, att_w_out, rw_mu, rw_w_r, rw_w_k, rw_w_v, rw_w_o, rw_w0, rw_w1, rw_w2, rw_a0, rw_a1, rw_a2, rw_g1, rw_g2, rw_k_k, rw_k_a, rw_r_k, rw_gn_g, rw_gn_b):
    bp, tp, _ = x_prompt.shape
    bs_, ts, _ = x_sample.shape
    bf = lambda w: w.astype(jnp.bfloat16)
    w_up_bf, w_down_bf = bf(ffn_w_up), bf(ffn_w_down)
    ple_wp_bf, ple_wg_bf = bf(ple_w_p), bf(ple_w_g)

    yp = x_prompt.reshape(bp * tp, D_MODEL)
    ys = x_sample.reshape(bs_ * ts, D_MODEL)
    r3p = lambda t: t.reshape(bp, tp, -1)
    r3s = lambda t: t.reshape(bs_, ts, -1)
    f2 = lambda t: t.reshape(-1, t.shape[-1])

    for i in range(DEPTH):
        yp = _ffn_sub(yp, w_up_bf[i, 0], w_down_bf[i, 0], ln_g[i, 0], ln_b[i, 0])
        ys = _ffn_sub(ys, w_up_bf[i, 0], w_down_bf[i, 0], ln_g[i, 0], ln_b[i, 0])
        m = i % N_MIXERS
        if m == 0:
            gm_args = (gm_w_in, gm_ln_g, gm_ln_b, gm_ws, gm_bs, gm_w_out, ln_g[i, 1], ln_b[i, 1])
            yp, = _gmlp_block(yp, tp, *gm_args, False)
            ys, gm_v_s = _gmlp_block(ys, ts, *gm_args, True)
            gm_v_s = r3s(gm_v_s)
        elif m == 1:
            ssm_args = (ssm_w_in, ssm_conv_w, ssm_conv_b, ssm_dt_bias, ssm_a_log, ssm_d, ssm_norm_g)
            hp, conv_p, ssm_p = _mamba2_mixer_pallas(
                r3p(yp), jnp.zeros((bp, SSM_CONV - 1, SSM_CONV_DIM), yp.dtype),
                jnp.zeros((bp, SSM_HEADS, SSM_HEAD_DIM, SSM_STATE), yp.dtype), *ssm_args)
            hs, conv_s, ssm_s = _mamba2_mixer_pallas(r3s(ys), state_ssm_conv, state_ssm, *ssm_args)
            w_out = bf(ssm_w_out)
        elif m == 2:
            hp, k_p, v_p, ik_p = _dsa_prompt_pallas(r3p(yp), att_w_in, att_kn_g, att_kn_b)
            hs, k_s, v_s, ik_s = _dsa_sample_pallas(r3s(ys), cache_k, cache_v, cache_idx_k, page_table,
                                                    att_w_in, att_kn_g, att_kn_b)
            w_out = bf(att_w_out)
        else:
            rw_args = (rw_mu, rw_w_r, rw_w_k, rw_w_v, rw_w0, rw_w1, rw_w2, rw_a0, rw_a1, rw_a2,
                       rw_g1, rw_g2, rw_k_k, rw_k_a, rw_r_k, rw_gn_g, rw_gn_b)
            hp, gate_p, sh_p, wkv_p = _rwkv7_mixer_pallas(
                r3p(yp), jnp.zeros((bp, D_MODEL), yp.dtype),
                jnp.zeros((bp, RW_HEADS, RW_HEAD, RW_HEAD), yp.dtype), *rw_args)
            hs, gate_s, sh_s, wkv_s = _rwkv7_mixer_pallas(r3s(ys), state_rwkv_shift, state_rwkv_wkv, *rw_args)
            w_out = bf(rw_w_o)
        if m == 3:
            yp = _proj_gate_post_norm(yp, hp, gate_p, w_out, ln_g[i, 1], ln_b[i, 1])
            ys = _proj_gate_post_norm(ys, hs, gate_s, w_out, ln_g[i, 1], ln_b[i, 1])
        elif m != 0:
            yp = _proj_post_norm(yp, f2(hp), w_out, ln_g[i, 1], ln_b[i, 1])
            ys = _proj_post_norm(ys, f2(hs), w_out, ln_g[i, 1], ln_b[i, 1])
        yp = _ffn_sub(yp, w_up_bf[i, 1], w_down_bf[i, 1], ln_g[i, 2], ln_b[i, 2])
        ys = _ffn_sub(ys, w_up_bf[i, 1], w_down_bf[i, 1], ln_g[i, 2], ln_b[i, 2])
        yp = _ple_add(yp, f2(p_prompt[i]), ple_wp_bf[i], ple_wg_bf[i], ple_b_g[i])
        ys = _ple_add(ys, f2(p_sample[i]), ple_wp_bf[i], ple_wg_bf[i], ple_b_g[i])

    return (r3p(yp), r3s(ys), gm_v_s, conv_p, ssm_p, conv_s, ssm_s, k_p, v_p, ik_p, k_s, v_s, ik_s,
            sh_p, wkv_p, sh_s, wkv_s)
```

```python
import functools
import math

import jax
import jax.numpy as jnp
from jax import lax
from jax.experimental import pallas as pl
from jax.experimental.pallas import tpu as pltpu

D_MODEL = 1024
DEPTH = 4
N_MIXERS = 4
PLE_DIM = 256
D_FF = 2816
ALPHA = (2 * DEPTH) ** 0.25
LN_EPS = 1e-5

CHUNK = 128
GM_WIDTH = 2 * D_MODEL
GM_GROUPS = 8
GM_GROUP_DIM = GM_WIDTH // GM_GROUPS

SSM_D_INNER = 2 * D_MODEL
SSM_HEAD_DIM = 64
SSM_HEADS = SSM_D_INNER // SSM_HEAD_DIM
SSM_GROUPS = 4
SSM_HPG = SSM_HEADS // SSM_GROUPS
SSM_STATE = 128
SSM_CONV = 4
SSM_CONV_DIM = SSM_D_INNER + 2 * SSM_GROUPS * SSM_STATE
SSM_CHUNK = 128

ATT_HEADS = 16
ATT_KV_HEADS = 4
ATT_HEAD_DIM = D_MODEL // ATT_HEADS
ROPE_DIM = ATT_HEAD_DIM // 4
ROPE_THETA = 500000.0
IDX_HEADS = 8
IDX_DIM = 64
IDX_ROPE_DIM = IDX_DIM // 4
TOPK_MAX = 256
Q_BLOCK = 128
ATT_Q_DIM = ATT_HEADS * ATT_HEAD_DIM
ATT_KV_DIM = ATT_KV_HEADS * ATT_HEAD_DIM
ATT_IN_SPLITS = (ATT_Q_DIM, ATT_Q_DIM + ATT_KV_DIM, ATT_Q_DIM + 2 * ATT_KV_DIM,
                 ATT_Q_DIM + 2 * ATT_KV_DIM + IDX_HEADS * IDX_DIM,
                 ATT_Q_DIM + 2 * ATT_KV_DIM + IDX_HEADS * IDX_DIM + IDX_DIM)

RW_HEAD = 64
RW_HEADS = D_MODEL // RW_HEAD
RW_GN_EPS = 64e-5

V7X_VMEM_LIMIT_BYTES = 52 * 1024 * 1024
FF_TILE = D_FF // 2
ROW_TILE = 512


def _row_tile(m):
    return ROW_TILE if m % ROW_TILE == 0 else m


def _ln_rows(y, g, b):
    mu = jnp.mean(y, axis=-1, keepdims=True)
    yc = y - mu
    var = jnp.mean(yc * yc, axis=-1, keepdims=True)
    return yc * lax.rsqrt(var + LN_EPS) * g + b


def _ffn_kernel(x_ref, wa_ref, wb_ref, wd_ref, g_ref, b_ref, o_ref, acc_ref):
    f = pl.program_id(1)

    @pl.when(f == 0)
    def _():
        acc_ref[...] = jnp.zeros_like(acc_ref)

    xb = x_ref[...].astype(jnp.bfloat16)
    a = jnp.dot(xb, wa_ref[...], preferred_element_type=jnp.float32)
    b = jnp.dot(xb, wb_ref[...], preferred_element_type=jnp.float32)
    h = (a * jax.nn.sigmoid(a) * b).astype(jnp.bfloat16)
    acc_ref[...] += jnp.dot(h, wd_ref[...], preferred_element_type=jnp.float32)

    @pl.when(f == pl.num_programs(1) - 1)
    def _():
        y = ALPHA * x_ref[...] + 0.5 * acc_ref[...]
        o_ref[...] = _ln_rows(y, g_ref[...], b_ref[...])


def _ffn_sub(x2d, w_up, w_down, g, b):
    m = x2d.shape[0]
    tm = _row_tile(m)
    nf = D_FF // FF_TILE
    return pl.pallas_call(
        _ffn_kernel,
        grid=(m // tm, nf),
        in_specs=[
            pl.BlockSpec((tm, D_MODEL), lambda i, f: (i, 0)),
            pl.BlockSpec((D_MODEL, FF_TILE), lambda i, f: (0, f)),
            pl.BlockSpec((D_MODEL, FF_TILE), lambda i, f: (0, f + nf)),
            pl.BlockSpec((FF_TILE, D_MODEL), lambda i, f: (f, 0)),
            pl.BlockSpec((1, D_MODEL), lambda i, f: (0, 0)),
            pl.BlockSpec((1, D_MODEL), lambda i, f: (0, 0)),
        ],
        out_specs=pl.BlockSpec((tm, D_MODEL), lambda i, f: (i, 0)),
        out_shape=jax.ShapeDtypeStruct((m, D_MODEL), jnp.float32),
        scratch_shapes=[pltpu.VMEM((tm, D_MODEL), jnp.float32)],
        compiler_params=pltpu.CompilerParams(
            dimension_semantics=("parallel", "arbitrary"),
            vmem_limit_bytes=V7X_VMEM_LIMIT_BYTES),
        name="ffn_sub",
    )(x2d, w_up, w_up, w_down, g.reshape(1, D_MODEL), b.reshape(1, D_MODEL))


def _ple_kernel(x_ref, p_ref, wp_ref, wg_ref, bg_ref, o_ref):
    x = x_ref[...]
    gate = jax.nn.sigmoid(
        jnp.dot(x.astype(jnp.bfloat16), wg_ref[...], preferred_element_type=jnp.float32) + bg_ref[...])
    emb = jnp.dot(p_ref[...].astype(jnp.bfloat16), wp_ref[...], preferred_element_type=jnp.float32)
    o_ref[...] = x + gate * emb


def _ple_add(x2d, p2d, w_p, w_g, b_g):
    m = x2d.shape[0]
    tm = _row_tile(m)
    return pl.pallas_call(
        _ple_kernel,
        grid=(m // tm,),
        in_specs=[
            pl.BlockSpec((tm, D_MODEL), lambda i: (i, 0)),
            pl.BlockSpec((tm, PLE_DIM), lambda i: (i, 0)),
            pl.BlockSpec((PLE_DIM, D_MODEL), lambda i: (0, 0)),
            pl.BlockSpec((D_MODEL, D_MODEL), lambda i: (0, 0)),
            pl.BlockSpec((1, D_MODEL), lambda i: (0, 0)),
        ],
        out_specs=pl.BlockSpec((tm, D_MODEL), lambda i: (i, 0)),
        out_shape=jax.ShapeDtypeStruct((m, D_MODEL), jnp.float32),
        compiler_params=pltpu.CompilerParams(
            dimension_semantics=("parallel",),
            vmem_limit_bytes=V7X_VMEM_LIMIT_BYTES),
        name="ple_add",
    )(x2d, p2d, w_p, w_g, b_g.reshape(1, D_MODEL))


def _proj_ln_kernel(x_ref, h_ref, w_ref, g_ref, b_ref, o_ref):
    y = ALPHA * x_ref[...] + jnp.dot(h_ref[...].astype(jnp.bfloat16), w_ref[...],
                                     preferred_element_type=jnp.float32)
    o_ref[...] = _ln_rows(y, g_ref[...], b_ref[...])


def _proj_post_norm(x2d, h2d, w_out, g, b):
    m = x2d.shape[0]
    k = h2d.shape[1]
    tm = _row_tile(m)
    return pl.pallas_call(
        _proj_ln_kernel,
        grid=(m // tm,),
        in_specs=[
            pl.BlockSpec((tm, D_MODEL), lambda i: (i, 0)),
            pl.BlockSpec((tm, k), lambda i: (i, 0)),
            pl.BlockSpec((k, D_MODEL), lambda i: (0, 0)),
            pl.BlockSpec((1, D_MODEL), lambda i: (0, 0)),
            pl.BlockSpec((1, D_MODEL), lambda i: (0, 0)),
        ],
        out_specs=pl.BlockSpec((tm, D_MODEL), lambda i: (i, 0)),
        out_shape=jax.ShapeDtypeStruct((m, D_MODEL), jnp.float32),
        compiler_params=pltpu.CompilerParams(
            dimension_semantics=("parallel",),
            vmem_limit_bytes=V7X_VMEM_LIMIT_BYTES),
        name="proj_post_norm",
    )(x2d, h2d, w_out, g.reshape(1, D_MODEL), b.reshape(1, D_MODEL))


MXU_DTYPE = jnp.bfloat16
KEY_GROUP = 512
INT32_MIN = -2 ** 31
MASK_NEG = -1e30


def _rope_lane_tables(pos, rot_dim, head_dim):
    half = rot_dim // 2
    inv = ROPE_THETA ** (-jnp.arange(half, dtype=jnp.float32) / half)
    ang = pos.astype(jnp.float32)[:, None] * inv[None, :]
    cos, sin = jnp.cos(ang), jnp.sin(ang)
    n = pos.shape[0]
    rest = head_dim - rot_dim
    c = jnp.concatenate([cos, cos, jnp.ones((n, rest), jnp.float32)], axis=1)
    s1 = jnp.concatenate([-sin, jnp.zeros((n, half + rest), jnp.float32)], axis=1)
    s2 = jnp.concatenate([jnp.zeros((n, half), jnp.float32), sin, jnp.zeros((n, rest), jnp.float32)], axis=1)
    reps = 128 // head_dim
    tile = lambda t: jnp.tile(t, (1, reps))
    return tile(c), tile(s1), tile(s2), cos.T, sin.T


def _rope_lanes(t, c, s1, s2, half):
    n = t.shape[1]
    reps = n // 128
    tl = lambda a: jnp.concatenate([a] * reps, axis=1)
    return t * tl(c) + pltpu.roll(t, n - half, 1) * tl(s1) + pltpu.roll(t, half, 1) * tl(s2)


def _rope_rows(t, cT, sT, head_dim, half):
    pieces = []
    for h in range(t.shape[0] // head_dim):
        x1 = t[h * head_dim:h * head_dim + half]
        x2 = t[h * head_dim + half:h * head_dim + 2 * half]
        pieces += [x1 * cT - x2 * sT, x2 * cT + x1 * sT, t[h * head_dim + 2 * half:(h + 1) * head_dim]]
    return jnp.concatenate(pieces, axis=0)


def _dsa_proj_kernel(x_ref, wq_ref, wiq_ref, wv_ref, wvx_ref, wiw_ref, wkT_ref, wikT_ref,
                     c_ref, s1_ref, s2_ref, cT_ref, sT_ref, kng_ref, knb_ref, one_ref,
                     q_ref, iq_ref, v_ref, vx_ref, iw_ref, kT_ref, kTb_ref, ikT_ref, ikTb_ref):
    xb = x_ref[...].astype(MXU_DTYPE)
    c, s1, s2 = c_ref[...], s1_ref[...], s2_ref[...]
    cT, sT = cT_ref[...], sT_ref[...]
    dot = lambda a, b: jnp.dot(a, b, preferred_element_type=jnp.float32)
    dot_t = lambda w, a: lax.dot_general(w, a, (((1,), (1,)), ((), ())), preferred_element_type=jnp.float32)

    q = _rope_lanes(dot(xb, wq_ref[...]), c, s1, s2, ROPE_DIM // 2)
    q_ref[...] = (q * (ATT_HEAD_DIM ** -0.5)).astype(q_ref.dtype)
    iq = _rope_lanes(dot(xb, wiq_ref[...]), c, s1, s2, IDX_ROPE_DIM // 2)
    iq_ref[...] = iq.astype(iq_ref.dtype)
    v_ref[...] = dot(xb, wv_ref[...])
    vx_ref[...] = (dot(xb, wvx_ref[...]) + one_ref[...]).astype(vx_ref.dtype)
    iw_ref[...] = dot(xb, wiw_ref[...]) * (IDX_HEADS ** -0.5 * IDX_DIM ** -0.5)

    kT = _rope_rows(dot_t(wkT_ref[...], xb), cT, sT, ATT_HEAD_DIM, ROPE_DIM // 2)
    kT_ref[0, 0] = kT
    kTb_ref[0, 0] = kT.astype(kTb_ref.dtype)
    ikT = dot_t(wikT_ref[...], xb)
    mu = jnp.mean(ikT, axis=0, keepdims=True)
    ikc = ikT - mu
    var = jnp.mean(ikc * ikc, axis=0, keepdims=True)
    ikT = ikc * lax.rsqrt(var + LN_EPS) * kng_ref[...] + knb_ref[...]
    ikT = _rope_rows(ikT, cT, sT, IDX_DIM, IDX_ROPE_DIM // 2)
    ikT_ref[0, 0] = ikT
    ikTb_ref[0, 0] = ikT.astype(ikTb_ref.dtype)


def _dsa_project(x3d, pos, w_in, kn_g, kn_b):
    b_, t_, _ = x3d.shape
    tk = KEY_GROUP if t_ % KEY_GROUP == 0 else t_
    ng = t_ // tk
    m = b_ * t_
    w_q, w_k, w_v, w_iq, w_ik, w_iw = jnp.split(w_in, list(ATT_IN_SPLITS), axis=1)
    cast = lambda w: w.astype(MXU_DTYPE)
    w_vx = jnp.pad(w_v.reshape(D_MODEL, ATT_KV_HEADS, ATT_HEAD_DIM),
                   ((0, 0), (0, 0), (0, 128 - ATT_HEAD_DIM))).reshape(D_MODEL, ATT_KV_HEADS * 128)
    one_col = jnp.tile((jnp.arange(128) == ATT_HEAD_DIM).astype(jnp.float32), ATT_KV_HEADS)[None, :]
    w_iw_pad = jnp.pad(w_iw, ((0, 0), (0, 128 - IDX_HEADS)))
    c, s1, s2, cT, sT = _rope_lane_tables(pos, ROPE_DIM, ATT_HEAD_DIM)
    full = lambda shape: pl.BlockSpec(shape, lambda b, i: (0,) * len(shape))
    rows = lambda n: pl.BlockSpec((tk, n), lambda b, i: (b * ng + i, 0))
    ptab = lambda n: pl.BlockSpec((tk, n), lambda b, i: (i, 0))
    grp = lambda n: pl.BlockSpec((1, 1, n, tk), lambda b, i: (b, i, 0, 0))
    sds = jax.ShapeDtypeStruct
    return pl.pallas_call(
        _dsa_proj_kernel,
        grid=(b_, ng),
        in_specs=[rows(D_MODEL), full((D_MODEL, ATT_Q_DIM)), full((D_MODEL, IDX_HEADS * IDX_DIM)),
                  full((D_MODEL, ATT_KV_DIM)), full((D_MODEL, ATT_KV_HEADS * 128)), full((D_MODEL, 128)),
                  full((ATT_KV_DIM, D_MODEL)), full((IDX_DIM, D_MODEL)),
                  ptab(128), ptab(128), ptab(128),
                  pl.BlockSpec((ROPE_DIM // 2, tk), lambda b, i: (0, i)),
                  pl.BlockSpec((ROPE_DIM // 2, tk), lambda b, i: (0, i)),
                  full((IDX_DIM, 1)), full((IDX_DIM, 1)), full((1, ATT_KV_HEADS * 128))],
        out_specs=[rows(ATT_Q_DIM), rows(IDX_HEADS * IDX_DIM), rows(ATT_KV_DIM), rows(ATT_KV_HEADS * 128),
                   rows(128), grp(ATT_KV_DIM), grp(ATT_KV_DIM), grp(IDX_DIM), grp(IDX_DIM)],
        out_shape=[sds((m, ATT_Q_DIM), MXU_DTYPE), sds((m, IDX_HEADS * IDX_DIM), MXU_DTYPE),
                   sds((m, ATT_KV_DIM), jnp.float32), sds((m, ATT_KV_HEADS * 128), MXU_DTYPE),
                   sds((m, 128), jnp.float32),
                   sds((b_, ng, ATT_KV_DIM, tk), jnp.float32), sds((b_, ng, ATT_KV_DIM, tk), MXU_DTYPE),
                   sds((b_, ng, IDX_DIM, tk), jnp.float32), sds((b_, ng, IDX_DIM, tk), MXU_DTYPE)],
        compiler_params=pltpu.CompilerParams(
            dimension_semantics=("parallel", "parallel"),
            vmem_limit_bytes=V7X_VMEM_LIMIT_BYTES),
        name="dsa_project",
    )(x3d.reshape(m, D_MODEL), cast(w_q), cast(w_iq), cast(w_v), cast(w_vx), cast(w_iw_pad),
      cast(w_k.T), cast(w_ik.T), c, s1, s2, cT, sT, kn_g.reshape(IDX_DIM, 1), kn_b.reshape(IDX_DIM, 1), one_col)


def _untranspose_groups(tg):
    b_, g_, r_, tk = tg.shape
    return jnp.transpose(tg, (0, 1, 3, 2)).reshape(b_, g_ * tk, r_)


def _dsa_attend_kernel(iq_ref, iw_ref, ikT_ref, q_ref, kT_ref, vx_ref, o_ref, key_ref, m_ref, acc_ref, *,
                       topk, col_bits):
    j = pl.program_id(1)
    tq = iq_ref.shape[0]
    tk = key_ref.shape[2]
    n_groups = (j * tq + tq + tk - 1) // tk
    row = j * tq + lax.broadcasted_iota(jnp.int32, (tq, tk), 0)
    col0 = lax.broadcasted_iota(jnp.int32, (tq, tk), 1)
    dot = lambda a, b: jnp.dot(a, b, preferred_element_type=jnp.float32)

    def score_body(g, carry):
        ikT = ikT_ref[0, g]
        sc = jnp.zeros((tq, tk), jnp.float32)
        for h in range(IDX_HEADS):
            s = dot(iq_ref[:, h * IDX_DIM:(h + 1) * IDX_DIM], ikT)
            sc = sc + iw_ref[:, h:h + 1] * jnp.maximum(s, 0.0)
        bits = pltpu.bitcast(sc, jnp.int32)
        key = jnp.where(bits >= 0, bits, bits ^ jnp.int32(0x7FFFFFFF))
        key_ref[g] = jnp.where(col0 + g * tk <= row, key, jnp.int32(INT32_MIN))
        return carry

    lax.fori_loop(0, n_groups, score_body, 0)

    def bit_body(i, thr):
        cand = thr ^ lax.shift_left(jnp.int32(1), jnp.int32(31) - i)

        def count_body(g, cnt):
            hit = jnp.where(key_ref[g] >= cand, 1.0, 0.0)
            for l in range(tk // 128):
                cnt = cnt + hit[:, l * 128:(l + 1) * 128]
            return cnt

        cnt = lax.fori_loop(0, n_groups, count_body, jnp.zeros((tq, 128), jnp.float32))
        total = jnp.sum(cnt, axis=1, keepdims=True)
        return jnp.where(total >= float(topk), cand, thr)

    thr = lax.fori_loop(0, 32, bit_body, jnp.full((tq, 1), INT32_MIN, jnp.int32))

    def lane_fold(hit, cnt):
        for l in range(tk // 128):
            cnt = cnt + hit[:, l * 128:(l + 1) * 128]
        return cnt

    def above_body(g, cnt):
        return lane_fold(jnp.where(key_ref[g] > thr, 1.0, 0.0), cnt)

    n_above = jnp.sum(lax.fori_loop(0, n_groups, above_body, jnp.zeros((tq, 128), jnp.float32)),
                      axis=1, keepdims=True)
    need = float(topk) - n_above

    def col_body(i, last):
        cand = last | lax.shift_left(jnp.int32(1), jnp.int32(col_bits - 1) - i)

        def tie_body(g, cnt):
            hit = jnp.where((key_ref[g] == thr) & (col0 + g * tk < cand), 1.0, 0.0)
            return lane_fold(hit, cnt)

        ties = jnp.sum(lax.fori_loop(0, n_groups, tie_body, jnp.zeros((tq, 128), jnp.float32)),
                       axis=1, keepdims=True)
        return jnp.where(ties < need, cand, last)

    last_tie = lax.fori_loop(0, col_bits, col_body, jnp.zeros((tq, 1), jnp.int32))

    m_ref[...] = jnp.full(m_ref.shape, MASK_NEG, jnp.float32)
    acc_ref[...] = jnp.zeros(acc_ref.shape, jnp.float32)
    gsz = ATT_HEADS // ATT_KV_HEADS

    def attend_body(g, carry):
        key = key_ref[g]
        col = col0 + g * tk
        keep = (key > thr) | ((key == thr) & (col <= last_tie))
        bias = jnp.where(keep & (col <= row), 0.0, MASK_NEG)
        start = pl.multiple_of(g * tk, tk)
        for h in range(ATT_HEADS):
            kv = h // gsz
            s = dot(q_ref[:, h * ATT_HEAD_DIM:(h + 1) * ATT_HEAD_DIM],
                    kT_ref[0, g, kv * ATT_HEAD_DIM:(kv + 1) * ATT_HEAD_DIM, :]) + bias
            m_old = m_ref[h]
            m_new = jnp.maximum(m_old, jnp.max(s, axis=1, keepdims=True))
            p = jnp.exp(s - m_new).astype(vx_ref.dtype)
            pv = dot(p, vx_ref[0, pl.ds(start, tk), kv * 128:(kv + 1) * 128])
            acc_ref[h] = jnp.exp(m_old - m_new) * acc_ref[h] + pv
            m_ref[h] = m_new
        return carry

    lax.fori_loop(0, n_groups, attend_body, 0)

    for h in range(ATT_HEADS):
        a = acc_ref[h]
        o_ref[:, h * ATT_HEAD_DIM:(h + 1) * ATT_HEAD_DIM] = (
            a[:, :ATT_HEAD_DIM] / a[:, ATT_HEAD_DIM:ATT_HEAD_DIM + 1]).astype(o_ref.dtype)


def _dsa_attend(b_, t_, q, iq, iw, ikTb, kTb, vx):
    ng, tk = kTb.shape[1], kTb.shape[3]
    tq = Q_BLOCK
    nq = t_ // tq
    rows = lambda n: pl.BlockSpec((tq, n), lambda b, j: (b * nq + j, 0))
    return pl.pallas_call(
        functools.partial(_dsa_attend_kernel, topk=min(TOPK_MAX, t_ // 4), col_bits=max(1, (t_ - 1).bit_length())),
        grid=(b_, nq),
        in_specs=[rows(IDX_HEADS * IDX_DIM), rows(128),
                  pl.BlockSpec((1, ng, IDX_DIM, tk), lambda b, j: (b, 0, 0, 0)),
                  rows(ATT_Q_DIM),
                  pl.BlockSpec((1, ng, ATT_KV_DIM, tk), lambda b, j: (b, 0, 0, 0)),
                  pl.BlockSpec((1, t_, ATT_KV_HEADS * 128), lambda b, j: (b, 0, 0))],
        out_specs=rows(ATT_Q_DIM),
        out_shape=jax.ShapeDtypeStruct((b_ * t_, ATT_Q_DIM), MXU_DTYPE),
        scratch_shapes=[pltpu.VMEM((ng, tq, tk), jnp.int32),
                        pltpu.VMEM((ATT_HEADS, tq, 1), jnp.float32),
                        pltpu.VMEM((ATT_HEADS, tq, 128), jnp.float32)],
        compiler_params=pltpu.CompilerParams(
            dimension_semantics=("parallel", "arbitrary"),
            vmem_limit_bytes=V7X_VMEM_LIMIT_BYTES),
        name="dsa_attend",
    )(iq, iw, ikTb, q, kTb, vx.reshape(b_, t_, ATT_KV_HEADS * 128))


def _dsa_proj_q_lanes_kernel(x_ref, wqT_ref, wiqT_ref, wiwT_ref, wk_ref, wv_ref, wvxT_ref, wik_ref,
                             c_ref, s1_ref, s2_ref, cT_ref, sT_ref, kng_ref, knb_ref, onerow_ref,
                             qT_ref, iqT_ref, iwT_ref, k_ref, khd_ref, v_ref, vxT_ref, ik_ref, ikb_ref):
    xb = x_ref[...].astype(MXU_DTYPE)
    tm = xb.shape[0]
    c, s1, s2 = c_ref[...], s1_ref[...], s2_ref[...]
    cT, sT = cT_ref[...], sT_ref[...]
    dot = lambda a, b: jnp.dot(a, b, preferred_element_type=jnp.float32)
    dot_t = lambda w, a: lax.dot_general(w, a, (((1,), (1,)), ((), ())), preferred_element_type=jnp.float32)

    qT = _rope_rows(dot_t(wqT_ref[...], xb), cT, sT, ATT_HEAD_DIM, ROPE_DIM // 2) * (ATT_HEAD_DIM ** -0.5)
    iqT = _rope_rows(dot_t(wiqT_ref[...], xb), cT, sT, IDX_DIM, IDX_ROPE_DIM // 2)
    iwT = dot_t(wiwT_ref[...], xb) * (IDX_HEADS ** -0.5 * IDX_DIM ** -0.5)
    for t in range(tm // Q_BLOCK):
        lanes = slice(t * Q_BLOCK, (t + 1) * Q_BLOCK)
        qT_ref[0, t] = qT[:, lanes].astype(qT_ref.dtype)
        iqT_ref[0, t] = iqT[:, lanes].astype(iqT_ref.dtype)
        iwT_ref[0, t] = iwT[:, lanes]

    k = _rope_lanes(dot(xb, wk_ref[...]), c, s1, s2, ROPE_DIM // 2)
    k_ref[...] = k
    for g in range(ATT_KV_HEADS):
        khd_ref[g] = k[:, g * ATT_HEAD_DIM:(g + 1) * ATT_HEAD_DIM].astype(khd_ref.dtype)
    v_ref[...] = dot(xb, wv_ref[...])
    vxT_ref[0, 0] = (dot_t(wvxT_ref[...], xb) + onerow_ref[...]).astype(vxT_ref.dtype)

    ik = dot(xb, wik_ref[...])
    real = lax.broadcasted_iota(jnp.int32, ik.shape, 1) < IDX_DIM
    mu = jnp.sum(ik, axis=-1, keepdims=True) * (1.0 / IDX_DIM)
    ikc = jnp.where(real, ik - mu, 0.0)
    var = jnp.sum(ikc * ikc, axis=-1, keepdims=True) * (1.0 / IDX_DIM)
    ikn = _rope_lanes(ikc * lax.rsqrt(var + LN_EPS) * kng_ref[...] + knb_ref[...], c, s1, s2, IDX_ROPE_DIM // 2)
    ik_ref[...] = ikn[:, :IDX_DIM]
    ikb_ref[...] = ikn[:, :IDX_DIM].astype(ikb_ref.dtype)


def _dsa_project_q_lanes(x3d, pos, w_in, kn_g, kn_b):
    b_, t_, _ = x3d.shape
    tk = KEY_GROUP
    ng = t_ // tk
    nq = tk // Q_BLOCK
    m = b_ * t_
    w_q, w_k, w_v, w_iq, w_ik, w_iw = jnp.split(w_in, list(ATT_IN_SPLITS), axis=1)
    cast = lambda w: w.astype(MXU_DTYPE)
    w_vxT = jnp.pad(w_v.T.reshape(ATT_KV_HEADS, ATT_HEAD_DIM, D_MODEL),
                    ((0, 0), (0, 128 - ATT_HEAD_DIM), (0, 0))).reshape(ATT_KV_HEADS * 128, D_MODEL)
    one_row = jnp.tile((jnp.arange(128) == ATT_HEAD_DIM).astype(jnp.float32), ATT_KV_HEADS)[:, None]
    pad_lanes = lambda a: jnp.pad(a, ((0, 0), (0, 128 - a.shape[1])))
    c, s1, s2, cT, sT = _rope_lane_tables(pos, ROPE_DIM, ATT_HEAD_DIM)
    full = lambda shape: pl.BlockSpec(shape, lambda b, i: (0,) * len(shape))
    rows = lambda n: pl.BlockSpec((tk, n), lambda b, i: (b * ng + i, 0))
    ptab = lambda n: pl.BlockSpec((tk, n), lambda b, i: (i, 0))
    qtile = lambda n: pl.BlockSpec((1, nq, n, Q_BLOCK), lambda b, i: (b, i, 0, 0))
    sds = jax.ShapeDtypeStruct
    return pl.pallas_call(
        _dsa_proj_q_lanes_kernel,
        grid=(b_, ng),
        in_specs=[rows(D_MODEL), full((ATT_Q_DIM, D_MODEL)), full((IDX_HEADS * IDX_DIM, D_MODEL)),
                  full((IDX_HEADS, D_MODEL)), full((D_MODEL, ATT_KV_DIM)), full((D_MODEL, ATT_KV_DIM)),
                  full((ATT_KV_HEADS * 128, D_MODEL)), full((D_MODEL, 128)),
                  ptab(128), ptab(128), ptab(128),
                  pl.BlockSpec((ROPE_DIM // 2, tk), lambda b, i: (0, i)),
                  pl.BlockSpec((ROPE_DIM // 2, tk), lambda b, i: (0, i)),
                  full((1, 128)), full((1, 128)), full((ATT_KV_HEADS * 128, 1))],
        out_specs=[qtile(ATT_Q_DIM), qtile(IDX_HEADS * IDX_DIM), qtile(IDX_HEADS),
                   rows(ATT_KV_DIM), pl.BlockSpec((ATT_KV_HEADS, tk, ATT_HEAD_DIM), lambda b, i: (0, b * ng + i, 0)),
                   rows(ATT_KV_DIM), pl.BlockSpec((1, 1, ATT_KV_HEADS * 128, tk), lambda b, i: (b, i, 0, 0)),
                   rows(IDX_DIM), rows(IDX_DIM)],
        out_shape=[sds((b_, t_ // Q_BLOCK, ATT_Q_DIM, Q_BLOCK), MXU_DTYPE),
                   sds((b_, t_ // Q_BLOCK, IDX_HEADS * IDX_DIM, Q_BLOCK), MXU_DTYPE),
                   sds((b_, t_ // Q_BLOCK, IDX_HEADS, Q_BLOCK), jnp.float32),
                   sds((m, ATT_KV_DIM), jnp.float32), sds((ATT_KV_HEADS, m, ATT_HEAD_DIM), MXU_DTYPE),
                   sds((m, ATT_KV_DIM), jnp.float32), sds((b_, ng, ATT_KV_HEADS * 128, tk), MXU_DTYPE),
                   sds((m, IDX_DIM), jnp.float32), sds((m, IDX_DIM), MXU_DTYPE)],
        compiler_params=pltpu.CompilerParams(
            dimension_semantics=("parallel", "parallel"),
            vmem_limit_bytes=V7X_VMEM_LIMIT_BYTES),
        name="dsa_project_q_lanes",
    )(x3d.reshape(m, D_MODEL), cast(w_q.T), cast(w_iq.T), cast(w_iw.T), cast(w_k), cast(w_v), cast(w_vxT),
      cast(pad_lanes(w_ik)), c, s1, s2, cT, sT, pad_lanes(kn_g.reshape(1, IDX_DIM)),
      pad_lanes(kn_b.reshape(1, IDX_DIM)), one_row)


def _tree_sum(parts):
    while len(parts) > 1:
        parts = [parts[i] + parts[i + 1] for i in range(0, len(parts) - 1, 2)] + (
            [parts[-1]] if len(parts) % 2 else [])
    return parts[0]


def _dsa_attend_q_lanes_kernel(iqT_ref, iwT_ref, ik_ref, qT_ref, k_ref, vxT_ref, o_ref,
                               key_ref, bias_ref, m_ref, acc_ref, *, topk, col_bits):
    j = pl.program_id(1)
    tk, tq = key_ref.shape[1], key_ref.shape[2]
    n_groups = (j * tq + tq + tk - 1) // tk
    qpos = j * tq + lax.broadcasted_iota(jnp.int32, (tk, tq), 1)
    kpos0 = lax.broadcasted_iota(jnp.int32, (tk, tq), 0)
    dot = lambda a, b: jnp.dot(a, b, preferred_element_type=jnp.float32)

    def score_body(g, carry):
        start = pl.multiple_of(g * tk, tk)
        w_iq = jnp.concatenate([iqT_ref[0, 0, h * IDX_DIM:(h + 1) * IDX_DIM, :] for h in range(IDX_HEADS)], axis=1)
        s_all = dot(ik_ref[0, pl.ds(start, tk), :], w_iq)
        sc = _tree_sum([iwT_ref[0, 0, h:h + 1, :] * jnp.maximum(s_all[:, h * tq:(h + 1) * tq], 0.0)
                        for h in range(IDX_HEADS)])
        key_ref[g] = jnp.where(kpos0 + g * tk <= qpos, _sortable_key(sc), jnp.int32(INT32_MIN))
        return carry

    lax.fori_loop(0, n_groups, score_body, 0)

    def count_keys(pred):
        def body(g, part):
            hit = jnp.where(pred(key_ref[g], kpos0 + g * tk), 1.0, 0.0)
            return part + _tree_sum([hit[r * SUBLANES:(r + 1) * SUBLANES] for r in range(tk // SUBLANES)])
        part = lax.fori_loop(0, n_groups, body, jnp.zeros((SUBLANES, tq), jnp.float32))
        return jnp.sum(part, axis=0, keepdims=True)

    def bit_body(i, thr):
        cand = thr ^ lax.shift_left(jnp.int32(1), jnp.int32(31) - i)
        return jnp.where(count_keys(lambda k, kp: k >= cand) >= float(topk), cand, thr)

    thr = lax.fori_loop(0, 32, bit_body, jnp.full((1, tq), INT32_MIN, jnp.int32))

    need = float(topk) - count_keys(lambda k, kp: k > thr)

    def pos_body(i, last):
        cand = last | lax.shift_left(jnp.int32(1), jnp.int32(col_bits - 1) - i)
        return jnp.where(count_keys(lambda k, kp: (k == thr) & (kp < cand)) < need, cand, last)

    n_tied = count_keys(lambda k, kp: k == thr)
    excess = jnp.max(jnp.where(n_tied > need, 1.0, 0.0), axis=1, keepdims=True)
    last_tie = lax.cond(excess[0, 0] > 0.0,
                        lambda: lax.fori_loop(0, col_bits, pos_body, jnp.zeros((1, tq), jnp.int32)),
                        lambda: jnp.full((1, tq), 2 ** col_bits - 1, jnp.int32))

    m_ref[...] = jnp.full(m_ref.shape, MASK_NEG, jnp.float32)
    acc_ref[...] = jnp.zeros(acc_ref.shape, jnp.float32)
    gsz = ATT_HEADS // ATT_KV_HEADS

    def attend_body(g, carry):
        start = pl.multiple_of(g * tk, tk)
        key = key_ref[g]
        kpos = kpos0 + g * tk
        keep = (key > thr) | ((key == thr) & (kpos <= last_tie))
        bias_ref[...] = jnp.where(keep & (kpos <= qpos), 0.0, MASK_NEG)
        logits = []
        for kv in range(ATT_KV_HEADS):
            w_q = jnp.concatenate([qT_ref[0, 0, (kv * gsz + i) * ATT_HEAD_DIM:(kv * gsz + i + 1) * ATT_HEAD_DIM, :]
                                   for i in range(gsz)], axis=1)
            logits.append(dot(k_ref[kv, pl.ds(start, tk), :], w_q))
        for kv in range(ATT_KV_HEADS):
            s = logits[kv] + jnp.concatenate([bias_ref[...]] * gsz, axis=1)
            m_old = m_ref[kv]
            m_new = jnp.maximum(m_old, jnp.max(s, axis=0, keepdims=True))
            p = jnp.exp(s - m_new).astype(vxT_ref.dtype)
            pv = dot(vxT_ref[0, g, kv * 128:(kv + 1) * 128, :], p)
            acc_ref[kv] = jnp.exp(m_old - m_new) * acc_ref[kv] + pv
            m_ref[kv] = m_new
        return carry

    lax.fori_loop(0, n_groups, attend_body, 0)

    for h in range(ATT_HEADS):
        a = acc_ref[h // gsz, :, (h % gsz) * tq:(h % gsz + 1) * tq]
        o = (a / a[ATT_HEAD_DIM:ATT_HEAD_DIM + 1, :]).T
        o_ref[:, h * ATT_HEAD_DIM:(h + 1) * ATT_HEAD_DIM] = o[:, :ATT_HEAD_DIM].astype(o_ref.dtype)


def _dsa_attend_q_lanes(b_, t_, qT, iqT, iwT, ikb, khd, vxT):
    ng, tk = vxT.shape[1], vxT.shape[3]
    tq = Q_BLOCK
    nq = t_ // tq
    qtile = lambda n: pl.BlockSpec((1, 1, n, tq), lambda b, j: (b, j, 0, 0))
    return pl.pallas_call(
        functools.partial(_dsa_attend_q_lanes_kernel, topk=min(TOPK_MAX, t_ // 4),
                          col_bits=max(1, (t_ - 1).bit_length())),
        grid=(b_, nq),
        in_specs=[qtile(IDX_HEADS * IDX_DIM), qtile(IDX_HEADS),
                  pl.BlockSpec((1, t_, IDX_DIM), lambda b, j: (b, 0, 0)),
                  qtile(ATT_Q_DIM),
                  pl.BlockSpec((ATT_KV_HEADS, t_, ATT_HEAD_DIM), lambda b, j: (0, b, 0)),
                  pl.BlockSpec((1, ng, ATT_KV_HEADS * 128, tk), lambda b, j: (b, 0, 0, 0))],
        out_specs=pl.BlockSpec((tq, ATT_Q_DIM), lambda b, j: (b * nq + j, 0)),
        out_shape=jax.ShapeDtypeStruct((b_ * t_, ATT_Q_DIM), MXU_DTYPE),
        scratch_shapes=[pltpu.VMEM((ng, tk, tq), jnp.int32),
                        pltpu.VMEM((tk, tq), jnp.float32),
                        pltpu.VMEM((ATT_KV_HEADS, 1, tq * (ATT_HEADS // ATT_KV_HEADS)), jnp.float32),
                        pltpu.VMEM((ATT_KV_HEADS, 128, tq * (ATT_HEADS // ATT_KV_HEADS)), jnp.float32)],
        compiler_params=pltpu.CompilerParams(
            dimension_semantics=("parallel", "arbitrary"),
            vmem_limit_bytes=V7X_VMEM_LIMIT_BYTES),
        name="dsa_attend_q_lanes",
    )(iqT, iwT, ikb.reshape(b_, t_, IDX_DIM), qT, khd, vxT)


def _dsa_prompt_pallas(x3d, w_in, kn_g, kn_b):
    b_, t_, _ = x3d.shape
    qT, iqT, iwT, k, khd, v, vxT, ik, ikb = _dsa_project_q_lanes(x3d, jnp.arange(t_), w_in, kn_g, kn_b)
    o = _dsa_attend_q_lanes(b_, t_, qT, iqT, iwT, ikb, khd, vxT)
    kv4 = lambda u: u.reshape(b_, t_, ATT_KV_HEADS, ATT_HEAD_DIM)
    return o, kv4(k), kv4(v), ik.reshape(b_, t_, IDX_DIM)


RW_ROW_TILE = 256


def _rwkv_proj_kernel(x_ref, xp_ref, mu_ref, wr_ref, wk_ref, wv_ref, w1_ref, w2_ref, a1_ref, a2_ref,
                      g1_ref, g2_ref, w0_ref, a0_ref, r_ref, d_ref, k_ref, v_ref, a_ref, g_ref):
    x = x_ref[...]
    dx = xp_ref[...] - x
    mix = lambda c: (x + dx * mu_ref[c:c + 1, :]).astype(MXU_DTYPE)
    dot = lambda a, b: jnp.dot(a.astype(MXU_DTYPE), b, preferred_element_type=jnp.float32)
    r_ref[...] = dot(mix(0), wr_ref[...])
    lora_w = dot(jnp.tanh(dot(mix(1), w1_ref[...])), w2_ref[...])
    w_log = -jax.nn.softplus(-(w0_ref[...] + lora_w)) - 0.5
    d_ref[...] = jnp.exp(-jnp.exp(w_log))
    k_ref[...] = dot(mix(2), wk_ref[...])
    v_ref[...] = dot(mix(3), wv_ref[...])
    a_ref[...] = jax.nn.sigmoid(a0_ref[...] + dot(dot(mix(4), a1_ref[...]), a2_ref[...]))
    g_ref[...] = dot(jax.nn.sigmoid(dot(mix(5), g1_ref[...])), g2_ref[...])


def _rwkv_project(x2d, xprev2d, mu, w_r, w_k, w_v, w0, w1, w2, a0, a1, a2, g1, g2):
    m = x2d.shape[0]
    tm = RW_ROW_TILE if m % RW_ROW_TILE == 0 else m
    cast = lambda w: w.astype(MXU_DTYPE)
    full = lambda a: pl.BlockSpec(a.shape, lambda i: (0,) * a.ndim)
    rows = pl.BlockSpec((tm, D_MODEL), lambda i: (i, 0))
    consts = [mu, cast(w_r), cast(w_k), cast(w_v), cast(w1), cast(w2), cast(a1), cast(a2), cast(g1), cast(g2),
              w0.reshape(1, D_MODEL), a0.reshape(1, D_MODEL)]
    return pl.pallas_call(
        _rwkv_proj_kernel,
        grid=(m // tm,),
        in_specs=[rows, rows] + [full(a) for a in consts],
        out_specs=[rows] * 6,
        out_shape=[jax.ShapeDtypeStruct((m, D_MODEL), jnp.float32)] * 6,
        compiler_params=pltpu.CompilerParams(
            dimension_semantics=("parallel",),
            vmem_limit_bytes=V7X_VMEM_LIMIT_BYTES),
        name="rwkv_project",
    )(x2d, xprev2d, *consts)


RW_LANES = 128
RW_TIME_CHUNK = 32


def _rwkv_scan_kernel(r_ref, d_ref, k_ref, v_ref, a_ref, s0_ref, kk_ref, ka_ref, rk_ref, gg_ref, gb_ref,
                      z_ref, s_out_ref, s_ref, vec_ref):
    c = pl.program_id(1)
    n = RW_HEAD

    @pl.when(c == 0)
    def _():
        s_ref[...] = s0_ref[...]

    def step(t, carry):
        r, k, v, a = r_ref[t], k_ref[t], v_ref[t], a_ref[t]
        kkr = k * kk_ref[...]
        nrm = jnp.sqrt(jnp.sum(kkr * kkr, axis=0, keepdims=True))
        kk = kkr / jnp.maximum(nrm, 1e-12)
        kmod = k * (1.0 + (a - 1.0) * ka_ref[...])
        vec_ref[0] = kk
        vec_ref[1] = d_ref[t]
        vec_ref[2] = kk * a
        vec_ref[3] = kmod
        vec_ref[4] = r
        row = lambda q, j: vec_ref[q, j:j + 1, :]
        sa = jnp.zeros((n, RW_LANES), jnp.float32)
        for j in range(n):
            sa = sa + s_ref[j] * row(0, j)
        y = jnp.zeros((n, RW_LANES), jnp.float32)
        for j in range(n):
            sn = s_ref[j] * row(1, j) - sa * row(2, j) + v * row(3, j)
            s_ref[j] = sn
            y = y + sn * row(4, j)
        mu = jnp.mean(y, axis=0, keepdims=True)
        yc = y - mu
        var = jnp.mean(yc * yc, axis=0, keepdims=True)
        bonus = jnp.sum(r * kmod * rk_ref[...], axis=0, keepdims=True)
        z_ref[t] = yc * lax.rsqrt(var + RW_GN_EPS) * gg_ref[...] + gb_ref[...] + bonus * v
        return carry

    lax.fori_loop(0, r_ref.shape[0], step, 0)

    @pl.when(c == pl.num_programs(1) - 1)
    def _():
        s_out_ref[...] = s_ref[...]


def _rwkv_scan(rT, dT, kT, vT, aT, s0T, k_k, k_a, r_k, gn_g, gn_b):
    t_, n, bh = rT.shape
    tc = RW_TIME_CHUNK if t_ % RW_TIME_CHUNK == 0 else t_
    reps = RW_LANES // RW_HEADS
    table = lambda p: jnp.tile(p.reshape(RW_HEADS, n).T, (1, reps))
    seq = pl.BlockSpec((tc, n, RW_LANES), lambda l, c: (c, 0, l))
    state = pl.BlockSpec((n, n, RW_LANES), lambda l, c: (0, 0, l))
    tab = pl.BlockSpec((n, RW_LANES), lambda l, c: (0, 0))
    return pl.pallas_call(
        _rwkv_scan_kernel,
        grid=(bh // RW_LANES, t_ // tc),
        in_specs=[seq] * 5 + [state] + [tab] * 5,
        out_specs=[seq, state],
        out_shape=[jax.ShapeDtypeStruct((t_, n, bh), jnp.float32),
                   jax.ShapeDtypeStruct((n, n, bh), jnp.float32)],
        scratch_shapes=[pltpu.VMEM((n, n, RW_LANES), jnp.float32),
                        pltpu.VMEM((5, n, RW_LANES), jnp.float32)],
        compiler_params=pltpu.CompilerParams(
            dimension_semantics=("parallel", "arbitrary"),
            vmem_limit_bytes=V7X_VMEM_LIMIT_BYTES),
        name="rwkv_scan",
    )(rT, dT, kT, vT, aT, s0T, table(k_k), table(k_a), table(r_k), table(gn_g), table(gn_b))


def _rwkv7_mixer_pallas(x3d, shift, wkv, mu, w_r, w_k, w_v, w0, w1, w2, a0, a1, a2, g1, g2,
                        k_k, k_a, r_k, gn_g, gn_b):
    b_, t_, _ = x3d.shape
    m = b_ * t_
    x_prev = jnp.concatenate([shift[:, None, :], x3d[:, :-1]], axis=1)
    r, d, k, v, a, g = _rwkv_project(x3d.reshape(m, D_MODEL), x_prev.reshape(m, D_MODEL), mu,
                                     w_r, w_k, w_v, w0, w1, w2, a0, a1, a2, g1, g2)
    to_scan = lambda u: jnp.transpose(u.reshape(b_, t_, RW_HEADS, RW_HEAD), (1, 3, 0, 2)).reshape(
        t_, RW_HEAD, b_ * RW_HEADS)
    s0T = jnp.transpose(wkv.astype(jnp.float32), (3, 2, 0, 1)).reshape(RW_HEAD, RW_HEAD, b_ * RW_HEADS)
    zT, sT = _rwkv_scan(to_scan(r), to_scan(d), to_scan(k), to_scan(v), to_scan(a), s0T,
                        k_k, k_a, r_k, gn_g, gn_b)
    z = jnp.transpose(zT.reshape(t_, RW_HEAD, b_, RW_HEADS), (2, 0, 3, 1)).reshape(m, D_MODEL)
    s_new = jnp.transpose(sT.reshape(RW_HEAD, RW_HEAD, b_, RW_HEADS), (2, 3, 1, 0)).astype(wkv.dtype)
    return z, g, x3d[:, -1], s_new


def _proj_gate_ln_kernel(x_ref, h_ref, gate_ref, w_ref, g_ref, b_ref, o_ref):
    h = (h_ref[...] * gate_ref[...]).astype(MXU_DTYPE)
    y = ALPHA * x_ref[...] + jnp.dot(h, w_ref[...], preferred_element_type=jnp.float32)
    o_ref[...] = _ln_rows(y, g_ref[...], b_ref[...])


def _proj_gate_post_norm(x2d, h2d, gate2d, w_out, g, b):
    m = x2d.shape[0]
    tm = _row_tile(m)
    rows = pl.BlockSpec((tm, D_MODEL), lambda i: (i, 0))
    vec = pl.BlockSpec((1, D_MODEL), lambda i: (0, 0))
    return pl.pallas_call(
        _proj_gate_ln_kernel,
        grid=(m // tm,),
        in_specs=[rows, rows, rows, pl.BlockSpec((D_MODEL, D_MODEL), lambda i: (0, 0)), vec, vec],
        out_specs=rows,
        out_shape=jax.ShapeDtypeStruct((m, D_MODEL), jnp.float32),
        compiler_params=pltpu.CompilerParams(
            dimension_semantics=("parallel",),
            vmem_limit_bytes=V7X_VMEM_LIMIT_BYTES),
        name="proj_gate_post_norm",
    )(x2d, h2d, gate2d, w_out, g.reshape(1, D_MODEL), b.reshape(1, D_MODEL))


GM_ROW_TILE = 256


def _gmlp_kernel(x_ref, win_ref, lng_ref, lnb_ref, mixw_ref, mixb_ref, wout_ref, g_ref, b_ref, *out_refs,
                 chunk_len, emit_v):
    x = x_ref[...]
    h = jax.nn.gelu(jnp.dot(x.astype(MXU_DTYPE), win_ref[...], preferred_element_type=jnp.float32))
    u = h[:, :GM_WIDTH]
    v = _ln_rows(h[:, GM_WIDTH:], lng_ref[...], lnb_ref[...])
    if emit_v:
        out_refs[1][...] = v
    if chunk_len == 1:
        gated = u * (v * mixw_ref[...] + mixb_ref[...])
    else:
        tm = x.shape[0]
        causal = (lax.broadcasted_iota(jnp.int32, (chunk_len, chunk_len), 0)
                  >= lax.broadcasted_iota(jnp.int32, (chunk_len, chunk_len), 1))
        vb = v.astype(MXU_DTYPE)
        cols = []
        for g in range(GM_GROUPS):
            w = jnp.where(causal, mixw_ref[g], 0.0).astype(MXU_DTYPE)
            bias = mixb_ref[:, g:g + 1]
            lanes = slice(g * GM_GROUP_DIM, (g + 1) * GM_GROUP_DIM)
            rows = [jnp.dot(w, vb[c * chunk_len:(c + 1) * chunk_len, lanes],
                            preferred_element_type=jnp.float32) + bias
                    for c in range(tm // chunk_len)]
            cols.append(jnp.concatenate(rows, axis=0))
        gated = u * jnp.concatenate(cols, axis=1)
    y = ALPHA * x + jnp.dot(gated.astype(MXU_DTYPE), wout_ref[...], preferred_element_type=jnp.float32)
    out_refs[0][...] = _ln_rows(y, g_ref[...], b_ref[...])


def _gmlp_block(x2d, seq_len, w_in, ln_g, ln_b, ws, bs, w_out, g, b, emit_v):
    m = x2d.shape[0]
    chunk_len = min(seq_len, CHUNK)
    if chunk_len == 1:
        tm = m
        mixw = jnp.repeat(ws[:, 0, 0], GM_GROUP_DIM)[None, :]
        mixb = jnp.repeat(bs[:, 0], GM_GROUP_DIM)[None, :]
    else:
        tm = GM_ROW_TILE
        mixw = ws[:, :chunk_len, :chunk_len]
        mixb = bs[:, :chunk_len].T
    full = lambda a: pl.BlockSpec(a.shape, lambda i: (0,) * a.ndim)
    rows = lambda n: pl.BlockSpec((tm, n), lambda i: (i, 0))
    consts = [w_in.astype(MXU_DTYPE), ln_g.reshape(1, GM_WIDTH), ln_b.reshape(1, GM_WIDTH), mixw, mixb,
              w_out.astype(MXU_DTYPE), g.reshape(1, D_MODEL), b.reshape(1, D_MODEL)]
    out_specs = [rows(D_MODEL)] + ([rows(GM_WIDTH)] if emit_v else [])
    out_shape = [jax.ShapeDtypeStruct((m, D_MODEL), jnp.float32)] + (
        [jax.ShapeDtypeStruct((m, GM_WIDTH), jnp.float32)] if emit_v else [])
    return pl.pallas_call(
        functools.partial(_gmlp_kernel, chunk_len=chunk_len, emit_v=emit_v),
        grid=(m // tm,),
        in_specs=[rows(D_MODEL)] + [full(a) for a in consts],
        out_specs=out_specs,
        out_shape=out_shape,
        compiler_params=pltpu.CompilerParams(
            dimension_semantics=("parallel",),
            vmem_limit_bytes=V7X_VMEM_LIMIT_BYTES),
        name="gmlp_block",
    )(x2d, *consts)


SSM_ROW_TILE = 256
SSM_BC_DIM = SSM_GROUPS * SSM_STATE
SSM_DT_LANES = 128
SUBLANES = 8


def _ssm_activate(xb, xbc, taps, wz_ref, wdt_ref, cw_ref, cb_ref, dtb_ref, z_ref, xs_ref, bm_ref, cm_ref, dt_ref):
    conv = cb_ref[...] + xbc * cw_ref[SSM_CONV - 1:SSM_CONV, :]
    for j in range(SSM_CONV - 1):
        conv = conv + taps[j] * cw_ref[j:j + 1, :]
    act = conv * jax.nn.sigmoid(conv)
    xs_ref[...] = act[:, :SSM_D_INNER]
    bm_ref[...] = act[:, SSM_D_INNER:SSM_D_INNER + SSM_BC_DIM].astype(bm_ref.dtype)
    cm_ref[...] = act[:, SSM_D_INNER + SSM_BC_DIM:].astype(cm_ref.dtype)
    z_ref[...] = jnp.dot(xb, wz_ref[...], preferred_element_type=jnp.float32)
    dt_ref[...] = jax.nn.softplus(jnp.dot(xb, wdt_ref[...], preferred_element_type=jnp.float32) + dtb_ref[...])


def _ssm_proj_seq_kernel(x_ref, halo_ref, cs_ref, wx_ref, wz_ref, wdt_ref, cw_ref, cb_ref, dtb_ref,
                         z_ref, xs_ref, bm_ref, cm_ref, dt_ref, tail_ref):
    i = pl.program_id(1)
    xb = x_ref[...].astype(MXU_DTYPE)
    xbc = jnp.dot(xb, wx_ref[...], preferred_element_type=jnp.float32)
    tm = xbc.shape[0]
    prev = jnp.dot(halo_ref[...].astype(MXU_DTYPE), wx_ref[...], preferred_element_type=jnp.float32)
    prev = jnp.where(i == 0, cs_ref[0], prev)
    row = lax.broadcasted_iota(jnp.int32, (tm, 1), 0)
    pad = jnp.zeros((tm - SUBLANES, xbc.shape[1]), jnp.float32)
    taps = []
    for j in range(SSM_CONV - 1):
        back = SSM_CONV - 1 - j
        head = jnp.concatenate([pltpu.roll(prev, back, 0), pad], axis=0)
        taps.append(jnp.where(row < back, head, pltpu.roll(xbc, back, 0)))
    _ssm_activate(xb, xbc, taps, wz_ref, wdt_ref, cw_ref, cb_ref, dtb_ref, z_ref, xs_ref, bm_ref, cm_ref, dt_ref)
    tail_ref[0] = xbc[tm - SUBLANES:, :]


def _ssm_proj_step_kernel(x_ref, st_ref, wx_ref, wz_ref, wdt_ref, cw_ref, cb_ref, dtb_ref,
                          z_ref, xs_ref, bm_ref, cm_ref, dt_ref, st_out_ref):
    xb = x_ref[...].astype(MXU_DTYPE)
    xbc = jnp.dot(xb, wx_ref[...], preferred_element_type=jnp.float32)
    taps = [st_ref[j] for j in range(SSM_CONV - 1)]
    _ssm_activate(xb, xbc, taps, wz_ref, wdt_ref, cw_ref, cb_ref, dtb_ref, z_ref, xs_ref, bm_ref, cm_ref, dt_ref)
    for j in range(SSM_CONV - 2):
        st_out_ref[j] = st_ref[j + 1]
    st_out_ref[SSM_CONV - 2] = xbc


def _ssm_project(x3d, conv_state, w_in, conv_w, conv_b, dt_bias):
    b_, t_, _ = x3d.shape
    m = b_ * t_
    w_z, w_x, w_dt = jnp.split(w_in, [SSM_D_INNER, SSM_D_INNER + SSM_CONV_DIM], axis=1)
    cast = lambda w: w.astype(MXU_DTYPE)
    consts = [cast(w_x), cast(w_z), cast(jnp.pad(w_dt, ((0, 0), (0, SSM_DT_LANES - SSM_HEADS)))),
              conv_w, conv_b.reshape(1, SSM_CONV_DIM),
              jnp.pad(dt_bias, (0, SSM_DT_LANES - SSM_HEADS)).reshape(1, SSM_DT_LANES)]
    sds = jax.ShapeDtypeStruct
    outs = [sds((m, SSM_D_INNER), jnp.float32), sds((m, SSM_D_INNER), jnp.float32),
            sds((m, SSM_BC_DIM), MXU_DTYPE), sds((m, SSM_BC_DIM), MXU_DTYPE), sds((m, SSM_DT_LANES), jnp.float32)]
    widths = [SSM_D_INNER, SSM_D_INNER, SSM_BC_DIM, SSM_BC_DIM, SSM_DT_LANES]
    params = dict(vmem_limit_bytes=V7X_VMEM_LIMIT_BYTES)
    x2d = x3d.reshape(m, D_MODEL)
    if t_ == 1:
        full = lambda a: pl.BlockSpec(a.shape, lambda i: (0,) * a.ndim)
        st = jnp.transpose(conv_state, (1, 0, 2))
        res = pl.pallas_call(
            _ssm_proj_step_kernel,
            grid=(1,),
            in_specs=[full(x2d), full(st)] + [full(a) for a in consts],
            out_specs=[pl.BlockSpec((m, w), lambda i: (0, 0)) for w in widths] + [full(st)],
            out_shape=outs + [sds(st.shape, jnp.float32)],
            compiler_params=pltpu.CompilerParams(dimension_semantics=("arbitrary",), **params),
            name="ssm_project_step",
        )(x2d, st, *consts)
        return list(res[:5]) + [jnp.transpose(res[5], (1, 0, 2))]
    tm = SSM_ROW_TILE
    nt = t_ // tm
    full = lambda a: pl.BlockSpec(a.shape, lambda b, i: (0,) * a.ndim)
    rows = lambda w: pl.BlockSpec((tm, w), lambda b, i: (b * nt + i, 0))
    halo = pl.BlockSpec((SUBLANES, D_MODEL), lambda b, i: (jnp.maximum((b * nt + i) * (tm // SUBLANES) - 1, 0), 0))
    cs8 = jnp.pad(conv_state, ((0, 0), (SUBLANES - (SSM_CONV - 1), 0), (0, 0)))
    tail = pl.BlockSpec((1, SUBLANES, SSM_CONV_DIM), lambda b, i: (b, 0, 0))
    res = pl.pallas_call(
        _ssm_proj_seq_kernel,
        grid=(b_, nt),
        in_specs=[rows(D_MODEL), halo, tail] + [full(a) for a in consts],
        out_specs=[rows(w) for w in widths] + [tail],
        out_shape=outs + [sds((b_, SUBLANES, SSM_CONV_DIM), jnp.float32)],
        compiler_params=pltpu.CompilerParams(dimension_semantics=("parallel", "arbitrary"), **params),
        name="ssm_project_seq",
    )(x2d, x2d, cs8, *consts)
    return list(res[:5]) + [res[5][:, SUBLANES - (SSM_CONV - 1):, :]]


def _ssm_gate_norm(y, xs, z, dskip, normg):
    yg = (y + xs * dskip) * (z * jax.nn.sigmoid(z))
    gw = SSM_D_INNER // SSM_GROUPS
    outs = []
    for g in range(SSM_GROUPS):
        part = yg[:, g * gw:(g + 1) * gw]
        ms = jnp.mean(part * part, axis=-1, keepdims=True)
        outs.append(part * lax.rsqrt(ms + LN_EPS))
    return jnp.concatenate(outs, axis=1) * normg


def _ssm_chunk_kernel(xs_ref, bm_ref, cm_ref, dt_ref, z_ref, aneg_ref, dskip_ref, normg_ref,
                      yg_ref, h_out_ref, h_ref, yT_ref, xe_ref):
    c = pl.program_id(1)
    l = xs_ref.shape[0]
    hd = SSM_HEAD_DIM

    @pl.when(c == 0)
    def _():
        h_ref[...] = jnp.zeros_like(h_ref)

    dot = lambda u, w: jnp.dot(u, w, preferred_element_type=jnp.float32)
    dt = dt_ref[...]
    a = dt * aneg_ref[...]
    r_i = lax.broadcasted_iota(jnp.int32, (l, l), 0)
    c_i = lax.broadcasted_iota(jnp.int32, (l, l), 1)
    tril = jnp.where(r_i >= c_i, 1.0, 0.0)
    hi = lax.Precision.HIGHEST
    acum = jnp.dot(tril, a, precision=hi, preferred_element_type=jnp.float32)
    acum_t = jnp.dot(a.T, tril.T, precision=hi, preferred_element_type=jnp.float32)
    dt_t = dt.T
    to_end_t = jnp.exp(acum_t[:, l - 1:l] - acum_t)
    from_start_t = jnp.exp(acum_t)
    chunk_decay = jnp.exp(acum[l - 1:l, :])
    upper = r_i <= c_i
    xs = xs_ref[...]
    for g in range(SSM_GROUPS):
        bm = bm_ref[:, g * SSM_STATE:(g + 1) * SSM_STATE]
        cm_t = cm_ref[:, g * SSM_STATE:(g + 1) * SSM_STATE].astype(jnp.float32).T.astype(MXU_DTYPE)
        cb_t = dot(bm, cm_t)
        h_in = h_ref[g * SSM_HPG:(g + 1) * SSM_HPG].reshape(SSM_HPG * hd, SSM_STATE)
        y_off = dot(h_in.astype(MXU_DTYPE), cm_t)
        for e in range(SSM_HPG):
            h = g * SSM_HPG + e
            if h % 2 == 0:
                xs_pair_t = xs[:, h * hd:(h + 2) * hd].T
            xdt_t = xs_pair_t[(h % 2) * hd:(h % 2 + 1) * hd] * dt_t[h:h + 1, :]
            seg = jnp.exp(jnp.where(upper, acum_t[h:h + 1, :] - acum[:, h:h + 1], -jnp.inf))
            y_diag = dot(xdt_t.astype(MXU_DTYPE), (cb_t * seg).astype(MXU_DTYPE))
            yT_ref[h * hd:(h + 1) * hd, :] = y_diag + y_off[e * hd:(e + 1) * hd] * from_start_t[h:h + 1, :]
            xe_ref[e * hd:(e + 1) * hd, :] = (xdt_t * to_end_t[h:h + 1, :]).astype(xe_ref.dtype)
        states = dot(xe_ref[...], bm)
        for e in range(SSM_HPG):
            h = g * SSM_HPG + e
            h_ref[h] = h_ref[h] * chunk_decay[:, h:h + 1] + states[e * hd:(e + 1) * hd]
    y = jnp.concatenate([yT_ref[i * l:(i + 1) * l, :].T for i in range(SSM_D_INNER // l)], axis=1)
    yg_ref[...] = _ssm_gate_norm(y, xs, z_ref[...], dskip_ref[...], normg_ref[...]).astype(yg_ref.dtype)

    @pl.when(c == pl.num_programs(1) - 1)
    def _():
        h_out_ref[0] = h_ref[...]


def _ssm_head_lanes(p):
    return jnp.pad(p.astype(jnp.float32), (0, SSM_DT_LANES - SSM_HEADS)).reshape(1, SSM_DT_LANES)


def _ssm_chunk_scan(b_, t_, xs, bm, cm, dt, z, a_log, d_skip, norm_g):
    l = SSM_CHUNK
    nc = t_ // l
    rows = lambda w: pl.BlockSpec((l, w), lambda b, c: (b * nc + c, 0))
    vec = lambda w: pl.BlockSpec((1, w), lambda b, c: (0, 0))
    aneg = _ssm_head_lanes(-jnp.exp(a_log.astype(jnp.float32)))
    dskip = jnp.repeat(d_skip, SSM_HEAD_DIM).reshape(1, SSM_D_INNER)
    yg, h_new = pl.pallas_call(
        _ssm_chunk_kernel,
        grid=(b_, nc),
        in_specs=[rows(SSM_D_INNER), rows(SSM_BC_DIM), rows(SSM_BC_DIM), rows(SSM_DT_LANES), rows(SSM_D_INNER),
                  vec(SSM_DT_LANES), vec(SSM_D_INNER), vec(SSM_D_INNER)],
        out_specs=[rows(SSM_D_INNER),
                   pl.BlockSpec((1, SSM_HEADS, SSM_HEAD_DIM, SSM_STATE), lambda b, c: (b, 0, 0, 0))],
        out_shape=[jax.ShapeDtypeStruct((b_ * t_, SSM_D_INNER), MXU_DTYPE),
                   jax.ShapeDtypeStruct((b_, SSM_HEADS, SSM_HEAD_DIM, SSM_STATE), jnp.float32)],
        scratch_shapes=[pltpu.VMEM((SSM_HEADS, SSM_HEAD_DIM, SSM_STATE), jnp.float32),
                        pltpu.VMEM((SSM_D_INNER, l), jnp.float32),
                        pltpu.VMEM((SSM_HPG * SSM_HEAD_DIM, l), MXU_DTYPE)],
        compiler_params=pltpu.CompilerParams(
            dimension_semantics=("parallel", "arbitrary"),
            vmem_limit_bytes=V7X_VMEM_LIMIT_BYTES),
        name="ssm_chunk_scan",
    )(xs, bm, cm, dt, z, aneg, dskip, norm_g.reshape(1, SSM_D_INNER))
    return yg, h_new


def _ssm_step_kernel(h0_ref, xs_ref, dt_ref, an_ref, bm_ref, cm_ref, y_ref, h_ref):
    h0 = h0_ref[0]
    dt = dt_ref[0]
    decay = jnp.exp(dt * an_ref[...])
    xdt = xs_ref[0] * dt
    bm = bm_ref[0].astype(jnp.float32)
    cm = cm_ref[0].astype(jnp.float32)
    h_ref[0] = h0 * decay + xdt * bm
    cb = jnp.sum(cm * bm, axis=-1, keepdims=True)
    y_ref[0] = cb * xdt + jnp.sum(cm * h0, axis=-1, keepdims=True) * decay


def _ssm_step(state, xs, bm, cm, dt, a_log):
    b_ = state.shape[0]
    per_head = lambda u: jnp.repeat(u.reshape(b_, SSM_GROUPS, 1, SSM_STATE), SSM_HPG, axis=1)
    xs4 = xs.reshape(b_, SSM_HEADS, SSM_HEAD_DIM, 1)
    dt4 = dt[:, :SSM_HEADS].reshape(b_, SSM_HEADS, 1, 1)
    an = (-jnp.exp(a_log.astype(jnp.float32))).reshape(SSM_HEADS, 1, 1)
    blk = lambda a: pl.BlockSpec((1,) + a.shape[1:], lambda b: (b, 0, 0, 0))
    args = [state.astype(jnp.float32), xs4, dt4, an, per_head(bm), per_head(cm)]
    y4, h_new = pl.pallas_call(
        _ssm_step_kernel,
        grid=(b_,),
        in_specs=[blk(args[0]), blk(xs4), blk(dt4), pl.BlockSpec(an.shape, lambda b: (0, 0, 0)),
                  blk(args[4]), blk(args[5])],
        out_specs=[blk(xs4), blk(args[0])],
        out_shape=[jax.ShapeDtypeStruct(xs4.shape, jnp.float32), jax.ShapeDtypeStruct(state.shape, jnp.float32)],
        compiler_params=pltpu.CompilerParams(
            dimension_semantics=("parallel",),
            vmem_limit_bytes=V7X_VMEM_LIMIT_BYTES),
        name="ssm_step",
    )(*args)
    return y4.reshape(b_, SSM_D_INNER), h_new


def _ssm_gate_norm_kernel(y_ref, xs_ref, z_ref, dskip_ref, normg_ref, o_ref):
    o_ref[...] = _ssm_gate_norm(y_ref[...], xs_ref[...], z_ref[...], dskip_ref[...], normg_ref[...]).astype(o_ref.dtype)


def _ssm_gate_norm_rows(y, xs, z, d_skip, norm_g):
    full = lambda a: pl.BlockSpec(a.shape, lambda i: (0,) * a.ndim)
    args = [y, xs, z, jnp.repeat(d_skip, SSM_HEAD_DIM).reshape(1, SSM_D_INNER), norm_g.reshape(1, SSM_D_INNER)]
    return pl.pallas_call(
        _ssm_gate_norm_kernel,
        grid=(1,),
        in_specs=[full(a) for a in args],
        out_specs=full(y),
        out_shape=jax.ShapeDtypeStruct(y.shape, MXU_DTYPE),
        name="ssm_gate_norm",
    )(*args)


def _mamba2_mixer_pallas(x3d, conv_state, ssm_state, w_in, conv_w, conv_b, dt_bias, a_log, d_skip, norm_g):
    b_, t_, _ = x3d.shape
    z, xs, bm, cm, dt, conv_new = _ssm_project(x3d, conv_state, w_in, conv_w, conv_b, dt_bias)
    if t_ == 1:
        y, h_new = _ssm_step(ssm_state, xs, bm, cm, dt, a_log)
        yg = _ssm_gate_norm_rows(y, xs, z, d_skip, norm_g)
    else:
        yg, h_new = _ssm_chunk_scan(b_, t_, xs, bm, cm, dt, z, a_log, d_skip, norm_g)
    return yg, conv_new, h_new.astype(ssm_state.dtype)


PAGES_PER_STEP = 8


def _sortable_key(score):
    bits = pltpu.bitcast(score, jnp.int32)
    return jnp.where(bits >= 0, bits, bits ^ jnp.int32(0x7FFFFFFF))


def _dsa_decode_kernel(pt_ref, iq_ref, iw_ref, q_ref, ikn_ref, kn_ref, vn_ref, *rest,
                       topk, col_bits, n_steps, pages):
    idx_refs, k_refs, v_refs = rest[:pages], rest[pages:2 * pages], rest[2 * pages:3 * pages]
    o_ref, key_ref, sel_ref, m_ref, l_ref, acc_ref = rest[3 * pages:]
    s = pl.program_id(1)
    nk = key_ref.shape[2]
    nt = (((1,), (1,)), ((), ()))
    iq = iq_ref[0]
    iw = iw_ref[0]
    lane = lax.broadcasted_iota(jnp.int32, (1, nk), 1)

    def row_dot(a, row):
        return jnp.sum(a.astype(jnp.float32) * row.astype(jnp.float32), axis=1, keepdims=True)

    def index_score(ik):
        if ik.shape[0] == 1:
            sc = row_dot(iq, ik)
        else:
            sc = lax.dot_general(iq, ik, nt, preferred_element_type=jnp.float32)
        return jnp.sum(iw * jnp.maximum(sc, 0.0), axis=0, keepdims=True)

    def fold(hit):
        out = hit[:, 0:128]
        for l in range(1, nk // 128):
            out = out + hit[:, l * 128:(l + 1) * 128]
        return out

    @pl.when(s < n_steps)
    def _score():
        ik = jnp.concatenate([r[0] for r in idx_refs], axis=0).astype(MXU_DTYPE)
        key_ref[s] = _sortable_key(index_score(ik))

    @pl.when(s == n_steps - 1)
    def _select():
        key_new = _sortable_key(index_score(ikn_ref[0]))

        def count(pred_past, pred_new):
            cnt = jnp.zeros((1, 128), jnp.float32)
            for st in range(n_steps):
                cnt = cnt + fold(jnp.where(pred_past(key_ref[st], lane + st * nk), 1.0, 0.0))
            return jnp.sum(cnt, axis=1, keepdims=True) + jnp.where(pred_new(key_new), 1.0, 0.0)

        def bit_body(i, thr):
            cand = thr ^ lax.shift_left(jnp.int32(1), jnp.int32(31) - i)
            total = count(lambda k, c: k >= cand, lambda k: k >= cand)
            return jnp.where(total >= float(topk), cand, thr)

        thr = lax.fori_loop(0, 32, bit_body, jnp.full((1, 1), INT32_MIN, jnp.int32))
        need = float(topk) - count(lambda k, c: k > thr, lambda k: k > thr)

        def col_body(i, last):
            cand = last | lax.shift_left(jnp.int32(1), jnp.int32(col_bits - 1) - i)
            ties = count(lambda k, c: (k == thr) & (c < cand),
                         lambda k: (k == thr) & (jnp.int32(n_steps * nk) < cand))
            return jnp.where(ties < need, cand, last)

        last_tie = lax.fori_loop(0, col_bits, col_body, jnp.zeros((1, 1), jnp.int32))
        sel_ref[0] = jnp.broadcast_to(thr, sel_ref.shape[1:])
        sel_ref[1] = jnp.broadcast_to(last_tie, sel_ref.shape[1:])
        sel_ref[2] = jnp.broadcast_to(key_new, sel_ref.shape[1:])

    gsz = ATT_HEADS // ATT_KV_HEADS
    q = q_ref[0]
    q_wide = jnp.concatenate([q] * ATT_KV_HEADS, axis=1)
    head_i = lax.broadcasted_iota(jnp.int32, q_wide.shape, 0)
    col_i = lax.broadcasted_iota(jnp.int32, q_wide.shape, 1)
    own_group = (col_i // ATT_HEAD_DIM) == (head_i // gsz)
    q_blk = jnp.where(own_group, q_wide, jnp.zeros_like(q_wide))

    def keep_mask(key, col):
        thr, last_tie = sel_ref[0, 0:1, 0:1], sel_ref[1, 0:1, 0:1]
        return (key > thr) | ((key == thr) & (col <= last_tie))

    def online_update(logits, weighted_values):
        m_old = m_ref[...]
        m_new = jnp.maximum(m_old, jnp.max(logits, axis=1, keepdims=True))
        p = jnp.exp(logits - m_new)
        alpha = jnp.exp(m_old - m_new)
        l_ref[...] = alpha * l_ref[...] + jnp.sum(p, axis=1, keepdims=True)
        acc_ref[...] = alpha * acc_ref[...] + weighted_values(p.astype(MXU_DTYPE))
        m_ref[...] = m_new

    @pl.when(s == n_steps)
    def _init():
        m_ref[...] = jnp.full(m_ref.shape, MASK_NEG, jnp.float32)
        l_ref[...] = jnp.zeros(l_ref.shape, jnp.float32)
        acc_ref[...] = jnp.zeros(acc_ref.shape, jnp.float32)

    @pl.when(s >= n_steps)
    def _attend():
        st = s - n_steps
        bias = jnp.where(keep_mask(key_ref[st], lane + st * nk), 0.0, MASK_NEG)
        kk = jnp.concatenate([r[0] for r in k_refs], axis=0).astype(MXU_DTYPE)
        vv = jnp.concatenate([r[0] for r in v_refs], axis=0).astype(MXU_DTYPE)
        online_update(lax.dot_general(q_blk, kk, nt, preferred_element_type=jnp.float32) + bias,
                      lambda p: jnp.dot(p, vv, preferred_element_type=jnp.float32))

    @pl.when(s == 2 * n_steps - 1)
    def _finish():
        keep_new = keep_mask(sel_ref[2, 0:1, 0:1], jnp.int32(n_steps * nk))
        logit = row_dot(q_blk, kn_ref[0])
        v_row = vn_ref[0].astype(jnp.float32)
        online_update(logit + jnp.where(keep_new, 0.0, MASK_NEG), lambda p: p.astype(jnp.float32) * v_row)
        out = jnp.where(own_group, acc_ref[...] / l_ref[...], 0.0)
        o = out[:, 0:ATT_HEAD_DIM]
        for g in range(1, ATT_KV_HEADS):
            o = o + out[:, g * ATT_HEAD_DIM:(g + 1) * ATT_HEAD_DIM]
        o_ref[0] = o.astype(o_ref.dtype)


def _dsa_decode(q, iq, iw, ik_new, k_new, v_new, cache_k, cache_v, cache_idx_k, page_table):
    b_, n_pages = page_table.shape
    n_pool, page = cache_k.shape[0], cache_k.shape[1]
    pages = PAGES_PER_STEP
    n_steps = n_pages // pages
    past = n_pages * page
    ck = cache_k.reshape(n_pool, page, ATT_KV_DIM)
    cv = cache_v.reshape(n_pool, page, ATT_KV_DIM)
    per_seq = lambda a: pl.BlockSpec((1,) + a.shape[1:], lambda b, s, pt: (b,) + (0,) * (a.ndim - 1))

    def paged(width, j, attend_phase):
        def index(b, s, pt):
            grp = jnp.maximum(s - n_steps, 0) if attend_phase else jnp.minimum(s, n_steps - 1)
            return (pt[b, grp * pages + j], 0, 0)
        return pl.BlockSpec((1, page, width), index)

    small = [iq.reshape(b_, IDX_HEADS, IDX_DIM), iw[:, :IDX_HEADS].reshape(b_, IDX_HEADS, 1),
             q.reshape(b_, ATT_HEADS, ATT_HEAD_DIM), ik_new.astype(MXU_DTYPE).reshape(b_, 1, IDX_DIM),
             k_new.astype(MXU_DTYPE).reshape(b_, 1, ATT_KV_DIM), v_new.astype(MXU_DTYPE).reshape(b_, 1, ATT_KV_DIM)]
    grid_spec = pltpu.PrefetchScalarGridSpec(
        num_scalar_prefetch=1,
        grid=(b_, 2 * n_steps),
        in_specs=[per_seq(a) for a in small]
        + [paged(IDX_DIM, j, False) for j in range(pages)]
        + [paged(ATT_KV_DIM, j, True) for j in range(pages)]
        + [paged(ATT_KV_DIM, j, True) for j in range(pages)],
        out_specs=pl.BlockSpec((1, ATT_HEADS, ATT_HEAD_DIM), lambda b, s, pt: (b, 0, 0)),
        scratch_shapes=[pltpu.VMEM((n_steps, 1, pages * page), jnp.int32),
                        pltpu.VMEM((3, SUBLANES, 128), jnp.int32),
                        pltpu.VMEM((ATT_HEADS, 1), jnp.float32),
                        pltpu.VMEM((ATT_HEADS, 1), jnp.float32),
                        pltpu.VMEM((ATT_HEADS, ATT_KV_DIM), jnp.float32)])
    o = pl.pallas_call(
        functools.partial(_dsa_decode_kernel, topk=min(TOPK_MAX, (past + 1) // 4),
                          col_bits=max(1, past.bit_length()), n_steps=n_steps, pages=pages),
        grid_spec=grid_spec,
        out_shape=jax.ShapeDtypeStruct((b_, ATT_HEADS, ATT_HEAD_DIM), MXU_DTYPE),
        compiler_params=pltpu.CompilerParams(
            dimension_semantics=("parallel", "arbitrary"),
            vmem_limit_bytes=V7X_VMEM_LIMIT_BYTES),
        name="dsa_decode",
    )(page_table, *small, *([cache_idx_k] * pages), *([ck] * pages), *([cv] * pages))
    return o.reshape(b_, ATT_Q_DIM)


def _dsa_sample_pallas(x3d, cache_k, cache_v, cache_idx_k, page_table, w_in, kn_g, kn_b):
    b_, t_, _ = x3d.shape
    past = page_table.shape[1] * cache_k.shape[1]
    pos = jnp.full((b_,), past, jnp.int32)
    q, iq, v, _, iw, kT, _, ikT, _ = _dsa_project(x3d.reshape(1, b_, D_MODEL), pos, w_in, kn_g, kn_b)
    k = _untranspose_groups(kT)[0]
    ik = _untranspose_groups(ikT)[0]
    o = _dsa_decode(q, iq, iw, ik, k, v, cache_k, cache_v, cache_idx_k, page_table)
    kv4 = lambda u: u.reshape(b_, t_, ATT_KV_HEADS, ATT_HEAD_DIM)
    return o, kv4(k), kv4(v), ik.reshape(b_, t_, IDX_DIM)


def _layer_norm(x, g, b):
    xf = x.astype(jnp.float32)
    mu = jnp.mean(xf, -1, keepdims=True)
    var = jnp.mean(jnp.square(xf - mu), -1, keepdims=True)
    return ((xf - mu) * lax.rsqrt(var + LN_EPS)).astype(x.dtype) * g + b


def _rope_partial(x, pos, rot_dim):
    half = rot_dim // 2
    inv = ROPE_THETA ** (-jnp.arange(half, dtype=jnp.float32) / half)
    ang = pos.astype(jnp.float32)[:, None] * inv[None, :]
    cos = jnp.cos(ang)[:, None, :]
    sin = jnp.sin(ang)[:, None, :]
    xf = x[..., :rot_dim].astype(jnp.float32)
    x1, x2 = xf[..., :half], xf[..., half:]
    rot = jnp.concatenate([x1 * cos - x2 * sin, x2 * cos + x1 * sin], axis=-1).astype(x.dtype)
    return jnp.concatenate([rot, x[..., rot_dim:]], axis=-1)


def _gather_rows(rows, idx):
    return jax.vmap(lambda r, i: r[i])(rows, idx)


def _gmlp_mixer(x, w_in, ln_g, ln_b, ws, bs):
    B_, T, _ = x.shape
    u, v = jnp.split(jax.nn.gelu(x @ w_in), 2, axis=-1)
    v = _layer_norm(v, ln_g, ln_b)
    l = min(T, CHUNK)
    c = T // l
    mask = jnp.tril(jnp.ones((l, l), dtype=bool))
    w = jnp.where(mask, ws[:, :l, :l], 0.0)
    vc = v.reshape(B_, c, l, GM_GROUPS, GM_GROUP_DIM)
    mixed = jnp.einsum('gts,bcsgd->bctgd', w, vc) + jnp.transpose(bs[:, :l])[:, :, None]
    return u * mixed.reshape(B_, T, GM_WIDTH), v


def _ssd_chunked(xs, dt, a, bm, cm, h0):
    B_, T = xs.shape[:2]
    l = min(T, SSM_CHUNK)
    c = T // l
    blk = lambda t: t.reshape((B_, c, l) + t.shape[2:])
    xdt = blk(xs.astype(jnp.float32) * dt[..., None])
    bc, cc, acum = blk(bm), blk(cm), jnp.cumsum(blk(a), axis=2)
    at = jnp.moveaxis(acum, 2, -1)
    causal = jnp.tril(jnp.ones((l, l), dtype=bool))
    seg = jnp.exp(jnp.where(causal, at[..., :, None] - at[..., None, :], -jnp.inf))
    cb = jnp.einsum('bctgn,bcsgn->bcgts', cc, bc)
    y_diag = jnp.einsum('bcgts,bcgets,bcsgep->bctgep', cb, seg, xdt)
    states = jnp.einsum('bclgn,bclge,bclgep->bcgepn', bc, jnp.exp(acum[:, :, -1:] - acum), xdt)
    chunk_decay = jnp.exp(acum[:, :, -1])

    def step(h, inp):
        dec, st = inp
        return h * dec[..., None, None] + st, h

    h_last, h_in = lax.scan(step, h0, (jnp.moveaxis(chunk_decay, 1, 0), jnp.moveaxis(states, 1, 0)))
    y_off = jnp.einsum('bctgn,bcgepn,bctge->bctgep', cc, jnp.moveaxis(h_in, 0, 1), jnp.exp(acum))
    return (y_diag + y_off).reshape(B_, T, SSM_GROUPS, SSM_HPG, SSM_HEAD_DIM), h_last


def _mamba2_mixer(x, conv_state, ssm_state, w_in, conv_w, conv_b, dt_bias, a_log, d_skip, norm_g):
    B_, T, _ = x.shape
    z, xbc, dt = jnp.split(x @ w_in, [SSM_D_INNER, SSM_D_INNER + SSM_CONV_DIM], axis=-1)
    xbc_ext = jnp.concatenate([conv_state, xbc], axis=1)
    conv = conv_b
    for j in range(SSM_CONV):
        conv = conv + xbc_ext[:, j:j + T] * conv_w[j]
    xbc = jax.nn.silu(conv)
    xs, bm, cm = jnp.split(xbc, [SSM_D_INNER, SSM_D_INNER + SSM_GROUPS * SSM_STATE], axis=-1)
    xs = xs.reshape(B_, T, SSM_GROUPS, SSM_HPG, SSM_HEAD_DIM)
    bm = bm.reshape(B_, T, SSM_GROUPS, SSM_STATE)
    cm = cm.reshape(B_, T, SSM_GROUPS, SSM_STATE)
    dt = jax.nn.softplus((dt + dt_bias).astype(jnp.float32)).reshape(B_, T, SSM_GROUPS, SSM_HPG)
    a_neg = -jnp.exp(a_log.astype(jnp.float32)).reshape(SSM_GROUPS, SSM_HPG)
    h0 = ssm_state.astype(jnp.float32).reshape(B_, SSM_GROUPS, SSM_HPG, SSM_HEAD_DIM, SSM_STATE)
    y, h_last = _ssd_chunked(xs, dt, dt * a_neg, bm, cm, h0)
    y = y.astype(x.dtype) + xs * d_skip.reshape(SSM_GROUPS, SSM_HPG, 1)
    yg = (y.reshape(B_, T, SSM_D_INNER) * jax.nn.silu(z)).reshape(B_, T, SSM_GROUPS, -1).astype(jnp.float32)
    yg = (yg * lax.rsqrt(jnp.mean(jnp.square(yg), -1, keepdims=True) + LN_EPS)).astype(x.dtype)
    yg = yg.reshape(B_, T, SSM_D_INNER) * norm_g
    new_ssm = h_last.reshape(B_, SSM_HEADS, SSM_HEAD_DIM, SSM_STATE).astype(ssm_state.dtype)
    return yg, xbc_ext[:, T:], new_ssm


def _dsa_project_jax(x, pos, w_in, kn_g, kn_b):
    B_, T, _ = x.shape
    q, k, v, iq, ik, iw = jnp.split(x @ w_in, list(ATT_IN_SPLITS), axis=-1)
    q = _rope_partial(q.reshape(B_, T, ATT_HEADS, ATT_HEAD_DIM), pos, ROPE_DIM)
    k = _rope_partial(k.reshape(B_, T, ATT_KV_HEADS, ATT_HEAD_DIM), pos, ROPE_DIM)
    v = v.reshape(B_, T, ATT_KV_HEADS, ATT_HEAD_DIM)
    iq = _rope_partial(iq.reshape(B_, T, IDX_HEADS, IDX_DIM), pos, IDX_ROPE_DIM)
    ik = _rope_partial(_layer_norm(ik, kn_g, kn_b)[:, :, None, :], pos, IDX_ROPE_DIM)[:, :, 0, :]
    iw = iw * (IDX_HEADS ** -0.5 * IDX_DIM ** -0.5)
    return q, k, v, iq, ik, iw


def _dsa_select(iq, iw, ik, qpos, topk):
    s = jnp.einsum('bqhd,bsd->bqhs', iq, ik)
    score = jnp.einsum('bqh,bqhs->bqs', iw, jax.nn.relu(s)).astype(jnp.float32)
    adm = jnp.arange(ik.shape[1])[None, :] <= qpos[:, None]
    score = jnp.where(adm[None], score, -jnp.inf)
    _, idx = lax.top_k(score, topk)
    return idx, idx <= qpos[None, :, None]


def _sparse_attend(q, k_sel, v_sel, valid):
    B_, Q = q.shape[:2]
    qg = q.reshape(B_, Q, ATT_KV_HEADS, ATT_HEADS // ATT_KV_HEADS, ATT_HEAD_DIM)
    s = jnp.einsum('bqhgd,bqkhd->bqhgk', qg, k_sel).astype(jnp.float32) * (ATT_HEAD_DIM ** -0.5)
    s = jnp.where(valid[:, :, None, None, :], s, -jnp.inf)
    p = jax.nn.softmax(s, axis=-1).astype(v_sel.dtype)
    o = jnp.einsum('bqhgk,bqkhd->bqhgd', p, v_sel)
    return o.reshape(B_, Q, ATT_Q_DIM)


def _dsa_prompt(x, w_in, kn_g, kn_b):
    B_, T, _ = x.shape
    q, k, v, iq, ik, iw = _dsa_project_jax(x, jnp.arange(T), w_in, kn_g, kn_b)
    topk = min(TOPK_MAX, T // 4)

    def block(bi):
        t0 = bi * Q_BLOCK
        sl = lambda t: lax.dynamic_slice_in_dim(t, t0, Q_BLOCK, axis=1)
        qpos = t0 + jnp.arange(Q_BLOCK)
        idx, valid = _dsa_select(sl(iq), sl(iw), ik, qpos, topk)
        return _sparse_attend(sl(q), _gather_rows(k, idx), _gather_rows(v, idx), valid)

    o = lax.map(block, jnp.arange(T // Q_BLOCK))
    o = jnp.moveaxis(o, 0, 1).reshape(B_, T, ATT_Q_DIM)
    return o, k, v, ik


def _dsa_sample(x, cache_k, cache_v, cache_idx_k, page_table, w_in, kn_g, kn_b):
    B_, T, _ = x.shape
    page = cache_k.shape[1]
    past = page_table.shape[1] * page
    pos = past + jnp.arange(T)
    q, k, v, iq, ik, iw = _dsa_project_jax(x, pos, w_in, kn_g, kn_b)
    ik_all = jnp.concatenate([cache_idx_k[page_table].reshape(B_, past, IDX_DIM), ik], axis=1)
    idx, valid = _dsa_select(iq, iw, ik_all, pos, min(TOPK_MAX, (past + T) // 4))
    past_idx = jnp.minimum(idx, past - 1)
    phys = jnp.take_along_axis(page_table, (past_idx // page).reshape(B_, -1), axis=1).reshape(idx.shape)
    off = past_idx % page
    new_idx = jnp.clip(idx - past, 0, T - 1)
    is_new = (idx >= past)[..., None, None]
    k_sel = jnp.where(is_new, _gather_rows(k, new_idx), cache_k[phys, off])
    v_sel = jnp.where(is_new, _gather_rows(v, new_idx), cache_v[phys, off])
    o = _sparse_attend(q, k_sel, v_sel, valid)
    return o, k, v, ik


def _rwkv7_mixer(x, shift, wkv, mu, w_r, w_k, w_v, w0, w1, w2, a0, a1, a2, g1, g2,
                 k_k, k_a, r_k, gn_g, gn_b):
    B_, T, _ = x.shape
    x_prev = jnp.concatenate([shift[:, None, :], x[:, :-1]], axis=1)
    xm = x[None] + (x_prev - x)[None] * mu[:, None, None, :]
    xr, xw, xk, xv, xa, xg = xm
    r = xr @ w_r
    w_log = -jax.nn.softplus(-(w0 + jnp.tanh(xw @ w1) @ w2)) - 0.5
    k = xk @ w_k
    v = xv @ w_v
    a = jax.nn.sigmoid(a0 + (xa @ a1) @ a2)
    g = jax.nn.sigmoid(xg @ g1) @ g2
    heads = lambda t: t.reshape(B_, T, RW_HEADS, RW_HEAD)
    kk = heads(k * k_k).astype(jnp.float32)
    kk = kk / jnp.maximum(jnp.sqrt(jnp.sum(kk * kk, -1, keepdims=True)), 1e-12)
    k = k * (1.0 + (a - 1.0) * k_a)
    decay = jnp.exp(-jnp.exp(w_log.astype(jnp.float32)))
    r, k, v, a, decay = heads(r), heads(k), heads(v), heads(a), heads(decay)
    seq = tuple(jnp.moveaxis(t.astype(jnp.float32), 1, 0) for t in (r, decay, k, v, kk, kk * a))

    def step(s, inp):
        r_t, d_t, k_t, v_t, kk_t, b_t = inp
        sa = jnp.einsum('bhij,bhj->bhi', s, kk_t)
        s = s * d_t[:, :, None, :] - sa[..., None] * b_t[:, :, None, :] + v_t[..., None] * k_t[:, :, None, :]
        return s, jnp.einsum('bhij,bhj->bhi', s, r_t)

    s_last, y = lax.scan(step, wkv.astype(jnp.float32), seq)
    y = jnp.moveaxis(y, 0, 1)
    mu_y = jnp.mean(y, -1, keepdims=True)
    var_y = jnp.mean(jnp.square(y - mu_y), -1, keepdims=True)
    yn = ((y - mu_y) * lax.rsqrt(var_y + RW_GN_EPS)).reshape(B_, T, D_MODEL).astype(x.dtype) * gn_g + gn_b
    bonus = (jnp.sum(r * k * r_k, -1, keepdims=True) * v).reshape(B_, T, D_MODEL)
    return (yn + bonus) * g, x[:, -1], s_last.astype(wkv.dtype)


def kernel(x_prompt, x_sample, state_ssm_conv, state_ssm, cache_k, cache_v, cache_idx_k, state_rwkv_shift, state_rwkv_wkv, page_table, p_prompt, p_sample, ln_g, ln_b, ffn_w_up, ffn_w_down, ple_w_p, ple_w_g, ple_b_g, gm_w_in, gm_ln_g, gm_ln_b, gm_ws, gm_bs, gm_w_out, ssm_w_in, ssm_conv_w, ssm_conv_b, ssm_dt_bias, ssm_a_log, ssm_d, ssm_norm_g, ssm_w_out, att_w_in, att_kn_g, att_kn_b, att_w_out, rw_mu, rw_w_r, rw_w_k, rw_w_v, rw_w_o, rw_w0, rw_w1, rw_w2, rw_a0, rw_a1, rw_a2, rw_g1, rw_g2, rw_k_k, rw_k_a, rw_r_k, rw_gn_g, rw_gn_b):
    bp, tp, _ = x_prompt.shape
    bs_, ts, _ = x_sample.shape
    bf = lambda w: w.astype(jnp.bfloat16)
    w_up_bf, w_down_bf = bf(ffn_w_up), bf(ffn_w_down)
    ple_wp_bf, ple_wg_bf = bf(ple_w_p), bf(ple_w_g)

    yp = x_prompt.reshape(bp * tp, D_MODEL)
    ys = x_sample.reshape(bs_ * ts, D_MODEL)
    r3p = lambda t: t.reshape(bp, tp, -1)
    r3s = lambda t: t.reshape(bs_, ts, -1)
    f2 = lambda t: t.reshape(-1, t.shape[-1])

    for i in range(DEPTH):
        yp = _ffn_sub(yp, w_up_bf[i, 0], w_down_bf[i, 0], ln_g[i, 0], ln_b[i, 0])
        ys = _ffn_sub(ys, w_up_bf[i, 0], w_down_bf[i, 0], ln_g[i, 0], ln_b[i, 0])
        m = i % N_MIXERS
        if m == 0:
            gm_args = (gm_w_in, gm_ln_g, gm_ln_b, gm_ws, gm_bs, gm_w_out, ln_g[i, 1], ln_b[i, 1])
            yp, = _gmlp_block(yp, tp, *gm_args, False)
            ys, gm_v_s = _gmlp_block(ys, ts, *gm_args, True)
            gm_v_s = r3s(gm_v_s)
        elif m == 1:
            ssm_args = (ssm_w_in, ssm_conv_w, ssm_conv_b, ssm_dt_bias, ssm_a_log, ssm_d, ssm_norm_g)
            hp, conv_p, ssm_p = _mamba2_mixer_pallas(
                r3p(yp), jnp.zeros((bp, SSM_CONV - 1, SSM_CONV_DIM), yp.dtype),
                jnp.zeros((bp, SSM_HEADS, SSM_HEAD_DIM, SSM_STATE), yp.dtype), *ssm_args)
            hs, conv_s, ssm_s = _mamba2_mixer_pallas(r3s(ys), state_ssm_conv, state_ssm, *ssm_args)
            w_out = bf(ssm_w_out)
        elif m == 2:
            hp, k_p, v_p, ik_p = _dsa_prompt_pallas(r3p(yp), att_w_in, att_kn_g, att_kn_b)
            hs, k_s, v_s, ik_s = _dsa_sample_pallas(r3s(ys), cache_k, cache_v, cache_idx_k, page_table,
                                                    att_w_in, att_kn_g, att_kn_b)
            w_out = bf(att_w_out)
        else:
            rw_args = (rw_mu, rw_w_r, rw_w_k, rw_w_v, rw_w0, rw_w1, rw_w2, rw_a0, rw_a1, rw_a2,
                       rw_g1, rw_g2, rw_k_k, rw_k_a, rw_r_k, rw_gn_g, rw_gn_b)
            hp, gate_p, sh_p, wkv_p = _rwkv7_mixer_pallas(
                r3p(yp), jnp.zeros((bp, D_MODEL), yp.dtype),
                jnp.zeros((bp, RW_HEADS, RW_HEAD, RW_HEAD), yp.dtype), *rw_args)
            hs, gate_s, sh_s, wkv_s = _rwkv7_mixer_pallas(r3s(ys), state_rwkv_shift, state_rwkv_wkv, *rw_args)
            w_out = bf(rw_w_o)
        if m == 3:
            yp = _proj_gate_post_norm(yp, hp, gate_p, w_out, ln_g[i, 1], ln_b[i, 1])
            ys = _proj_gate_post_norm(ys, hs, gate_s, w_out, ln_g[i, 1], ln_b[i, 1])
        elif m != 0:
            yp = _proj_post_norm(yp, f2(hp), w_out, ln_g[i, 1], ln_b[i, 1])
            ys = _proj_post_norm(ys, f2(hs), w_out, ln_g[i, 1], ln_b[i, 1])
        yp = _ffn_sub(yp, w_up_bf[i, 1], w_down_bf[i, 1], ln_g[i, 2], ln_b[i, 2])
        ys = _ffn_sub(ys, w_up_bf[i, 1], w_down_bf[i, 1], ln_g[i, 2], ln_b[i, 2])
        yp = _ple_add(yp, f2(p_prompt[i]), ple_wp_bf[i], ple_wg_bf[i], ple_b_g[i])
        ys = _ple_add(ys, f2(p_sample[i]), ple_wp_bf[i], ple_wg_bf[i], ple_b_g[i])

    return (r3p(yp), r3s(ys), gm_v_s, conv_p, ssm_p, conv_s, ssm_s, k_p, v_p, ik_p, k_s, v_s, ik_s,
            sh_p, wkv_p, sh_s, wkv_s)
```

```python
import functools
import math

import jax
import jax.numpy as jnp
from jax import lax
from jax.experimental import pallas as pl
from jax.experimental.pallas import tpu as pltpu

D_MODEL = 1024
DEPTH = 4
N_MIXERS = 4
PLE_DIM = 256
D_FF = 2816
ALPHA = (2 * DEPTH) ** 0.25
LN_EPS = 1e-5

CHUNK = 128
GM_WIDTH = 2 * D_MODEL
GM_GROUPS = 8
GM_GROUP_DIM = GM_WIDTH // GM_GROUPS

SSM_D_INNER = 2 * D_MODEL
SSM_HEAD_DIM = 64
SSM_HEADS = SSM_D_INNER // SSM_HEAD_DIM
SSM_GROUPS = 4
SSM_HPG = SSM_HEADS // SSM_GROUPS
SSM_STATE = 128
SSM_CONV = 4
SSM_CONV_DIM = SSM_D_INNER + 2 * SSM_GROUPS * SSM_STATE
SSM_CHUNK = 128

ATT_HEADS = 16
ATT_KV_HEADS = 4
ATT_HEAD_DIM = D_MODEL // ATT_HEADS
ROPE_DIM = ATT_HEAD_DIM // 4
ROPE_THETA = 500000.0
IDX_HEADS = 8
IDX_DIM = 64
IDX_ROPE_DIM = IDX_DIM // 4
TOPK_MAX = 256
Q_BLOCK = 128
ATT_Q_DIM = ATT_HEADS * ATT_HEAD_DIM
ATT_KV_DIM = ATT_KV_HEADS * ATT_HEAD_DIM
ATT_IN_SPLITS = (ATT_Q_DIM, ATT_Q_DIM + ATT_KV_DIM, ATT_Q_DIM + 2 * ATT_KV_DIM,
                 ATT_Q_DIM + 2 * ATT_KV_DIM + IDX_HEADS * IDX_DIM,
                 ATT_Q_DIM + 2 * ATT_KV_DIM + IDX_HEADS * IDX_DIM + IDX_DIM)

RW_HEAD = 64
RW_HEADS = D_MODEL // RW_HEAD
RW_GN_EPS = 64e-5

V7X_VMEM_LIMIT_BYTES = 52 * 1024 * 1024
FF_TILE = D_FF // 2
ROW_TILE = 512


def _row_tile(m):
    return ROW_TILE if m % ROW_TILE == 0 else m


def _ln_rows(y, g, b):
    mu = jnp.mean(y, axis=-1, keepdims=True)
    yc = y - mu
    var = jnp.mean(yc * yc, axis=-1, keepdims=True)
    return yc * lax.rsqrt(var + LN_EPS) * g + b


def _ffn_kernel(x_ref, wa_ref, wb_ref, wd_ref, g_ref, b_ref, o_ref, acc_ref):
    f = pl.program_id(1)

    @pl.when(f == 0)
    def _():
        acc_ref[...] = jnp.zeros_like(acc_ref)

    xb = x_ref[...].astype(jnp.bfloat16)
    a = jnp.dot(xb, wa_ref[...], preferred_element_type=jnp.float32)
    b = jnp.dot(xb, wb_ref[...], preferred_element_type=jnp.float32)
    h = (a * jax.nn.sigmoid(a) * b).astype(jnp.bfloat16)
    acc_ref[...] += jnp.dot(h, wd_ref[...], preferred_element_type=jnp.float32)

    @pl.when(f == pl.num_programs(1) - 1)
    def _():
        y = ALPHA * x_ref[...] + 0.5 * acc_ref[...]
        o_ref[...] = _ln_rows(y, g_ref[...], b_ref[...])


def _ffn_sub(x2d, w_up, w_down, layer, half, g, b):
    m = x2d.shape[0]
    tm = _row_tile(m)
    nf = D_FF // FF_TILE
    return pl.pallas_call(
        _ffn_kernel,
        grid=(m // tm, nf),
        in_specs=[
            pl.BlockSpec((tm, D_MODEL), lambda i, f: (i, 0)),
            pl.BlockSpec((None, None, D_MODEL, FF_TILE), lambda i, f: (layer, half, 0, f)),
            pl.BlockSpec((None, None, D_MODEL, FF_TILE), lambda i, f: (layer, half, 0, f + nf)),
            pl.BlockSpec((None, None, FF_TILE, D_MODEL), lambda i, f: (layer, half, f, 0)),
            pl.BlockSpec((1, D_MODEL), lambda i, f: (0, 0)),
            pl.BlockSpec((1, D_MODEL), lambda i, f: (0, 0)),
        ],
        out_specs=pl.BlockSpec((tm, D_MODEL), lambda i, f: (i, 0)),
        out_shape=jax.ShapeDtypeStruct((m, D_MODEL), jnp.float32),
        scratch_shapes=[pltpu.VMEM((tm, D_MODEL), jnp.float32)],
        compiler_params=pltpu.CompilerParams(
            dimension_semantics=("parallel", "arbitrary"),
            vmem_limit_bytes=V7X_VMEM_LIMIT_BYTES),
        name="ffn_sub",
    )(x2d, w_up, w_up, w_down, g.reshape(1, D_MODEL), b.reshape(1, D_MODEL))


def _ple_kernel(x_ref, p_ref, wp_ref, wg_ref, bg_ref, o_ref):
    x = x_ref[...]
    gate = jax.nn.sigmoid(
        jnp.dot(x.astype(jnp.bfloat16), wg_ref[...], preferred_element_type=jnp.float32) + bg_ref[...])
    emb = jnp.dot(p_ref[...].astype(jnp.bfloat16), wp_ref[...], preferred_element_type=jnp.float32)
    o_ref[...] = x + gate * emb


def _ple_add(x2d, p3d, w_p, w_g, b_g, layer):
    m = x2d.shape[0]
    tm = _row_tile(m)
    return pl.pallas_call(
        _ple_kernel,
        grid=(m // tm,),
        in_specs=[
            pl.BlockSpec((tm, D_MODEL), lambda i: (i, 0)),
            pl.BlockSpec((None, tm, PLE_DIM), lambda i: (layer, i, 0)),
            pl.BlockSpec((None, PLE_DIM, D_MODEL), lambda i: (layer, 0, 0)),
            pl.BlockSpec((None, D_MODEL, D_MODEL), lambda i: (layer, 0, 0)),
            pl.BlockSpec((1, D_MODEL), lambda i: (0, 0)),
        ],
        out_specs=pl.BlockSpec((tm, D_MODEL), lambda i: (i, 0)),
        out_shape=jax.ShapeDtypeStruct((m, D_MODEL), jnp.float32),
        compiler_params=pltpu.CompilerParams(
            dimension_semantics=("parallel",),
            vmem_limit_bytes=V7X_VMEM_LIMIT_BYTES),
        name="ple_add",
    )(x2d, p3d, w_p, w_g, b_g.reshape(1, D_MODEL))


def _proj_ln_kernel(x_ref, h_ref, w_ref, g_ref, b_ref, o_ref):
    y = ALPHA * x_ref[...] + jnp.dot(h_ref[...].astype(jnp.bfloat16), w_ref[...],
                                     preferred_element_type=jnp.float32)
    o_ref[...] = _ln_rows(y, g_ref[...], b_ref[...])


def _proj_post_norm(x2d, h2d, w_out, g, b):
    m = x2d.shape[0]
    k = h2d.shape[1]
    tm = _row_tile(m)
    return pl.pallas_call(
        _proj_ln_kernel,
        grid=(m // tm,),
        in_specs=[
            pl.BlockSpec((tm, D_MODEL), lambda i: (i, 0)),
            pl.BlockSpec((tm, k), lambda i: (i, 0)),
            pl.BlockSpec((k, D_MODEL), lambda i: (0, 0)),
            pl.BlockSpec((1, D_MODEL), lambda i: (0, 0)),
            pl.BlockSpec((1, D_MODEL), lambda i: (0, 0)),
        ],
        out_specs=pl.BlockSpec((tm, D_MODEL), lambda i: (i, 0)),
        out_shape=jax.ShapeDtypeStruct((m, D_MODEL), jnp.float32),
        compiler_params=pltpu.CompilerParams(
            dimension_semantics=("parallel",),
            vmem_limit_bytes=V7X_VMEM_LIMIT_BYTES),
        name="proj_post_norm",
    )(x2d, h2d, w_out, g.reshape(1, D_MODEL), b.reshape(1, D_MODEL))


MXU_DTYPE = jnp.bfloat16
KEY_GROUP = 512
INT32_MIN = -2 ** 31
MASK_NEG = -1e30


def _rope_lane_tables(pos, rot_dim, head_dim):
    half = rot_dim // 2
    inv = ROPE_THETA ** (-jnp.arange(half, dtype=jnp.float32) / half)
    ang = pos.astype(jnp.float32)[:, None] * inv[None, :]
    cos, sin = jnp.cos(ang), jnp.sin(ang)
    n = pos.shape[0]
    rest = head_dim - rot_dim
    c = jnp.concatenate([cos, cos, jnp.ones((n, rest), jnp.float32)], axis=1)
    s1 = jnp.concatenate([-sin, jnp.zeros((n, half + rest), jnp.float32)], axis=1)
    s2 = jnp.concatenate([jnp.zeros((n, half), jnp.float32), sin, jnp.zeros((n, rest), jnp.float32)], axis=1)
    reps = 128 // head_dim
    tile = lambda t: jnp.tile(t, (1, reps))
    return tile(c), tile(s1), tile(s2), cos.T, sin.T


def _rope_lanes(t, c, s1, s2, half):
    n = t.shape[1]
    reps = n // 128
    tl = lambda a: jnp.concatenate([a] * reps, axis=1)
    return t * tl(c) + pltpu.roll(t, n - half, 1) * tl(s1) + pltpu.roll(t, half, 1) * tl(s2)


def _rope_rows(t, cT, sT, head_dim, half):
    pieces = []
    for h in range(t.shape[0] // head_dim):
        x1 = t[h * head_dim:h * head_dim + half]
        x2 = t[h * head_dim + half:h * head_dim + 2 * half]
        pieces += [x1 * cT - x2 * sT, x2 * cT + x1 * sT, t[h * head_dim + 2 * half:(h + 1) * head_dim]]
    return jnp.concatenate(pieces, axis=0)


def _dsa_proj_kernel(x_ref, wq_ref, wiq_ref, wv_ref, wvx_ref, wiw_ref, wkT_ref, wikT_ref,
                     c_ref, s1_ref, s2_ref, cT_ref, sT_ref, kng_ref, knb_ref, one_ref,
                     q_ref, iq_ref, v_ref, vx_ref, iw_ref, kT_ref, kTb_ref, ikT_ref, ikTb_ref):
    xb = x_ref[...].astype(MXU_DTYPE)
    c, s1, s2 = c_ref[...], s1_ref[...], s2_ref[...]
    cT, sT = cT_ref[...], sT_ref[...]
    dot = lambda a, b: jnp.dot(a, b, preferred_element_type=jnp.float32)
    dot_t = lambda w, a: lax.dot_general(w, a, (((1,), (1,)), ((), ())), preferred_element_type=jnp.float32)

    q = _rope_lanes(dot(xb, wq_ref[...]), c, s1, s2, ROPE_DIM // 2)
    q_ref[...] = (q * (ATT_HEAD_DIM ** -0.5)).astype(q_ref.dtype)
    iq = _rope_lanes(dot(xb, wiq_ref[...]), c, s1, s2, IDX_ROPE_DIM // 2)
    iq_ref[...] = iq.astype(iq_ref.dtype)
    v_ref[...] = dot(xb, wv_ref[...])
    vx_ref[...] = (dot(xb, wvx_ref[...]) + one_ref[...]).astype(vx_ref.dtype)
    iw_ref[...] = dot(xb, wiw_ref[...]) * (IDX_HEADS ** -0.5 * IDX_DIM ** -0.5)

    kT = _rope_rows(dot_t(wkT_ref[...], xb), cT, sT, ATT_HEAD_DIM, ROPE_DIM // 2)
    kT_ref[0, 0] = kT
    kTb_ref[0, 0] = kT.astype(kTb_ref.dtype)
    ikT = dot_t(wikT_ref[...], xb)
    mu = jnp.mean(ikT, axis=0, keepdims=True)
    ikc = ikT - mu
    var = jnp.mean(ikc * ikc, axis=0, keepdims=True)
    ikT = ikc * lax.rsqrt(var + LN_EPS) * kng_ref[...] + knb_ref[...]
    ikT = _rope_rows(ikT, cT, sT, IDX_DIM, IDX_ROPE_DIM // 2)
    ikT_ref[0, 0] = ikT
    ikTb_ref[0, 0] = ikT.astype(ikTb_ref.dtype)


def _dsa_project(x3d, pos, w_in, kn_g, kn_b):
    b_, t_, _ = x3d.shape
    tk = KEY_GROUP if t_ % KEY_GROUP == 0 else t_
    ng = t_ // tk
    m = b_ * t_
    w_q, w_k, w_v, w_iq, w_ik, w_iw = jnp.split(w_in, list(ATT_IN_SPLITS), axis=1)
    cast = lambda w: w.astype(MXU_DTYPE)
    w_vx = jnp.pad(w_v.reshape(D_MODEL, ATT_KV_HEADS, ATT_HEAD_DIM),
                   ((0, 0), (0, 0), (0, 128 - ATT_HEAD_DIM))).reshape(D_MODEL, ATT_KV_HEADS * 128)
    one_col = jnp.tile((jnp.arange(128) == ATT_HEAD_DIM).astype(jnp.float32), ATT_KV_HEADS)[None, :]
    w_iw_pad = jnp.pad(w_iw, ((0, 0), (0, 128 - IDX_HEADS)))
    c, s1, s2, cT, sT = _rope_lane_tables(pos, ROPE_DIM, ATT_HEAD_DIM)
    full = lambda shape: pl.BlockSpec(shape, lambda b, i: (0,) * len(shape))
    rows = lambda n: pl.BlockSpec((tk, n), lambda b, i: (b * ng + i, 0))
    ptab = lambda n: pl.BlockSpec((tk, n), lambda b, i: (i, 0))
    grp = lambda n: pl.BlockSpec((1, 1, n, tk), lambda b, i: (b, i, 0, 0))
    sds = jax.ShapeDtypeStruct
    return pl.pallas_call(
        _dsa_proj_kernel,
        grid=(b_, ng),
        in_specs=[rows(D_MODEL), full((D_MODEL, ATT_Q_DIM)), full((D_MODEL, IDX_HEADS * IDX_DIM)),
                  full((D_MODEL, ATT_KV_DIM)), full((D_MODEL, ATT_KV_HEADS * 128)), full((D_MODEL, 128)),
                  full((ATT_KV_DIM, D_MODEL)), full((IDX_DIM, D_MODEL)),
                  ptab(128), ptab(128), ptab(128),
                  pl.BlockSpec((ROPE_DIM // 2, tk), lambda b, i: (0, i)),
                  pl.BlockSpec((ROPE_DIM // 2, tk), lambda b, i: (0, i)),
                  full((IDX_DIM, 1)), full((IDX_DIM, 1)), full((1, ATT_KV_HEADS * 128))],
        out_specs=[rows(ATT_Q_DIM), rows(IDX_HEADS * IDX_DIM), rows(ATT_KV_DIM), rows(ATT_KV_HEADS * 128),
                   rows(128), grp(ATT_KV_DIM), grp(ATT_KV_DIM), grp(IDX_DIM), grp(IDX_DIM)],
        out_shape=[sds((m, ATT_Q_DIM), MXU_DTYPE), sds((m, IDX_HEADS * IDX_DIM), MXU_DTYPE),
                   sds((m, ATT_KV_DIM), jnp.float32), sds((m, ATT_KV_HEADS * 128), MXU_DTYPE),
                   sds((m, 128), jnp.float32),
                   sds((b_, ng, ATT_KV_DIM, tk), jnp.float32), sds((b_, ng, ATT_KV_DIM, tk), MXU_DTYPE),
                   sds((b_, ng, IDX_DIM, tk), jnp.float32), sds((b_, ng, IDX_DIM, tk), MXU_DTYPE)],
        compiler_params=pltpu.CompilerParams(
            dimension_semantics=("parallel", "parallel"),
            vmem_limit_bytes=V7X_VMEM_LIMIT_BYTES),
        name="dsa_project",
    )(x3d.reshape(m, D_MODEL), cast(w_q), cast(w_iq), cast(w_v), cast(w_vx), cast(w_iw_pad),
      cast(w_k.T), cast(w_ik.T), c, s1, s2, cT, sT, kn_g.reshape(IDX_DIM, 1), kn_b.reshape(IDX_DIM, 1), one_col)


def _untranspose_groups(tg):
    b_, g_, r_, tk = tg.shape
    return jnp.transpose(tg, (0, 1, 3, 2)).reshape(b_, g_ * tk, r_)


def _dsa_attend_kernel(iq_ref, iw_ref, ikT_ref, q_ref, kT_ref, vx_ref, o_ref, key_ref, m_ref, acc_ref, *,
                       topk, col_bits):
    j = pl.program_id(1)
    tq = iq_ref.shape[0]
    tk = key_ref.shape[2]
    n_groups = (j * tq + tq + tk - 1) // tk
    row = j * tq + lax.broadcasted_iota(jnp.int32, (tq, tk), 0)
    col0 = lax.broadcasted_iota(jnp.int32, (tq, tk), 1)
    dot = lambda a, b: jnp.dot(a, b, preferred_element_type=jnp.float32)

    def score_body(g, carry):
        ikT = ikT_ref[0, g]
        sc = jnp.zeros((tq, tk), jnp.float32)
        for h in range(IDX_HEADS):
            s = dot(iq_ref[:, h * IDX_DIM:(h + 1) * IDX_DIM], ikT)
            sc = sc + iw_ref[:, h:h + 1] * jnp.maximum(s, 0.0)
        bits = pltpu.bitcast(sc, jnp.int32)
        key = jnp.where(bits >= 0, bits, bits ^ jnp.int32(0x7FFFFFFF))
        key_ref[g] = jnp.where(col0 + g * tk <= row, key, jnp.int32(INT32_MIN))
        return carry

    lax.fori_loop(0, n_groups, score_body, 0)

    def bit_body(i, thr):
        cand = thr ^ lax.shift_left(jnp.int32(1), jnp.int32(31) - i)

        def count_body(g, cnt):
            hit = jnp.where(key_ref[g] >= cand, 1.0, 0.0)
            for l in range(tk // 128):
                cnt = cnt + hit[:, l * 128:(l + 1) * 128]
            return cnt

        cnt = lax.fori_loop(0, n_groups, count_body, jnp.zeros((tq, 128), jnp.float32))
        total = jnp.sum(cnt, axis=1, keepdims=True)
        return jnp.where(total >= float(topk), cand, thr)

    thr = lax.fori_loop(0, 32, bit_body, jnp.full((tq, 1), INT32_MIN, jnp.int32))

    def lane_fold(hit, cnt):
        for l in range(tk // 128):
            cnt = cnt + hit[:, l * 128:(l + 1) * 128]
        return cnt

    def above_body(g, cnt):
        return lane_fold(jnp.where(key_ref[g] > thr, 1.0, 0.0), cnt)

    n_above = jnp.sum(lax.fori_loop(0, n_groups, above_body, jnp.zeros((tq, 128), jnp.float32)),
                      axis=1, keepdims=True)
    need = float(topk) - n_above

    def col_body(i, last):
        cand = last | lax.shift_left(jnp.int32(1), jnp.int32(col_bits - 1) - i)

        def tie_body(g, cnt):
            hit = jnp.where((key_ref[g] == thr) & (col0 + g * tk < cand), 1.0, 0.0)
            return lane_fold(hit, cnt)

        ties = jnp.sum(lax.fori_loop(0, n_groups, tie_body, jnp.zeros((tq, 128), jnp.float32)),
                       axis=1, keepdims=True)
        return jnp.where(ties < need, cand, last)

    last_tie = lax.fori_loop(0, col_bits, col_body, jnp.zeros((tq, 1), jnp.int32))

    m_ref[...] = jnp.full(m_ref.shape, MASK_NEG, jnp.float32)
    acc_ref[...] = jnp.zeros(acc_ref.shape, jnp.float32)
    gsz = ATT_HEADS // ATT_KV_HEADS

    def attend_body(g, carry):
        key = key_ref[g]
        col = col0 + g * tk
        keep = (key > thr) | ((key == thr) & (col <= last_tie))
        bias = jnp.where(keep & (col <= row), 0.0, MASK_NEG)
        start = pl.multiple_of(g * tk, tk)
        for h in range(ATT_HEADS):
            kv = h // gsz
            s = dot(q_ref[:, h * ATT_HEAD_DIM:(h + 1) * ATT_HEAD_DIM],
                    kT_ref[0, g, kv * ATT_HEAD_DIM:(kv + 1) * ATT_HEAD_DIM, :]) + bias
            m_old = m_ref[h]
            m_new = jnp.maximum(m_old, jnp.max(s, axis=1, keepdims=True))
            p = jnp.exp(s - m_new).astype(vx_ref.dtype)
            pv = dot(p, vx_ref[0, pl.ds(start, tk), kv * 128:(kv + 1) * 128])
            acc_ref[h] = jnp.exp(m_old - m_new) * acc_ref[h] + pv
            m_ref[h] = m_new
        return carry

    lax.fori_loop(0, n_groups, attend_body, 0)

    for h in range(ATT_HEADS):
        a = acc_ref[h]
        o_ref[:, h * ATT_HEAD_DIM:(h + 1) * ATT_HEAD_DIM] = (
            a[:, :ATT_HEAD_DIM] / a[:, ATT_HEAD_DIM:ATT_HEAD_DIM + 1]).astype(o_ref.dtype)


def _dsa_attend(b_, t_, q, iq, iw, ikTb, kTb, vx):
    ng, tk = kTb.shape[1], kTb.shape[3]
    tq = Q_BLOCK
    nq = t_ // tq
    rows = lambda n: pl.BlockSpec((tq, n), lambda b, j: (b * nq + j, 0))
    return pl.pallas_call(
        functools.partial(_dsa_attend_kernel, topk=min(TOPK_MAX, t_ // 4), col_bits=max(1, (t_ - 1).bit_length())),
        grid=(b_, nq),
        in_specs=[rows(IDX_HEADS * IDX_DIM), rows(128),
                  pl.BlockSpec((1, ng, IDX_DIM, tk), lambda b, j: (b, 0, 0, 0)),
                  rows(ATT_Q_DIM),
                  pl.BlockSpec((1, ng, ATT_KV_DIM, tk), lambda b, j: (b, 0, 0, 0)),
                  pl.BlockSpec((1, t_, ATT_KV_HEADS * 128), lambda b, j: (b, 0, 0))],
        out_specs=rows(ATT_Q_DIM),
        out_shape=jax.ShapeDtypeStruct((b_ * t_, ATT_Q_DIM), MXU_DTYPE),
        scratch_shapes=[pltpu.VMEM((ng, tq, tk), jnp.int32),
                        pltpu.VMEM((ATT_HEADS, tq, 1), jnp.float32),
                        pltpu.VMEM((ATT_HEADS, tq, 128), jnp.float32)],
        compiler_params=pltpu.CompilerParams(
            dimension_semantics=("parallel", "arbitrary"),
            vmem_limit_bytes=V7X_VMEM_LIMIT_BYTES),
        name="dsa_attend",
    )(iq, iw, ikTb, q, kTb, vx.reshape(b_, t_, ATT_KV_HEADS * 128))


def _dsa_proj_q_lanes_kernel(x_ref, wqT_ref, wiqT_ref, wiwT_ref, wk_ref, wv_ref, wvxT_ref, wik_ref,
                             c_ref, s1_ref, s2_ref, cT_ref, sT_ref, kng_ref, knb_ref, onerow_ref,
                             qT_ref, iqT_ref, iwT_ref, k_ref, khd_ref, v_ref, vxT_ref, ik_ref, ikb_ref):
    xb = x_ref[...].astype(MXU_DTYPE)
    tm = xb.shape[0]
    c, s1, s2 = c_ref[...], s1_ref[...], s2_ref[...]
    cT, sT = cT_ref[...], sT_ref[...]
    dot = lambda a, b: jnp.dot(a, b, preferred_element_type=jnp.float32)
    dot_t = lambda w, a: lax.dot_general(w, a, (((1,), (1,)), ((), ())), preferred_element_type=jnp.float32)

    qT = _rope_rows(dot_t(wqT_ref[...], xb), cT, sT, ATT_HEAD_DIM, ROPE_DIM // 2) * (ATT_HEAD_DIM ** -0.5)
    iqT = _rope_rows(dot_t(wiqT_ref[...], xb), cT, sT, IDX_DIM, IDX_ROPE_DIM // 2)
    iwT = dot_t(wiwT_ref[...], xb) * (IDX_HEADS ** -0.5 * IDX_DIM ** -0.5)
    for t in range(tm // Q_BLOCK):
        lanes = slice(t * Q_BLOCK, (t + 1) * Q_BLOCK)
        qT_ref[0, t] = qT[:, lanes].astype(qT_ref.dtype)
        iqT_ref[0, t] = iqT[:, lanes].astype(iqT_ref.dtype)
        iwT_ref[0, t] = iwT[:, lanes]

    k = _rope_lanes(dot(xb, wk_ref[...]), c, s1, s2, ROPE_DIM // 2)
    k_ref[...] = k
    for g in range(ATT_KV_HEADS):
        khd_ref[g] = k[:, g * ATT_HEAD_DIM:(g + 1) * ATT_HEAD_DIM].astype(khd_ref.dtype)
    v_ref[...] = dot(xb, wv_ref[...])
    vxT_ref[0, 0] = (dot_t(wvxT_ref[...], xb) + onerow_ref[...]).astype(vxT_ref.dtype)

    ik = dot(xb, wik_ref[...])
    real = lax.broadcasted_iota(jnp.int32, ik.shape, 1) < IDX_DIM
    mu = jnp.sum(ik, axis=-1, keepdims=True) * (1.0 / IDX_DIM)
    ikc = jnp.where(real, ik - mu, 0.0)
    var = jnp.sum(ikc * ikc, axis=-1, keepdims=True) * (1.0 / IDX_DIM)
    ikn = _rope_lanes(ikc * lax.rsqrt(var + LN_EPS) * kng_ref[...] + knb_ref[...], c, s1, s2, IDX_ROPE_DIM // 2)
    ik_ref[...] = ikn[:, :IDX_DIM]
    ikb_ref[...] = ikn[:, :IDX_DIM].astype(ikb_ref.dtype)


def _dsa_project_q_lanes(x3d, pos, w_in, kn_g, kn_b):
    b_, t_, _ = x3d.shape
    tk = KEY_GROUP
    ng = t_ // tk
    nq = tk // Q_BLOCK
    m = b_ * t_
    w_q, w_k, w_v, w_iq, w_ik, w_iw = jnp.split(w_in, list(ATT_IN_SPLITS), axis=1)
    cast = lambda w: w.astype(MXU_DTYPE)
    w_vxT = jnp.pad(w_v.T.reshape(ATT_KV_HEADS, ATT_HEAD_DIM, D_MODEL),
                    ((0, 0), (0, 128 - ATT_HEAD_DIM), (0, 0))).reshape(ATT_KV_HEADS * 128, D_MODEL)
    one_row = jnp.tile((jnp.arange(128) == ATT_HEAD_DIM).astype(jnp.float32), ATT_KV_HEADS)[:, None]
    pad_lanes = lambda a: jnp.pad(a, ((0, 0), (0, 128 - a.shape[1])))
    c, s1, s2, cT, sT = _rope_lane_tables(pos, ROPE_DIM, ATT_HEAD_DIM)
    full = lambda shape: pl.BlockSpec(shape, lambda b, i: (0,) * len(shape))
    rows = lambda n: pl.BlockSpec((tk, n), lambda b, i: (b * ng + i, 0))
    ptab = lambda n: pl.BlockSpec((tk, n), lambda b, i: (i, 0))
    qtile = lambda n: pl.BlockSpec((1, nq, n, Q_BLOCK), lambda b, i: (b, i, 0, 0))
    sds = jax.ShapeDtypeStruct
    return pl.pallas_call(
        _dsa_proj_q_lanes_kernel,
        grid=(b_, ng),
        in_specs=[rows(D_MODEL), full((ATT_Q_DIM, D_MODEL)), full((IDX_HEADS * IDX_DIM, D_MODEL)),
                  full((IDX_HEADS, D_MODEL)), full((D_MODEL, ATT_KV_DIM)), full((D_MODEL, ATT_KV_DIM)),
                  full((ATT_KV_HEADS * 128, D_MODEL)), full((D_MODEL, 128)),
                  ptab(128), ptab(128), ptab(128),
                  pl.BlockSpec((ROPE_DIM // 2, tk), lambda b, i: (0, i)),
                  pl.BlockSpec((ROPE_DIM // 2, tk), lambda b, i: (0, i)),
                  full((1, 128)), full((1, 128)), full((ATT_KV_HEADS * 128, 1))],
        out_specs=[qtile(ATT_Q_DIM), qtile(IDX_HEADS * IDX_DIM), qtile(IDX_HEADS),
                   rows(ATT_KV_DIM), pl.BlockSpec((ATT_KV_HEADS, tk, ATT_HEAD_DIM), lambda b, i: (0, b * ng + i, 0)),
                   rows(ATT_KV_DIM), pl.BlockSpec((1, 1, ATT_KV_HEADS * 128, tk), lambda b, i: (b, i, 0, 0)),
                   rows(IDX_DIM), rows(IDX_DIM)],
        out_shape=[sds((b_, t_ // Q_BLOCK, ATT_Q_DIM, Q_BLOCK), MXU_DTYPE),
                   sds((b_, t_ // Q_BLOCK, IDX_HEADS * IDX_DIM, Q_BLOCK), MXU_DTYPE),
                   sds((b_, t_ // Q_BLOCK, IDX_HEADS, Q_BLOCK), jnp.float32),
                   sds((m, ATT_KV_DIM), jnp.float32), sds((ATT_KV_HEADS, m, ATT_HEAD_DIM), MXU_DTYPE),
                   sds((m, ATT_KV_DIM), jnp.float32), sds((b_, ng, ATT_KV_HEADS * 128, tk), MXU_DTYPE),
                   sds((m, IDX_DIM), jnp.float32), sds((m, IDX_DIM), MXU_DTYPE)],
        compiler_params=pltpu.CompilerParams(
            dimension_semantics=("parallel", "parallel"),
            vmem_limit_bytes=V7X_VMEM_LIMIT_BYTES),
        name="dsa_project_q_lanes",
    )(x3d.reshape(m, D_MODEL), cast(w_q.T), cast(w_iq.T), cast(w_iw.T), cast(w_k), cast(w_v), cast(w_vxT),
      cast(pad_lanes(w_ik)), c, s1, s2, cT, sT, pad_lanes(kn_g.reshape(1, IDX_DIM)),
      pad_lanes(kn_b.reshape(1, IDX_DIM)), one_row)


def _tree_sum(parts):
    while len(parts) > 1:
        parts = [parts[i] + parts[i + 1] for i in range(0, len(parts) - 1, 2)] + (
            [parts[-1]] if len(parts) % 2 else [])
    return parts[0]


def _dsa_attend_q_lanes_kernel(iqT_ref, iwT_ref, ik_ref, qT_ref, k_ref, vxT_ref, o_ref,
                               key_ref, bias_ref, m_ref, acc_ref, *, topk, col_bits):
    j = pl.program_id(1)
    tk, tq = key_ref.shape[1], key_ref.shape[2]
    n_groups = (j * tq + tq + tk - 1) // tk
    qpos = j * tq + lax.broadcasted_iota(jnp.int32, (tk, tq), 1)
    kpos0 = lax.broadcasted_iota(jnp.int32, (tk, tq), 0)
    dot = lambda a, b: jnp.dot(a, b, preferred_element_type=jnp.float32)

    def score_body(g, carry):
        start = pl.multiple_of(g * tk, tk)
        w_iq = jnp.concatenate([iqT_ref[0, 0, h * IDX_DIM:(h + 1) * IDX_DIM, :] for h in range(IDX_HEADS)], axis=1)
        s_all = dot(ik_ref[0, pl.ds(start, tk), :], w_iq)
        sc = _tree_sum([iwT_ref[0, 0, h:h + 1, :] * jnp.maximum(s_all[:, h * tq:(h + 1) * tq], 0.0)
                        for h in range(IDX_HEADS)])
        key_ref[g] = jnp.where(kpos0 + g * tk <= qpos, _sortable_key(sc), jnp.int32(INT32_MIN))
        return carry

    lax.fori_loop(0, n_groups, score_body, 0)

    def count_keys(pred):
        def body(g, part):
            hit = jnp.where(pred(key_ref[g], kpos0 + g * tk), 1.0, 0.0)
            return part + _tree_sum([hit[r * SUBLANES:(r + 1) * SUBLANES] for r in range(tk // SUBLANES)])
        part = lax.fori_loop(0, n_groups, body, jnp.zeros((SUBLANES, tq), jnp.float32))
        return jnp.sum(part, axis=0, keepdims=True)

    def bit_body(i, thr):
        cand = thr ^ lax.shift_left(jnp.int32(1), jnp.int32(31) - i)
        return jnp.where(count_keys(lambda k, kp: k >= cand) >= float(topk), cand, thr)

    thr = lax.fori_loop(0, 32, bit_body, jnp.full((1, tq), INT32_MIN, jnp.int32))

    need = float(topk) - count_keys(lambda k, kp: k > thr)

    def pos_body(i, last):
        cand = last | lax.shift_left(jnp.int32(1), jnp.int32(col_bits - 1) - i)
        return jnp.where(count_keys(lambda k, kp: (k == thr) & (kp < cand)) < need, cand, last)

    n_tied = count_keys(lambda k, kp: k == thr)
    excess = jnp.max(jnp.where(n_tied > need, 1.0, 0.0), axis=1, keepdims=True)
    last_tie = lax.cond(excess[0, 0] > 0.0,
                        lambda: lax.fori_loop(0, col_bits, pos_body, jnp.zeros((1, tq), jnp.int32)),
                        lambda: jnp.full((1, tq), 2 ** col_bits - 1, jnp.int32))

    m_ref[...] = jnp.full(m_ref.shape, MASK_NEG, jnp.float32)
    acc_ref[...] = jnp.zeros(acc_ref.shape, jnp.float32)
    gsz = ATT_HEADS // ATT_KV_HEADS

    def attend_body(g, carry):
        start = pl.multiple_of(g * tk, tk)
        key = key_ref[g]
        kpos = kpos0 + g * tk
        keep = (key > thr) | ((key == thr) & (kpos <= last_tie))
        bias_ref[...] = jnp.where(keep & (kpos <= qpos), 0.0, MASK_NEG)
        logits = []
        for kv in range(ATT_KV_HEADS):
            w_q = jnp.concatenate([qT_ref[0, 0, (kv * gsz + i) * ATT_HEAD_DIM:(kv * gsz + i + 1) * ATT_HEAD_DIM, :]
                                   for i in range(gsz)], axis=1)
            logits.append(dot(k_ref[kv, pl.ds(start, tk), :], w_q))
        for kv in range(ATT_KV_HEADS):
            s = logits[kv] + jnp.concatenate([bias_ref[...]] * gsz, axis=1)
            m_old = m_ref[kv]
            m_new = jnp.maximum(m_old, jnp.max(s, axis=0, keepdims=True))
            p = jnp.exp(s - m_new).astype(vxT_ref.dtype)
            pv = dot(vxT_ref[0, g, kv * 128:(kv + 1) * 128, :], p)
            acc_ref[kv] = jnp.exp(m_old - m_new) * acc_ref[kv] + pv
            m_ref[kv] = m_new
        return carry

    lax.fori_loop(0, n_groups, attend_body, 0)

    for h in range(ATT_HEADS):
        a = acc_ref[h // gsz, :, (h % gsz) * tq:(h % gsz + 1) * tq]
        o = (a / a[ATT_HEAD_DIM:ATT_HEAD_DIM + 1, :]).T
        o_ref[:, h * ATT_HEAD_DIM:(h + 1) * ATT_HEAD_DIM] = o[:, :ATT_HEAD_DIM].astype(o_ref.dtype)


def _dsa_attend_q_lanes(b_, t_, qT, iqT, iwT, ikb, khd, vxT):
    ng, tk = vxT.shape[1], vxT.shape[3]
    tq = Q_BLOCK
    nq = t_ // tq
    qtile = lambda n: pl.BlockSpec((1, 1, n, tq), lambda b, j: (b, j, 0, 0))
    return pl.pallas_call(
        functools.partial(_dsa_attend_q_lanes_kernel, topk=min(TOPK_MAX, t_ // 4),
                          col_bits=max(1, (t_ - 1).bit_length())),
        grid=(b_, nq),
        in_specs=[qtile(IDX_HEADS * IDX_DIM), qtile(IDX_HEADS),
                  pl.BlockSpec((1, t_, IDX_DIM), lambda b, j: (b, 0, 0)),
                  qtile(ATT_Q_DIM),
                  pl.BlockSpec((ATT_KV_HEADS, t_, ATT_HEAD_DIM), lambda b, j: (0, b, 0)),
                  pl.BlockSpec((1, ng, ATT_KV_HEADS * 128, tk), lambda b, j: (b, 0, 0, 0))],
        out_specs=pl.BlockSpec((tq, ATT_Q_DIM), lambda b, j: (b * nq + j, 0)),
        out_shape=jax.ShapeDtypeStruct((b_ * t_, ATT_Q_DIM), MXU_DTYPE),
        scratch_shapes=[pltpu.VMEM((ng, tk, tq), jnp.int32),
                        pltpu.VMEM((tk, tq), jnp.float32),
                        pltpu.VMEM((ATT_KV_HEADS, 1, tq * (ATT_HEADS // ATT_KV_HEADS)), jnp.float32),
                        pltpu.VMEM((ATT_KV_HEADS, 128, tq * (ATT_HEADS // ATT_KV_HEADS)), jnp.float32)],
        compiler_params=pltpu.CompilerParams(
            dimension_semantics=("parallel", "arbitrary"),
            vmem_limit_bytes=V7X_VMEM_LIMIT_BYTES),
        name="dsa_attend_q_lanes",
    )(iqT, iwT, ikb.reshape(b_, t_, IDX_DIM), qT, khd, vxT)


def _dsa_prompt_pallas(x3d, w_in, kn_g, kn_b):
    b_, t_, _ = x3d.shape
    qT, iqT, iwT, k, khd, v, vxT, ik, ikb = _dsa_project_q_lanes(x3d, jnp.arange(t_), w_in, kn_g, kn_b)
    o = _dsa_attend_q_lanes(b_, t_, qT, iqT, iwT, ikb, khd, vxT)
    kv4 = lambda u: u.reshape(b_, t_, ATT_KV_HEADS, ATT_HEAD_DIM)
    return o, kv4(k), kv4(v), ik.reshape(b_, t_, IDX_DIM)


RW_ROW_TILE = 256


def _rwkv_proj_kernel(x_ref, xp_ref, mu_ref, wr_ref, wk_ref, wv_ref, w1_ref, w2_ref, a1_ref, a2_ref,
                      g1_ref, g2_ref, w0_ref, a0_ref, r_ref, d_ref, k_ref, v_ref, a_ref, g_ref):
    x = x_ref[...]
    dx = xp_ref[...] - x
    mix = lambda c: (x + dx * mu_ref[c:c + 1, :]).astype(MXU_DTYPE)
    dot = lambda a, b: jnp.dot(a.astype(MXU_DTYPE), b, preferred_element_type=jnp.float32)
    r_ref[...] = dot(mix(0), wr_ref[...])
    lora_w = dot(jnp.tanh(dot(mix(1), w1_ref[...])), w2_ref[...])
    w_log = -jax.nn.softplus(-(w0_ref[...] + lora_w)) - 0.5
    d_ref[...] = jnp.exp(-jnp.exp(w_log))
    k_ref[...] = dot(mix(2), wk_ref[...])
    v_ref[...] = dot(mix(3), wv_ref[...])
    a_ref[...] = jax.nn.sigmoid(a0_ref[...] + dot(dot(mix(4), a1_ref[...]), a2_ref[...]))
    g_ref[...] = dot(jax.nn.sigmoid(dot(mix(5), g1_ref[...])), g2_ref[...])


def _rwkv_project(x2d, xprev2d, mu, w_r, w_k, w_v, w0, w1, w2, a0, a1, a2, g1, g2):
    m = x2d.shape[0]
    tm = RW_ROW_TILE if m % RW_ROW_TILE == 0 else m
    cast = lambda w: w.astype(MXU_DTYPE)
    full = lambda a: pl.BlockSpec(a.shape, lambda i: (0,) * a.ndim)
    rows = pl.BlockSpec((tm, D_MODEL), lambda i: (i, 0))
    consts = [mu, cast(w_r), cast(w_k), cast(w_v), cast(w1), cast(w2), cast(a1), cast(a2), cast(g1), cast(g2),
              w0.reshape(1, D_MODEL), a0.reshape(1, D_MODEL)]
    return pl.pallas_call(
        _rwkv_proj_kernel,
        grid=(m // tm,),
        in_specs=[rows, rows] + [full(a) for a in consts],
        out_specs=[rows] * 6,
        out_shape=[jax.ShapeDtypeStruct((m, D_MODEL), jnp.float32)] * 6,
        compiler_params=pltpu.CompilerParams(
            dimension_semantics=("parallel",),
            vmem_limit_bytes=V7X_VMEM_LIMIT_BYTES),
        name="rwkv_project",
    )(x2d, xprev2d, *consts)


RW_LANES = 128
RW_TIME_CHUNK = 32


def _rwkv_scan_kernel(r_ref, d_ref, k_ref, v_ref, a_ref, s0_ref, kk_ref, ka_ref, rk_ref, gg_ref, gb_ref,
                      z_ref, s_out_ref, s_ref, vec_ref):
    c = pl.program_id(1)
    n = RW_HEAD

    @pl.when(c == 0)
    def _():
        s_ref[...] = s0_ref[...]

    def step(t, carry):
        r, k, v, a = r_ref[t], k_ref[t], v_ref[t], a_ref[t]
        kkr = k * kk_ref[...]
        nrm = jnp.sqrt(jnp.sum(kkr * kkr, axis=0, keepdims=True))
        kk = kkr / jnp.maximum(nrm, 1e-12)
        kmod = k * (1.0 + (a - 1.0) * ka_ref[...])
        vec_ref[0] = kk
        vec_ref[1] = d_ref[t]
        vec_ref[2] = kk * a
        vec_ref[3] = kmod
        vec_ref[4] = r
        row = lambda q, j: vec_ref[q, j:j + 1, :]
        sa = jnp.zeros((n, RW_LANES), jnp.float32)
        for j in range(n):
            sa = sa + s_ref[j] * row(0, j)
        y = jnp.zeros((n, RW_LANES), jnp.float32)
        for j in range(n):
            sn = s_ref[j] * row(1, j) - sa * row(2, j) + v * row(3, j)
            s_ref[j] = sn
            y = y + sn * row(4, j)
        mu = jnp.mean(y, axis=0, keepdims=True)
        yc = y - mu
        var = jnp.mean(yc * yc, axis=0, keepdims=True)
        bonus = jnp.sum(r * kmod * rk_ref[...], axis=0, keepdims=True)
        z_ref[t] = yc * lax.rsqrt(var + RW_GN_EPS) * gg_ref[...] + gb_ref[...] + bonus * v
        return carry

    lax.fori_loop(0, r_ref.shape[0], step, 0)

    @pl.when(c == pl.num_programs(1) - 1)
    def _():
        s_out_ref[...] = s_ref[...]


def _rwkv_scan(rT, dT, kT, vT, aT, s0T, k_k, k_a, r_k, gn_g, gn_b):
    t_, n, bh = rT.shape
    tc = RW_TIME_CHUNK if t_ % RW_TIME_CHUNK == 0 else t_
    reps = RW_LANES // RW_HEADS
    table = lambda p: jnp.tile(p.reshape(RW_HEADS, n).T, (1, reps))
    seq = pl.BlockSpec((tc, n, RW_LANES), lambda l, c: (c, 0, l))
    state = pl.BlockSpec((n, n, RW_LANES), lambda l, c: (0, 0, l))
    tab = pl.BlockSpec((n, RW_LANES), lambda l, c: (0, 0))
    return pl.pallas_call(
        _rwkv_scan_kernel,
        grid=(bh // RW_LANES, t_ // tc),
        in_specs=[seq] * 5 + [state] + [tab] * 5,
        out_specs=[seq, state],
        out_shape=[jax.ShapeDtypeStruct((t_, n, bh), jnp.float32),
                   jax.ShapeDtypeStruct((n, n, bh), jnp.float32)],
        scratch_shapes=[pltpu.VMEM((n, n, RW_LANES), jnp.float32),
                        pltpu.VMEM((5, n, RW_LANES), jnp.float32)],
        compiler_params=pltpu.CompilerParams(
            dimension_semantics=("parallel", "arbitrary"),
            vmem_limit_bytes=V7X_VMEM_LIMIT_BYTES),
        name="rwkv_scan",
    )(rT, dT, kT, vT, aT, s0T, table(k_k), table(k_a), table(r_k), table(gn_g), table(gn_b))


def _rwkv7_mixer_pallas(x3d, shift, wkv, mu, w_r, w_k, w_v, w0, w1, w2, a0, a1, a2, g1, g2,
                        k_k, k_a, r_k, gn_g, gn_b):
    b_, t_, _ = x3d.shape
    m = b_ * t_
    x_prev = jnp.concatenate([shift[:, None, :], x3d[:, :-1]], axis=1)
    r, d, k, v, a, g = _rwkv_project(x3d.reshape(m, D_MODEL), x_prev.reshape(m, D_MODEL), mu,
                                     w_r, w_k, w_v, w0, w1, w2, a0, a1, a2, g1, g2)
    to_scan = lambda u: jnp.transpose(u.reshape(b_, t_, RW_HEADS, RW_HEAD), (1, 3, 0, 2)).reshape(
        t_, RW_HEAD, b_ * RW_HEADS)
    s0T = jnp.transpose(wkv.astype(jnp.float32), (3, 2, 0, 1)).reshape(RW_HEAD, RW_HEAD, b_ * RW_HEADS)
    zT, sT = _rwkv_scan(to_scan(r), to_scan(d), to_scan(k), to_scan(v), to_scan(a), s0T,
                        k_k, k_a, r_k, gn_g, gn_b)
    z = jnp.transpose(zT.reshape(t_, RW_HEAD, b_, RW_HEADS), (2, 0, 3, 1)).reshape(m, D_MODEL)
    s_new = jnp.transpose(sT.reshape(RW_HEAD, RW_HEAD, b_, RW_HEADS), (2, 3, 1, 0)).astype(wkv.dtype)
    return z, g, x3d[:, -1], s_new


def _proj_gate_ln_kernel(x_ref, h_ref, gate_ref, w_ref, g_ref, b_ref, o_ref):
    h = (h_ref[...] * gate_ref[...]).astype(MXU_DTYPE)
    y = ALPHA * x_ref[...] + jnp.dot(h, w_ref[...], preferred_element_type=jnp.float32)
    o_ref[...] = _ln_rows(y, g_ref[...], b_ref[...])


def _proj_gate_post_norm(x2d, h2d, gate2d, w_out, g, b):
    m = x2d.shape[0]
    tm = _row_tile(m)
    rows = pl.BlockSpec((tm, D_MODEL), lambda i: (i, 0))
    vec = pl.BlockSpec((1, D_MODEL), lambda i: (0, 0))
    return pl.pallas_call(
        _proj_gate_ln_kernel,
        grid=(m // tm,),
        in_specs=[rows, rows, rows, pl.BlockSpec((D_MODEL, D_MODEL), lambda i: (0, 0)), vec, vec],
        out_specs=rows,
        out_shape=jax.ShapeDtypeStruct((m, D_MODEL), jnp.float32),
        compiler_params=pltpu.CompilerParams(
            dimension_semantics=("parallel",),
            vmem_limit_bytes=V7X_VMEM_LIMIT_BYTES),
        name="proj_gate_post_norm",
    )(x2d, h2d, gate2d, w_out, g.reshape(1, D_MODEL), b.reshape(1, D_MODEL))


GM_ROW_TILE = 256


def _gmlp_kernel(x_ref, win_ref, lng_ref, lnb_ref, mixw_ref, mixb_ref, wout_ref, g_ref, b_ref, *out_refs,
                 chunk_len, emit_v):
    x = x_ref[...]
    h = jax.nn.gelu(jnp.dot(x.astype(MXU_DTYPE), win_ref[...], preferred_element_type=jnp.float32))
    u = h[:, :GM_WIDTH]
    v = _ln_rows(h[:, GM_WIDTH:], lng_ref[...], lnb_ref[...])
    if emit_v:
        out_refs[1][...] = v
    if chunk_len == 1:
        gated = u * (v * mixw_ref[...] + mixb_ref[...])
    else:
        tm = x.shape[0]
        causal = (lax.broadcasted_iota(jnp.int32, (chunk_len, chunk_len), 0)
                  >= lax.broadcasted_iota(jnp.int32, (chunk_len, chunk_len), 1))
        vb = v.astype(MXU_DTYPE)
        cols = []
        for g in range(GM_GROUPS):
            w = jnp.where(causal, mixw_ref[g], 0.0).astype(MXU_DTYPE)
            bias = mixb_ref[:, g:g + 1]
            lanes = slice(g * GM_GROUP_DIM, (g + 1) * GM_GROUP_DIM)
            rows = [jnp.dot(w, vb[c * chunk_len:(c + 1) * chunk_len, lanes],
                            preferred_element_type=jnp.float32) + bias
                    for c in range(tm // chunk_len)]
            cols.append(jnp.concatenate(rows, axis=0))
        gated = u * jnp.concatenate(cols, axis=1)
    y = ALPHA * x + jnp.dot(gated.astype(MXU_DTYPE), wout_ref[...], preferred_element_type=jnp.float32)
    out_refs[0][...] = _ln_rows(y, g_ref[...], b_ref[...])


def _gmlp_block(x2d, seq_len, w_in, ln_g, ln_b, ws, bs, w_out, g, b, emit_v):
    m = x2d.shape[0]
    chunk_len = min(seq_len, CHUNK)
    if chunk_len == 1:
        tm = m
        mixw = jnp.repeat(ws[:, 0, 0], GM_GROUP_DIM)[None, :]
        mixb = jnp.repeat(bs[:, 0], GM_GROUP_DIM)[None, :]
    else:
        tm = GM_ROW_TILE
        mixw = ws[:, :chunk_len, :chunk_len]
        mixb = bs[:, :chunk_len].T
    full = lambda a: pl.BlockSpec(a.shape, lambda i: (0,) * a.ndim)
    rows = lambda n: pl.BlockSpec((tm, n), lambda i: (i, 0))
    consts = [w_in.astype(MXU_DTYPE), ln_g.reshape(1, GM_WIDTH), ln_b.reshape(1, GM_WIDTH), mixw, mixb,
              w_out.astype(MXU_DTYPE), g.reshape(1, D_MODEL), b.reshape(1, D_MODEL)]
    out_specs = [rows(D_MODEL)] + ([rows(GM_WIDTH)] if emit_v else [])
    out_shape = [jax.ShapeDtypeStruct((m, D_MODEL), jnp.float32)] + (
        [jax.ShapeDtypeStruct((m, GM_WIDTH), jnp.float32)] if emit_v else [])
    return pl.pallas_call(
        functools.partial(_gmlp_kernel, chunk_len=chunk_len, emit_v=emit_v),
        grid=(m // tm,),
        in_specs=[rows(D_MODEL)] + [full(a) for a in consts],
        out_specs=out_specs,
        out_shape=out_shape,
        compiler_params=pltpu.CompilerParams(
            dimension_semantics=("parallel",),
            vmem_limit_bytes=V7X_VMEM_LIMIT_BYTES),
        name="gmlp_block",
    )(x2d, *consts)


SSM_ROW_TILE = 256
SSM_BC_DIM = SSM_GROUPS * SSM_STATE
SSM_DT_LANES = 128
SUBLANES = 8


def _ssm_activate(xb, xbc, taps, wz_ref, wdt_ref, cw_ref, cb_ref, dtb_ref, z_ref, xs_ref, bm_ref, cm_ref, dt_ref):
    conv = cb_ref[...] + xbc * cw_ref[SSM_CONV - 1:SSM_CONV, :]
    for j in range(SSM_CONV - 1):
        conv = conv + taps[j] * cw_ref[j:j + 1, :]
    act = conv * jax.nn.sigmoid(conv)
    xs_ref[...] = act[:, :SSM_D_INNER]
    bm_ref[...] = act[:, SSM_D_INNER:SSM_D_INNER + SSM_BC_DIM].astype(bm_ref.dtype)
    cm_ref[...] = act[:, SSM_D_INNER + SSM_BC_DIM:].astype(cm_ref.dtype)
    z_ref[...] = jnp.dot(xb, wz_ref[...], preferred_element_type=jnp.float32)
    dt_ref[...] = jax.nn.softplus(jnp.dot(xb, wdt_ref[...], preferred_element_type=jnp.float32) + dtb_ref[...])


def _ssm_proj_seq_kernel(x_ref, halo_ref, cs_ref, wx_ref, wz_ref, wdt_ref, cw_ref, cb_ref, dtb_ref,
                         z_ref, xs_ref, bm_ref, cm_ref, dt_ref, tail_ref):
    i = pl.program_id(1)
    xb = x_ref[...].astype(MXU_DTYPE)
    xbc = jnp.dot(xb, wx_ref[...], preferred_element_type=jnp.float32)
    tm = xbc.shape[0]
    prev = jnp.dot(halo_ref[...].astype(MXU_DTYPE), wx_ref[...], preferred_element_type=jnp.float32)
    prev = jnp.where(i == 0, cs_ref[0], prev)
    row = lax.broadcasted_iota(jnp.int32, (tm, 1), 0)
    pad = jnp.zeros((tm - SUBLANES, xbc.shape[1]), jnp.float32)
    taps = []
    for j in range(SSM_CONV - 1):
        back = SSM_CONV - 1 - j
        head = jnp.concatenate([pltpu.roll(prev, back, 0), pad], axis=0)
        taps.append(jnp.where(row < back, head, pltpu.roll(xbc, back, 0)))
    _ssm_activate(xb, xbc, taps, wz_ref, wdt_ref, cw_ref, cb_ref, dtb_ref, z_ref, xs_ref, bm_ref, cm_ref, dt_ref)
    tail_ref[0] = xbc[tm - SUBLANES:, :]


def _ssm_proj_step_kernel(x_ref, st_ref, wx_ref, wz_ref, wdt_ref, cw_ref, cb_ref, dtb_ref,
                          z_ref, xs_ref, bm_ref, cm_ref, dt_ref, st_out_ref):
    xb = x_ref[...].astype(MXU_DTYPE)
    xbc = jnp.dot(xb, wx_ref[...], preferred_element_type=jnp.float32)
    taps = [st_ref[j] for j in range(SSM_CONV - 1)]
    _ssm_activate(xb, xbc, taps, wz_ref, wdt_ref, cw_ref, cb_ref, dtb_ref, z_ref, xs_ref, bm_ref, cm_ref, dt_ref)
    for j in range(SSM_CONV - 2):
        st_out_ref[j] = st_ref[j + 1]
    st_out_ref[SSM_CONV - 2] = xbc


def _ssm_project(x3d, conv_state, w_in, conv_w, conv_b, dt_bias):
    b_, t_, _ = x3d.shape
    m = b_ * t_
    w_z, w_x, w_dt = jnp.split(w_in, [SSM_D_INNER, SSM_D_INNER + SSM_CONV_DIM], axis=1)
    cast = lambda w: w.astype(MXU_DTYPE)
    consts = [cast(w_x), cast(w_z), cast(jnp.pad(w_dt, ((0, 0), (0, SSM_DT_LANES - SSM_HEADS)))),
              conv_w, conv_b.reshape(1, SSM_CONV_DIM),
              jnp.pad(dt_bias, (0, SSM_DT_LANES - SSM_HEADS)).reshape(1, SSM_DT_LANES)]
    sds = jax.ShapeDtypeStruct
    outs = [sds((m, SSM_D_INNER), jnp.float32), sds((m, SSM_D_INNER), jnp.float32),
            sds((m, SSM_BC_DIM), MXU_DTYPE), sds((m, SSM_BC_DIM), MXU_DTYPE), sds((m, SSM_DT_LANES), jnp.float32)]
    widths = [SSM_D_INNER, SSM_D_INNER, SSM_BC_DIM, SSM_BC_DIM, SSM_DT_LANES]
    params = dict(vmem_limit_bytes=V7X_VMEM_LIMIT_BYTES)
    x2d = x3d.reshape(m, D_MODEL)
    if t_ == 1:
        full = lambda a: pl.BlockSpec(a.shape, lambda i: (0,) * a.ndim)
        st = jnp.transpose(conv_state, (1, 0, 2))
        res = pl.pallas_call(
            _ssm_proj_step_kernel,
            grid=(1,),
            in_specs=[full(x2d), full(st)] + [full(a) for a in consts],
            out_specs=[pl.BlockSpec((m, w), lambda i: (0, 0)) for w in widths] + [full(st)],
            out_shape=outs + [sds(st.shape, jnp.float32)],
            compiler_params=pltpu.CompilerParams(dimension_semantics=("arbitrary",), **params),
            name="ssm_project_step",
        )(x2d, st, *consts)
        return list(res[:5]) + [jnp.transpose(res[5], (1, 0, 2))]
    tm = SSM_ROW_TILE
    nt = t_ // tm
    full = lambda a: pl.BlockSpec(a.shape, lambda b, i: (0,) * a.ndim)
    rows = lambda w: pl.BlockSpec((tm, w), lambda b, i: (b * nt + i, 0))
    halo = pl.BlockSpec((SUBLANES, D_MODEL), lambda b, i: (jnp.maximum((b * nt + i) * (tm // SUBLANES) - 1, 0), 0))
    cs8 = jnp.pad(conv_state, ((0, 0), (SUBLANES - (SSM_CONV - 1), 0), (0, 0)))
    tail = pl.BlockSpec((1, SUBLANES, SSM_CONV_DIM), lambda b, i: (b, 0, 0))
    res = pl.pallas_call(
        _ssm_proj_seq_kernel,
        grid=(b_, nt),
        in_specs=[rows(D_MODEL), halo, tail] + [full(a) for a in consts],
        out_specs=[rows(w) for w in widths] + [tail],
        out_shape=outs + [sds((b_, SUBLANES, SSM_CONV_DIM), jnp.float32)],
        compiler_params=pltpu.CompilerParams(dimension_semantics=("parallel", "arbitrary"), **params),
        name="ssm_project_seq",
    )(x2d, x2d, cs8, *consts)
    return list(res[:5]) + [res[5][:, SUBLANES - (SSM_CONV - 1):, :]]


def _ssm_gate_norm(y, xs, z, dskip, normg):
    yg = (y + xs * dskip) * (z * jax.nn.sigmoid(z))
    gw = SSM_D_INNER // SSM_GROUPS
    outs = []
    for g in range(SSM_GROUPS):
        part = yg[:, g * gw:(g + 1) * gw]
        ms = jnp.mean(part * part, axis=-1, keepdims=True)
        outs.append(part * lax.rsqrt(ms + LN_EPS))
    return jnp.concatenate(outs, axis=1) * normg


def _ssm_chunk_kernel(xs_ref, bm_ref, cm_ref, dt_ref, z_ref, aneg_ref, dskip_ref, normg_ref,
                      yg_ref, h_out_ref, h_ref, yT_ref, xe_ref):
    c = pl.program_id(1)
    l = xs_ref.shape[0]
    hd = SSM_HEAD_DIM

    @pl.when(c == 0)
    def _():
        h_ref[...] = jnp.zeros_like(h_ref)

    dot = lambda u, w: jnp.dot(u, w, preferred_element_type=jnp.float32)
    dt = dt_ref[...]
    a = dt * aneg_ref[...]
    r_i = lax.broadcasted_iota(jnp.int32, (l, l), 0)
    c_i = lax.broadcasted_iota(jnp.int32, (l, l), 1)
    tril = jnp.where(r_i >= c_i, 1.0, 0.0)
    hi = lax.Precision.HIGHEST
    acum = jnp.dot(tril, a, precision=hi, preferred_element_type=jnp.float32)
    acum_t = jnp.dot(a.T, tril.T, precision=hi, preferred_element_type=jnp.float32)
    dt_t = dt.T
    to_end_t = jnp.exp(acum_t[:, l - 1:l] - acum_t)
    from_start_t = jnp.exp(acum_t)
    chunk_decay = jnp.exp(acum[l - 1:l, :])
    upper = r_i <= c_i
    xs = xs_ref[...]
    for g in range(SSM_GROUPS):
        bm = bm_ref[:, g * SSM_STATE:(g + 1) * SSM_STATE]
        cm_t = cm_ref[:, g * SSM_STATE:(g + 1) * SSM_STATE].astype(jnp.float32).T.astype(MXU_DTYPE)
        cb_t = dot(bm, cm_t)
        h_in = h_ref[g * SSM_HPG:(g + 1) * SSM_HPG].reshape(SSM_HPG * hd, SSM_STATE)
        y_off = dot(h_in.astype(MXU_DTYPE), cm_t)
        for e in range(SSM_HPG):
            h = g * SSM_HPG + e
            if h % 2 == 0:
                xs_pair_t = xs[:, h * hd:(h + 2) * hd].T
            xdt_t = xs_pair_t[(h % 2) * hd:(h % 2 + 1) * hd] * dt_t[h:h + 1, :]
            seg = jnp.exp(jnp.where(upper, acum_t[h:h + 1, :] - acum[:, h:h + 1], -jnp.inf))
            y_diag = dot(xdt_t.astype(MXU_DTYPE), (cb_t * seg).astype(MXU_DTYPE))
            yT_ref[h * hd:(h + 1) * hd, :] = y_diag + y_off[e * hd:(e + 1) * hd] * from_start_t[h:h + 1, :]
            xe_ref[e * hd:(e + 1) * hd, :] = (xdt_t * to_end_t[h:h + 1, :]).astype(xe_ref.dtype)
        states = dot(xe_ref[...], bm)
        for e in range(SSM_HPG):
            h = g * SSM_HPG + e
            h_ref[h] = h_ref[h] * chunk_decay[:, h:h + 1] + states[e * hd:(e + 1) * hd]
    y = jnp.concatenate([yT_ref[i * l:(i + 1) * l, :].T for i in range(SSM_D_INNER // l)], axis=1)
    yg_ref[...] = _ssm_gate_norm(y, xs, z_ref[...], dskip_ref[...], normg_ref[...]).astype(yg_ref.dtype)

    @pl.when(c == pl.num_programs(1) - 1)
    def _():
        h_out_ref[0] = h_ref[...]


def _ssm_head_lanes(p):
    return jnp.pad(p.astype(jnp.float32), (0, SSM_DT_LANES - SSM_HEADS)).reshape(1, SSM_DT_LANES)


def _ssm_chunk_scan(b_, t_, xs, bm, cm, dt, z, a_log, d_skip, norm_g):
    l = SSM_CHUNK
    nc = t_ // l
    rows = lambda w: pl.BlockSpec((l, w), lambda b, c: (b * nc + c, 0))
    vec = lambda w: pl.BlockSpec((1, w), lambda b, c: (0, 0))
    aneg = _ssm_head_lanes(-jnp.exp(a_log.astype(jnp.float32)))
    dskip = jnp.repeat(d_skip, SSM_HEAD_DIM).reshape(1, SSM_D_INNER)
    yg, h_new = pl.pallas_call(
        _ssm_chunk_kernel,
        grid=(b_, nc),
        in_specs=[rows(SSM_D_INNER), rows(SSM_BC_DIM), rows(SSM_BC_DIM), rows(SSM_DT_LANES), rows(SSM_D_INNER),
                  vec(SSM_DT_LANES), vec(SSM_D_INNER), vec(SSM_D_INNER)],
        out_specs=[rows(SSM_D_INNER),
                   pl.BlockSpec((1, SSM_HEADS, SSM_HEAD_DIM, SSM_STATE), lambda b, c: (b, 0, 0, 0))],
        out_shape=[jax.ShapeDtypeStruct((b_ * t_, SSM_D_INNER), MXU_DTYPE),
                   jax.ShapeDtypeStruct((b_, SSM_HEADS, SSM_HEAD_DIM, SSM_STATE), jnp.float32)],
        scratch_shapes=[pltpu.VMEM((SSM_HEADS, SSM_HEAD_DIM, SSM_STATE), jnp.float32),
                        pltpu.VMEM((SSM_D_INNER, l), jnp.float32),
                        pltpu.VMEM((SSM_HPG * SSM_HEAD_DIM, l), MXU_DTYPE)],
        compiler_params=pltpu.CompilerParams(
            dimension_semantics=("parallel", "arbitrary"),
            vmem_limit_bytes=V7X_VMEM_LIMIT_BYTES),
        name="ssm_chunk_scan",
    )(xs, bm, cm, dt, z, aneg, dskip, norm_g.reshape(1, SSM_D_INNER))
    return yg, h_new


def _ssm_step_kernel(h0_ref, xs_ref, dt_ref, an_ref, bm_ref, cm_ref, y_ref, h_ref):
    h0 = h0_ref[0]
    dt = dt_ref[0]
    decay = jnp.exp(dt * an_ref[...])
    xdt = xs_ref[0] * dt
    bm = bm_ref[0].astype(jnp.float32)
    cm = cm_ref[0].astype(jnp.float32)
    h_ref[0] = h0 * decay + xdt * bm
    cb = jnp.sum(cm * bm, axis=-1, keepdims=True)
    y_ref[0] = cb * xdt + jnp.sum(cm * h0, axis=-1, keepdims=True) * decay


def _ssm_step(state, xs, bm, cm, dt, a_log):
    b_ = state.shape[0]
    per_head = lambda u: jnp.repeat(u.reshape(b_, SSM_GROUPS, 1, SSM_STATE), SSM_HPG, axis=1)
    xs4 = xs.reshape(b_, SSM_HEADS, SSM_HEAD_DIM, 1)
    dt4 = dt[:, :SSM_HEADS].reshape(b_, SSM_HEADS, 1, 1)
    an = (-jnp.exp(a_log.astype(jnp.float32))).reshape(SSM_HEADS, 1, 1)
    blk = lambda a: pl.BlockSpec((1,) + a.shape[1:], lambda b: (b, 0, 0, 0))
    args = [state.astype(jnp.float32), xs4, dt4, an, per_head(bm), per_head(cm)]
    y4, h_new = pl.pallas_call(
        _ssm_step_kernel,
        grid=(b_,),
        in_specs=[blk(args[0]), blk(xs4), blk(dt4), pl.BlockSpec(an.shape, lambda b: (0, 0, 0)),
                  blk(args[4]), blk(args[5])],
        out_specs=[blk(xs4), blk(args[0])],
        out_shape=[jax.ShapeDtypeStruct(xs4.shape, jnp.float32), jax.ShapeDtypeStruct(state.shape, jnp.float32)],
        compiler_params=pltpu.CompilerParams(
            dimension_semantics=("parallel",),
            vmem_limit_bytes=V7X_VMEM_LIMIT_BYTES),
        name="ssm_step",
    )(*args)
    return y4.reshape(b_, SSM_D_INNER), h_new


def _ssm_gate_norm_kernel(y_ref, xs_ref, z_ref, dskip_ref, normg_ref, o_ref):
    o_ref[...] = _ssm_gate_norm(y_ref[...], xs_ref[...], z_ref[...], dskip_ref[...], normg_ref[...]).astype(o_ref.dtype)


def _ssm_gate_norm_rows(y, xs, z, d_skip, norm_g):
    full = lambda a: pl.BlockSpec(a.shape, lambda i: (0,) * a.ndim)
    args = [y, xs, z, jnp.repeat(d_skip, SSM_HEAD_DIM).reshape(1, SSM_D_INNER), norm_g.reshape(1, SSM_D_INNER)]
    return pl.pallas_call(
        _ssm_gate_norm_kernel,
        grid=(1,),
        in_specs=[full(a) for a in args],
        out_specs=full(y),
        out_shape=jax.ShapeDtypeStruct(y.shape, MXU_DTYPE),
        name="ssm_gate_norm",
    )(*args)


def _mamba2_mixer_pallas(x3d, conv_state, ssm_state, w_in, conv_w, conv_b, dt_bias, a_log, d_skip, norm_g):
    b_, t_, _ = x3d.shape
    z, xs, bm, cm, dt, conv_new = _ssm_project(x3d, conv_state, w_in, conv_w, conv_b, dt_bias)
    if t_ == 1:
        y, h_new = _ssm_step(ssm_state, xs, bm, cm, dt, a_log)
        yg = _ssm_gate_norm_rows(y, xs, z, d_skip, norm_g)
    else:
        yg, h_new = _ssm_chunk_scan(b_, t_, xs, bm, cm, dt, z, a_log, d_skip, norm_g)
    return yg, conv_new, h_new.astype(ssm_state.dtype)


PAGES_PER_STEP = 8


def _sortable_key(score):
    bits = pltpu.bitcast(score, jnp.int32)
    return jnp.where(bits >= 0, bits, bits ^ jnp.int32(0x7FFFFFFF))


def _dsa_decode_kernel(pt_ref, iq_ref, iw_ref, q_ref, ikn_ref, kn_ref, vn_ref, *rest,
                       topk, col_bits, n_steps, pages):
    idx_refs, k_refs, v_refs = rest[:pages], rest[pages:2 * pages], rest[2 * pages:3 * pages]
    o_ref, key_ref, sel_ref, m_ref, l_ref, acc_ref = rest[3 * pages:]
    s = pl.program_id(1)
    nk = key_ref.shape[2]
    nt = (((1,), (1,)), ((), ()))
    iq = iq_ref[0]
    iw = iw_ref[0]
    lane = lax.broadcasted_iota(jnp.int32, (1, nk), 1)

    def row_dot(a, row):
        return jnp.sum(a.astype(jnp.float32) * row.astype(jnp.float32), axis=1, keepdims=True)

    def index_score(sc):
        return jnp.sum(iw * jnp.maximum(sc, 0.0), axis=0, keepdims=True)

    def fold(hit):
        out = hit[:, 0:128]
        for l in range(1, nk // 128):
            out = out + hit[:, l * 128:(l + 1) * 128]
        return out

    @pl.when(s < n_steps)
    def _score():
        ik_t = jnp.concatenate([r[0] for r in idx_refs], axis=1).astype(MXU_DTYPE)
        key_ref[s] = _sortable_key(index_score(jnp.dot(iq, ik_t, preferred_element_type=jnp.float32)))

    @pl.when(s == n_steps - 1)
    def _select():
        key_new = _sortable_key(index_score(row_dot(iq, ikn_ref[0])))

        def count(pred_past, pred_new):
            cnt = jnp.zeros((1, 128), jnp.float32)
            for st in range(n_steps):
                cnt = cnt + fold(jnp.where(pred_past(key_ref[st], lane + st * nk), 1.0, 0.0))
            return jnp.sum(cnt, axis=1, keepdims=True) + jnp.where(pred_new(key_new), 1.0, 0.0)

        def bit_body(i, thr):
            cand = thr ^ lax.shift_left(jnp.int32(1), jnp.int32(31) - i)
            total = count(lambda k, c: k >= cand, lambda k: k >= cand)
            return jnp.where(total >= float(topk), cand, thr)

        thr = lax.fori_loop(0, 32, bit_body, jnp.full((1, 1), INT32_MIN, jnp.int32))
        need = float(topk) - count(lambda k, c: k > thr, lambda k: k > thr)

        def col_body(i, last):
            cand = last | lax.shift_left(jnp.int32(1), jnp.int32(col_bits - 1) - i)
            ties = count(lambda k, c: (k == thr) & (c < cand),
                         lambda k: (k == thr) & (jnp.int32(n_steps * nk) < cand))
            return jnp.where(ties < need, cand, last)

        last_tie = lax.fori_loop(0, col_bits, col_body, jnp.zeros((1, 1), jnp.int32))
        sel_ref[0] = jnp.broadcast_to(thr, sel_ref.shape[1:])
        sel_ref[1] = jnp.broadcast_to(last_tie, sel_ref.shape[1:])
        sel_ref[2] = jnp.broadcast_to(key_new, sel_ref.shape[1:])

    gsz = ATT_HEADS // ATT_KV_HEADS
    q = q_ref[0]
    q_wide = jnp.concatenate([q] * ATT_KV_HEADS, axis=1)
    head_i = lax.broadcasted_iota(jnp.int32, q_wide.shape, 0)
    col_i = lax.broadcasted_iota(jnp.int32, q_wide.shape, 1)
    own_group = (col_i // ATT_HEAD_DIM) == (head_i // gsz)
    q_blk = jnp.where(own_group, q_wide, jnp.zeros_like(q_wide))

    def keep_mask(key, col):
        thr, last_tie = sel_ref[0, 0:1, 0:1], sel_ref[1, 0:1, 0:1]
        return (key > thr) | ((key == thr) & (col <= last_tie))

    def online_update(logits, weighted_values):
        m_old = m_ref[...]
        m_new = jnp.maximum(m_old, jnp.max(logits, axis=1, keepdims=True))
        p = jnp.exp(logits - m_new)
        alpha = jnp.exp(m_old - m_new)
        l_ref[...] = alpha * l_ref[...] + jnp.sum(p, axis=1, keepdims=True)
        acc_ref[...] = alpha * acc_ref[...] + weighted_values(p.astype(MXU_DTYPE))
        m_ref[...] = m_new

    @pl.when(s == n_steps)
    def _init():
        m_ref[...] = jnp.full(m_ref.shape, MASK_NEG, jnp.float32)
        l_ref[...] = jnp.zeros(l_ref.shape, jnp.float32)
        acc_ref[...] = jnp.zeros(acc_ref.shape, jnp.float32)

    @pl.when(s >= n_steps)
    def _attend():
        st = s - n_steps
        bias = jnp.where(keep_mask(key_ref[st], lane + st * nk), 0.0, MASK_NEG)
        k_t = jnp.concatenate([r[0] for r in k_refs], axis=1).astype(MXU_DTYPE)
        v_t = jnp.concatenate([r[0] for r in v_refs], axis=1).astype(MXU_DTYPE)
        online_update(jnp.dot(q_blk, k_t, preferred_element_type=jnp.float32) + bias,
                      lambda p: lax.dot_general(p, v_t, nt, preferred_element_type=jnp.float32))

    @pl.when(s == 2 * n_steps - 1)
    def _finish():
        keep_new = keep_mask(sel_ref[2, 0:1, 0:1], jnp.int32(n_steps * nk))
        logit = row_dot(q_blk, kn_ref[0])
        v_row = vn_ref[0].astype(jnp.float32)
        online_update(logit + jnp.where(keep_new, 0.0, MASK_NEG), lambda p: p.astype(jnp.float32) * v_row)
        out = jnp.where(own_group, acc_ref[...] / l_ref[...], 0.0)
        o = out[:, 0:ATT_HEAD_DIM]
        for g in range(1, ATT_KV_HEADS):
            o = o + out[:, g * ATT_HEAD_DIM:(g + 1) * ATT_HEAD_DIM]
        o_ref[0] = o.astype(o_ref.dtype)


def _dsa_decode(q, iq, iw, ik_new, k_new, v_new, cache_k, cache_v, cache_idx_k, page_table):
    b_, n_pages = page_table.shape
    n_pool, page = cache_k.shape[0], cache_k.shape[1]
    pages = PAGES_PER_STEP
    n_steps = n_pages // pages
    past = n_pages * page
    ck = jnp.transpose(cache_k, (0, 2, 3, 1)).reshape(n_pool, ATT_KV_DIM, page)
    cv = jnp.transpose(cache_v, (0, 2, 3, 1)).reshape(n_pool, ATT_KV_DIM, page)
    cik = jnp.swapaxes(cache_idx_k, 1, 2)
    per_seq = lambda a: pl.BlockSpec((1,) + a.shape[1:], lambda b, s, pt: (b,) + (0,) * (a.ndim - 1))

    def paged(width, j, attend_phase):
        def index(b, s, pt):
            grp = jnp.maximum(s - n_steps, 0) if attend_phase else jnp.minimum(s, n_steps - 1)
            return (pt[b, grp * pages + j], 0, 0)
        return pl.BlockSpec((1, width, page), index)

    small = [iq.reshape(b_, IDX_HEADS, IDX_DIM), iw[:, :IDX_HEADS].reshape(b_, IDX_HEADS, 1),
             q.reshape(b_, ATT_HEADS, ATT_HEAD_DIM), ik_new.astype(MXU_DTYPE).reshape(b_, 1, IDX_DIM),
             k_new.astype(MXU_DTYPE).reshape(b_, 1, ATT_KV_DIM), v_new.astype(MXU_DTYPE).reshape(b_, 1, ATT_KV_DIM)]
    grid_spec = pltpu.PrefetchScalarGridSpec(
        num_scalar_prefetch=1,
        grid=(b_, 2 * n_steps),
        in_specs=[per_seq(a) for a in small]
        + [paged(IDX_DIM, j, False) for j in range(pages)]
        + [paged(ATT_KV_DIM, j, True) for j in range(pages)]
        + [paged(ATT_KV_DIM, j, True) for j in range(pages)],
        out_specs=pl.BlockSpec((1, ATT_HEADS, ATT_HEAD_DIM), lambda b, s, pt: (b, 0, 0)),
        scratch_shapes=[pltpu.VMEM((n_steps, 1, pages * page), jnp.int32),
                        pltpu.VMEM((3, SUBLANES, 128), jnp.int32),
                        pltpu.VMEM((ATT_HEADS, 1), jnp.float32),
                        pltpu.VMEM((ATT_HEADS, 1), jnp.float32),
                        pltpu.VMEM((ATT_HEADS, ATT_KV_DIM), jnp.float32)])
    o = pl.pallas_call(
        functools.partial(_dsa_decode_kernel, topk=min(TOPK_MAX, (past + 1) // 4),
                          col_bits=max(1, past.bit_length()), n_steps=n_steps, pages=pages),
        grid_spec=grid_spec,
        out_shape=jax.ShapeDtypeStruct((b_, ATT_HEADS, ATT_HEAD_DIM), MXU_DTYPE),
        compiler_params=pltpu.CompilerParams(
            dimension_semantics=("parallel", "arbitrary"),
            vmem_limit_bytes=V7X_VMEM_LIMIT_BYTES),
        name="dsa_decode",
    )(page_table, *small, *([cik] * pages), *([ck] * pages), *([cv] * pages))
    return o.reshape(b_, ATT_Q_DIM)


def _dsa_sample_pallas(x3d, cache_k, cache_v, cache_idx_k, page_table, w_in, kn_g, kn_b):
    b_, t_, _ = x3d.shape
    past = page_table.shape[1] * cache_k.shape[1]
    pos = jnp.full((b_,), past, jnp.int32)
    q, iq, v, _, iw, kT, _, ikT, _ = _dsa_project(x3d.reshape(1, b_, D_MODEL), pos, w_in, kn_g, kn_b)
    k = _untranspose_groups(kT)[0]
    ik = _untranspose_groups(ikT)[0]
    o = _dsa_decode(q, iq, iw, ik, k, v, cache_k, cache_v, cache_idx_k, page_table)
    kv4 = lambda u: u.reshape(b_, t_, ATT_KV_HEADS, ATT_HEAD_DIM)
    return o, kv4(k), kv4(v), ik.reshape(b_, t_, IDX_DIM)


def _layer_norm(x, g, b):
    xf = x.astype(jnp.float32)
    mu = jnp.mean(xf, -1, keepdims=True)
    var = jnp.mean(jnp.square(xf - mu), -1, keepdims=True)
    return ((xf - mu) * lax.rsqrt(var + LN_EPS)).astype(x.dtype) * g + b


def _rope_partial(x, pos, rot_dim):
    half = rot_dim // 2
    inv = ROPE_THETA ** (-jnp.arange(half, dtype=jnp.float32) / half)
    ang = pos.astype(jnp.float32)[:, None] * inv[None, :]
    cos = jnp.cos(ang)[:, None, :]
    sin = jnp.sin(ang)[:, None, :]
    xf = x[..., :rot_dim].astype(jnp.float32)
    x1, x2 = xf[..., :half], xf[..., half:]
    rot = jnp.concatenate([x1 * cos - x2 * sin, x2 * cos + x1 * sin], axis=-1).astype(x.dtype)
    return jnp.concatenate([rot, x[..., rot_dim:]], axis=-1)


def _gather_rows(rows, idx):
    return jax.vmap(lambda r, i: r[i])(rows, idx)


def _gmlp_mixer(x, w_in, ln_g, ln_b, ws, bs):
    B_, T, _ = x.shape
    u, v = jnp.split(jax.nn.gelu(x @ w_in), 2, axis=-1)
    v = _layer_norm(v, ln_g, ln_b)
    l = min(T, CHUNK)
    c = T // l
    mask = jnp.tril(jnp.ones((l, l), dtype=bool))
    w = jnp.where(mask, ws[:, :l, :l], 0.0)
    vc = v.reshape(B_, c, l, GM_GROUPS, GM_GROUP_DIM)
    mixed = jnp.einsum('gts,bcsgd->bctgd', w, vc) + jnp.transpose(bs[:, :l])[:, :, None]
    return u * mixed.reshape(B_, T, GM_WIDTH), v


def _ssd_chunked(xs, dt, a, bm, cm, h0):
    B_, T = xs.shape[:2]
    l = min(T, SSM_CHUNK)
    c = T // l
    blk = lambda t: t.reshape((B_, c, l) + t.shape[2:])
    xdt = blk(xs.astype(jnp.float32) * dt[..., None])
    bc, cc, acum = blk(bm), blk(cm), jnp.cumsum(blk(a), axis=2)
    at = jnp.moveaxis(acum, 2, -1)
    causal = jnp.tril(jnp.ones((l, l), dtype=bool))
    seg = jnp.exp(jnp.where(causal, at[..., :, None] - at[..., None, :], -jnp.inf))
    cb = jnp.einsum('bctgn,bcsgn->bcgts', cc, bc)
    y_diag = jnp.einsum('bcgts,bcgets,bcsgep->bctgep', cb, seg, xdt)
    states = jnp.einsum('bclgn,bclge,bclgep->bcgepn', bc, jnp.exp(acum[:, :, -1:] - acum), xdt)
    chunk_decay = jnp.exp(acum[:, :, -1])

    def step(h, inp):
        dec, st = inp
        return h * dec[..., None, None] + st, h

    h_last, h_in = lax.scan(step, h0, (jnp.moveaxis(chunk_decay, 1, 0), jnp.moveaxis(states, 1, 0)))
    y_off = jnp.einsum('bctgn,bcgepn,bctge->bctgep', cc, jnp.moveaxis(h_in, 0, 1), jnp.exp(acum))
    return (y_diag + y_off).reshape(B_, T, SSM_GROUPS, SSM_HPG, SSM_HEAD_DIM), h_last


def _mamba2_mixer(x, conv_state, ssm_state, w_in, conv_w, conv_b, dt_bias, a_log, d_skip, norm_g):
    B_, T, _ = x.shape
    z, xbc, dt = jnp.split(x @ w_in, [SSM_D_INNER, SSM_D_INNER + SSM_CONV_DIM], axis=-1)
    xbc_ext = jnp.concatenate([conv_state, xbc], axis=1)
    conv = conv_b
    for j in range(SSM_CONV):
        conv = conv + xbc_ext[:, j:j + T] * conv_w[j]
    xbc = jax.nn.silu(conv)
    xs, bm, cm = jnp.split(xbc, [SSM_D_INNER, SSM_D_INNER + SSM_GROUPS * SSM_STATE], axis=-1)
    xs = xs.reshape(B_, T, SSM_GROUPS, SSM_HPG, SSM_HEAD_DIM)
    bm = bm.reshape(B_, T, SSM_GROUPS, SSM_STATE)
    cm = cm.reshape(B_, T, SSM_GROUPS, SSM_STATE)
    dt = jax.nn.softplus((dt + dt_bias).astype(jnp.float32)).reshape(B_, T, SSM_GROUPS, SSM_HPG)
    a_neg = -jnp.exp(a_log.astype(jnp.float32)).reshape(SSM_GROUPS, SSM_HPG)
    h0 = ssm_state.astype(jnp.float32).reshape(B_, SSM_GROUPS, SSM_HPG, SSM_HEAD_DIM, SSM_STATE)
    y, h_last = _ssd_chunked(xs, dt, dt * a_neg, bm, cm, h0)
    y = y.astype(x.dtype) + xs * d_skip.reshape(SSM_GROUPS, SSM_HPG, 1)
    yg = (y.reshape(B_, T, SSM_D_INNER) * jax.nn.silu(z)).reshape(B_, T, SSM_GROUPS, -1).astype(jnp.float32)
    yg = (yg * lax.rsqrt(jnp.mean(jnp.square(yg), -1, keepdims=True) + LN_EPS)).astype(x.dtype)
    yg = yg.reshape(B_, T, SSM_D_INNER) * norm_g
    new_ssm = h_last.reshape(B_, SSM_HEADS, SSM_HEAD_DIM, SSM_STATE).astype(ssm_state.dtype)
    return yg, xbc_ext[:, T:], new_ssm


def _dsa_project_jax(x, pos, w_in, kn_g, kn_b):
    B_, T, _ = x.shape
    q, k, v, iq, ik, iw = jnp.split(x @ w_in, list(ATT_IN_SPLITS), axis=-1)
    q = _rope_partial(q.reshape(B_, T, ATT_HEADS, ATT_HEAD_DIM), pos, ROPE_DIM)
    k = _rope_partial(k.reshape(B_, T, ATT_KV_HEADS, ATT_HEAD_DIM), pos, ROPE_DIM)
    v = v.reshape(B_, T, ATT_KV_HEADS, ATT_HEAD_DIM)
    iq = _rope_partial(iq.reshape(B_, T, IDX_HEADS, IDX_DIM), pos, IDX_ROPE_DIM)
    ik = _rope_partial(_layer_norm(ik, kn_g, kn_b)[:, :, None, :], pos, IDX_ROPE_DIM)[:, :, 0, :]
    iw = iw * (IDX_HEADS ** -0.5 * IDX_DIM ** -0.5)
    return q, k, v, iq, ik, iw


def _dsa_select(iq, iw, ik, qpos, topk):
    s = jnp.einsum('bqhd,bsd->bqhs', iq, ik)
    score = jnp.einsum('bqh,bqhs->bqs', iw, jax.nn.relu(s)).astype(jnp.float32)
    adm = jnp.arange(ik.shape[1])[None, :] <= qpos[:, None]
    score = jnp.where(adm[None], score, -jnp.inf)
    _, idx = lax.top_k(score, topk)
    return idx, idx <= qpos[None, :, None]


def _sparse_attend(q, k_sel, v_sel, valid):
    B_, Q = q.shape[:2]
    qg = q.reshape(B_, Q, ATT_KV_HEADS, ATT_HEADS // ATT_KV_HEADS, ATT_HEAD_DIM)
    s = jnp.einsum('bqhgd,bqkhd->bqhgk', qg, k_sel).astype(jnp.float32) * (ATT_HEAD_DIM ** -0.5)
    s = jnp.where(valid[:, :, None, None, :], s, -jnp.inf)
    p = jax.nn.softmax(s, axis=-1).astype(v_sel.dtype)
    o = jnp.einsum('bqhgk,bqkhd->bqhgd', p, v_sel)
    return o.reshape(B_, Q, ATT_Q_DIM)


def _dsa_prompt(x, w_in, kn_g, kn_b):
    B_, T, _ = x.shape
    q, k, v, iq, ik, iw = _dsa_project_jax(x, jnp.arange(T), w_in, kn_g, kn_b)
    topk = min(TOPK_MAX, T // 4)

    def block(bi):
        t0 = bi * Q_BLOCK
        sl = lambda t: lax.dynamic_slice_in_dim(t, t0, Q_BLOCK, axis=1)
        qpos = t0 + jnp.arange(Q_BLOCK)
        idx, valid = _dsa_select(sl(iq), sl(iw), ik, qpos, topk)
        return _sparse_attend(sl(q), _gather_rows(k, idx), _gather_rows(v, idx), valid)

    o = lax.map(block, jnp.arange(T // Q_BLOCK))
    o = jnp.moveaxis(o, 0, 1).reshape(B_, T, ATT_Q_DIM)
    return o, k, v, ik


def _dsa_sample(x, cache_k, cache_v, cache_idx_k, page_table, w_in, kn_g, kn_b):
    B_, T, _ = x.shape
    page = cache_k.shape[1]
    past = page_table.shape[1] * page
    pos = past + jnp.arange(T)
    q, k, v, iq, ik, iw = _dsa_project_jax(x, pos, w_in, kn_g, kn_b)
    ik_all = jnp.concatenate([cache_idx_k[page_table].reshape(B_, past, IDX_DIM), ik], axis=1)
    idx, valid = _dsa_select(iq, iw, ik_all, pos, min(TOPK_MAX, (past + T) // 4))
    past_idx = jnp.minimum(idx, past - 1)
    phys = jnp.take_along_axis(page_table, (past_idx // page).reshape(B_, -1), axis=1).reshape(idx.shape)
    off = past_idx % page
    new_idx = jnp.clip(idx - past, 0, T - 1)
    is_new = (idx >= past)[..., None, None]
    k_sel = jnp.where(is_new, _gather_rows(k, new_idx), cache_k[phys, off])
    v_sel = jnp.where(is_new, _gather_rows(v, new_idx), cache_v[phys, off])
    o = _sparse_attend(q, k_sel, v_sel, valid)
    return o, k, v, ik


def _rwkv7_mixer(x, shift, wkv, mu, w_r, w_k, w_v, w0, w1, w2, a0, a1, a2, g1, g2,
                 k_k, k_a, r_k, gn_g, gn_b):
    B_, T, _ = x.shape
    x_prev = jnp.concatenate([shift[:, None, :], x[:, :-1]], axis=1)
    xm = x[None] + (x_prev - x)[None] * mu[:, None, None, :]
    xr, xw, xk, xv, xa, xg = xm
    r = xr @ w_r
    w_log = -jax.nn.softplus(-(w0 + jnp.tanh(xw @ w1) @ w2)) - 0.5
    k = xk @ w_k
    v = xv @ w_v
    a = jax.nn.sigmoid(a0 + (xa @ a1) @ a2)
    g = jax.nn.sigmoid(xg @ g1) @ g2
    heads = lambda t: t.reshape(B_, T, RW_HEADS, RW_HEAD)
    kk = heads(k * k_k).astype(jnp.float32)
    kk = kk / jnp.maximum(jnp.sqrt(jnp.sum(kk * kk, -1, keepdims=True)), 1e-12)
    k = k * (1.0 + (a - 1.0) * k_a)
    decay = jnp.exp(-jnp.exp(w_log.astype(jnp.float32)))
    r, k, v, a, decay = heads(r), heads(k), heads(v), heads(a), heads(decay)
    seq = tuple(jnp.moveaxis(t.astype(jnp.float32), 1, 0) for t in (r, decay, k, v, kk, kk * a))

    def step(s, inp):
        r_t, d_t, k_t, v_t, kk_t, b_t = inp
        sa = jnp.einsum('bhij,bhj->bhi', s, kk_t)
        s = s * d_t[:, :, None, :] - sa[..., None] * b_t[:, :, None, :] + v_t[..., None] * k_t[:, :, None, :]
        return s, jnp.einsum('bhij,bhj->bhi', s, r_t)

    s_last, y = lax.scan(step, wkv.astype(jnp.float32), seq)
    y = jnp.moveaxis(y, 0, 1)
    mu_y = jnp.mean(y, -1, keepdims=True)
    var_y = jnp.mean(jnp.square(y - mu_y), -1, keepdims=True)
    yn = ((y - mu_y) * lax.rsqrt(var_y + RW_GN_EPS)).reshape(B_, T, D_MODEL).astype(x.dtype) * gn_g + gn_b
    bonus = (jnp.sum(r * k * r_k, -1, keepdims=True) * v).reshape(B_, T, D_MODEL)
    return (yn + bonus) * g, x[:, -1], s_last.astype(wkv.dtype)


def kernel(x_prompt, x_sample, state_ssm_conv, state_ssm, cache_k, cache_v, cache_idx_k, state_rwkv_shift, state_rwkv_wkv, page_table, p_prompt, p_sample, ln_g, ln_b, ffn_w_up, ffn_w_down, ple_w_p, ple_w_g, ple_b_g, gm_w_in, gm_ln_g, gm_ln_b, gm_ws, gm_bs, gm_w_out, ssm_w_in, ssm_conv_w, ssm_conv_b, ssm_dt_bias, ssm_a_log, ssm_d, ssm_norm_g, ssm_w_out, att_w_in, att_kn_g, att_kn_b, att_w_out, rw_mu, rw_w_r, rw_w_k, rw_w_v, rw_w_o, rw_w0, rw_w1, rw_w2, rw_a0, rw_a1, rw_a2, rw_g1, rw_g2, rw_k_k, rw_k_a, rw_r_k, rw_gn_g, rw_gn_b):
    bp, tp, _ = x_prompt.shape
    bs_, ts, _ = x_sample.shape
    bf = lambda w: w.astype(jnp.bfloat16)
    w_up_bf, w_down_bf = bf(ffn_w_up), bf(ffn_w_down)
    ple_wp_bf, ple_wg_bf = bf(ple_w_p), bf(ple_w_g)
    pp3 = p_prompt.reshape(DEPTH, bp * tp, PLE_DIM)
    ps3 = p_sample.reshape(DEPTH, bs_ * ts, PLE_DIM)

    yp = x_prompt.reshape(bp * tp, D_MODEL)
    ys = x_sample.reshape(bs_ * ts, D_MODEL)
    r3p = lambda t: t.reshape(bp, tp, -1)
    r3s = lambda t: t.reshape(bs_, ts, -1)
    f2 = lambda t: t.reshape(-1, t.shape[-1])

    for i in range(DEPTH):
        yp = _ffn_sub(yp, w_up_bf, w_down_bf, i, 0, ln_g[i, 0], ln_b[i, 0])
        ys = _ffn_sub(ys, w_up_bf, w_down_bf, i, 0, ln_g[i, 0], ln_b[i, 0])
        m = i % N_MIXERS
        if m == 0:
            gm_args = (gm_w_in, gm_ln_g, gm_ln_b, gm_ws, gm_bs, gm_w_out, ln_g[i, 1], ln_b[i, 1])
            yp, = _gmlp_block(yp, tp, *gm_args, False)
            ys, gm_v_s = _gmlp_block(ys, ts, *gm_args, True)
            gm_v_s = r3s(gm_v_s)
        elif m == 1:
            ssm_args = (ssm_w_in, ssm_conv_w, ssm_conv_b, ssm_dt_bias, ssm_a_log, ssm_d, ssm_norm_g)
            hp, conv_p, ssm_p = _mamba2_mixer_pallas(
                r3p(yp), jnp.zeros((bp, SSM_CONV - 1, SSM_CONV_DIM), yp.dtype),
                jnp.zeros((bp, SSM_HEADS, SSM_HEAD_DIM, SSM_STATE), yp.dtype), *ssm_args)
            hs, conv_s, ssm_s = _mamba2_mixer_pallas(r3s(ys), state_ssm_conv, state_ssm, *ssm_args)
            w_out = bf(ssm_w_out)
        elif m == 2:
            hp, k_p, v_p, ik_p = _dsa_prompt_pallas(r3p(yp), att_w_in, att_kn_g, att_kn_b)
            hs, k_s, v_s, ik_s = _dsa_sample_pallas(r3s(ys), cache_k, cache_v, cache_idx_k, page_table,
                                                    att_w_in, att_kn_g, att_kn_b)
            w_out = bf(att_w_out)
        else:
            rw_args = (rw_mu, rw_w_r, rw_w_k, rw_w_v, rw_w0, rw_w1, rw_w2, rw_a0, rw_a1, rw_a2,
                       rw_g1, rw_g2, rw_k_k, rw_k_a, rw_r_k, rw_gn_g, rw_gn_b)
            hp, gate_p, sh_p, wkv_p = _rwkv7_mixer_pallas(
                r3p(yp), jnp.zeros((bp, D_MODEL), yp.dtype),
                jnp.zeros((bp, RW_HEADS, RW_HEAD, RW_HEAD), yp.dtype), *rw_args)
            hs, gate_s, sh_s, wkv_s = _rwkv7_mixer_pallas(r3s(ys), state_rwkv_shift, state_rwkv_wkv, *rw_args)
            w_out = bf(rw_w_o)
        if m == 3:
            yp = _proj_gate_post_norm(yp, hp, gate_p, w_out, ln_g[i, 1], ln_b[i, 1])
            ys = _proj_gate_post_norm(ys, hs, gate_s, w_out, ln_g[i, 1], ln_b[i, 1])
        elif m != 0:
            yp = _proj_post_norm(yp, f2(hp), w_out, ln_g[i, 1], ln_b[i, 1])
            ys = _proj_post_norm(ys, f2(hs), w_out, ln_g[i, 1], ln_b[i, 1])
        yp = _ffn_sub(yp, w_up_bf, w_down_bf, i, 1, ln_g[i, 2], ln_b[i, 2])
        ys = _ffn_sub(ys, w_up_bf, w_down_bf, i, 1, ln_g[i, 2], ln_b[i, 2])
        yp = _ple_add(yp, pp3, ple_wp_bf, ple_wg_bf, ple_b_g[i], i)
        ys = _ple_add(ys, ps3, ple_wp_bf, ple_wg_bf, ple_b_g[i], i)

    return (r3p(yp), r3s(ys), gm_v_s, conv_p, ssm_p, conv_s, ssm_s, k_p, v_p, ik_p, k_s, v_s, ik_s,
            sh_p, wkv_p, sh_s, wkv_s)
```

```python
import functools
import math

import jax
import jax.numpy as jnp
from jax import lax
from jax.experimental import pallas as pl
from jax.experimental.pallas import tpu as pltpu

D_MODEL = 1024
DEPTH = 4
N_MIXERS = 4
PLE_DIM = 256
D_FF = 2816
ALPHA = (2 * DEPTH) ** 0.25
LN_EPS = 1e-5

CHUNK = 128
GM_WIDTH = 2 * D_MODEL
GM_GROUPS = 8
GM_GROUP_DIM = GM_WIDTH // GM_GROUPS

SSM_D_INNER = 2 * D_MODEL
SSM_HEAD_DIM = 64
SSM_HEADS = SSM_D_INNER // SSM_HEAD_DIM
SSM_GROUPS = 4
SSM_HPG = SSM_HEADS // SSM_GROUPS
SSM_STATE = 128
SSM_CONV = 4
SSM_CONV_DIM = SSM_D_INNER + 2 * SSM_GROUPS * SSM_STATE
SSM_CHUNK = 128

ATT_HEADS = 16
ATT_KV_HEADS = 4
ATT_HEAD_DIM = D_MODEL // ATT_HEADS
ROPE_DIM = ATT_HEAD_DIM // 4
ROPE_THETA = 500000.0
IDX_HEADS = 8
IDX_DIM = 64
IDX_ROPE_DIM = IDX_DIM // 4
TOPK_MAX = 256
Q_BLOCK = 128
ATT_Q_DIM = ATT_HEADS * ATT_HEAD_DIM
ATT_KV_DIM = ATT_KV_HEADS * ATT_HEAD_DIM
ATT_IN_SPLITS = (ATT_Q_DIM, ATT_Q_DIM + ATT_KV_DIM, ATT_Q_DIM + 2 * ATT_KV_DIM,
                 ATT_Q_DIM + 2 * ATT_KV_DIM + IDX_HEADS * IDX_DIM,
                 ATT_Q_DIM + 2 * ATT_KV_DIM + IDX_HEADS * IDX_DIM + IDX_DIM)

RW_HEAD = 64
RW_HEADS = D_MODEL // RW_HEAD
RW_GN_EPS = 64e-5

V7X_VMEM_LIMIT_BYTES = 52 * 1024 * 1024
FF_TILE = D_FF // 2
ROW_TILE = 512


def _row_tile(m):
    return ROW_TILE if m % ROW_TILE == 0 else m


def _ln_rows(y, g, b):
    mu = jnp.mean(y, axis=-1, keepdims=True)
    yc = y - mu
    var = jnp.mean(yc * yc, axis=-1, keepdims=True)
    return yc * lax.rsqrt(var + LN_EPS) * g + b


def _ffn_kernel(x_ref, wa_ref, wb_ref, wd_ref, g_ref, b_ref, o_ref, acc_ref):
    f = pl.program_id(1)

    @pl.when(f == 0)
    def _():
        acc_ref[...] = jnp.zeros_like(acc_ref)

    xb = x_ref[...].astype(jnp.bfloat16)
    a = jnp.dot(xb, wa_ref[...], preferred_element_type=jnp.float32)
    b = jnp.dot(xb, wb_ref[...], preferred_element_type=jnp.float32)
    h = (a * jax.nn.sigmoid(a) * b).astype(jnp.bfloat16)
    acc_ref[...] += jnp.dot(h, wd_ref[...], preferred_element_type=jnp.float32)

    @pl.when(f == pl.num_programs(1) - 1)
    def _():
        y = ALPHA * x_ref[...] + 0.5 * acc_ref[...]
        o_ref[...] = _ln_rows(y, g_ref[...], b_ref[...])


def _ffn_sub(x2d, w_up, w_down, layer, half, g, b):
    m = x2d.shape[0]
    tm = _row_tile(m)
    nf = D_FF // FF_TILE
    return pl.pallas_call(
        _ffn_kernel,
        grid=(m // tm, nf),
        in_specs=[
            pl.BlockSpec((tm, D_MODEL), lambda i, f: (i, 0)),
            pl.BlockSpec((None, None, D_MODEL, FF_TILE), lambda i, f: (layer, half, 0, f)),
            pl.BlockSpec((None, None, D_MODEL, FF_TILE), lambda i, f: (layer, half, 0, f + nf)),
            pl.BlockSpec((None, None, FF_TILE, D_MODEL), lambda i, f: (layer, half, f, 0)),
            pl.BlockSpec((1, D_MODEL), lambda i, f: (0, 0)),
            pl.BlockSpec((1, D_MODEL), lambda i, f: (0, 0)),
        ],
        out_specs=pl.BlockSpec((tm, D_MODEL), lambda i, f: (i, 0)),
        out_shape=jax.ShapeDtypeStruct((m, D_MODEL), jnp.float32),
        scratch_shapes=[pltpu.VMEM((tm, D_MODEL), jnp.float32)],
        compiler_params=pltpu.CompilerParams(
            dimension_semantics=("parallel", "arbitrary"),
            vmem_limit_bytes=V7X_VMEM_LIMIT_BYTES),
        name="ffn_sub",
    )(x2d, w_up, w_up, w_down, g.reshape(1, D_MODEL), b.reshape(1, D_MODEL))


def _ple_kernel(x_ref, p_ref, wp_ref, wg_ref, bg_ref, o_ref):
    x = x_ref[...]
    gate = jax.nn.sigmoid(
        jnp.dot(x.astype(jnp.bfloat16), wg_ref[...], preferred_element_type=jnp.float32) + bg_ref[...])
    emb = jnp.dot(p_ref[...].astype(jnp.bfloat16), wp_ref[...], preferred_element_type=jnp.float32)
    o_ref[...] = x + gate * emb


def _ple_add(x2d, p3d, w_p, w_g, b_g, layer):
    m = x2d.shape[0]
    tm = _row_tile(m)
    return pl.pallas_call(
        _ple_kernel,
        grid=(m // tm,),
        in_specs=[
            pl.BlockSpec((tm, D_MODEL), lambda i: (i, 0)),
            pl.BlockSpec((None, tm, PLE_DIM), lambda i: (layer, i, 0)),
            pl.BlockSpec((None, PLE_DIM, D_MODEL), lambda i: (layer, 0, 0)),
            pl.BlockSpec((None, D_MODEL, D_MODEL), lambda i: (layer, 0, 0)),
            pl.BlockSpec((1, D_MODEL), lambda i: (0, 0)),
        ],
        out_specs=pl.BlockSpec((tm, D_MODEL), lambda i: (i, 0)),
        out_shape=jax.ShapeDtypeStruct((m, D_MODEL), jnp.float32),
        compiler_params=pltpu.CompilerParams(
            dimension_semantics=("parallel",),
            vmem_limit_bytes=V7X_VMEM_LIMIT_BYTES),
        name="ple_add",
    )(x2d, p3d, w_p, w_g, b_g.reshape(1, D_MODEL))


def _proj_ln_kernel(x_ref, h_ref, w_ref, g_ref, b_ref, o_ref):
    y = ALPHA * x_ref[...] + jnp.dot(h_ref[...].astype(jnp.bfloat16), w_ref[...],
                                     preferred_element_type=jnp.float32)
    o_ref[...] = _ln_rows(y, g_ref[...], b_ref[...])


def _proj_post_norm(x2d, h2d, w_out, g, b):
    m = x2d.shape[0]
    k = h2d.shape[1]
    tm = _row_tile(m)
    return pl.pallas_call(
        _proj_ln_kernel,
        grid=(m // tm,),
        in_specs=[
            pl.BlockSpec((tm, D_MODEL), lambda i: (i, 0)),
            pl.BlockSpec((tm, k), lambda i: (i, 0)),
            pl.BlockSpec((k, D_MODEL), lambda i: (0, 0)),
            pl.BlockSpec((1, D_MODEL), lambda i: (0, 0)),
            pl.BlockSpec((1, D_MODEL), lambda i: (0, 0)),
        ],
        out_specs=pl.BlockSpec((tm, D_MODEL), lambda i: (i, 0)),
        out_shape=jax.ShapeDtypeStruct((m, D_MODEL), jnp.float32),
        compiler_params=pltpu.CompilerParams(
            dimension_semantics=("parallel",),
            vmem_limit_bytes=V7X_VMEM_LIMIT_BYTES),
        name="proj_post_norm",
    )(x2d, h2d, w_out, g.reshape(1, D_MODEL), b.reshape(1, D_MODEL))


MXU_DTYPE = jnp.bfloat16
KEY_GROUP = 512
INT32_MIN = -2 ** 31
MASK_NEG = -1e30


def _rope_lane_tables(pos, rot_dim, head_dim):
    half = rot_dim // 2
    inv = ROPE_THETA ** (-jnp.arange(half, dtype=jnp.float32) / half)
    ang = pos.astype(jnp.float32)[:, None] * inv[None, :]
    cos, sin = jnp.cos(ang), jnp.sin(ang)
    n = pos.shape[0]
    rest = head_dim - rot_dim
    c = jnp.concatenate([cos, cos, jnp.ones((n, rest), jnp.float32)], axis=1)
    s1 = jnp.concatenate([-sin, jnp.zeros((n, half + rest), jnp.float32)], axis=1)
    s2 = jnp.concatenate([jnp.zeros((n, half), jnp.float32), sin, jnp.zeros((n, rest), jnp.float32)], axis=1)
    reps = 128 // head_dim
    tile = lambda t: jnp.tile(t, (1, reps))
    return tile(c), tile(s1), tile(s2), cos.T, sin.T


def _rope_lanes(t, c, s1, s2, half):
    n = t.shape[1]
    reps = n // 128
    tl = lambda a: jnp.concatenate([a] * reps, axis=1)
    return t * tl(c) + pltpu.roll(t, n - half, 1) * tl(s1) + pltpu.roll(t, half, 1) * tl(s2)


def _rope_rows(t, cT, sT, head_dim, half):
    pieces = []
    for h in range(t.shape[0] // head_dim):
        x1 = t[h * head_dim:h * head_dim + half]
        x2 = t[h * head_dim + half:h * head_dim + 2 * half]
        pieces += [x1 * cT - x2 * sT, x2 * cT + x1 * sT, t[h * head_dim + 2 * half:(h + 1) * head_dim]]
    return jnp.concatenate(pieces, axis=0)


def _dsa_proj_kernel(x_ref, wq_ref, wiq_ref, wv_ref, wvx_ref, wiw_ref, wkT_ref, wikT_ref,
                     c_ref, s1_ref, s2_ref, cT_ref, sT_ref, kng_ref, knb_ref, one_ref,
                     q_ref, iq_ref, v_ref, vx_ref, iw_ref, kT_ref, kTb_ref, ikT_ref, ikTb_ref):
    xb = x_ref[...].astype(MXU_DTYPE)
    c, s1, s2 = c_ref[...], s1_ref[...], s2_ref[...]
    cT, sT = cT_ref[...], sT_ref[...]
    dot = lambda a, b: jnp.dot(a, b, preferred_element_type=jnp.float32)
    dot_t = lambda w, a: lax.dot_general(w, a, (((1,), (1,)), ((), ())), preferred_element_type=jnp.float32)

    q = _rope_lanes(dot(xb, wq_ref[...]), c, s1, s2, ROPE_DIM // 2)
    q_ref[...] = (q * (ATT_HEAD_DIM ** -0.5)).astype(q_ref.dtype)
    iq = _rope_lanes(dot(xb, wiq_ref[...]), c, s1, s2, IDX_ROPE_DIM // 2)
    iq_ref[...] = iq.astype(iq_ref.dtype)
    v_ref[...] = dot(xb, wv_ref[...])
    vx_ref[...] = (dot(xb, wvx_ref[...]) + one_ref[...]).astype(vx_ref.dtype)
    iw_ref[...] = dot(xb, wiw_ref[...]) * (IDX_HEADS ** -0.5 * IDX_DIM ** -0.5)

    kT = _rope_rows(dot_t(wkT_ref[...], xb), cT, sT, ATT_HEAD_DIM, ROPE_DIM // 2)
    kT_ref[0, 0] = kT
    kTb_ref[0, 0] = kT.astype(kTb_ref.dtype)
    ikT = dot_t(wikT_ref[...], xb)
    mu = jnp.mean(ikT, axis=0, keepdims=True)
    ikc = ikT - mu
    var = jnp.mean(ikc * ikc, axis=0, keepdims=True)
    ikT = ikc * lax.rsqrt(var + LN_EPS) * kng_ref[...] + knb_ref[...]
    ikT = _rope_rows(ikT, cT, sT, IDX_DIM, IDX_ROPE_DIM // 2)
    ikT_ref[0, 0] = ikT
    ikTb_ref[0, 0] = ikT.astype(ikTb_ref.dtype)


def _dsa_project(x3d, pos, w_in, kn_g, kn_b):
    b_, t_, _ = x3d.shape
    tk = KEY_GROUP if t_ % KEY_GROUP == 0 else t_
    ng = t_ // tk
    m = b_ * t_
    w_q, w_k, w_v, w_iq, w_ik, w_iw = jnp.split(w_in, list(ATT_IN_SPLITS), axis=1)
    cast = lambda w: w.astype(MXU_DTYPE)
    w_vx = jnp.pad(w_v.reshape(D_MODEL, ATT_KV_HEADS, ATT_HEAD_DIM),
                   ((0, 0), (0, 0), (0, 128 - ATT_HEAD_DIM))).reshape(D_MODEL, ATT_KV_HEADS * 128)
    one_col = jnp.tile((jnp.arange(128) == ATT_HEAD_DIM).astype(jnp.float32), ATT_KV_HEADS)[None, :]
    w_iw_pad = jnp.pad(w_iw, ((0, 0), (0, 128 - IDX_HEADS)))
    c, s1, s2, cT, sT = _rope_lane_tables(pos, ROPE_DIM, ATT_HEAD_DIM)
    full = lambda shape: pl.BlockSpec(shape, lambda b, i: (0,) * len(shape))
    rows = lambda n: pl.BlockSpec((tk, n), lambda b, i: (b * ng + i, 0))
    ptab = lambda n: pl.BlockSpec((tk, n), lambda b, i: (i, 0))
    grp = lambda n: pl.BlockSpec((1, 1, n, tk), lambda b, i: (b, i, 0, 0))
    sds = jax.ShapeDtypeStruct
    return pl.pallas_call(
        _dsa_proj_kernel,
        grid=(b_, ng),
        in_specs=[rows(D_MODEL), full((D_MODEL, ATT_Q_DIM)), full((D_MODEL, IDX_HEADS * IDX_DIM)),
                  full((D_MODEL, ATT_KV_DIM)), full((D_MODEL, ATT_KV_HEADS * 128)), full((D_MODEL, 128)),
                  full((ATT_KV_DIM, D_MODEL)), full((IDX_DIM, D_MODEL)),
                  ptab(128), ptab(128), ptab(128),
                  pl.BlockSpec((ROPE_DIM // 2, tk), lambda b, i: (0, i)),
                  pl.BlockSpec((ROPE_DIM // 2, tk), lambda b, i: (0, i)),
                  full((IDX_DIM, 1)), full((IDX_DIM, 1)), full((1, ATT_KV_HEADS * 128))],
        out_specs=[rows(ATT_Q_DIM), rows(IDX_HEADS * IDX_DIM), rows(ATT_KV_DIM), rows(ATT_KV_HEADS * 128),
                   rows(128), grp(ATT_KV_DIM), grp(ATT_KV_DIM), grp(IDX_DIM), grp(IDX_DIM)],
        out_shape=[sds((m, ATT_Q_DIM), MXU_DTYPE), sds((m, IDX_HEADS * IDX_DIM), MXU_DTYPE),
                   sds((m, ATT_KV_DIM), jnp.float32), sds((m, ATT_KV_HEADS * 128), MXU_DTYPE),
                   sds((m, 128), jnp.float32),
                   sds((b_, ng, ATT_KV_DIM, tk), jnp.float32), sds((b_, ng, ATT_KV_DIM, tk), MXU_DTYPE),
                   sds((b_, ng, IDX_DIM, tk), jnp.float32), sds((b_, ng, IDX_DIM, tk), MXU_DTYPE)],
        compiler_params=pltpu.CompilerParams(
            dimension_semantics=("parallel", "parallel"),
            vmem_limit_bytes=V7X_VMEM_LIMIT_BYTES),
        name="dsa_project",
    )(x3d.reshape(m, D_MODEL), cast(w_q), cast(w_iq), cast(w_v), cast(w_vx), cast(w_iw_pad),
      cast(w_k.T), cast(w_ik.T), c, s1, s2, cT, sT, kn_g.reshape(IDX_DIM, 1), kn_b.reshape(IDX_DIM, 1), one_col)


def _untranspose_groups(tg):
    b_, g_, r_, tk = tg.shape
    return jnp.transpose(tg, (0, 1, 3, 2)).reshape(b_, g_ * tk, r_)


def _dsa_attend_kernel(iq_ref, iw_ref, ikT_ref, q_ref, kT_ref, vx_ref, o_ref, key_ref, m_ref, acc_ref, *,
                       topk, col_bits):
    j = pl.program_id(1)
    tq = iq_ref.shape[0]
    tk = key_ref.shape[2]
    n_groups = (j * tq + tq + tk - 1) // tk
    row = j * tq + lax.broadcasted_iota(jnp.int32, (tq, tk), 0)
    col0 = lax.broadcasted_iota(jnp.int32, (tq, tk), 1)
    dot = lambda a, b: jnp.dot(a, b, preferred_element_type=jnp.float32)

    def score_body(g, carry):
        ikT = ikT_ref[0, g]
        sc = jnp.zeros((tq, tk), jnp.float32)
        for h in range(IDX_HEADS):
            s = dot(iq_ref[:, h * IDX_DIM:(h + 1) * IDX_DIM], ikT)
            sc = sc + iw_ref[:, h:h + 1] * jnp.maximum(s, 0.0)
        bits = pltpu.bitcast(sc, jnp.int32)
        key = jnp.where(bits >= 0, bits, bits ^ jnp.int32(0x7FFFFFFF))
        key_ref[g] = jnp.where(col0 + g * tk <= row, key, jnp.int32(INT32_MIN))
        return carry

    lax.fori_loop(0, n_groups, score_body, 0)

    def bit_body(i, thr):
        cand = thr ^ lax.shift_left(jnp.int32(1), jnp.int32(31) - i)

        def count_body(g, cnt):
            hit = jnp.where(key_ref[g] >= cand, 1.0, 0.0)
            for l in range(tk // 128):
                cnt = cnt + hit[:, l * 128:(l + 1) * 128]
            return cnt

        cnt = lax.fori_loop(0, n_groups, count_body, jnp.zeros((tq, 128), jnp.float32))
        total = jnp.sum(cnt, axis=1, keepdims=True)
        return jnp.where(total >= float(topk), cand, thr)

    thr = lax.fori_loop(0, 32, bit_body, jnp.full((tq, 1), INT32_MIN, jnp.int32))

    def lane_fold(hit, cnt):
        for l in range(tk // 128):
            cnt = cnt + hit[:, l * 128:(l + 1) * 128]
        return cnt

    def above_body(g, cnt):
        return lane_fold(jnp.where(key_ref[g] > thr, 1.0, 0.0), cnt)

    n_above = jnp.sum(lax.fori_loop(0, n_groups, above_body, jnp.zeros((tq, 128), jnp.float32)),
                      axis=1, keepdims=True)
    need = float(topk) - n_above

    def col_body(i, last):
        cand = last | lax.shift_left(jnp.int32(1), jnp.int32(col_bits - 1) - i)

        def tie_body(g, cnt):
            hit = jnp.where((key_ref[g] == thr) & (col0 + g * tk < cand), 1.0, 0.0)
            return lane_fold(hit, cnt)

        ties = jnp.sum(lax.fori_loop(0, n_groups, tie_body, jnp.zeros((tq, 128), jnp.float32)),
                       axis=1, keepdims=True)
        return jnp.where(ties < need, cand, last)

    last_tie = lax.fori_loop(0, col_bits, col_body, jnp.zeros((tq, 1), jnp.int32))

    m_ref[...] = jnp.full(m_ref.shape, MASK_NEG, jnp.float32)
    acc_ref[...] = jnp.zeros(acc_ref.shape, jnp.float32)
    gsz = ATT_HEADS // ATT_KV_HEADS

    def attend_body(g, carry):
        key = key_ref[g]
        col = col0 + g * tk
        keep = (key > thr) | ((key == thr) & (col <= last_tie))
        bias = jnp.where(keep & (col <= row), 0.0, MASK_NEG)
        start = pl.multiple_of(g * tk, tk)
        for h in range(ATT_HEADS):
            kv = h // gsz
            s = dot(q_ref[:, h * ATT_HEAD_DIM:(h + 1) * ATT_HEAD_DIM],
                    kT_ref[0, g, kv * ATT_HEAD_DIM:(kv + 1) * ATT_HEAD_DIM, :]) + bias
            m_old = m_ref[h]
            m_new = jnp.maximum(m_old, jnp.max(s, axis=1, keepdims=True))
            p = jnp.exp(s - m_new).astype(vx_ref.dtype)
            pv = dot(p, vx_ref[0, pl.ds(start, tk), kv * 128:(kv + 1) * 128])
            acc_ref[h] = jnp.exp(m_old - m_new) * acc_ref[h] + pv
            m_ref[h] = m_new
        return carry

    lax.fori_loop(0, n_groups, attend_body, 0)

    for h in range(ATT_HEADS):
        a = acc_ref[h]
        o_ref[:, h * ATT_HEAD_DIM:(h + 1) * ATT_HEAD_DIM] = (
            a[:, :ATT_HEAD_DIM] / a[:, ATT_HEAD_DIM:ATT_HEAD_DIM + 1]).astype(o_ref.dtype)


def _dsa_attend(b_, t_, q, iq, iw, ikTb, kTb, vx):
    ng, tk = kTb.shape[1], kTb.shape[3]
    tq = Q_BLOCK
    nq = t_ // tq
    rows = lambda n: pl.BlockSpec((tq, n), lambda b, j: (b * nq + j, 0))
    return pl.pallas_call(
        functools.partial(_dsa_attend_kernel, topk=min(TOPK_MAX, t_ // 4), col_bits=max(1, (t_ - 1).bit_length())),
        grid=(b_, nq),
        in_specs=[rows(IDX_HEADS * IDX_DIM), rows(128),
                  pl.BlockSpec((1, ng, IDX_DIM, tk), lambda b, j: (b, 0, 0, 0)),
                  rows(ATT_Q_DIM),
                  pl.BlockSpec((1, ng, ATT_KV_DIM, tk), lambda b, j: (b, 0, 0, 0)),
                  pl.BlockSpec((1, t_, ATT_KV_HEADS * 128), lambda b, j: (b, 0, 0))],
        out_specs=rows(ATT_Q_DIM),
        out_shape=jax.ShapeDtypeStruct((b_ * t_, ATT_Q_DIM), MXU_DTYPE),
        scratch_shapes=[pltpu.VMEM((ng, tq, tk), jnp.int32),
                        pltpu.VMEM((ATT_HEADS, tq, 1), jnp.float32),
                        pltpu.VMEM((ATT_HEADS, tq, 128), jnp.float32)],
        compiler_params=pltpu.CompilerParams(
            dimension_semantics=("parallel", "arbitrary"),
            vmem_limit_bytes=V7X_VMEM_LIMIT_BYTES),
        name="dsa_attend",
    )(iq, iw, ikTb, q, kTb, vx.reshape(b_, t_, ATT_KV_HEADS * 128))


def _dsa_proj_q_lanes_kernel(x_ref, wqT_ref, wiqT_ref, wiwT_ref, wk_ref, wv_ref, wvxT_ref, wik_ref,
                             c_ref, s1_ref, s2_ref, cT_ref, sT_ref, kng_ref, knb_ref, onerow_ref,
                             qT_ref, iqT_ref, iwT_ref, k_ref, khd_ref, v_ref, vxT_ref, ik_ref, ikb_ref):
    xb = x_ref[...].astype(MXU_DTYPE)
    tm = xb.shape[0]
    c, s1, s2 = c_ref[...], s1_ref[...], s2_ref[...]
    cT, sT = cT_ref[...], sT_ref[...]
    dot = lambda a, b: jnp.dot(a, b, preferred_element_type=jnp.float32)
    dot_t = lambda w, a: lax.dot_general(w, a, (((1,), (1,)), ((), ())), preferred_element_type=jnp.float32)

    qT = _rope_rows(dot_t(wqT_ref[...], xb), cT, sT, ATT_HEAD_DIM, ROPE_DIM // 2) * (ATT_HEAD_DIM ** -0.5)
    iqT = _rope_rows(dot_t(wiqT_ref[...], xb), cT, sT, IDX_DIM, IDX_ROPE_DIM // 2)
    iwT = dot_t(wiwT_ref[...], xb) * (IDX_HEADS ** -0.5 * IDX_DIM ** -0.5)
    for t in range(tm // Q_BLOCK):
        lanes = slice(t * Q_BLOCK, (t + 1) * Q_BLOCK)
        qT_ref[0, t] = qT[:, lanes].astype(qT_ref.dtype)
        iqT_ref[0, t] = iqT[:, lanes].astype(iqT_ref.dtype)
        iwT_ref[0, t] = iwT[:, lanes]

    k = _rope_lanes(dot(xb, wk_ref[...]), c, s1, s2, ROPE_DIM // 2)
    k_ref[...] = k
    for g in range(ATT_KV_HEADS):
        khd_ref[g] = k[:, g * ATT_HEAD_DIM:(g + 1) * ATT_HEAD_DIM].astype(khd_ref.dtype)
    v_ref[...] = dot(xb, wv_ref[...])
    vxT_ref[0, 0] = (dot_t(wvxT_ref[...], xb) + onerow_ref[...]).astype(vxT_ref.dtype)

    ik = dot(xb, wik_ref[...])
    real = lax.broadcasted_iota(jnp.int32, ik.shape, 1) < IDX_DIM
    mu = jnp.sum(ik, axis=-1, keepdims=True) * (1.0 / IDX_DIM)
    ikc = jnp.where(real, ik - mu, 0.0)
    var = jnp.sum(ikc * ikc, axis=-1, keepdims=True) * (1.0 / IDX_DIM)
    ikn = _rope_lanes(ikc * lax.rsqrt(var + LN_EPS) * kng_ref[...] + knb_ref[...], c, s1, s2, IDX_ROPE_DIM // 2)
    ik_ref[...] = ikn[:, :IDX_DIM]
    ikb_ref[...] = ikn[:, :IDX_DIM].astype(ikb_ref.dtype)


def _dsa_project_q_lanes(x3d, pos, w_in, kn_g, kn_b):
    b_, t_, _ = x3d.shape
    tk = KEY_GROUP
    ng = t_ // tk
    nq = tk // Q_BLOCK
    m = b_ * t_
    w_q, w_k, w_v, w_iq, w_ik, w_iw = jnp.split(w_in, list(ATT_IN_SPLITS), axis=1)
    cast = lambda w: w.astype(MXU_DTYPE)
    w_vxT = jnp.pad(w_v.T.reshape(ATT_KV_HEADS, ATT_HEAD_DIM, D_MODEL),
                    ((0, 0), (0, 128 - ATT_HEAD_DIM), (0, 0))).reshape(ATT_KV_HEADS * 128, D_MODEL)
    one_row = jnp.tile((jnp.arange(128) == ATT_HEAD_DIM).astype(jnp.float32), ATT_KV_HEADS)[:, None]
    pad_lanes = lambda a: jnp.pad(a, ((0, 0), (0, 128 - a.shape[1])))
    c, s1, s2, cT, sT = _rope_lane_tables(pos, ROPE_DIM, ATT_HEAD_DIM)
    full = lambda shape: pl.BlockSpec(shape, lambda b, i: (0,) * len(shape))
    rows = lambda n: pl.BlockSpec((tk, n), lambda b, i: (b * ng + i, 0))
    ptab = lambda n: pl.BlockSpec((tk, n), lambda b, i: (i, 0))
    qtile = lambda n: pl.BlockSpec((1, nq, n, Q_BLOCK), lambda b, i: (b, i, 0, 0))
    sds = jax.ShapeDtypeStruct
    return pl.pallas_call(
        _dsa_proj_q_lanes_kernel,
        grid=(b_, ng),
        in_specs=[rows(D_MODEL), full((ATT_Q_DIM, D_MODEL)), full((IDX_HEADS * IDX_DIM, D_MODEL)),
                  full((IDX_HEADS, D_MODEL)), full((D_MODEL, ATT_KV_DIM)), full((D_MODEL, ATT_KV_DIM)),
                  full((ATT_KV_HEADS * 128, D_MODEL)), full((D_MODEL, 128)),
                  ptab(128), ptab(128), ptab(128),
                  pl.BlockSpec((ROPE_DIM // 2, tk), lambda b, i: (0, i)),
                  pl.BlockSpec((ROPE_DIM // 2, tk), lambda b, i: (0, i)),
                  full((1, 128)), full((1, 128)), full((ATT_KV_HEADS * 128, 1))],
        out_specs=[qtile(ATT_Q_DIM), qtile(IDX_HEADS * IDX_DIM), qtile(IDX_HEADS),
                   rows(ATT_KV_DIM), pl.BlockSpec((ATT_KV_HEADS, tk, ATT_HEAD_DIM), lambda b, i: (0, b * ng + i, 0)),
                   rows(ATT_KV_DIM), pl.BlockSpec((1, 1, ATT_KV_HEADS * 128, tk), lambda b, i: (b, i, 0, 0)),
                   rows(IDX_DIM), rows(IDX_DIM)],
        out_shape=[sds((b_, t_ // Q_BLOCK, ATT_Q_DIM, Q_BLOCK), MXU_DTYPE),
                   sds((b_, t_ // Q_BLOCK, IDX_HEADS * IDX_DIM, Q_BLOCK), MXU_DTYPE),
                   sds((b_, t_ // Q_BLOCK, IDX_HEADS, Q_BLOCK), jnp.float32),
                   sds((m, ATT_KV_DIM), jnp.float32), sds((ATT_KV_HEADS, m, ATT_HEAD_DIM), MXU_DTYPE),
                   sds((m, ATT_KV_DIM), jnp.float32), sds((b_, ng, ATT_KV_HEADS * 128, tk), MXU_DTYPE),
                   sds((m, IDX_DIM), jnp.float32), sds((m, IDX_DIM), MXU_DTYPE)],
        compiler_params=pltpu.CompilerParams(
            dimension_semantics=("parallel", "parallel"),
            vmem_limit_bytes=V7X_VMEM_LIMIT_BYTES),
        name="dsa_project_q_lanes",
    )(x3d.reshape(m, D_MODEL), cast(w_q.T), cast(w_iq.T), cast(w_iw.T), cast(w_k), cast(w_v), cast(w_vxT),
      cast(pad_lanes(w_ik)), c, s1, s2, cT, sT, pad_lanes(kn_g.reshape(1, IDX_DIM)),
      pad_lanes(kn_b.reshape(1, IDX_DIM)), one_row)


def _tree_sum(parts):
    while len(parts) > 1:
        parts = [parts[i] + parts[i + 1] for i in range(0, len(parts) - 1, 2)] + (
            [parts[-1]] if len(parts) % 2 else [])
    return parts[0]


def _dsa_attend_q_lanes_kernel(iqT_ref, iwT_ref, ik_ref, qT_ref, k_ref, vxT_ref, o_ref,
                               key_ref, bias_ref, m_ref, acc_ref, *, topk, col_bits):
    j = pl.program_id(1)
    tk, tq = key_ref.shape[1], key_ref.shape[2]
    n_groups = (j * tq + tq + tk - 1) // tk
    qpos = j * tq + lax.broadcasted_iota(jnp.int32, (tk, tq), 1)
    kpos0 = lax.broadcasted_iota(jnp.int32, (tk, tq), 0)
    dot = lambda a, b: jnp.dot(a, b, preferred_element_type=jnp.float32)

    def score_body(g, carry):
        start = pl.multiple_of(g * tk, tk)
        w_iq = jnp.concatenate([iqT_ref[0, 0, h * IDX_DIM:(h + 1) * IDX_DIM, :] for h in range(IDX_HEADS)], axis=1)
        s_all = dot(ik_ref[0, pl.ds(start, tk), :], w_iq)
        sc = _tree_sum([iwT_ref[0, 0, h:h + 1, :] * jnp.maximum(s_all[:, h * tq:(h + 1) * tq], 0.0)
                        for h in range(IDX_HEADS)])
        key_ref[g] = jnp.where(kpos0 + g * tk <= qpos, _sortable_key(sc), jnp.int32(INT32_MIN))
        return carry

    lax.fori_loop(0, n_groups, score_body, 0)

    def count_keys(pred):
        def body(g, part):
            hit = jnp.where(pred(key_ref[g], kpos0 + g * tk), 1.0, 0.0)
            return part + _tree_sum([hit[r * SUBLANES:(r + 1) * SUBLANES] for r in range(tk // SUBLANES)])
        part = lax.fori_loop(0, n_groups, body, jnp.zeros((SUBLANES, tq), jnp.float32))
        return jnp.sum(part, axis=0, keepdims=True)

    def bit_body(i, thr):
        cand = thr ^ lax.shift_left(jnp.int32(1), jnp.int32(31) - i)
        return jnp.where(count_keys(lambda k, kp: k >= cand) >= float(topk), cand, thr)

    thr = lax.fori_loop(0, 32, bit_body, jnp.full((1, tq), INT32_MIN, jnp.int32))

    need = float(topk) - count_keys(lambda k, kp: k > thr)

    def pos_body(i, last):
        cand = last | lax.shift_left(jnp.int32(1), jnp.int32(col_bits - 1) - i)
        return jnp.where(count_keys(lambda k, kp: (k == thr) & (kp < cand)) < need, cand, last)

    n_tied = count_keys(lambda k, kp: k == thr)
    excess = jnp.max(jnp.where(n_tied > need, 1.0, 0.0), axis=1, keepdims=True)
    last_tie = lax.cond(excess[0, 0] > 0.0,
                        lambda: lax.fori_loop(0, col_bits, pos_body, jnp.zeros((1, tq), jnp.int32)),
                        lambda: jnp.full((1, tq), 2 ** col_bits - 1, jnp.int32))

    m_ref[...] = jnp.full(m_ref.shape, MASK_NEG, jnp.float32)
    acc_ref[...] = jnp.zeros(acc_ref.shape, jnp.float32)
    gsz = ATT_HEADS // ATT_KV_HEADS

    def attend_body(g, carry):
        start = pl.multiple_of(g * tk, tk)
        key = key_ref[g]
        kpos = kpos0 + g * tk
        keep = (key > thr) | ((key == thr) & (kpos <= last_tie))
        bias_ref[...] = jnp.where(keep & (kpos <= qpos), 0.0, MASK_NEG)
        logits = []
        for kv in range(ATT_KV_HEADS):
            w_q = jnp.concatenate([qT_ref[0, 0, (kv * gsz + i) * ATT_HEAD_DIM:(kv * gsz + i + 1) * ATT_HEAD_DIM, :]
                                   for i in range(gsz)], axis=1)
            logits.append(dot(k_ref[kv, pl.ds(start, tk), :], w_q))
        for kv in range(ATT_KV_HEADS):
            s = logits[kv] + jnp.concatenate([bias_ref[...]] * gsz, axis=1)
            m_old = m_ref[kv]
            m_new = jnp.maximum(m_old, jnp.max(s, axis=0, keepdims=True))
            p = jnp.exp(s - m_new).astype(vxT_ref.dtype)
            pv = dot(vxT_ref[0, g, kv * 128:(kv + 1) * 128, :], p)
            acc_ref[kv] = jnp.exp(m_old - m_new) * acc_ref[kv] + pv
            m_ref[kv] = m_new
        return carry

    lax.fori_loop(0, n_groups, attend_body, 0)

    for h in range(ATT_HEADS):
        a = acc_ref[h // gsz, :, (h % gsz) * tq:(h % gsz + 1) * tq]
        o = (a / a[ATT_HEAD_DIM:ATT_HEAD_DIM + 1, :]).T
        o_ref[:, h * ATT_HEAD_DIM:(h + 1) * ATT_HEAD_DIM] = o[:, :ATT_HEAD_DIM].astype(o_ref.dtype)


def _dsa_attend_q_lanes(b_, t_, qT, iqT, iwT, ikb, khd, vxT):
    ng, tk = vxT.shape[1], vxT.shape[3]
    tq = Q_BLOCK
    nq = t_ // tq
    qtile = lambda n: pl.BlockSpec((1, 1, n, tq), lambda b, j: (b, j, 0, 0))
    return pl.pallas_call(
        functools.partial(_dsa_attend_q_lanes_kernel, topk=min(TOPK_MAX, t_ // 4),
                          col_bits=max(1, (t_ - 1).bit_length())),
        grid=(b_, nq),
        in_specs=[qtile(IDX_HEADS * IDX_DIM), qtile(IDX_HEADS),
                  pl.BlockSpec((1, t_, IDX_DIM), lambda b, j: (b, 0, 0)),
                  qtile(ATT_Q_DIM),
                  pl.BlockSpec((ATT_KV_HEADS, t_, ATT_HEAD_DIM), lambda b, j: (0, b, 0)),
                  pl.BlockSpec((1, ng, ATT_KV_HEADS * 128, tk), lambda b, j: (b, 0, 0, 0))],
        out_specs=pl.BlockSpec((tq, ATT_Q_DIM), lambda b, j: (b * nq + j, 0)),
        out_shape=jax.ShapeDtypeStruct((b_ * t_, ATT_Q_DIM), MXU_DTYPE),
        scratch_shapes=[pltpu.VMEM((ng, tk, tq), jnp.int32),
                        pltpu.VMEM((tk, tq), jnp.float32),
                        pltpu.VMEM((ATT_KV_HEADS, 1, tq * (ATT_HEADS // ATT_KV_HEADS)), jnp.float32),
                        pltpu.VMEM((ATT_KV_HEADS, 128, tq * (ATT_HEADS // ATT_KV_HEADS)), jnp.float32)],
        compiler_params=pltpu.CompilerParams(
            dimension_semantics=("parallel", "arbitrary"),
            vmem_limit_bytes=V7X_VMEM_LIMIT_BYTES),
        name="dsa_attend_q_lanes",
    )(iqT, iwT, ikb.reshape(b_, t_, IDX_DIM), qT, khd, vxT)


def _dsa_prompt_pallas(x3d, w_in, kn_g, kn_b):
    b_, t_, _ = x3d.shape
    qT, iqT, iwT, k, khd, v, vxT, ik, ikb = _dsa_project_q_lanes(x3d, jnp.arange(t_), w_in, kn_g, kn_b)
    o = _dsa_attend_q_lanes(b_, t_, qT, iqT, iwT, ikb, khd, vxT)
    kv4 = lambda u: u.reshape(b_, t_, ATT_KV_HEADS, ATT_HEAD_DIM)
    return o, kv4(k), kv4(v), ik.reshape(b_, t_, IDX_DIM)


RW_ROW_TILE = 256


RW_PAIRS = RW_HEADS // 2
RW_PAIR_LANES = 2 * RW_HEAD


def _rwkv_project_rows(x, xp, mu_ref, wr_ref, wk_ref, wv_ref, w1_ref, w2_ref, a1_ref, a2_ref,
                       g1_ref, g2_ref, w0_ref, a0_ref):
    dx = xp - x
    mix = lambda c: (x + dx * mu_ref[c:c + 1, :]).astype(MXU_DTYPE)
    dot = lambda a, b: jnp.dot(a.astype(MXU_DTYPE), b, preferred_element_type=jnp.float32)
    r = dot(mix(0), wr_ref[...])
    lora_w = dot(jnp.tanh(dot(mix(1), w1_ref[...])), w2_ref[...])
    w_log = -jax.nn.softplus(-(w0_ref[...] + lora_w)) - 0.5
    d = jnp.exp(-jnp.exp(w_log))
    k = dot(mix(2), wk_ref[...])
    v = dot(mix(3), wv_ref[...])
    a = jax.nn.sigmoid(a0_ref[...] + dot(dot(mix(4), a1_ref[...]), a2_ref[...]))
    g = dot(jax.nn.sigmoid(dot(mix(5), g1_ref[...])), g2_ref[...])
    return r, d, k, v, a, g


def _rwkv_proj_step_kernel(x_ref, xp_ref, *refs):
    vals = _rwkv_project_rows(x_ref[...], xp_ref[...], *refs[:12])
    for ref, val in zip(refs[12:], vals):
        ref[...] = val


def _rwkv_proj_seq_kernel(x_ref, halo_ref, shift_ref, *refs):
    i = pl.program_id(1)
    x = x_ref[...]
    prev = jnp.where(i == 0, shift_ref[0], halo_ref[...])[SUBLANES - 1:SUBLANES, :]
    first = lax.broadcasted_iota(jnp.int32, (x.shape[0], 1), 0) == 0
    xp = jnp.where(first, prev, pltpu.roll(x, 1, 0))
    vals = _rwkv_project_rows(x, xp, *refs[:12])
    for ref, val in zip(refs[12:17], vals[:5]):
        for hp in range(RW_PAIRS):
            ref[:, hp, :] = val[:, hp * RW_PAIR_LANES:(hp + 1) * RW_PAIR_LANES]
    refs[17][...] = vals[5]


def _rwkv_consts(mu, w_r, w_k, w_v, w0, w1, w2, a0, a1, a2, g1, g2):
    cast = lambda w: w.astype(MXU_DTYPE)
    return [mu, cast(w_r), cast(w_k), cast(w_v), cast(w1), cast(w2), cast(a1), cast(a2), cast(g1), cast(g2),
            w0.reshape(1, D_MODEL), a0.reshape(1, D_MODEL)]


def _rwkv_project_step(x2d, xprev2d, *params):
    m = x2d.shape[0]
    consts = _rwkv_consts(*params)
    full = lambda a: pl.BlockSpec(a.shape, lambda i: (0,) * a.ndim)
    rows = pl.BlockSpec((m, D_MODEL), lambda i: (0, 0))
    return pl.pallas_call(
        _rwkv_proj_step_kernel,
        grid=(1,),
        in_specs=[rows, rows] + [full(a) for a in consts],
        out_specs=[rows] * 6,
        out_shape=[jax.ShapeDtypeStruct((m, D_MODEL), jnp.float32)] * 6,
        compiler_params=pltpu.CompilerParams(
            dimension_semantics=("arbitrary",),
            vmem_limit_bytes=V7X_VMEM_LIMIT_BYTES),
        name="rwkv_project_step",
    )(x2d, xprev2d, *consts)


def _rwkv_project_seq(x3d, shift, *params):
    b_, t_, _ = x3d.shape
    m = b_ * t_
    tm = RW_ROW_TILE
    nt = t_ // tm
    consts = _rwkv_consts(*params)
    full = lambda a: pl.BlockSpec(a.shape, lambda b, i: (0,) * a.ndim)
    rows = pl.BlockSpec((tm, D_MODEL), lambda b, i: (b * nt + i, 0))
    halo = pl.BlockSpec((SUBLANES, D_MODEL), lambda b, i: (jnp.maximum((b * nt + i) * (tm // SUBLANES) - 1, 0), 0))
    shift8 = jnp.pad(shift[:, None, :], ((0, 0), (SUBLANES - 1, 0), (0, 0)))
    tmaj = pl.BlockSpec((tm, RW_PAIRS, RW_PAIR_LANES), lambda b, i: (i, b, 0))
    x2d = x3d.reshape(m, D_MODEL)
    return pl.pallas_call(
        _rwkv_proj_seq_kernel,
        grid=(b_, nt),
        in_specs=[rows, halo, pl.BlockSpec((1, SUBLANES, D_MODEL), lambda b, i: (b, 0, 0))]
        + [full(a) for a in consts],
        out_specs=[tmaj] * 5 + [rows],
        out_shape=[jax.ShapeDtypeStruct((t_, b_ * RW_PAIRS, RW_PAIR_LANES), jnp.float32)] * 5
        + [jax.ShapeDtypeStruct((m, D_MODEL), jnp.float32)],
        compiler_params=pltpu.CompilerParams(
            dimension_semantics=("parallel", "parallel"),
            vmem_limit_bytes=V7X_VMEM_LIMIT_BYTES),
        name="rwkv_project_seq",
    )(x2d, x2d, shift8, *consts)


RW_LANES = 128
RW_TIME_CHUNK = 32


def _rwkv_scan_kernel(r_ref, d_ref, k_ref, v_ref, a_ref, s0_ref, kk_ref, ka_ref, rk_ref, gg_ref, gb_ref,
                      z_ref, s_out_ref, s_ref, vec_ref):
    c = pl.program_id(1)
    n = RW_HEAD
    tc = r_ref.shape[0]
    low_half = lax.broadcasted_iota(jnp.int32, (n, RW_LANES), 1) < n

    @pl.when(c == 0)
    def _():
        s_ref[...] = s0_ref[...]

    def swap_layout(x):
        xt = jnp.concatenate([x, x], axis=0).T
        return jnp.where(low_half, xt[:n], xt[n:])

    def prepare(t, slot):
        r, k, a = swap_layout(r_ref[t]), swap_layout(k_ref[t]), swap_layout(a_ref[t])
        kkr = k * kk_ref[...]
        nrm = jnp.sqrt(jnp.sum(kkr * kkr, axis=0, keepdims=True))
        kk = kkr / jnp.maximum(nrm, 1e-12)
        vec_ref[slot, 0] = kk
        vec_ref[slot, 1] = swap_layout(d_ref[t])
        vec_ref[slot, 2] = kk * a
        vec_ref[slot, 3] = k * (1.0 + (a - 1.0) * ka_ref[...])
        vec_ref[slot, 4] = r
        vec_ref[slot, 5] = swap_layout(v_ref[t])

    def step(t, slot):
        row = lambda q, j: vec_ref[slot, q, j:j + 1, :]
        v = vec_ref[slot, 5]
        lanes = 4
        sa_parts = [s_ref[j] * row(0, j) for j in range(lanes)]
        for j in range(lanes, n):
            sa_parts[j % lanes] = sa_parts[j % lanes] + s_ref[j] * row(0, j)
        sa = _tree_sum(sa_parts)
        y_parts = []
        for j in range(n):
            sn = s_ref[j] * row(1, j) - sa * row(2, j) + v * row(3, j)
            s_ref[j] = sn
            if j < lanes:
                y_parts.append(sn * row(4, j))
            else:
                y_parts[j % lanes] = y_parts[j % lanes] + sn * row(4, j)
        y = _tree_sum(y_parts)
        mu = jnp.mean(y, axis=0, keepdims=True)
        yc = y - mu
        var = jnp.mean(yc * yc, axis=0, keepdims=True)
        bonus = jnp.sum(vec_ref[slot, 4] * vec_ref[slot, 3] * rk_ref[...], axis=0, keepdims=True)
        z_ref[t] = swap_layout(yc * lax.rsqrt(var + RW_GN_EPS) * gg_ref[...] + gb_ref[...] + bonus * v)

    prepare(0, 0)
    if tc == 1:
        step(0, 0)
    else:
        def two_steps(i, carry):
            t = 2 * i
            prepare(t + 1, 1)
            step(t, 0)
            prepare(jnp.minimum(t + 2, tc - 1), 0)
            step(t + 1, 1)
            return carry

        lax.fori_loop(0, tc // 2, two_steps, 0)

    @pl.when(c == pl.num_programs(1) - 1)
    def _():
        s_out_ref[...] = s_ref[...]


RW_SEQ_PER_TILE = RW_LANES // RW_HEADS


def _rwkv_lane_heads():
    half = jnp.arange(2)[:, None, None]
    pair = jnp.arange(RW_PAIRS)[None, None, :]
    return jnp.broadcast_to(2 * pair + half, (2, RW_SEQ_PER_TILE, RW_PAIRS)).reshape(RW_LANES)


def _rwkv_scan(r, d, k, v, a, s0, k_k, k_a, r_k, gn_g, gn_b):
    t_, rows, _ = r.shape
    n = RW_HEAD
    tile_rows = RW_SEQ_PER_TILE * RW_PAIRS
    tc = RW_TIME_CHUNK if t_ % RW_TIME_CHUNK == 0 else t_
    table = lambda p: p.reshape(RW_HEADS, n)[_rwkv_lane_heads()].T
    seq = pl.BlockSpec((tc, tile_rows, RW_PAIR_LANES), lambda l, c: (c, l, 0))
    state = pl.BlockSpec((n, n, RW_LANES), lambda l, c: (0, 0, l))
    tab = pl.BlockSpec((n, RW_LANES), lambda l, c: (0, 0))
    return pl.pallas_call(
        _rwkv_scan_kernel,
        grid=(rows // tile_rows, t_ // tc),
        in_specs=[seq] * 5 + [state] + [tab] * 5,
        out_specs=[seq, state],
        out_shape=[jax.ShapeDtypeStruct(r.shape, jnp.float32),
                   jax.ShapeDtypeStruct(s0.shape, jnp.float32)],
        scratch_shapes=[pltpu.VMEM((n, n, RW_LANES), jnp.float32),
                        pltpu.VMEM((2, 6, n, RW_LANES), jnp.float32)],
        compiler_params=pltpu.CompilerParams(
            dimension_semantics=("parallel", "arbitrary"),
            vmem_limit_bytes=V7X_VMEM_LIMIT_BYTES),
        name="rwkv_scan",
    )(r, d, k, v, a, s0, table(k_k), table(k_a), table(r_k), table(gn_g), table(gn_b))


def _rwkv_state_to_lanes(wkv):
    b_ = wkv.shape[0]
    w = wkv.astype(jnp.float32).reshape(b_ // RW_SEQ_PER_TILE, RW_SEQ_PER_TILE, RW_PAIRS, 2, RW_HEAD, RW_HEAD)
    return jnp.transpose(w, (5, 4, 0, 3, 1, 2)).reshape(RW_HEAD, RW_HEAD, b_ * RW_HEADS)


def _rwkv_state_from_lanes(s, b_):
    w = s.reshape(RW_HEAD, RW_HEAD, b_ // RW_SEQ_PER_TILE, 2, RW_SEQ_PER_TILE, RW_PAIRS)
    return jnp.transpose(w, (2, 4, 5, 3, 1, 0)).reshape(b_, RW_HEADS, RW_HEAD, RW_HEAD)


def _rwkv7_mixer_pallas(x3d, shift, wkv, mu, w_r, w_k, w_v, w0, w1, w2, a0, a1, a2, g1, g2,
                        k_k, k_a, r_k, gn_g, gn_b):
    b_, t_, _ = x3d.shape
    params = (mu, w_r, w_k, w_v, w0, w1, w2, a0, a1, a2, g1, g2)
    if t_ == 1:
        *seqs, g = _rwkv_project_step(x3d.reshape(b_, D_MODEL), shift, *params)
        seqs = [u.reshape(1, b_ * RW_PAIRS, RW_PAIR_LANES) for u in seqs]
    else:
        *seqs, g = _rwkv_project_seq(x3d, shift, *params)
    z, s = _rwkv_scan(*seqs, _rwkv_state_to_lanes(wkv), k_k, k_a, r_k, gn_g, gn_b)
    return z, g, x3d[:, -1], _rwkv_state_from_lanes(s, b_).astype(wkv.dtype)


def _proj_gate_ln_kernel(x_ref, h_ref, gate_ref, w_ref, g_ref, b_ref, o_ref):
    h = (h_ref[...] * gate_ref[...]).astype(MXU_DTYPE)
    y = ALPHA * x_ref[...] + jnp.dot(h, w_ref[...], preferred_element_type=jnp.float32)
    o_ref[...] = _ln_rows(y, g_ref[...], b_ref[...])


def _proj_gate_ln_tmajor_kernel(x_ref, h_ref, gate_ref, w_ref, g_ref, b_ref, o_ref):
    h = jnp.concatenate([h_ref[:, hp, :] for hp in range(RW_PAIRS)], axis=1)
    h = (h * gate_ref[...]).astype(MXU_DTYPE)
    y = ALPHA * x_ref[...] + jnp.dot(h, w_ref[...], preferred_element_type=jnp.float32)
    o_ref[...] = _ln_rows(y, g_ref[...], b_ref[...])


def _proj_gate_post_norm(x2d, h_tmajor, gate2d, w_out, g, b):
    m = x2d.shape[0]
    t_ = h_tmajor.shape[0]
    b_ = m // t_
    weight = pl.BlockSpec((D_MODEL, D_MODEL), lambda *_: (0, 0))
    vec = pl.BlockSpec((1, D_MODEL), lambda *_: (0, 0))
    args = (gate2d, w_out, g.reshape(1, D_MODEL), b.reshape(1, D_MODEL))
    out_shape = jax.ShapeDtypeStruct((m, D_MODEL), jnp.float32)
    if t_ == 1:
        rows = pl.BlockSpec((m, D_MODEL), lambda i: (0, 0))
        return pl.pallas_call(
            _proj_gate_ln_kernel, grid=(1,), in_specs=[rows, rows, rows, weight, vec, vec], out_specs=rows,
            out_shape=out_shape, name="proj_gate_post_norm_step",
        )(x2d, h_tmajor.reshape(m, D_MODEL), *args)
    tm = ROW_TILE
    nt = t_ // tm
    rows = pl.BlockSpec((tm, D_MODEL), lambda bb, i: (bb * nt + i, 0))
    return pl.pallas_call(
        _proj_gate_ln_tmajor_kernel,
        grid=(b_, nt),
        in_specs=[rows, pl.BlockSpec((tm, RW_PAIRS, RW_PAIR_LANES), lambda bb, i: (i, bb, 0)), rows,
                  weight, vec, vec],
        out_specs=rows,
        out_shape=out_shape,
        compiler_params=pltpu.CompilerParams(
            dimension_semantics=("parallel", "parallel"),
            vmem_limit_bytes=V7X_VMEM_LIMIT_BYTES),
        name="proj_gate_post_norm",
    )(x2d, h_tmajor, *args)


GM_ROW_TILE = 256


def _gmlp_kernel(x_ref, win_ref, lng_ref, lnb_ref, mixw_ref, mixb_ref, wout_ref, g_ref, b_ref, *out_refs,
                 chunk_len, emit_v):
    x = x_ref[...]
    h = jax.nn.gelu(jnp.dot(x.astype(MXU_DTYPE), win_ref[...], preferred_element_type=jnp.float32))
    u = h[:, :GM_WIDTH]
    v = _ln_rows(h[:, GM_WIDTH:], lng_ref[...], lnb_ref[...])
    if emit_v:
        out_refs[1][...] = v
    if chunk_len == 1:
        gated = u * (v * mixw_ref[...] + mixb_ref[...])
    else:
        tm = x.shape[0]
        causal = (lax.broadcasted_iota(jnp.int32, (chunk_len, chunk_len), 0)
                  >= lax.broadcasted_iota(jnp.int32, (chunk_len, chunk_len), 1))
        vb = v.astype(MXU_DTYPE)
        cols = []
        for g in range(GM_GROUPS):
            w = jnp.where(causal, mixw_ref[g], 0.0).astype(MXU_DTYPE)
            bias = mixb_ref[:, g:g + 1]
            lanes = slice(g * GM_GROUP_DIM, (g + 1) * GM_GROUP_DIM)
            rows = [jnp.dot(w, vb[c * chunk_len:(c + 1) * chunk_len, lanes],
                            preferred_element_type=jnp.float32) + bias
                    for c in range(tm // chunk_len)]
            cols.append(jnp.concatenate(rows, axis=0))
        gated = u * jnp.concatenate(cols, axis=1)
    y = ALPHA * x + jnp.dot(gated.astype(MXU_DTYPE), wout_ref[...], preferred_element_type=jnp.float32)
    out_refs[0][...] = _ln_rows(y, g_ref[...], b_ref[...])


def _gmlp_block(x2d, seq_len, w_in, ln_g, ln_b, ws, bs, w_out, g, b, emit_v):
    m = x2d.shape[0]
    chunk_len = min(seq_len, CHUNK)
    if chunk_len == 1:
        tm = m
        mixw = jnp.repeat(ws[:, 0, 0], GM_GROUP_DIM)[None, :]
        mixb = jnp.repeat(bs[:, 0], GM_GROUP_DIM)[None, :]
    else:
        tm = GM_ROW_TILE
        mixw = ws[:, :chunk_len, :chunk_len]
        mixb = bs[:, :chunk_len].T
    full = lambda a: pl.BlockSpec(a.shape, lambda i: (0,) * a.ndim)
    rows = lambda n: pl.BlockSpec((tm, n), lambda i: (i, 0))
    consts = [w_in.astype(MXU_DTYPE), ln_g.reshape(1, GM_WIDTH), ln_b.reshape(1, GM_WIDTH), mixw, mixb,
              w_out.astype(MXU_DTYPE), g.reshape(1, D_MODEL), b.reshape(1, D_MODEL)]
    out_specs = [rows(D_MODEL)] + ([rows(GM_WIDTH)] if emit_v else [])
    out_shape = [jax.ShapeDtypeStruct((m, D_MODEL), jnp.float32)] + (
        [jax.ShapeDtypeStruct((m, GM_WIDTH), jnp.float32)] if emit_v else [])
    return pl.pallas_call(
        functools.partial(_gmlp_kernel, chunk_len=chunk_len, emit_v=emit_v),
        grid=(m // tm,),
        in_specs=[rows(D_MODEL)] + [full(a) for a in consts],
        out_specs=out_specs,
        out_shape=out_shape,
        compiler_params=pltpu.CompilerParams(
            dimension_semantics=("parallel",),
            vmem_limit_bytes=V7X_VMEM_LIMIT_BYTES),
        name="gmlp_block",
    )(x2d, *consts)


SSM_ROW_TILE = 256
SSM_BC_DIM = SSM_GROUPS * SSM_STATE
SSM_DT_LANES = 128
SUBLANES = 8


def _ssm_activate(xb, xbc, taps, wz_ref, wdt_ref, cw_ref, cb_ref, dtb_ref, z_ref, xs_ref, bm_ref, cm_ref, dt_ref):
    conv = cb_ref[...] + xbc * cw_ref[SSM_CONV - 1:SSM_CONV, :]
    for j in range(SSM_CONV - 1):
        conv = conv + taps[j] * cw_ref[j:j + 1, :]
    act = conv * jax.nn.sigmoid(conv)
    xs_ref[...] = act[:, :SSM_D_INNER]
    bm_ref[...] = act[:, SSM_D_INNER:SSM_D_INNER + SSM_BC_DIM].astype(bm_ref.dtype)
    cm_ref[...] = act[:, SSM_D_INNER + SSM_BC_DIM:].astype(cm_ref.dtype)
    z_ref[...] = jnp.dot(xb, wz_ref[...], preferred_element_type=jnp.float32)
    dt_ref[...] = jax.nn.softplus(jnp.dot(xb, wdt_ref[...], preferred_element_type=jnp.float32) + dtb_ref[...])


def _ssm_proj_seq_kernel(x_ref, halo_ref, cs_ref, wx_ref, wz_ref, wdt_ref, cw_ref, cb_ref, dtb_ref,
                         z_ref, xs_ref, bm_ref, cm_ref, dt_ref, tail_ref):
    i = pl.program_id(1)
    xb = x_ref[...].astype(MXU_DTYPE)
    xbc = jnp.dot(xb, wx_ref[...], preferred_element_type=jnp.float32)
    tm = xbc.shape[0]
    prev = jnp.dot(halo_ref[...].astype(MXU_DTYPE), wx_ref[...], preferred_element_type=jnp.float32)
    prev = jnp.where(i == 0, cs_ref[0], prev)
    row = lax.broadcasted_iota(jnp.int32, (tm, 1), 0)
    pad = jnp.zeros((tm - SUBLANES, xbc.shape[1]), jnp.float32)
    taps = []
    for j in range(SSM_CONV - 1):
        back = SSM_CONV - 1 - j
        head = jnp.concatenate([pltpu.roll(prev, back, 0), pad], axis=0)
        taps.append(jnp.where(row < back, head, pltpu.roll(xbc, back, 0)))
    _ssm_activate(xb, xbc, taps, wz_ref, wdt_ref, cw_ref, cb_ref, dtb_ref, z_ref, xs_ref, bm_ref, cm_ref, dt_ref)
    tail_ref[0] = xbc[tm - SUBLANES:, :]


def _ssm_proj_step_kernel(x_ref, st_ref, wx_ref, wz_ref, wdt_ref, cw_ref, cb_ref, dtb_ref,
                          z_ref, xs_ref, bm_ref, cm_ref, dt_ref, st_out_ref):
    xb = x_ref[...].astype(MXU_DTYPE)
    xbc = jnp.dot(xb, wx_ref[...], preferred_element_type=jnp.float32)
    taps = [st_ref[j] for j in range(SSM_CONV - 1)]
    _ssm_activate(xb, xbc, taps, wz_ref, wdt_ref, cw_ref, cb_ref, dtb_ref, z_ref, xs_ref, bm_ref, cm_ref, dt_ref)
    for j in range(SSM_CONV - 2):
        st_out_ref[j] = st_ref[j + 1]
    st_out_ref[SSM_CONV - 2] = xbc


def _ssm_project(x3d, conv_state, w_in, conv_w, conv_b, dt_bias):
    b_, t_, _ = x3d.shape
    m = b_ * t_
    w_z, w_x, w_dt = jnp.split(w_in, [SSM_D_INNER, SSM_D_INNER + SSM_CONV_DIM], axis=1)
    cast = lambda w: w.astype(MXU_DTYPE)
    consts = [cast(w_x), cast(w_z), cast(jnp.pad(w_dt, ((0, 0), (0, SSM_DT_LANES - SSM_HEADS)))),
              conv_w, conv_b.reshape(1, SSM_CONV_DIM),
              jnp.pad(dt_bias, (0, SSM_DT_LANES - SSM_HEADS)).reshape(1, SSM_DT_LANES)]
    sds = jax.ShapeDtypeStruct
    outs = [sds((m, SSM_D_INNER), jnp.float32), sds((m, SSM_D_INNER), jnp.float32),
            sds((m, SSM_BC_DIM), MXU_DTYPE), sds((m, SSM_BC_DIM), MXU_DTYPE), sds((m, SSM_DT_LANES), jnp.float32)]
    widths = [SSM_D_INNER, SSM_D_INNER, SSM_BC_DIM, SSM_BC_DIM, SSM_DT_LANES]
    params = dict(vmem_limit_bytes=V7X_VMEM_LIMIT_BYTES)
    x2d = x3d.reshape(m, D_MODEL)
    if t_ == 1:
        full = lambda a: pl.BlockSpec(a.shape, lambda i: (0,) * a.ndim)
        st = jnp.transpose(conv_state, (1, 0, 2))
        res = pl.pallas_call(
            _ssm_proj_step_kernel,
            grid=(1,),
            in_specs=[full(x2d), full(st)] + [full(a) for a in consts],
            out_specs=[pl.BlockSpec((m, w), lambda i: (0, 0)) for w in widths] + [full(st)],
            out_shape=outs + [sds(st.shape, jnp.float32)],
            compiler_params=pltpu.CompilerParams(dimension_semantics=("arbitrary",), **params),
            name="ssm_project_step",
        )(x2d, st, *consts)
        return list(res[:5]) + [jnp.transpose(res[5], (1, 0, 2))]
    tm = SSM_ROW_TILE
    nt = t_ // tm
    full = lambda a: pl.BlockSpec(a.shape, lambda b, i: (0,) * a.ndim)
    rows = lambda w: pl.BlockSpec((tm, w), lambda b, i: (b * nt + i, 0))
    halo = pl.BlockSpec((SUBLANES, D_MODEL), lambda b, i: (jnp.maximum((b * nt + i) * (tm // SUBLANES) - 1, 0), 0))
    cs8 = jnp.pad(conv_state, ((0, 0), (SUBLANES - (SSM_CONV - 1), 0), (0, 0)))
    tail = pl.BlockSpec((1, SUBLANES, SSM_CONV_DIM), lambda b, i: (b, 0, 0))
    res = pl.pallas_call(
        _ssm_proj_seq_kernel,
        grid=(b_, nt),
        in_specs=[rows(D_MODEL), halo, tail] + [full(a) for a in consts],
        out_specs=[rows(w) for w in widths] + [tail],
        out_shape=outs + [sds((b_, SUBLANES, SSM_CONV_DIM), jnp.float32)],
        compiler_params=pltpu.CompilerParams(dimension_semantics=("parallel", "arbitrary"), **params),
        name="ssm_project_seq",
    )(x2d, x2d, cs8, *consts)
    return list(res[:5]) + [res[5][:, SUBLANES - (SSM_CONV - 1):, :]]


def _ssm_gate_norm(y, xs, z, dskip, normg):
    yg = (y + xs * dskip) * (z * jax.nn.sigmoid(z))
    gw = SSM_D_INNER // SSM_GROUPS
    outs = []
    for g in range(SSM_GROUPS):
        part = yg[:, g * gw:(g + 1) * gw]
        ms = jnp.mean(part * part, axis=-1, keepdims=True)
        outs.append(part * lax.rsqrt(ms + LN_EPS))
    return jnp.concatenate(outs, axis=1) * normg


def _ssm_chunk_kernel(xs_ref, bm_ref, cm_ref, dt_ref, z_ref, aneg_ref, dskip_ref, normg_ref,
                      yg_ref, h_out_ref, h_ref, yT_ref, xe_ref):
    c = pl.program_id(1)
    l = xs_ref.shape[0]
    hd = SSM_HEAD_DIM

    @pl.when(c == 0)
    def _():
        h_ref[...] = jnp.zeros_like(h_ref)

    dot = lambda u, w: jnp.dot(u, w, preferred_element_type=jnp.float32)
    dt = dt_ref[...]
    a = dt * aneg_ref[...]
    r_i = lax.broadcasted_iota(jnp.int32, (l, l), 0)
    c_i = lax.broadcasted_iota(jnp.int32, (l, l), 1)
    tril = jnp.where(r_i >= c_i, 1.0, 0.0)
    hi = lax.Precision.HIGHEST
    acum = jnp.dot(tril, a, precision=hi, preferred_element_type=jnp.float32)
    acum_t = jnp.dot(a.T, tril.T, precision=hi, preferred_element_type=jnp.float32)
    dt_t = dt.T
    to_end_t = jnp.exp(acum_t[:, l - 1:l] - acum_t)
    from_start_t = jnp.exp(acum_t)
    chunk_decay = jnp.exp(acum[l - 1:l, :])
    upper = r_i <= c_i
    xs = xs_ref[...]
    for g in range(SSM_GROUPS):
        bm = bm_ref[:, g * SSM_STATE:(g + 1) * SSM_STATE]
        cm_t = cm_ref[:, g * SSM_STATE:(g + 1) * SSM_STATE].astype(jnp.float32).T.astype(MXU_DTYPE)
        cb_t = dot(bm, cm_t)
        h_in = h_ref[g * SSM_HPG:(g + 1) * SSM_HPG].reshape(SSM_HPG * hd, SSM_STATE)
        y_off = dot(h_in.astype(MXU_DTYPE), cm_t)
        for e in range(SSM_HPG):
            h = g * SSM_HPG + e
            if h % 2 == 0:
                xs_pair_t = xs[:, h * hd:(h + 2) * hd].T
            xdt_t = xs_pair_t[(h % 2) * hd:(h % 2 + 1) * hd] * dt_t[h:h + 1, :]
            seg = jnp.exp(jnp.where(upper, acum_t[h:h + 1, :] - acum[:, h:h + 1], -jnp.inf))
            y_diag = dot(xdt_t.astype(MXU_DTYPE), (cb_t * seg).astype(MXU_DTYPE))
            yT_ref[h * hd:(h + 1) * hd, :] = y_diag + y_off[e * hd:(e + 1) * hd] * from_start_t[h:h + 1, :]
            xe_ref[e * hd:(e + 1) * hd, :] = (xdt_t * to_end_t[h:h + 1, :]).astype(xe_ref.dtype)
        states = dot(xe_ref[...], bm)
        for e in range(SSM_HPG):
            h = g * SSM_HPG + e
            h_ref[h] = h_ref[h] * chunk_decay[:, h:h + 1] + states[e * hd:(e + 1) * hd]
    y = jnp.concatenate([yT_ref[i * l:(i + 1) * l, :].T for i in range(SSM_D_INNER // l)], axis=1)
    yg_ref[...] = _ssm_gate_norm(y, xs, z_ref[...], dskip_ref[...], normg_ref[...]).astype(yg_ref.dtype)

    @pl.when(c == pl.num_programs(1) - 1)
    def _():
        h_out_ref[0] = h_ref[...]


def _ssm_head_lanes(p):
    return jnp.pad(p.astype(jnp.float32), (0, SSM_DT_LANES - SSM_HEADS)).reshape(1, SSM_DT_LANES)


def _ssm_chunk_scan(b_, t_, xs, bm, cm, dt, z, a_log, d_skip, norm_g):
    l = SSM_CHUNK
    nc = t_ // l
    rows = lambda w: pl.BlockSpec((l, w), lambda b, c: (b * nc + c, 0))
    vec = lambda w: pl.BlockSpec((1, w), lambda b, c: (0, 0))
    aneg = _ssm_head_lanes(-jnp.exp(a_log.astype(jnp.float32)))
    dskip = jnp.repeat(d_skip, SSM_HEAD_DIM).reshape(1, SSM_D_INNER)
    yg, h_new = pl.pallas_call(
        _ssm_chunk_kernel,
        grid=(b_, nc),
        in_specs=[rows(SSM_D_INNER), rows(SSM_BC_DIM), rows(SSM_BC_DIM), rows(SSM_DT_LANES), rows(SSM_D_INNER),
                  vec(SSM_DT_LANES), vec(SSM_D_INNER), vec(SSM_D_INNER)],
        out_specs=[rows(SSM_D_INNER),
                   pl.BlockSpec((1, SSM_HEADS, SSM_HEAD_DIM, SSM_STATE), lambda b, c: (b, 0, 0, 0))],
        out_shape=[jax.ShapeDtypeStruct((b_ * t_, SSM_D_INNER), MXU_DTYPE),
                   jax.ShapeDtypeStruct((b_, SSM_HEADS, SSM_HEAD_DIM, SSM_STATE), jnp.float32)],
        scratch_shapes=[pltpu.VMEM((SSM_HEADS, SSM_HEAD_DIM, SSM_STATE), jnp.float32),
                        pltpu.VMEM((SSM_D_INNER, l), jnp.float32),
                        pltpu.VMEM((SSM_HPG * SSM_HEAD_DIM, l), MXU_DTYPE)],
        compiler_params=pltpu.CompilerParams(
            dimension_semantics=("parallel", "arbitrary"),
            vmem_limit_bytes=V7X_VMEM_LIMIT_BYTES),
        name="ssm_chunk_scan",
    )(xs, bm, cm, dt, z, aneg, dskip, norm_g.reshape(1, SSM_D_INNER))
    return yg, h_new


def _ssm_step_kernel(h0_ref, xs_ref, dt_ref, an_ref, bm_ref, cm_ref, y_ref, h_ref):
    h0 = h0_ref[0]
    dt = dt_ref[0]
    decay = jnp.exp(dt * an_ref[...])
    xdt = xs_ref[0] * dt
    bm = bm_ref[0].astype(jnp.float32)
    cm = cm_ref[0].astype(jnp.float32)
    h_ref[0] = h0 * decay + xdt * bm
    cb = jnp.sum(cm * bm, axis=-1, keepdims=True)
    y_ref[0] = cb * xdt + jnp.sum(cm * h0, axis=-1, keepdims=True) * decay


def _ssm_step(state, xs, bm, cm, dt, a_log):
    b_ = state.shape[0]
    per_head = lambda u: jnp.repeat(u.reshape(b_, SSM_GROUPS, 1, SSM_STATE), SSM_HPG, axis=1)
    xs4 = xs.reshape(b_, SSM_HEADS, SSM_HEAD_DIM, 1)
    dt4 = dt[:, :SSM_HEADS].reshape(b_, SSM_HEADS, 1, 1)
    an = (-jnp.exp(a_log.astype(jnp.float32))).reshape(SSM_HEADS, 1, 1)
    blk = lambda a: pl.BlockSpec((1,) + a.shape[1:], lambda b: (b, 0, 0, 0))
    args = [state.astype(jnp.float32), xs4, dt4, an, per_head(bm), per_head(cm)]
    y4, h_new = pl.pallas_call(
        _ssm_step_kernel,
        grid=(b_,),
        in_specs=[blk(args[0]), blk(xs4), blk(dt4), pl.BlockSpec(an.shape, lambda b: (0, 0, 0)),
                  blk(args[4]), blk(args[5])],
        out_specs=[blk(xs4), blk(args[0])],
        out_shape=[jax.ShapeDtypeStruct(xs4.shape, jnp.float32), jax.ShapeDtypeStruct(state.shape, jnp.float32)],
        compiler_params=pltpu.CompilerParams(
            dimension_semantics=("parallel",),
            vmem_limit_bytes=V7X_VMEM_LIMIT_BYTES),
        name="ssm_step",
    )(*args)
    return y4.reshape(b_, SSM_D_INNER), h_new


def _ssm_gate_norm_kernel(y_ref, xs_ref, z_ref, dskip_ref, normg_ref, o_ref):
    o_ref[...] = _ssm_gate_norm(y_ref[...], xs_ref[...], z_ref[...], dskip_ref[...], normg_ref[...]).astype(o_ref.dtype)


def _ssm_gate_norm_rows(y, xs, z, d_skip, norm_g):
    full = lambda a: pl.BlockSpec(a.shape, lambda i: (0,) * a.ndim)
    args = [y, xs, z, jnp.repeat(d_skip, SSM_HEAD_DIM).reshape(1, SSM_D_INNER), norm_g.reshape(1, SSM_D_INNER)]
    return pl.pallas_call(
        _ssm_gate_norm_kernel,
        grid=(1,),
        in_specs=[full(a) for a in args],
        out_specs=full(y),
        out_shape=jax.ShapeDtypeStruct(y.shape, MXU_DTYPE),
        name="ssm_gate_norm",
    )(*args)


def _mamba2_mixer_pallas(x3d, conv_state, ssm_state, w_in, conv_w, conv_b, dt_bias, a_log, d_skip, norm_g):
    b_, t_, _ = x3d.shape
    z, xs, bm, cm, dt, conv_new = _ssm_project(x3d, conv_state, w_in, conv_w, conv_b, dt_bias)
    if t_ == 1:
        y, h_new = _ssm_step(ssm_state, xs, bm, cm, dt, a_log)
        yg = _ssm_gate_norm_rows(y, xs, z, d_skip, norm_g)
    else:
        yg, h_new = _ssm_chunk_scan(b_, t_, xs, bm, cm, dt, z, a_log, d_skip, norm_g)
    return yg, conv_new, h_new.astype(ssm_state.dtype)


PAGES_PER_STEP = 8


def _sortable_key(score):
    bits = pltpu.bitcast(score, jnp.int32)
    return jnp.where(bits >= 0, bits, bits ^ jnp.int32(0x7FFFFFFF))


def _dsa_decode_kernel(pt_ref, iq_ref, iw_ref, q_ref, ikn_ref, kn_ref, vn_ref, *rest,
                       topk, col_bits, n_steps, pages):
    idx_refs, k_refs, v_refs = rest[:pages], rest[pages:2 * pages], rest[2 * pages:3 * pages]
    o_ref, key_ref, sel_ref, m_ref, l_ref, acc_ref = rest[3 * pages:]
    s = pl.program_id(1)
    nk = key_ref.shape[2]
    nt = (((1,), (1,)), ((), ()))
    iq = iq_ref[0]
    iw = iw_ref[0]
    lane = lax.broadcasted_iota(jnp.int32, (1, nk), 1)

    def row_dot(a, row):
        return jnp.sum(a.astype(jnp.float32) * row.astype(jnp.float32), axis=1, keepdims=True)

    def index_score(sc):
        return jnp.sum(iw * jnp.maximum(sc, 0.0), axis=0, keepdims=True)

    def fold(hit):
        out = hit[:, 0:128]
        for l in range(1, nk // 128):
            out = out + hit[:, l * 128:(l + 1) * 128]
        return out

    @pl.when(s < n_steps)
    def _score():
        ik_t = jnp.concatenate([r[0] for r in idx_refs], axis=1).astype(MXU_DTYPE)
        key_ref[s] = _sortable_key(index_score(jnp.dot(iq, ik_t, preferred_element_type=jnp.float32)))

    @pl.when(s == n_steps - 1)
    def _select():
        key_new = _sortable_key(index_score(row_dot(iq, ikn_ref[0])))

        def count(pred_past, pred_new):
            cnt = jnp.zeros((1, 128), jnp.float32)
            for st in range(n_steps):
                cnt = cnt + fold(jnp.where(pred_past(key_ref[st], lane + st * nk), 1.0, 0.0))
            return jnp.sum(cnt, axis=1, keepdims=True) + jnp.where(pred_new(key_new), 1.0, 0.0)

        def bit_body(i, thr):
            cand = thr ^ lax.shift_left(jnp.int32(1), jnp.int32(31) - i)
            total = count(lambda k, c: k >= cand, lambda k: k >= cand)
            return jnp.where(total >= float(topk), cand, thr)

        thr = lax.fori_loop(0, 32, bit_body, jnp.full((1, 1), INT32_MIN, jnp.int32))
        need = float(topk) - count(lambda k, c: k > thr, lambda k: k > thr)

        def col_body(i, last):
            cand = last | lax.shift_left(jnp.int32(1), jnp.int32(col_bits - 1) - i)
            ties = count(lambda k, c: (k == thr) & (c < cand),
                         lambda k: (k == thr) & (jnp.int32(n_steps * nk) < cand))
            return jnp.where(ties < need, cand, last)

        last_tie = lax.fori_loop(0, col_bits, col_body, jnp.zeros((1, 1), jnp.int32))
        sel_ref[0] = jnp.broadcast_to(thr, sel_ref.shape[1:])
        sel_ref[1] = jnp.broadcast_to(last_tie, sel_ref.shape[1:])
        sel_ref[2] = jnp.broadcast_to(key_new, sel_ref.shape[1:])

    gsz = ATT_HEADS // ATT_KV_HEADS
    q = q_ref[0]
    q_wide = jnp.concatenate([q] * ATT_KV_HEADS, axis=1)
    head_i = lax.broadcasted_iota(jnp.int32, q_wide.shape, 0)
    col_i = lax.broadcasted_iota(jnp.int32, q_wide.shape, 1)
    own_group = (col_i // ATT_HEAD_DIM) == (head_i // gsz)
    q_blk = jnp.where(own_group, q_wide, jnp.zeros_like(q_wide))

    def keep_mask(key, col):
        thr, last_tie = sel_ref[0, 0:1, 0:1], sel_ref[1, 0:1, 0:1]
        return (key > thr) | ((key == thr) & (col <= last_tie))

    def online_update(logits, weighted_values):
        m_old = m_ref[...]
        m_new = jnp.maximum(m_old, jnp.max(logits, axis=1, keepdims=True))
        p = jnp.exp(logits - m_new)
        alpha = jnp.exp(m_old - m_new)
        l_ref[...] = alpha * l_ref[...] + jnp.sum(p, axis=1, keepdims=True)
        acc_ref[...] = alpha * acc_ref[...] + weighted_values(p.astype(MXU_DTYPE))
        m_ref[...] = m_new

    @pl.when(s == n_steps)
    def _init():
        m_ref[...] = jnp.full(m_ref.shape, MASK_NEG, jnp.float32)
        l_ref[...] = jnp.zeros(l_ref.shape, jnp.float32)
        acc_ref[...] = jnp.zeros(acc_ref.shape, jnp.float32)

    @pl.when(s >= n_steps)
    def _attend():
        st = s - n_steps
        bias = jnp.where(keep_mask(key_ref[st], lane + st * nk), 0.0, MASK_NEG)
        k_t = jnp.concatenate([r[0] for r in k_refs], axis=1).astype(MXU_DTYPE)
        v_t = jnp.concatenate([r[0] for r in v_refs], axis=1).astype(MXU_DTYPE)
        online_update(jnp.dot(q_blk, k_t, preferred_element_type=jnp.float32) + bias,
                      lambda p: lax.dot_general(p, v_t, nt, preferred_element_type=jnp.float32))

    @pl.when(s == 2 * n_steps - 1)
    def _finish():
        keep_new = keep_mask(sel_ref[2, 0:1, 0:1], jnp.int32(n_steps * nk))
        logit = row_dot(q_blk, kn_ref[0])
        v_row = vn_ref[0].astype(jnp.float32)
        online_update(logit + jnp.where(keep_new, 0.0, MASK_NEG), lambda p: p.astype(jnp.float32) * v_row)
        out = jnp.where(own_group, acc_ref[...] / l_ref[...], 0.0)
        o = out[:, 0:ATT_HEAD_DIM]
        for g in range(1, ATT_KV_HEADS):
            o = o + out[:, g * ATT_HEAD_DIM:(g + 1) * ATT_HEAD_DIM]
        o_ref[0] = o.astype(o_ref.dtype)


def _dsa_decode(q, iq, iw, ik_new, k_new, v_new, cache_k, cache_v, cache_idx_k, page_table):
    b_, n_pages = page_table.shape
    n_pool, page = cache_k.shape[0], cache_k.shape[1]
    pages = PAGES_PER_STEP
    n_steps = n_pages // pages
    past = n_pages * page
    ck = jnp.transpose(cache_k, (0, 2, 3, 1)).reshape(n_pool, ATT_KV_DIM, page)
    cv = jnp.transpose(cache_v, (0, 2, 3, 1)).reshape(n_pool, ATT_KV_DIM, page)
    cik = jnp.swapaxes(cache_idx_k, 1, 2)
    per_seq = lambda a: pl.BlockSpec((1,) + a.shape[1:], lambda b, s, pt: (b,) + (0,) * (a.ndim - 1))

    def paged(width, j, attend_phase):
        def index(b, s, pt):
            grp = jnp.maximum(s - n_steps, 0) if attend_phase else jnp.minimum(s, n_steps - 1)
            return (pt[b, grp * pages + j], 0, 0)
        return pl.BlockSpec((1, width, page), index)

    small = [iq.reshape(b_, IDX_HEADS, IDX_DIM), iw[:, :IDX_HEADS].reshape(b_, IDX_HEADS, 1),
             q.reshape(b_, ATT_HEADS, ATT_HEAD_DIM), ik_new.astype(MXU_DTYPE).reshape(b_, 1, IDX_DIM),
             k_new.astype(MXU_DTYPE).reshape(b_, 1, ATT_KV_DIM), v_new.astype(MXU_DTYPE).reshape(b_, 1, ATT_KV_DIM)]
    grid_spec = pltpu.PrefetchScalarGridSpec(
        num_scalar_prefetch=1,
        grid=(b_, 2 * n_steps),
        in_specs=[per_seq(a) for a in small]
        + [paged(IDX_DIM, j, False) for j in range(pages)]
        + [paged(ATT_KV_DIM, j, True) for j in range(pages)]
        + [paged(ATT_KV_DIM, j, True) for j in range(pages)],
        out_specs=pl.BlockSpec((1, ATT_HEADS, ATT_HEAD_DIM), lambda b, s, pt: (b, 0, 0)),
        scratch_shapes=[pltpu.VMEM((n_steps, 1, pages * page), jnp.int32),
                        pltpu.VMEM((3, SUBLANES, 128), jnp.int32),
                        pltpu.VMEM((ATT_HEADS, 1), jnp.float32),
                        pltpu.VMEM((ATT_HEADS, 1), jnp.float32),
                        pltpu.VMEM((ATT_HEADS, ATT_KV_DIM), jnp.float32)])
    o = pl.pallas_call(
        functools.partial(_dsa_decode_kernel, topk=min(TOPK_MAX, (past + 1) // 4),
                          col_bits=max(1, past.bit_length()), n_steps=n_steps, pages=pages),
        grid_spec=grid_spec,
        out_shape=jax.ShapeDtypeStruct((b_, ATT_HEADS, ATT_HEAD_DIM), MXU_DTYPE),
        compiler_params=pltpu.CompilerParams(
            dimension_semantics=("parallel", "arbitrary"),
            vmem_limit_bytes=V7X_VMEM_LIMIT_BYTES),
        name="dsa_decode",
    )(page_table, *small, *([cik] * pages), *([ck] * pages), *([cv] * pages))
    return o.reshape(b_, ATT_Q_DIM)


def _dsa_sample_pallas(x3d, cache_k, cache_v, cache_idx_k, page_table, w_in, kn_g, kn_b):
    b_, t_, _ = x3d.shape
    past = page_table.shape[1] * cache_k.shape[1]
    pos = jnp.full((b_,), past, jnp.int32)
    q, iq, v, _, iw, kT, _, ikT, _ = _dsa_project(x3d.reshape(1, b_, D_MODEL), pos, w_in, kn_g, kn_b)
    k = _untranspose_groups(kT)[0]
    ik = _untranspose_groups(ikT)[0]
    o = _dsa_decode(q, iq, iw, ik, k, v, cache_k, cache_v, cache_idx_k, page_table)
    kv4 = lambda u: u.reshape(b_, t_, ATT_KV_HEADS, ATT_HEAD_DIM)
    return o, kv4(k), kv4(v), ik.reshape(b_, t_, IDX_DIM)


def _layer_norm(x, g, b):
    xf = x.astype(jnp.float32)
    mu = jnp.mean(xf, -1, keepdims=True)
    var = jnp.mean(jnp.square(xf - mu), -1, keepdims=True)
    return ((xf - mu) * lax.rsqrt(var + LN_EPS)).astype(x.dtype) * g + b


def _rope_partial(x, pos, rot_dim):
    half = rot_dim // 2
    inv = ROPE_THETA ** (-jnp.arange(half, dtype=jnp.float32) / half)
    ang = pos.astype(jnp.float32)[:, None] * inv[None, :]
    cos = jnp.cos(ang)[:, None, :]
    sin = jnp.sin(ang)[:, None, :]
    xf = x[..., :rot_dim].astype(jnp.float32)
    x1, x2 = xf[..., :half], xf[..., half:]
    rot = jnp.concatenate([x1 * cos - x2 * sin, x2 * cos + x1 * sin], axis=-1).astype(x.dtype)
    return jnp.concatenate([rot, x[..., rot_dim:]], axis=-1)


def _gather_rows(rows, idx):
    return jax.vmap(lambda r, i: r[i])(rows, idx)


def _gmlp_mixer(x, w_in, ln_g, ln_b, ws, bs):
    B_, T, _ = x.shape
    u, v = jnp.split(jax.nn.gelu(x @ w_in), 2, axis=-1)
    v = _layer_norm(v, ln_g, ln_b)
    l = min(T, CHUNK)
    c = T // l
    mask = jnp.tril(jnp.ones((l, l), dtype=bool))
    w = jnp.where(mask, ws[:, :l, :l], 0.0)
    vc = v.reshape(B_, c, l, GM_GROUPS, GM_GROUP_DIM)
    mixed = jnp.einsum('gts,bcsgd->bctgd', w, vc) + jnp.transpose(bs[:, :l])[:, :, None]
    return u * mixed.reshape(B_, T, GM_WIDTH), v


def _ssd_chunked(xs, dt, a, bm, cm, h0):
    B_, T = xs.shape[:2]
    l = min(T, SSM_CHUNK)
    c = T // l
    blk = lambda t: t.reshape((B_, c, l) + t.shape[2:])
    xdt = blk(xs.astype(jnp.float32) * dt[..., None])
    bc, cc, acum = blk(bm), blk(cm), jnp.cumsum(blk(a), axis=2)
    at = jnp.moveaxis(acum, 2, -1)
    causal = jnp.tril(jnp.ones((l, l), dtype=bool))
    seg = jnp.exp(jnp.where(causal, at[..., :, None] - at[..., None, :], -jnp.inf))
    cb = jnp.einsum('bctgn,bcsgn->bcgts', cc, bc)
    y_diag = jnp.einsum('bcgts,bcgets,bcsgep->bctgep', cb, seg, xdt)
    states = jnp.einsum('bclgn,bclge,bclgep->bcgepn', bc, jnp.exp(acum[:, :, -1:] - acum), xdt)
    chunk_decay = jnp.exp(acum[:, :, -1])

    def step(h, inp):
        dec, st = inp
        return h * dec[..., None, None] + st, h

    h_last, h_in = lax.scan(step, h0, (jnp.moveaxis(chunk_decay, 1, 0), jnp.moveaxis(states, 1, 0)))
    y_off = jnp.einsum('bctgn,bcgepn,bctge->bctgep', cc, jnp.moveaxis(h_in, 0, 1), jnp.exp(acum))
    return (y_diag + y_off).reshape(B_, T, SSM_GROUPS, SSM_HPG, SSM_HEAD_DIM), h_last


def _mamba2_mixer(x, conv_state, ssm_state, w_in, conv_w, conv_b, dt_bias, a_log, d_skip, norm_g):
    B_, T, _ = x.shape
    z, xbc, dt = jnp.split(x @ w_in, [SSM_D_INNER, SSM_D_INNER + SSM_CONV_DIM], axis=-1)
    xbc_ext = jnp.concatenate([conv_state, xbc], axis=1)
    conv = conv_b
    for j in range(SSM_CONV):
        conv = conv + xbc_ext[:, j:j + T] * conv_w[j]
    xbc = jax.nn.silu(conv)
    xs, bm, cm = jnp.split(xbc, [SSM_D_INNER, SSM_D_INNER + SSM_GROUPS * SSM_STATE], axis=-1)
    xs = xs.reshape(B_, T, SSM_GROUPS, SSM_HPG, SSM_HEAD_DIM)
    bm = bm.reshape(B_, T, SSM_GROUPS, SSM_STATE)
    cm = cm.reshape(B_, T, SSM_GROUPS, SSM_STATE)
    dt = jax.nn.softplus((dt + dt_bias).astype(jnp.float32)).reshape(B_, T, SSM_GROUPS, SSM_HPG)
    a_neg = -jnp.exp(a_log.astype(jnp.float32)).reshape(SSM_GROUPS, SSM_HPG)
    h0 = ssm_state.astype(jnp.float32).reshape(B_, SSM_GROUPS, SSM_HPG, SSM_HEAD_DIM, SSM_STATE)
    y, h_last = _ssd_chunked(xs, dt, dt * a_neg, bm, cm, h0)
    y = y.astype(x.dtype) + xs * d_skip.reshape(SSM_GROUPS, SSM_HPG, 1)
    yg = (y.reshape(B_, T, SSM_D_INNER) * jax.nn.silu(z)).reshape(B_, T, SSM_GROUPS, -1).astype(jnp.float32)
    yg = (yg * lax.rsqrt(jnp.mean(jnp.square(yg), -1, keepdims=True) + LN_EPS)).astype(x.dtype)
    yg = yg.reshape(B_, T, SSM_D_INNER) * norm_g
    new_ssm = h_last.reshape(B_, SSM_HEADS, SSM_HEAD_DIM, SSM_STATE).astype(ssm_state.dtype)
    return yg, xbc_ext[:, T:], new_ssm


def _dsa_project_jax(x, pos, w_in, kn_g, kn_b):
    B_, T, _ = x.shape
    q, k, v, iq, ik, iw = jnp.split(x @ w_in, list(ATT_IN_SPLITS), axis=-1)
    q = _rope_partial(q.reshape(B_, T, ATT_HEADS, ATT_HEAD_DIM), pos, ROPE_DIM)
    k = _rope_partial(k.reshape(B_, T, ATT_KV_HEADS, ATT_HEAD_DIM), pos, ROPE_DIM)
    v = v.reshape(B_, T, ATT_KV_HEADS, ATT_HEAD_DIM)
    iq = _rope_partial(iq.reshape(B_, T, IDX_HEADS, IDX_DIM), pos, IDX_ROPE_DIM)
    ik = _rope_partial(_layer_norm(ik, kn_g, kn_b)[:, :, None, :], pos, IDX_ROPE_DIM)[:, :, 0, :]
    iw = iw * (IDX_HEADS ** -0.5 * IDX_DIM ** -0.5)
    return q, k, v, iq, ik, iw


def _dsa_select(iq, iw, ik, qpos, topk):
    s = jnp.einsum('bqhd,bsd->bqhs', iq, ik)
    score = jnp.einsum('bqh,bqhs->bqs', iw, jax.nn.relu(s)).astype(jnp.float32)
    adm = jnp.arange(ik.shape[1])[None, :] <= qpos[:, None]
    score = jnp.where(adm[None], score, -jnp.inf)
    _, idx = lax.top_k(score, topk)
    return idx, idx <= qpos[None, :, None]


def _sparse_attend(q, k_sel, v_sel, valid):
    B_, Q = q.shape[:2]
    qg = q.reshape(B_, Q, ATT_KV_HEADS, ATT_HEADS // ATT_KV_HEADS, ATT_HEAD_DIM)
    s = jnp.einsum('bqhgd,bqkhd->bqhgk', qg, k_sel).astype(jnp.float32) * (ATT_HEAD_DIM ** -0.5)
    s = jnp.where(valid[:, :, None, None, :], s, -jnp.inf)
    p = jax.nn.softmax(s, axis=-1).astype(v_sel.dtype)
    o = jnp.einsum('bqhgk,bqkhd->bqhgd', p, v_sel)
    return o.reshape(B_, Q, ATT_Q_DIM)


def _dsa_prompt(x, w_in, kn_g, kn_b):
    B_, T, _ = x.shape
    q, k, v, iq, ik, iw = _dsa_project_jax(x, jnp.arange(T), w_in, kn_g, kn_b)
    topk = min(TOPK_MAX, T // 4)

    def block(bi):
        t0 = bi * Q_BLOCK
        sl = lambda t: lax.dynamic_slice_in_dim(t, t0, Q_BLOCK, axis=1)
        qpos = t0 + jnp.arange(Q_BLOCK)
        idx, valid = _dsa_select(sl(iq), sl(iw), ik, qpos, topk)
        return _sparse_attend(sl(q), _gather_rows(k, idx), _gather_rows(v, idx), valid)

    o = lax.map(block, jnp.arange(T // Q_BLOCK))
    o = jnp.moveaxis(o, 0, 1).reshape(B_, T, ATT_Q_DIM)
    return o, k, v, ik


def _dsa_sample(x, cache_k, cache_v, cache_idx_k, page_table, w_in, kn_g, kn_b):
    B_, T, _ = x.shape
    page = cache_k.shape[1]
    past = page_table.shape[1] * page
    pos = past + jnp.arange(T)
    q, k, v, iq, ik, iw = _dsa_project_jax(x, pos, w_in, kn_g, kn_b)
    ik_all = jnp.concatenate([cache_idx_k[page_table].reshape(B_, past, IDX_DIM), ik], axis=1)
    idx, valid = _dsa_select(iq, iw, ik_all, pos, min(TOPK_MAX, (past + T) // 4))
    past_idx = jnp.minimum(idx, past - 1)
    phys = jnp.take_along_axis(page_table, (past_idx // page).reshape(B_, -1), axis=1).reshape(idx.shape)
    off = past_idx % page
    new_idx = jnp.clip(idx - past, 0, T - 1)
    is_new = (idx >= past)[..., None, None]
    k_sel = jnp.where(is_new, _gather_rows(k, new_idx), cache_k[phys, off])
    v_sel = jnp.where(is_new, _gather_rows(v, new_idx), cache_v[phys, off])
    o = _sparse_attend(q, k_sel, v_sel, valid)
    return o, k, v, ik


def _rwkv7_mixer(x, shift, wkv, mu, w_r, w_k, w_v, w0, w1, w2, a0, a1, a2, g1, g2,
                 k_k, k_a, r_k, gn_g, gn_b):
    B_, T, _ = x.shape
    x_prev = jnp.concatenate([shift[:, None, :], x[:, :-1]], axis=1)
    xm = x[None] + (x_prev - x)[None] * mu[:, None, None, :]
    xr, xw, xk, xv, xa, xg = xm
    r = xr @ w_r
    w_log = -jax.nn.softplus(-(w0 + jnp.tanh(xw @ w1) @ w2)) - 0.5
    k = xk @ w_k
    v = xv @ w_v
    a = jax.nn.sigmoid(a0 + (xa @ a1) @ a2)
    g = jax.nn.sigmoid(xg @ g1) @ g2
    heads = lambda t: t.reshape(B_, T, RW_HEADS, RW_HEAD)
    kk = heads(k * k_k).astype(jnp.float32)
    kk = kk / jnp.maximum(jnp.sqrt(jnp.sum(kk * kk, -1, keepdims=True)), 1e-12)
    k = k * (1.0 + (a - 1.0) * k_a)
    decay = jnp.exp(-jnp.exp(w_log.astype(jnp.float32)))
    r, k, v, a, decay = heads(r), heads(k), heads(v), heads(a), heads(decay)
    seq = tuple(jnp.moveaxis(t.astype(jnp.float32), 1, 0) for t in (r, decay, k, v, kk, kk * a))

    def step(s, inp):
        r_t, d_t, k_t, v_t, kk_t, b_t = inp
        sa = jnp.einsum('bhij,bhj->bhi', s, kk_t)
        s = s * d_t[:, :, None, :] - sa[..., None] * b_t[:, :, None, :] + v_t[..., None] * k_t[:, :, None, :]
        return s, jnp.einsum('bhij,bhj->bhi', s, r_t)

    s_last, y = lax.scan(step, wkv.astype(jnp.float32), seq)
    y = jnp.moveaxis(y, 0, 1)
    mu_y = jnp.mean(y, -1, keepdims=True)
    var_y = jnp.mean(jnp.square(y - mu_y), -1, keepdims=True)
    yn = ((y - mu_y) * lax.rsqrt(var_y + RW_GN_EPS)).reshape(B_, T, D_MODEL).astype(x.dtype) * gn_g + gn_b
    bonus = (jnp.sum(r * k * r_k, -1, keepdims=True) * v).reshape(B_, T, D_MODEL)
    return (yn + bonus) * g, x[:, -1], s_last.astype(wkv.dtype)


def kernel(x_prompt, x_sample, state_ssm_conv, state_ssm, cache_k, cache_v, cache_idx_k, state_rwkv_shift, state_rwkv_wkv, page_table, p_prompt, p_sample, ln_g, ln_b, ffn_w_up, ffn_w_down, ple_w_p, ple_w_g, ple_b_g, gm_w_in, gm_ln_g, gm_ln_b, gm_ws, gm_bs, gm_w_out, ssm_w_in, ssm_conv_w, ssm_conv_b, ssm_dt_bias, ssm_a_log, ssm_d, ssm_norm_g, ssm_w_out, att_w_in, att_kn_g, att_kn_b, att_w_out, rw_mu, rw_w_r, rw_w_k, rw_w_v, rw_w_o, rw_w0, rw_w1, rw_w2, rw_a0, rw_a1, rw_a2, rw_g1, rw_g2, rw_k_k, rw_k_a, rw_r_k, rw_gn_g, rw_gn_b):
    bp, tp, _ = x_prompt.shape
    bs_, ts, _ = x_sample.shape
    bf = lambda w: w.astype(jnp.bfloat16)
    w_up_bf, w_down_bf = bf(ffn_w_up), bf(ffn_w_down)
    ple_wp_bf, ple_wg_bf = bf(ple_w_p), bf(ple_w_g)
    pp3 = p_prompt.reshape(DEPTH, bp * tp, PLE_DIM)
    ps3 = p_sample.reshape(DEPTH, bs_ * ts, PLE_DIM)

    yp = x_prompt.reshape(bp * tp, D_MODEL)
    ys = x_sample.reshape(bs_ * ts, D_MODEL)
    r3p = lambda t: t.reshape(bp, tp, -1)
    r3s = lambda t: t.reshape(bs_, ts, -1)
    f2 = lambda t: t.reshape(-1, t.shape[-1])

    for i in range(DEPTH):
        yp = _ffn_sub(yp, w_up_bf, w_down_bf, i, 0, ln_g[i, 0], ln_b[i, 0])
        ys = _ffn_sub(ys, w_up_bf, w_down_bf, i, 0, ln_g[i, 0], ln_b[i, 0])
        m = i % N_MIXERS
        if m == 0:
            gm_args = (gm_w_in, gm_ln_g, gm_ln_b, gm_ws, gm_bs, gm_w_out, ln_g[i, 1], ln_b[i, 1])
            yp, = _gmlp_block(yp, tp, *gm_args, False)
            ys, gm_v_s = _gmlp_block(ys, ts, *gm_args, True)
            gm_v_s = r3s(gm_v_s)
        elif m == 1:
            ssm_args = (ssm_w_in, ssm_conv_w, ssm_conv_b, ssm_dt_bias, ssm_a_log, ssm_d, ssm_norm_g)
            hp, conv_p, ssm_p = _mamba2_mixer_pallas(
                r3p(yp), jnp.zeros((bp, SSM_CONV - 1, SSM_CONV_DIM), yp.dtype),
                jnp.zeros((bp, SSM_HEADS, SSM_HEAD_DIM, SSM_STATE), yp.dtype), *ssm_args)
            hs, conv_s, ssm_s = _mamba2_mixer_pallas(r3s(ys), state_ssm_conv, state_ssm, *ssm_args)
            w_out = bf(ssm_w_out)
        elif m == 2:
            hp, k_p, v_p, ik_p = _dsa_prompt_pallas(r3p(yp), att_w_in, att_kn_g, att_kn_b)
            hs, k_s, v_s, ik_s = _dsa_sample_pallas(r3s(ys), cache_k, cache_v, cache_idx_k, page_table,
                                                    att_w_in, att_kn_g, att_kn_b)
            w_out = bf(att_w_out)
        else:
            rw_args = (rw_mu, rw_w_r, rw_w_k, rw_w_v, rw_w0, rw_w1, rw_w2, rw_a0, rw_a1, rw_a2,
                       rw_g1, rw_g2, rw_k_k, rw_k_a, rw_r_k, rw_gn_g, rw_gn_b)
            hp, gate_p, sh_p, wkv_p = _rwkv7_mixer_pallas(
                r3p(yp), jnp.zeros((bp, D_MODEL), yp.dtype),
                jnp.zeros((bp, RW_HEADS, RW_HEAD, RW_HEAD), yp.dtype), *rw_args)
            hs, gate_s, sh_s, wkv_s = _rwkv7_mixer_pallas(r3s(ys), state_rwkv_shift, state_rwkv_wkv, *rw_args)
            w_out = bf(rw_w_o)
        if m == 3:
            yp = _proj_gate_post_norm(yp, hp, gate_p, w_out, ln_g[i, 1], ln_b[i, 1])
            ys = _proj_gate_post_norm(ys, hs, gate_s, w_out, ln_g[i, 1], ln_b[i, 1])
        elif m != 0:
            yp = _proj_post_norm(yp, f2(hp), w_out, ln_g[i, 1], ln_b[i, 1])
            ys = _proj_post_norm(ys, f2(hs), w_out, ln_g[i, 1], ln_b[i, 1])
        yp = _ffn_sub(yp, w_up_bf, w_down_bf, i, 1, ln_g[i, 2], ln_b[i, 2])
        ys = _ffn_sub(ys, w_up_bf, w_down_bf, i, 1, ln_g[i, 2], ln_b[i, 2])
        yp = _ple_add(yp, pp3, ple_wp_bf, ple_wg_bf, ple_b_g[i], i)
        ys = _ple_add(ys, ps3, ple_wp_bf, ple_wg_bf, ple_b_g[i], i)

    return (r3p(yp), r3s(ys), gm_v_s, conv_p, ssm_p, conv_s, ssm_s, k_p, v_p, ik_p, k_s, v_s, ik_s,
            sh_p, wkv_p, sh_s, wkv_s)
```

```python
import functools
import math

import jax
import jax.numpy as jnp
from jax import lax
from jax.experimental import pallas as pl
from jax.experimental.pallas import tpu as pltpu

D_MODEL = 1024
DEPTH = 4
N_MIXERS = 4
PLE_DIM = 256
D_FF = 2816
ALPHA = (2 * DEPTH) ** 0.25
LN_EPS = 1e-5

CHUNK = 128
GM_WIDTH = 2 * D_MODEL
GM_GROUPS = 8
GM_GROUP_DIM = GM_WIDTH // GM_GROUPS

SSM_D_INNER = 2 * D_MODEL
SSM_HEAD_DIM = 64
SSM_HEADS = SSM_D_INNER // SSM_HEAD_DIM
SSM_GROUPS = 4
SSM_HPG = SSM_HEADS // SSM_GROUPS
SSM_STATE = 128
SSM_CONV = 4
SSM_CONV_DIM = SSM_D_INNER + 2 * SSM_GROUPS * SSM_STATE
SSM_CHUNK = 128

ATT_HEADS = 16
ATT_KV_HEADS = 4
ATT_HEAD_DIM = D_MODEL // ATT_HEADS
ROPE_DIM = ATT_HEAD_DIM // 4
ROPE_THETA = 500000.0
IDX_HEADS = 8
IDX_DIM = 64
IDX_ROPE_DIM = IDX_DIM // 4
TOPK_MAX = 256
Q_BLOCK = 128
ATT_Q_DIM = ATT_HEADS * ATT_HEAD_DIM
ATT_KV_DIM = ATT_KV_HEADS * ATT_HEAD_DIM
ATT_IN_SPLITS = (ATT_Q_DIM, ATT_Q_DIM + ATT_KV_DIM, ATT_Q_DIM + 2 * ATT_KV_DIM,
                 ATT_Q_DIM + 2 * ATT_KV_DIM + IDX_HEADS * IDX_DIM,
                 ATT_Q_DIM + 2 * ATT_KV_DIM + IDX_HEADS * IDX_DIM + IDX_DIM)

RW_HEAD = 64
RW_HEADS = D_MODEL // RW_HEAD
RW_GN_EPS = 64e-5

V7X_VMEM_LIMIT_BYTES = 52 * 1024 * 1024
FF_TILE = D_FF // 2
ROW_TILE = 512


def _row_tile(m):
    return ROW_TILE if m % ROW_TILE == 0 else m


def _ln_rows(y, g, b):
    mu = jnp.mean(y, axis=-1, keepdims=True)
    yc = y - mu
    var = jnp.mean(yc * yc, axis=-1, keepdims=True)
    return yc * lax.rsqrt(var + LN_EPS) * g + b


FFN_ROW_TILE = 1024
FFN_ROW_PARTS = 2


def _ffn_kernel(x_ref, wu_ref, wd_ref, g_ref, b_ref, o_ref, *, parts):
    rows = x_ref.shape[0] // parts
    for p in range(parts):
        x = x_ref[p * rows:(p + 1) * rows, :]
        xb = x.astype(MXU_DTYPE)
        acc = None
        for f in range(D_FF // FF_TILE):
            cols = slice(f * FF_TILE, (f + 1) * FF_TILE)
            gate = jnp.dot(xb, wu_ref[:, cols], preferred_element_type=jnp.float32)
            lin = jnp.dot(xb, wu_ref[:, D_FF + f * FF_TILE:D_FF + (f + 1) * FF_TILE],
                          preferred_element_type=jnp.float32)
            h = (gate * jax.nn.sigmoid(gate) * lin).astype(MXU_DTYPE)
            part = jnp.dot(h, wd_ref[cols, :], preferred_element_type=jnp.float32)
            acc = part if acc is None else acc + part
        o_ref[p * rows:(p + 1) * rows, :] = _ln_rows(ALPHA * x + 0.5 * acc, g_ref[...], b_ref[...])


def _ffn_sub(x2d, w_up, w_down, layer, half, g, b):
    m = x2d.shape[0]
    tm = FFN_ROW_TILE if m % FFN_ROW_TILE == 0 else m
    parts = FFN_ROW_PARTS if tm == FFN_ROW_TILE else 1
    resident = dict(pipeline_mode=pl.Buffered(1))
    return pl.pallas_call(
        functools.partial(_ffn_kernel, parts=parts),
        grid=(m // tm,),
        in_specs=[
            pl.BlockSpec((tm, D_MODEL), lambda i: (i, 0)),
            pl.BlockSpec((None, None, D_MODEL, 2 * D_FF), lambda i: (layer, half, 0, 0), **resident),
            pl.BlockSpec((None, None, D_FF, D_MODEL), lambda i: (layer, half, 0, 0), **resident),
            pl.BlockSpec((1, D_MODEL), lambda i: (0, 0)),
            pl.BlockSpec((1, D_MODEL), lambda i: (0, 0)),
        ],
        out_specs=pl.BlockSpec((tm, D_MODEL), lambda i: (i, 0)),
        out_shape=jax.ShapeDtypeStruct((m, D_MODEL), jnp.float32),
        compiler_params=pltpu.CompilerParams(
            dimension_semantics=("parallel",),
            vmem_limit_bytes=V7X_VMEM_LIMIT_BYTES),
        name="ffn_sub",
    )(x2d, w_up, w_down, g.reshape(1, D_MODEL), b.reshape(1, D_MODEL))


def _ple_kernel(x_ref, p_ref, wp_ref, wg_ref, bg_ref, o_ref):
    x = x_ref[...]
    gate = jax.nn.sigmoid(
        jnp.dot(x.astype(jnp.bfloat16), wg_ref[...], preferred_element_type=jnp.float32) + bg_ref[...])
    emb = jnp.dot(p_ref[...].astype(jnp.bfloat16), wp_ref[...], preferred_element_type=jnp.float32)
    o_ref[...] = x + gate * emb


def _ple_add(x2d, p3d, w_p, w_g, b_g, layer):
    m = x2d.shape[0]
    tm = _row_tile(m)
    return pl.pallas_call(
        _ple_kernel,
        grid=(m // tm,),
        in_specs=[
            pl.BlockSpec((tm, D_MODEL), lambda i: (i, 0)),
            pl.BlockSpec((None, tm, PLE_DIM), lambda i: (layer, i, 0)),
            pl.BlockSpec((None, PLE_DIM, D_MODEL), lambda i: (layer, 0, 0)),
            pl.BlockSpec((None, D_MODEL, D_MODEL), lambda i: (layer, 0, 0)),
            pl.BlockSpec((1, D_MODEL), lambda i: (0, 0)),
        ],
        out_specs=pl.BlockSpec((tm, D_MODEL), lambda i: (i, 0)),
        out_shape=jax.ShapeDtypeStruct((m, D_MODEL), jnp.float32),
        compiler_params=pltpu.CompilerParams(
            dimension_semantics=("parallel",),
            vmem_limit_bytes=V7X_VMEM_LIMIT_BYTES),
        name="ple_add",
    )(x2d, p3d, w_p, w_g, b_g.reshape(1, D_MODEL))


def _proj_ln_kernel(x_ref, h_ref, w_ref, g_ref, b_ref, o_ref):
    y = ALPHA * x_ref[...] + jnp.dot(h_ref[...].astype(jnp.bfloat16), w_ref[...],
                                     preferred_element_type=jnp.float32)
    o_ref[...] = _ln_rows(y, g_ref[...], b_ref[...])


def _proj_post_norm(x2d, h2d, w_out, g, b):
    m = x2d.shape[0]
    k = h2d.shape[1]
    tm = _row_tile(m)
    return pl.pallas_call(
        _proj_ln_kernel,
        grid=(m // tm,),
        in_specs=[
            pl.BlockSpec((tm, D_MODEL), lambda i: (i, 0)),
            pl.BlockSpec((tm, k), lambda i: (i, 0)),
            pl.BlockSpec((k, D_MODEL), lambda i: (0, 0)),
            pl.BlockSpec((1, D_MODEL), lambda i: (0, 0)),
            pl.BlockSpec((1, D_MODEL), lambda i: (0, 0)),
        ],
        out_specs=pl.BlockSpec((tm, D_MODEL), lambda i: (i, 0)),
        out_shape=jax.ShapeDtypeStruct((m, D_MODEL), jnp.float32),
        compiler_params=pltpu.CompilerParams(
            dimension_semantics=("parallel",),
            vmem_limit_bytes=V7X_VMEM_LIMIT_BYTES),
        name="proj_post_norm",
    )(x2d, h2d, w_out, g.reshape(1, D_MODEL), b.reshape(1, D_MODEL))


MXU_DTYPE = jnp.bfloat16
KEY_GROUP = 512
INT32_MIN = -2 ** 31
MASK_NEG = -1e30


def _rope_lane_tables(pos, rot_dim, head_dim):
    half = rot_dim // 2
    inv = ROPE_THETA ** (-jnp.arange(half, dtype=jnp.float32) / half)
    ang = pos.astype(jnp.float32)[:, None] * inv[None, :]
    cos, sin = jnp.cos(ang), jnp.sin(ang)
    n = pos.shape[0]
    rest = head_dim - rot_dim
    c = jnp.concatenate([cos, cos, jnp.ones((n, rest), jnp.float32)], axis=1)
    s1 = jnp.concatenate([-sin, jnp.zeros((n, half + rest), jnp.float32)], axis=1)
    s2 = jnp.concatenate([jnp.zeros((n, half), jnp.float32), sin, jnp.zeros((n, rest), jnp.float32)], axis=1)
    reps = 128 // head_dim
    tile = lambda t: jnp.tile(t, (1, reps))
    return tile(c), tile(s1), tile(s2), cos.T, sin.T


def _rope_lanes(t, c, s1, s2, half):
    n = t.shape[1]
    reps = n // 128
    tl = lambda a: jnp.concatenate([a] * reps, axis=1)
    return t * tl(c) + pltpu.roll(t, n - half, 1) * tl(s1) + pltpu.roll(t, half, 1) * tl(s2)


def _rope_rows(t, cT, sT, head_dim, half):
    pieces = []
    for h in range(t.shape[0] // head_dim):
        x1 = t[h * head_dim:h * head_dim + half]
        x2 = t[h * head_dim + half:h * head_dim + 2 * half]
        pieces += [x1 * cT - x2 * sT, x2 * cT + x1 * sT, t[h * head_dim + 2 * half:(h + 1) * head_dim]]
    return jnp.concatenate(pieces, axis=0)


def _dsa_proj_kernel(x_ref, wq_ref, wiq_ref, wv_ref, wvx_ref, wiw_ref, wkT_ref, wikT_ref,
                     c_ref, s1_ref, s2_ref, cT_ref, sT_ref, kng_ref, knb_ref, one_ref,
                     q_ref, iq_ref, v_ref, vx_ref, iw_ref, kT_ref, kTb_ref, ikT_ref, ikTb_ref):
    xb = x_ref[...].astype(MXU_DTYPE)
    c, s1, s2 = c_ref[...], s1_ref[...], s2_ref[...]
    cT, sT = cT_ref[...], sT_ref[...]
    dot = lambda a, b: jnp.dot(a, b, preferred_element_type=jnp.float32)
    dot_t = lambda w, a: lax.dot_general(w, a, (((1,), (1,)), ((), ())), preferred_element_type=jnp.float32)

    q = _rope_lanes(dot(xb, wq_ref[...]), c, s1, s2, ROPE_DIM // 2)
    q_ref[...] = (q * (ATT_HEAD_DIM ** -0.5)).astype(q_ref.dtype)
    iq = _rope_lanes(dot(xb, wiq_ref[...]), c, s1, s2, IDX_ROPE_DIM // 2)
    iq_ref[...] = iq.astype(iq_ref.dtype)
    v_ref[...] = dot(xb, wv_ref[...])
    vx_ref[...] = (dot(xb, wvx_ref[...]) + one_ref[...]).astype(vx_ref.dtype)
    iw_ref[...] = dot(xb, wiw_ref[...]) * (IDX_HEADS ** -0.5 * IDX_DIM ** -0.5)

    kT = _rope_rows(dot_t(wkT_ref[...], xb), cT, sT, ATT_HEAD_DIM, ROPE_DIM // 2)
    kT_ref[0, 0] = kT
    kTb_ref[0, 0] = kT.astype(kTb_ref.dtype)
    ikT = dot_t(wikT_ref[...], xb)
    mu = jnp.mean(ikT, axis=0, keepdims=True)
    ikc = ikT - mu
    var = jnp.mean(ikc * ikc, axis=0, keepdims=True)
    ikT = ikc * lax.rsqrt(var + LN_EPS) * kng_ref[...] + knb_ref[...]
    ikT = _rope_rows(ikT, cT, sT, IDX_DIM, IDX_ROPE_DIM // 2)
    ikT_ref[0, 0] = ikT
    ikTb_ref[0, 0] = ikT.astype(ikTb_ref.dtype)


def _dsa_project(x3d, pos, w_in, kn_g, kn_b):
    b_, t_, _ = x3d.shape
    tk = KEY_GROUP if t_ % KEY_GROUP == 0 else t_
    ng = t_ // tk
    m = b_ * t_
    w_q, w_k, w_v, w_iq, w_ik, w_iw = jnp.split(w_in, list(ATT_IN_SPLITS), axis=1)
    cast = lambda w: w.astype(MXU_DTYPE)
    w_vx = jnp.pad(w_v.reshape(D_MODEL, ATT_KV_HEADS, ATT_HEAD_DIM),
                   ((0, 0), (0, 0), (0, 128 - ATT_HEAD_DIM))).reshape(D_MODEL, ATT_KV_HEADS * 128)
    one_col = jnp.tile((jnp.arange(128) == ATT_HEAD_DIM).astype(jnp.float32), ATT_KV_HEADS)[None, :]
    w_iw_pad = jnp.pad(w_iw, ((0, 0), (0, 128 - IDX_HEADS)))
    c, s1, s2, cT, sT = _rope_lane_tables(pos, ROPE_DIM, ATT_HEAD_DIM)
    full = lambda shape: pl.BlockSpec(shape, lambda b, i: (0,) * len(shape))
    rows = lambda n: pl.BlockSpec((tk, n), lambda b, i: (b * ng + i, 0))
    ptab = lambda n: pl.BlockSpec((tk, n), lambda b, i: (i, 0))
    grp = lambda n: pl.BlockSpec((1, 1, n, tk), lambda b, i: (b, i, 0, 0))
    sds = jax.ShapeDtypeStruct
    return pl.pallas_call(
        _dsa_proj_kernel,
        grid=(b_, ng),
        in_specs=[rows(D_MODEL), full((D_MODEL, ATT_Q_DIM)), full((D_MODEL, IDX_HEADS * IDX_DIM)),
                  full((D_MODEL, ATT_KV_DIM)), full((D_MODEL, ATT_KV_HEADS * 128)), full((D_MODEL, 128)),
                  full((ATT_KV_DIM, D_MODEL)), full((IDX_DIM, D_MODEL)),
                  ptab(128), ptab(128), ptab(128),
                  pl.BlockSpec((ROPE_DIM // 2, tk), lambda b, i: (0, i)),
                  pl.BlockSpec((ROPE_DIM // 2, tk), lambda b, i: (0, i)),
                  full((IDX_DIM, 1)), full((IDX_DIM, 1)), full((1, ATT_KV_HEADS * 128))],
        out_specs=[rows(ATT_Q_DIM), rows(IDX_HEADS * IDX_DIM), rows(ATT_KV_DIM), rows(ATT_KV_HEADS * 128),
                   rows(128), grp(ATT_KV_DIM), grp(ATT_KV_DIM), grp(IDX_DIM), grp(IDX_DIM)],
        out_shape=[sds((m, ATT_Q_DIM), MXU_DTYPE), sds((m, IDX_HEADS * IDX_DIM), MXU_DTYPE),
                   sds((m, ATT_KV_DIM), jnp.float32), sds((m, ATT_KV_HEADS * 128), MXU_DTYPE),
                   sds((m, 128), jnp.float32),
                   sds((b_, ng, ATT_KV_DIM, tk), jnp.float32), sds((b_, ng, ATT_KV_DIM, tk), MXU_DTYPE),
                   sds((b_, ng, IDX_DIM, tk), jnp.float32), sds((b_, ng, IDX_DIM, tk), MXU_DTYPE)],
        compiler_params=pltpu.CompilerParams(
            dimension_semantics=("parallel", "parallel"),
            vmem_limit_bytes=V7X_VMEM_LIMIT_BYTES),
        name="dsa_project",
    )(x3d.reshape(m, D_MODEL), cast(w_q), cast(w_iq), cast(w_v), cast(w_vx), cast(w_iw_pad),
      cast(w_k.T), cast(w_ik.T), c, s1, s2, cT, sT, kn_g.reshape(IDX_DIM, 1), kn_b.reshape(IDX_DIM, 1), one_col)


def _untranspose_groups(tg):
    b_, g_, r_, tk = tg.shape
    return jnp.transpose(tg, (0, 1, 3, 2)).reshape(b_, g_ * tk, r_)


def _dsa_attend_kernel(iq_ref, iw_ref, ikT_ref, q_ref, kT_ref, vx_ref, o_ref, key_ref, m_ref, acc_ref, *,
                       topk, col_bits):
    j = pl.program_id(1)
    tq = iq_ref.shape[0]
    tk = key_ref.shape[2]
    n_groups = (j * tq + tq + tk - 1) // tk
    row = j * tq + lax.broadcasted_iota(jnp.int32, (tq, tk), 0)
    col0 = lax.broadcasted_iota(jnp.int32, (tq, tk), 1)
    dot = lambda a, b: jnp.dot(a, b, preferred_element_type=jnp.float32)

    def score_body(g, carry):
        ikT = ikT_ref[0, g]
        sc = jnp.zeros((tq, tk), jnp.float32)
        for h in range(IDX_HEADS):
            s = dot(iq_ref[:, h * IDX_DIM:(h + 1) * IDX_DIM], ikT)
            sc = sc + iw_ref[:, h:h + 1] * jnp.maximum(s, 0.0)
        bits = pltpu.bitcast(sc, jnp.int32)
        key = jnp.where(bits >= 0, bits, bits ^ jnp.int32(0x7FFFFFFF))
        key_ref[g] = jnp.where(col0 + g * tk <= row, key, jnp.int32(INT32_MIN))
        return carry

    lax.fori_loop(0, n_groups, score_body, 0)

    def bit_body(i, thr):
        cand = thr ^ lax.shift_left(jnp.int32(1), jnp.int32(31) - i)

        def count_body(g, cnt):
            hit = jnp.where(key_ref[g] >= cand, 1.0, 0.0)
            for l in range(tk // 128):
                cnt = cnt + hit[:, l * 128:(l + 1) * 128]
            return cnt

        cnt = lax.fori_loop(0, n_groups, count_body, jnp.zeros((tq, 128), jnp.float32))
        total = jnp.sum(cnt, axis=1, keepdims=True)
        return jnp.where(total >= float(topk), cand, thr)

    thr = lax.fori_loop(0, 32, bit_body, jnp.full((tq, 1), INT32_MIN, jnp.int32))

    def lane_fold(hit, cnt):
        for l in range(tk // 128):
            cnt = cnt + hit[:, l * 128:(l + 1) * 128]
        return cnt

    def above_body(g, cnt):
        return lane_fold(jnp.where(key_ref[g] > thr, 1.0, 0.0), cnt)

    n_above = jnp.sum(lax.fori_loop(0, n_groups, above_body, jnp.zeros((tq, 128), jnp.float32)),
                      axis=1, keepdims=True)
    need = float(topk) - n_above

    def col_body(i, last):
        cand = last | lax.shift_left(jnp.int32(1), jnp.int32(col_bits - 1) - i)

        def tie_body(g, cnt):
            hit = jnp.where((key_ref[g] == thr) & (col0 + g * tk < cand), 1.0, 0.0)
            return lane_fold(hit, cnt)

        ties = jnp.sum(lax.fori_loop(0, n_groups, tie_body, jnp.zeros((tq, 128), jnp.float32)),
                       axis=1, keepdims=True)
        return jnp.where(ties < need, cand, last)

    last_tie = lax.fori_loop(0, col_bits, col_body, jnp.zeros((tq, 1), jnp.int32))

    m_ref[...] = jnp.full(m_ref.shape, MASK_NEG, jnp.float32)
    acc_ref[...] = jnp.zeros(acc_ref.shape, jnp.float32)
    gsz = ATT_HEADS // ATT_KV_HEADS

    def attend_body(g, carry):
        key = key_ref[g]
        col = col0 + g * tk
        keep = (key > thr) | ((key == thr) & (col <= last_tie))
        bias = jnp.where(keep & (col <= row), 0.0, MASK_NEG)
        start = pl.multiple_of(g * tk, tk)
        for h in range(ATT_HEADS):
            kv = h // gsz
            s = dot(q_ref[:, h * ATT_HEAD_DIM:(h + 1) * ATT_HEAD_DIM],
                    kT_ref[0, g, kv * ATT_HEAD_DIM:(kv + 1) * ATT_HEAD_DIM, :]) + bias
            m_old = m_ref[h]
            m_new = jnp.maximum(m_old, jnp.max(s, axis=1, keepdims=True))
            p = jnp.exp(s - m_new).astype(vx_ref.dtype)
            pv = dot(p, vx_ref[0, pl.ds(start, tk), kv * 128:(kv + 1) * 128])
            acc_ref[h] = jnp.exp(m_old - m_new) * acc_ref[h] + pv
            m_ref[h] = m_new
        return carry

    lax.fori_loop(0, n_groups, attend_body, 0)

    for h in range(ATT_HEADS):
        a = acc_ref[h]
        o_ref[:, h * ATT_HEAD_DIM:(h + 1) * ATT_HEAD_DIM] = (
            a[:, :ATT_HEAD_DIM] / a[:, ATT_HEAD_DIM:ATT_HEAD_DIM + 1]).astype(o_ref.dtype)


def _dsa_attend(b_, t_, q, iq, iw, ikTb, kTb, vx):
    ng, tk = kTb.shape[1], kTb.shape[3]
    tq = Q_BLOCK
    nq = t_ // tq
    rows = lambda n: pl.BlockSpec((tq, n), lambda b, j: (b * nq + j, 0))
    return pl.pallas_call(
        functools.partial(_dsa_attend_kernel, topk=min(TOPK_MAX, t_ // 4), col_bits=max(1, (t_ - 1).bit_length())),
        grid=(b_, nq),
        in_specs=[rows(IDX_HEADS * IDX_DIM), rows(128),
                  pl.BlockSpec((1, ng, IDX_DIM, tk), lambda b, j: (b, 0, 0, 0)),
                  rows(ATT_Q_DIM),
                  pl.BlockSpec((1, ng, ATT_KV_DIM, tk), lambda b, j: (b, 0, 0, 0)),
                  pl.BlockSpec((1, t_, ATT_KV_HEADS * 128), lambda b, j: (b, 0, 0))],
        out_specs=rows(ATT_Q_DIM),
        out_shape=jax.ShapeDtypeStruct((b_ * t_, ATT_Q_DIM), MXU_DTYPE),
        scratch_shapes=[pltpu.VMEM((ng, tq, tk), jnp.int32),
                        pltpu.VMEM((ATT_HEADS, tq, 1), jnp.float32),
                        pltpu.VMEM((ATT_HEADS, tq, 128), jnp.float32)],
        compiler_params=pltpu.CompilerParams(
            dimension_semantics=("parallel", "arbitrary"),
            vmem_limit_bytes=V7X_VMEM_LIMIT_BYTES),
        name="dsa_attend",
    )(iq, iw, ikTb, q, kTb, vx.reshape(b_, t_, ATT_KV_HEADS * 128))


def _dsa_proj_q_lanes_kernel(x_ref, wqT_ref, wiqT_ref, wiwT_ref, wk_ref, wv_ref, wvxT_ref, wik_ref,
                             c_ref, s1_ref, s2_ref, cT_ref, sT_ref, kng_ref, knb_ref, onerow_ref,
                             qT_ref, iqT_ref, iwT_ref, k_ref, khd_ref, v_ref, vxT_ref, ik_ref, ikb_ref):
    xb = x_ref[...].astype(MXU_DTYPE)
    tm = xb.shape[0]
    c, s1, s2 = c_ref[...], s1_ref[...], s2_ref[...]
    cT, sT = cT_ref[...], sT_ref[...]
    dot = lambda a, b: jnp.dot(a, b, preferred_element_type=jnp.float32)
    dot_t = lambda w, a: lax.dot_general(w, a, (((1,), (1,)), ((), ())), preferred_element_type=jnp.float32)

    qT = _rope_rows(dot_t(wqT_ref[...], xb), cT, sT, ATT_HEAD_DIM, ROPE_DIM // 2) * (ATT_HEAD_DIM ** -0.5)
    iqT = _rope_rows(dot_t(wiqT_ref[...], xb), cT, sT, IDX_DIM, IDX_ROPE_DIM // 2)
    iwT = dot_t(wiwT_ref[...], xb) * (IDX_HEADS ** -0.5 * IDX_DIM ** -0.5)
    for t in range(tm // Q_BLOCK):
        lanes = slice(t * Q_BLOCK, (t + 1) * Q_BLOCK)
        qT_ref[0, t] = qT[:, lanes].astype(qT_ref.dtype)
        iqT_ref[0, t] = iqT[:, lanes].astype(iqT_ref.dtype)
        iwT_ref[0, t] = iwT[:, lanes]

    k = _rope_lanes(dot(xb, wk_ref[...]), c, s1, s2, ROPE_DIM // 2)
    k_ref[...] = k
    for g in range(ATT_KV_HEADS):
        khd_ref[g] = k[:, g * ATT_HEAD_DIM:(g + 1) * ATT_HEAD_DIM].astype(khd_ref.dtype)
    v_ref[...] = dot(xb, wv_ref[...])
    vxT_ref[0, 0] = (dot_t(wvxT_ref[...], xb) + onerow_ref[...]).astype(vxT_ref.dtype)

    ik = dot(xb, wik_ref[...])
    real = lax.broadcasted_iota(jnp.int32, ik.shape, 1) < IDX_DIM
    mu = jnp.sum(ik, axis=-1, keepdims=True) * (1.0 / IDX_DIM)
    ikc = jnp.where(real, ik - mu, 0.0)
    var = jnp.sum(ikc * ikc, axis=-1, keepdims=True) * (1.0 / IDX_DIM)
    ikn = _rope_lanes(ikc * lax.rsqrt(var + LN_EPS) * kng_ref[...] + knb_ref[...], c, s1, s2, IDX_ROPE_DIM // 2)
    ik_ref[...] = ikn[:, :IDX_DIM]
    ikb_ref[...] = ikn[:, :IDX_DIM].astype(ikb_ref.dtype)


def _dsa_project_q_lanes(x3d, pos, w_in, kn_g, kn_b):
    b_, t_, _ = x3d.shape
    tk = KEY_GROUP
    ng = t_ // tk
    nq = tk // Q_BLOCK
    m = b_ * t_
    w_q, w_k, w_v, w_iq, w_ik, w_iw = jnp.split(w_in, list(ATT_IN_SPLITS), axis=1)
    cast = lambda w: w.astype(MXU_DTYPE)
    w_vxT = jnp.pad(w_v.T.reshape(ATT_KV_HEADS, ATT_HEAD_DIM, D_MODEL),
                    ((0, 0), (0, 128 - ATT_HEAD_DIM), (0, 0))).reshape(ATT_KV_HEADS * 128, D_MODEL)
    one_row = jnp.tile((jnp.arange(128) == ATT_HEAD_DIM).astype(jnp.float32), ATT_KV_HEADS)[:, None]
    pad_lanes = lambda a: jnp.pad(a, ((0, 0), (0, 128 - a.shape[1])))
    c, s1, s2, cT, sT = _rope_lane_tables(pos, ROPE_DIM, ATT_HEAD_DIM)
    full = lambda shape: pl.BlockSpec(shape, lambda b, i: (0,) * len(shape))
    rows = lambda n: pl.BlockSpec((tk, n), lambda b, i: (b * ng + i, 0))
    ptab = lambda n: pl.BlockSpec((tk, n), lambda b, i: (i, 0))
    qtile = lambda n: pl.BlockSpec((1, nq, n, Q_BLOCK), lambda b, i: (b, i, 0, 0))
    sds = jax.ShapeDtypeStruct
    return pl.pallas_call(
        _dsa_proj_q_lanes_kernel,
        grid=(b_, ng),
        in_specs=[rows(D_MODEL), full((ATT_Q_DIM, D_MODEL)), full((IDX_HEADS * IDX_DIM, D_MODEL)),
                  full((IDX_HEADS, D_MODEL)), full((D_MODEL, ATT_KV_DIM)), full((D_MODEL, ATT_KV_DIM)),
                  full((ATT_KV_HEADS * 128, D_MODEL)), full((D_MODEL, 128)),
                  ptab(128), ptab(128), ptab(128),
                  pl.BlockSpec((ROPE_DIM // 2, tk), lambda b, i: (0, i)),
                  pl.BlockSpec((ROPE_DIM // 2, tk), lambda b, i: (0, i)),
                  full((1, 128)), full((1, 128)), full((ATT_KV_HEADS * 128, 1))],
        out_specs=[qtile(ATT_Q_DIM), qtile(IDX_HEADS * IDX_DIM), qtile(IDX_HEADS),
                   rows(ATT_KV_DIM), pl.BlockSpec((ATT_KV_HEADS, tk, ATT_HEAD_DIM), lambda b, i: (0, b * ng + i, 0)),
                   rows(ATT_KV_DIM), pl.BlockSpec((1, 1, ATT_KV_HEADS * 128, tk), lambda b, i: (b, i, 0, 0)),
                   rows(IDX_DIM), rows(IDX_DIM)],
        out_shape=[sds((b_, t_ // Q_BLOCK, ATT_Q_DIM, Q_BLOCK), MXU_DTYPE),
                   sds((b_, t_ // Q_BLOCK, IDX_HEADS * IDX_DIM, Q_BLOCK), MXU_DTYPE),
                   sds((b_, t_ // Q_BLOCK, IDX_HEADS, Q_BLOCK), jnp.float32),
                   sds((m, ATT_KV_DIM), jnp.float32), sds((ATT_KV_HEADS, m, ATT_HEAD_DIM), MXU_DTYPE),
                   sds((m, ATT_KV_DIM), jnp.float32), sds((b_, ng, ATT_KV_HEADS * 128, tk), MXU_DTYPE),
                   sds((m, IDX_DIM), jnp.float32), sds((m, IDX_DIM), MXU_DTYPE)],
        compiler_params=pltpu.CompilerParams(
            dimension_semantics=("parallel", "parallel"),
            vmem_limit_bytes=V7X_VMEM_LIMIT_BYTES),
        name="dsa_project_q_lanes",
    )(x3d.reshape(m, D_MODEL), cast(w_q.T), cast(w_iq.T), cast(w_iw.T), cast(w_k), cast(w_v), cast(w_vxT),
      cast(pad_lanes(w_ik)), c, s1, s2, cT, sT, pad_lanes(kn_g.reshape(1, IDX_DIM)),
      pad_lanes(kn_b.reshape(1, IDX_DIM)), one_row)


def _tree_sum(parts):
    while len(parts) > 1:
        parts = [parts[i] + parts[i + 1] for i in range(0, len(parts) - 1, 2)] + (
            [parts[-1]] if len(parts) % 2 else [])
    return parts[0]


def _dsa_attend_q_lanes_kernel(iqT_ref, iwT_ref, ik_ref, qT_ref, k_ref, vxT_ref, o_ref,
                               key_ref, bias_ref, m_ref, acc_ref, *, topk, col_bits):
    j = pl.program_id(1)
    tk, tq = key_ref.shape[1], key_ref.shape[2]
    n_groups = (j * tq + tq + tk - 1) // tk
    qpos = j * tq + lax.broadcasted_iota(jnp.int32, (tk, tq), 1)
    kpos0 = lax.broadcasted_iota(jnp.int32, (tk, tq), 0)
    dot = lambda a, b: jnp.dot(a, b, preferred_element_type=jnp.float32)

    def score_body(g, carry):
        start = pl.multiple_of(g * tk, tk)
        w_iq = jnp.concatenate([iqT_ref[0, 0, h * IDX_DIM:(h + 1) * IDX_DIM, :] for h in range(IDX_HEADS)], axis=1)
        s_all = dot(ik_ref[0, pl.ds(start, tk), :], w_iq)
        sc = _tree_sum([iwT_ref[0, 0, h:h + 1, :] * jnp.maximum(s_all[:, h * tq:(h + 1) * tq], 0.0)
                        for h in range(IDX_HEADS)])
        key_ref[g] = jnp.where(kpos0 + g * tk <= qpos, _sortable_key(sc), jnp.int32(INT32_MIN))
        return carry

    lax.fori_loop(0, n_groups, score_body, 0)

    def count_keys(pred):
        def body(g, part):
            hit = jnp.where(pred(key_ref[g], kpos0 + g * tk), 1.0, 0.0)
            return part + _tree_sum([hit[r * SUBLANES:(r + 1) * SUBLANES] for r in range(tk // SUBLANES)])
        part = lax.fori_loop(0, n_groups, body, jnp.zeros((SUBLANES, tq), jnp.float32))
        return jnp.sum(part, axis=0, keepdims=True)

    def bit_body(i, thr):
        cand = thr ^ lax.shift_left(jnp.int32(1), jnp.int32(31) - i)
        return jnp.where(count_keys(lambda k, kp: k >= cand) >= float(topk), cand, thr)

    thr = lax.fori_loop(0, 32, bit_body, jnp.full((1, tq), INT32_MIN, jnp.int32))

    need = float(topk) - count_keys(lambda k, kp: k > thr)

    def pos_body(i, last):
        cand = last | lax.shift_left(jnp.int32(1), jnp.int32(col_bits - 1) - i)
        return jnp.where(count_keys(lambda k, kp: (k == thr) & (kp < cand)) < need, cand, last)

    n_tied = count_keys(lambda k, kp: k == thr)
    excess = jnp.max(jnp.where(n_tied > need, 1.0, 0.0), axis=1, keepdims=True)
    last_tie = lax.cond(excess[0, 0] > 0.0,
                        lambda: lax.fori_loop(0, col_bits, pos_body, jnp.zeros((1, tq), jnp.int32)),
                        lambda: jnp.full((1, tq), 2 ** col_bits - 1, jnp.int32))

    m_ref[...] = jnp.full(m_ref.shape, MASK_NEG, jnp.float32)
    acc_ref[...] = jnp.zeros(acc_ref.shape, jnp.float32)
    gsz = ATT_HEADS // ATT_KV_HEADS

    def attend_body(g, carry):
        start = pl.multiple_of(g * tk, tk)
        key = key_ref[g]
        kpos = kpos0 + g * tk
        keep = (key > thr) | ((key == thr) & (kpos <= last_tie))
        bias_ref[...] = jnp.where(keep & (kpos <= qpos), 0.0, MASK_NEG)
        logits = []
        for kv in range(ATT_KV_HEADS):
            w_q = jnp.concatenate([qT_ref[0, 0, (kv * gsz + i) * ATT_HEAD_DIM:(kv * gsz + i + 1) * ATT_HEAD_DIM, :]
                                   for i in range(gsz)], axis=1)
            logits.append(dot(k_ref[kv, pl.ds(start, tk), :], w_q))
        for kv in range(ATT_KV_HEADS):
            s = logits[kv] + jnp.concatenate([bias_ref[...]] * gsz, axis=1)
            m_old = m_ref[kv]
            m_new = jnp.maximum(m_old, jnp.max(s, axis=0, keepdims=True))
            p = jnp.exp(s - m_new).astype(vxT_ref.dtype)
            pv = dot(vxT_ref[0, g, kv * 128:(kv + 1) * 128, :], p)
            acc_ref[kv] = jnp.exp(m_old - m_new) * acc_ref[kv] + pv
            m_ref[kv] = m_new
        return carry

    lax.fori_loop(0, n_groups, attend_body, 0)

    for h in range(ATT_HEADS):
        a = acc_ref[h // gsz, :, (h % gsz) * tq:(h % gsz + 1) * tq]
        o = (a / a[ATT_HEAD_DIM:ATT_HEAD_DIM + 1, :]).T
        o_ref[:, h * ATT_HEAD_DIM:(h + 1) * ATT_HEAD_DIM] = o[:, :ATT_HEAD_DIM].astype(o_ref.dtype)


def _dsa_attend_q_lanes(b_, t_, qT, iqT, iwT, ikb, khd, vxT):
    ng, tk = vxT.shape[1], vxT.shape[3]
    tq = Q_BLOCK
    nq = t_ // tq
    qtile = lambda n: pl.BlockSpec((1, 1, n, tq), lambda b, j: (b, j, 0, 0))
    return pl.pallas_call(
        functools.partial(_dsa_attend_q_lanes_kernel, topk=min(TOPK_MAX, t_ // 4),
                          col_bits=max(1, (t_ - 1).bit_length())),
        grid=(b_, nq),
        in_specs=[qtile(IDX_HEADS * IDX_DIM), qtile(IDX_HEADS),
                  pl.BlockSpec((1, t_, IDX_DIM), lambda b, j: (b, 0, 0)),
                  qtile(ATT_Q_DIM),
                  pl.BlockSpec((ATT_KV_HEADS, t_, ATT_HEAD_DIM), lambda b, j: (0, b, 0)),
                  pl.BlockSpec((1, ng, ATT_KV_HEADS * 128, tk), lambda b, j: (b, 0, 0, 0))],
        out_specs=pl.BlockSpec((tq, ATT_Q_DIM), lambda b, j: (b * nq + j, 0)),
        out_shape=jax.ShapeDtypeStruct((b_ * t_, ATT_Q_DIM), MXU_DTYPE),
        scratch_shapes=[pltpu.VMEM((ng, tk, tq), jnp.int32),
                        pltpu.VMEM((tk, tq), jnp.float32),
                        pltpu.VMEM((ATT_KV_HEADS, 1, tq * (ATT_HEADS // ATT_KV_HEADS)), jnp.float32),
                        pltpu.VMEM((ATT_KV_HEADS, 128, tq * (ATT_HEADS // ATT_KV_HEADS)), jnp.float32)],
        compiler_params=pltpu.CompilerParams(
            dimension_semantics=("parallel", "arbitrary"),
            vmem_limit_bytes=V7X_VMEM_LIMIT_BYTES),
        name="dsa_attend_q_lanes",
    )(iqT, iwT, ikb.reshape(b_, t_, IDX_DIM), qT, khd, vxT)


def _dsa_prompt_pallas(x3d, w_in, kn_g, kn_b):
    b_, t_, _ = x3d.shape
    qT, iqT, iwT, k, khd, v, vxT, ik, ikb = _dsa_project_q_lanes(x3d, jnp.arange(t_), w_in, kn_g, kn_b)
    o = _dsa_attend_q_lanes(b_, t_, qT, iqT, iwT, ikb, khd, vxT)
    kv4 = lambda u: u.reshape(b_, t_, ATT_KV_HEADS, ATT_HEAD_DIM)
    return o, kv4(k), kv4(v), ik.reshape(b_, t_, IDX_DIM)


RW_ROW_TILE = 256


RW_PAIRS = RW_HEADS // 2
RW_PAIR_LANES = 2 * RW_HEAD


def _rwkv_project_rows(x, xp, mu_ref, wr_ref, wk_ref, wv_ref, w1_ref, w2_ref, a1_ref, a2_ref,
                       g1_ref, g2_ref, w0_ref, a0_ref):
    dx = xp - x
    mix = lambda c: (x + dx * mu_ref[c:c + 1, :]).astype(MXU_DTYPE)
    dot = lambda a, b: jnp.dot(a.astype(MXU_DTYPE), b, preferred_element_type=jnp.float32)
    r = dot(mix(0), wr_ref[...])
    lora_w = dot(jnp.tanh(dot(mix(1), w1_ref[...])), w2_ref[...])
    w_log = -jax.nn.softplus(-(w0_ref[...] + lora_w)) - 0.5
    d = jnp.exp(-jnp.exp(w_log))
    k = dot(mix(2), wk_ref[...])
    v = dot(mix(3), wv_ref[...])
    a = jax.nn.sigmoid(a0_ref[...] + dot(dot(mix(4), a1_ref[...]), a2_ref[...]))
    g = dot(jax.nn.sigmoid(dot(mix(5), g1_ref[...])), g2_ref[...])
    return r, d, k, v, a, g


def _rwkv_proj_step_kernel(x_ref, xp_ref, *refs):
    vals = _rwkv_project_rows(x_ref[...], xp_ref[...], *refs[:12])
    for ref, val in zip(refs[12:], vals):
        ref[...] = val


def _rwkv_proj_seq_kernel(x_ref, halo_ref, shift_ref, *refs):
    i = pl.program_id(1)
    x = x_ref[...]
    prev = jnp.where(i == 0, shift_ref[0], halo_ref[...])[SUBLANES - 1:SUBLANES, :]
    first = lax.broadcasted_iota(jnp.int32, (x.shape[0], 1), 0) == 0
    xp = jnp.where(first, prev, pltpu.roll(x, 1, 0))
    vals = _rwkv_project_rows(x, xp, *refs[:12])
    for ref, val in zip(refs[12:17], vals[:5]):
        for hp in range(RW_PAIRS):
            ref[:, hp, :] = val[:, hp * RW_PAIR_LANES:(hp + 1) * RW_PAIR_LANES]
    refs[17][...] = vals[5]


def _rwkv_consts(mu, w_r, w_k, w_v, w0, w1, w2, a0, a1, a2, g1, g2):
    cast = lambda w: w.astype(MXU_DTYPE)
    return [mu, cast(w_r), cast(w_k), cast(w_v), cast(w1), cast(w2), cast(a1), cast(a2), cast(g1), cast(g2),
            w0.reshape(1, D_MODEL), a0.reshape(1, D_MODEL)]


def _rwkv_project_step(x2d, xprev2d, *params):
    m = x2d.shape[0]
    consts = _rwkv_consts(*params)
    full = lambda a: pl.BlockSpec(a.shape, lambda i: (0,) * a.ndim)
    rows = pl.BlockSpec((m, D_MODEL), lambda i: (0, 0))
    return pl.pallas_call(
        _rwkv_proj_step_kernel,
        grid=(1,),
        in_specs=[rows, rows] + [full(a) for a in consts],
        out_specs=[rows] * 6,
        out_shape=[jax.ShapeDtypeStruct((m, D_MODEL), jnp.float32)] * 6,
        compiler_params=pltpu.CompilerParams(
            dimension_semantics=("arbitrary",),
            vmem_limit_bytes=V7X_VMEM_LIMIT_BYTES),
        name="rwkv_project_step",
    )(x2d, xprev2d, *consts)


def _rwkv_project_seq(x3d, shift, *params):
    b_, t_, _ = x3d.shape
    m = b_ * t_
    tm = RW_ROW_TILE
    nt = t_ // tm
    consts = _rwkv_consts(*params)
    full = lambda a: pl.BlockSpec(a.shape, lambda b, i: (0,) * a.ndim)
    rows = pl.BlockSpec((tm, D_MODEL), lambda b, i: (b * nt + i, 0))
    halo = pl.BlockSpec((SUBLANES, D_MODEL), lambda b, i: (jnp.maximum((b * nt + i) * (tm // SUBLANES) - 1, 0), 0))
    shift8 = jnp.pad(shift[:, None, :], ((0, 0), (SUBLANES - 1, 0), (0, 0)))
    tmaj = pl.BlockSpec((tm, RW_PAIRS, RW_PAIR_LANES), lambda b, i: (i, b, 0))
    x2d = x3d.reshape(m, D_MODEL)
    return pl.pallas_call(
        _rwkv_proj_seq_kernel,
        grid=(b_, nt),
        in_specs=[rows, halo, pl.BlockSpec((1, SUBLANES, D_MODEL), lambda b, i: (b, 0, 0))]
        + [full(a) for a in consts],
        out_specs=[tmaj] * 5 + [rows],
        out_shape=[jax.ShapeDtypeStruct((t_, b_ * RW_PAIRS, RW_PAIR_LANES), jnp.float32)] * 5
        + [jax.ShapeDtypeStruct((m, D_MODEL), jnp.float32)],
        compiler_params=pltpu.CompilerParams(
            dimension_semantics=("parallel", "parallel"),
            vmem_limit_bytes=V7X_VMEM_LIMIT_BYTES),
        name="rwkv_project_seq",
    )(x2d, x2d, shift8, *consts)


RW_LANES = 128
RW_TIME_CHUNK = 32


def _rwkv_scan_kernel(r_ref, d_ref, k_ref, v_ref, a_ref, s0_ref, kk_ref, ka_ref, rk_ref, gg_ref, gb_ref,
                      z_ref, s_out_ref, s_ref, vec_ref):
    c = pl.program_id(1)
    n = RW_HEAD
    tc = r_ref.shape[0]
    low_half = lax.broadcasted_iota(jnp.int32, (n, RW_LANES), 1) < n

    @pl.when(c == 0)
    def _():
        s_ref[...] = s0_ref[...]

    def swap_layout(x):
        xt = jnp.concatenate([x, x], axis=0).T
        return jnp.where(low_half, xt[:n], xt[n:])

    def prepare(t, slot):
        r, k, a = swap_layout(r_ref[t]), swap_layout(k_ref[t]), swap_layout(a_ref[t])
        kkr = k * kk_ref[...]
        nrm = jnp.sqrt(jnp.sum(kkr * kkr, axis=0, keepdims=True))
        kk = kkr / jnp.maximum(nrm, 1e-12)
        vec_ref[slot, 0] = kk
        vec_ref[slot, 1] = swap_layout(d_ref[t])
        vec_ref[slot, 2] = kk * a
        vec_ref[slot, 3] = k * (1.0 + (a - 1.0) * ka_ref[...])
        vec_ref[slot, 4] = r
        vec_ref[slot, 5] = swap_layout(v_ref[t])

    def step(t, slot):
        row = lambda q, j: vec_ref[slot, q, j:j + 1, :]
        v = vec_ref[slot, 5]
        lanes = 4
        sa_parts = [s_ref[j] * row(0, j) for j in range(lanes)]
        for j in range(lanes, n):
            sa_parts[j % lanes] = sa_parts[j % lanes] + s_ref[j] * row(0, j)
        sa = _tree_sum(sa_parts)
        y_parts = []
        for j in range(n):
            sn = s_ref[j] * row(1, j) - sa * row(2, j) + v * row(3, j)
            s_ref[j] = sn
            if j < lanes:
                y_parts.append(sn * row(4, j))
            else:
                y_parts[j % lanes] = y_parts[j % lanes] + sn * row(4, j)
        y = _tree_sum(y_parts)
        mu = jnp.mean(y, axis=0, keepdims=True)
        yc = y - mu
        var = jnp.mean(yc * yc, axis=0, keepdims=True)
        bonus = jnp.sum(vec_ref[slot, 4] * vec_ref[slot, 3] * rk_ref[...], axis=0, keepdims=True)
        z_ref[t] = swap_layout(yc * lax.rsqrt(var + RW_GN_EPS) * gg_ref[...] + gb_ref[...] + bonus * v)

    prepare(0, 0)
    if tc == 1:
        step(0, 0)
    else:
        def two_steps(i, carry):
            t = 2 * i
            prepare(t + 1, 1)
            step(t, 0)
            prepare(jnp.minimum(t + 2, tc - 1), 0)
            step(t + 1, 1)
            return carry

        lax.fori_loop(0, tc // 2, two_steps, 0)

    @pl.when(c == pl.num_programs(1) - 1)
    def _():
        s_out_ref[...] = s_ref[...]


RW_SEQ_PER_TILE = RW_LANES // RW_HEADS


def _rwkv_lane_heads():
    half = jnp.arange(2)[:, None, None]
    pair = jnp.arange(RW_PAIRS)[None, None, :]
    return jnp.broadcast_to(2 * pair + half, (2, RW_SEQ_PER_TILE, RW_PAIRS)).reshape(RW_LANES)


def _rwkv_scan(r, d, k, v, a, s0, k_k, k_a, r_k, gn_g, gn_b):
    t_, rows, _ = r.shape
    n = RW_HEAD
    tile_rows = RW_SEQ_PER_TILE * RW_PAIRS
    tc = RW_TIME_CHUNK if t_ % RW_TIME_CHUNK == 0 else t_
    table = lambda p: p.reshape(RW_HEADS, n)[_rwkv_lane_heads()].T
    seq = pl.BlockSpec((tc, tile_rows, RW_PAIR_LANES), lambda l, c: (c, l, 0))
    state = pl.BlockSpec((n, n, RW_LANES), lambda l, c: (0, 0, l))
    tab = pl.BlockSpec((n, RW_LANES), lambda l, c: (0, 0))
    return pl.pallas_call(
        _rwkv_scan_kernel,
        grid=(rows // tile_rows, t_ // tc),
        in_specs=[seq] * 5 + [state] + [tab] * 5,
        out_specs=[seq, state],
        out_shape=[jax.ShapeDtypeStruct(r.shape, jnp.float32),
                   jax.ShapeDtypeStruct(s0.shape, jnp.float32)],
        scratch_shapes=[pltpu.VMEM((n, n, RW_LANES), jnp.float32),
                        pltpu.VMEM((2, 6, n, RW_LANES), jnp.float32)],
        compiler_params=pltpu.CompilerParams(
            dimension_semantics=("parallel", "arbitrary"),
            vmem_limit_bytes=V7X_VMEM_LIMIT_BYTES),
        name="rwkv_scan",
    )(r, d, k, v, a, s0, table(k_k), table(k_a), table(r_k), table(gn_g), table(gn_b))


def _rwkv_state_to_lanes(wkv):
    b_ = wkv.shape[0]
    w = wkv.astype(jnp.float32).reshape(b_ // RW_SEQ_PER_TILE, RW_SEQ_PER_TILE, RW_PAIRS, 2, RW_HEAD, RW_HEAD)
    return jnp.transpose(w, (5, 4, 0, 3, 1, 2)).reshape(RW_HEAD, RW_HEAD, b_ * RW_HEADS)


def _rwkv_state_from_lanes(s, b_):
    w = s.reshape(RW_HEAD, RW_HEAD, b_ // RW_SEQ_PER_TILE, 2, RW_SEQ_PER_TILE, RW_PAIRS)
    return jnp.transpose(w, (2, 4, 5, 3, 1, 0)).reshape(b_, RW_HEADS, RW_HEAD, RW_HEAD)


def _rwkv7_mixer_pallas(x3d, shift, wkv, mu, w_r, w_k, w_v, w0, w1, w2, a0, a1, a2, g1, g2,
                        k_k, k_a, r_k, gn_g, gn_b):
    b_, t_, _ = x3d.shape
    params = (mu, w_r, w_k, w_v, w0, w1, w2, a0, a1, a2, g1, g2)
    if t_ == 1:
        *seqs, g = _rwkv_project_step(x3d.reshape(b_, D_MODEL), shift, *params)
        seqs = [u.reshape(1, b_ * RW_PAIRS, RW_PAIR_LANES) for u in seqs]
    else:
        *seqs, g = _rwkv_project_seq(x3d, shift, *params)
    z, s = _rwkv_scan(*seqs, _rwkv_state_to_lanes(wkv), k_k, k_a, r_k, gn_g, gn_b)
    return z, g, x3d[:, -1], _rwkv_state_from_lanes(s, b_).astype(wkv.dtype)


def _proj_gate_ln_kernel(x_ref, h_ref, gate_ref, w_ref, g_ref, b_ref, o_ref):
    h = (h_ref[...] * gate_ref[...]).astype(MXU_DTYPE)
    y = ALPHA * x_ref[...] + jnp.dot(h, w_ref[...], preferred_element_type=jnp.float32)
    o_ref[...] = _ln_rows(y, g_ref[...], b_ref[...])


def _proj_gate_ln_tmajor_kernel(x_ref, h_ref, gate_ref, w_ref, g_ref, b_ref, o_ref):
    h = jnp.concatenate([h_ref[:, hp, :] for hp in range(RW_PAIRS)], axis=1)
    h = (h * gate_ref[...]).astype(MXU_DTYPE)
    y = ALPHA * x_ref[...] + jnp.dot(h, w_ref[...], preferred_element_type=jnp.float32)
    o_ref[...] = _ln_rows(y, g_ref[...], b_ref[...])


def _proj_gate_post_norm(x2d, h_tmajor, gate2d, w_out, g, b):
    m = x2d.shape[0]
    t_ = h_tmajor.shape[0]
    b_ = m // t_
    weight = pl.BlockSpec((D_MODEL, D_MODEL), lambda *_: (0, 0))
    vec = pl.BlockSpec((1, D_MODEL), lambda *_: (0, 0))
    args = (gate2d, w_out, g.reshape(1, D_MODEL), b.reshape(1, D_MODEL))
    out_shape = jax.ShapeDtypeStruct((m, D_MODEL), jnp.float32)
    if t_ == 1:
        rows = pl.BlockSpec((m, D_MODEL), lambda i: (0, 0))
        return pl.pallas_call(
            _proj_gate_ln_kernel, grid=(1,), in_specs=[rows, rows, rows, weight, vec, vec], out_specs=rows,
            out_shape=out_shape, name="proj_gate_post_norm_step",
        )(x2d, h_tmajor.reshape(m, D_MODEL), *args)
    tm = ROW_TILE
    nt = t_ // tm
    rows = pl.BlockSpec((tm, D_MODEL), lambda bb, i: (bb * nt + i, 0))
    return pl.pallas_call(
        _proj_gate_ln_tmajor_kernel,
        grid=(b_, nt),
        in_specs=[rows, pl.BlockSpec((tm, RW_PAIRS, RW_PAIR_LANES), lambda bb, i: (i, bb, 0)), rows,
                  weight, vec, vec],
        out_specs=rows,
        out_shape=out_shape,
        compiler_params=pltpu.CompilerParams(
            dimension_semantics=("parallel", "parallel"),
            vmem_limit_bytes=V7X_VMEM_LIMIT_BYTES),
        name="proj_gate_post_norm",
    )(x2d, h_tmajor, *args)


GM_ROW_TILE = 256


def _gmlp_kernel(x_ref, win_ref, lng_ref, lnb_ref, mixw_ref, mixb_ref, wout_ref, g_ref, b_ref, *out_refs,
                 chunk_len, emit_v):
    x = x_ref[...]
    h = jax.nn.gelu(jnp.dot(x.astype(MXU_DTYPE), win_ref[...], preferred_element_type=jnp.float32))
    u = h[:, :GM_WIDTH]
    v = _ln_rows(h[:, GM_WIDTH:], lng_ref[...], lnb_ref[...])
    if emit_v:
        out_refs[1][...] = v
    if chunk_len == 1:
        gated = u * (v * mixw_ref[...] + mixb_ref[...])
    else:
        tm = x.shape[0]
        causal = (lax.broadcasted_iota(jnp.int32, (chunk_len, chunk_len), 0)
                  >= lax.broadcasted_iota(jnp.int32, (chunk_len, chunk_len), 1))
        vb = v.astype(MXU_DTYPE)
        cols = []
        for g in range(GM_GROUPS):
            w = jnp.where(causal, mixw_ref[g], 0.0).astype(MXU_DTYPE)
            bias = mixb_ref[:, g:g + 1]
            lanes = slice(g * GM_GROUP_DIM, (g + 1) * GM_GROUP_DIM)
            rows = [jnp.dot(w, vb[c * chunk_len:(c + 1) * chunk_len, lanes],
                            preferred_element_type=jnp.float32) + bias
                    for c in range(tm // chunk_len)]
            cols.append(jnp.concatenate(rows, axis=0))
        gated = u * jnp.concatenate(cols, axis=1)
    y = ALPHA * x + jnp.dot(gated.astype(MXU_DTYPE), wout_ref[...], preferred_element_type=jnp.float32)
    out_refs[0][...] = _ln_rows(y, g_ref[...], b_ref[...])


def _gmlp_block(x2d, seq_len, w_in, ln_g, ln_b, ws, bs, w_out, g, b, emit_v):
    m = x2d.shape[0]
    chunk_len = min(seq_len, CHUNK)
    if chunk_len == 1:
        tm = m
        mixw = jnp.repeat(ws[:, 0, 0], GM_GROUP_DIM)[None, :]
        mixb = jnp.repeat(bs[:, 0], GM_GROUP_DIM)[None, :]
    else:
        tm = GM_ROW_TILE
        mixw = ws[:, :chunk_len, :chunk_len]
        mixb = bs[:, :chunk_len].T
    full = lambda a: pl.BlockSpec(a.shape, lambda i: (0,) * a.ndim)
    rows = lambda n: pl.BlockSpec((tm, n), lambda i: (i, 0))
    consts = [w_in.astype(MXU_DTYPE), ln_g.reshape(1, GM_WIDTH), ln_b.reshape(1, GM_WIDTH), mixw, mixb,
              w_out.astype(MXU_DTYPE), g.reshape(1, D_MODEL), b.reshape(1, D_MODEL)]
    out_specs = [rows(D_MODEL)] + ([rows(GM_WIDTH)] if emit_v else [])
    out_shape = [jax.ShapeDtypeStruct((m, D_MODEL), jnp.float32)] + (
        [jax.ShapeDtypeStruct((m, GM_WIDTH), jnp.float32)] if emit_v else [])
    return pl.pallas_call(
        functools.partial(_gmlp_kernel, chunk_len=chunk_len, emit_v=emit_v),
        grid=(m // tm,),
        in_specs=[rows(D_MODEL)] + [full(a) for a in consts],
        out_specs=out_specs,
        out_shape=out_shape,
        compiler_params=pltpu.CompilerParams(
            dimension_semantics=("parallel",),
            vmem_limit_bytes=V7X_VMEM_LIMIT_BYTES),
        name="gmlp_block",
    )(x2d, *consts)


SSM_ROW_TILE = 256
SSM_BC_DIM = SSM_GROUPS * SSM_STATE
SSM_DT_LANES = 128
SUBLANES = 8


def _ssm_activate(xb, xbc, taps, wz_ref, wdt_ref, cw_ref, cb_ref, dtb_ref, z_ref, xs_ref, bm_ref, cm_ref, dt_ref):
    conv = cb_ref[...] + xbc * cw_ref[SSM_CONV - 1:SSM_CONV, :]
    for j in range(SSM_CONV - 1):
        conv = conv + taps[j] * cw_ref[j:j + 1, :]
    act = conv * jax.nn.sigmoid(conv)
    xs_ref[...] = act[:, :SSM_D_INNER]
    bm_ref[...] = act[:, SSM_D_INNER:SSM_D_INNER + SSM_BC_DIM].astype(bm_ref.dtype)
    cm_ref[...] = act[:, SSM_D_INNER + SSM_BC_DIM:].astype(cm_ref.dtype)
    z_ref[...] = jnp.dot(xb, wz_ref[...], preferred_element_type=jnp.float32)
    dt_ref[...] = jax.nn.softplus(jnp.dot(xb, wdt_ref[...], preferred_element_type=jnp.float32) + dtb_ref[...])


def _ssm_proj_seq_kernel(x_ref, halo_ref, cs_ref, wx_ref, wz_ref, wdt_ref, cw_ref, cb_ref, dtb_ref,
                         z_ref, xs_ref, bm_ref, cm_ref, dt_ref, tail_ref):
    i = pl.program_id(1)
    xb = x_ref[...].astype(MXU_DTYPE)
    xbc = jnp.dot(xb, wx_ref[...], preferred_element_type=jnp.float32)
    tm = xbc.shape[0]
    prev = jnp.dot(halo_ref[...].astype(MXU_DTYPE), wx_ref[...], preferred_element_type=jnp.float32)
    prev = jnp.where(i == 0, cs_ref[0], prev)
    row = lax.broadcasted_iota(jnp.int32, (tm, 1), 0)
    pad = jnp.zeros((tm - SUBLANES, xbc.shape[1]), jnp.float32)
    taps = []
    for j in range(SSM_CONV - 1):
        back = SSM_CONV - 1 - j
        head = jnp.concatenate([pltpu.roll(prev, back, 0), pad], axis=0)
        taps.append(jnp.where(row < back, head, pltpu.roll(xbc, back, 0)))
    _ssm_activate(xb, xbc, taps, wz_ref, wdt_ref, cw_ref, cb_ref, dtb_ref, z_ref, xs_ref, bm_ref, cm_ref, dt_ref)
    tail_ref[0] = xbc[tm - SUBLANES:, :]


def _ssm_proj_step_kernel(x_ref, st_ref, wx_ref, wz_ref, wdt_ref, cw_ref, cb_ref, dtb_ref,
                          z_ref, xs_ref, bm_ref, cm_ref, dt_ref, st_out_ref):
    xb = x_ref[...].astype(MXU_DTYPE)
    xbc = jnp.dot(xb, wx_ref[...], preferred_element_type=jnp.float32)
    taps = [st_ref[j] for j in range(SSM_CONV - 1)]
    _ssm_activate(xb, xbc, taps, wz_ref, wdt_ref, cw_ref, cb_ref, dtb_ref, z_ref, xs_ref, bm_ref, cm_ref, dt_ref)
    for j in range(SSM_CONV - 2):
        st_out_ref[j] = st_ref[j + 1]
    st_out_ref[SSM_CONV - 2] = xbc


def _ssm_project(x3d, conv_state, w_in, conv_w, conv_b, dt_bias):
    b_, t_, _ = x3d.shape
    m = b_ * t_
    w_z, w_x, w_dt = jnp.split(w_in, [SSM_D_INNER, SSM_D_INNER + SSM_CONV_DIM], axis=1)
    cast = lambda w: w.astype(MXU_DTYPE)
    consts = [cast(w_x), cast(w_z), cast(jnp.pad(w_dt, ((0, 0), (0, SSM_DT_LANES - SSM_HEADS)))),
              conv_w, conv_b.reshape(1, SSM_CONV_DIM),
              jnp.pad(dt_bias, (0, SSM_DT_LANES - SSM_HEADS)).reshape(1, SSM_DT_LANES)]
    sds = jax.ShapeDtypeStruct
    outs = [sds((m, SSM_D_INNER), jnp.float32), sds((m, SSM_D_INNER), jnp.float32),
            sds((m, SSM_BC_DIM), MXU_DTYPE), sds((m, SSM_BC_DIM), MXU_DTYPE), sds((m, SSM_DT_LANES), jnp.float32)]
    widths = [SSM_D_INNER, SSM_D_INNER, SSM_BC_DIM, SSM_BC_DIM, SSM_DT_LANES]
    params = dict(vmem_limit_bytes=V7X_VMEM_LIMIT_BYTES)
    x2d = x3d.reshape(m, D_MODEL)
    if t_ == 1:
        full = lambda a: pl.BlockSpec(a.shape, lambda i: (0,) * a.ndim)
        st = jnp.transpose(conv_state, (1, 0, 2))
        res = pl.pallas_call(
            _ssm_proj_step_kernel,
            grid=(1,),
            in_specs=[full(x2d), full(st)] + [full(a) for a in consts],
            out_specs=[pl.BlockSpec((m, w), lambda i: (0, 0)) for w in widths] + [full(st)],
            out_shape=outs + [sds(st.shape, jnp.float32)],
            compiler_params=pltpu.CompilerParams(dimension_semantics=("arbitrary",), **params),
            name="ssm_project_step",
        )(x2d, st, *consts)
        return list(res[:5]) + [jnp.transpose(res[5], (1, 0, 2))]
    tm = SSM_ROW_TILE
    nt = t_ // tm
    full = lambda a: pl.BlockSpec(a.shape, lambda b, i: (0,) * a.ndim)
    rows = lambda w: pl.BlockSpec((tm, w), lambda b, i: (b * nt + i, 0))
    halo = pl.BlockSpec((SUBLANES, D_MODEL), lambda b, i: (jnp.maximum((b * nt + i) * (tm // SUBLANES) - 1, 0), 0))
    cs8 = jnp.pad(conv_state, ((0, 0), (SUBLANES - (SSM_CONV - 1), 0), (0, 0)))
    tail = pl.BlockSpec((1, SUBLANES, SSM_CONV_DIM), lambda b, i: (b, 0, 0))
    res = pl.pallas_call(
        _ssm_proj_seq_kernel,
        grid=(b_, nt),
        in_specs=[rows(D_MODEL), halo, tail] + [full(a) for a in consts],
        out_specs=[rows(w) for w in widths] + [tail],
        out_shape=outs + [sds((b_, SUBLANES, SSM_CONV_DIM), jnp.float32)],
        compiler_params=pltpu.CompilerParams(dimension_semantics=("parallel", "arbitrary"), **params),
        name="ssm_project_seq",
    )(x2d, x2d, cs8, *consts)
    return list(res[:5]) + [res[5][:, SUBLANES - (SSM_CONV - 1):, :]]


def _ssm_gate_norm(y, xs, z, dskip, normg):
    yg = (y + xs * dskip) * (z * jax.nn.sigmoid(z))
    gw = SSM_D_INNER // SSM_GROUPS
    outs = []
    for g in range(SSM_GROUPS):
        part = yg[:, g * gw:(g + 1) * gw]
        ms = jnp.mean(part * part, axis=-1, keepdims=True)
        outs.append(part * lax.rsqrt(ms + LN_EPS))
    return jnp.concatenate(outs, axis=1) * normg


def _ssm_chunk_kernel(xs_ref, bm_ref, cm_ref, dt_ref, z_ref, aneg_ref, dskip_ref, normg_ref,
                      yg_ref, h_out_ref, h_ref, yT_ref, xe_ref):
    c = pl.program_id(1)
    l = xs_ref.shape[0]
    hd = SSM_HEAD_DIM

    @pl.when(c == 0)
    def _():
        h_ref[...] = jnp.zeros_like(h_ref)

    dot = lambda u, w: jnp.dot(u, w, preferred_element_type=jnp.float32)
    dt = dt_ref[...]
    a = dt * aneg_ref[...]
    r_i = lax.broadcasted_iota(jnp.int32, (l, l), 0)
    c_i = lax.broadcasted_iota(jnp.int32, (l, l), 1)
    tril = jnp.where(r_i >= c_i, 1.0, 0.0)
    hi = lax.Precision.HIGHEST
    acum = jnp.dot(tril, a, precision=hi, preferred_element_type=jnp.float32)
    acum_t = jnp.dot(a.T, tril.T, precision=hi, preferred_element_type=jnp.float32)
    dt_t = dt.T
    to_end_t = jnp.exp(acum_t[:, l - 1:l] - acum_t)
    from_start_t = jnp.exp(acum_t)
    chunk_decay = jnp.exp(acum[l - 1:l, :])
    upper = r_i <= c_i
    xs = xs_ref[...]
    for g in range(SSM_GROUPS):
        bm = bm_ref[:, g * SSM_STATE:(g + 1) * SSM_STATE]
        cm_t = cm_ref[:, g * SSM_STATE:(g + 1) * SSM_STATE].astype(jnp.float32).T.astype(MXU_DTYPE)
        cb_t = dot(bm, cm_t)
        h_in = h_ref[g * SSM_HPG:(g + 1) * SSM_HPG].reshape(SSM_HPG * hd, SSM_STATE)
        y_off = dot(h_in.astype(MXU_DTYPE), cm_t)
        for e in range(SSM_HPG):
            h = g * SSM_HPG + e
            if h % 2 == 0:
                xs_pair_t = xs[:, h * hd:(h + 2) * hd].T
            xdt_t = xs_pair_t[(h % 2) * hd:(h % 2 + 1) * hd] * dt_t[h:h + 1, :]
            seg = jnp.exp(jnp.where(upper, acum_t[h:h + 1, :] - acum[:, h:h + 1], -jnp.inf))
            y_diag = dot(xdt_t.astype(MXU_DTYPE), (cb_t * seg).astype(MXU_DTYPE))
            yT_ref[h * hd:(h + 1) * hd, :] = y_diag + y_off[e * hd:(e + 1) * hd] * from_start_t[h:h + 1, :]
            xe_ref[e * hd:(e + 1) * hd, :] = (xdt_t * to_end_t[h:h + 1, :]).astype(xe_ref.dtype)
        states = dot(xe_ref[...], bm)
        for e in range(SSM_HPG):
            h = g * SSM_HPG + e
            h_ref[h] = h_ref[h] * chunk_decay[:, h:h + 1] + states[e * hd:(e + 1) * hd]
    y = jnp.concatenate([yT_ref[i * l:(i + 1) * l, :].T for i in range(SSM_D_INNER // l)], axis=1)
    yg_ref[...] = _ssm_gate_norm(y, xs, z_ref[...], dskip_ref[...], normg_ref[...]).astype(yg_ref.dtype)

    @pl.when(c == pl.num_programs(1) - 1)
    def _():
        h_out_ref[0] = h_ref[...]


def _ssm_head_lanes(p):
    return jnp.pad(p.astype(jnp.float32), (0, SSM_DT_LANES - SSM_HEADS)).reshape(1, SSM_DT_LANES)


def _ssm_chunk_scan(b_, t_, xs, bm, cm, dt, z, a_log, d_skip, norm_g):
    l = SSM_CHUNK
    nc = t_ // l
    rows = lambda w: pl.BlockSpec((l, w), lambda b, c: (b * nc + c, 0))
    vec = lambda w: pl.BlockSpec((1, w), lambda b, c: (0, 0))
    aneg = _ssm_head_lanes(-jnp.exp(a_log.astype(jnp.float32)))
    dskip = jnp.repeat(d_skip, SSM_HEAD_DIM).reshape(1, SSM_D_INNER)
    yg, h_new = pl.pallas_call(
        _ssm_chunk_kernel,
        grid=(b_, nc),
        in_specs=[rows(SSM_D_INNER), rows(SSM_BC_DIM), rows(SSM_BC_DIM), rows(SSM_DT_LANES), rows(SSM_D_INNER),
                  vec(SSM_DT_LANES), vec(SSM_D_INNER), vec(SSM_D_INNER)],
        out_specs=[rows(SSM_D_INNER),
                   pl.BlockSpec((1, SSM_HEADS, SSM_HEAD_DIM, SSM_STATE), lambda b, c: (b, 0, 0, 0))],
        out_shape=[jax.ShapeDtypeStruct((b_ * t_, SSM_D_INNER), MXU_DTYPE),
                   jax.ShapeDtypeStruct((b_, SSM_HEADS, SSM_HEAD_DIM, SSM_STATE), jnp.float32)],
        scratch_shapes=[pltpu.VMEM((SSM_HEADS, SSM_HEAD_DIM, SSM_STATE), jnp.float32),
                        pltpu.VMEM((SSM_D_INNER, l), jnp.float32),
                        pltpu.VMEM((SSM_HPG * SSM_HEAD_DIM, l), MXU_DTYPE)],
        compiler_params=pltpu.CompilerParams(
            dimension_semantics=("parallel", "arbitrary"),
            vmem_limit_bytes=V7X_VMEM_LIMIT_BYTES),
        name="ssm_chunk_scan",
    )(xs, bm, cm, dt, z, aneg, dskip, norm_g.reshape(1, SSM_D_INNER))
    return yg, h_new


def _ssm_step_kernel(h0_ref, xs_ref, dt_ref, an_ref, bm_ref, cm_ref, y_ref, h_ref):
    h0 = h0_ref[0]
    dt = dt_ref[0]
    decay = jnp.exp(dt * an_ref[...])
    xdt = xs_ref[0] * dt
    bm = bm_ref[0].astype(jnp.float32)
    cm = cm_ref[0].astype(jnp.float32)
    h_ref[0] = h0 * decay + xdt * bm
    cb = jnp.sum(cm * bm, axis=-1, keepdims=True)
    y_ref[0] = cb * xdt + jnp.sum(cm * h0, axis=-1, keepdims=True) * decay


def _ssm_step(state, xs, bm, cm, dt, a_log):
    b_ = state.shape[0]
    per_head = lambda u: jnp.repeat(u.reshape(b_, SSM_GROUPS, 1, SSM_STATE), SSM_HPG, axis=1)
    xs4 = xs.reshape(b_, SSM_HEADS, SSM_HEAD_DIM, 1)
    dt4 = dt[:, :SSM_HEADS].reshape(b_, SSM_HEADS, 1, 1)
    an = (-jnp.exp(a_log.astype(jnp.float32))).reshape(SSM_HEADS, 1, 1)
    blk = lambda a: pl.BlockSpec((1,) + a.shape[1:], lambda b: (b, 0, 0, 0))
    args = [state.astype(jnp.float32), xs4, dt4, an, per_head(bm), per_head(cm)]
    y4, h_new = pl.pallas_call(
        _ssm_step_kernel,
        grid=(b_,),
        in_specs=[blk(args[0]), blk(xs4), blk(dt4), pl.BlockSpec(an.shape, lambda b: (0, 0, 0)),
                  blk(args[4]), blk(args[5])],
        out_specs=[blk(xs4), blk(args[0])],
        out_shape=[jax.ShapeDtypeStruct(xs4.shape, jnp.float32), jax.ShapeDtypeStruct(state.shape, jnp.float32)],
        compiler_params=pltpu.CompilerParams(
            dimension_semantics=("parallel",),
            vmem_limit_bytes=V7X_VMEM_LIMIT_BYTES),
        name="ssm_step",
    )(*args)
    return y4.reshape(b_, SSM_D_INNER), h_new


def _ssm_gate_norm_kernel(y_ref, xs_ref, z_ref, dskip_ref, normg_ref, o_ref):
    o_ref[...] = _ssm_gate_norm(y_ref[...], xs_ref[...], z_ref[...], dskip_ref[...], normg_ref[...]).astype(o_ref.dtype)


def _ssm_gate_norm_rows(y, xs, z, d_skip, norm_g):
    full = lambda a: pl.BlockSpec(a.shape, lambda i: (0,) * a.ndim)
    args = [y, xs, z, jnp.repeat(d_skip, SSM_HEAD_DIM).reshape(1, SSM_D_INNER), norm_g.reshape(1, SSM_D_INNER)]
    return pl.pallas_call(
        _ssm_gate_norm_kernel,
        grid=(1,),
        in_specs=[full(a) for a in args],
        out_specs=full(y),
        out_shape=jax.ShapeDtypeStruct(y.shape, MXU_DTYPE),
        name="ssm_gate_norm",
    )(*args)


def _mamba2_mixer_pallas(x3d, conv_state, ssm_state, w_in, conv_w, conv_b, dt_bias, a_log, d_skip, norm_g):
    b_, t_, _ = x3d.shape
    z, xs, bm, cm, dt, conv_new = _ssm_project(x3d, conv_state, w_in, conv_w, conv_b, dt_bias)
    if t_ == 1:
        y, h_new = _ssm_step(ssm_state, xs, bm, cm, dt, a_log)
        yg = _ssm_gate_norm_rows(y, xs, z, d_skip, norm_g)
    else:
        yg, h_new = _ssm_chunk_scan(b_, t_, xs, bm, cm, dt, z, a_log, d_skip, norm_g)
    return yg, conv_new, h_new.astype(ssm_state.dtype)


PAGES_PER_STEP = 8


def _sortable_key(score):
    bits = pltpu.bitcast(score, jnp.int32)
    return jnp.where(bits >= 0, bits, bits ^ jnp.int32(0x7FFFFFFF))


def _dsa_decode_kernel(pt_ref, iq_ref, iw_ref, q_ref, ikn_ref, kn_ref, vn_ref, *rest,
                       topk, col_bits, n_steps, pages):
    idx_refs, k_refs, v_refs = rest[:pages], rest[pages:2 * pages], rest[2 * pages:3 * pages]
    o_ref, key_ref, dense_ref, sel_ref, m_ref, l_ref, acc_ref = rest[3 * pages:]
    s = pl.program_id(1)
    nk = key_ref.shape[2]
    nt = (((1,), (1,)), ((), ()))
    iq = iq_ref[0]
    iw = iw_ref[0]
    lane = lax.broadcasted_iota(jnp.int32, (1, nk), 1)

    def row_dot(a, row):
        return jnp.sum(a.astype(jnp.float32) * row.astype(jnp.float32), axis=1, keepdims=True)

    def index_score(sc):
        return jnp.sum(iw * jnp.maximum(sc, 0.0), axis=0, keepdims=True)

    def fold(hit):
        out = hit[:, 0:128]
        for l in range(1, nk // 128):
            out = out + hit[:, l * 128:(l + 1) * 128]
        return out

    step_row = lax.broadcasted_iota(jnp.int32, dense_ref.shape, 0)

    @pl.when(s == 0)
    def _reset():
        dense_ref[...] = jnp.full(dense_ref.shape, INT32_MIN, jnp.int32)

    @pl.when(s < n_steps)
    def _score():
        ik_t = jnp.concatenate([r[0] for r in idx_refs], axis=1).astype(MXU_DTYPE)
        key = _sortable_key(index_score(jnp.dot(iq, ik_t, preferred_element_type=jnp.float32)))
        key_ref[s] = key
        dense_ref[...] = jnp.where(step_row == s, key, dense_ref[...])

    @pl.when(s == n_steps - 1)
    def _select():
        key_new = _sortable_key(index_score(row_dot(iq, ikn_ref[0])))
        dense = dense_ref[...]
        dense_col = step_row * nk + lax.broadcasted_iota(jnp.int32, dense_ref.shape, 1)

        def count(pred_past, pred_new):
            hit = jnp.where(pred_past(dense, dense_col), 1.0, 0.0)
            cnt = _tree_sum([hit[:, l * 128:(l + 1) * 128] for l in range(nk // 128)])
            total = jnp.sum(jnp.sum(cnt, axis=1, keepdims=True), axis=0, keepdims=True)
            return total + jnp.where(pred_new(key_new), 1.0, 0.0)

        def at_least(cand):
            return count(lambda k, c: k >= cand, lambda k: k >= cand) >= float(topk)

        def two_bits(i, thr):
            hi = lax.shift_left(jnp.int32(1), jnp.int32(31) - 2 * i)
            lo = lax.shift_left(jnp.int32(1), jnp.int32(30) - 2 * i)
            c1, c2, c3 = thr ^ lo, thr ^ hi, thr ^ hi ^ lo
            return jnp.where(at_least(c3), c3, jnp.where(at_least(c2), c2, jnp.where(at_least(c1), c1, thr)))

        thr = lax.fori_loop(0, 16, two_bits, jnp.full((1, 1), INT32_MIN, jnp.int32))
        need = float(topk) - count(lambda k, c: k > thr, lambda k: k > thr)

        def col_body(i, last):
            cand = last | lax.shift_left(jnp.int32(1), jnp.int32(col_bits - 1) - i)
            ties = count(lambda k, c: (k == thr) & (c < cand),
                         lambda k: (k == thr) & (jnp.int32(n_steps * nk) < cand))
            return jnp.where(ties < need, cand, last)

        n_tied = count(lambda k, c: k == thr, lambda k: k == thr)
        last_tie = lax.cond(n_tied[0, 0] > need[0, 0],
                            lambda: lax.fori_loop(0, col_bits, col_body, jnp.zeros((1, 1), jnp.int32)),
                            lambda: jnp.full((1, 1), 2 ** col_bits - 1, jnp.int32))
        sel_ref[0] = jnp.broadcast_to(thr, sel_ref.shape[1:])
        sel_ref[1] = jnp.broadcast_to(last_tie, sel_ref.shape[1:])
        sel_ref[2] = jnp.broadcast_to(key_new, sel_ref.shape[1:])

    gsz = ATT_HEADS // ATT_KV_HEADS
    q = q_ref[0]
    q_wide = jnp.concatenate([q] * ATT_KV_HEADS, axis=1)
    head_i = lax.broadcasted_iota(jnp.int32, q_wide.shape, 0)
    col_i = lax.broadcasted_iota(jnp.int32, q_wide.shape, 1)
    own_group = (col_i // ATT_HEAD_DIM) == (head_i // gsz)
    q_blk = jnp.where(own_group, q_wide, jnp.zeros_like(q_wide))

    def keep_mask(key, col):
        thr, last_tie = sel_ref[0, 0:1, 0:1], sel_ref[1, 0:1, 0:1]
        return (key > thr) | ((key == thr) & (col <= last_tie))

    def online_update(logits, weighted_values):
        m_old = m_ref[...]
        m_new = jnp.maximum(m_old, jnp.max(logits, axis=1, keepdims=True))
        p = jnp.exp(logits - m_new)
        alpha = jnp.exp(m_old - m_new)
        l_ref[...] = alpha * l_ref[...] + jnp.sum(p, axis=1, keepdims=True)
        acc_ref[...] = alpha * acc_ref[...] + weighted_values(p.astype(MXU_DTYPE))
        m_ref[...] = m_new

    @pl.when(s == n_steps)
    def _init():
        m_ref[...] = jnp.full(m_ref.shape, MASK_NEG, jnp.float32)
        l_ref[...] = jnp.zeros(l_ref.shape, jnp.float32)
        acc_ref[...] = jnp.zeros(acc_ref.shape, jnp.float32)

    @pl.when(s >= n_steps)
    def _attend():
        st = s - n_steps
        bias = jnp.where(keep_mask(key_ref[st], lane + st * nk), 0.0, MASK_NEG)
        k_t = jnp.concatenate([r[0] for r in k_refs], axis=1).astype(MXU_DTYPE)
        v_t = jnp.concatenate([r[0] for r in v_refs], axis=1).astype(MXU_DTYPE)
        online_update(jnp.dot(q_blk, k_t, preferred_element_type=jnp.float32) + bias,
                      lambda p: lax.dot_general(p, v_t, nt, preferred_element_type=jnp.float32))

    @pl.when(s == 2 * n_steps - 1)
    def _finish():
        keep_new = keep_mask(sel_ref[2, 0:1, 0:1], jnp.int32(n_steps * nk))
        logit = row_dot(q_blk, kn_ref[0])
        v_row = vn_ref[0].astype(jnp.float32)
        online_update(logit + jnp.where(keep_new, 0.0, MASK_NEG), lambda p: p.astype(jnp.float32) * v_row)
        out = jnp.where(own_group, acc_ref[...] / l_ref[...], 0.0)
        o = out[:, 0:ATT_HEAD_DIM]
        for g in range(1, ATT_KV_HEADS):
            o = o + out[:, g * ATT_HEAD_DIM:(g + 1) * ATT_HEAD_DIM]
        o_ref[0] = o.astype(o_ref.dtype)


def _dsa_decode(q, iq, iw, ik_new, k_new, v_new, cache_k, cache_v, cache_idx_k, page_table):
    b_, n_pages = page_table.shape
    n_pool, page = cache_k.shape[0], cache_k.shape[1]
    pages = PAGES_PER_STEP
    n_steps = n_pages // pages
    past = n_pages * page
    ck = jnp.transpose(cache_k, (0, 2, 3, 1)).reshape(n_pool, ATT_KV_DIM, page)
    cv = jnp.transpose(cache_v, (0, 2, 3, 1)).reshape(n_pool, ATT_KV_DIM, page)
    cik = jnp.swapaxes(cache_idx_k, 1, 2)
    per_seq = lambda a: pl.BlockSpec((1,) + a.shape[1:], lambda b, s, pt: (b,) + (0,) * (a.ndim - 1))

    def paged(width, j, attend_phase):
        def index(b, s, pt):
            grp = jnp.maximum(s - n_steps, 0) if attend_phase else jnp.minimum(s, n_steps - 1)
            return (pt[b, grp * pages + j], 0, 0)
        return pl.BlockSpec((1, width, page), index)

    small = [iq.reshape(b_, IDX_HEADS, IDX_DIM), iw[:, :IDX_HEADS].reshape(b_, IDX_HEADS, 1),
             q.reshape(b_, ATT_HEADS, ATT_HEAD_DIM), ik_new.astype(MXU_DTYPE).reshape(b_, 1, IDX_DIM),
             k_new.astype(MXU_DTYPE).reshape(b_, 1, ATT_KV_DIM), v_new.astype(MXU_DTYPE).reshape(b_, 1, ATT_KV_DIM)]
    grid_spec = pltpu.PrefetchScalarGridSpec(
        num_scalar_prefetch=1,
        grid=(b_, 2 * n_steps),
        in_specs=[per_seq(a) for a in small]
        + [paged(IDX_DIM, j, False) for j in range(pages)]
        + [paged(ATT_KV_DIM, j, True) for j in range(pages)]
        + [paged(ATT_KV_DIM, j, True) for j in range(pages)],
        out_specs=pl.BlockSpec((1, ATT_HEADS, ATT_HEAD_DIM), lambda b, s, pt: (b, 0, 0)),
        scratch_shapes=[pltpu.VMEM((n_steps, 1, pages * page), jnp.int32),
                        pltpu.VMEM((n_steps, pages * page), jnp.int32),
                        pltpu.VMEM((3, SUBLANES, 128), jnp.int32),
                        pltpu.VMEM((ATT_HEADS, 1), jnp.float32),
                        pltpu.VMEM((ATT_HEADS, 1), jnp.float32),
                        pltpu.VMEM((ATT_HEADS, ATT_KV_DIM), jnp.float32)])
    o = pl.pallas_call(
        functools.partial(_dsa_decode_kernel, topk=min(TOPK_MAX, (past + 1) // 4),
                          col_bits=max(1, past.bit_length()), n_steps=n_steps, pages=pages),
        grid_spec=grid_spec,
        out_shape=jax.ShapeDtypeStruct((b_, ATT_HEADS, ATT_HEAD_DIM), MXU_DTYPE),
        compiler_params=pltpu.CompilerParams(
            dimension_semantics=("parallel", "arbitrary"),
            vmem_limit_bytes=V7X_VMEM_LIMIT_BYTES),
        name="dsa_decode",
    )(page_table, *small, *([cik] * pages), *([ck] * pages), *([cv] * pages))
    return o.reshape(b_, ATT_Q_DIM)


def _dsa_sample_pallas(x3d, cache_k, cache_v, cache_idx_k, page_table, w_in, kn_g, kn_b):
    b_, t_, _ = x3d.shape
    past = page_table.shape[1] * cache_k.shape[1]
    pos = jnp.full((b_,), past, jnp.int32)
    q, iq, v, _, iw, kT, _, ikT, _ = _dsa_project(x3d.reshape(1, b_, D_MODEL), pos, w_in, kn_g, kn_b)
    k = _untranspose_groups(kT)[0]
    ik = _untranspose_groups(ikT)[0]
    o = _dsa_decode(q, iq, iw, ik, k, v, cache_k, cache_v, cache_idx_k, page_table)
    kv4 = lambda u: u.reshape(b_, t_, ATT_KV_HEADS, ATT_HEAD_DIM)
    return o, kv4(k), kv4(v), ik.reshape(b_, t_, IDX_DIM)


def _layer_norm(x, g, b):
    xf = x.astype(jnp.float32)
    mu = jnp.mean(xf, -1, keepdims=True)
    var = jnp.mean(jnp.square(xf - mu), -1, keepdims=True)
    return ((xf - mu) * lax.rsqrt(var + LN_EPS)).astype(x.dtype) * g + b


def _rope_partial(x, pos, rot_dim):
    half = rot_dim // 2
    inv = ROPE_THETA ** (-jnp.arange(half, dtype=jnp.float32) / half)
    ang = pos.astype(jnp.float32)[:, None] * inv[None, :]
    cos = jnp.cos(ang)[:, None, :]
    sin = jnp.sin(ang)[:, None, :]
    xf = x[..., :rot_dim].astype(jnp.float32)
    x1, x2 = xf[..., :half], xf[..., half:]
    rot = jnp.concatenate([x1 * cos - x2 * sin, x2 * cos + x1 * sin], axis=-1).astype(x.dtype)
    return jnp.concatenate([rot, x[..., rot_dim:]], axis=-1)


def _gather_rows(rows, idx):
    return jax.vmap(lambda r, i: r[i])(rows, idx)


def _gmlp_mixer(x, w_in, ln_g, ln_b, ws, bs):
    B_, T, _ = x.shape
    u, v = jnp.split(jax.nn.gelu(x @ w_in), 2, axis=-1)
    v = _layer_norm(v, ln_g, ln_b)
    l = min(T, CHUNK)
    c = T // l
    mask = jnp.tril(jnp.ones((l, l), dtype=bool))
    w = jnp.where(mask, ws[:, :l, :l], 0.0)
    vc = v.reshape(B_, c, l, GM_GROUPS, GM_GROUP_DIM)
    mixed = jnp.einsum('gts,bcsgd->bctgd', w, vc) + jnp.transpose(bs[:, :l])[:, :, None]
    return u * mixed.reshape(B_, T, GM_WIDTH), v


def _ssd_chunked(xs, dt, a, bm, cm, h0):
    B_, T = xs.shape[:2]
    l = min(T, SSM_CHUNK)
    c = T // l
    blk = lambda t: t.reshape((B_, c, l) + t.shape[2:])
    xdt = blk(xs.astype(jnp.float32) * dt[..., None])
    bc, cc, acum = blk(bm), blk(cm), jnp.cumsum(blk(a), axis=2)
    at = jnp.moveaxis(acum, 2, -1)
    causal = jnp.tril(jnp.ones((l, l), dtype=bool))
    seg = jnp.exp(jnp.where(causal, at[..., :, None] - at[..., None, :], -jnp.inf))
    cb = jnp.einsum('bctgn,bcsgn->bcgts', cc, bc)
    y_diag = jnp.einsum('bcgts,bcgets,bcsgep->bctgep', cb, seg, xdt)
    states = jnp.einsum('bclgn,bclge,bclgep->bcgepn', bc, jnp.exp(acum[:, :, -1:] - acum), xdt)
    chunk_decay = jnp.exp(acum[:, :, -1])

    def step(h, inp):
        dec, st = inp
        return h * dec[..., None, None] + st, h

    h_last, h_in = lax.scan(step, h0, (jnp.moveaxis(chunk_decay, 1, 0), jnp.moveaxis(states, 1, 0)))
    y_off = jnp.einsum('bctgn,bcgepn,bctge->bctgep', cc, jnp.moveaxis(h_in, 0, 1), jnp.exp(acum))
    return (y_diag + y_off).reshape(B_, T, SSM_GROUPS, SSM_HPG, SSM_HEAD_DIM), h_last


def _mamba2_mixer(x, conv_state, ssm_state, w_in, conv_w, conv_b, dt_bias, a_log, d_skip, norm_g):
    B_, T, _ = x.shape
    z, xbc, dt = jnp.split(x @ w_in, [SSM_D_INNER, SSM_D_INNER + SSM_CONV_DIM], axis=-1)
    xbc_ext = jnp.concatenate([conv_state, xbc], axis=1)
    conv = conv_b
    for j in range(SSM_CONV):
        conv = conv + xbc_ext[:, j:j + T] * conv_w[j]
    xbc = jax.nn.silu(conv)
    xs, bm, cm = jnp.split(xbc, [SSM_D_INNER, SSM_D_INNER + SSM_GROUPS * SSM_STATE], axis=-1)
    xs = xs.reshape(B_, T, SSM_GROUPS, SSM_HPG, SSM_HEAD_DIM)
    bm = bm.reshape(B_, T, SSM_GROUPS, SSM_STATE)
    cm = cm.reshape(B_, T, SSM_GROUPS, SSM_STATE)
    dt = jax.nn.softplus((dt + dt_bias).astype(jnp.float32)).reshape(B_, T, SSM_GROUPS, SSM_HPG)
    a_neg = -jnp.exp(a_log.astype(jnp.float32)).reshape(SSM_GROUPS, SSM_HPG)
    h0 = ssm_state.astype(jnp.float32).reshape(B_, SSM_GROUPS, SSM_HPG, SSM_HEAD_DIM, SSM_STATE)
    y, h_last = _ssd_chunked(xs, dt, dt * a_neg, bm, cm, h0)
    y = y.astype(x.dtype) + xs * d_skip.reshape(SSM_GROUPS, SSM_HPG, 1)
    yg = (y.reshape(B_, T, SSM_D_INNER) * jax.nn.silu(z)).reshape(B_, T, SSM_GROUPS, -1).astype(jnp.float32)
    yg = (yg * lax.rsqrt(jnp.mean(jnp.square(yg), -1, keepdims=True) + LN_EPS)).astype(x.dtype)
    yg = yg.reshape(B_, T, SSM_D_INNER) * norm_g
    new_ssm = h_last.reshape(B_, SSM_HEADS, SSM_HEAD_DIM, SSM_STATE).astype(ssm_state.dtype)
    return yg, xbc_ext[:, T:], new_ssm


def _dsa_project_jax(x, pos, w_in, kn_g, kn_b):
    B_, T, _ = x.shape
    q, k, v, iq, ik, iw = jnp.split(x @ w_in, list(ATT_IN_SPLITS), axis=-1)
    q = _rope_partial(q.reshape(B_, T, ATT_HEADS, ATT_HEAD_DIM), pos, ROPE_DIM)
    k = _rope_partial(k.reshape(B_, T, ATT_KV_HEADS, ATT_HEAD_DIM), pos, ROPE_DIM)
    v = v.reshape(B_, T, ATT_KV_HEADS, ATT_HEAD_DIM)
    iq = _rope_partial(iq.reshape(B_, T, IDX_HEADS, IDX_DIM), pos, IDX_ROPE_DIM)
    ik = _rope_partial(_layer_norm(ik, kn_g, kn_b)[:, :, None, :], pos, IDX_ROPE_DIM)[:, :, 0, :]
    iw = iw * (IDX_HEADS ** -0.5 * IDX_DIM ** -0.5)
    return q, k, v, iq, ik, iw


def _dsa_select(iq, iw, ik, qpos, topk):
    s = jnp.einsum('bqhd,bsd->bqhs', iq, ik)
    score = jnp.einsum('bqh,bqhs->bqs', iw, jax.nn.relu(s)).astype(jnp.float32)
    adm = jnp.arange(ik.shape[1])[None, :] <= qpos[:, None]
    score = jnp.where(adm[None], score, -jnp.inf)
    _, idx = lax.top_k(score, topk)
    return idx, idx <= qpos[None, :, None]


def _sparse_attend(q, k_sel, v_sel, valid):
    B_, Q = q.shape[:2]
    qg = q.reshape(B_, Q, ATT_KV_HEADS, ATT_HEADS // ATT_KV_HEADS, ATT_HEAD_DIM)
    s = jnp.einsum('bqhgd,bqkhd->bqhgk', qg, k_sel).astype(jnp.float32) * (ATT_HEAD_DIM ** -0.5)
    s = jnp.where(valid[:, :, None, None, :], s, -jnp.inf)
    p = jax.nn.softmax(s, axis=-1).astype(v_sel.dtype)
    o = jnp.einsum('bqhgk,bqkhd->bqhgd', p, v_sel)
    return o.reshape(B_, Q, ATT_Q_DIM)


def _dsa_prompt(x, w_in, kn_g, kn_b):
    B_, T, _ = x.shape
    q, k, v, iq, ik, iw = _dsa_project_jax(x, jnp.arange(T), w_in, kn_g, kn_b)
    topk = min(TOPK_MAX, T // 4)

    def block(bi):
        t0 = bi * Q_BLOCK
        sl = lambda t: lax.dynamic_slice_in_dim(t, t0, Q_BLOCK, axis=1)
        qpos = t0 + jnp.arange(Q_BLOCK)
        idx, valid = _dsa_select(sl(iq), sl(iw), ik, qpos, topk)
        return _sparse_attend(sl(q), _gather_rows(k, idx), _gather_rows(v, idx), valid)

    o = lax.map(block, jnp.arange(T // Q_BLOCK))
    o = jnp.moveaxis(o, 0, 1).reshape(B_, T, ATT_Q_DIM)
    return o, k, v, ik


def _dsa_sample(x, cache_k, cache_v, cache_idx_k, page_table, w_in, kn_g, kn_b):
    B_, T, _ = x.shape
    page = cache_k.shape[1]
    past = page_table.shape[1] * page
    pos = past + jnp.arange(T)
    q, k, v, iq, ik, iw = _dsa_project_jax(x, pos, w_in, kn_g, kn_b)
    ik_all = jnp.concatenate([cache_idx_k[page_table].reshape(B_, past, IDX_DIM), ik], axis=1)
    idx, valid = _dsa_select(iq, iw, ik_all, pos, min(TOPK_MAX, (past + T) // 4))
    past_idx = jnp.minimum(idx, past - 1)
    phys = jnp.take_along_axis(page_table, (past_idx // page).reshape(B_, -1), axis=1).reshape(idx.shape)
    off = past_idx % page
    new_idx = jnp.clip(idx - past, 0, T - 1)
    is_new = (idx >= past)[..., None, None]
    k_sel = jnp.where(is_new, _gather_rows(k, new_idx), cache_k[phys, off])
    v_sel = jnp.where(is_new, _gather_rows(v, new_idx), cache_v[phys, off])
    o = _sparse_attend(q, k_sel, v_sel, valid)
    return o, k, v, ik


def _rwkv7_mixer(x, shift, wkv, mu, w_r, w_k, w_v, w0, w1, w2, a0, a1, a2, g1, g2,
                 k_k, k_a, r_k, gn_g, gn_b):
    B_, T, _ = x.shape
    x_prev = jnp.concatenate([shift[:, None, :], x[:, :-1]], axis=1)
    xm = x[None] + (x_prev - x)[None] * mu[:, None, None, :]
    xr, xw, xk, xv, xa, xg = xm
    r = xr @ w_r
    w_log = -jax.nn.softplus(-(w0 + jnp.tanh(xw @ w1) @ w2)) - 0.5
    k = xk @ w_k
    v = xv @ w_v
    a = jax.nn.sigmoid(a0 + (xa @ a1) @ a2)
    g = jax.nn.sigmoid(xg @ g1) @ g2
    heads = lambda t: t.reshape(B_, T, RW_HEADS, RW_HEAD)
    kk = heads(k * k_k).astype(jnp.float32)
    kk = kk / jnp.maximum(jnp.sqrt(jnp.sum(kk * kk, -1, keepdims=True)), 1e-12)
    k = k * (1.0 + (a - 1.0) * k_a)
    decay = jnp.exp(-jnp.exp(w_log.astype(jnp.float32)))
    r, k, v, a, decay = heads(r), heads(k), heads(v), heads(a), heads(decay)
    seq = tuple(jnp.moveaxis(t.astype(jnp.float32), 1, 0) for t in (r, decay, k, v, kk, kk * a))

    def step(s, inp):
        r_t, d_t, k_t, v_t, kk_t, b_t = inp
        sa = jnp.einsum('bhij,bhj->bhi', s, kk_t)
        s = s * d_t[:, :, None, :] - sa[..., None] * b_t[:, :, None, :] + v_t[..., None] * k_t[:, :, None, :]
        return s, jnp.einsum('bhij,bhj->bhi', s, r_t)

    s_last, y = lax.scan(step, wkv.astype(jnp.float32), seq)
    y = jnp.moveaxis(y, 0, 1)
    mu_y = jnp.mean(y, -1, keepdims=True)
    var_y = jnp.mean(jnp.square(y - mu_y), -1, keepdims=True)
    yn = ((y - mu_y) * lax.rsqrt(var_y + RW_GN_EPS)).reshape(B_, T, D_MODEL).astype(x.dtype) * gn_g + gn_b
    bonus = (jnp.sum(r * k * r_k, -1, keepdims=True) * v).reshape(B_, T, D_MODEL)
    return (yn + bonus) * g, x[:, -1], s_last.astype(wkv.dtype)


def kernel(x_prompt, x_sample, state_ssm_conv, state_ssm, cache_k, cache_v, cache_idx_k, state_rwkv_shift, state_rwkv_wkv, page_table, p_prompt, p_sample, ln_g, ln_b, ffn_w_up, ffn_w_down, ple_w_p, ple_w_g, ple_b_g, gm_w_in, gm_ln_g, gm_ln_b, gm_ws, gm_bs, gm_w_out, ssm_w_in, ssm_conv_w, ssm_conv_b, ssm_dt_bias, ssm_a_log, ssm_d, ssm_norm_g, ssm_w_out, att_w_in, att_kn_g, att_kn_b, att_w_out, rw_mu, rw_w_r, rw_w_k, rw_w_v, rw_w_o, rw_w0, rw_w1, rw_w2, rw_a0, rw_a1, rw_a2, rw_g1, rw_g2, rw_k_k, rw_k_a, rw_r_k, rw_gn_g, rw_gn_b):
    bp, tp, _ = x_prompt.shape
    bs_, ts, _ = x_sample.shape
    bf = lambda w: w.astype(jnp.bfloat16)
    w_up_bf, w_down_bf = bf(ffn_w_up), bf(ffn_w_down)
    ple_wp_bf, ple_wg_bf = bf(ple_w_p), bf(ple_w_g)
    pp3 = p_prompt.reshape(DEPTH, bp * tp, PLE_DIM)
    ps3 = p_sample.reshape(DEPTH, bs_ * ts, PLE_DIM)

    yp = x_prompt.reshape(bp * tp, D_MODEL)
    ys = x_sample.reshape(bs_ * ts, D_MODEL)
    r3p = lambda t: t.reshape(bp, tp, -1)
    r3s = lambda t: t.reshape(bs_, ts, -1)
    f2 = lambda t: t.reshape(-1, t.shape[-1])

    for i in range(DEPTH):
        yp = _ffn_sub(yp, w_up_bf, w_down_bf, i, 0, ln_g[i, 0], ln_b[i, 0])
        ys = _ffn_sub(ys, w_up_bf, w_down_bf, i, 0, ln_g[i, 0], ln_b[i, 0])
        m = i % N_MIXERS
        if m == 0:
            gm_args = (gm_w_in, gm_ln_g, gm_ln_b, gm_ws, gm_bs, gm_w_out, ln_g[i, 1], ln_b[i, 1])
            yp, = _gmlp_block(yp, tp, *gm_args, False)
            ys, gm_v_s = _gmlp_block(ys, ts, *gm_args, True)
            gm_v_s = r3s(gm_v_s)
        elif m == 1:
            ssm_args = (ssm_w_in, ssm_conv_w, ssm_conv_b, ssm_dt_bias, ssm_a_log, ssm_d, ssm_norm_g)
            hp, conv_p, ssm_p = _mamba2_mixer_pallas(
                r3p(yp), jnp.zeros((bp, SSM_CONV - 1, SSM_CONV_DIM), yp.dtype),
                jnp.zeros((bp, SSM_HEADS, SSM_HEAD_DIM, SSM_STATE), yp.dtype), *ssm_args)
            hs, conv_s, ssm_s = _mamba2_mixer_pallas(r3s(ys), state_ssm_conv, state_ssm, *ssm_args)
            w_out = bf(ssm_w_out)
        elif m == 2:
            hp, k_p, v_p, ik_p = _dsa_prompt_pallas(r3p(yp), att_w_in, att_kn_g, att_kn_b)
            hs, k_s, v_s, ik_s = _dsa_sample_pallas(r3s(ys), cache_k, cache_v, cache_idx_k, page_table,
                                                    att_w_in, att_kn_g, att_kn_b)
            w_out = bf(att_w_out)
        else:
            rw_args = (rw_mu, rw_w_r, rw_w_k, rw_w_v, rw_w0, rw_w1, rw_w2, rw_a0, rw_a1, rw_a2,
                       rw_g1, rw_g2, rw_k_k, rw_k_a, rw_r_k, rw_gn_g, rw_gn_b)
            hp, gate_p, sh_p, wkv_p = _rwkv7_mixer_pallas(
                r3p(yp), jnp.zeros((bp, D_MODEL), yp.dtype),
                jnp.zeros((bp, RW_HEADS, RW_HEAD, RW_HEAD), yp.dtype), *rw_args)
            hs, gate_s, sh_s, wkv_s = _rwkv7_mixer_pallas(r3s(ys), state_rwkv_shift, state_rwkv_wkv, *rw_args)
            w_out = bf(rw_w_o)
        if m == 3:
            yp = _proj_gate_post_norm(yp, hp, gate_p, w_out, ln_g[i, 1], ln_b[i, 1])
            ys = _proj_gate_post_norm(ys, hs, gate_s, w_out, ln_g[i, 1], ln_b[i, 1])
        elif m != 0:
            yp = _proj_post_norm(yp, f2(hp), w_out, ln_g[i, 1], ln_b[i, 1])
            ys = _proj_post_norm(ys, f2(hs), w_out, ln_g[i, 1], ln_b[i, 1])
        yp = _ffn_sub(yp, w_up_bf, w_down_bf, i, 1, ln_g[i, 2], ln_b[i, 2])
        ys = _ffn_sub(ys, w_up_bf, w_down_bf, i, 1, ln_g[i, 2], ln_b[i, 2])
        yp = _ple_add(yp, pp3, ple_wp_bf, ple_wg_bf, ple_b_g[i], i)
        ys = _ple_add(ys, ps3, ple_wp_bf, ple_wg_bf, ple_b_g[i], i)

    return (r3p(yp), r3s(ys), gm_v_s, conv_p, ssm_p, conv_s, ssm_s, k_p, v_p, ik_p, k_s, v_s, ik_s,
            sh_p, wkv_p, sh_s, wkv_s)
```

```python
import functools
import math

import jax
import jax.numpy as jnp
from jax import lax
from jax.experimental import pallas as pl
from jax.experimental.pallas import tpu as pltpu

D_MODEL = 1024
DEPTH = 4
N_MIXERS = 4
PLE_DIM = 256
D_FF = 2816
ALPHA = (2 * DEPTH) ** 0.25
LN_EPS = 1e-5

CHUNK = 128
GM_WIDTH = 2 * D_MODEL
GM_GROUPS = 8
GM_GROUP_DIM = GM_WIDTH // GM_GROUPS

SSM_D_INNER = 2 * D_MODEL
SSM_HEAD_DIM = 64
SSM_HEADS = SSM_D_INNER // SSM_HEAD_DIM
SSM_GROUPS = 4
SSM_HPG = SSM_HEADS // SSM_GROUPS
SSM_STATE = 128
SSM_CONV = 4
SSM_CONV_DIM = SSM_D_INNER + 2 * SSM_GROUPS * SSM_STATE
SSM_CHUNK = 128

ATT_HEADS = 16
ATT_KV_HEADS = 4
ATT_HEAD_DIM = D_MODEL // ATT_HEADS
ROPE_DIM = ATT_HEAD_DIM // 4
ROPE_THETA = 500000.0
IDX_HEADS = 8
IDX_DIM = 64
IDX_ROPE_DIM = IDX_DIM // 4
TOPK_MAX = 256
Q_BLOCK = 128
ATT_Q_DIM = ATT_HEADS * ATT_HEAD_DIM
ATT_KV_DIM = ATT_KV_HEADS * ATT_HEAD_DIM
ATT_IN_SPLITS = (ATT_Q_DIM, ATT_Q_DIM + ATT_KV_DIM, ATT_Q_DIM + 2 * ATT_KV_DIM,
                 ATT_Q_DIM + 2 * ATT_KV_DIM + IDX_HEADS * IDX_DIM,
                 ATT_Q_DIM + 2 * ATT_KV_DIM + IDX_HEADS * IDX_DIM + IDX_DIM)

RW_HEAD = 64
RW_HEADS = D_MODEL // RW_HEAD
RW_GN_EPS = 64e-5

V7X_VMEM_LIMIT_BYTES = 52 * 1024 * 1024
FF_TILE = D_FF // 2
ROW_TILE = 512


def _row_tile(m):
    return ROW_TILE if m % ROW_TILE == 0 else m


def _ln_rows(y, g, b):
    mu = jnp.mean(y, axis=-1, keepdims=True)
    yc = y - mu
    var = jnp.mean(yc * yc, axis=-1, keepdims=True)
    return yc * lax.rsqrt(var + LN_EPS) * g + b


FFN_ROW_TILE = 1024
FFN_ROW_PARTS = 2


def _ffn_kernel(x_ref, wu_ref, wd_ref, g_ref, b_ref, o_ref, *, parts):
    rows = x_ref.shape[0] // parts
    for p in range(parts):
        x = x_ref[p * rows:(p + 1) * rows, :]
        xb = x.astype(MXU_DTYPE)
        acc = None
        for f in range(D_FF // FF_TILE):
            cols = slice(f * FF_TILE, (f + 1) * FF_TILE)
            gate = jnp.dot(xb, wu_ref[:, cols], preferred_element_type=jnp.float32)
            lin = jnp.dot(xb, wu_ref[:, D_FF + f * FF_TILE:D_FF + (f + 1) * FF_TILE],
                          preferred_element_type=jnp.float32)
            h = (gate * jax.nn.sigmoid(gate) * lin).astype(MXU_DTYPE)
            part = jnp.dot(h, wd_ref[cols, :], preferred_element_type=jnp.float32)
            acc = part if acc is None else acc + part
        o_ref[p * rows:(p + 1) * rows, :] = _ln_rows(ALPHA * x + 0.5 * acc, g_ref[...], b_ref[...])


def _ffn_sub(x2d, w_up, w_down, layer, half, g, b):
    m = x2d.shape[0]
    tm = FFN_ROW_TILE if m % FFN_ROW_TILE == 0 else m
    parts = FFN_ROW_PARTS if tm == FFN_ROW_TILE else 1
    resident = dict(pipeline_mode=pl.Buffered(1))
    return pl.pallas_call(
        functools.partial(_ffn_kernel, parts=parts),
        grid=(m // tm,),
        in_specs=[
            pl.BlockSpec((tm, D_MODEL), lambda i: (i, 0)),
            pl.BlockSpec((None, None, D_MODEL, 2 * D_FF), lambda i: (layer, half, 0, 0), **resident),
            pl.BlockSpec((None, None, D_FF, D_MODEL), lambda i: (layer, half, 0, 0), **resident),
            pl.BlockSpec((1, D_MODEL), lambda i: (0, 0)),
            pl.BlockSpec((1, D_MODEL), lambda i: (0, 0)),
        ],
        out_specs=pl.BlockSpec((tm, D_MODEL), lambda i: (i, 0)),
        out_shape=jax.ShapeDtypeStruct((m, D_MODEL), jnp.float32),
        compiler_params=pltpu.CompilerParams(
            dimension_semantics=("parallel",),
            vmem_limit_bytes=V7X_VMEM_LIMIT_BYTES),
        name="ffn_sub",
    )(x2d, w_up, w_down, g.reshape(1, D_MODEL), b.reshape(1, D_MODEL))


def _ple_kernel(x_ref, p_ref, wp_ref, wg_ref, bg_ref, o_ref):
    x = x_ref[...]
    gate = jax.nn.sigmoid(
        jnp.dot(x.astype(jnp.bfloat16), wg_ref[...], preferred_element_type=jnp.float32) + bg_ref[...])
    emb = jnp.dot(p_ref[...].astype(jnp.bfloat16), wp_ref[...], preferred_element_type=jnp.float32)
    o_ref[...] = x + gate * emb


def _ple_add(x2d, p3d, w_p, w_g, b_g, layer):
    m = x2d.shape[0]
    tm = _row_tile(m)
    return pl.pallas_call(
        _ple_kernel,
        grid=(m // tm,),
        in_specs=[
            pl.BlockSpec((tm, D_MODEL), lambda i: (i, 0)),
            pl.BlockSpec((None, tm, PLE_DIM), lambda i: (layer, i, 0)),
            pl.BlockSpec((None, PLE_DIM, D_MODEL), lambda i: (layer, 0, 0)),
            pl.BlockSpec((None, D_MODEL, D_MODEL), lambda i: (layer, 0, 0)),
            pl.BlockSpec((1, D_MODEL), lambda i: (0, 0)),
        ],
        out_specs=pl.BlockSpec((tm, D_MODEL), lambda i: (i, 0)),
        out_shape=jax.ShapeDtypeStruct((m, D_MODEL), jnp.float32),
        compiler_params=pltpu.CompilerParams(
            dimension_semantics=("parallel",),
            vmem_limit_bytes=V7X_VMEM_LIMIT_BYTES),
        name="ple_add",
    )(x2d, p3d, w_p, w_g, b_g.reshape(1, D_MODEL))


def _proj_ln_kernel(x_ref, h_ref, w_ref, g_ref, b_ref, o_ref):
    y = ALPHA * x_ref[...] + jnp.dot(h_ref[...].astype(jnp.bfloat16), w_ref[...],
                                     preferred_element_type=jnp.float32)
    o_ref[...] = _ln_rows(y, g_ref[...], b_ref[...])


def _proj_post_norm(x2d, h2d, w_out, g, b):
    m = x2d.shape[0]
    k = h2d.shape[1]
    tm = _row_tile(m)
    return pl.pallas_call(
        _proj_ln_kernel,
        grid=(m // tm,),
        in_specs=[
            pl.BlockSpec((tm, D_MODEL), lambda i: (i, 0)),
            pl.BlockSpec((tm, k), lambda i: (i, 0)),
            pl.BlockSpec((k, D_MODEL), lambda i: (0, 0)),
            pl.BlockSpec((1, D_MODEL), lambda i: (0, 0)),
            pl.BlockSpec((1, D_MODEL), lambda i: (0, 0)),
        ],
        out_specs=pl.BlockSpec((tm, D_MODEL), lambda i: (i, 0)),
        out_shape=jax.ShapeDtypeStruct((m, D_MODEL), jnp.float32),
        compiler_params=pltpu.CompilerParams(
            dimension_semantics=("parallel",),
            vmem_limit_bytes=V7X_VMEM_LIMIT_BYTES),
        name="proj_post_norm",
    )(x2d, h2d, w_out, g.reshape(1, D_MODEL), b.reshape(1, D_MODEL))


MXU_DTYPE = jnp.bfloat16
KEY_GROUP = 512
INT32_MIN = -2 ** 31
MASK_NEG = -1e30


def _rope_lane_tables(pos, rot_dim, head_dim):
    half = rot_dim // 2
    inv = ROPE_THETA ** (-jnp.arange(half, dtype=jnp.float32) / half)
    ang = pos.astype(jnp.float32)[:, None] * inv[None, :]
    cos, sin = jnp.cos(ang), jnp.sin(ang)
    n = pos.shape[0]
    rest = head_dim - rot_dim
    c = jnp.concatenate([cos, cos, jnp.ones((n, rest), jnp.float32)], axis=1)
    s1 = jnp.concatenate([-sin, jnp.zeros((n, half + rest), jnp.float32)], axis=1)
    s2 = jnp.concatenate([jnp.zeros((n, half), jnp.float32), sin, jnp.zeros((n, rest), jnp.float32)], axis=1)
    reps = 128 // head_dim
    tile = lambda t: jnp.tile(t, (1, reps))
    return tile(c), tile(s1), tile(s2), cos.T, sin.T


def _rope_lanes(t, c, s1, s2, half):
    n = t.shape[1]
    reps = n // 128
    tl = lambda a: jnp.concatenate([a] * reps, axis=1)
    return t * tl(c) + pltpu.roll(t, n - half, 1) * tl(s1) + pltpu.roll(t, half, 1) * tl(s2)


def _rope_rows(t, cT, sT, head_dim, half):
    pieces = []
    for h in range(t.shape[0] // head_dim):
        x1 = t[h * head_dim:h * head_dim + half]
        x2 = t[h * head_dim + half:h * head_dim + 2 * half]
        pieces += [x1 * cT - x2 * sT, x2 * cT + x1 * sT, t[h * head_dim + 2 * half:(h + 1) * head_dim]]
    return jnp.concatenate(pieces, axis=0)


def _dsa_proj_kernel(x_ref, wq_ref, wiq_ref, wv_ref, wvx_ref, wiw_ref, wkT_ref, wikT_ref,
                     c_ref, s1_ref, s2_ref, cT_ref, sT_ref, kng_ref, knb_ref, one_ref,
                     q_ref, iq_ref, v_ref, vx_ref, iw_ref, kT_ref, kTb_ref, ikT_ref, ikTb_ref):
    xb = x_ref[...].astype(MXU_DTYPE)
    c, s1, s2 = c_ref[...], s1_ref[...], s2_ref[...]
    cT, sT = cT_ref[...], sT_ref[...]
    dot = lambda a, b: jnp.dot(a, b, preferred_element_type=jnp.float32)
    dot_t = lambda w, a: lax.dot_general(w, a, (((1,), (1,)), ((), ())), preferred_element_type=jnp.float32)

    q = _rope_lanes(dot(xb, wq_ref[...]), c, s1, s2, ROPE_DIM // 2)
    q_ref[...] = (q * (ATT_HEAD_DIM ** -0.5)).astype(q_ref.dtype)
    iq = _rope_lanes(dot(xb, wiq_ref[...]), c, s1, s2, IDX_ROPE_DIM // 2)
    iq_ref[...] = iq.astype(iq_ref.dtype)
    v_ref[...] = dot(xb, wv_ref[...])
    vx_ref[...] = (dot(xb, wvx_ref[...]) + one_ref[...]).astype(vx_ref.dtype)
    iw_ref[...] = dot(xb, wiw_ref[...]) * (IDX_HEADS ** -0.5 * IDX_DIM ** -0.5)

    kT = _rope_rows(dot_t(wkT_ref[...], xb), cT, sT, ATT_HEAD_DIM, ROPE_DIM // 2)
    kT_ref[0, 0] = kT
    kTb_ref[0, 0] = kT.astype(kTb_ref.dtype)
    ikT = dot_t(wikT_ref[...], xb)
    mu = jnp.mean(ikT, axis=0, keepdims=True)
    ikc = ikT - mu
    var = jnp.mean(ikc * ikc, axis=0, keepdims=True)
    ikT = ikc * lax.rsqrt(var + LN_EPS) * kng_ref[...] + knb_ref[...]
    ikT = _rope_rows(ikT, cT, sT, IDX_DIM, IDX_ROPE_DIM // 2)
    ikT_ref[0, 0] = ikT
    ikTb_ref[0, 0] = ikT.astype(ikTb_ref.dtype)


def _dsa_project(x3d, pos, w_in, kn_g, kn_b):
    b_, t_, _ = x3d.shape
    tk = KEY_GROUP if t_ % KEY_GROUP == 0 else t_
    ng = t_ // tk
    m = b_ * t_
    w_q, w_k, w_v, w_iq, w_ik, w_iw = jnp.split(w_in, list(ATT_IN_SPLITS), axis=1)
    cast = lambda w: w.astype(MXU_DTYPE)
    w_vx = jnp.pad(w_v.reshape(D_MODEL, ATT_KV_HEADS, ATT_HEAD_DIM),
                   ((0, 0), (0, 0), (0, 128 - ATT_HEAD_DIM))).reshape(D_MODEL, ATT_KV_HEADS * 128)
    one_col = jnp.tile((jnp.arange(128) == ATT_HEAD_DIM).astype(jnp.float32), ATT_KV_HEADS)[None, :]
    w_iw_pad = jnp.pad(w_iw, ((0, 0), (0, 128 - IDX_HEADS)))
    c, s1, s2, cT, sT = _rope_lane_tables(pos, ROPE_DIM, ATT_HEAD_DIM)
    full = lambda shape: pl.BlockSpec(shape, lambda b, i: (0,) * len(shape))
    rows = lambda n: pl.BlockSpec((tk, n), lambda b, i: (b * ng + i, 0))
    ptab = lambda n: pl.BlockSpec((tk, n), lambda b, i: (i, 0))
    grp = lambda n: pl.BlockSpec((1, 1, n, tk), lambda b, i: (b, i, 0, 0))
    sds = jax.ShapeDtypeStruct
    return pl.pallas_call(
        _dsa_proj_kernel,
        grid=(b_, ng),
        in_specs=[rows(D_MODEL), full((D_MODEL, ATT_Q_DIM)), full((D_MODEL, IDX_HEADS * IDX_DIM)),
                  full((D_MODEL, ATT_KV_DIM)), full((D_MODEL, ATT_KV_HEADS * 128)), full((D_MODEL, 128)),
                  full((ATT_KV_DIM, D_MODEL)), full((IDX_DIM, D_MODEL)),
                  ptab(128), ptab(128), ptab(128),
                  pl.BlockSpec((ROPE_DIM // 2, tk), lambda b, i: (0, i)),
                  pl.BlockSpec((ROPE_DIM // 2, tk), lambda b, i: (0, i)),
                  full((IDX_DIM, 1)), full((IDX_DIM, 1)), full((1, ATT_KV_HEADS * 128))],
        out_specs=[rows(ATT_Q_DIM), rows(IDX_HEADS * IDX_DIM), rows(ATT_KV_DIM), rows(ATT_KV_HEADS * 128),
                   rows(128), grp(ATT_KV_DIM), grp(ATT_KV_DIM), grp(IDX_DIM), grp(IDX_DIM)],
        out_shape=[sds((m, ATT_Q_DIM), MXU_DTYPE), sds((m, IDX_HEADS * IDX_DIM), MXU_DTYPE),
                   sds((m, ATT_KV_DIM), jnp.float32), sds((m, ATT_KV_HEADS * 128), MXU_DTYPE),
                   sds((m, 128), jnp.float32),
                   sds((b_, ng, ATT_KV_DIM, tk), jnp.float32), sds((b_, ng, ATT_KV_DIM, tk), MXU_DTYPE),
                   sds((b_, ng, IDX_DIM, tk), jnp.float32), sds((b_, ng, IDX_DIM, tk), MXU_DTYPE)],
        compiler_params=pltpu.CompilerParams(
            dimension_semantics=("parallel", "parallel"),
            vmem_limit_bytes=V7X_VMEM_LIMIT_BYTES),
        name="dsa_project",
    )(x3d.reshape(m, D_MODEL), cast(w_q), cast(w_iq), cast(w_v), cast(w_vx), cast(w_iw_pad),
      cast(w_k.T), cast(w_ik.T), c, s1, s2, cT, sT, kn_g.reshape(IDX_DIM, 1), kn_b.reshape(IDX_DIM, 1), one_col)


def _untranspose_groups(tg):
    b_, g_, r_, tk = tg.shape
    return jnp.transpose(tg, (0, 1, 3, 2)).reshape(b_, g_ * tk, r_)


def _dsa_attend_kernel(iq_ref, iw_ref, ikT_ref, q_ref, kT_ref, vx_ref, o_ref, key_ref, m_ref, acc_ref, *,
                       topk, col_bits):
    j = pl.program_id(1)
    tq = iq_ref.shape[0]
    tk = key_ref.shape[2]
    n_groups = (j * tq + tq + tk - 1) // tk
    row = j * tq + lax.broadcasted_iota(jnp.int32, (tq, tk), 0)
    col0 = lax.broadcasted_iota(jnp.int32, (tq, tk), 1)
    dot = lambda a, b: jnp.dot(a, b, preferred_element_type=jnp.float32)

    def score_body(g, carry):
        ikT = ikT_ref[0, g]
        sc = jnp.zeros((tq, tk), jnp.float32)
        for h in range(IDX_HEADS):
            s = dot(iq_ref[:, h * IDX_DIM:(h + 1) * IDX_DIM], ikT)
            sc = sc + iw_ref[:, h:h + 1] * jnp.maximum(s, 0.0)
        bits = pltpu.bitcast(sc, jnp.int32)
        key = jnp.where(bits >= 0, bits, bits ^ jnp.int32(0x7FFFFFFF))
        key_ref[g] = jnp.where(col0 + g * tk <= row, key, jnp.int32(INT32_MIN))
        return carry

    lax.fori_loop(0, n_groups, score_body, 0)

    def bit_body(i, thr):
        cand = thr ^ lax.shift_left(jnp.int32(1), jnp.int32(31) - i)

        def count_body(g, cnt):
            hit = jnp.where(key_ref[g] >= cand, 1.0, 0.0)
            for l in range(tk // 128):
                cnt = cnt + hit[:, l * 128:(l + 1) * 128]
            return cnt

        cnt = lax.fori_loop(0, n_groups, count_body, jnp.zeros((tq, 128), jnp.float32))
        total = jnp.sum(cnt, axis=1, keepdims=True)
        return jnp.where(total >= float(topk), cand, thr)

    thr = lax.fori_loop(0, 32, bit_body, jnp.full((tq, 1), INT32_MIN, jnp.int32))

    def lane_fold(hit, cnt):
        for l in range(tk // 128):
            cnt = cnt + hit[:, l * 128:(l + 1) * 128]
        return cnt

    def above_body(g, cnt):
        return lane_fold(jnp.where(key_ref[g] > thr, 1.0, 0.0), cnt)

    n_above = jnp.sum(lax.fori_loop(0, n_groups, above_body, jnp.zeros((tq, 128), jnp.float32)),
                      axis=1, keepdims=True)
    need = float(topk) - n_above

    def col_body(i, last):
        cand = last | lax.shift_left(jnp.int32(1), jnp.int32(col_bits - 1) - i)

        def tie_body(g, cnt):
            hit = jnp.where((key_ref[g] == thr) & (col0 + g * tk < cand), 1.0, 0.0)
            return lane_fold(hit, cnt)

        ties = jnp.sum(lax.fori_loop(0, n_groups, tie_body, jnp.zeros((tq, 128), jnp.float32)),
                       axis=1, keepdims=True)
        return jnp.where(ties < need, cand, last)

    last_tie = lax.fori_loop(0, col_bits, col_body, jnp.zeros((tq, 1), jnp.int32))

    m_ref[...] = jnp.full(m_ref.shape, MASK_NEG, jnp.float32)
    acc_ref[...] = jnp.zeros(acc_ref.shape, jnp.float32)
    gsz = ATT_HEADS // ATT_KV_HEADS

    def attend_body(g, carry):
        key = key_ref[g]
        col = col0 + g * tk
        keep = (key > thr) | ((key == thr) & (col <= last_tie))
        bias = jnp.where(keep & (col <= row), 0.0, MASK_NEG)
        start = pl.multiple_of(g * tk, tk)
        for h in range(ATT_HEADS):
            kv = h // gsz
            s = dot(q_ref[:, h * ATT_HEAD_DIM:(h + 1) * ATT_HEAD_DIM],
                    kT_ref[0, g, kv * ATT_HEAD_DIM:(kv + 1) * ATT_HEAD_DIM, :]) + bias
            m_old = m_ref[h]
            m_new = jnp.maximum(m_old, jnp.max(s, axis=1, keepdims=True))
            p = jnp.exp(s - m_new).astype(vx_ref.dtype)
            pv = dot(p, vx_ref[0, pl.ds(start, tk), kv * 128:(kv + 1) * 128])
            acc_ref[h] = jnp.exp(m_old - m_new) * acc_ref[h] + pv
            m_ref[h] = m_new
        return carry

    lax.fori_loop(0, n_groups, attend_body, 0)

    for h in range(ATT_HEADS):
        a = acc_ref[h]
        o_ref[:, h * ATT_HEAD_DIM:(h + 1) * ATT_HEAD_DIM] = (
            a[:, :ATT_HEAD_DIM] / a[:, ATT_HEAD_DIM:ATT_HEAD_DIM + 1]).astype(o_ref.dtype)


def _dsa_attend(b_, t_, q, iq, iw, ikTb, kTb, vx):
    ng, tk = kTb.shape[1], kTb.shape[3]
    tq = Q_BLOCK
    nq = t_ // tq
    rows = lambda n: pl.BlockSpec((tq, n), lambda b, j: (b * nq + j, 0))
    return pl.pallas_call(
        functools.partial(_dsa_attend_kernel, topk=min(TOPK_MAX, t_ // 4), col_bits=max(1, (t_ - 1).bit_length())),
        grid=(b_, nq),
        in_specs=[rows(IDX_HEADS * IDX_DIM), rows(128),
                  pl.BlockSpec((1, ng, IDX_DIM, tk), lambda b, j: (b, 0, 0, 0)),
                  rows(ATT_Q_DIM),
                  pl.BlockSpec((1, ng, ATT_KV_DIM, tk), lambda b, j: (b, 0, 0, 0)),
                  pl.BlockSpec((1, t_, ATT_KV_HEADS * 128), lambda b, j: (b, 0, 0))],
        out_specs=rows(ATT_Q_DIM),
        out_shape=jax.ShapeDtypeStruct((b_ * t_, ATT_Q_DIM), MXU_DTYPE),
        scratch_shapes=[pltpu.VMEM((ng, tq, tk), jnp.int32),
                        pltpu.VMEM((ATT_HEADS, tq, 1), jnp.float32),
                        pltpu.VMEM((ATT_HEADS, tq, 128), jnp.float32)],
        compiler_params=pltpu.CompilerParams(
            dimension_semantics=("parallel", "arbitrary"),
            vmem_limit_bytes=V7X_VMEM_LIMIT_BYTES),
        name="dsa_attend",
    )(iq, iw, ikTb, q, kTb, vx.reshape(b_, t_, ATT_KV_HEADS * 128))


def _dsa_proj_q_lanes_kernel(x_ref, wqT_ref, wiqT_ref, wiwT_ref, wk_ref, wv_ref, wvxT_ref, wik_ref,
                             c_ref, s1_ref, s2_ref, cT_ref, sT_ref, kng_ref, knb_ref, onerow_ref,
                             qT_ref, iqT_ref, iwT_ref, k_ref, khd_ref, v_ref, vxT_ref, ik_ref, ikb_ref):
    xb = x_ref[...].astype(MXU_DTYPE)
    tm = xb.shape[0]
    c, s1, s2 = c_ref[...], s1_ref[...], s2_ref[...]
    cT, sT = cT_ref[...], sT_ref[...]
    dot = lambda a, b: jnp.dot(a, b, preferred_element_type=jnp.float32)
    dot_t = lambda w, a: lax.dot_general(w, a, (((1,), (1,)), ((), ())), preferred_element_type=jnp.float32)

    qT = _rope_rows(dot_t(wqT_ref[...], xb), cT, sT, ATT_HEAD_DIM, ROPE_DIM // 2) * (ATT_HEAD_DIM ** -0.5)
    iqT = _rope_rows(dot_t(wiqT_ref[...], xb), cT, sT, IDX_DIM, IDX_ROPE_DIM // 2)
    iwT = dot_t(wiwT_ref[...], xb) * (IDX_HEADS ** -0.5 * IDX_DIM ** -0.5)
    for t in range(tm // Q_BLOCK):
        lanes = slice(t * Q_BLOCK, (t + 1) * Q_BLOCK)
        qT_ref[0, t] = qT[:, lanes].astype(qT_ref.dtype)
        iqT_ref[0, t] = iqT[:, lanes].astype(iqT_ref.dtype)
        iwT_ref[0, t] = iwT[:, lanes]

    k = _rope_lanes(dot(xb, wk_ref[...]), c, s1, s2, ROPE_DIM // 2)
    k_ref[...] = k
    for g in range(ATT_KV_HEADS):
        khd_ref[g] = k[:, g * ATT_HEAD_DIM:(g + 1) * ATT_HEAD_DIM].astype(khd_ref.dtype)
    v_ref[...] = dot(xb, wv_ref[...])
    vxT_ref[0, 0] = (dot_t(wvxT_ref[...], xb) + onerow_ref[...]).astype(vxT_ref.dtype)

    ik = dot(xb, wik_ref[...])
    real = lax.broadcasted_iota(jnp.int32, ik.shape, 1) < IDX_DIM
    mu = jnp.sum(ik, axis=-1, keepdims=True) * (1.0 / IDX_DIM)
    ikc = jnp.where(real, ik - mu, 0.0)
    var = jnp.sum(ikc * ikc, axis=-1, keepdims=True) * (1.0 / IDX_DIM)
    ikn = _rope_lanes(ikc * lax.rsqrt(var + LN_EPS) * kng_ref[...] + knb_ref[...], c, s1, s2, IDX_ROPE_DIM // 2)
    ik_ref[...] = ikn[:, :IDX_DIM]
    ikb_ref[...] = ikn[:, :IDX_DIM].astype(ikb_ref.dtype)


def _dsa_project_q_lanes(x3d, pos, w_in, kn_g, kn_b):
    b_, t_, _ = x3d.shape
    tk = KEY_GROUP
    ng = t_ // tk
    nq = tk // Q_BLOCK
    m = b_ * t_
    w_q, w_k, w_v, w_iq, w_ik, w_iw = jnp.split(w_in, list(ATT_IN_SPLITS), axis=1)
    cast = lambda w: w.astype(MXU_DTYPE)
    w_vxT = jnp.pad(w_v.T.reshape(ATT_KV_HEADS, ATT_HEAD_DIM, D_MODEL),
                    ((0, 0), (0, 128 - ATT_HEAD_DIM), (0, 0))).reshape(ATT_KV_HEADS * 128, D_MODEL)
    one_row = jnp.tile((jnp.arange(128) == ATT_HEAD_DIM).astype(jnp.float32), ATT_KV_HEADS)[:, None]
    pad_lanes = lambda a: jnp.pad(a, ((0, 0), (0, 128 - a.shape[1])))
    c, s1, s2, cT, sT = _rope_lane_tables(pos, ROPE_DIM, ATT_HEAD_DIM)
    full = lambda shape: pl.BlockSpec(shape, lambda b, i: (0,) * len(shape))
    rows = lambda n: pl.BlockSpec((tk, n), lambda b, i: (b * ng + i, 0))
    ptab = lambda n: pl.BlockSpec((tk, n), lambda b, i: (i, 0))
    qtile = lambda n: pl.BlockSpec((1, nq, n, Q_BLOCK), lambda b, i: (b, i, 0, 0))
    sds = jax.ShapeDtypeStruct
    return pl.pallas_call(
        _dsa_proj_q_lanes_kernel,
        grid=(b_, ng),
        in_specs=[rows(D_MODEL), full((ATT_Q_DIM, D_MODEL)), full((IDX_HEADS * IDX_DIM, D_MODEL)),
                  full((IDX_HEADS, D_MODEL)), full((D_MODEL, ATT_KV_DIM)), full((D_MODEL, ATT_KV_DIM)),
                  full((ATT_KV_HEADS * 128, D_MODEL)), full((D_MODEL, 128)),
                  ptab(128), ptab(128), ptab(128),
                  pl.BlockSpec((ROPE_DIM // 2, tk), lambda b, i: (0, i)),
                  pl.BlockSpec((ROPE_DIM // 2, tk), lambda b, i: (0, i)),
                  full((1, 128)), full((1, 128)), full((ATT_KV_HEADS * 128, 1))],
        out_specs=[qtile(ATT_Q_DIM), qtile(IDX_HEADS * IDX_DIM), qtile(IDX_HEADS),
                   rows(ATT_KV_DIM), pl.BlockSpec((ATT_KV_HEADS, tk, ATT_HEAD_DIM), lambda b, i: (0, b * ng + i, 0)),
                   rows(ATT_KV_DIM), pl.BlockSpec((1, 1, ATT_KV_HEADS * 128, tk), lambda b, i: (b, i, 0, 0)),
                   rows(IDX_DIM), rows(IDX_DIM)],
        out_shape=[sds((b_, t_ // Q_BLOCK, ATT_Q_DIM, Q_BLOCK), MXU_DTYPE),
                   sds((b_, t_ // Q_BLOCK, IDX_HEADS * IDX_DIM, Q_BLOCK), MXU_DTYPE),
                   sds((b_, t_ // Q_BLOCK, IDX_HEADS, Q_BLOCK), jnp.float32),
                   sds((m, ATT_KV_DIM), jnp.float32), sds((ATT_KV_HEADS, m, ATT_HEAD_DIM), MXU_DTYPE),
                   sds((m, ATT_KV_DIM), jnp.float32), sds((b_, ng, ATT_KV_HEADS * 128, tk), MXU_DTYPE),
                   sds((m, IDX_DIM), jnp.float32), sds((m, IDX_DIM), MXU_DTYPE)],
        compiler_params=pltpu.CompilerParams(
            dimension_semantics=("parallel", "parallel"),
            vmem_limit_bytes=V7X_VMEM_LIMIT_BYTES),
        name="dsa_project_q_lanes",
    )(x3d.reshape(m, D_MODEL), cast(w_q.T), cast(w_iq.T), cast(w_iw.T), cast(w_k), cast(w_v), cast(w_vxT),
      cast(pad_lanes(w_ik)), c, s1, s2, cT, sT, pad_lanes(kn_g.reshape(1, IDX_DIM)),
      pad_lanes(kn_b.reshape(1, IDX_DIM)), one_row)


def _tree_sum(parts):
    while len(parts) > 1:
        parts = [parts[i] + parts[i + 1] for i in range(0, len(parts) - 1, 2)] + (
            [parts[-1]] if len(parts) % 2 else [])
    return parts[0]


def _dsa_attend_q_lanes_kernel(iqT_ref, iwT_ref, ik_ref, qT_ref, k_ref, vxT_ref, o_ref,
                               key_ref, bias_ref, m_ref, acc_ref, *, topk, col_bits):
    j = pl.program_id(1)
    tk, tq = key_ref.shape[1], key_ref.shape[2]
    n_groups = (j * tq + tq + tk - 1) // tk
    qpos = j * tq + lax.broadcasted_iota(jnp.int32, (tk, tq), 1)
    kpos0 = lax.broadcasted_iota(jnp.int32, (tk, tq), 0)
    dot = lambda a, b: jnp.dot(a, b, preferred_element_type=jnp.float32)

    def score_body(g, carry):
        start = pl.multiple_of(g * tk, tk)
        w_iq = jnp.concatenate([iqT_ref[0, 0, h * IDX_DIM:(h + 1) * IDX_DIM, :] for h in range(IDX_HEADS)], axis=1)
        s_all = dot(ik_ref[0, pl.ds(start, tk), :], w_iq)
        sc = _tree_sum([iwT_ref[0, 0, h:h + 1, :] * jnp.maximum(s_all[:, h * tq:(h + 1) * tq], 0.0)
                        for h in range(IDX_HEADS)])
        key_ref[g] = jnp.where(kpos0 + g * tk <= qpos, _sortable_key(sc), jnp.int32(INT32_MIN))
        return carry

    lax.fori_loop(0, n_groups, score_body, 0)

    def count_keys(pred):
        def body(g, part):
            hit = jnp.where(pred(key_ref[g], kpos0 + g * tk), 1.0, 0.0)
            return part + _tree_sum([hit[r * SUBLANES:(r + 1) * SUBLANES] for r in range(tk // SUBLANES)])
        part = lax.fori_loop(0, n_groups, body, jnp.zeros((SUBLANES, tq), jnp.float32))
        return jnp.sum(part, axis=0, keepdims=True)

    def bit_body(i, thr):
        cand = thr ^ lax.shift_left(jnp.int32(1), jnp.int32(31) - i)
        return jnp.where(count_keys(lambda k, kp: k >= cand) >= float(topk), cand, thr)

    thr = lax.fori_loop(0, 32, bit_body, jnp.full((1, tq), INT32_MIN, jnp.int32))

    need = float(topk) - count_keys(lambda k, kp: k > thr)

    def pos_body(i, last):
        cand = last | lax.shift_left(jnp.int32(1), jnp.int32(col_bits - 1) - i)
        return jnp.where(count_keys(lambda k, kp: (k == thr) & (kp < cand)) < need, cand, last)

    n_tied = count_keys(lambda k, kp: k == thr)
    excess = jnp.max(jnp.where(n_tied > need, 1.0, 0.0), axis=1, keepdims=True)
    last_tie = lax.cond(excess[0, 0] > 0.0,
                        lambda: lax.fori_loop(0, col_bits, pos_body, jnp.zeros((1, tq), jnp.int32)),
                        lambda: jnp.full((1, tq), 2 ** col_bits - 1, jnp.int32))

    m_ref[...] = jnp.full(m_ref.shape, MASK_NEG, jnp.float32)
    acc_ref[...] = jnp.zeros(acc_ref.shape, jnp.float32)
    gsz = ATT_HEADS // ATT_KV_HEADS

    def attend_body(g, carry):
        start = pl.multiple_of(g * tk, tk)
        key = key_ref[g]
        kpos = kpos0 + g * tk
        keep = (key > thr) | ((key == thr) & (kpos <= last_tie))
        bias_ref[...] = jnp.where(keep & (kpos <= qpos), 0.0, MASK_NEG)
        logits = []
        for kv in range(ATT_KV_HEADS):
            w_q = jnp.concatenate([qT_ref[0, 0, (kv * gsz + i) * ATT_HEAD_DIM:(kv * gsz + i + 1) * ATT_HEAD_DIM, :]
                                   for i in range(gsz)], axis=1)
            logits.append(dot(k_ref[kv, pl.ds(start, tk), :], w_q))
        for kv in range(ATT_KV_HEADS):
            s = logits[kv] + jnp.concatenate([bias_ref[...]] * gsz, axis=1)
            m_old = m_ref[kv]
            m_new = jnp.maximum(m_old, jnp.max(s, axis=0, keepdims=True))
            p = jnp.exp(s - m_new).astype(vxT_ref.dtype)
            pv = dot(vxT_ref[0, g, kv * 128:(kv + 1) * 128, :], p)
            acc_ref[kv] = jnp.exp(m_old - m_new) * acc_ref[kv] + pv
            m_ref[kv] = m_new
        return carry

    lax.fori_loop(0, n_groups, attend_body, 0)

    for h in range(ATT_HEADS):
        a = acc_ref[h // gsz, :, (h % gsz) * tq:(h % gsz + 1) * tq]
        o = (a / a[ATT_HEAD_DIM:ATT_HEAD_DIM + 1, :]).T
        o_ref[:, h * ATT_HEAD_DIM:(h + 1) * ATT_HEAD_DIM] = o[:, :ATT_HEAD_DIM].astype(o_ref.dtype)


def _dsa_attend_q_lanes(b_, t_, qT, iqT, iwT, ikb, khd, vxT):
    ng, tk = vxT.shape[1], vxT.shape[3]
    tq = Q_BLOCK
    nq = t_ // tq
    qtile = lambda n: pl.BlockSpec((1, 1, n, tq), lambda b, j: (b, j, 0, 0))
    return pl.pallas_call(
        functools.partial(_dsa_attend_q_lanes_kernel, topk=min(TOPK_MAX, t_ // 4),
                          col_bits=max(1, (t_ - 1).bit_length())),
        grid=(b_, nq),
        in_specs=[qtile(IDX_HEADS * IDX_DIM), qtile(IDX_HEADS),
                  pl.BlockSpec((1, t_, IDX_DIM), lambda b, j: (b, 0, 0)),
                  qtile(ATT_Q_DIM),
                  pl.BlockSpec((ATT_KV_HEADS, t_, ATT_HEAD_DIM), lambda b, j: (0, b, 0)),
                  pl.BlockSpec((1, ng, ATT_KV_HEADS * 128, tk), lambda b, j: (b, 0, 0, 0))],
        out_specs=pl.BlockSpec((tq, ATT_Q_DIM), lambda b, j: (b * nq + j, 0)),
        out_shape=jax.ShapeDtypeStruct((b_ * t_, ATT_Q_DIM), MXU_DTYPE),
        scratch_shapes=[pltpu.VMEM((ng, tk, tq), jnp.int32),
                        pltpu.VMEM((tk, tq), jnp.float32),
                        pltpu.VMEM((ATT_KV_HEADS, 1, tq * (ATT_HEADS // ATT_KV_HEADS)), jnp.float32),
                        pltpu.VMEM((ATT_KV_HEADS, 128, tq * (ATT_HEADS // ATT_KV_HEADS)), jnp.float32)],
        compiler_params=pltpu.CompilerParams(
            dimension_semantics=("parallel", "arbitrary"),
            vmem_limit_bytes=V7X_VMEM_LIMIT_BYTES),
        name="dsa_attend_q_lanes",
    )(iqT, iwT, ikb.reshape(b_, t_, IDX_DIM), qT, khd, vxT)


def _dsa_prompt_pallas(x3d, w_in, kn_g, kn_b):
    b_, t_, _ = x3d.shape
    qT, iqT, iwT, k, khd, v, vxT, ik, ikb = _dsa_project_q_lanes(x3d, jnp.arange(t_), w_in, kn_g, kn_b)
    o = _dsa_attend_q_lanes(b_, t_, qT, iqT, iwT, ikb, khd, vxT)
    kv4 = lambda u: u.reshape(b_, t_, ATT_KV_HEADS, ATT_HEAD_DIM)
    return o, kv4(k), kv4(v), ik.reshape(b_, t_, IDX_DIM)


RW_ROW_TILE = 256


RW_PAIRS = RW_HEADS // 2
RW_PAIR_LANES = 2 * RW_HEAD


def _rwkv_project_rows(x, xp, mu_ref, wr_ref, wk_ref, wv_ref, w1_ref, w2_ref, a1_ref, a2_ref,
                       g1_ref, g2_ref, w0_ref, a0_ref):
    dx = xp - x
    mix = lambda c: (x + dx * mu_ref[c:c + 1, :]).astype(MXU_DTYPE)
    dot = lambda a, b: jnp.dot(a.astype(MXU_DTYPE), b, preferred_element_type=jnp.float32)
    r = dot(mix(0), wr_ref[...])
    lora_w = dot(jnp.tanh(dot(mix(1), w1_ref[...])), w2_ref[...])
    w_log = -jax.nn.softplus(-(w0_ref[...] + lora_w)) - 0.5
    d = jnp.exp(-jnp.exp(w_log))
    k = dot(mix(2), wk_ref[...])
    v = dot(mix(3), wv_ref[...])
    a = jax.nn.sigmoid(a0_ref[...] + dot(dot(mix(4), a1_ref[...]), a2_ref[...]))
    g = dot(jax.nn.sigmoid(dot(mix(5), g1_ref[...])), g2_ref[...])
    return r, d, k, v, a, g


def _rwkv_proj_step_kernel(x_ref, xp_ref, *refs):
    vals = _rwkv_project_rows(x_ref[...], xp_ref[...], *refs[:12])
    for ref, val in zip(refs[12:], vals):
        ref[...] = val


def _rwkv_proj_seq_kernel(x_ref, halo_ref, shift_ref, *refs):
    i = pl.program_id(1)
    x = x_ref[...]
    prev = jnp.where(i == 0, shift_ref[0], halo_ref[...])[SUBLANES - 1:SUBLANES, :]
    first = lax.broadcasted_iota(jnp.int32, (x.shape[0], 1), 0) == 0
    xp = jnp.where(first, prev, pltpu.roll(x, 1, 0))
    vals = _rwkv_project_rows(x, xp, *refs[:12])
    for ref, val in zip(refs[12:], vals):
        ref[...] = val


def _rwkv_consts(mu, w_r, w_k, w_v, w0, w1, w2, a0, a1, a2, g1, g2):
    cast = lambda w: w.astype(MXU_DTYPE)
    return [mu, cast(w_r), cast(w_k), cast(w_v), cast(w1), cast(w2), cast(a1), cast(a2), cast(g1), cast(g2),
            w0.reshape(1, D_MODEL), a0.reshape(1, D_MODEL)]


def _rwkv_project_step(x2d, xprev2d, *params):
    m = x2d.shape[0]
    consts = _rwkv_consts(*params)
    full = lambda a: pl.BlockSpec(a.shape, lambda i: (0,) * a.ndim)
    rows = pl.BlockSpec((m, D_MODEL), lambda i: (0, 0))
    return pl.pallas_call(
        _rwkv_proj_step_kernel,
        grid=(1,),
        in_specs=[rows, rows] + [full(a) for a in consts],
        out_specs=[rows] * 6,
        out_shape=[jax.ShapeDtypeStruct((m, D_MODEL), jnp.float32)] * 6,
        compiler_params=pltpu.CompilerParams(
            dimension_semantics=("arbitrary",),
            vmem_limit_bytes=V7X_VMEM_LIMIT_BYTES),
        name="rwkv_project_step",
    )(x2d, xprev2d, *consts)


def _rwkv_project_seq(x3d, shift, *params):
    b_, t_, _ = x3d.shape
    m = b_ * t_
    tm = RW_ROW_TILE
    nt = t_ // tm
    consts = _rwkv_consts(*params)
    full = lambda a: pl.BlockSpec(a.shape, lambda b, i: (0,) * a.ndim)
    rows = pl.BlockSpec((tm, D_MODEL), lambda b, i: (b * nt + i, 0))
    halo = pl.BlockSpec((SUBLANES, D_MODEL), lambda b, i: (jnp.maximum((b * nt + i) * (tm // SUBLANES) - 1, 0), 0))
    shift8 = jnp.pad(shift[:, None, :], ((0, 0), (SUBLANES - 1, 0), (0, 0)))
    x2d = x3d.reshape(m, D_MODEL)
    return pl.pallas_call(
        _rwkv_proj_seq_kernel,
        grid=(b_, nt),
        in_specs=[rows, halo, pl.BlockSpec((1, SUBLANES, D_MODEL), lambda b, i: (b, 0, 0))]
        + [full(a) for a in consts],
        out_specs=[rows] * 6,
        out_shape=[jax.ShapeDtypeStruct((m, D_MODEL), jnp.float32)] * 6,
        compiler_params=pltpu.CompilerParams(
            dimension_semantics=("parallel", "parallel"),
            vmem_limit_bytes=V7X_VMEM_LIMIT_BYTES),
        name="rwkv_project_seq",
    )(x2d, x2d, shift8, *consts)


RW_LANES = 128
RW_TIME_CHUNK = 32


def _rwkv_scan_kernel(r_ref, d_ref, k_ref, v_ref, a_ref, s0_ref, kk_ref, ka_ref, rk_ref, gg_ref, gb_ref,
                      z_ref, s_out_ref, s_ref, vec_ref):
    c = pl.program_id(1)
    n = RW_HEAD
    tc = r_ref.shape[1]
    low_half = lax.broadcasted_iota(jnp.int32, (n, RW_LANES), 1) < n

    @pl.when(c == 0)
    def _():
        s_ref[...] = s0_ref[...]

    def swap_layout(x):
        xt = jnp.concatenate([x, x], axis=0).T
        return jnp.where(low_half, xt[:n], xt[n:])

    def load_step(ref, t):
        rows = ref[:, t, :]
        return swap_layout(jnp.concatenate(
            [rows[:, p * RW_PAIR_LANES:(p + 1) * RW_PAIR_LANES] for p in range(RW_PAIRS)], axis=0))

    def store_step(ref, t, val):
        tile = swap_layout(val)
        ref[:, t, :] = jnp.concatenate(
            [tile[p * RW_SEQ_PER_TILE:(p + 1) * RW_SEQ_PER_TILE] for p in range(RW_PAIRS)], axis=1)

    def prepare(t, slot):
        r, k, a = load_step(r_ref, t), load_step(k_ref, t), load_step(a_ref, t)
        kkr = k * kk_ref[...]
        nrm = jnp.sqrt(jnp.sum(kkr * kkr, axis=0, keepdims=True))
        kk = kkr / jnp.maximum(nrm, 1e-12)
        vec_ref[slot, 0] = kk
        vec_ref[slot, 1] = load_step(d_ref, t)
        vec_ref[slot, 2] = kk * a
        vec_ref[slot, 3] = k * (1.0 + (a - 1.0) * ka_ref[...])
        vec_ref[slot, 4] = r
        vec_ref[slot, 5] = load_step(v_ref, t)

    def step(t, slot):
        row = lambda q, j: vec_ref[slot, q, j:j + 1, :]
        v = vec_ref[slot, 5]
        lanes = 4
        sa_parts = [s_ref[j] * row(0, j) for j in range(lanes)]
        for j in range(lanes, n):
            sa_parts[j % lanes] = sa_parts[j % lanes] + s_ref[j] * row(0, j)
        sa = _tree_sum(sa_parts)
        y_parts = []
        for j in range(n):
            sn = s_ref[j] * row(1, j) - sa * row(2, j) + v * row(3, j)
            s_ref[j] = sn
            if j < lanes:
                y_parts.append(sn * row(4, j))
            else:
                y_parts[j % lanes] = y_parts[j % lanes] + sn * row(4, j)
        y = _tree_sum(y_parts)
        mu = jnp.mean(y, axis=0, keepdims=True)
        yc = y - mu
        var = jnp.mean(yc * yc, axis=0, keepdims=True)
        bonus = jnp.sum(vec_ref[slot, 4] * vec_ref[slot, 3] * rk_ref[...], axis=0, keepdims=True)
        store_step(z_ref, t, yc * lax.rsqrt(var + RW_GN_EPS) * gg_ref[...] + gb_ref[...] + bonus * v)

    prepare(0, 0)
    if tc == 1:
        step(0, 0)
    else:
        def two_steps(i, carry):
            t = 2 * i
            prepare(t + 1, 1)
            step(t, 0)
            prepare(jnp.minimum(t + 2, tc - 1), 0)
            step(t + 1, 1)
            return carry

        lax.fori_loop(0, tc // 2, two_steps, 0)

    @pl.when(c == pl.num_programs(1) - 1)
    def _():
        s_out_ref[...] = s_ref[...]


RW_SEQ_PER_TILE = RW_LANES // RW_HEADS


def _rwkv_lane_heads():
    half = jnp.arange(2)[:, None, None]
    pair = jnp.arange(RW_PAIRS)[None, :, None]
    return jnp.broadcast_to(2 * pair + half, (2, RW_PAIRS, RW_SEQ_PER_TILE)).reshape(RW_LANES)


def _rwkv_scan(r, d, k, v, a, s0, k_k, k_a, r_k, gn_g, gn_b):
    b_, t_, _ = r.shape
    n = RW_HEAD
    tc = RW_TIME_CHUNK if t_ % RW_TIME_CHUNK == 0 else t_
    table = lambda p: p.reshape(RW_HEADS, n)[_rwkv_lane_heads()].T
    seq = pl.BlockSpec((RW_SEQ_PER_TILE, tc, D_MODEL), lambda l, c: (l, c, 0))
    state = pl.BlockSpec((n, n, RW_LANES), lambda l, c: (0, 0, l))
    tab = pl.BlockSpec((n, RW_LANES), lambda l, c: (0, 0))
    return pl.pallas_call(
        _rwkv_scan_kernel,
        grid=(b_ // RW_SEQ_PER_TILE, t_ // tc),
        in_specs=[seq] * 5 + [state] + [tab] * 5,
        out_specs=[seq, state],
        out_shape=[jax.ShapeDtypeStruct(r.shape, jnp.float32),
                   jax.ShapeDtypeStruct(s0.shape, jnp.float32)],
        scratch_shapes=[pltpu.VMEM((n, n, RW_LANES), jnp.float32),
                        pltpu.VMEM((2, 6, n, RW_LANES), jnp.float32)],
        compiler_params=pltpu.CompilerParams(
            dimension_semantics=("parallel", "arbitrary"),
            vmem_limit_bytes=V7X_VMEM_LIMIT_BYTES),
        name="rwkv_scan",
    )(r, d, k, v, a, s0, table(k_k), table(k_a), table(r_k), table(gn_g), table(gn_b))


def _rwkv_state_to_lanes(wkv):
    b_ = wkv.shape[0]
    w = wkv.astype(jnp.float32).reshape(b_ // RW_SEQ_PER_TILE, RW_SEQ_PER_TILE, RW_PAIRS, 2, RW_HEAD, RW_HEAD)
    return jnp.transpose(w, (5, 4, 0, 3, 2, 1)).reshape(RW_HEAD, RW_HEAD, b_ * RW_HEADS)


def _rwkv_state_from_lanes(s, b_):
    w = s.reshape(RW_HEAD, RW_HEAD, b_ // RW_SEQ_PER_TILE, 2, RW_PAIRS, RW_SEQ_PER_TILE)
    return jnp.transpose(w, (2, 5, 4, 3, 1, 0)).reshape(b_, RW_HEADS, RW_HEAD, RW_HEAD)


def _rwkv7_mixer_pallas(x3d, shift, wkv, mu, w_r, w_k, w_v, w0, w1, w2, a0, a1, a2, g1, g2,
                        k_k, k_a, r_k, gn_g, gn_b):
    b_, t_, _ = x3d.shape
    params = (mu, w_r, w_k, w_v, w0, w1, w2, a0, a1, a2, g1, g2)
    if t_ == 1:
        *seqs, g = _rwkv_project_step(x3d.reshape(b_, D_MODEL), shift, *params)
    else:
        *seqs, g = _rwkv_project_seq(x3d, shift, *params)
    seqs = [u.reshape(b_, t_, D_MODEL) for u in seqs]
    z, s = _rwkv_scan(*seqs, _rwkv_state_to_lanes(wkv), k_k, k_a, r_k, gn_g, gn_b)
    return z.reshape(b_ * t_, D_MODEL), g, x3d[:, -1], _rwkv_state_from_lanes(s, b_).astype(wkv.dtype)


def _proj_gate_ln_kernel(x_ref, h_ref, gate_ref, w_ref, g_ref, b_ref, o_ref):
    h = (h_ref[...] * gate_ref[...]).astype(MXU_DTYPE)
    y = ALPHA * x_ref[...] + jnp.dot(h, w_ref[...], preferred_element_type=jnp.float32)
    o_ref[...] = _ln_rows(y, g_ref[...], b_ref[...])


def _proj_gate_post_norm(x2d, h2d, gate2d, w_out, g, b):
    m = x2d.shape[0]
    tm = _row_tile(m)
    rows = pl.BlockSpec((tm, D_MODEL), lambda i: (i, 0))
    vec = pl.BlockSpec((1, D_MODEL), lambda i: (0, 0))
    return pl.pallas_call(
        _proj_gate_ln_kernel,
        grid=(m // tm,),
        in_specs=[rows, rows, rows, pl.BlockSpec((D_MODEL, D_MODEL), lambda i: (0, 0)), vec, vec],
        out_specs=rows,
        out_shape=jax.ShapeDtypeStruct((m, D_MODEL), jnp.float32),
        compiler_params=pltpu.CompilerParams(
            dimension_semantics=("parallel",),
            vmem_limit_bytes=V7X_VMEM_LIMIT_BYTES),
        name="proj_gate_post_norm",
    )(x2d, h2d, gate2d, w_out, g.reshape(1, D_MODEL), b.reshape(1, D_MODEL))


GM_ROW_TILE = 256


def _gmlp_kernel(x_ref, win_ref, lng_ref, lnb_ref, mixw_ref, mixb_ref, wout_ref, g_ref, b_ref, *out_refs,
                 chunk_len, emit_v):
    x = x_ref[...]
    h = jax.nn.gelu(jnp.dot(x.astype(MXU_DTYPE), win_ref[...], preferred_element_type=jnp.float32))
    u = h[:, :GM_WIDTH]
    v = _ln_rows(h[:, GM_WIDTH:], lng_ref[...], lnb_ref[...])
    if emit_v:
        out_refs[1][...] = v
    if chunk_len == 1:
        gated = u * (v * mixw_ref[...] + mixb_ref[...])
    else:
        tm = x.shape[0]
        causal = (lax.broadcasted_iota(jnp.int32, (chunk_len, chunk_len), 0)
                  >= lax.broadcasted_iota(jnp.int32, (chunk_len, chunk_len), 1))
        vb = v.astype(MXU_DTYPE)
        cols = []
        for g in range(GM_GROUPS):
            w = jnp.where(causal, mixw_ref[g], 0.0).astype(MXU_DTYPE)
            bias = mixb_ref[:, g:g + 1]
            lanes = slice(g * GM_GROUP_DIM, (g + 1) * GM_GROUP_DIM)
            rows = [jnp.dot(w, vb[c * chunk_len:(c + 1) * chunk_len, lanes],
                            preferred_element_type=jnp.float32) + bias
                    for c in range(tm // chunk_len)]
            cols.append(jnp.concatenate(rows, axis=0))
        gated = u * jnp.concatenate(cols, axis=1)
    y = ALPHA * x + jnp.dot(gated.astype(MXU_DTYPE), wout_ref[...], preferred_element_type=jnp.float32)
    out_refs[0][...] = _ln_rows(y, g_ref[...], b_ref[...])


def _gmlp_block(x2d, seq_len, w_in, ln_g, ln_b, ws, bs, w_out, g, b, emit_v):
    m = x2d.shape[0]
    chunk_len = min(seq_len, CHUNK)
    if chunk_len == 1:
        tm = m
        mixw = jnp.repeat(ws[:, 0, 0], GM_GROUP_DIM)[None, :]
        mixb = jnp.repeat(bs[:, 0], GM_GROUP_DIM)[None, :]
    else:
        tm = GM_ROW_TILE
        mixw = ws[:, :chunk_len, :chunk_len]
        mixb = bs[:, :chunk_len].T
    full = lambda a: pl.BlockSpec(a.shape, lambda i: (0,) * a.ndim)
    rows = lambda n: pl.BlockSpec((tm, n), lambda i: (i, 0))
    consts = [w_in.astype(MXU_DTYPE), ln_g.reshape(1, GM_WIDTH), ln_b.reshape(1, GM_WIDTH), mixw, mixb,
              w_out.astype(MXU_DTYPE), g.reshape(1, D_MODEL), b.reshape(1, D_MODEL)]
    out_specs = [rows(D_MODEL)] + ([rows(GM_WIDTH)] if emit_v else [])
    out_shape = [jax.ShapeDtypeStruct((m, D_MODEL), jnp.float32)] + (
        [jax.ShapeDtypeStruct((m, GM_WIDTH), jnp.float32)] if emit_v else [])
    return pl.pallas_call(
        functools.partial(_gmlp_kernel, chunk_len=chunk_len, emit_v=emit_v),
        grid=(m // tm,),
        in_specs=[rows(D_MODEL)] + [full(a) for a in consts],
        out_specs=out_specs,
        out_shape=out_shape,
        compiler_params=pltpu.CompilerParams(
            dimension_semantics=("parallel",),
            vmem_limit_bytes=V7X_VMEM_LIMIT_BYTES),
        name="gmlp_block",
    )(x2d, *consts)


SSM_ROW_TILE = 256
SSM_BC_DIM = SSM_GROUPS * SSM_STATE
SSM_DT_LANES = 128
SUBLANES = 8


def _ssm_activate(xb, xbc, taps, wz_ref, wdt_ref, cw_ref, cb_ref, dtb_ref, z_ref, xs_ref, bm_ref, cm_ref, dt_ref):
    conv = cb_ref[...] + xbc * cw_ref[SSM_CONV - 1:SSM_CONV, :]
    for j in range(SSM_CONV - 1):
        conv = conv + taps[j] * cw_ref[j:j + 1, :]
    act = conv * jax.nn.sigmoid(conv)
    xs_ref[...] = act[:, :SSM_D_INNER]
    bm_ref[...] = act[:, SSM_D_INNER:SSM_D_INNER + SSM_BC_DIM].astype(bm_ref.dtype)
    cm_ref[...] = act[:, SSM_D_INNER + SSM_BC_DIM:].astype(cm_ref.dtype)
    z_ref[...] = jnp.dot(xb, wz_ref[...], preferred_element_type=jnp.float32)
    dt_ref[...] = jax.nn.softplus(jnp.dot(xb, wdt_ref[...], preferred_element_type=jnp.float32) + dtb_ref[...])


def _ssm_proj_seq_kernel(x_ref, halo_ref, cs_ref, wx_ref, wz_ref, wdt_ref, cw_ref, cb_ref, dtb_ref,
                         z_ref, xs_ref, bm_ref, cm_ref, dt_ref, tail_ref):
    i = pl.program_id(1)
    xb = x_ref[...].astype(MXU_DTYPE)
    xbc = jnp.dot(xb, wx_ref[...], preferred_element_type=jnp.float32)
    tm = xbc.shape[0]
    prev = jnp.dot(halo_ref[...].astype(MXU_DTYPE), wx_ref[...], preferred_element_type=jnp.float32)
    prev = jnp.where(i == 0, cs_ref[0], prev)
    row = lax.broadcasted_iota(jnp.int32, (tm, 1), 0)
    pad = jnp.zeros((tm - SUBLANES, xbc.shape[1]), jnp.float32)
    taps = []
    for j in range(SSM_CONV - 1):
        back = SSM_CONV - 1 - j
        head = jnp.concatenate([pltpu.roll(prev, back, 0), pad], axis=0)
        taps.append(jnp.where(row < back, head, pltpu.roll(xbc, back, 0)))
    _ssm_activate(xb, xbc, taps, wz_ref, wdt_ref, cw_ref, cb_ref, dtb_ref, z_ref, xs_ref, bm_ref, cm_ref, dt_ref)
    tail_ref[0] = xbc[tm - SUBLANES:, :]


def _ssm_proj_step_kernel(x_ref, st_ref, wx_ref, wz_ref, wdt_ref, cw_ref, cb_ref, dtb_ref,
                          z_ref, xs_ref, bm_ref, cm_ref, dt_ref, st_out_ref):
    xb = x_ref[...].astype(MXU_DTYPE)
    xbc = jnp.dot(xb, wx_ref[...], preferred_element_type=jnp.float32)
    taps = [st_ref[j] for j in range(SSM_CONV - 1)]
    _ssm_activate(xb, xbc, taps, wz_ref, wdt_ref, cw_ref, cb_ref, dtb_ref, z_ref, xs_ref, bm_ref, cm_ref, dt_ref)
    for j in range(SSM_CONV - 2):
        st_out_ref[j] = st_ref[j + 1]
    st_out_ref[SSM_CONV - 2] = xbc


def _ssm_project(x3d, conv_state, w_in, conv_w, conv_b, dt_bias):
    b_, t_, _ = x3d.shape
    m = b_ * t_
    w_z, w_x, w_dt = jnp.split(w_in, [SSM_D_INNER, SSM_D_INNER + SSM_CONV_DIM], axis=1)
    cast = lambda w: w.astype(MXU_DTYPE)
    consts = [cast(w_x), cast(w_z), cast(jnp.pad(w_dt, ((0, 0), (0, SSM_DT_LANES - SSM_HEADS)))),
              conv_w, conv_b.reshape(1, SSM_CONV_DIM),
              jnp.pad(dt_bias, (0, SSM_DT_LANES - SSM_HEADS)).reshape(1, SSM_DT_LANES)]
    sds = jax.ShapeDtypeStruct
    outs = [sds((m, SSM_D_INNER), jnp.float32), sds((m, SSM_D_INNER), jnp.float32),
            sds((m, SSM_BC_DIM), MXU_DTYPE), sds((m, SSM_BC_DIM), MXU_DTYPE), sds((m, SSM_DT_LANES), jnp.float32)]
    widths = [SSM_D_INNER, SSM_D_INNER, SSM_BC_DIM, SSM_BC_DIM, SSM_DT_LANES]
    params = dict(vmem_limit_bytes=V7X_VMEM_LIMIT_BYTES)
    x2d = x3d.reshape(m, D_MODEL)
    if t_ == 1:
        full = lambda a: pl.BlockSpec(a.shape, lambda i: (0,) * a.ndim)
        st = jnp.transpose(conv_state, (1, 0, 2))
        res = pl.pallas_call(
            _ssm_proj_step_kernel,
            grid=(1,),
            in_specs=[full(x2d), full(st)] + [full(a) for a in consts],
            out_specs=[pl.BlockSpec((m, w), lambda i: (0, 0)) for w in widths] + [full(st)],
            out_shape=outs + [sds(st.shape, jnp.float32)],
            compiler_params=pltpu.CompilerParams(dimension_semantics=("arbitrary",), **params),
            name="ssm_project_step",
        )(x2d, st, *consts)
        return list(res[:5]) + [jnp.transpose(res[5], (1, 0, 2))]
    tm = SSM_ROW_TILE
    nt = t_ // tm
    full = lambda a: pl.BlockSpec(a.shape, lambda b, i: (0,) * a.ndim)
    rows = lambda w: pl.BlockSpec((tm, w), lambda b, i: (b * nt + i, 0))
    halo = pl.BlockSpec((SUBLANES, D_MODEL), lambda b, i: (jnp.maximum((b * nt + i) * (tm // SUBLANES) - 1, 0), 0))
    cs8 = jnp.pad(conv_state, ((0, 0), (SUBLANES - (SSM_CONV - 1), 0), (0, 0)))
    tail = pl.BlockSpec((1, SUBLANES, SSM_CONV_DIM), lambda b, i: (b, 0, 0))
    res = pl.pallas_call(
        _ssm_proj_seq_kernel,
        grid=(b_, nt),
        in_specs=[rows(D_MODEL), halo, tail] + [full(a) for a in consts],
        out_specs=[rows(w) for w in widths] + [tail],
        out_shape=outs + [sds((b_, SUBLANES, SSM_CONV_DIM), jnp.float32)],
        compiler_params=pltpu.CompilerParams(dimension_semantics=("parallel", "arbitrary"), **params),
        name="ssm_project_seq",
    )(x2d, x2d, cs8, *consts)
    return list(res[:5]) + [res[5][:, SUBLANES - (SSM_CONV - 1):, :]]


def _ssm_gate_norm(y, xs, z, dskip, normg):
    yg = (y + xs * dskip) * (z * jax.nn.sigmoid(z))
    gw = SSM_D_INNER // SSM_GROUPS
    outs = []
    for g in range(SSM_GROUPS):
        part = yg[:, g * gw:(g + 1) * gw]
        ms = jnp.mean(part * part, axis=-1, keepdims=True)
        outs.append(part * lax.rsqrt(ms + LN_EPS))
    return jnp.concatenate(outs, axis=1) * normg


def _ssm_chunk_kernel(xs_ref, bm_ref, cm_ref, dt_ref, z_ref, aneg_ref, dskip_ref, normg_ref,
                      yg_ref, h_out_ref, h_ref, yT_ref, xe_ref):
    c = pl.program_id(1)
    l = xs_ref.shape[0]
    hd = SSM_HEAD_DIM

    @pl.when(c == 0)
    def _():
        h_ref[...] = jnp.zeros_like(h_ref)

    dot = lambda u, w: jnp.dot(u, w, preferred_element_type=jnp.float32)
    dt = dt_ref[...]
    a = dt * aneg_ref[...]
    r_i = lax.broadcasted_iota(jnp.int32, (l, l), 0)
    c_i = lax.broadcasted_iota(jnp.int32, (l, l), 1)
    tril = jnp.where(r_i >= c_i, 1.0, 0.0)
    hi = lax.Precision.HIGHEST
    acum = jnp.dot(tril, a, precision=hi, preferred_element_type=jnp.float32)
    acum_t = jnp.dot(a.T, tril.T, precision=hi, preferred_element_type=jnp.float32)
    dt_t = dt.T
    to_end_t = jnp.exp(acum_t[:, l - 1:l] - acum_t)
    from_start_t = jnp.exp(acum_t)
    chunk_decay = jnp.exp(acum[l - 1:l, :])
    upper = r_i <= c_i
    xs = xs_ref[...]
    for g in range(SSM_GROUPS):
        bm = bm_ref[:, g * SSM_STATE:(g + 1) * SSM_STATE]
        cm_t = cm_ref[:, g * SSM_STATE:(g + 1) * SSM_STATE].astype(jnp.float32).T.astype(MXU_DTYPE)
        cb_t = dot(bm, cm_t)
        h_in = h_ref[g * SSM_HPG:(g + 1) * SSM_HPG].reshape(SSM_HPG * hd, SSM_STATE)
        y_off = dot(h_in.astype(MXU_DTYPE), cm_t)
        for e in range(SSM_HPG):
            h = g * SSM_HPG + e
            if h % 2 == 0:
                xs_pair_t = xs[:, h * hd:(h + 2) * hd].T
            xdt_t = xs_pair_t[(h % 2) * hd:(h % 2 + 1) * hd] * dt_t[h:h + 1, :]
            seg = jnp.exp(jnp.where(upper, acum_t[h:h + 1, :] - acum[:, h:h + 1], -jnp.inf))
            y_diag = dot(xdt_t.astype(MXU_DTYPE), (cb_t * seg).astype(MXU_DTYPE))
            yT_ref[h * hd:(h + 1) * hd, :] = y_diag + y_off[e * hd:(e + 1) * hd] * from_start_t[h:h + 1, :]
            xe_ref[e * hd:(e + 1) * hd, :] = (xdt_t * to_end_t[h:h + 1, :]).astype(xe_ref.dtype)
        states = dot(xe_ref[...], bm)
        for e in range(SSM_HPG):
            h = g * SSM_HPG + e
            h_ref[h] = h_ref[h] * chunk_decay[:, h:h + 1] + states[e * hd:(e + 1) * hd]
    y = jnp.concatenate([yT_ref[i * l:(i + 1) * l, :].T for i in range(SSM_D_INNER // l)], axis=1)
    yg_ref[...] = _ssm_gate_norm(y, xs, z_ref[...], dskip_ref[...], normg_ref[...]).astype(yg_ref.dtype)

    @pl.when(c == pl.num_programs(1) - 1)
    def _():
        h_out_ref[0] = h_ref[...]


def _ssm_head_lanes(p):
    return jnp.pad(p.astype(jnp.float32), (0, SSM_DT_LANES - SSM_HEADS)).reshape(1, SSM_DT_LANES)


def _ssm_chunk_scan(b_, t_, xs, bm, cm, dt, z, a_log, d_skip, norm_g):
    l = SSM_CHUNK
    nc = t_ // l
    rows = lambda w: pl.BlockSpec((l, w), lambda b, c: (b * nc + c, 0))
    vec = lambda w: pl.BlockSpec((1, w), lambda b, c: (0, 0))
    aneg = _ssm_head_lanes(-jnp.exp(a_log.astype(jnp.float32)))
    dskip = jnp.repeat(d_skip, SSM_HEAD_DIM).reshape(1, SSM_D_INNER)
    yg, h_new = pl.pallas_call(
        _ssm_chunk_kernel,
        grid=(b_, nc),
        in_specs=[rows(SSM_D_INNER), rows(SSM_BC_DIM), rows(SSM_BC_DIM), rows(SSM_DT_LANES), rows(SSM_D_INNER),
                  vec(SSM_DT_LANES), vec(SSM_D_INNER), vec(SSM_D_INNER)],
        out_specs=[rows(SSM_D_INNER),
                   pl.BlockSpec((1, SSM_HEADS, SSM_HEAD_DIM, SSM_STATE), lambda b, c: (b, 0, 0, 0))],
        out_shape=[jax.ShapeDtypeStruct((b_ * t_, SSM_D_INNER), MXU_DTYPE),
                   jax.ShapeDtypeStruct((b_, SSM_HEADS, SSM_HEAD_DIM, SSM_STATE), jnp.float32)],
        scratch_shapes=[pltpu.VMEM((SSM_HEADS, SSM_HEAD_DIM, SSM_STATE), jnp.float32),
                        pltpu.VMEM((SSM_D_INNER, l), jnp.float32),
                        pltpu.VMEM((SSM_HPG * SSM_HEAD_DIM, l), MXU_DTYPE)],
        compiler_params=pltpu.CompilerParams(
            dimension_semantics=("parallel", "arbitrary"),
            vmem_limit_bytes=V7X_VMEM_LIMIT_BYTES),
        name="ssm_chunk_scan",
    )(xs, bm, cm, dt, z, aneg, dskip, norm_g.reshape(1, SSM_D_INNER))
    return yg, h_new


def _ssm_step_kernel(h0_ref, xs_ref, dt_ref, an_ref, bm_ref, cm_ref, y_ref, h_ref):
    h0 = h0_ref[0]
    dt = dt_ref[0]
    decay = jnp.exp(dt * an_ref[...])
    xdt = xs_ref[0] * dt
    bm = bm_ref[0].astype(jnp.float32)
    cm = cm_ref[0].astype(jnp.float32)
    h_ref[0] = h0 * decay + xdt * bm
    cb = jnp.sum(cm * bm, axis=-1, keepdims=True)
    y_ref[0] = cb * xdt + jnp.sum(cm * h0, axis=-1, keepdims=True) * decay


def _ssm_step(state, xs, bm, cm, dt, a_log):
    b_ = state.shape[0]
    per_head = lambda u: jnp.repeat(u.reshape(b_, SSM_GROUPS, 1, SSM_STATE), SSM_HPG, axis=1)
    xs4 = xs.reshape(b_, SSM_HEADS, SSM_HEAD_DIM, 1)
    dt4 = dt[:, :SSM_HEADS].reshape(b_, SSM_HEADS, 1, 1)
    an = (-jnp.exp(a_log.astype(jnp.float32))).reshape(SSM_HEADS, 1, 1)
    blk = lambda a: pl.BlockSpec((1,) + a.shape[1:], lambda b: (b, 0, 0, 0))
    args = [state.astype(jnp.float32), xs4, dt4, an, per_head(bm), per_head(cm)]
    y4, h_new = pl.pallas_call(
        _ssm_step_kernel,
        grid=(b_,),
        in_specs=[blk(args[0]), blk(xs4), blk(dt4), pl.BlockSpec(an.shape, lambda b: (0, 0, 0)),
                  blk(args[4]), blk(args[5])],
        out_specs=[blk(xs4), blk(args[0])],
        out_shape=[jax.ShapeDtypeStruct(xs4.shape, jnp.float32), jax.ShapeDtypeStruct(state.shape, jnp.float32)],
        compiler_params=pltpu.CompilerParams(
            dimension_semantics=("parallel",),
            vmem_limit_bytes=V7X_VMEM_LIMIT_BYTES),
        name="ssm_step",
    )(*args)
    return y4.reshape(b_, SSM_D_INNER), h_new


def _ssm_gate_norm_kernel(y_ref, xs_ref, z_ref, dskip_ref, normg_ref, o_ref):
    o_ref[...] = _ssm_gate_norm(y_ref[...], xs_ref[...], z_ref[...], dskip_ref[...], normg_ref[...]).astype(o_ref.dtype)


def _ssm_gate_norm_rows(y, xs, z, d_skip, norm_g):
    full = lambda a: pl.BlockSpec(a.shape, lambda i: (0,) * a.ndim)
    args = [y, xs, z, jnp.repeat(d_skip, SSM_HEAD_DIM).reshape(1, SSM_D_INNER), norm_g.reshape(1, SSM_D_INNER)]
    return pl.pallas_call(
        _ssm_gate_norm_kernel,
        grid=(1,),
        in_specs=[full(a) for a in args],
        out_specs=full(y),
        out_shape=jax.ShapeDtypeStruct(y.shape, MXU_DTYPE),
        name="ssm_gate_norm",
    )(*args)


def _mamba2_mixer_pallas(x3d, conv_state, ssm_state, w_in, conv_w, conv_b, dt_bias, a_log, d_skip, norm_g):
    b_, t_, _ = x3d.shape
    z, xs, bm, cm, dt, conv_new = _ssm_project(x3d, conv_state, w_in, conv_w, conv_b, dt_bias)
    if t_ == 1:
        y, h_new = _ssm_step(ssm_state, xs, bm, cm, dt, a_log)
        yg = _ssm_gate_norm_rows(y, xs, z, d_skip, norm_g)
    else:
        yg, h_new = _ssm_chunk_scan(b_, t_, xs, bm, cm, dt, z, a_log, d_skip, norm_g)
    return yg, conv_new, h_new.astype(ssm_state.dtype)


PAGES_PER_STEP = 8


def _sortable_key(score):
    bits = pltpu.bitcast(score, jnp.int32)
    return jnp.where(bits >= 0, bits, bits ^ jnp.int32(0x7FFFFFFF))


def _dsa_decode_kernel(pt_ref, iq_ref, iw_ref, q_ref, ikn_ref, kn_ref, vn_ref, *rest,
                       topk, col_bits, n_steps, pages):
    idx_refs, k_refs, v_refs = rest[:pages], rest[pages:2 * pages], rest[2 * pages:3 * pages]
    o_ref, key_ref, dense_ref, sel_ref, m_ref, l_ref, acc_ref = rest[3 * pages:]
    s = pl.program_id(1)
    nk = key_ref.shape[2]
    nt = (((1,), (1,)), ((), ()))
    iq = iq_ref[0]
    iw = iw_ref[0]
    lane = lax.broadcasted_iota(jnp.int32, (1, nk), 1)

    def row_dot(a, row):
        return jnp.sum(a.astype(jnp.float32) * row.astype(jnp.float32), axis=1, keepdims=True)

    def index_score(sc):
        return jnp.sum(iw * jnp.maximum(sc, 0.0), axis=0, keepdims=True)

    def fold(hit):
        out = hit[:, 0:128]
        for l in range(1, nk // 128):
            out = out + hit[:, l * 128:(l + 1) * 128]
        return out

    step_row = lax.broadcasted_iota(jnp.int32, dense_ref.shape, 0)

    @pl.when(s == 0)
    def _reset():
        dense_ref[...] = jnp.full(dense_ref.shape, INT32_MIN, jnp.int32)

    @pl.when(s < n_steps)
    def _score():
        ik_t = jnp.concatenate([r[0] for r in idx_refs], axis=1).astype(MXU_DTYPE)
        key = _sortable_key(index_score(jnp.dot(iq, ik_t, preferred_element_type=jnp.float32)))
        key_ref[s] = key
        dense_ref[...] = jnp.where(step_row == s, key, dense_ref[...])

    @pl.when(s == n_steps - 1)
    def _select():
        key_new = _sortable_key(index_score(row_dot(iq, ikn_ref[0])))
        dense = dense_ref[...]
        dense_col = step_row * nk + lax.broadcasted_iota(jnp.int32, dense_ref.shape, 1)

        def count(pred_past, pred_new):
            hit = jnp.where(pred_past(dense, dense_col), 1.0, 0.0)
            cnt = _tree_sum([hit[:, l * 128:(l + 1) * 128] for l in range(nk // 128)])
            total = jnp.sum(jnp.sum(cnt, axis=1, keepdims=True), axis=0, keepdims=True)
            return total + jnp.where(pred_new(key_new), 1.0, 0.0)

        def at_least(cand):
            return count(lambda k, c: k >= cand, lambda k: k >= cand) >= float(topk)

        def two_bits(i, thr):
            hi = lax.shift_left(jnp.int32(1), jnp.int32(31) - 2 * i)
            lo = lax.shift_left(jnp.int32(1), jnp.int32(30) - 2 * i)
            c1, c2, c3 = thr ^ lo, thr ^ hi, thr ^ hi ^ lo
            return jnp.where(at_least(c3), c3, jnp.where(at_least(c2), c2, jnp.where(at_least(c1), c1, thr)))

        thr = lax.fori_loop(0, 16, two_bits, jnp.full((1, 1), INT32_MIN, jnp.int32))
        need = float(topk) - count(lambda k, c: k > thr, lambda k: k > thr)

        def col_body(i, last):
            cand = last | lax.shift_left(jnp.int32(1), jnp.int32(col_bits - 1) - i)
            ties = count(lambda k, c: (k == thr) & (c < cand),
                         lambda k: (k == thr) & (jnp.int32(n_steps * nk) < cand))
            return jnp.where(ties < need, cand, last)

        n_tied = count(lambda k, c: k == thr, lambda k: k == thr)
        last_tie = lax.cond(n_tied[0, 0] > need[0, 0],
                            lambda: lax.fori_loop(0, col_bits, col_body, jnp.zeros((1, 1), jnp.int32)),
                            lambda: jnp.full((1, 1), 2 ** col_bits - 1, jnp.int32))
        sel_ref[0] = jnp.broadcast_to(thr, sel_ref.shape[1:])
        sel_ref[1] = jnp.broadcast_to(last_tie, sel_ref.shape[1:])
        sel_ref[2] = jnp.broadcast_to(key_new, sel_ref.shape[1:])

    gsz = ATT_HEADS // ATT_KV_HEADS
    q = q_ref[0]
    q_wide = jnp.concatenate([q] * ATT_KV_HEADS, axis=1)
    head_i = lax.broadcasted_iota(jnp.int32, q_wide.shape, 0)
    col_i = lax.broadcasted_iota(jnp.int32, q_wide.shape, 1)
    own_group = (col_i // ATT_HEAD_DIM) == (head_i // gsz)
    q_blk = jnp.where(own_group, q_wide, jnp.zeros_like(q_wide))

    def keep_mask(key, col):
        thr, last_tie = sel_ref[0, 0:1, 0:1], sel_ref[1, 0:1, 0:1]
        return (key > thr) | ((key == thr) & (col <= last_tie))

    def online_update(logits, weighted_values):
        m_old = m_ref[...]
        m_new = jnp.maximum(m_old, jnp.max(logits, axis=1, keepdims=True))
        p = jnp.exp(logits - m_new)
        alpha = jnp.exp(m_old - m_new)
        l_ref[...] = alpha * l_ref[...] + jnp.sum(p, axis=1, keepdims=True)
        acc_ref[...] = alpha * acc_ref[...] + weighted_values(p.astype(MXU_DTYPE))
        m_ref[...] = m_new

    @pl.when(s == n_steps)
    def _init():
        m_ref[...] = jnp.full(m_ref.shape, MASK_NEG, jnp.float32)
        l_ref[...] = jnp.zeros(l_ref.shape, jnp.float32)
        acc_ref[...] = jnp.zeros(acc_ref.shape, jnp.float32)

    @pl.when(s >= n_steps)
    def _attend():
        st = s - n_steps
        bias = jnp.where(keep_mask(key_ref[st], lane + st * nk), 0.0, MASK_NEG)
        k_t = jnp.concatenate([r[0] for r in k_refs], axis=1).astype(MXU_DTYPE)
        v_t = jnp.concatenate([r[0] for r in v_refs], axis=1).astype(MXU_DTYPE)
        online_update(jnp.dot(q_blk, k_t, preferred_element_type=jnp.float32) + bias,
                      lambda p: lax.dot_general(p, v_t, nt, preferred_element_type=jnp.float32))

    @pl.when(s == 2 * n_steps - 1)
    def _finish():
        keep_new = keep_mask(sel_ref[2, 0:1, 0:1], jnp.int32(n_steps * nk))
        logit = row_dot(q_blk, kn_ref[0])
        v_row = vn_ref[0].astype(jnp.float32)
        online_update(logit + jnp.where(keep_new, 0.0, MASK_NEG), lambda p: p.astype(jnp.float32) * v_row)
        out = jnp.where(own_group, acc_ref[...] / l_ref[...], 0.0)
        o = out[:, 0:ATT_HEAD_DIM]
        for g in range(1, ATT_KV_HEADS):
            o = o + out[:, g * ATT_HEAD_DIM:(g + 1) * ATT_HEAD_DIM]
        o_ref[0] = o.astype(o_ref.dtype)


def _dsa_decode(q, iq, iw, ik_new, k_new, v_new, cache_k, cache_v, cache_idx_k, page_table):
    b_, n_pages = page_table.shape
    n_pool, page = cache_k.shape[0], cache_k.shape[1]
    pages = PAGES_PER_STEP
    n_steps = n_pages // pages
    past = n_pages * page
    ck = jnp.transpose(cache_k, (0, 2, 3, 1)).reshape(n_pool, ATT_KV_DIM, page)
    cv = jnp.transpose(cache_v, (0, 2, 3, 1)).reshape(n_pool, ATT_KV_DIM, page)
    cik = jnp.swapaxes(cache_idx_k, 1, 2)
    per_seq = lambda a: pl.BlockSpec((1,) + a.shape[1:], lambda b, s, pt: (b,) + (0,) * (a.ndim - 1))

    def paged(width, j, attend_phase):
        def index(b, s, pt):
            grp = jnp.maximum(s - n_steps, 0) if attend_phase else jnp.minimum(s, n_steps - 1)
            return (pt[b, grp * pages + j], 0, 0)
        return pl.BlockSpec((1, width, page), index)

    small = [iq.reshape(b_, IDX_HEADS, IDX_DIM), iw[:, :IDX_HEADS].reshape(b_, IDX_HEADS, 1),
             q.reshape(b_, ATT_HEADS, ATT_HEAD_DIM), ik_new.astype(MXU_DTYPE).reshape(b_, 1, IDX_DIM),
             k_new.astype(MXU_DTYPE).reshape(b_, 1, ATT_KV_DIM), v_new.astype(MXU_DTYPE).reshape(b_, 1, ATT_KV_DIM)]
    grid_spec = pltpu.PrefetchScalarGridSpec(
        num_scalar_prefetch=1,
        grid=(b_, 2 * n_steps),
        in_specs=[per_seq(a) for a in small]
        + [paged(IDX_DIM, j, False) for j in range(pages)]
        + [paged(ATT_KV_DIM, j, True) for j in range(pages)]
        + [paged(ATT_KV_DIM, j, True) for j in range(pages)],
        out_specs=pl.BlockSpec((1, ATT_HEADS, ATT_HEAD_DIM), lambda b, s, pt: (b, 0, 0)),
        scratch_shapes=[pltpu.VMEM((n_steps, 1, pages * page), jnp.int32),
                        pltpu.VMEM((n_steps, pages * page), jnp.int32),
                        pltpu.VMEM((3, SUBLANES, 128), jnp.int32),
                        pltpu.VMEM((ATT_HEADS, 1), jnp.float32),
                        pltpu.VMEM((ATT_HEADS, 1), jnp.float32),
                        pltpu.VMEM((ATT_HEADS, ATT_KV_DIM), jnp.float32)])
    o = pl.pallas_call(
        functools.partial(_dsa_decode_kernel, topk=min(TOPK_MAX, (past + 1) // 4),
                          col_bits=max(1, past.bit_length()), n_steps=n_steps, pages=pages),
        grid_spec=grid_spec,
        out_shape=jax.ShapeDtypeStruct((b_, ATT_HEADS, ATT_HEAD_DIM), MXU_DTYPE),
        compiler_params=pltpu.CompilerParams(
            dimension_semantics=("parallel", "arbitrary"),
            vmem_limit_bytes=V7X_VMEM_LIMIT_BYTES),
        name="dsa_decode",
    )(page_table, *small, *([cik] * pages), *([ck] * pages), *([cv] * pages))
    return o.reshape(b_, ATT_Q_DIM)


def _dsa_sample_pallas(x3d, cache_k, cache_v, cache_idx_k, page_table, w_in, kn_g, kn_b):
    b_, t_, _ = x3d.shape
    past = page_table.shape[1] * cache_k.shape[1]
    pos = jnp.full((b_,), past, jnp.int32)
    q, iq, v, _, iw, kT, _, ikT, _ = _dsa_project(x3d.reshape(1, b_, D_MODEL), pos, w_in, kn_g, kn_b)
    k = _untranspose_groups(kT)[0]
    ik = _untranspose_groups(ikT)[0]
    o = _dsa_decode(q, iq, iw, ik, k, v, cache_k, cache_v, cache_idx_k, page_table)
    kv4 = lambda u: u.reshape(b_, t_, ATT_KV_HEADS, ATT_HEAD_DIM)
    return o, kv4(k), kv4(v), ik.reshape(b_, t_, IDX_DIM)


def _layer_norm(x, g, b):
    xf = x.astype(jnp.float32)
    mu = jnp.mean(xf, -1, keepdims=True)
    var = jnp.mean(jnp.square(xf - mu), -1, keepdims=True)
    return ((xf - mu) * lax.rsqrt(var + LN_EPS)).astype(x.dtype) * g + b


def _rope_partial(x, pos, rot_dim):
    half = rot_dim // 2
    inv = ROPE_THETA ** (-jnp.arange(half, dtype=jnp.float32) / half)
    ang = pos.astype(jnp.float32)[:, None] * inv[None, :]
    cos = jnp.cos(ang)[:, None, :]
    sin = jnp.sin(ang)[:, None, :]
    xf = x[..., :rot_dim].astype(jnp.float32)
    x1, x2 = xf[..., :half], xf[..., half:]
    rot = jnp.concatenate([x1 * cos - x2 * sin, x2 * cos + x1 * sin], axis=-1).astype(x.dtype)
    return jnp.concatenate([rot, x[..., rot_dim:]], axis=-1)


def _gather_rows(rows, idx):
    return jax.vmap(lambda r, i: r[i])(rows, idx)


def _gmlp_mixer(x, w_in, ln_g, ln_b, ws, bs):
    B_, T, _ = x.shape
    u, v = jnp.split(jax.nn.gelu(x @ w_in), 2, axis=-1)
    v = _layer_norm(v, ln_g, ln_b)
    l = min(T, CHUNK)
    c = T // l
    mask = jnp.tril(jnp.ones((l, l), dtype=bool))
    w = jnp.where(mask, ws[:, :l, :l], 0.0)
    vc = v.reshape(B_, c, l, GM_GROUPS, GM_GROUP_DIM)
    mixed = jnp.einsum('gts,bcsgd->bctgd', w, vc) + jnp.transpose(bs[:, :l])[:, :, None]
    return u * mixed.reshape(B_, T, GM_WIDTH), v


def _ssd_chunked(xs, dt, a, bm, cm, h0):
    B_, T = xs.shape[:2]
    l = min(T, SSM_CHUNK)
    c = T // l
    blk = lambda t: t.reshape((B_, c, l) + t.shape[2:])
    xdt = blk(xs.astype(jnp.float32) * dt[..., None])
    bc, cc, acum = blk(bm), blk(cm), jnp.cumsum(blk(a), axis=2)
    at = jnp.moveaxis(acum, 2, -1)
    causal = jnp.tril(jnp.ones((l, l), dtype=bool))
    seg = jnp.exp(jnp.where(causal, at[..., :, None] - at[..., None, :], -jnp.inf))
    cb = jnp.einsum('bctgn,bcsgn->bcgts', cc, bc)
    y_diag = jnp.einsum('bcgts,bcgets,bcsgep->bctgep', cb, seg, xdt)
    states = jnp.einsum('bclgn,bclge,bclgep->bcgepn', bc, jnp.exp(acum[:, :, -1:] - acum), xdt)
    chunk_decay = jnp.exp(acum[:, :, -1])

    def step(h, inp):
        dec, st = inp
        return h * dec[..., None, None] + st, h

    h_last, h_in = lax.scan(step, h0, (jnp.moveaxis(chunk_decay, 1, 0), jnp.moveaxis(states, 1, 0)))
    y_off = jnp.einsum('bctgn,bcgepn,bctge->bctgep', cc, jnp.moveaxis(h_in, 0, 1), jnp.exp(acum))
    return (y_diag + y_off).reshape(B_, T, SSM_GROUPS, SSM_HPG, SSM_HEAD_DIM), h_last


def _mamba2_mixer(x, conv_state, ssm_state, w_in, conv_w, conv_b, dt_bias, a_log, d_skip, norm_g):
    B_, T, _ = x.shape
    z, xbc, dt = jnp.split(x @ w_in, [SSM_D_INNER, SSM_D_INNER + SSM_CONV_DIM], axis=-1)
    xbc_ext = jnp.concatenate([conv_state, xbc], axis=1)
    conv = conv_b
    for j in range(SSM_CONV):
        conv = conv + xbc_ext[:, j:j + T] * conv_w[j]
    xbc = jax.nn.silu(conv)
    xs, bm, cm = jnp.split(xbc, [SSM_D_INNER, SSM_D_INNER + SSM_GROUPS * SSM_STATE], axis=-1)
    xs = xs.reshape(B_, T, SSM_GROUPS, SSM_HPG, SSM_HEAD_DIM)
    bm = bm.reshape(B_, T, SSM_GROUPS, SSM_STATE)
    cm = cm.reshape(B_, T, SSM_GROUPS, SSM_STATE)
    dt = jax.nn.softplus((dt + dt_bias).astype(jnp.float32)).reshape(B_, T, SSM_GROUPS, SSM_HPG)
    a_neg = -jnp.exp(a_log.astype(jnp.float32)).reshape(SSM_GROUPS, SSM_HPG)
    h0 = ssm_state.astype(jnp.float32).reshape(B_, SSM_GROUPS, SSM_HPG, SSM_HEAD_DIM, SSM_STATE)
    y, h_last = _ssd_chunked(xs, dt, dt * a_neg, bm, cm, h0)
    y = y.astype(x.dtype) + xs * d_skip.reshape(SSM_GROUPS, SSM_HPG, 1)
    yg = (y.reshape(B_, T, SSM_D_INNER) * jax.nn.silu(z)).reshape(B_, T, SSM_GROUPS, -1).astype(jnp.float32)
    yg = (yg * lax.rsqrt(jnp.mean(jnp.square(yg), -1, keepdims=True) + LN_EPS)).astype(x.dtype)
    yg = yg.reshape(B_, T, SSM_D_INNER) * norm_g
    new_ssm = h_last.reshape(B_, SSM_HEADS, SSM_HEAD_DIM, SSM_STATE).astype(ssm_state.dtype)
    return yg, xbc_ext[:, T:], new_ssm


def _dsa_project_jax(x, pos, w_in, kn_g, kn_b):
    B_, T, _ = x.shape
    q, k, v, iq, ik, iw = jnp.split(x @ w_in, list(ATT_IN_SPLITS), axis=-1)
    q = _rope_partial(q.reshape(B_, T, ATT_HEADS, ATT_HEAD_DIM), pos, ROPE_DIM)
    k = _rope_partial(k.reshape(B_, T, ATT_KV_HEADS, ATT_HEAD_DIM), pos, ROPE_DIM)
    v = v.reshape(B_, T, ATT_KV_HEADS, ATT_HEAD_DIM)
    iq = _rope_partial(iq.reshape(B_, T, IDX_HEADS, IDX_DIM), pos, IDX_ROPE_DIM)
    ik = _rope_partial(_layer_norm(ik, kn_g, kn_b)[:, :, None, :], pos, IDX_ROPE_DIM)[:, :, 0, :]
    iw = iw * (IDX_HEADS ** -0.5 * IDX_DIM ** -0.5)
    return q, k, v, iq, ik, iw


def _dsa_select(iq, iw, ik, qpos, topk):
    s = jnp.einsum('bqhd,bsd->bqhs', iq, ik)
    score = jnp.einsum('bqh,bqhs->bqs', iw, jax.nn.relu(s)).astype(jnp.float32)
    adm = jnp.arange(ik.shape[1])[None, :] <= qpos[:, None]
    score = jnp.where(adm[None], score, -jnp.inf)
    _, idx = lax.top_k(score, topk)
    return idx, idx <= qpos[None, :, None]


def _sparse_attend(q, k_sel, v_sel, valid):
    B_, Q = q.shape[:2]
    qg = q.reshape(B_, Q, ATT_KV_HEADS, ATT_HEADS // ATT_KV_HEADS, ATT_HEAD_DIM)
    s = jnp.einsum('bqhgd,bqkhd->bqhgk', qg, k_sel).astype(jnp.float32) * (ATT_HEAD_DIM ** -0.5)
    s = jnp.where(valid[:, :, None, None, :], s, -jnp.inf)
    p = jax.nn.softmax(s, axis=-1).astype(v_sel.dtype)
    o = jnp.einsum('bqhgk,bqkhd->bqhgd', p, v_sel)
    return o.reshape(B_, Q, ATT_Q_DIM)


def _dsa_prompt(x, w_in, kn_g, kn_b):
    B_, T, _ = x.shape
    q, k, v, iq, ik, iw = _dsa_project_jax(x, jnp.arange(T), w_in, kn_g, kn_b)
    topk = min(TOPK_MAX, T // 4)

    def block(bi):
        t0 = bi * Q_BLOCK
        sl = lambda t: lax.dynamic_slice_in_dim(t, t0, Q_BLOCK, axis=1)
        qpos = t0 + jnp.arange(Q_BLOCK)
        idx, valid = _dsa_select(sl(iq), sl(iw), ik, qpos, topk)
        return _sparse_attend(sl(q), _gather_rows(k, idx), _gather_rows(v, idx), valid)

    o = lax.map(block, jnp.arange(T // Q_BLOCK))
    o = jnp.moveaxis(o, 0, 1).reshape(B_, T, ATT_Q_DIM)
    return o, k, v, ik


def _dsa_sample(x, cache_k, cache_v, cache_idx_k, page_table, w_in, kn_g, kn_b):
    B_, T, _ = x.shape
    page = cache_k.shape[1]
    past = page_table.shape[1] * page
    pos = past + jnp.arange(T)
    q, k, v, iq, ik, iw = _dsa_project_jax(x, pos, w_in, kn_g, kn_b)
    ik_all = jnp.concatenate([cache_idx_k[page_table].reshape(B_, past, IDX_DIM), ik], axis=1)
    idx, valid = _dsa_select(iq, iw, ik_all, pos, min(TOPK_MAX, (past + T) // 4))
    past_idx = jnp.minimum(idx, past - 1)
    phys = jnp.take_along_axis(page_table, (past_idx // page).reshape(B_, -1), axis=1).reshape(idx.shape)
    off = past_idx % page
    new_idx = jnp.clip(idx - past, 0, T - 1)
    is_new = (idx >= past)[..., None, None]
    k_sel = jnp.where(is_new, _gather_rows(k, new_idx), cache_k[phys, off])
    v_sel = jnp.where(is_new, _gather_rows(v, new_idx), cache_v[phys, off])
    o = _sparse_attend(q, k_sel, v_sel, valid)
    return o, k, v, ik


def _rwkv7_mixer(x, shift, wkv, mu, w_r, w_k, w_v, w0, w1, w2, a0, a1, a2, g1, g2,
                 k_k, k_a, r_k, gn_g, gn_b):
    B_, T, _ = x.shape
    x_prev = jnp.concatenate([shift[:, None, :], x[:, :-1]], axis=1)
    xm = x[None] + (x_prev - x)[None] * mu[:, None, None, :]
    xr, xw, xk, xv, xa, xg = xm
    r = xr @ w_r
    w_log = -jax.nn.softplus(-(w0 + jnp.tanh(xw @ w1) @ w2)) - 0.5
    k = xk @ w_k
    v = xv @ w_v
    a = jax.nn.sigmoid(a0 + (xa @ a1) @ a2)
    g = jax.nn.sigmoid(xg @ g1) @ g2
    heads = lambda t: t.reshape(B_, T, RW_HEADS, RW_HEAD)
    kk = heads(k * k_k).astype(jnp.float32)
    kk = kk / jnp.maximum(jnp.sqrt(jnp.sum(kk * kk, -1, keepdims=True)), 1e-12)
    k = k * (1.0 + (a - 1.0) * k_a)
    decay = jnp.exp(-jnp.exp(w_log.astype(jnp.float32)))
    r, k, v, a, decay = heads(r), heads(k), heads(v), heads(a), heads(decay)
    seq = tuple(jnp.moveaxis(t.astype(jnp.float32), 1, 0) for t in (r, decay, k, v, kk, kk * a))

    def step(s, inp):
        r_t, d_t, k_t, v_t, kk_t, b_t = inp
        sa = jnp.einsum('bhij,bhj->bhi', s, kk_t)
        s = s * d_t[:, :, None, :] - sa[..., None] * b_t[:, :, None, :] + v_t[..., None] * k_t[:, :, None, :]
        return s, jnp.einsum('bhij,bhj->bhi', s, r_t)

    s_last, y = lax.scan(step, wkv.astype(jnp.float32), seq)
    y = jnp.moveaxis(y, 0, 1)
    mu_y = jnp.mean(y, -1, keepdims=True)
    var_y = jnp.mean(jnp.square(y - mu_y), -1, keepdims=True)
    yn = ((y - mu_y) * lax.rsqrt(var_y + RW_GN_EPS)).reshape(B_, T, D_MODEL).astype(x.dtype) * gn_g + gn_b
    bonus = (jnp.sum(r * k * r_k, -1, keepdims=True) * v).reshape(B_, T, D_MODEL)
    return (yn + bonus) * g, x[:, -1], s_last.astype(wkv.dtype)


def kernel(x_prompt, x_sample, state_ssm_conv, state_ssm, cache_k, cache_v, cache_idx_k, state_rwkv_shift, state_rwkv_wkv, page_table, p_prompt, p_sample, ln_g, ln_b, ffn_w_up, ffn_w_down, ple_w_p, ple_w_g, ple_b_g, gm_w_in, gm_ln_g, gm_ln_b, gm_ws, gm_bs, gm_w_out, ssm_w_in, ssm_conv_w, ssm_conv_b, ssm_dt_bias, ssm_a_log, ssm_d, ssm_norm_g, ssm_w_out, att_w_in, att_kn_g, att_kn_b, att_w_out, rw_mu, rw_w_r, rw_w_k, rw_w_v, rw_w_o, rw_w0, rw_w1, rw_w2, rw_a0, rw_a1, rw_a2, rw_g1, rw_g2, rw_k_k, rw_k_a, rw_r_k, rw_gn_g, rw_gn_b):
    bp, tp, _ = x_prompt.shape
    bs_, ts, _ = x_sample.shape
    bf = lambda w: w.astype(jnp.bfloat16)
    w_up_bf, w_down_bf = bf(ffn_w_up), bf(ffn_w_down)
    ple_wp_bf, ple_wg_bf = bf(ple_w_p), bf(ple_w_g)
    pp3 = p_prompt.reshape(DEPTH, bp * tp, PLE_DIM)
    ps3 = p_sample.reshape(DEPTH, bs_ * ts, PLE_DIM)

    yp = x_prompt.reshape(bp * tp, D_MODEL)
    ys = x_sample.reshape(bs_ * ts, D_MODEL)
    r3p = lambda t: t.reshape(bp, tp, -1)
    r3s = lambda t: t.reshape(bs_, ts, -1)
    f2 = lambda t: t.reshape(-1, t.shape[-1])

    for i in range(DEPTH):
        yp = _ffn_sub(yp, w_up_bf, w_down_bf, i, 0, ln_g[i, 0], ln_b[i, 0])
        ys = _ffn_sub(ys, w_up_bf, w_down_bf, i, 0, ln_g[i, 0], ln_b[i, 0])
        m = i % N_MIXERS
        if m == 0:
            gm_args = (gm_w_in, gm_ln_g, gm_ln_b, gm_ws, gm_bs, gm_w_out, ln_g[i, 1], ln_b[i, 1])
            yp, = _gmlp_block(yp, tp, *gm_args, False)
            ys, gm_v_s = _gmlp_block(ys, ts, *gm_args, True)
            gm_v_s = r3s(gm_v_s)
        elif m == 1:
            ssm_args = (ssm_w_in, ssm_conv_w, ssm_conv_b, ssm_dt_bias, ssm_a_log, ssm_d, ssm_norm_g)
            hp, conv_p, ssm_p = _mamba2_mixer_pallas(
                r3p(yp), jnp.zeros((bp, SSM_CONV - 1, SSM_CONV_DIM), yp.dtype),
                jnp.zeros((bp, SSM_HEADS, SSM_HEAD_DIM, SSM_STATE), yp.dtype), *ssm_args)
            hs, conv_s, ssm_s = _mamba2_mixer_pallas(r3s(ys), state_ssm_conv, state_ssm, *ssm_args)
            w_out = bf(ssm_w_out)
        elif m == 2:
            hp, k_p, v_p, ik_p = _dsa_prompt_pallas(r3p(yp), att_w_in, att_kn_g, att_kn_b)
            hs, k_s, v_s, ik_s = _dsa_sample_pallas(r3s(ys), cache_k, cache_v, cache_idx_k, page_table,
                                                    att_w_in, att_kn_g, att_kn_b)
            w_out = bf(att_w_out)
        else:
            rw_args = (rw_mu, rw_w_r, rw_w_k, rw_w_v, rw_w0, rw_w1, rw_w2, rw_a0, rw_a1, rw_a2,
                       rw_g1, rw_g2, rw_k_k, rw_k_a, rw_r_k, rw_gn_g, rw_gn_b)
            hp, gate_p, sh_p, wkv_p = _rwkv7_mixer_pallas(
                r3p(yp), jnp.zeros((bp, D_MODEL), yp.dtype),
                jnp.zeros((bp, RW_HEADS, RW_HEAD, RW_HEAD), yp.dtype), *rw_args)
            hs, gate_s, sh_s, wkv_s = _rwkv7_mixer_pallas(r3s(ys), state_rwkv_shift, state_rwkv_wkv, *rw_args)
            w_out = bf(rw_w_o)
        if m == 3:
            yp = _proj_gate_post_norm(yp, hp, gate_p, w_out, ln_g[i, 1], ln_b[i, 1])
            ys = _proj_gate_post_norm(ys, hs, gate_s, w_out, ln_g[i, 1], ln_b[i, 1])
        elif m != 0:
            yp = _proj_post_norm(yp, f2(hp), w_out, ln_g[i, 1], ln_b[i, 1])
            ys = _proj_post_norm(ys, f2(hs), w_out, ln_g[i, 1], ln_b[i, 1])
        yp = _ffn_sub(yp, w_up_bf, w_down_bf, i, 1, ln_g[i, 2], ln_b[i, 2])
        ys = _ffn_sub(ys, w_up_bf, w_down_bf, i, 1, ln_g[i, 2], ln_b[i, 2])
        yp = _ple_add(yp, pp3, ple_wp_bf, ple_wg_bf, ple_b_g[i], i)
        ys = _ple_add(ys, ps3, ple_wp_bf, ple_wg_bf, ple_b_g[i], i)

    return (r3p(yp), r3s(ys), gm_v_s, conv_p, ssm_p, conv_s, ssm_s, k_p, v_p, ik_p, k_s, v_s, ik_s,
            sh_p, wkv_p, sh_s, wkv_s)
```

```python
import functools
import math

import jax
import jax.numpy as jnp
from jax import lax
from jax.experimental import pallas as pl
from jax.experimental.pallas import tpu as pltpu

D_MODEL = 1024
DEPTH = 4
N_MIXERS = 4
PLE_DIM = 256
D_FF = 2816
ALPHA = (2 * DEPTH) ** 0.25
LN_EPS = 1e-5

CHUNK = 128
GM_WIDTH = 2 * D_MODEL
GM_GROUPS = 8
GM_GROUP_DIM = GM_WIDTH // GM_GROUPS

SSM_D_INNER = 2 * D_MODEL
SSM_HEAD_DIM = 64
SSM_HEADS = SSM_D_INNER // SSM_HEAD_DIM
SSM_GROUPS = 4
SSM_HPG = SSM_HEADS // SSM_GROUPS
SSM_STATE = 128
SSM_CONV = 4
SSM_CONV_DIM = SSM_D_INNER + 2 * SSM_GROUPS * SSM_STATE
SSM_CHUNK = 128

ATT_HEADS = 16
ATT_KV_HEADS = 4
ATT_HEAD_DIM = D_MODEL // ATT_HEADS
ROPE_DIM = ATT_HEAD_DIM // 4
ROPE_THETA = 500000.0
IDX_HEADS = 8
IDX_DIM = 64
IDX_ROPE_DIM = IDX_DIM // 4
TOPK_MAX = 256
Q_BLOCK = 128
ATT_Q_DIM = ATT_HEADS * ATT_HEAD_DIM
ATT_KV_DIM = ATT_KV_HEADS * ATT_HEAD_DIM
ATT_IN_SPLITS = (ATT_Q_DIM, ATT_Q_DIM + ATT_KV_DIM, ATT_Q_DIM + 2 * ATT_KV_DIM,
                 ATT_Q_DIM + 2 * ATT_KV_DIM + IDX_HEADS * IDX_DIM,
                 ATT_Q_DIM + 2 * ATT_KV_DIM + IDX_HEADS * IDX_DIM + IDX_DIM)

RW_HEAD = 64
RW_HEADS = D_MODEL // RW_HEAD
RW_GN_EPS = 64e-5

V7X_VMEM_LIMIT_BYTES = 52 * 1024 * 1024
FF_TILE = D_FF // 2
ROW_TILE = 512


def _row_tile(m):
    return ROW_TILE if m % ROW_TILE == 0 else m


def _ln_rows(y, g, b):
    mu = jnp.mean(y, axis=-1, keepdims=True)
    yc = y - mu
    var = jnp.mean(yc * yc, axis=-1, keepdims=True)
    return yc * lax.rsqrt(var + LN_EPS) * g + b


FFN_ROW_TILE = 1024
FFN_ROW_PARTS = 2


def _ffn_kernel(x_ref, wu_ref, wd_ref, g_ref, b_ref, *rest, parts, with_ple):
    o_ref = rest[-1]
    rows = x_ref.shape[0] // parts
    for p in range(parts):
        x = x_ref[p * rows:(p + 1) * rows, :]
        xb = x.astype(MXU_DTYPE)
        acc = None
        for f in range(D_FF // FF_TILE):
            cols = slice(f * FF_TILE, (f + 1) * FF_TILE)
            gate = jnp.dot(xb, wu_ref[:, cols], preferred_element_type=jnp.float32)
            lin = jnp.dot(xb, wu_ref[:, D_FF + f * FF_TILE:D_FF + (f + 1) * FF_TILE],
                          preferred_element_type=jnp.float32)
            h = (gate * jax.nn.sigmoid(gate) * lin).astype(MXU_DTYPE)
            part = jnp.dot(h, wd_ref[cols, :], preferred_element_type=jnp.float32)
            acc = part if acc is None else acc + part
        y = _ln_rows(ALPHA * x + 0.5 * acc, g_ref[...], b_ref[...])
        if with_ple:
            p_ref, wp_ref, wg_ref, bg_ref = rest[:4]
            gate = jax.nn.sigmoid(
                jnp.dot(y.astype(MXU_DTYPE), wg_ref[...], preferred_element_type=jnp.float32) + bg_ref[...])
            emb = jnp.dot(p_ref[p * rows:(p + 1) * rows, :].astype(MXU_DTYPE), wp_ref[...],
                          preferred_element_type=jnp.float32)
            y = y + gate * emb
        o_ref[p * rows:(p + 1) * rows, :] = y


def _ffn_sub(x2d, w_up, w_down, layer, half, g, b, ple=None):
    m = x2d.shape[0]
    tm = FFN_ROW_TILE if m % FFN_ROW_TILE == 0 else m
    parts = FFN_ROW_PARTS if tm == FFN_ROW_TILE else 1
    resident = dict(pipeline_mode=pl.Buffered(1))
    vec = pl.BlockSpec((1, D_MODEL), lambda i: (0, 0))
    in_specs = [
        pl.BlockSpec((tm, D_MODEL), lambda i: (i, 0)),
        pl.BlockSpec((None, None, D_MODEL, 2 * D_FF), lambda i: (layer, half, 0, 0), **resident),
        pl.BlockSpec((None, None, D_FF, D_MODEL), lambda i: (layer, half, 0, 0), **resident),
        vec, vec]
    args = [x2d, w_up, w_down, g.reshape(1, D_MODEL), b.reshape(1, D_MODEL)]
    if ple is not None:
        p3d, w_p, w_g, b_g = ple
        in_specs += [pl.BlockSpec((None, tm, PLE_DIM), lambda i: (layer, i, 0)),
                     pl.BlockSpec((None, PLE_DIM, D_MODEL), lambda i: (layer, 0, 0), **resident),
                     pl.BlockSpec((None, D_MODEL, D_MODEL), lambda i: (layer, 0, 0), **resident),
                     pl.BlockSpec((None, 1, D_MODEL), lambda i: (layer, 0, 0))]
        args += [p3d, w_p, w_g, b_g.reshape(DEPTH, 1, D_MODEL)]
    return pl.pallas_call(
        functools.partial(_ffn_kernel, parts=parts, with_ple=ple is not None),
        grid=(m // tm,),
        in_specs=in_specs,
        out_specs=pl.BlockSpec((tm, D_MODEL), lambda i: (i, 0)),
        out_shape=jax.ShapeDtypeStruct((m, D_MODEL), jnp.float32),
        compiler_params=pltpu.CompilerParams(
            dimension_semantics=("parallel",),
            vmem_limit_bytes=V7X_VMEM_LIMIT_BYTES),
        name="ffn_ple" if ple is not None else "ffn_sub",
    )(*args)


def _ple_kernel(x_ref, p_ref, wp_ref, wg_ref, bg_ref, o_ref):
    x = x_ref[...]
    gate = jax.nn.sigmoid(
        jnp.dot(x.astype(jnp.bfloat16), wg_ref[...], preferred_element_type=jnp.float32) + bg_ref[...])
    emb = jnp.dot(p_ref[...].astype(jnp.bfloat16), wp_ref[...], preferred_element_type=jnp.float32)
    o_ref[...] = x + gate * emb


def _ple_add(x2d, p3d, w_p, w_g, b_g, layer):
    m = x2d.shape[0]
    tm = _row_tile(m)
    return pl.pallas_call(
        _ple_kernel,
        grid=(m // tm,),
        in_specs=[
            pl.BlockSpec((tm, D_MODEL), lambda i: (i, 0)),
            pl.BlockSpec((None, tm, PLE_DIM), lambda i: (layer, i, 0)),
            pl.BlockSpec((None, PLE_DIM, D_MODEL), lambda i: (layer, 0, 0)),
            pl.BlockSpec((None, D_MODEL, D_MODEL), lambda i: (layer, 0, 0)),
            pl.BlockSpec((1, D_MODEL), lambda i: (0, 0)),
        ],
        out_specs=pl.BlockSpec((tm, D_MODEL), lambda i: (i, 0)),
        out_shape=jax.ShapeDtypeStruct((m, D_MODEL), jnp.float32),
        compiler_params=pltpu.CompilerParams(
            dimension_semantics=("parallel",),
            vmem_limit_bytes=V7X_VMEM_LIMIT_BYTES),
        name="ple_add",
    )(x2d, p3d, w_p, w_g, b_g.reshape(1, D_MODEL))


def _proj_ln_kernel(x_ref, h_ref, w_ref, g_ref, b_ref, o_ref):
    y = ALPHA * x_ref[...] + jnp.dot(h_ref[...].astype(jnp.bfloat16), w_ref[...],
                                     preferred_element_type=jnp.float32)
    o_ref[...] = _ln_rows(y, g_ref[...], b_ref[...])


def _proj_post_norm(x2d, h2d, w_out, g, b):
    m = x2d.shape[0]
    k = h2d.shape[1]
    tm = _row_tile(m)
    return pl.pallas_call(
        _proj_ln_kernel,
        grid=(m // tm,),
        in_specs=[
            pl.BlockSpec((tm, D_MODEL), lambda i: (i, 0)),
            pl.BlockSpec((tm, k), lambda i: (i, 0)),
            pl.BlockSpec((k, D_MODEL), lambda i: (0, 0)),
            pl.BlockSpec((1, D_MODEL), lambda i: (0, 0)),
            pl.BlockSpec((1, D_MODEL), lambda i: (0, 0)),
        ],
        out_specs=pl.BlockSpec((tm, D_MODEL), lambda i: (i, 0)),
        out_shape=jax.ShapeDtypeStruct((m, D_MODEL), jnp.float32),
        compiler_params=pltpu.CompilerParams(
            dimension_semantics=("parallel",),
            vmem_limit_bytes=V7X_VMEM_LIMIT_BYTES),
        name="proj_post_norm",
    )(x2d, h2d, w_out, g.reshape(1, D_MODEL), b.reshape(1, D_MODEL))


MXU_DTYPE = jnp.bfloat16
KEY_GROUP = 512
INT32_MIN = -2 ** 31
MASK_NEG = -1e30


def _rope_lane_tables(pos, rot_dim, head_dim):
    half = rot_dim // 2
    inv = ROPE_THETA ** (-jnp.arange(half, dtype=jnp.float32) / half)
    ang = pos.astype(jnp.float32)[:, None] * inv[None, :]
    cos, sin = jnp.cos(ang), jnp.sin(ang)
    n = pos.shape[0]
    rest = head_dim - rot_dim
    c = jnp.concatenate([cos, cos, jnp.ones((n, rest), jnp.float32)], axis=1)
    s1 = jnp.concatenate([-sin, jnp.zeros((n, half + rest), jnp.float32)], axis=1)
    s2 = jnp.concatenate([jnp.zeros((n, half), jnp.float32), sin, jnp.zeros((n, rest), jnp.float32)], axis=1)
    reps = 128 // head_dim
    tile = lambda t: jnp.tile(t, (1, reps))
    return tile(c), tile(s1), tile(s2), cos.T, sin.T


def _rope_lanes(t, c, s1, s2, half):
    n = t.shape[1]
    reps = n // 128
    tl = lambda a: jnp.concatenate([a] * reps, axis=1)
    return t * tl(c) + pltpu.roll(t, n - half, 1) * tl(s1) + pltpu.roll(t, half, 1) * tl(s2)


def _rope_rows(t, cT, sT, head_dim, half):
    pieces = []
    for h in range(t.shape[0] // head_dim):
        x1 = t[h * head_dim:h * head_dim + half]
        x2 = t[h * head_dim + half:h * head_dim + 2 * half]
        pieces += [x1 * cT - x2 * sT, x2 * cT + x1 * sT, t[h * head_dim + 2 * half:(h + 1) * head_dim]]
    return jnp.concatenate(pieces, axis=0)


def _dsa_proj_kernel(x_ref, wq_ref, wiq_ref, wv_ref, wvx_ref, wiw_ref, wkT_ref, wikT_ref,
                     c_ref, s1_ref, s2_ref, cT_ref, sT_ref, kng_ref, knb_ref, one_ref,
                     q_ref, iq_ref, v_ref, vx_ref, iw_ref, kT_ref, kTb_ref, ikT_ref, ikTb_ref):
    xb = x_ref[...].astype(MXU_DTYPE)
    c, s1, s2 = c_ref[...], s1_ref[...], s2_ref[...]
    cT, sT = cT_ref[...], sT_ref[...]
    dot = lambda a, b: jnp.dot(a, b, preferred_element_type=jnp.float32)
    dot_t = lambda w, a: lax.dot_general(w, a, (((1,), (1,)), ((), ())), preferred_element_type=jnp.float32)

    q = _rope_lanes(dot(xb, wq_ref[...]), c, s1, s2, ROPE_DIM // 2)
    q_ref[...] = (q * (ATT_HEAD_DIM ** -0.5)).astype(q_ref.dtype)
    iq = _rope_lanes(dot(xb, wiq_ref[...]), c, s1, s2, IDX_ROPE_DIM // 2)
    iq_ref[...] = iq.astype(iq_ref.dtype)
    v_ref[...] = dot(xb, wv_ref[...])
    vx_ref[...] = (dot(xb, wvx_ref[...]) + one_ref[...]).astype(vx_ref.dtype)
    iw_ref[...] = dot(xb, wiw_ref[...]) * (IDX_HEADS ** -0.5 * IDX_DIM ** -0.5)

    kT = _rope_rows(dot_t(wkT_ref[...], xb), cT, sT, ATT_HEAD_DIM, ROPE_DIM // 2)
    kT_ref[0, 0] = kT
    kTb_ref[0, 0] = kT.astype(kTb_ref.dtype)
    ikT = dot_t(wikT_ref[...], xb)
    mu = jnp.mean(ikT, axis=0, keepdims=True)
    ikc = ikT - mu
    var = jnp.mean(ikc * ikc, axis=0, keepdims=True)
    ikT = ikc * lax.rsqrt(var + LN_EPS) * kng_ref[...] + knb_ref[...]
    ikT = _rope_rows(ikT, cT, sT, IDX_DIM, IDX_ROPE_DIM // 2)
    ikT_ref[0, 0] = ikT
    ikTb_ref[0, 0] = ikT.astype(ikTb_ref.dtype)


def _dsa_project(x3d, pos, w_in, kn_g, kn_b):
    b_, t_, _ = x3d.shape
    tk = KEY_GROUP if t_ % KEY_GROUP == 0 else t_
    ng = t_ // tk
    m = b_ * t_
    w_q, w_k, w_v, w_iq, w_ik, w_iw = jnp.split(w_in, list(ATT_IN_SPLITS), axis=1)
    cast = lambda w: w.astype(MXU_DTYPE)
    w_vx = jnp.pad(w_v.reshape(D_MODEL, ATT_KV_HEADS, ATT_HEAD_DIM),
                   ((0, 0), (0, 0), (0, 128 - ATT_HEAD_DIM))).reshape(D_MODEL, ATT_KV_HEADS * 128)
    one_col = jnp.tile((jnp.arange(128) == ATT_HEAD_DIM).astype(jnp.float32), ATT_KV_HEADS)[None, :]
    w_iw_pad = jnp.pad(w_iw, ((0, 0), (0, 128 - IDX_HEADS)))
    c, s1, s2, cT, sT = _rope_lane_tables(pos, ROPE_DIM, ATT_HEAD_DIM)
    full = lambda shape: pl.BlockSpec(shape, lambda b, i: (0,) * len(shape))
    rows = lambda n: pl.BlockSpec((tk, n), lambda b, i: (b * ng + i, 0))
    ptab = lambda n: pl.BlockSpec((tk, n), lambda b, i: (i, 0))
    grp = lambda n: pl.BlockSpec((1, 1, n, tk), lambda b, i: (b, i, 0, 0))
    sds = jax.ShapeDtypeStruct
    return pl.pallas_call(
        _dsa_proj_kernel,
        grid=(b_, ng),
        in_specs=[rows(D_MODEL), full((D_MODEL, ATT_Q_DIM)), full((D_MODEL, IDX_HEADS * IDX_DIM)),
                  full((D_MODEL, ATT_KV_DIM)), full((D_MODEL, ATT_KV_HEADS * 128)), full((D_MODEL, 128)),
                  full((ATT_KV_DIM, D_MODEL)), full((IDX_DIM, D_MODEL)),
                  ptab(128), ptab(128), ptab(128),
                  pl.BlockSpec((ROPE_DIM // 2, tk), lambda b, i: (0, i)),
                  pl.BlockSpec((ROPE_DIM // 2, tk), lambda b, i: (0, i)),
                  full((IDX_DIM, 1)), full((IDX_DIM, 1)), full((1, ATT_KV_HEADS * 128))],
        out_specs=[rows(ATT_Q_DIM), rows(IDX_HEADS * IDX_DIM), rows(ATT_KV_DIM), rows(ATT_KV_HEADS * 128),
                   rows(128), grp(ATT_KV_DIM), grp(ATT_KV_DIM), grp(IDX_DIM), grp(IDX_DIM)],
        out_shape=[sds((m, ATT_Q_DIM), MXU_DTYPE), sds((m, IDX_HEADS * IDX_DIM), MXU_DTYPE),
                   sds((m, ATT_KV_DIM), jnp.float32), sds((m, ATT_KV_HEADS * 128), MXU_DTYPE),
                   sds((m, 128), jnp.float32),
                   sds((b_, ng, ATT_KV_DIM, tk), jnp.float32), sds((b_, ng, ATT_KV_DIM, tk), MXU_DTYPE),
                   sds((b_, ng, IDX_DIM, tk), jnp.float32), sds((b_, ng, IDX_DIM, tk), MXU_DTYPE)],
        compiler_params=pltpu.CompilerParams(
            dimension_semantics=("parallel", "parallel"),
            vmem_limit_bytes=V7X_VMEM_LIMIT_BYTES),
        name="dsa_project",
    )(x3d.reshape(m, D_MODEL), cast(w_q), cast(w_iq), cast(w_v), cast(w_vx), cast(w_iw_pad),
      cast(w_k.T), cast(w_ik.T), c, s1, s2, cT, sT, kn_g.reshape(IDX_DIM, 1), kn_b.reshape(IDX_DIM, 1), one_col)


def _untranspose_groups(tg):
    b_, g_, r_, tk = tg.shape
    return jnp.transpose(tg, (0, 1, 3, 2)).reshape(b_, g_ * tk, r_)


def _dsa_attend_kernel(iq_ref, iw_ref, ikT_ref, q_ref, kT_ref, vx_ref, o_ref, key_ref, m_ref, acc_ref, *,
                       topk, col_bits):
    j = pl.program_id(1)
    tq = iq_ref.shape[0]
    tk = key_ref.shape[2]
    n_groups = (j * tq + tq + tk - 1) // tk
    row = j * tq + lax.broadcasted_iota(jnp.int32, (tq, tk), 0)
    col0 = lax.broadcasted_iota(jnp.int32, (tq, tk), 1)
    dot = lambda a, b: jnp.dot(a, b, preferred_element_type=jnp.float32)

    def score_body(g, carry):
        ikT = ikT_ref[0, g]
        sc = jnp.zeros((tq, tk), jnp.float32)
        for h in range(IDX_HEADS):
            s = dot(iq_ref[:, h * IDX_DIM:(h + 1) * IDX_DIM], ikT)
            sc = sc + iw_ref[:, h:h + 1] * jnp.maximum(s, 0.0)
        bits = pltpu.bitcast(sc, jnp.int32)
        key = jnp.where(bits >= 0, bits, bits ^ jnp.int32(0x7FFFFFFF))
        key_ref[g] = jnp.where(col0 + g * tk <= row, key, jnp.int32(INT32_MIN))
        return carry

    lax.fori_loop(0, n_groups, score_body, 0)

    def bit_body(i, thr):
        cand = thr ^ lax.shift_left(jnp.int32(1), jnp.int32(31) - i)

        def count_body(g, cnt):
            hit = jnp.where(key_ref[g] >= cand, 1.0, 0.0)
            for l in range(tk // 128):
                cnt = cnt + hit[:, l * 128:(l + 1) * 128]
            return cnt

        cnt = lax.fori_loop(0, n_groups, count_body, jnp.zeros((tq, 128), jnp.float32))
        total = jnp.sum(cnt, axis=1, keepdims=True)
        return jnp.where(total >= float(topk), cand, thr)

    thr = lax.fori_loop(0, 32, bit_body, jnp.full((tq, 1), INT32_MIN, jnp.int32))

    def lane_fold(hit, cnt):
        for l in range(tk // 128):
            cnt = cnt + hit[:, l * 128:(l + 1) * 128]
        return cnt

    def above_body(g, cnt):
        return lane_fold(jnp.where(key_ref[g] > thr, 1.0, 0.0), cnt)

    n_above = jnp.sum(lax.fori_loop(0, n_groups, above_body, jnp.zeros((tq, 128), jnp.float32)),
                      axis=1, keepdims=True)
    need = float(topk) - n_above

    def col_body(i, last):
        cand = last | lax.shift_left(jnp.int32(1), jnp.int32(col_bits - 1) - i)

        def tie_body(g, cnt):
            hit = jnp.where((key_ref[g] == thr) & (col0 + g * tk < cand), 1.0, 0.0)
            return lane_fold(hit, cnt)

        ties = jnp.sum(lax.fori_loop(0, n_groups, tie_body, jnp.zeros((tq, 128), jnp.float32)),
                       axis=1, keepdims=True)
        return jnp.where(ties < need, cand, last)

    last_tie = lax.fori_loop(0, col_bits, col_body, jnp.zeros((tq, 1), jnp.int32))

    m_ref[...] = jnp.full(m_ref.shape, MASK_NEG, jnp.float32)
    acc_ref[...] = jnp.zeros(acc_ref.shape, jnp.float32)
    gsz = ATT_HEADS // ATT_KV_HEADS

    def attend_body(g, carry):
        key = key_ref[g]
        col = col0 + g * tk
        keep = (key > thr) | ((key == thr) & (col <= last_tie))
        bias = jnp.where(keep & (col <= row), 0.0, MASK_NEG)
        start = pl.multiple_of(g * tk, tk)
        for h in range(ATT_HEADS):
            kv = h // gsz
            s = dot(q_ref[:, h * ATT_HEAD_DIM:(h + 1) * ATT_HEAD_DIM],
                    kT_ref[0, g, kv * ATT_HEAD_DIM:(kv + 1) * ATT_HEAD_DIM, :]) + bias
            m_old = m_ref[h]
            m_new = jnp.maximum(m_old, jnp.max(s, axis=1, keepdims=True))
            p = jnp.exp(s - m_new).astype(vx_ref.dtype)
            pv = dot(p, vx_ref[0, pl.ds(start, tk), kv * 128:(kv + 1) * 128])
            acc_ref[h] = jnp.exp(m_old - m_new) * acc_ref[h] + pv
            m_ref[h] = m_new
        return carry

    lax.fori_loop(0, n_groups, attend_body, 0)

    for h in range(ATT_HEADS):
        a = acc_ref[h]
        o_ref[:, h * ATT_HEAD_DIM:(h + 1) * ATT_HEAD_DIM] = (
            a[:, :ATT_HEAD_DIM] / a[:, ATT_HEAD_DIM:ATT_HEAD_DIM + 1]).astype(o_ref.dtype)


def _dsa_attend(b_, t_, q, iq, iw, ikTb, kTb, vx):
    ng, tk = kTb.shape[1], kTb.shape[3]
    tq = Q_BLOCK
    nq = t_ // tq
    rows = lambda n: pl.BlockSpec((tq, n), lambda b, j: (b * nq + j, 0))
    return pl.pallas_call(
        functools.partial(_dsa_attend_kernel, topk=min(TOPK_MAX, t_ // 4), col_bits=max(1, (t_ - 1).bit_length())),
        grid=(b_, nq),
        in_specs=[rows(IDX_HEADS * IDX_DIM), rows(128),
                  pl.BlockSpec((1, ng, IDX_DIM, tk), lambda b, j: (b, 0, 0, 0)),
                  rows(ATT_Q_DIM),
                  pl.BlockSpec((1, ng, ATT_KV_DIM, tk), lambda b, j: (b, 0, 0, 0)),
                  pl.BlockSpec((1, t_, ATT_KV_HEADS * 128), lambda b, j: (b, 0, 0))],
        out_specs=rows(ATT_Q_DIM),
        out_shape=jax.ShapeDtypeStruct((b_ * t_, ATT_Q_DIM), MXU_DTYPE),
        scratch_shapes=[pltpu.VMEM((ng, tq, tk), jnp.int32),
                        pltpu.VMEM((ATT_HEADS, tq, 1), jnp.float32),
                        pltpu.VMEM((ATT_HEADS, tq, 128), jnp.float32)],
        compiler_params=pltpu.CompilerParams(
            dimension_semantics=("parallel", "arbitrary"),
            vmem_limit_bytes=V7X_VMEM_LIMIT_BYTES),
        name="dsa_attend",
    )(iq, iw, ikTb, q, kTb, vx.reshape(b_, t_, ATT_KV_HEADS * 128))


def _dsa_proj_q_lanes_kernel(x_ref, wqT_ref, wiqT_ref, wiwT_ref, wk_ref, wv_ref, wvxT_ref, wik_ref,
                             c_ref, s1_ref, s2_ref, cT_ref, sT_ref, kng_ref, knb_ref, onerow_ref,
                             qT_ref, iqT_ref, iwT_ref, k_ref, khd_ref, v_ref, vxT_ref, ik_ref, ikb_ref):
    xb = x_ref[...].astype(MXU_DTYPE)
    tm = xb.shape[0]
    c, s1, s2 = c_ref[...], s1_ref[...], s2_ref[...]
    cT, sT = cT_ref[...], sT_ref[...]
    dot = lambda a, b: jnp.dot(a, b, preferred_element_type=jnp.float32)
    dot_t = lambda w, a: lax.dot_general(w, a, (((1,), (1,)), ((), ())), preferred_element_type=jnp.float32)

    qT = _rope_rows(dot_t(wqT_ref[...], xb), cT, sT, ATT_HEAD_DIM, ROPE_DIM // 2) * (ATT_HEAD_DIM ** -0.5)
    iqT = _rope_rows(dot_t(wiqT_ref[...], xb), cT, sT, IDX_DIM, IDX_ROPE_DIM // 2)
    iwT = dot_t(wiwT_ref[...], xb) * (IDX_HEADS ** -0.5 * IDX_DIM ** -0.5)
    for t in range(tm // Q_BLOCK):
        lanes = slice(t * Q_BLOCK, (t + 1) * Q_BLOCK)
        qT_ref[0, t] = qT[:, lanes].astype(qT_ref.dtype)
        iqT_ref[0, t] = iqT[:, lanes].astype(iqT_ref.dtype)
        iwT_ref[0, t] = iwT[:, lanes]

    k = _rope_lanes(dot(xb, wk_ref[...]), c, s1, s2, ROPE_DIM // 2)
    k_ref[...] = k
    for g in range(ATT_KV_HEADS):
        khd_ref[g] = k[:, g * ATT_HEAD_DIM:(g + 1) * ATT_HEAD_DIM].astype(khd_ref.dtype)
    v_ref[...] = dot(xb, wv_ref[...])
    vxT_ref[0, 0] = (dot_t(wvxT_ref[...], xb) + onerow_ref[...]).astype(vxT_ref.dtype)

    ik = dot(xb, wik_ref[...])
    real = lax.broadcasted_iota(jnp.int32, ik.shape, 1) < IDX_DIM
    mu = jnp.sum(ik, axis=-1, keepdims=True) * (1.0 / IDX_DIM)
    ikc = jnp.where(real, ik - mu, 0.0)
    var = jnp.sum(ikc * ikc, axis=-1, keepdims=True) * (1.0 / IDX_DIM)
    ikn = _rope_lanes(ikc * lax.rsqrt(var + LN_EPS) * kng_ref[...] + knb_ref[...], c, s1, s2, IDX_ROPE_DIM // 2)
    ik_ref[...] = ikn[:, :IDX_DIM]
    ikb_ref[...] = ikn[:, :IDX_DIM].astype(ikb_ref.dtype)


def _dsa_project_q_lanes(x3d, pos, w_in, kn_g, kn_b):
    b_, t_, _ = x3d.shape
    tk = KEY_GROUP
    ng = t_ // tk
    nq = tk // Q_BLOCK
    m = b_ * t_
    w_q, w_k, w_v, w_iq, w_ik, w_iw = jnp.split(w_in, list(ATT_IN_SPLITS), axis=1)
    cast = lambda w: w.astype(MXU_DTYPE)
    w_vxT = jnp.pad(w_v.T.reshape(ATT_KV_HEADS, ATT_HEAD_DIM, D_MODEL),
                    ((0, 0), (0, 128 - ATT_HEAD_DIM), (0, 0))).reshape(ATT_KV_HEADS * 128, D_MODEL)
    one_row = jnp.tile((jnp.arange(128) == ATT_HEAD_DIM).astype(jnp.float32), ATT_KV_HEADS)[:, None]
    pad_lanes = lambda a: jnp.pad(a, ((0, 0), (0, 128 - a.shape[1])))
    c, s1, s2, cT, sT = _rope_lane_tables(pos, ROPE_DIM, ATT_HEAD_DIM)
    full = lambda shape: pl.BlockSpec(shape, lambda b, i: (0,) * len(shape))
    rows = lambda n: pl.BlockSpec((tk, n), lambda b, i: (b * ng + i, 0))
    ptab = lambda n: pl.BlockSpec((tk, n), lambda b, i: (i, 0))
    qtile = lambda n: pl.BlockSpec((1, nq, n, Q_BLOCK), lambda b, i: (b, i, 0, 0))
    sds = jax.ShapeDtypeStruct
    return pl.pallas_call(
        _dsa_proj_q_lanes_kernel,
        grid=(b_, ng),
        in_specs=[rows(D_MODEL), full((ATT_Q_DIM, D_MODEL)), full((IDX_HEADS * IDX_DIM, D_MODEL)),
                  full((IDX_HEADS, D_MODEL)), full((D_MODEL, ATT_KV_DIM)), full((D_MODEL, ATT_KV_DIM)),
                  full((ATT_KV_HEADS * 128, D_MODEL)), full((D_MODEL, 128)),
                  ptab(128), ptab(128), ptab(128),
                  pl.BlockSpec((ROPE_DIM // 2, tk), lambda b, i: (0, i)),
                  pl.BlockSpec((ROPE_DIM // 2, tk), lambda b, i: (0, i)),
                  full((1, 128)), full((1, 128)), full((ATT_KV_HEADS * 128, 1))],
        out_specs=[qtile(ATT_Q_DIM), qtile(IDX_HEADS * IDX_DIM), qtile(IDX_HEADS),
                   rows(ATT_KV_DIM), pl.BlockSpec((ATT_KV_HEADS, tk, ATT_HEAD_DIM), lambda b, i: (0, b * ng + i, 0)),
                   rows(ATT_KV_DIM), pl.BlockSpec((1, 1, ATT_KV_HEADS * 128, tk), lambda b, i: (b, i, 0, 0)),
                   rows(IDX_DIM), rows(IDX_DIM)],
        out_shape=[sds((b_, t_ // Q_BLOCK, ATT_Q_DIM, Q_BLOCK), MXU_DTYPE),
                   sds((b_, t_ // Q_BLOCK, IDX_HEADS * IDX_DIM, Q_BLOCK), MXU_DTYPE),
                   sds((b_, t_ // Q_BLOCK, IDX_HEADS, Q_BLOCK), jnp.float32),
                   sds((m, ATT_KV_DIM), jnp.float32), sds((ATT_KV_HEADS, m, ATT_HEAD_DIM), MXU_DTYPE),
                   sds((m, ATT_KV_DIM), jnp.float32), sds((b_, ng, ATT_KV_HEADS * 128, tk), MXU_DTYPE),
                   sds((m, IDX_DIM), jnp.float32), sds((m, IDX_DIM), MXU_DTYPE)],
        compiler_params=pltpu.CompilerParams(
            dimension_semantics=("parallel", "parallel"),
            vmem_limit_bytes=V7X_VMEM_LIMIT_BYTES),
        name="dsa_project_q_lanes",
    )(x3d.reshape(m, D_MODEL), cast(w_q.T), cast(w_iq.T), cast(w_iw.T), cast(w_k), cast(w_v), cast(w_vxT),
      cast(pad_lanes(w_ik)), c, s1, s2, cT, sT, pad_lanes(kn_g.reshape(1, IDX_DIM)),
      pad_lanes(kn_b.reshape(1, IDX_DIM)), one_row)


def _tree_sum(parts):
    while len(parts) > 1:
        parts = [parts[i] + parts[i + 1] for i in range(0, len(parts) - 1, 2)] + (
            [parts[-1]] if len(parts) % 2 else [])
    return parts[0]


def _dsa_attend_q_lanes_kernel(iqT_ref, iwT_ref, ik_ref, qT_ref, k_ref, vxT_ref, o_ref,
                               key_ref, bias_ref, m_ref, acc_ref, *, topk, col_bits):
    j = pl.program_id(1)
    tk, tq = key_ref.shape[1], key_ref.shape[2]
    n_groups = (j * tq + tq + tk - 1) // tk
    qpos = j * tq + lax.broadcasted_iota(jnp.int32, (tk, tq), 1)
    kpos0 = lax.broadcasted_iota(jnp.int32, (tk, tq), 0)
    dot = lambda a, b: jnp.dot(a, b, preferred_element_type=jnp.float32)

    def score_body(g, carry):
        start = pl.multiple_of(g * tk, tk)
        w_iq = jnp.concatenate([iqT_ref[0, 0, h * IDX_DIM:(h + 1) * IDX_DIM, :] for h in range(IDX_HEADS)], axis=1)
        s_all = dot(ik_ref[0, pl.ds(start, tk), :], w_iq)
        sc = _tree_sum([iwT_ref[0, 0, h:h + 1, :] * jnp.maximum(s_all[:, h * tq:(h + 1) * tq], 0.0)
                        for h in range(IDX_HEADS)])
        key_ref[g] = jnp.where(kpos0 + g * tk <= qpos, _sortable_key(sc), jnp.int32(INT32_MIN))
        return carry

    lax.fori_loop(0, n_groups, score_body, 0)

    def count_keys(pred):
        def body(g, part):
            hit = jnp.where(pred(key_ref[g], kpos0 + g * tk), 1.0, 0.0)
            return part + _tree_sum([hit[r * SUBLANES:(r + 1) * SUBLANES] for r in range(tk // SUBLANES)])
        part = lax.fori_loop(0, n_groups, body, jnp.zeros((SUBLANES, tq), jnp.float32))
        return jnp.sum(part, axis=0, keepdims=True)

    def bit_body(i, thr):
        cand = thr ^ lax.shift_left(jnp.int32(1), jnp.int32(31) - i)
        return jnp.where(count_keys(lambda k, kp: k >= cand) >= float(topk), cand, thr)

    thr = lax.fori_loop(0, 32, bit_body, jnp.full((1, tq), INT32_MIN, jnp.int32))

    need = float(topk) - count_keys(lambda k, kp: k > thr)

    def pos_body(i, last):
        cand = last | lax.shift_left(jnp.int32(1), jnp.int32(col_bits - 1) - i)
        return jnp.where(count_keys(lambda k, kp: (k == thr) & (kp < cand)) < need, cand, last)

    n_tied = count_keys(lambda k, kp: k == thr)
    excess = jnp.max(jnp.where(n_tied > need, 1.0, 0.0), axis=1, keepdims=True)
    last_tie = lax.cond(excess[0, 0] > 0.0,
                        lambda: lax.fori_loop(0, col_bits, pos_body, jnp.zeros((1, tq), jnp.int32)),
                        lambda: jnp.full((1, tq), 2 ** col_bits - 1, jnp.int32))

    m_ref[...] = jnp.full(m_ref.shape, MASK_NEG, jnp.float32)
    acc_ref[...] = jnp.zeros(acc_ref.shape, jnp.float32)
    gsz = ATT_HEADS // ATT_KV_HEADS

    def attend_body(g, carry):
        start = pl.multiple_of(g * tk, tk)
        key = key_ref[g]
        kpos = kpos0 + g * tk
        keep = (key > thr) | ((key == thr) & (kpos <= last_tie))
        bias_ref[...] = jnp.where(keep & (kpos <= qpos), 0.0, MASK_NEG)
        logits = []
        for kv in range(ATT_KV_HEADS):
            w_q = jnp.concatenate([qT_ref[0, 0, (kv * gsz + i) * ATT_HEAD_DIM:(kv * gsz + i + 1) * ATT_HEAD_DIM, :]
                                   for i in range(gsz)], axis=1)
            logits.append(dot(k_ref[kv, pl.ds(start, tk), :], w_q))
        for kv in range(ATT_KV_HEADS):
            s = logits[kv] + jnp.concatenate([bias_ref[...]] * gsz, axis=1)
            m_old = m_ref[kv]
            m_new = jnp.maximum(m_old, jnp.max(s, axis=0, keepdims=True))
            p = jnp.exp(s - m_new).astype(vxT_ref.dtype)
            pv = dot(vxT_ref[0, g, kv * 128:(kv + 1) * 128, :], p)
            acc_ref[kv] = jnp.exp(m_old - m_new) * acc_ref[kv] + pv
            m_ref[kv] = m_new
        return carry

    lax.fori_loop(0, n_groups, attend_body, 0)

    for h in range(ATT_HEADS):
        a = acc_ref[h // gsz, :, (h % gsz) * tq:(h % gsz + 1) * tq]
        o = (a / a[ATT_HEAD_DIM:ATT_HEAD_DIM + 1, :]).T
        o_ref[:, h * ATT_HEAD_DIM:(h + 1) * ATT_HEAD_DIM] = o[:, :ATT_HEAD_DIM].astype(o_ref.dtype)


def _dsa_attend_q_lanes(b_, t_, qT, iqT, iwT, ikb, khd, vxT):
    ng, tk = vxT.shape[1], vxT.shape[3]
    tq = Q_BLOCK
    nq = t_ // tq
    qtile = lambda n: pl.BlockSpec((1, 1, n, tq), lambda b, j: (b, j, 0, 0))
    return pl.pallas_call(
        functools.partial(_dsa_attend_q_lanes_kernel, topk=min(TOPK_MAX, t_ // 4),
                          col_bits=max(1, (t_ - 1).bit_length())),
        grid=(b_, nq),
        in_specs=[qtile(IDX_HEADS * IDX_DIM), qtile(IDX_HEADS),
                  pl.BlockSpec((1, t_, IDX_DIM), lambda b, j: (b, 0, 0)),
                  qtile(ATT_Q_DIM),
                  pl.BlockSpec((ATT_KV_HEADS, t_, ATT_HEAD_DIM), lambda b, j: (0, b, 0)),
                  pl.BlockSpec((1, ng, ATT_KV_HEADS * 128, tk), lambda b, j: (b, 0, 0, 0))],
        out_specs=pl.BlockSpec((tq, ATT_Q_DIM), lambda b, j: (b * nq + j, 0)),
        out_shape=jax.ShapeDtypeStruct((b_ * t_, ATT_Q_DIM), MXU_DTYPE),
        scratch_shapes=[pltpu.VMEM((ng, tk, tq), jnp.int32),
                        pltpu.VMEM((tk, tq), jnp.float32),
                        pltpu.VMEM((ATT_KV_HEADS, 1, tq * (ATT_HEADS // ATT_KV_HEADS)), jnp.float32),
                        pltpu.VMEM((ATT_KV_HEADS, 128, tq * (ATT_HEADS // ATT_KV_HEADS)), jnp.float32)],
        compiler_params=pltpu.CompilerParams(
            dimension_semantics=("parallel", "arbitrary"),
            vmem_limit_bytes=V7X_VMEM_LIMIT_BYTES),
        name="dsa_attend_q_lanes",
    )(iqT, iwT, ikb.reshape(b_, t_, IDX_DIM), qT, khd, vxT)


def _dsa_prompt_pallas(x3d, w_in, kn_g, kn_b):
    b_, t_, _ = x3d.shape
    qT, iqT, iwT, k, khd, v, vxT, ik, ikb = _dsa_project_q_lanes(x3d, jnp.arange(t_), w_in, kn_g, kn_b)
    o = _dsa_attend_q_lanes(b_, t_, qT, iqT, iwT, ikb, khd, vxT)
    kv4 = lambda u: u.reshape(b_, t_, ATT_KV_HEADS, ATT_HEAD_DIM)
    return o, kv4(k), kv4(v), ik.reshape(b_, t_, IDX_DIM)


RW_ROW_TILE = 256


RW_PAIRS = RW_HEADS // 2
RW_PAIR_LANES = 2 * RW_HEAD


def _rwkv_project_rows(x, xp, mu_ref, wr_ref, wk_ref, wv_ref, w1_ref, w2_ref, a1_ref, a2_ref,
                       g1_ref, g2_ref, w0_ref, a0_ref):
    dx = xp - x
    mix = lambda c: (x + dx * mu_ref[c:c + 1, :]).astype(MXU_DTYPE)
    dot = lambda a, b: jnp.dot(a.astype(MXU_DTYPE), b, preferred_element_type=jnp.float32)
    r = dot(mix(0), wr_ref[...])
    lora_w = dot(jnp.tanh(dot(mix(1), w1_ref[...])), w2_ref[...])
    w_log = -jax.nn.softplus(-(w0_ref[...] + lora_w)) - 0.5
    d = jnp.exp(-jnp.exp(w_log))
    k = dot(mix(2), wk_ref[...])
    v = dot(mix(3), wv_ref[...])
    a = jax.nn.sigmoid(a0_ref[...] + dot(dot(mix(4), a1_ref[...]), a2_ref[...]))
    g = dot(jax.nn.sigmoid(dot(mix(5), g1_ref[...])), g2_ref[...])
    return r, d, k, v, a, g


def _rwkv_proj_step_kernel(x_ref, xp_ref, *refs):
    vals = _rwkv_project_rows(x_ref[...], xp_ref[...], *refs[:12])
    for ref, val in zip(refs[12:], vals):
        ref[...] = val


def _rwkv_proj_seq_kernel(x_ref, halo_ref, shift_ref, *refs):
    i = pl.program_id(1)
    x = x_ref[...]
    prev = jnp.where(i == 0, shift_ref[0], halo_ref[...])[SUBLANES - 1:SUBLANES, :]
    first = lax.broadcasted_iota(jnp.int32, (x.shape[0], 1), 0) == 0
    xp = jnp.where(first, prev, pltpu.roll(x, 1, 0))
    vals = _rwkv_project_rows(x, xp, *refs[:12])
    for ref, val in zip(refs[12:], vals):
        ref[...] = val


def _rwkv_consts(mu, w_r, w_k, w_v, w0, w1, w2, a0, a1, a2, g1, g2):
    cast = lambda w: w.astype(MXU_DTYPE)
    return [mu, cast(w_r), cast(w_k), cast(w_v), cast(w1), cast(w2), cast(a1), cast(a2), cast(g1), cast(g2),
            w0.reshape(1, D_MODEL), a0.reshape(1, D_MODEL)]


def _rwkv_project_step(x2d, xprev2d, *params):
    m = x2d.shape[0]
    consts = _rwkv_consts(*params)
    full = lambda a: pl.BlockSpec(a.shape, lambda i: (0,) * a.ndim)
    rows = pl.BlockSpec((m, D_MODEL), lambda i: (0, 0))
    return pl.pallas_call(
        _rwkv_proj_step_kernel,
        grid=(1,),
        in_specs=[rows, rows] + [full(a) for a in consts],
        out_specs=[rows] * 6,
        out_shape=[jax.ShapeDtypeStruct((m, D_MODEL), jnp.float32)] * 6,
        compiler_params=pltpu.CompilerParams(
            dimension_semantics=("arbitrary",),
            vmem_limit_bytes=V7X_VMEM_LIMIT_BYTES),
        name="rwkv_project_step",
    )(x2d, xprev2d, *consts)


def _rwkv_project_seq(x3d, shift, *params):
    b_, t_, _ = x3d.shape
    m = b_ * t_
    tm = RW_ROW_TILE
    nt = t_ // tm
    consts = _rwkv_consts(*params)
    full = lambda a: pl.BlockSpec(a.shape, lambda b, i: (0,) * a.ndim)
    rows = pl.BlockSpec((tm, D_MODEL), lambda b, i: (b * nt + i, 0))
    halo = pl.BlockSpec((SUBLANES, D_MODEL), lambda b, i: (jnp.maximum((b * nt + i) * (tm // SUBLANES) - 1, 0), 0))
    shift8 = jnp.pad(shift[:, None, :], ((0, 0), (SUBLANES - 1, 0), (0, 0)))
    x2d = x3d.reshape(m, D_MODEL)
    return pl.pallas_call(
        _rwkv_proj_seq_kernel,
        grid=(b_, nt),
        in_specs=[rows, halo, pl.BlockSpec((1, SUBLANES, D_MODEL), lambda b, i: (b, 0, 0))]
        + [full(a) for a in consts],
        out_specs=[rows] * 6,
        out_shape=[jax.ShapeDtypeStruct((m, D_MODEL), jnp.float32)] * 6,
        compiler_params=pltpu.CompilerParams(
            dimension_semantics=("parallel", "parallel"),
            vmem_limit_bytes=V7X_VMEM_LIMIT_BYTES),
        name="rwkv_project_seq",
    )(x2d, x2d, shift8, *consts)


RW_LANES = 128
RW_TIME_CHUNK = 32


def _rwkv_scan_kernel(r_ref, d_ref, k_ref, v_ref, a_ref, s0_ref, kk_ref, ka_ref, rk_ref, gg_ref, gb_ref,
                      z_ref, s_out_ref, s_ref, vec_ref):
    c = pl.program_id(1)
    n = RW_HEAD
    tc = r_ref.shape[1]
    low_half = lax.broadcasted_iota(jnp.int32, (n, RW_LANES), 1) < n

    @pl.when(c == 0)
    def _():
        s_ref[...] = s0_ref[...]

    def swap_layout(x):
        xt = jnp.concatenate([x, x], axis=0).T
        return jnp.where(low_half, xt[:n], xt[n:])

    def load_step(ref, t):
        rows = ref[:, t, :]
        return swap_layout(jnp.concatenate(
            [rows[:, p * RW_PAIR_LANES:(p + 1) * RW_PAIR_LANES] for p in range(RW_PAIRS)], axis=0))

    def store_step(ref, t, val):
        tile = swap_layout(val)
        ref[:, t, :] = jnp.concatenate(
            [tile[p * RW_SEQ_PER_TILE:(p + 1) * RW_SEQ_PER_TILE] for p in range(RW_PAIRS)], axis=1)

    def prepare(t, slot):
        r, k, a = load_step(r_ref, t), load_step(k_ref, t), load_step(a_ref, t)
        kkr = k * kk_ref[...]
        nrm = jnp.sqrt(jnp.sum(kkr * kkr, axis=0, keepdims=True))
        kk = kkr / jnp.maximum(nrm, 1e-12)
        vec_ref[slot, 0] = kk
        vec_ref[slot, 1] = load_step(d_ref, t)
        vec_ref[slot, 2] = kk * a
        vec_ref[slot, 3] = k * (1.0 + (a - 1.0) * ka_ref[...])
        vec_ref[slot, 4] = r
        vec_ref[slot, 5] = load_step(v_ref, t)

    def step(t, slot):
        row = lambda q, j: vec_ref[slot, q, j:j + 1, :]
        v = vec_ref[slot, 5]
        lanes = 4
        sa_parts = [s_ref[j] * row(0, j) for j in range(lanes)]
        for j in range(lanes, n):
            sa_parts[j % lanes] = sa_parts[j % lanes] + s_ref[j] * row(0, j)
        sa = _tree_sum(sa_parts)
        y_parts = []
        for j in range(n):
            sn = s_ref[j] * row(1, j) - sa * row(2, j) + v * row(3, j)
            s_ref[j] = sn
            if j < lanes:
                y_parts.append(sn * row(4, j))
            else:
                y_parts[j % lanes] = y_parts[j % lanes] + sn * row(4, j)
        y = _tree_sum(y_parts)
        mu = jnp.mean(y, axis=0, keepdims=True)
        yc = y - mu
        var = jnp.mean(yc * yc, axis=0, keepdims=True)
        bonus = jnp.sum(vec_ref[slot, 4] * vec_ref[slot, 3] * rk_ref[...], axis=0, keepdims=True)
        store_step(z_ref, t, yc * lax.rsqrt(var + RW_GN_EPS) * gg_ref[...] + gb_ref[...] + bonus * v)

    prepare(0, 0)
    if tc == 1:
        step(0, 0)
    else:
        def two_steps(i, carry):
            t = 2 * i
            prepare(t + 1, 1)
            step(t, 0)
            prepare(jnp.minimum(t + 2, tc - 1), 0)
            step(t + 1, 1)
            return carry

        lax.fori_loop(0, tc // 2, two_steps, 0)

    @pl.when(c == pl.num_programs(1) - 1)
    def _():
        s_out_ref[...] = s_ref[...]


RW_SEQ_PER_TILE = RW_LANES // RW_HEADS


def _rwkv_lane_heads():
    half = jnp.arange(2)[:, None, None]
    pair = jnp.arange(RW_PAIRS)[None, :, None]
    return jnp.broadcast_to(2 * pair + half, (2, RW_PAIRS, RW_SEQ_PER_TILE)).reshape(RW_LANES)


def _rwkv_scan(r, d, k, v, a, s0, k_k, k_a, r_k, gn_g, gn_b):
    b_, t_, _ = r.shape
    n = RW_HEAD
    tc = RW_TIME_CHUNK if t_ % RW_TIME_CHUNK == 0 else t_
    table = lambda p: p.reshape(RW_HEADS, n)[_rwkv_lane_heads()].T
    seq = pl.BlockSpec((RW_SEQ_PER_TILE, tc, D_MODEL), lambda l, c: (l, c, 0))
    state = pl.BlockSpec((n, n, RW_LANES), lambda l, c: (0, 0, l))
    tab = pl.BlockSpec((n, RW_LANES), lambda l, c: (0, 0))
    return pl.pallas_call(
        _rwkv_scan_kernel,
        grid=(b_ // RW_SEQ_PER_TILE, t_ // tc),
        in_specs=[seq] * 5 + [state] + [tab] * 5,
        out_specs=[seq, state],
        out_shape=[jax.ShapeDtypeStruct(r.shape, jnp.float32),
                   jax.ShapeDtypeStruct(s0.shape, jnp.float32)],
        scratch_shapes=[pltpu.VMEM((n, n, RW_LANES), jnp.float32),
                        pltpu.VMEM((2, 6, n, RW_LANES), jnp.float32)],
        compiler_params=pltpu.CompilerParams(
            dimension_semantics=("parallel", "arbitrary"),
            vmem_limit_bytes=V7X_VMEM_LIMIT_BYTES),
        name="rwkv_scan",
    )(r, d, k, v, a, s0, table(k_k), table(k_a), table(r_k), table(gn_g), table(gn_b))


def _rwkv_state_to_lanes(wkv):
    b_ = wkv.shape[0]
    w = wkv.astype(jnp.float32).reshape(b_ // RW_SEQ_PER_TILE, RW_SEQ_PER_TILE, RW_PAIRS, 2, RW_HEAD, RW_HEAD)
    return jnp.transpose(w, (5, 4, 0, 3, 2, 1)).reshape(RW_HEAD, RW_HEAD, b_ * RW_HEADS)


def _rwkv_state_from_lanes(s, b_):
    w = s.reshape(RW_HEAD, RW_HEAD, b_ // RW_SEQ_PER_TILE, 2, RW_PAIRS, RW_SEQ_PER_TILE)
    return jnp.transpose(w, (2, 5, 4, 3, 1, 0)).reshape(b_, RW_HEADS, RW_HEAD, RW_HEAD)


def _rwkv7_mixer_pallas(x3d, shift, wkv, mu, w_r, w_k, w_v, w0, w1, w2, a0, a1, a2, g1, g2,
                        k_k, k_a, r_k, gn_g, gn_b):
    b_, t_, _ = x3d.shape
    params = (mu, w_r, w_k, w_v, w0, w1, w2, a0, a1, a2, g1, g2)
    if t_ == 1:
        *seqs, g = _rwkv_project_step(x3d.reshape(b_, D_MODEL), shift, *params)
    else:
        *seqs, g = _rwkv_project_seq(x3d, shift, *params)
    seqs = [u.reshape(b_, t_, D_MODEL) for u in seqs]
    z, s = _rwkv_scan(*seqs, _rwkv_state_to_lanes(wkv), k_k, k_a, r_k, gn_g, gn_b)
    return z.reshape(b_ * t_, D_MODEL), g, x3d[:, -1], _rwkv_state_from_lanes(s, b_).astype(wkv.dtype)


def _proj_gate_ln_kernel(x_ref, h_ref, gate_ref, w_ref, g_ref, b_ref, o_ref):
    h = (h_ref[...] * gate_ref[...]).astype(MXU_DTYPE)
    y = ALPHA * x_ref[...] + jnp.dot(h, w_ref[...], preferred_element_type=jnp.float32)
    o_ref[...] = _ln_rows(y, g_ref[...], b_ref[...])


def _proj_gate_post_norm(x2d, h2d, gate2d, w_out, g, b):
    m = x2d.shape[0]
    tm = _row_tile(m)
    rows = pl.BlockSpec((tm, D_MODEL), lambda i: (i, 0))
    vec = pl.BlockSpec((1, D_MODEL), lambda i: (0, 0))
    return pl.pallas_call(
        _proj_gate_ln_kernel,
        grid=(m // tm,),
        in_specs=[rows, rows, rows, pl.BlockSpec((D_MODEL, D_MODEL), lambda i: (0, 0)), vec, vec],
        out_specs=rows,
        out_shape=jax.ShapeDtypeStruct((m, D_MODEL), jnp.float32),
        compiler_params=pltpu.CompilerParams(
            dimension_semantics=("parallel",),
            vmem_limit_bytes=V7X_VMEM_LIMIT_BYTES),
        name="proj_gate_post_norm",
    )(x2d, h2d, gate2d, w_out, g.reshape(1, D_MODEL), b.reshape(1, D_MODEL))


GM_ROW_TILE = 256


def _gmlp_kernel(x_ref, win_ref, lng_ref, lnb_ref, mixw_ref, mixb_ref, wout_ref, g_ref, b_ref, *out_refs,
                 chunk_len, emit_v):
    x = x_ref[...]
    h = jax.nn.gelu(jnp.dot(x.astype(MXU_DTYPE), win_ref[...], preferred_element_type=jnp.float32))
    u = h[:, :GM_WIDTH]
    v = _ln_rows(h[:, GM_WIDTH:], lng_ref[...], lnb_ref[...])
    if emit_v:
        out_refs[1][...] = v
    if chunk_len == 1:
        gated = u * (v * mixw_ref[...] + mixb_ref[...])
    else:
        tm = x.shape[0]
        causal = (lax.broadcasted_iota(jnp.int32, (chunk_len, chunk_len), 0)
                  >= lax.broadcasted_iota(jnp.int32, (chunk_len, chunk_len), 1))
        vb = v.astype(MXU_DTYPE)
        cols = []
        for g in range(GM_GROUPS):
            w = jnp.where(causal, mixw_ref[g], 0.0).astype(MXU_DTYPE)
            bias = mixb_ref[:, g:g + 1]
            lanes = slice(g * GM_GROUP_DIM, (g + 1) * GM_GROUP_DIM)
            rows = [jnp.dot(w, vb[c * chunk_len:(c + 1) * chunk_len, lanes],
                            preferred_element_type=jnp.float32) + bias
                    for c in range(tm // chunk_len)]
            cols.append(jnp.concatenate(rows, axis=0))
        gated = u * jnp.concatenate(cols, axis=1)
    y = ALPHA * x + jnp.dot(gated.astype(MXU_DTYPE), wout_ref[...], preferred_element_type=jnp.float32)
    out_refs[0][...] = _ln_rows(y, g_ref[...], b_ref[...])


def _gmlp_block(x2d, seq_len, w_in, ln_g, ln_b, ws, bs, w_out, g, b, emit_v):
    m = x2d.shape[0]
    chunk_len = min(seq_len, CHUNK)
    if chunk_len == 1:
        tm = m
        mixw = jnp.repeat(ws[:, 0, 0], GM_GROUP_DIM)[None, :]
        mixb = jnp.repeat(bs[:, 0], GM_GROUP_DIM)[None, :]
    else:
        tm = GM_ROW_TILE
        mixw = ws[:, :chunk_len, :chunk_len]
        mixb = bs[:, :chunk_len].T
    full = lambda a: pl.BlockSpec(a.shape, lambda i: (0,) * a.ndim)
    rows = lambda n: pl.BlockSpec((tm, n), lambda i: (i, 0))
    consts = [w_in.astype(MXU_DTYPE), ln_g.reshape(1, GM_WIDTH), ln_b.reshape(1, GM_WIDTH), mixw, mixb,
              w_out.astype(MXU_DTYPE), g.reshape(1, D_MODEL), b.reshape(1, D_MODEL)]
    out_specs = [rows(D_MODEL)] + ([rows(GM_WIDTH)] if emit_v else [])
    out_shape = [jax.ShapeDtypeStruct((m, D_MODEL), jnp.float32)] + (
        [jax.ShapeDtypeStruct((m, GM_WIDTH), jnp.float32)] if emit_v else [])
    return pl.pallas_call(
        functools.partial(_gmlp_kernel, chunk_len=chunk_len, emit_v=emit_v),
        grid=(m // tm,),
        in_specs=[rows(D_MODEL)] + [full(a) for a in consts],
        out_specs=out_specs,
        out_shape=out_shape,
        compiler_params=pltpu.CompilerParams(
            dimension_semantics=("parallel",),
            vmem_limit_bytes=V7X_VMEM_LIMIT_BYTES),
        name="gmlp_block",
    )(x2d, *consts)


SSM_ROW_TILE = 256
SSM_BC_DIM = SSM_GROUPS * SSM_STATE
SSM_DT_LANES = 128
SUBLANES = 8


def _ssm_activate(xb, xbc, taps, wz_ref, wdt_ref, cw_ref, cb_ref, dtb_ref, z_ref, xs_ref, bm_ref, cm_ref, dt_ref):
    conv = cb_ref[...] + xbc * cw_ref[SSM_CONV - 1:SSM_CONV, :]
    for j in range(SSM_CONV - 1):
        conv = conv + taps[j] * cw_ref[j:j + 1, :]
    act = conv * jax.nn.sigmoid(conv)
    xs_ref[...] = act[:, :SSM_D_INNER]
    bm_ref[...] = act[:, SSM_D_INNER:SSM_D_INNER + SSM_BC_DIM].astype(bm_ref.dtype)
    cm_ref[...] = act[:, SSM_D_INNER + SSM_BC_DIM:].astype(cm_ref.dtype)
    z_ref[...] = jnp.dot(xb, wz_ref[...], preferred_element_type=jnp.float32)
    dt_ref[...] = jax.nn.softplus(jnp.dot(xb, wdt_ref[...], preferred_element_type=jnp.float32) + dtb_ref[...])


def _ssm_proj_seq_kernel(x_ref, halo_ref, cs_ref, wx_ref, wz_ref, wdt_ref, cw_ref, cb_ref, dtb_ref,
                         z_ref, xs_ref, bm_ref, cm_ref, dt_ref, tail_ref):
    i = pl.program_id(1)
    xb = x_ref[...].astype(MXU_DTYPE)
    xbc = jnp.dot(xb, wx_ref[...], preferred_element_type=jnp.float32)
    tm = xbc.shape[0]
    prev = jnp.dot(halo_ref[...].astype(MXU_DTYPE), wx_ref[...], preferred_element_type=jnp.float32)
    prev = jnp.where(i == 0, cs_ref[0], prev)
    row = lax.broadcasted_iota(jnp.int32, (SUBLANES, 1), 0)
    taps = []
    for j in range(SSM_CONV - 1):
        back = SSM_CONV - 1 - j
        rolled = pltpu.roll(xbc, back, 0)
        top = jnp.where(row < back, pltpu.roll(prev, back, 0), rolled[:SUBLANES])
        taps.append(jnp.concatenate([top, rolled[SUBLANES:]], axis=0))
    _ssm_activate(xb, xbc, taps, wz_ref, wdt_ref, cw_ref, cb_ref, dtb_ref, z_ref, xs_ref, bm_ref, cm_ref, dt_ref)
    tail_ref[0] = xbc[tm - SUBLANES:, :]


def _ssm_proj_step_kernel(x_ref, st_ref, wx_ref, wz_ref, wdt_ref, cw_ref, cb_ref, dtb_ref,
                          z_ref, xs_ref, bm_ref, cm_ref, dt_ref, st_out_ref):
    xb = x_ref[...].astype(MXU_DTYPE)
    xbc = jnp.dot(xb, wx_ref[...], preferred_element_type=jnp.float32)
    taps = [st_ref[j] for j in range(SSM_CONV - 1)]
    _ssm_activate(xb, xbc, taps, wz_ref, wdt_ref, cw_ref, cb_ref, dtb_ref, z_ref, xs_ref, bm_ref, cm_ref, dt_ref)
    for j in range(SSM_CONV - 2):
        st_out_ref[j] = st_ref[j + 1]
    st_out_ref[SSM_CONV - 2] = xbc


def _ssm_project(x3d, conv_state, w_in, conv_w, conv_b, dt_bias):
    b_, t_, _ = x3d.shape
    m = b_ * t_
    w_z, w_x, w_dt = jnp.split(w_in, [SSM_D_INNER, SSM_D_INNER + SSM_CONV_DIM], axis=1)
    cast = lambda w: w.astype(MXU_DTYPE)
    consts = [cast(w_x), cast(w_z), cast(jnp.pad(w_dt, ((0, 0), (0, SSM_DT_LANES - SSM_HEADS)))),
              conv_w, conv_b.reshape(1, SSM_CONV_DIM),
              jnp.pad(dt_bias, (0, SSM_DT_LANES - SSM_HEADS)).reshape(1, SSM_DT_LANES)]
    sds = jax.ShapeDtypeStruct
    outs = [sds((m, SSM_D_INNER), jnp.float32), sds((m, SSM_D_INNER), jnp.float32),
            sds((m, SSM_BC_DIM), MXU_DTYPE), sds((m, SSM_BC_DIM), MXU_DTYPE), sds((m, SSM_DT_LANES), jnp.float32)]
    widths = [SSM_D_INNER, SSM_D_INNER, SSM_BC_DIM, SSM_BC_DIM, SSM_DT_LANES]
    params = dict(vmem_limit_bytes=V7X_VMEM_LIMIT_BYTES)
    x2d = x3d.reshape(m, D_MODEL)
    if t_ == 1:
        full = lambda a: pl.BlockSpec(a.shape, lambda i: (0,) * a.ndim)
        st = jnp.transpose(conv_state, (1, 0, 2))
        res = pl.pallas_call(
            _ssm_proj_step_kernel,
            grid=(1,),
            in_specs=[full(x2d), full(st)] + [full(a) for a in consts],
            out_specs=[pl.BlockSpec((m, w), lambda i: (0, 0)) for w in widths] + [full(st)],
            out_shape=outs + [sds(st.shape, jnp.float32)],
            compiler_params=pltpu.CompilerParams(dimension_semantics=("arbitrary",), **params),
            name="ssm_project_step",
        )(x2d, st, *consts)
        return list(res[:5]) + [jnp.transpose(res[5], (1, 0, 2))]
    tm = SSM_ROW_TILE
    nt = t_ // tm
    full = lambda a: pl.BlockSpec(a.shape, lambda b, i: (0,) * a.ndim)
    rows = lambda w: pl.BlockSpec((tm, w), lambda b, i: (b * nt + i, 0))
    halo = pl.BlockSpec((SUBLANES, D_MODEL), lambda b, i: (jnp.maximum((b * nt + i) * (tm // SUBLANES) - 1, 0), 0))
    cs8 = jnp.pad(conv_state, ((0, 0), (SUBLANES - (SSM_CONV - 1), 0), (0, 0)))
    tail = pl.BlockSpec((1, SUBLANES, SSM_CONV_DIM), lambda b, i: (b, 0, 0))
    res = pl.pallas_call(
        _ssm_proj_seq_kernel,
        grid=(b_, nt),
        in_specs=[rows(D_MODEL), halo, tail] + [full(a) for a in consts],
        out_specs=[rows(w) for w in widths] + [tail],
        out_shape=outs + [sds((b_, SUBLANES, SSM_CONV_DIM), jnp.float32)],
        compiler_params=pltpu.CompilerParams(dimension_semantics=("parallel", "arbitrary"), **params),
        name="ssm_project_seq",
    )(x2d, x2d, cs8, *consts)
    return list(res[:5]) + [res[5][:, SUBLANES - (SSM_CONV - 1):, :]]


def _ssm_gate_norm(y, xs, z, dskip, normg):
    yg = (y + xs * dskip) * (z * jax.nn.sigmoid(z))
    gw = SSM_D_INNER // SSM_GROUPS
    outs = []
    for g in range(SSM_GROUPS):
        part = yg[:, g * gw:(g + 1) * gw]
        ms = jnp.mean(part * part, axis=-1, keepdims=True)
        outs.append(part * lax.rsqrt(ms + LN_EPS))
    return jnp.concatenate(outs, axis=1) * normg


def _ssm_chunk_kernel(xs_ref, bm_ref, cm_ref, dt_ref, z_ref, aneg_ref, dskip_ref, normg_ref,
                      yg_ref, h_out_ref, h_ref, yT_ref, xe_ref):
    c = pl.program_id(1)
    l = xs_ref.shape[0]
    hd = SSM_HEAD_DIM

    @pl.when(c == 0)
    def _():
        h_ref[...] = jnp.zeros_like(h_ref)

    dot = lambda u, w: jnp.dot(u, w, preferred_element_type=jnp.float32)
    dt = dt_ref[...]
    a = dt * aneg_ref[...]
    r_i = lax.broadcasted_iota(jnp.int32, (l, l), 0)
    c_i = lax.broadcasted_iota(jnp.int32, (l, l), 1)
    tril = jnp.where(r_i >= c_i, 1.0, 0.0)
    hi = lax.Precision.HIGHEST
    acum = jnp.dot(tril, a, precision=hi, preferred_element_type=jnp.float32)
    acum_t = jnp.dot(a.T, tril.T, precision=hi, preferred_element_type=jnp.float32)
    dt_t = dt.T
    to_end_t = jnp.exp(acum_t[:, l - 1:l] - acum_t)
    from_start_t = jnp.exp(acum_t)
    chunk_decay = jnp.exp(acum[l - 1:l, :])
    upper = r_i <= c_i
    xs = xs_ref[...]
    for g in range(SSM_GROUPS):
        bm = bm_ref[:, g * SSM_STATE:(g + 1) * SSM_STATE]
        cm_t = cm_ref[:, g * SSM_STATE:(g + 1) * SSM_STATE].astype(jnp.float32).T.astype(MXU_DTYPE)
        cb_t = dot(bm, cm_t)
        h_in = h_ref[g * SSM_HPG:(g + 1) * SSM_HPG].reshape(SSM_HPG * hd, SSM_STATE)
        y_off = dot(h_in.astype(MXU_DTYPE), cm_t)
        for e in range(SSM_HPG):
            h = g * SSM_HPG + e
            if h % 2 == 0:
                xs_pair_t = xs[:, h * hd:(h + 2) * hd].T
            xdt_t = xs_pair_t[(h % 2) * hd:(h % 2 + 1) * hd] * dt_t[h:h + 1, :]
            seg = jnp.exp(jnp.where(upper, acum_t[h:h + 1, :] - acum[:, h:h + 1], -jnp.inf))
            y_diag = dot(xdt_t.astype(MXU_DTYPE), (cb_t * seg).astype(MXU_DTYPE))
            yT_ref[h * hd:(h + 1) * hd, :] = y_diag + y_off[e * hd:(e + 1) * hd] * from_start_t[h:h + 1, :]
            xe_ref[e * hd:(e + 1) * hd, :] = (xdt_t * to_end_t[h:h + 1, :]).astype(xe_ref.dtype)
        states = dot(xe_ref[...], bm)
        for e in range(SSM_HPG):
            h = g * SSM_HPG + e
            h_ref[h] = h_ref[h] * chunk_decay[:, h:h + 1] + states[e * hd:(e + 1) * hd]
    y = jnp.concatenate([yT_ref[i * l:(i + 1) * l, :].T for i in range(SSM_D_INNER // l)], axis=1)
    yg_ref[...] = _ssm_gate_norm(y, xs, z_ref[...], dskip_ref[...], normg_ref[...]).astype(yg_ref.dtype)

    @pl.when(c == pl.num_programs(1) - 1)
    def _():
        h_out_ref[0] = h_ref[...]


def _ssm_head_lanes(p):
    return jnp.pad(p.astype(jnp.float32), (0, SSM_DT_LANES - SSM_HEADS)).reshape(1, SSM_DT_LANES)


def _ssm_chunk_scan(b_, t_, xs, bm, cm, dt, z, a_log, d_skip, norm_g):
    l = SSM_CHUNK
    nc = t_ // l
    rows = lambda w: pl.BlockSpec((l, w), lambda b, c: (b * nc + c, 0))
    vec = lambda w: pl.BlockSpec((1, w), lambda b, c: (0, 0))
    aneg = _ssm_head_lanes(-jnp.exp(a_log.astype(jnp.float32)))
    dskip = jnp.repeat(d_skip, SSM_HEAD_DIM).reshape(1, SSM_D_INNER)
    yg, h_new = pl.pallas_call(
        _ssm_chunk_kernel,
        grid=(b_, nc),
        in_specs=[rows(SSM_D_INNER), rows(SSM_BC_DIM), rows(SSM_BC_DIM), rows(SSM_DT_LANES), rows(SSM_D_INNER),
                  vec(SSM_DT_LANES), vec(SSM_D_INNER), vec(SSM_D_INNER)],
        out_specs=[rows(SSM_D_INNER),
                   pl.BlockSpec((1, SSM_HEADS, SSM_HEAD_DIM, SSM_STATE), lambda b, c: (b, 0, 0, 0))],
        out_shape=[jax.ShapeDtypeStruct((b_ * t_, SSM_D_INNER), MXU_DTYPE),
                   jax.ShapeDtypeStruct((b_, SSM_HEADS, SSM_HEAD_DIM, SSM_STATE), jnp.float32)],
        scratch_shapes=[pltpu.VMEM((SSM_HEADS, SSM_HEAD_DIM, SSM_STATE), jnp.float32),
                        pltpu.VMEM((SSM_D_INNER, l), jnp.float32),
                        pltpu.VMEM((SSM_HPG * SSM_HEAD_DIM, l), MXU_DTYPE)],
        compiler_params=pltpu.CompilerParams(
            dimension_semantics=("parallel", "arbitrary"),
            vmem_limit_bytes=V7X_VMEM_LIMIT_BYTES),
        name="ssm_chunk_scan",
    )(xs, bm, cm, dt, z, aneg, dskip, norm_g.reshape(1, SSM_D_INNER))
    return yg, h_new


def _ssm_step_kernel(h0_ref, xs_ref, dt_ref, an_ref, bm_ref, cm_ref, y_ref, h_ref):
    h0 = h0_ref[0]
    dt = dt_ref[0]
    decay = jnp.exp(dt * an_ref[...])
    xdt = xs_ref[0] * dt
    bm = bm_ref[0].astype(jnp.float32)
    cm = cm_ref[0].astype(jnp.float32)
    h_ref[0] = h0 * decay + xdt * bm
    cb = jnp.sum(cm * bm, axis=-1, keepdims=True)
    y_ref[0] = cb * xdt + jnp.sum(cm * h0, axis=-1, keepdims=True) * decay


def _ssm_step(state, xs, bm, cm, dt, a_log):
    b_ = state.shape[0]
    per_head = lambda u: jnp.repeat(u.reshape(b_, SSM_GROUPS, 1, SSM_STATE), SSM_HPG, axis=1)
    xs4 = xs.reshape(b_, SSM_HEADS, SSM_HEAD_DIM, 1)
    dt4 = dt[:, :SSM_HEADS].reshape(b_, SSM_HEADS, 1, 1)
    an = (-jnp.exp(a_log.astype(jnp.float32))).reshape(SSM_HEADS, 1, 1)
    blk = lambda a: pl.BlockSpec((1,) + a.shape[1:], lambda b: (b, 0, 0, 0))
    args = [state.astype(jnp.float32), xs4, dt4, an, per_head(bm), per_head(cm)]
    y4, h_new = pl.pallas_call(
        _ssm_step_kernel,
        grid=(b_,),
        in_specs=[blk(args[0]), blk(xs4), blk(dt4), pl.BlockSpec(an.shape, lambda b: (0, 0, 0)),
                  blk(args[4]), blk(args[5])],
        out_specs=[blk(xs4), blk(args[0])],
        out_shape=[jax.ShapeDtypeStruct(xs4.shape, jnp.float32), jax.ShapeDtypeStruct(state.shape, jnp.float32)],
        compiler_params=pltpu.CompilerParams(
            dimension_semantics=("parallel",),
            vmem_limit_bytes=V7X_VMEM_LIMIT_BYTES),
        name="ssm_step",
    )(*args)
    return y4.reshape(b_, SSM_D_INNER), h_new


def _ssm_gate_norm_kernel(y_ref, xs_ref, z_ref, dskip_ref, normg_ref, o_ref):
    o_ref[...] = _ssm_gate_norm(y_ref[...], xs_ref[...], z_ref[...], dskip_ref[...], normg_ref[...]).astype(o_ref.dtype)


def _ssm_gate_norm_rows(y, xs, z, d_skip, norm_g):
    full = lambda a: pl.BlockSpec(a.shape, lambda i: (0,) * a.ndim)
    args = [y, xs, z, jnp.repeat(d_skip, SSM_HEAD_DIM).reshape(1, SSM_D_INNER), norm_g.reshape(1, SSM_D_INNER)]
    return pl.pallas_call(
        _ssm_gate_norm_kernel,
        grid=(1,),
        in_specs=[full(a) for a in args],
        out_specs=full(y),
        out_shape=jax.ShapeDtypeStruct(y.shape, MXU_DTYPE),
        name="ssm_gate_norm",
    )(*args)


def _mamba2_mixer_pallas(x3d, conv_state, ssm_state, w_in, conv_w, conv_b, dt_bias, a_log, d_skip, norm_g):
    b_, t_, _ = x3d.shape
    z, xs, bm, cm, dt, conv_new = _ssm_project(x3d, conv_state, w_in, conv_w, conv_b, dt_bias)
    if t_ == 1:
        y, h_new = _ssm_step(ssm_state, xs, bm, cm, dt, a_log)
        yg = _ssm_gate_norm_rows(y, xs, z, d_skip, norm_g)
    else:
        yg, h_new = _ssm_chunk_scan(b_, t_, xs, bm, cm, dt, z, a_log, d_skip, norm_g)
    return yg, conv_new, h_new.astype(ssm_state.dtype)


PAGES_PER_STEP = 8


def _sortable_key(score):
    bits = pltpu.bitcast(score, jnp.int32)
    return jnp.where(bits >= 0, bits, bits ^ jnp.int32(0x7FFFFFFF))


def _dsa_decode_kernel(pt_ref, iq_ref, iw_ref, q_ref, ikn_ref, kn_ref, vn_ref, *rest,
                       topk, col_bits, n_steps, pages):
    idx_refs, k_refs, v_refs = rest[:pages], rest[pages:2 * pages], rest[2 * pages:3 * pages]
    o_ref, key_ref, dense_ref, sel_ref, m_ref, l_ref, acc_ref = rest[3 * pages:]
    s = pl.program_id(1)
    nk = key_ref.shape[2]
    nt = (((1,), (1,)), ((), ()))
    iq = iq_ref[0]
    iw = iw_ref[0]
    lane = lax.broadcasted_iota(jnp.int32, (1, nk), 1)

    def row_dot(a, row):
        return jnp.sum(a.astype(jnp.float32) * row.astype(jnp.float32), axis=1, keepdims=True)

    def index_score(sc):
        return jnp.sum(iw * jnp.maximum(sc, 0.0), axis=0, keepdims=True)

    def fold(hit):
        out = hit[:, 0:128]
        for l in range(1, nk // 128):
            out = out + hit[:, l * 128:(l + 1) * 128]
        return out

    step_row = lax.broadcasted_iota(jnp.int32, dense_ref.shape, 0)

    @pl.when(s == 0)
    def _reset():
        dense_ref[...] = jnp.full(dense_ref.shape, INT32_MIN, jnp.int32)

    @pl.when(s < n_steps)
    def _score():
        ik_t = jnp.concatenate([r[0] for r in idx_refs], axis=1).astype(MXU_DTYPE)
        key = _sortable_key(index_score(jnp.dot(iq, ik_t, preferred_element_type=jnp.float32)))
        key_ref[s] = key
        dense_ref[...] = jnp.where(step_row == s, key, dense_ref[...])

    @pl.when(s == n_steps - 1)
    def _select():
        key_new = _sortable_key(index_score(row_dot(iq, ikn_ref[0])))
        dense = dense_ref[...]
        dense_col = step_row * nk + lax.broadcasted_iota(jnp.int32, dense_ref.shape, 1)

        def count(pred_past, pred_new):
            hit = jnp.where(pred_past(dense, dense_col), 1.0, 0.0)
            cnt = _tree_sum([hit[:, l * 128:(l + 1) * 128] for l in range(nk // 128)])
            total = jnp.sum(jnp.sum(cnt, axis=1, keepdims=True), axis=0, keepdims=True)
            return total + jnp.where(pred_new(key_new), 1.0, 0.0)

        def at_least(cand):
            return count(lambda k, c: k >= cand, lambda k: k >= cand) >= float(topk)

        def two_bits(i, thr):
            hi = lax.shift_left(jnp.int32(1), jnp.int32(31) - 2 * i)
            lo = lax.shift_left(jnp.int32(1), jnp.int32(30) - 2 * i)
            c1, c2, c3 = thr ^ lo, thr ^ hi, thr ^ hi ^ lo
            return jnp.where(at_least(c3), c3, jnp.where(at_least(c2), c2, jnp.where(at_least(c1), c1, thr)))

        thr = lax.fori_loop(0, 16, two_bits, jnp.full((1, 1), INT32_MIN, jnp.int32))
        need = float(topk) - count(lambda k, c: k > thr, lambda k: k > thr)

        def col_body(i, last):
            cand = last | lax.shift_left(jnp.int32(1), jnp.int32(col_bits - 1) - i)
            ties = count(lambda k, c: (k == thr) & (c < cand),
                         lambda k: (k == thr) & (jnp.int32(n_steps * nk) < cand))
            return jnp.where(ties < need, cand, last)

        n_tied = count(lambda k, c: k == thr, lambda k: k == thr)
        last_tie = lax.cond(n_tied[0, 0] > need[0, 0],
                            lambda: lax.fori_loop(0, col_bits, col_body, jnp.zeros((1, 1), jnp.int32)),
                            lambda: jnp.full((1, 1), 2 ** col_bits - 1, jnp.int32))
        sel_ref[0] = jnp.broadcast_to(thr, sel_ref.shape[1:])
        sel_ref[1] = jnp.broadcast_to(last_tie, sel_ref.shape[1:])
        sel_ref[2] = jnp.broadcast_to(key_new, sel_ref.shape[1:])

    gsz = ATT_HEADS // ATT_KV_HEADS
    q = q_ref[0]
    q_wide = jnp.concatenate([q] * ATT_KV_HEADS, axis=1)
    head_i = lax.broadcasted_iota(jnp.int32, q_wide.shape, 0)
    col_i = lax.broadcasted_iota(jnp.int32, q_wide.shape, 1)
    own_group = (col_i // ATT_HEAD_DIM) == (head_i // gsz)
    q_blk = jnp.where(own_group, q_wide, jnp.zeros_like(q_wide))

    def keep_mask(key, col):
        thr, last_tie = sel_ref[0, 0:1, 0:1], sel_ref[1, 0:1, 0:1]
        return (key > thr) | ((key == thr) & (col <= last_tie))

    def online_update(logits, weighted_values):
        m_old = m_ref[...]
        m_new = jnp.maximum(m_old, jnp.max(logits, axis=1, keepdims=True))
        p = jnp.exp(logits - m_new)
        alpha = jnp.exp(m_old - m_new)
        l_ref[...] = alpha * l_ref[...] + jnp.sum(p, axis=1, keepdims=True)
        acc_ref[...] = alpha * acc_ref[...] + weighted_values(p.astype(MXU_DTYPE))
        m_ref[...] = m_new

    @pl.when(s == n_steps)
    def _init():
        m_ref[...] = jnp.full(m_ref.shape, MASK_NEG, jnp.float32)
        l_ref[...] = jnp.zeros(l_ref.shape, jnp.float32)
        acc_ref[...] = jnp.zeros(acc_ref.shape, jnp.float32)

    @pl.when(s >= n_steps)
    def _attend():
        st = s - n_steps
        bias = jnp.where(keep_mask(key_ref[st], lane + st * nk), 0.0, MASK_NEG)
        k_t = jnp.concatenate([r[0] for r in k_refs], axis=1).astype(MXU_DTYPE)
        v_t = jnp.concatenate([r[0] for r in v_refs], axis=1).astype(MXU_DTYPE)
        online_update(jnp.dot(q_blk, k_t, preferred_element_type=jnp.float32) + bias,
                      lambda p: lax.dot_general(p, v_t, nt, preferred_element_type=jnp.float32))

    @pl.when(s == 2 * n_steps - 1)
    def _finish():
        keep_new = keep_mask(sel_ref[2, 0:1, 0:1], jnp.int32(n_steps * nk))
        logit = row_dot(q_blk, kn_ref[0])
        v_row = vn_ref[0].astype(jnp.float32)
        online_update(logit + jnp.where(keep_new, 0.0, MASK_NEG), lambda p: p.astype(jnp.float32) * v_row)
        out = jnp.where(own_group, acc_ref[...] / l_ref[...], 0.0)
        o = out[:, 0:ATT_HEAD_DIM]
        for g in range(1, ATT_KV_HEADS):
            o = o + out[:, g * ATT_HEAD_DIM:(g + 1) * ATT_HEAD_DIM]
        o_ref[0] = o.astype(o_ref.dtype)


def _dsa_decode(q, iq, iw, ik_new, k_new, v_new, cache_k, cache_v, cache_idx_k, page_table):
    b_, n_pages = page_table.shape
    n_pool, page = cache_k.shape[0], cache_k.shape[1]
    pages = PAGES_PER_STEP
    n_steps = n_pages // pages
    past = n_pages * page
    ck = jnp.transpose(cache_k, (0, 2, 3, 1)).reshape(n_pool, ATT_KV_DIM, page)
    cv = jnp.transpose(cache_v, (0, 2, 3, 1)).reshape(n_pool, ATT_KV_DIM, page)
    cik = jnp.swapaxes(cache_idx_k, 1, 2)
    per_seq = lambda a: pl.BlockSpec((1,) + a.shape[1:], lambda b, s, pt: (b,) + (0,) * (a.ndim - 1))

    def paged(width, j, attend_phase):
        def index(b, s, pt):
            grp = jnp.maximum(s - n_steps, 0) if attend_phase else jnp.minimum(s, n_steps - 1)
            return (pt[b, grp * pages + j], 0, 0)
        return pl.BlockSpec((1, width, page), index)

    small = [iq.reshape(b_, IDX_HEADS, IDX_DIM), iw[:, :IDX_HEADS].reshape(b_, IDX_HEADS, 1),
             q.reshape(b_, ATT_HEADS, ATT_HEAD_DIM), ik_new.astype(MXU_DTYPE).reshape(b_, 1, IDX_DIM),
             k_new.astype(MXU_DTYPE).reshape(b_, 1, ATT_KV_DIM), v_new.astype(MXU_DTYPE).reshape(b_, 1, ATT_KV_DIM)]
    grid_spec = pltpu.PrefetchScalarGridSpec(
        num_scalar_prefetch=1,
        grid=(b_, 2 * n_steps),
        in_specs=[per_seq(a) for a in small]
        + [paged(IDX_DIM, j, False) for j in range(pages)]
        + [paged(ATT_KV_DIM, j, True) for j in range(pages)]
        + [paged(ATT_KV_DIM, j, True) for j in range(pages)],
        out_specs=pl.BlockSpec((1, ATT_HEADS, ATT_HEAD_DIM), lambda b, s, pt: (b, 0, 0)),
        scratch_shapes=[pltpu.VMEM((n_steps, 1, pages * page), jnp.int32),
                        pltpu.VMEM((n_steps, pages * page), jnp.int32),
                        pltpu.VMEM((3, SUBLANES, 128), jnp.int32),
                        pltpu.VMEM((ATT_HEADS, 1), jnp.float32),
                        pltpu.VMEM((ATT_HEADS, 1), jnp.float32),
                        pltpu.VMEM((ATT_HEADS, ATT_KV_DIM), jnp.float32)])
    o = pl.pallas_call(
        functools.partial(_dsa_decode_kernel, topk=min(TOPK_MAX, (past + 1) // 4),
                          col_bits=max(1, past.bit_length()), n_steps=n_steps, pages=pages),
        grid_spec=grid_spec,
        out_shape=jax.ShapeDtypeStruct((b_, ATT_HEADS, ATT_HEAD_DIM), MXU_DTYPE),
        compiler_params=pltpu.CompilerParams(
            dimension_semantics=("parallel", "arbitrary"),
            vmem_limit_bytes=V7X_VMEM_LIMIT_BYTES),
        name="dsa_decode",
    )(page_table, *small, *([cik] * pages), *([ck] * pages), *([cv] * pages))
    return o.reshape(b_, ATT_Q_DIM)


def _decode_score_kernel(pt_ref, iq_ref, iw_ref, ikn_ref, *rest):
    idx_refs, (key_ref, knew_ref) = rest[:-2], rest[-2:]
    iq, iw = iq_ref[0], iw_ref[0]
    weigh = lambda sc: jnp.sum(iw * jnp.maximum(sc, 0.0), axis=0, keepdims=True)
    ik_t = jnp.concatenate([r[0] for r in idx_refs], axis=1).astype(MXU_DTYPE)
    key_ref[0] = _sortable_key(weigh(jnp.dot(iq, ik_t, preferred_element_type=jnp.float32)))

    @pl.when(pl.program_id(1) == 0)
    def _():
        sc_new = jnp.sum(iq.astype(jnp.float32) * ikn_ref[0].astype(jnp.float32), axis=1, keepdims=True)
        knew_ref[0] = jnp.broadcast_to(_sortable_key(weigh(sc_new)), knew_ref.shape[1:])


def _decode_select_kernel(keys_ref, knew_ref, thr_ref, last_ref, *, topk, col_bits):
    keys = keys_ref[...]
    key_new = knew_ref[:, 0:1]
    past = keys.shape[1]
    col = lax.broadcasted_iota(jnp.int32, keys.shape, 1)

    def count(pred_past, pred_new):
        hit = jnp.where(pred_past(keys, col), 1.0, 0.0)
        cnt = _tree_sum([hit[:, l * 128:(l + 1) * 128] for l in range(past // 128)])
        return jnp.sum(cnt, axis=1, keepdims=True) + jnp.where(pred_new(key_new), 1.0, 0.0)

    def at_least(cand):
        return count(lambda k, c: k >= cand, lambda k: k >= cand) >= float(topk)

    def two_bits(i, thr):
        hi = lax.shift_left(jnp.int32(1), jnp.int32(31) - 2 * i)
        lo = lax.shift_left(jnp.int32(1), jnp.int32(30) - 2 * i)
        c1, c2, c3 = thr ^ lo, thr ^ hi, thr ^ hi ^ lo
        return jnp.where(at_least(c3), c3, jnp.where(at_least(c2), c2, jnp.where(at_least(c1), c1, thr)))

    thr = lax.fori_loop(0, 16, two_bits, jnp.full(key_new.shape, INT32_MIN, jnp.int32))
    need = float(topk) - count(lambda k, c: k > thr, lambda k: k > thr)

    def col_body(i, last):
        cand = last | lax.shift_left(jnp.int32(1), jnp.int32(col_bits - 1) - i)
        ties = count(lambda k, c: (k == thr) & (c < cand), lambda k: (k == thr) & (jnp.int32(past) < cand))
        return jnp.where(ties < need, cand, last)

    n_tied = count(lambda k, c: k == thr, lambda k: k == thr)
    excess = jnp.max(jnp.where(n_tied > need, 1.0, 0.0), axis=0, keepdims=True)
    last_tie = lax.cond(excess[0, 0] > 0.0,
                        lambda: lax.fori_loop(0, col_bits, col_body, jnp.zeros(key_new.shape, jnp.int32)),
                        lambda: jnp.full(key_new.shape, 2 ** col_bits - 1, jnp.int32))
    thr_ref[...] = jnp.broadcast_to(thr, thr_ref.shape)
    last_ref[...] = jnp.broadcast_to(last_tie, last_ref.shape)


def _decode_attend_kernel(pt_ref, q_ref, kn_ref, vn_ref, key_ref, knew_ref, thr_ref, last_ref, *rest,
                          n_steps, pages):
    k_refs, v_refs = rest[:pages], rest[pages:2 * pages]
    o_ref, m_ref, l_ref, acc_ref = rest[2 * pages:]
    s = pl.program_id(1)
    nk = key_ref.shape[2]
    nt = (((1,), (1,)), ((), ()))
    thr, last_tie = thr_ref[0, :, 0:1], last_ref[0, :, 0:1]
    keep_mask = lambda key, col: (key > thr) | ((key == thr) & (col <= last_tie))
    gsz = ATT_HEADS // ATT_KV_HEADS
    q = q_ref[0]
    q_wide = jnp.concatenate([q] * ATT_KV_HEADS, axis=1)
    head_i = lax.broadcasted_iota(jnp.int32, q_wide.shape, 0)
    col_i = lax.broadcasted_iota(jnp.int32, q_wide.shape, 1)
    own_group = (col_i // ATT_HEAD_DIM) == (head_i // gsz)
    q_blk = jnp.where(own_group, q_wide, jnp.zeros_like(q_wide))

    def online_update(logits, weighted_values):
        m_old = m_ref[...]
        m_new = jnp.maximum(m_old, jnp.max(logits, axis=1, keepdims=True))
        p = jnp.exp(logits - m_new)
        alpha = jnp.exp(m_old - m_new)
        l_ref[...] = alpha * l_ref[...] + jnp.sum(p, axis=1, keepdims=True)
        acc_ref[...] = alpha * acc_ref[...] + weighted_values(p.astype(MXU_DTYPE))
        m_ref[...] = m_new

    @pl.when(s == 0)
    def _init():
        m_ref[...] = jnp.full(m_ref.shape, MASK_NEG, jnp.float32)
        l_ref[...] = jnp.zeros(l_ref.shape, jnp.float32)
        acc_ref[...] = jnp.zeros(acc_ref.shape, jnp.float32)

    col = lax.broadcasted_iota(jnp.int32, (1, nk), 1) + s * nk
    bias = jnp.where(keep_mask(key_ref[0], col), 0.0, MASK_NEG)
    k_t = jnp.concatenate([r[0] for r in k_refs], axis=1).astype(MXU_DTYPE)
    v_t = jnp.concatenate([r[0] for r in v_refs], axis=1).astype(MXU_DTYPE)
    online_update(jnp.dot(q_blk, k_t, preferred_element_type=jnp.float32) + bias,
                  lambda p: lax.dot_general(p, v_t, nt, preferred_element_type=jnp.float32))

    @pl.when(s == n_steps - 1)
    def _finish():
        keep_new = keep_mask(knew_ref[0, :, 0:1], jnp.int32(n_steps * nk))
        logit = jnp.sum(q_blk.astype(jnp.float32) * kn_ref[0].astype(jnp.float32), axis=1, keepdims=True)
        v_row = vn_ref[0].astype(jnp.float32)
        online_update(logit + jnp.where(keep_new, 0.0, MASK_NEG), lambda p: p.astype(jnp.float32) * v_row)
        out = jnp.where(own_group, acc_ref[...] / l_ref[...], 0.0)
        o = out[:, 0:ATT_HEAD_DIM]
        for g in range(1, ATT_KV_HEADS):
            o = o + out[:, g * ATT_HEAD_DIM:(g + 1) * ATT_HEAD_DIM]
        o_ref[0] = o.astype(o_ref.dtype)


def _dsa_decode_split(q, iq, iw, ik_new, k_new, v_new, cache_k, cache_v, cache_idx_k, page_table):
    b_, n_pages = page_table.shape
    n_pool, page = cache_k.shape[0], cache_k.shape[1]
    pages = PAGES_PER_STEP
    n_steps = n_pages // pages
    nk = pages * page
    past = n_pages * page
    ck = jnp.transpose(cache_k, (0, 2, 3, 1)).reshape(n_pool, ATT_KV_DIM, page)
    cv = jnp.transpose(cache_v, (0, 2, 3, 1)).reshape(n_pool, ATT_KV_DIM, page)
    cik = jnp.swapaxes(cache_idx_k, 1, 2)
    per_seq = lambda a: pl.BlockSpec((1,) + a.shape[1:], lambda b, s, pt: (b,) + (0,) * (a.ndim - 1))
    paged = lambda width, j: pl.BlockSpec((1, width, page), lambda b, s, pt: (pt[b, s * pages + j], 0, 0))
    key_blk = pl.BlockSpec((1, 1, nk), lambda b, s, pt: (b, 0, s))
    lane_blk = pl.BlockSpec((1, 1, 128), lambda b, s, pt: (b, 0, 0))
    params = pltpu.CompilerParams(dimension_semantics=("parallel", "arbitrary"),
                                  vmem_limit_bytes=V7X_VMEM_LIMIT_BYTES)

    score_in = [iq.reshape(b_, IDX_HEADS, IDX_DIM), iw[:, :IDX_HEADS].reshape(b_, IDX_HEADS, 1),
                ik_new.astype(MXU_DTYPE).reshape(b_, 1, IDX_DIM)]
    keys, key_new = pl.pallas_call(
        _decode_score_kernel,
        grid_spec=pltpu.PrefetchScalarGridSpec(
            num_scalar_prefetch=1, grid=(b_, n_steps),
            in_specs=[per_seq(a) for a in score_in] + [paged(IDX_DIM, j) for j in range(pages)],
            out_specs=[key_blk, lane_blk]),
        out_shape=[jax.ShapeDtypeStruct((b_, 1, past), jnp.int32), jax.ShapeDtypeStruct((b_, 1, 128), jnp.int32)],
        compiler_params=params,
        name="dsa_decode_score",
    )(page_table, *score_in, *([cik] * pages))

    whole = lambda shape: pl.BlockSpec(shape, lambda i: (0,) * len(shape))
    thr, last_tie = pl.pallas_call(
        functools.partial(_decode_select_kernel, topk=min(TOPK_MAX, (past + 1) // 4),
                          col_bits=max(1, past.bit_length())),
        grid=(1,),
        in_specs=[whole((b_, past)), whole((b_, 128))],
        out_specs=[whole((b_, 128)), whole((b_, 128))],
        out_shape=[jax.ShapeDtypeStruct((b_, 128), jnp.int32)] * 2,
        compiler_params=pltpu.CompilerParams(vmem_limit_bytes=V7X_VMEM_LIMIT_BYTES),
        name="dsa_decode_select",
    )(keys.reshape(b_, past), key_new.reshape(b_, 128))

    attend_in = [q.reshape(b_, ATT_HEADS, ATT_HEAD_DIM), k_new.astype(MXU_DTYPE).reshape(b_, 1, ATT_KV_DIM),
                 v_new.astype(MXU_DTYPE).reshape(b_, 1, ATT_KV_DIM)]
    o = pl.pallas_call(
        functools.partial(_decode_attend_kernel, n_steps=n_steps, pages=pages),
        grid_spec=pltpu.PrefetchScalarGridSpec(
            num_scalar_prefetch=1, grid=(b_, n_steps),
            in_specs=[per_seq(a) for a in attend_in] + [key_blk, lane_blk, lane_blk, lane_blk]
            + [paged(ATT_KV_DIM, j) for j in range(pages)] * 2,
            out_specs=pl.BlockSpec((1, ATT_HEADS, ATT_HEAD_DIM), lambda b, s, pt: (b, 0, 0)),
            scratch_shapes=[pltpu.VMEM((ATT_HEADS, 1), jnp.float32), pltpu.VMEM((ATT_HEADS, 1), jnp.float32),
                            pltpu.VMEM((ATT_HEADS, ATT_KV_DIM), jnp.float32)]),
        out_shape=jax.ShapeDtypeStruct((b_, ATT_HEADS, ATT_HEAD_DIM), MXU_DTYPE),
        compiler_params=params,
        name="dsa_decode_attend",
    )(page_table, *attend_in, keys, key_new, thr.reshape(b_, 1, 128), last_tie.reshape(b_, 1, 128),
      *([ck] * pages), *([cv] * pages))
    return o.reshape(b_, ATT_Q_DIM)


def _dsa_sample_pallas(x3d, cache_k, cache_v, cache_idx_k, page_table, w_in, kn_g, kn_b):
    b_, t_, _ = x3d.shape
    past = page_table.shape[1] * cache_k.shape[1]
    pos = jnp.full((b_,), past, jnp.int32)
    q, iq, v, _, iw, kT, _, ikT, _ = _dsa_project(x3d.reshape(1, b_, D_MODEL), pos, w_in, kn_g, kn_b)
    k = _untranspose_groups(kT)[0]
    ik = _untranspose_groups(ikT)[0]
    o = _dsa_decode_split(q, iq, iw, ik, k, v, cache_k, cache_v, cache_idx_k, page_table)
    kv4 = lambda u: u.reshape(b_, t_, ATT_KV_HEADS, ATT_HEAD_DIM)
    return o, kv4(k), kv4(v), ik.reshape(b_, t_, IDX_DIM)


def _layer_norm(x, g, b):
    xf = x.astype(jnp.float32)
    mu = jnp.mean(xf, -1, keepdims=True)
    var = jnp.mean(jnp.square(xf - mu), -1, keepdims=True)
    return ((xf - mu) * lax.rsqrt(var + LN_EPS)).astype(x.dtype) * g + b


def _rope_partial(x, pos, rot_dim):
    half = rot_dim // 2
    inv = ROPE_THETA ** (-jnp.arange(half, dtype=jnp.float32) / half)
    ang = pos.astype(jnp.float32)[:, None] * inv[None, :]
    cos = jnp.cos(ang)[:, None, :]
    sin = jnp.sin(ang)[:, None, :]
    xf = x[..., :rot_dim].astype(jnp.float32)
    x1, x2 = xf[..., :half], xf[..., half:]
    rot = jnp.concatenate([x1 * cos - x2 * sin, x2 * cos + x1 * sin], axis=-1).astype(x.dtype)
    return jnp.concatenate([rot, x[..., rot_dim:]], axis=-1)


def _gather_rows(rows, idx):
    return jax.vmap(lambda r, i: r[i])(rows, idx)


def _gmlp_mixer(x, w_in, ln_g, ln_b, ws, bs):
    B_, T, _ = x.shape
    u, v = jnp.split(jax.nn.gelu(x @ w_in), 2, axis=-1)
    v = _layer_norm(v, ln_g, ln_b)
    l = min(T, CHUNK)
    c = T // l
    mask = jnp.tril(jnp.ones((l, l), dtype=bool))
    w = jnp.where(mask, ws[:, :l, :l], 0.0)
    vc = v.reshape(B_, c, l, GM_GROUPS, GM_GROUP_DIM)
    mixed = jnp.einsum('gts,bcsgd->bctgd', w, vc) + jnp.transpose(bs[:, :l])[:, :, None]
    return u * mixed.reshape(B_, T, GM_WIDTH), v


def _ssd_chunked(xs, dt, a, bm, cm, h0):
    B_, T = xs.shape[:2]
    l = min(T, SSM_CHUNK)
    c = T // l
    blk = lambda t: t.reshape((B_, c, l) + t.shape[2:])
    xdt = blk(xs.astype(jnp.float32) * dt[..., None])
    bc, cc, acum = blk(bm), blk(cm), jnp.cumsum(blk(a), axis=2)
    at = jnp.moveaxis(acum, 2, -1)
    causal = jnp.tril(jnp.ones((l, l), dtype=bool))
    seg = jnp.exp(jnp.where(causal, at[..., :, None] - at[..., None, :], -jnp.inf))
    cb = jnp.einsum('bctgn,bcsgn->bcgts', cc, bc)
    y_diag = jnp.einsum('bcgts,bcgets,bcsgep->bctgep', cb, seg, xdt)
    states = jnp.einsum('bclgn,bclge,bclgep->bcgepn', bc, jnp.exp(acum[:, :, -1:] - acum), xdt)
    chunk_decay = jnp.exp(acum[:, :, -1])

    def step(h, inp):
        dec, st = inp
        return h * dec[..., None, None] + st, h

    h_last, h_in = lax.scan(step, h0, (jnp.moveaxis(chunk_decay, 1, 0), jnp.moveaxis(states, 1, 0)))
    y_off = jnp.einsum('bctgn,bcgepn,bctge->bctgep', cc, jnp.moveaxis(h_in, 0, 1), jnp.exp(acum))
    return (y_diag + y_off).reshape(B_, T, SSM_GROUPS, SSM_HPG, SSM_HEAD_DIM), h_last


def _mamba2_mixer(x, conv_state, ssm_state, w_in, conv_w, conv_b, dt_bias, a_log, d_skip, norm_g):
    B_, T, _ = x.shape
    z, xbc, dt = jnp.split(x @ w_in, [SSM_D_INNER, SSM_D_INNER + SSM_CONV_DIM], axis=-1)
    xbc_ext = jnp.concatenate([conv_state, xbc], axis=1)
    conv = conv_b
    for j in range(SSM_CONV):
        conv = conv + xbc_ext[:, j:j + T] * conv_w[j]
    xbc = jax.nn.silu(conv)
    xs, bm, cm = jnp.split(xbc, [SSM_D_INNER, SSM_D_INNER + SSM_GROUPS * SSM_STATE], axis=-1)
    xs = xs.reshape(B_, T, SSM_GROUPS, SSM_HPG, SSM_HEAD_DIM)
    bm = bm.reshape(B_, T, SSM_GROUPS, SSM_STATE)
    cm = cm.reshape(B_, T, SSM_GROUPS, SSM_STATE)
    dt = jax.nn.softplus((dt + dt_bias).astype(jnp.float32)).reshape(B_, T, SSM_GROUPS, SSM_HPG)
    a_neg = -jnp.exp(a_log.astype(jnp.float32)).reshape(SSM_GROUPS, SSM_HPG)
    h0 = ssm_state.astype(jnp.float32).reshape(B_, SSM_GROUPS, SSM_HPG, SSM_HEAD_DIM, SSM_STATE)
    y, h_last = _ssd_chunked(xs, dt, dt * a_neg, bm, cm, h0)
    y = y.astype(x.dtype) + xs * d_skip.reshape(SSM_GROUPS, SSM_HPG, 1)
    yg = (y.reshape(B_, T, SSM_D_INNER) * jax.nn.silu(z)).reshape(B_, T, SSM_GROUPS, -1).astype(jnp.float32)
    yg = (yg * lax.rsqrt(jnp.mean(jnp.square(yg), -1, keepdims=True) + LN_EPS)).astype(x.dtype)
    yg = yg.reshape(B_, T, SSM_D_INNER) * norm_g
    new_ssm = h_last.reshape(B_, SSM_HEADS, SSM_HEAD_DIM, SSM_STATE).astype(ssm_state.dtype)
    return yg, xbc_ext[:, T:], new_ssm


def _dsa_project_jax(x, pos, w_in, kn_g, kn_b):
    B_, T, _ = x.shape
    q, k, v, iq, ik, iw = jnp.split(x @ w_in, list(ATT_IN_SPLITS), axis=-1)
    q = _rope_partial(q.reshape(B_, T, ATT_HEADS, ATT_HEAD_DIM), pos, ROPE_DIM)
    k = _rope_partial(k.reshape(B_, T, ATT_KV_HEADS, ATT_HEAD_DIM), pos, ROPE_DIM)
    v = v.reshape(B_, T, ATT_KV_HEADS, ATT_HEAD_DIM)
    iq = _rope_partial(iq.reshape(B_, T, IDX_HEADS, IDX_DIM), pos, IDX_ROPE_DIM)
    ik = _rope_partial(_layer_norm(ik, kn_g, kn_b)[:, :, None, :], pos, IDX_ROPE_DIM)[:, :, 0, :]
    iw = iw * (IDX_HEADS ** -0.5 * IDX_DIM ** -0.5)
    return q, k, v, iq, ik, iw


def _dsa_select(iq, iw, ik, qpos, topk):
    s = jnp.einsum('bqhd,bsd->bqhs', iq, ik)
    score = jnp.einsum('bqh,bqhs->bqs', iw, jax.nn.relu(s)).astype(jnp.float32)
    adm = jnp.arange(ik.shape[1])[None, :] <= qpos[:, None]
    score = jnp.where(adm[None], score, -jnp.inf)
    _, idx = lax.top_k(score, topk)
    return idx, idx <= qpos[None, :, None]


def _sparse_attend(q, k_sel, v_sel, valid):
    B_, Q = q.shape[:2]
    qg = q.reshape(B_, Q, ATT_KV_HEADS, ATT_HEADS // ATT_KV_HEADS, ATT_HEAD_DIM)
    s = jnp.einsum('bqhgd,bqkhd->bqhgk', qg, k_sel).astype(jnp.float32) * (ATT_HEAD_DIM ** -0.5)
    s = jnp.where(valid[:, :, None, None, :], s, -jnp.inf)
    p = jax.nn.softmax(s, axis=-1).astype(v_sel.dtype)
    o = jnp.einsum('bqhgk,bqkhd->bqhgd', p, v_sel)
    return o.reshape(B_, Q, ATT_Q_DIM)


def _dsa_prompt(x, w_in, kn_g, kn_b):
    B_, T, _ = x.shape
    q, k, v, iq, ik, iw = _dsa_project_jax(x, jnp.arange(T), w_in, kn_g, kn_b)
    topk = min(TOPK_MAX, T // 4)

    def block(bi):
        t0 = bi * Q_BLOCK
        sl = lambda t: lax.dynamic_slice_in_dim(t, t0, Q_BLOCK, axis=1)
        qpos = t0 + jnp.arange(Q_BLOCK)
        idx, valid = _dsa_select(sl(iq), sl(iw), ik, qpos, topk)
        return _sparse_attend(sl(q), _gather_rows(k, idx), _gather_rows(v, idx), valid)

    o = lax.map(block, jnp.arange(T // Q_BLOCK))
    o = jnp.moveaxis(o, 0, 1).reshape(B_, T, ATT_Q_DIM)
    return o, k, v, ik


def _dsa_sample(x, cache_k, cache_v, cache_idx_k, page_table, w_in, kn_g, kn_b):
    B_, T, _ = x.shape
    page = cache_k.shape[1]
    past = page_table.shape[1] * page
    pos = past + jnp.arange(T)
    q, k, v, iq, ik, iw = _dsa_project_jax(x, pos, w_in, kn_g, kn_b)
    ik_all = jnp.concatenate([cache_idx_k[page_table].reshape(B_, past, IDX_DIM), ik], axis=1)
    idx, valid = _dsa_select(iq, iw, ik_all, pos, min(TOPK_MAX, (past + T) // 4))
    past_idx = jnp.minimum(idx, past - 1)
    phys = jnp.take_along_axis(page_table, (past_idx // page).reshape(B_, -1), axis=1).reshape(idx.shape)
    off = past_idx % page
    new_idx = jnp.clip(idx - past, 0, T - 1)
    is_new = (idx >= past)[..., None, None]
    k_sel = jnp.where(is_new, _gather_rows(k, new_idx), cache_k[phys, off])
    v_sel = jnp.where(is_new, _gather_rows(v, new_idx), cache_v[phys, off])
    o = _sparse_attend(q, k_sel, v_sel, valid)
    return o, k, v, ik


def _rwkv7_mixer(x, shift, wkv, mu, w_r, w_k, w_v, w0, w1, w2, a0, a1, a2, g1, g2,
                 k_k, k_a, r_k, gn_g, gn_b):
    B_, T, _ = x.shape
    x_prev = jnp.concatenate([shift[:, None, :], x[:, :-1]], axis=1)
    xm = x[None] + (x_prev - x)[None] * mu[:, None, None, :]
    xr, xw, xk, xv, xa, xg = xm
    r = xr @ w_r
    w_log = -jax.nn.softplus(-(w0 + jnp.tanh(xw @ w1) @ w2)) - 0.5
    k = xk @ w_k
    v = xv @ w_v
    a = jax.nn.sigmoid(a0 + (xa @ a1) @ a2)
    g = jax.nn.sigmoid(xg @ g1) @ g2
    heads = lambda t: t.reshape(B_, T, RW_HEADS, RW_HEAD)
    kk = heads(k * k_k).astype(jnp.float32)
    kk = kk / jnp.maximum(jnp.sqrt(jnp.sum(kk * kk, -1, keepdims=True)), 1e-12)
    k = k * (1.0 + (a - 1.0) * k_a)
    decay = jnp.exp(-jnp.exp(w_log.astype(jnp.float32)))
    r, k, v, a, decay = heads(r), heads(k), heads(v), heads(a), heads(decay)
    seq = tuple(jnp.moveaxis(t.astype(jnp.float32), 1, 0) for t in (r, decay, k, v, kk, kk * a))

    def step(s, inp):
        r_t, d_t, k_t, v_t, kk_t, b_t = inp
        sa = jnp.einsum('bhij,bhj->bhi', s, kk_t)
        s = s * d_t[:, :, None, :] - sa[..., None] * b_t[:, :, None, :] + v_t[..., None] * k_t[:, :, None, :]
        return s, jnp.einsum('bhij,bhj->bhi', s, r_t)

    s_last, y = lax.scan(step, wkv.astype(jnp.float32), seq)
    y = jnp.moveaxis(y, 0, 1)
    mu_y = jnp.mean(y, -1, keepdims=True)
    var_y = jnp.mean(jnp.square(y - mu_y), -1, keepdims=True)
    yn = ((y - mu_y) * lax.rsqrt(var_y + RW_GN_EPS)).reshape(B_, T, D_MODEL).astype(x.dtype) * gn_g + gn_b
    bonus = (jnp.sum(r * k * r_k, -1, keepdims=True) * v).reshape(B_, T, D_MODEL)
    return (yn + bonus) * g, x[:, -1], s_last.astype(wkv.dtype)


def kernel(x_prompt, x_sample, state_ssm_conv, state_ssm, cache_k, cache_v, cache_idx_k, state_rwkv_shift, state_rwkv_wkv, page_table, p_prompt, p_sample, ln_g, ln_b, ffn_w_up, ffn_w_down, ple_w_p, ple_w_g, ple_b_g, gm_w_in, gm_ln_g, gm_ln_b, gm_ws, gm_bs, gm_w_out, ssm_w_in, ssm_conv_w, ssm_conv_b, ssm_dt_bias, ssm_a_log, ssm_d, ssm_norm_g, ssm_w_out, att_w_in, att_kn_g, att_kn_b, att_w_out, rw_mu, rw_w_r, rw_w_k, rw_w_v, rw_w_o, rw_w0, rw_w1, rw_w2, rw_a0, rw_a1, rw_a2, rw_g1, rw_g2, rw_k_k, rw_k_a, rw_r_k, rw_gn_g, rw_gn_b):
    bp, tp, _ = x_prompt.shape
    bs_, ts, _ = x_sample.shape
    bf = lambda w: w.astype(jnp.bfloat16)
    w_up_bf, w_down_bf = bf(ffn_w_up), bf(ffn_w_down)
    ple_wp_bf, ple_wg_bf = bf(ple_w_p), bf(ple_w_g)
    pp3 = p_prompt.reshape(DEPTH, bp * tp, PLE_DIM)
    ps3 = p_sample.reshape(DEPTH, bs_ * ts, PLE_DIM)

    yp = x_prompt.reshape(bp * tp, D_MODEL)
    ys = x_sample.reshape(bs_ * ts, D_MODEL)
    r3p = lambda t: t.reshape(bp, tp, -1)
    r3s = lambda t: t.reshape(bs_, ts, -1)
    f2 = lambda t: t.reshape(-1, t.shape[-1])

    for i in range(DEPTH):
        yp = _ffn_sub(yp, w_up_bf, w_down_bf, i, 0, ln_g[i, 0], ln_b[i, 0])
        ys = _ffn_sub(ys, w_up_bf, w_down_bf, i, 0, ln_g[i, 0], ln_b[i, 0])
        m = i % N_MIXERS
        if m == 0:
            gm_args = (gm_w_in, gm_ln_g, gm_ln_b, gm_ws, gm_bs, gm_w_out, ln_g[i, 1], ln_b[i, 1])
            yp, = _gmlp_block(yp, tp, *gm_args, False)
            ys, gm_v_s = _gmlp_block(ys, ts, *gm_args, True)
            gm_v_s = r3s(gm_v_s)
        elif m == 1:
            ssm_args = (ssm_w_in, ssm_conv_w, ssm_conv_b, ssm_dt_bias, ssm_a_log, ssm_d, ssm_norm_g)
            hp, conv_p, ssm_p = _mamba2_mixer_pallas(
                r3p(yp), jnp.zeros((bp, SSM_CONV - 1, SSM_CONV_DIM), yp.dtype),
                jnp.zeros((bp, SSM_HEADS, SSM_HEAD_DIM, SSM_STATE), yp.dtype), *ssm_args)
            hs, conv_s, ssm_s = _mamba2_mixer_pallas(r3s(ys), state_ssm_conv, state_ssm, *ssm_args)
            w_out = bf(ssm_w_out)
        elif m == 2:
            hp, k_p, v_p, ik_p = _dsa_prompt_pallas(r3p(yp), att_w_in, att_kn_g, att_kn_b)
            hs, k_s, v_s, ik_s = _dsa_sample_pallas(r3s(ys), cache_k, cache_v, cache_idx_k, page_table,
                                                    att_w_in, att_kn_g, att_kn_b)
            w_out = bf(att_w_out)
        else:
            rw_args = (rw_mu, rw_w_r, rw_w_k, rw_w_v, rw_w0, rw_w1, rw_w2, rw_a0, rw_a1, rw_a2,
                       rw_g1, rw_g2, rw_k_k, rw_k_a, rw_r_k, rw_gn_g, rw_gn_b)
            hp, gate_p, sh_p, wkv_p = _rwkv7_mixer_pallas(
                r3p(yp), jnp.zeros((bp, D_MODEL), yp.dtype),
                jnp.zeros((bp, RW_HEADS, RW_HEAD, RW_HEAD), yp.dtype), *rw_args)
            hs, gate_s, sh_s, wkv_s = _rwkv7_mixer_pallas(r3s(ys), state_rwkv_shift, state_rwkv_wkv, *rw_args)
            w_out = bf(rw_w_o)
        if m == 3:
            yp = _proj_gate_post_norm(yp, hp, gate_p, w_out, ln_g[i, 1], ln_b[i, 1])
            ys = _proj_gate_post_norm(ys, hs, gate_s, w_out, ln_g[i, 1], ln_b[i, 1])
        elif m != 0:
            yp = _proj_post_norm(yp, f2(hp), w_out, ln_g[i, 1], ln_b[i, 1])
            ys = _proj_post_norm(ys, f2(hs), w_out, ln_g[i, 1], ln_b[i, 1])
        yp = _ffn_sub(yp, w_up_bf, w_down_bf, i, 1, ln_g[i, 2], ln_b[i, 2], (pp3, ple_wp_bf, ple_wg_bf, ple_b_g))
        ys = _ffn_sub(ys, w_up_bf, w_down_bf, i, 1, ln_g[i, 2], ln_b[i, 2], (ps3, ple_wp_bf, ple_wg_bf, ple_b_g))

    return (r3p(yp), r3s(ys), gm_v_s, conv_p, ssm_p, conv_s, ssm_s, k_p, v_p, ik_p, k_s, v_s, ik_s,
            sh_p, wkv_p, sh_s, wkv_s)
```

```python
import functools

import jax
import jax.numpy as jnp
from jax import lax
from jax.experimental import pallas as pl
from jax.experimental.pallas import tpu as pltpu

D_MODEL = 1024
DEPTH = 4
N_MIXERS = 4
PLE_DIM = 256
D_FF = 2816
ALPHA = (2 * DEPTH) ** 0.25
LN_EPS = 1e-5

CHUNK = 128
GM_WIDTH = 2 * D_MODEL
GM_GROUPS = 8
GM_GROUP_DIM = GM_WIDTH // GM_GROUPS

SSM_D_INNER = 2 * D_MODEL
SSM_HEAD_DIM = 64
SSM_HEADS = SSM_D_INNER // SSM_HEAD_DIM
SSM_GROUPS = 4
SSM_HPG = SSM_HEADS // SSM_GROUPS
SSM_STATE = 128
SSM_CONV = 4
SSM_CONV_DIM = SSM_D_INNER + 2 * SSM_GROUPS * SSM_STATE
SSM_CHUNK = 128

ATT_HEADS = 16
ATT_KV_HEADS = 4
ATT_HEAD_DIM = D_MODEL // ATT_HEADS
ROPE_DIM = ATT_HEAD_DIM // 4
ROPE_THETA = 500000.0
IDX_HEADS = 8
IDX_DIM = 64
IDX_ROPE_DIM = IDX_DIM // 4
TOPK_MAX = 256
Q_BLOCK = 128
ATT_Q_DIM = ATT_HEADS * ATT_HEAD_DIM
ATT_KV_DIM = ATT_KV_HEADS * ATT_HEAD_DIM
ATT_IN_SPLITS = (ATT_Q_DIM, ATT_Q_DIM + ATT_KV_DIM, ATT_Q_DIM + 2 * ATT_KV_DIM,
                 ATT_Q_DIM + 2 * ATT_KV_DIM + IDX_HEADS * IDX_DIM,
                 ATT_Q_DIM + 2 * ATT_KV_DIM + IDX_HEADS * IDX_DIM + IDX_DIM)

RW_HEAD = 64
RW_HEADS = D_MODEL // RW_HEAD
RW_GN_EPS = 64e-5

V7X_VMEM_LIMIT_BYTES = 52 * 1024 * 1024
FF_TILE = D_FF // 2
ROW_TILE = 512


def _row_tile(m):
    return ROW_TILE if m % ROW_TILE == 0 else m


def _ln_rows(y, g, b):
    mu = jnp.mean(y, axis=-1, keepdims=True)
    yc = y - mu
    var = jnp.mean(yc * yc, axis=-1, keepdims=True)
    return yc * lax.rsqrt(var + LN_EPS) * g + b


FFN_ROW_TILE = 1024
FFN_ROW_PARTS = 2


def _ffn_kernel(x_ref, wu_ref, wd_ref, g_ref, b_ref, *rest, parts, with_ple):
    o_ref = rest[-1]
    rows = x_ref.shape[0] // parts
    for p in range(parts):
        x = x_ref[p * rows:(p + 1) * rows, :]
        xb = x.astype(MXU_DTYPE)
        acc = None
        for f in range(D_FF // FF_TILE):
            cols = slice(f * FF_TILE, (f + 1) * FF_TILE)
            gate = jnp.dot(xb, wu_ref[:, cols], preferred_element_type=jnp.float32)
            lin = jnp.dot(xb, wu_ref[:, D_FF + f * FF_TILE:D_FF + (f + 1) * FF_TILE],
                          preferred_element_type=jnp.float32)
            h = (gate * jax.nn.sigmoid(gate) * lin).astype(MXU_DTYPE)
            part = jnp.dot(h, wd_ref[cols, :], preferred_element_type=jnp.float32)
            acc = part if acc is None else acc + part
        y = _ln_rows(ALPHA * x + 0.5 * acc, g_ref[...], b_ref[...])
        if with_ple:
            p_ref, wp_ref, wg_ref, bg_ref = rest[:4]
            gate = jax.nn.sigmoid(
                jnp.dot(y.astype(MXU_DTYPE), wg_ref[...], preferred_element_type=jnp.float32) + bg_ref[...])
            emb = jnp.dot(p_ref[p * rows:(p + 1) * rows, :].astype(MXU_DTYPE), wp_ref[...],
                          preferred_element_type=jnp.float32)
            y = y + gate * emb
        o_ref[p * rows:(p + 1) * rows, :] = y


def _ffn_sub(x2d, w_up, w_down, layer, half, g, b, ple=None):
    m = x2d.shape[0]
    tm = FFN_ROW_TILE if m % FFN_ROW_TILE == 0 else m
    parts = FFN_ROW_PARTS if tm == FFN_ROW_TILE else 1
    resident = dict(pipeline_mode=pl.Buffered(1))
    vec = pl.BlockSpec((1, D_MODEL), lambda i: (0, 0))
    in_specs = [
        pl.BlockSpec((tm, D_MODEL), lambda i: (i, 0)),
        pl.BlockSpec((None, None, D_MODEL, 2 * D_FF), lambda i: (layer, half, 0, 0), **resident),
        pl.BlockSpec((None, None, D_FF, D_MODEL), lambda i: (layer, half, 0, 0), **resident),
        vec, vec]
    args = [x2d, w_up, w_down, g.reshape(1, D_MODEL), b.reshape(1, D_MODEL)]
    if ple is not None:
        p3d, w_p, w_g, b_g = ple
        in_specs += [pl.BlockSpec((None, tm, PLE_DIM), lambda i: (layer, i, 0)),
                     pl.BlockSpec((None, PLE_DIM, D_MODEL), lambda i: (layer, 0, 0), **resident),
                     pl.BlockSpec((None, D_MODEL, D_MODEL), lambda i: (layer, 0, 0), **resident),
                     pl.BlockSpec((None, 1, D_MODEL), lambda i: (layer, 0, 0))]
        args += [p3d, w_p, w_g, b_g.reshape(DEPTH, 1, D_MODEL)]
    return pl.pallas_call(
        functools.partial(_ffn_kernel, parts=parts, with_ple=ple is not None),
        grid=(m // tm,),
        in_specs=in_specs,
        out_specs=pl.BlockSpec((tm, D_MODEL), lambda i: (i, 0)),
        out_shape=jax.ShapeDtypeStruct((m, D_MODEL), jnp.float32),
        compiler_params=pltpu.CompilerParams(
            dimension_semantics=("parallel",),
            vmem_limit_bytes=V7X_VMEM_LIMIT_BYTES),
        name="ffn_ple" if ple is not None else "ffn_sub",
    )(*args)


def _proj_ln_kernel(x_ref, h_ref, w_ref, g_ref, b_ref, o_ref):
    y = ALPHA * x_ref[...] + jnp.dot(h_ref[...].astype(jnp.bfloat16), w_ref[...],
                                     preferred_element_type=jnp.float32)
    o_ref[...] = _ln_rows(y, g_ref[...], b_ref[...])


def _proj_post_norm(x2d, h2d, w_out, g, b):
    m = x2d.shape[0]
    k = h2d.shape[1]
    tm = _row_tile(m)
    return pl.pallas_call(
        _proj_ln_kernel,
        grid=(m // tm,),
        in_specs=[
            pl.BlockSpec((tm, D_MODEL), lambda i: (i, 0)),
            pl.BlockSpec((tm, k), lambda i: (i, 0)),
            pl.BlockSpec((k, D_MODEL), lambda i: (0, 0)),
            pl.BlockSpec((1, D_MODEL), lambda i: (0, 0)),
            pl.BlockSpec((1, D_MODEL), lambda i: (0, 0)),
        ],
        out_specs=pl.BlockSpec((tm, D_MODEL), lambda i: (i, 0)),
        out_shape=jax.ShapeDtypeStruct((m, D_MODEL), jnp.float32),
        compiler_params=pltpu.CompilerParams(
            dimension_semantics=("parallel",),
            vmem_limit_bytes=V7X_VMEM_LIMIT_BYTES),
        name="proj_post_norm",
    )(x2d, h2d, w_out, g.reshape(1, D_MODEL), b.reshape(1, D_MODEL))


MXU_DTYPE = jnp.bfloat16
KEY_GROUP = 512
INT32_MIN = -2 ** 31
MASK_NEG = -1e30


def _rope_lane_tables(pos, rot_dim, head_dim):
    half = rot_dim // 2
    inv = ROPE_THETA ** (-jnp.arange(half, dtype=jnp.float32) / half)
    ang = pos.astype(jnp.float32)[:, None] * inv[None, :]
    cos, sin = jnp.cos(ang), jnp.sin(ang)
    n = pos.shape[0]
    rest = head_dim - rot_dim
    c = jnp.concatenate([cos, cos, jnp.ones((n, rest), jnp.float32)], axis=1)
    s1 = jnp.concatenate([-sin, jnp.zeros((n, half + rest), jnp.float32)], axis=1)
    s2 = jnp.concatenate([jnp.zeros((n, half), jnp.float32), sin, jnp.zeros((n, rest), jnp.float32)], axis=1)
    reps = 128 // head_dim
    tile = lambda t: jnp.tile(t, (1, reps))
    return tile(c), tile(s1), tile(s2), cos.T, sin.T


def _rope_lanes(t, c, s1, s2, half):
    n = t.shape[1]
    reps = n // 128
    tl = lambda a: jnp.concatenate([a] * reps, axis=1)
    return t * tl(c) + pltpu.roll(t, n - half, 1) * tl(s1) + pltpu.roll(t, half, 1) * tl(s2)


def _rope_rows(t, cT, sT, head_dim, half):
    pieces = []
    for h in range(t.shape[0] // head_dim):
        x1 = t[h * head_dim:h * head_dim + half]
        x2 = t[h * head_dim + half:h * head_dim + 2 * half]
        pieces += [x1 * cT - x2 * sT, x2 * cT + x1 * sT, t[h * head_dim + 2 * half:(h + 1) * head_dim]]
    return jnp.concatenate(pieces, axis=0)


def _dsa_proj_kernel(x_ref, wq_ref, wiq_ref, wv_ref, wvx_ref, wiw_ref, wkT_ref, wikT_ref,
                     c_ref, s1_ref, s2_ref, cT_ref, sT_ref, kng_ref, knb_ref, one_ref,
                     q_ref, iq_ref, v_ref, vx_ref, iw_ref, kT_ref, kTb_ref, ikT_ref, ikTb_ref):
    xb = x_ref[...].astype(MXU_DTYPE)
    c, s1, s2 = c_ref[...], s1_ref[...], s2_ref[...]
    cT, sT = cT_ref[...], sT_ref[...]
    dot = lambda a, b: jnp.dot(a, b, preferred_element_type=jnp.float32)
    dot_t = lambda w, a: lax.dot_general(w, a, (((1,), (1,)), ((), ())), preferred_element_type=jnp.float32)

    q = _rope_lanes(dot(xb, wq_ref[...]), c, s1, s2, ROPE_DIM // 2)
    q_ref[...] = (q * (ATT_HEAD_DIM ** -0.5)).astype(q_ref.dtype)
    iq = _rope_lanes(dot(xb, wiq_ref[...]), c, s1, s2, IDX_ROPE_DIM // 2)
    iq_ref[...] = iq.astype(iq_ref.dtype)
    v_ref[...] = dot(xb, wv_ref[...])
    vx_ref[...] = (dot(xb, wvx_ref[...]) + one_ref[...]).astype(vx_ref.dtype)
    iw_ref[...] = dot(xb, wiw_ref[...]) * (IDX_HEADS ** -0.5 * IDX_DIM ** -0.5)

    kT = _rope_rows(dot_t(wkT_ref[...], xb), cT, sT, ATT_HEAD_DIM, ROPE_DIM // 2)
    kT_ref[0, 0] = kT
    kTb_ref[0, 0] = kT.astype(kTb_ref.dtype)
    ikT = dot_t(wikT_ref[...], xb)
    mu = jnp.mean(ikT, axis=0, keepdims=True)
    ikc = ikT - mu
    var = jnp.mean(ikc * ikc, axis=0, keepdims=True)
    ikT = ikc * lax.rsqrt(var + LN_EPS) * kng_ref[...] + knb_ref[...]
    ikT = _rope_rows(ikT, cT, sT, IDX_DIM, IDX_ROPE_DIM // 2)
    ikT_ref[0, 0] = ikT
    ikTb_ref[0, 0] = ikT.astype(ikTb_ref.dtype)


def _dsa_project(x3d, pos, w_in, kn_g, kn_b):
    b_, t_, _ = x3d.shape
    tk = KEY_GROUP if t_ % KEY_GROUP == 0 else t_
    ng = t_ // tk
    m = b_ * t_
    w_q, w_k, w_v, w_iq, w_ik, w_iw = jnp.split(w_in, list(ATT_IN_SPLITS), axis=1)
    cast = lambda w: w.astype(MXU_DTYPE)
    w_vx = jnp.pad(w_v.reshape(D_MODEL, ATT_KV_HEADS, ATT_HEAD_DIM),
                   ((0, 0), (0, 0), (0, 128 - ATT_HEAD_DIM))).reshape(D_MODEL, ATT_KV_HEADS * 128)
    one_col = jnp.tile((jnp.arange(128) == ATT_HEAD_DIM).astype(jnp.float32), ATT_KV_HEADS)[None, :]
    w_iw_pad = jnp.pad(w_iw, ((0, 0), (0, 128 - IDX_HEADS)))
    c, s1, s2, cT, sT = _rope_lane_tables(pos, ROPE_DIM, ATT_HEAD_DIM)
    full = lambda shape: pl.BlockSpec(shape, lambda b, i: (0,) * len(shape))
    rows = lambda n: pl.BlockSpec((tk, n), lambda b, i: (b * ng + i, 0))
    ptab = lambda n: pl.BlockSpec((tk, n), lambda b, i: (i, 0))
    grp = lambda n: pl.BlockSpec((1, 1, n, tk), lambda b, i: (b, i, 0, 0))
    sds = jax.ShapeDtypeStruct
    return pl.pallas_call(
        _dsa_proj_kernel,
        grid=(b_, ng),
        in_specs=[rows(D_MODEL), full((D_MODEL, ATT_Q_DIM)), full((D_MODEL, IDX_HEADS * IDX_DIM)),
                  full((D_MODEL, ATT_KV_DIM)), full((D_MODEL, ATT_KV_HEADS * 128)), full((D_MODEL, 128)),
                  full((ATT_KV_DIM, D_MODEL)), full((IDX_DIM, D_MODEL)),
                  ptab(128), ptab(128), ptab(128),
                  pl.BlockSpec((ROPE_DIM // 2, tk), lambda b, i: (0, i)),
                  pl.BlockSpec((ROPE_DIM // 2, tk), lambda b, i: (0, i)),
                  full((IDX_DIM, 1)), full((IDX_DIM, 1)), full((1, ATT_KV_HEADS * 128))],
        out_specs=[rows(ATT_Q_DIM), rows(IDX_HEADS * IDX_DIM), rows(ATT_KV_DIM), rows(ATT_KV_HEADS * 128),
                   rows(128), grp(ATT_KV_DIM), grp(ATT_KV_DIM), grp(IDX_DIM), grp(IDX_DIM)],
        out_shape=[sds((m, ATT_Q_DIM), MXU_DTYPE), sds((m, IDX_HEADS * IDX_DIM), MXU_DTYPE),
                   sds((m, ATT_KV_DIM), jnp.float32), sds((m, ATT_KV_HEADS * 128), MXU_DTYPE),
                   sds((m, 128), jnp.float32),
                   sds((b_, ng, ATT_KV_DIM, tk), jnp.float32), sds((b_, ng, ATT_KV_DIM, tk), MXU_DTYPE),
                   sds((b_, ng, IDX_DIM, tk), jnp.float32), sds((b_, ng, IDX_DIM, tk), MXU_DTYPE)],
        compiler_params=pltpu.CompilerParams(
            dimension_semantics=("parallel", "parallel"),
            vmem_limit_bytes=V7X_VMEM_LIMIT_BYTES),
        name="dsa_project",
    )(x3d.reshape(m, D_MODEL), cast(w_q), cast(w_iq), cast(w_v), cast(w_vx), cast(w_iw_pad),
      cast(w_k.T), cast(w_ik.T), c, s1, s2, cT, sT, kn_g.reshape(IDX_DIM, 1), kn_b.reshape(IDX_DIM, 1), one_col)


def _untranspose_groups(tg):
    b_, g_, r_, tk = tg.shape
    return jnp.transpose(tg, (0, 1, 3, 2)).reshape(b_, g_ * tk, r_)


def _dsa_proj_q_lanes_kernel(x_ref, wqT_ref, wiqT_ref, wiwT_ref, wk_ref, wv_ref, wvxT_ref, wik_ref,
                             c_ref, s1_ref, s2_ref, cT_ref, sT_ref, kng_ref, knb_ref, onerow_ref,
                             qT_ref, iqT_ref, iwT_ref, k_ref, khd_ref, v_ref, vxT_ref, ik_ref, ikb_ref):
    xb = x_ref[...].astype(MXU_DTYPE)
    tm = xb.shape[0]
    c, s1, s2 = c_ref[...], s1_ref[...], s2_ref[...]
    cT, sT = cT_ref[...], sT_ref[...]
    dot = lambda a, b: jnp.dot(a, b, preferred_element_type=jnp.float32)
    dot_t = lambda w, a: lax.dot_general(w, a, (((1,), (1,)), ((), ())), preferred_element_type=jnp.float32)

    qT = _rope_rows(dot_t(wqT_ref[...], xb), cT, sT, ATT_HEAD_DIM, ROPE_DIM // 2) * (ATT_HEAD_DIM ** -0.5)
    iqT = _rope_rows(dot_t(wiqT_ref[...], xb), cT, sT, IDX_DIM, IDX_ROPE_DIM // 2)
    iwT = dot_t(wiwT_ref[...], xb) * (IDX_HEADS ** -0.5 * IDX_DIM ** -0.5)
    for t in range(tm // Q_BLOCK):
        lanes = slice(t * Q_BLOCK, (t + 1) * Q_BLOCK)
        qT_ref[0, t] = qT[:, lanes].astype(qT_ref.dtype)
        iqT_ref[0, t] = iqT[:, lanes].astype(iqT_ref.dtype)
        iwT_ref[0, t] = iwT[:, lanes]

    k = _rope_lanes(dot(xb, wk_ref[...]), c, s1, s2, ROPE_DIM // 2)
    k_ref[...] = k
    for g in range(ATT_KV_HEADS):
        khd_ref[g] = k[:, g * ATT_HEAD_DIM:(g + 1) * ATT_HEAD_DIM].astype(khd_ref.dtype)
    v_ref[...] = dot(xb, wv_ref[...])
    vxT_ref[0, 0] = (dot_t(wvxT_ref[...], xb) + onerow_ref[...]).astype(vxT_ref.dtype)

    ik = dot(xb, wik_ref[...])
    real = lax.broadcasted_iota(jnp.int32, ik.shape, 1) < IDX_DIM
    mu = jnp.sum(ik, axis=-1, keepdims=True) * (1.0 / IDX_DIM)
    ikc = jnp.where(real, ik - mu, 0.0)
    var = jnp.sum(ikc * ikc, axis=-1, keepdims=True) * (1.0 / IDX_DIM)
    ikn = _rope_lanes(ikc * lax.rsqrt(var + LN_EPS) * kng_ref[...] + knb_ref[...], c, s1, s2, IDX_ROPE_DIM // 2)
    ik_ref[...] = ikn[:, :IDX_DIM]
    ikb_ref[...] = ikn[:, :IDX_DIM].astype(ikb_ref.dtype)


def _dsa_project_q_lanes(x3d, pos, w_in, kn_g, kn_b):
    b_, t_, _ = x3d.shape
    tk = KEY_GROUP
    ng = t_ // tk
    nq = tk // Q_BLOCK
    m = b_ * t_
    w_q, w_k, w_v, w_iq, w_ik, w_iw = jnp.split(w_in, list(ATT_IN_SPLITS), axis=1)
    cast = lambda w: w.astype(MXU_DTYPE)
    w_vxT = jnp.pad(w_v.T.reshape(ATT_KV_HEADS, ATT_HEAD_DIM, D_MODEL),
                    ((0, 0), (0, 128 - ATT_HEAD_DIM), (0, 0))).reshape(ATT_KV_HEADS * 128, D_MODEL)
    one_row = jnp.tile((jnp.arange(128) == ATT_HEAD_DIM).astype(jnp.float32), ATT_KV_HEADS)[:, None]
    pad_lanes = lambda a: jnp.pad(a, ((0, 0), (0, 128 - a.shape[1])))
    c, s1, s2, cT, sT = _rope_lane_tables(pos, ROPE_DIM, ATT_HEAD_DIM)
    full = lambda shape: pl.BlockSpec(shape, lambda b, i: (0,) * len(shape))
    rows = lambda n: pl.BlockSpec((tk, n), lambda b, i: (b * ng + i, 0))
    ptab = lambda n: pl.BlockSpec((tk, n), lambda b, i: (i, 0))
    qtile = lambda n: pl.BlockSpec((1, nq, n, Q_BLOCK), lambda b, i: (b, i, 0, 0))
    sds = jax.ShapeDtypeStruct
    return pl.pallas_call(
        _dsa_proj_q_lanes_kernel,
        grid=(b_, ng),
        in_specs=[rows(D_MODEL), full((ATT_Q_DIM, D_MODEL)), full((IDX_HEADS * IDX_DIM, D_MODEL)),
                  full((IDX_HEADS, D_MODEL)), full((D_MODEL, ATT_KV_DIM)), full((D_MODEL, ATT_KV_DIM)),
                  full((ATT_KV_HEADS * 128, D_MODEL)), full((D_MODEL, 128)),
                  ptab(128), ptab(128), ptab(128),
                  pl.BlockSpec((ROPE_DIM // 2, tk), lambda b, i: (0, i)),
                  pl.BlockSpec((ROPE_DIM // 2, tk), lambda b, i: (0, i)),
                  full((1, 128)), full((1, 128)), full((ATT_KV_HEADS * 128, 1))],
        out_specs=[qtile(ATT_Q_DIM), qtile(IDX_HEADS * IDX_DIM), qtile(IDX_HEADS),
                   rows(ATT_KV_DIM), pl.BlockSpec((ATT_KV_HEADS, tk, ATT_HEAD_DIM), lambda b, i: (0, b * ng + i, 0)),
                   rows(ATT_KV_DIM), pl.BlockSpec((1, 1, ATT_KV_HEADS * 128, tk), lambda b, i: (b, i, 0, 0)),
                   rows(IDX_DIM), rows(IDX_DIM)],
        out_shape=[sds((b_, t_ // Q_BLOCK, ATT_Q_DIM, Q_BLOCK), MXU_DTYPE),
                   sds((b_, t_ // Q_BLOCK, IDX_HEADS * IDX_DIM, Q_BLOCK), MXU_DTYPE),
                   sds((b_, t_ // Q_BLOCK, IDX_HEADS, Q_BLOCK), jnp.float32),
                   sds((m, ATT_KV_DIM), jnp.float32), sds((ATT_KV_HEADS, m, ATT_HEAD_DIM), MXU_DTYPE),
                   sds((m, ATT_KV_DIM), jnp.float32), sds((b_, ng, ATT_KV_HEADS * 128, tk), MXU_DTYPE),
                   sds((m, IDX_DIM), jnp.float32), sds((m, IDX_DIM), MXU_DTYPE)],
        compiler_params=pltpu.CompilerParams(
            dimension_semantics=("parallel", "parallel"),
            vmem_limit_bytes=V7X_VMEM_LIMIT_BYTES),
        name="dsa_project_q_lanes",
    )(x3d.reshape(m, D_MODEL), cast(w_q.T), cast(w_iq.T), cast(w_iw.T), cast(w_k), cast(w_v), cast(w_vxT),
      cast(pad_lanes(w_ik)), c, s1, s2, cT, sT, pad_lanes(kn_g.reshape(1, IDX_DIM)),
      pad_lanes(kn_b.reshape(1, IDX_DIM)), one_row)


def _tree_sum(parts):
    while len(parts) > 1:
        parts = [parts[i] + parts[i + 1] for i in range(0, len(parts) - 1, 2)] + (
            [parts[-1]] if len(parts) % 2 else [])
    return parts[0]


def _dsa_attend_q_lanes_kernel(iqT_ref, iwT_ref, ik_ref, qT_ref, k_ref, vxT_ref, o_ref,
                               key_ref, bias_ref, m_ref, acc_ref, s_ref, *, topk, col_bits):
    j = pl.program_id(1)
    tk, tq = key_ref.shape[1], key_ref.shape[2]
    n_groups = (j * tq + tq + tk - 1) // tk
    qpos = j * tq + lax.broadcasted_iota(jnp.int32, (tk, tq), 1)
    kpos0 = lax.broadcasted_iota(jnp.int32, (tk, tq), 0)
    dot = lambda a, b: jnp.dot(a, b, preferred_element_type=jnp.float32)

    def score_body(g, carry):
        start = pl.multiple_of(g * tk, tk)
        w_iq = jnp.concatenate([iqT_ref[0, 0, h * IDX_DIM:(h + 1) * IDX_DIM, :] for h in range(IDX_HEADS)], axis=1)
        s_all = dot(ik_ref[0, pl.ds(start, tk), :], w_iq)
        sc = _tree_sum([iwT_ref[0, 0, h:h + 1, :] * jnp.maximum(s_all[:, h * tq:(h + 1) * tq], 0.0)
                        for h in range(IDX_HEADS)])
        key_ref[g] = jnp.where(kpos0 + g * tk <= qpos, _sortable_key(sc), jnp.int32(INT32_MIN))
        return carry

    lax.fori_loop(0, n_groups, score_body, 0)

    def count_keys(pred):
        def body(g, part):
            hit = jnp.where(pred(key_ref[g], kpos0 + g * tk), 1.0, 0.0)
            return part + _tree_sum([hit[r * SUBLANES:(r + 1) * SUBLANES] for r in range(tk // SUBLANES)])
        part = lax.fori_loop(0, n_groups, body, jnp.zeros((SUBLANES, tq), jnp.float32))
        return jnp.sum(part, axis=0, keepdims=True)

    def bit_body(i, thr):
        cand = thr ^ lax.shift_left(jnp.int32(1), jnp.int32(31) - i)
        return jnp.where(count_keys(lambda k, kp: k >= cand) >= float(topk), cand, thr)

    thr = lax.fori_loop(0, 32, bit_body, jnp.full((1, tq), INT32_MIN, jnp.int32))

    need = float(topk) - count_keys(lambda k, kp: k > thr)

    def pos_body(i, last):
        cand = last | lax.shift_left(jnp.int32(1), jnp.int32(col_bits - 1) - i)
        return jnp.where(count_keys(lambda k, kp: (k == thr) & (kp < cand)) < need, cand, last)

    n_tied = count_keys(lambda k, kp: k == thr)
    excess = jnp.max(jnp.where(n_tied > need, 1.0, 0.0), axis=1, keepdims=True)
    last_tie = lax.cond(excess[0, 0] > 0.0,
                        lambda: lax.fori_loop(0, col_bits, pos_body, jnp.zeros((1, tq), jnp.int32)),
                        lambda: jnp.full((1, tq), 2 ** col_bits - 1, jnp.int32))

    m_ref[...] = jnp.full(m_ref.shape, MASK_NEG, jnp.float32)
    acc_ref[...] = jnp.zeros(acc_ref.shape, jnp.float32)
    gsz = ATT_HEADS // ATT_KV_HEADS

    def attend_body(g, carry):
        start = pl.multiple_of(g * tk, tk)
        key = key_ref[g]
        kpos = kpos0 + g * tk
        keep = (key > thr) | ((key == thr) & (kpos <= last_tie))
        bias_ref[...] = jnp.where(keep & (kpos <= qpos), 0.0, MASK_NEG)
        for kv in range(ATT_KV_HEADS):
            w_q = jnp.concatenate([qT_ref[0, 0, (kv * gsz + i) * ATT_HEAD_DIM:(kv * gsz + i + 1) * ATT_HEAD_DIM, :]
                                   for i in range(gsz)], axis=1)
            s_ref[kv] = dot(k_ref[kv, pl.ds(start, tk), :], w_q)
        for kv in range(ATT_KV_HEADS):
            s = s_ref[kv] + jnp.concatenate([bias_ref[...]] * gsz, axis=1)
            m_old = m_ref[kv]
            m_new = jnp.maximum(m_old, jnp.max(s, axis=0, keepdims=True))
            p = jnp.exp(s - m_new).astype(vxT_ref.dtype)
            pv = dot(vxT_ref[0, g, kv * 128:(kv + 1) * 128, :], p)
            acc_ref[kv] = jnp.exp(m_old - m_new) * acc_ref[kv] + pv
            m_ref[kv] = m_new
        return carry

    lax.fori_loop(0, n_groups, attend_body, 0)

    for h in range(ATT_HEADS):
        a = acc_ref[h // gsz, :, (h % gsz) * tq:(h % gsz + 1) * tq]
        o = (a / a[ATT_HEAD_DIM:ATT_HEAD_DIM + 1, :]).T
        o_ref[:, h * ATT_HEAD_DIM:(h + 1) * ATT_HEAD_DIM] = o[:, :ATT_HEAD_DIM].astype(o_ref.dtype)


def _dsa_attend_q_lanes(b_, t_, qT, iqT, iwT, ikb, khd, vxT):
    ng, tk = vxT.shape[1], vxT.shape[3]
    tq = Q_BLOCK
    nq = t_ // tq
    qtile = lambda n: pl.BlockSpec((1, 1, n, tq), lambda b, j: (b, j, 0, 0))
    return pl.pallas_call(
        functools.partial(_dsa_attend_q_lanes_kernel, topk=min(TOPK_MAX, t_ // 4),
                          col_bits=max(1, (t_ - 1).bit_length())),
        grid=(b_, nq),
        in_specs=[qtile(IDX_HEADS * IDX_DIM), qtile(IDX_HEADS),
                  pl.BlockSpec((1, t_, IDX_DIM), lambda b, j: (b, 0, 0)),
                  qtile(ATT_Q_DIM),
                  pl.BlockSpec((ATT_KV_HEADS, t_, ATT_HEAD_DIM), lambda b, j: (0, b, 0)),
                  pl.BlockSpec((1, ng, ATT_KV_HEADS * 128, tk), lambda b, j: (b, 0, 0, 0))],
        out_specs=pl.BlockSpec((tq, ATT_Q_DIM), lambda b, j: (b * nq + j, 0)),
        out_shape=jax.ShapeDtypeStruct((b_ * t_, ATT_Q_DIM), MXU_DTYPE),
        scratch_shapes=[pltpu.VMEM((ng, tk, tq), jnp.int32),
                        pltpu.VMEM((tk, tq), jnp.float32),
                        pltpu.VMEM((ATT_KV_HEADS, 1, tq * (ATT_HEADS // ATT_KV_HEADS)), jnp.float32),
                        pltpu.VMEM((ATT_KV_HEADS, 128, tq * (ATT_HEADS // ATT_KV_HEADS)), jnp.float32),
                        pltpu.VMEM((ATT_KV_HEADS, tk, tq * (ATT_HEADS // ATT_KV_HEADS)), jnp.float32)],
        compiler_params=pltpu.CompilerParams(
            dimension_semantics=("parallel", "arbitrary"),
            vmem_limit_bytes=V7X_VMEM_LIMIT_BYTES),
        name="dsa_attend_q_lanes",
    )(iqT, iwT, ikb.reshape(b_, t_, IDX_DIM), qT, khd, vxT)


def _dsa_prompt_pallas(x3d, w_in, kn_g, kn_b):
    b_, t_, _ = x3d.shape
    qT, iqT, iwT, k, khd, v, vxT, ik, ikb = _dsa_project_q_lanes(x3d, jnp.arange(t_), w_in, kn_g, kn_b)
    o = _dsa_attend_q_lanes(b_, t_, qT, iqT, iwT, ikb, khd, vxT)
    kv4 = lambda u: u.reshape(b_, t_, ATT_KV_HEADS, ATT_HEAD_DIM)
    return o, kv4(k), kv4(v), ik.reshape(b_, t_, IDX_DIM)


RW_ROW_TILE = 256


RW_PAIRS = RW_HEADS // 2
RW_PAIR_LANES = 2 * RW_HEAD


def _rwkv_project_rows(x, xp, mu_ref, wr_ref, wk_ref, wv_ref, w1_ref, w2_ref, a1_ref, a2_ref,
                       g1_ref, g2_ref, w0_ref, a0_ref):
    dx = xp - x
    mix = lambda c: (x + dx * mu_ref[c:c + 1, :]).astype(MXU_DTYPE)
    dot = lambda a, b: jnp.dot(a.astype(MXU_DTYPE), b, preferred_element_type=jnp.float32)
    r = dot(mix(0), wr_ref[...])
    lora_w = dot(jnp.tanh(dot(mix(1), w1_ref[...])), w2_ref[...])
    w_log = -jax.nn.softplus(-(w0_ref[...] + lora_w)) - 0.5
    d = jnp.exp(-jnp.exp(w_log))
    k = dot(mix(2), wk_ref[...])
    v = dot(mix(3), wv_ref[...])
    a = jax.nn.sigmoid(a0_ref[...] + dot(dot(mix(4), a1_ref[...]), a2_ref[...]))
    g = dot(jax.nn.sigmoid(dot(mix(5), g1_ref[...])), g2_ref[...])
    return r, d, k, v, a, g


def _rwkv_proj_step_kernel(x_ref, xp_ref, *refs):
    vals = _rwkv_project_rows(x_ref[...], xp_ref[...], *refs[:12])
    for ref, val in zip(refs[12:], vals):
        ref[...] = val


def _rwkv_proj_seq_kernel(x_ref, halo_ref, shift_ref, *refs):
    i = pl.program_id(1)
    x = x_ref[...]
    prev = jnp.where(i == 0, shift_ref[0], halo_ref[...])[SUBLANES - 1:SUBLANES, :]
    first = lax.broadcasted_iota(jnp.int32, (x.shape[0], 1), 0) == 0
    xp = jnp.where(first, prev, pltpu.roll(x, 1, 0))
    vals = _rwkv_project_rows(x, xp, *refs[:12])
    for ref, val in zip(refs[12:], vals):
        ref[...] = val


def _rwkv_consts(mu, w_r, w_k, w_v, w0, w1, w2, a0, a1, a2, g1, g2):
    cast = lambda w: w.astype(MXU_DTYPE)
    return [mu, cast(w_r), cast(w_k), cast(w_v), cast(w1), cast(w2), cast(a1), cast(a2), cast(g1), cast(g2),
            w0.reshape(1, D_MODEL), a0.reshape(1, D_MODEL)]


def _rwkv_project_step(x2d, xprev2d, *params):
    m = x2d.shape[0]
    consts = _rwkv_consts(*params)
    full = lambda a: pl.BlockSpec(a.shape, lambda i: (0,) * a.ndim)
    rows = pl.BlockSpec((m, D_MODEL), lambda i: (0, 0))
    return pl.pallas_call(
        _rwkv_proj_step_kernel,
        grid=(1,),
        in_specs=[rows, rows] + [full(a) for a in consts],
        out_specs=[rows] * 6,
        out_shape=[jax.ShapeDtypeStruct((m, D_MODEL), jnp.float32)] * 6,
        compiler_params=pltpu.CompilerParams(
            dimension_semantics=("arbitrary",),
            vmem_limit_bytes=V7X_VMEM_LIMIT_BYTES),
        name="rwkv_project_step",
    )(x2d, xprev2d, *consts)


def _rwkv_project_seq(x3d, shift, *params):
    b_, t_, _ = x3d.shape
    m = b_ * t_
    tm = RW_ROW_TILE
    nt = t_ // tm
    consts = _rwkv_consts(*params)
    full = lambda a: pl.BlockSpec(a.shape, lambda b, i: (0,) * a.ndim)
    rows = pl.BlockSpec((tm, D_MODEL), lambda b, i: (b * nt + i, 0))
    halo = pl.BlockSpec((SUBLANES, D_MODEL), lambda b, i: (jnp.maximum((b * nt + i) * (tm // SUBLANES) - 1, 0), 0))
    shift8 = jnp.pad(shift[:, None, :], ((0, 0), (SUBLANES - 1, 0), (0, 0)))
    x2d = x3d.reshape(m, D_MODEL)
    return pl.pallas_call(
        _rwkv_proj_seq_kernel,
        grid=(b_, nt),
        in_specs=[rows, halo, pl.BlockSpec((1, SUBLANES, D_MODEL), lambda b, i: (b, 0, 0))]
        + [full(a) for a in consts],
        out_specs=[rows] * 6,
        out_shape=[jax.ShapeDtypeStruct((m, D_MODEL), jnp.float32)] * 6,
        compiler_params=pltpu.CompilerParams(
            dimension_semantics=("parallel", "parallel"),
            vmem_limit_bytes=V7X_VMEM_LIMIT_BYTES),
        name="rwkv_project_seq",
    )(x2d, x2d, shift8, *consts)


RW_LANES = 128
RW_TIME_CHUNK = 32


def _rwkv_scan_kernel(r_ref, d_ref, k_ref, v_ref, a_ref, s0_ref, kk_ref, ka_ref, rk_ref, gg_ref, gb_ref,
                      z_ref, s_out_ref, s_ref, vec_ref):
    c = pl.program_id(1)
    n = RW_HEAD
    tc = r_ref.shape[1]
    low_half = lax.broadcasted_iota(jnp.int32, (n, RW_LANES), 1) < n

    @pl.when(c == 0)
    def _():
        s_ref[...] = s0_ref[...]

    def swap_layout(x):
        xt = jnp.concatenate([x, x], axis=0).T
        return jnp.where(low_half, xt[:n], xt[n:])

    def load_step(ref, t):
        rows = ref[:, t, :]
        return swap_layout(jnp.concatenate(
            [rows[:, p * RW_PAIR_LANES:(p + 1) * RW_PAIR_LANES] for p in range(RW_PAIRS)], axis=0))

    def store_step(ref, t, val):
        tile = swap_layout(val)
        ref[:, t, :] = jnp.concatenate(
            [tile[p * RW_SEQ_PER_TILE:(p + 1) * RW_SEQ_PER_TILE] for p in range(RW_PAIRS)], axis=1)

    def prepare(t, slot):
        r, k, a = load_step(r_ref, t), load_step(k_ref, t), load_step(a_ref, t)
        kkr = k * kk_ref[...]
        nrm = jnp.sqrt(jnp.sum(kkr * kkr, axis=0, keepdims=True))
        kk = kkr / jnp.maximum(nrm, 1e-12)
        vec_ref[slot, 0] = kk
        vec_ref[slot, 1] = load_step(d_ref, t)
        vec_ref[slot, 2] = kk * a
        vec_ref[slot, 3] = k * (1.0 + (a - 1.0) * ka_ref[...])
        vec_ref[slot, 4] = r
        vec_ref[slot, 5] = load_step(v_ref, t)

    def step(t, slot):
        row = lambda q, j: vec_ref[slot, q, j:j + 1, :]
        v = vec_ref[slot, 5]
        lanes = 4
        sa_parts = [s_ref[j] * row(0, j) for j in range(lanes)]
        for j in range(lanes, n):
            sa_parts[j % lanes] = sa_parts[j % lanes] + s_ref[j] * row(0, j)
        sa = _tree_sum(sa_parts)
        y_parts = []
        for j in range(n):
            sn = s_ref[j] * row(1, j) - sa * row(2, j) + v * row(3, j)
            s_ref[j] = sn
            if j < lanes:
                y_parts.append(sn * row(4, j))
            else:
                y_parts[j % lanes] = y_parts[j % lanes] + sn * row(4, j)
        y = _tree_sum(y_parts)
        mu = jnp.mean(y, axis=0, keepdims=True)
        yc = y - mu
        var = jnp.mean(yc * yc, axis=0, keepdims=True)
        bonus = jnp.sum(vec_ref[slot, 4] * vec_ref[slot, 3] * rk_ref[...], axis=0, keepdims=True)
        store_step(z_ref, t, yc * lax.rsqrt(var + RW_GN_EPS) * gg_ref[...] + gb_ref[...] + bonus * v)

    prepare(0, 0)
    if tc == 1:
        step(0, 0)
    else:
        def two_steps(i, carry):
            t = 2 * i
            prepare(t + 1, 1)
            step(t, 0)
            prepare(jnp.minimum(t + 2, tc - 1), 0)
            step(t + 1, 1)
            return carry

        lax.fori_loop(0, tc // 2, two_steps, 0)

    @pl.when(c == pl.num_programs(1) - 1)
    def _():
        s_out_ref[...] = s_ref[...]


RW_SEQ_PER_TILE = RW_LANES // RW_HEADS


def _rwkv_lane_heads():
    half = jnp.arange(2)[:, None, None]
    pair = jnp.arange(RW_PAIRS)[None, :, None]
    return jnp.broadcast_to(2 * pair + half, (2, RW_PAIRS, RW_SEQ_PER_TILE)).reshape(RW_LANES)


def _rwkv_scan(r, d, k, v, a, s0, k_k, k_a, r_k, gn_g, gn_b):
    b_, t_, _ = r.shape
    n = RW_HEAD
    tc = RW_TIME_CHUNK if t_ % RW_TIME_CHUNK == 0 else t_
    table = lambda p: p.reshape(RW_HEADS, n)[_rwkv_lane_heads()].T
    seq = pl.BlockSpec((RW_SEQ_PER_TILE, tc, D_MODEL), lambda l, c: (l, c, 0))
    state = pl.BlockSpec((n, n, RW_LANES), lambda l, c: (0, 0, l))
    tab = pl.BlockSpec((n, RW_LANES), lambda l, c: (0, 0))
    return pl.pallas_call(
        _rwkv_scan_kernel,
        grid=(b_ // RW_SEQ_PER_TILE, t_ // tc),
        in_specs=[seq] * 5 + [state] + [tab] * 5,
        out_specs=[seq, state],
        out_shape=[jax.ShapeDtypeStruct(r.shape, jnp.float32),
                   jax.ShapeDtypeStruct(s0.shape, jnp.float32)],
        scratch_shapes=[pltpu.VMEM((n, n, RW_LANES), jnp.float32),
                        pltpu.VMEM((2, 6, n, RW_LANES), jnp.float32)],
        compiler_params=pltpu.CompilerParams(
            dimension_semantics=("parallel", "arbitrary"),
            vmem_limit_bytes=V7X_VMEM_LIMIT_BYTES),
        name="rwkv_scan",
    )(r, d, k, v, a, s0, table(k_k), table(k_a), table(r_k), table(gn_g), table(gn_b))


def _rwkv_state_to_lanes(wkv):
    b_ = wkv.shape[0]
    w = wkv.astype(jnp.float32).reshape(b_ // RW_SEQ_PER_TILE, RW_SEQ_PER_TILE, RW_PAIRS, 2, RW_HEAD, RW_HEAD)
    return jnp.transpose(w, (5, 4, 0, 3, 2, 1)).reshape(RW_HEAD, RW_HEAD, b_ * RW_HEADS)


def _rwkv_state_from_lanes(s, b_):
    w = s.reshape(RW_HEAD, RW_HEAD, b_ // RW_SEQ_PER_TILE, 2, RW_PAIRS, RW_SEQ_PER_TILE)
    return jnp.transpose(w, (2, 5, 4, 3, 1, 0)).reshape(b_, RW_HEADS, RW_HEAD, RW_HEAD)


def _rwkv7_mixer_pallas(x3d, shift, wkv, mu, w_r, w_k, w_v, w0, w1, w2, a0, a1, a2, g1, g2,
                        k_k, k_a, r_k, gn_g, gn_b):
    b_, t_, _ = x3d.shape
    params = (mu, w_r, w_k, w_v, w0, w1, w2, a0, a1, a2, g1, g2)
    if t_ == 1:
        *seqs, g = _rwkv_project_step(x3d.reshape(b_, D_MODEL), shift, *params)
    else:
        *seqs, g = _rwkv_project_seq(x3d, shift, *params)
    seqs = [u.reshape(b_, t_, D_MODEL) for u in seqs]
    z, s = _rwkv_scan(*seqs, _rwkv_state_to_lanes(wkv), k_k, k_a, r_k, gn_g, gn_b)
    return z.reshape(b_ * t_, D_MODEL), g, x3d[:, -1], _rwkv_state_from_lanes(s, b_).astype(wkv.dtype)


def _proj_gate_ln_kernel(x_ref, h_ref, gate_ref, w_ref, g_ref, b_ref, o_ref):
    h = (h_ref[...] * gate_ref[...]).astype(MXU_DTYPE)
    y = ALPHA * x_ref[...] + jnp.dot(h, w_ref[...], preferred_element_type=jnp.float32)
    o_ref[...] = _ln_rows(y, g_ref[...], b_ref[...])


def _proj_gate_post_norm(x2d, h2d, gate2d, w_out, g, b):
    m = x2d.shape[0]
    tm = _row_tile(m)
    rows = pl.BlockSpec((tm, D_MODEL), lambda i: (i, 0))
    vec = pl.BlockSpec((1, D_MODEL), lambda i: (0, 0))
    return pl.pallas_call(
        _proj_gate_ln_kernel,
        grid=(m // tm,),
        in_specs=[rows, rows, rows, pl.BlockSpec((D_MODEL, D_MODEL), lambda i: (0, 0)), vec, vec],
        out_specs=rows,
        out_shape=jax.ShapeDtypeStruct((m, D_MODEL), jnp.float32),
        compiler_params=pltpu.CompilerParams(
            dimension_semantics=("parallel",),
            vmem_limit_bytes=V7X_VMEM_LIMIT_BYTES),
        name="proj_gate_post_norm",
    )(x2d, h2d, gate2d, w_out, g.reshape(1, D_MODEL), b.reshape(1, D_MODEL))


GM_ROW_TILE = 256


def _gmlp_kernel(x_ref, win_ref, lng_ref, lnb_ref, mixw_ref, mixb_ref, wout_ref, g_ref, b_ref, *out_refs,
                 chunk_len, emit_v):
    x = x_ref[...]
    h = jax.nn.gelu(jnp.dot(x.astype(MXU_DTYPE), win_ref[...], preferred_element_type=jnp.float32))
    u = h[:, :GM_WIDTH]
    v = _ln_rows(h[:, GM_WIDTH:], lng_ref[...], lnb_ref[...])
    if emit_v:
        out_refs[1][...] = v
    if chunk_len == 1:
        gated = u * (v * mixw_ref[...] + mixb_ref[...])
    else:
        tm = x.shape[0]
        causal = (lax.broadcasted_iota(jnp.int32, (chunk_len, chunk_len), 0)
                  >= lax.broadcasted_iota(jnp.int32, (chunk_len, chunk_len), 1))
        vb = v.astype(MXU_DTYPE)
        cols = []
        for g in range(GM_GROUPS):
            w = jnp.where(causal, mixw_ref[g], 0.0).astype(MXU_DTYPE)
            bias = mixb_ref[:, g:g + 1]
            lanes = slice(g * GM_GROUP_DIM, (g + 1) * GM_GROUP_DIM)
            rows = [jnp.dot(w, vb[c * chunk_len:(c + 1) * chunk_len, lanes],
                            preferred_element_type=jnp.float32) + bias
                    for c in range(tm // chunk_len)]
            cols.append(jnp.concatenate(rows, axis=0))
        gated = u * jnp.concatenate(cols, axis=1)
    y = ALPHA * x + jnp.dot(gated.astype(MXU_DTYPE), wout_ref[...], preferred_element_type=jnp.float32)
    out_refs[0][...] = _ln_rows(y, g_ref[...], b_ref[...])


def _gmlp_block(x2d, seq_len, w_in, ln_g, ln_b, ws, bs, w_out, g, b, emit_v):
    m = x2d.shape[0]
    chunk_len = min(seq_len, CHUNK)
    if chunk_len == 1:
        tm = m
        mixw = jnp.repeat(ws[:, 0, 0], GM_GROUP_DIM)[None, :]
        mixb = jnp.repeat(bs[:, 0], GM_GROUP_DIM)[None, :]
    else:
        tm = GM_ROW_TILE
        mixw = ws[:, :chunk_len, :chunk_len]
        mixb = bs[:, :chunk_len].T
    full = lambda a: pl.BlockSpec(a.shape, lambda i: (0,) * a.ndim)
    rows = lambda n: pl.BlockSpec((tm, n), lambda i: (i, 0))
    consts = [w_in.astype(MXU_DTYPE), ln_g.reshape(1, GM_WIDTH), ln_b.reshape(1, GM_WIDTH), mixw, mixb,
              w_out.astype(MXU_DTYPE), g.reshape(1, D_MODEL), b.reshape(1, D_MODEL)]
    out_specs = [rows(D_MODEL)] + ([rows(GM_WIDTH)] if emit_v else [])
    out_shape = [jax.ShapeDtypeStruct((m, D_MODEL), jnp.float32)] + (
        [jax.ShapeDtypeStruct((m, GM_WIDTH), jnp.float32)] if emit_v else [])
    return pl.pallas_call(
        functools.partial(_gmlp_kernel, chunk_len=chunk_len, emit_v=emit_v),
        grid=(m // tm,),
        in_specs=[rows(D_MODEL)] + [full(a) for a in consts],
        out_specs=out_specs,
        out_shape=out_shape,
        compiler_params=pltpu.CompilerParams(
            dimension_semantics=("parallel",),
            vmem_limit_bytes=V7X_VMEM_LIMIT_BYTES),
        name="gmlp_block",
    )(x2d, *consts)


SSM_ROW_TILE = 256
SSM_BC_DIM = SSM_GROUPS * SSM_STATE
SSM_DT_LANES = 128
SUBLANES = 8


def _ssm_activate(xb, xbc, taps, wz_ref, wdt_ref, cw_ref, cb_ref, dtb_ref, z_ref, xs_ref, bm_ref, cm_ref, dt_ref):
    conv = cb_ref[...] + xbc * cw_ref[SSM_CONV - 1:SSM_CONV, :]
    for j in range(SSM_CONV - 1):
        conv = conv + taps[j] * cw_ref[j:j + 1, :]
    act = conv * jax.nn.sigmoid(conv)
    xs_ref[...] = act[:, :SSM_D_INNER]
    bm_ref[...] = act[:, SSM_D_INNER:SSM_D_INNER + SSM_BC_DIM].astype(bm_ref.dtype)
    cm_ref[...] = act[:, SSM_D_INNER + SSM_BC_DIM:].astype(cm_ref.dtype)
    z_ref[...] = jnp.dot(xb, wz_ref[...], preferred_element_type=jnp.float32)
    dt_ref[...] = jax.nn.softplus(jnp.dot(xb, wdt_ref[...], preferred_element_type=jnp.float32) + dtb_ref[...])


def _ssm_proj_seq_kernel(x_ref, halo_ref, cs_ref, wx_ref, wz_ref, wdt_ref, cw_ref, cb_ref, dtb_ref,
                         z_ref, xs_ref, bm_ref, cm_ref, dt_ref, tail_ref):
    i = pl.program_id(1)
    xb = x_ref[...].astype(MXU_DTYPE)
    xbc = jnp.dot(xb, wx_ref[...], preferred_element_type=jnp.float32)
    tm = xbc.shape[0]
    prev = jnp.dot(halo_ref[...].astype(MXU_DTYPE), wx_ref[...], preferred_element_type=jnp.float32)
    prev = jnp.where(i == 0, cs_ref[0], prev)
    row = lax.broadcasted_iota(jnp.int32, (SUBLANES, 1), 0)
    taps = []
    for j in range(SSM_CONV - 1):
        back = SSM_CONV - 1 - j
        rolled = pltpu.roll(xbc, back, 0)
        top = jnp.where(row < back, pltpu.roll(prev, back, 0), rolled[:SUBLANES])
        taps.append(jnp.concatenate([top, rolled[SUBLANES:]], axis=0))
    _ssm_activate(xb, xbc, taps, wz_ref, wdt_ref, cw_ref, cb_ref, dtb_ref, z_ref, xs_ref, bm_ref, cm_ref, dt_ref)
    tail_ref[0] = xbc[tm - SUBLANES:, :]


def _ssm_proj_step_kernel(x_ref, st_ref, wx_ref, wz_ref, wdt_ref, cw_ref, cb_ref, dtb_ref,
                          z_ref, xs_ref, bm_ref, cm_ref, dt_ref, st_out_ref):
    xb = x_ref[...].astype(MXU_DTYPE)
    xbc = jnp.dot(xb, wx_ref[...], preferred_element_type=jnp.float32)
    taps = [st_ref[j] for j in range(SSM_CONV - 1)]
    _ssm_activate(xb, xbc, taps, wz_ref, wdt_ref, cw_ref, cb_ref, dtb_ref, z_ref, xs_ref, bm_ref, cm_ref, dt_ref)
    for j in range(SSM_CONV - 2):
        st_out_ref[j] = st_ref[j + 1]
    st_out_ref[SSM_CONV - 2] = xbc


def _ssm_project(x3d, conv_state, w_in, conv_w, conv_b, dt_bias):
    b_, t_, _ = x3d.shape
    m = b_ * t_
    w_z, w_x, w_dt = jnp.split(w_in, [SSM_D_INNER, SSM_D_INNER + SSM_CONV_DIM], axis=1)
    cast = lambda w: w.astype(MXU_DTYPE)
    consts = [cast(w_x), cast(w_z), cast(jnp.pad(w_dt, ((0, 0), (0, SSM_DT_LANES - SSM_HEADS)))),
              conv_w, conv_b.reshape(1, SSM_CONV_DIM),
              jnp.pad(dt_bias, (0, SSM_DT_LANES - SSM_HEADS)).reshape(1, SSM_DT_LANES)]
    sds = jax.ShapeDtypeStruct
    outs = [sds((m, SSM_D_INNER), jnp.float32), sds((m, SSM_D_INNER), jnp.float32),
            sds((m, SSM_BC_DIM), MXU_DTYPE), sds((m, SSM_BC_DIM), MXU_DTYPE), sds((m, SSM_DT_LANES), jnp.float32)]
    widths = [SSM_D_INNER, SSM_D_INNER, SSM_BC_DIM, SSM_BC_DIM, SSM_DT_LANES]
    params = dict(vmem_limit_bytes=V7X_VMEM_LIMIT_BYTES)
    x2d = x3d.reshape(m, D_MODEL)
    if t_ == 1:
        full = lambda a: pl.BlockSpec(a.shape, lambda i: (0,) * a.ndim)
        st = jnp.transpose(conv_state, (1, 0, 2))
        res = pl.pallas_call(
            _ssm_proj_step_kernel,
            grid=(1,),
            in_specs=[full(x2d), full(st)] + [full(a) for a in consts],
            out_specs=[pl.BlockSpec((m, w), lambda i: (0, 0)) for w in widths] + [full(st)],
            out_shape=outs + [sds(st.shape, jnp.float32)],
            compiler_params=pltpu.CompilerParams(dimension_semantics=("arbitrary",), **params),
            name="ssm_project_step",
        )(x2d, st, *consts)
        return list(res[:5]) + [jnp.transpose(res[5], (1, 0, 2))]
    tm = SSM_ROW_TILE
    nt = t_ // tm
    full = lambda a: pl.BlockSpec(a.shape, lambda b, i: (0,) * a.ndim)
    rows = lambda w: pl.BlockSpec((tm, w), lambda b, i: (b * nt + i, 0))
    halo = pl.BlockSpec((SUBLANES, D_MODEL), lambda b, i: (jnp.maximum((b * nt + i) * (tm // SUBLANES) - 1, 0), 0))
    cs8 = jnp.pad(conv_state, ((0, 0), (SUBLANES - (SSM_CONV - 1), 0), (0, 0)))
    tail = pl.BlockSpec((1, SUBLANES, SSM_CONV_DIM), lambda b, i: (b, 0, 0))
    res = pl.pallas_call(
        _ssm_proj_seq_kernel,
        grid=(b_, nt),
        in_specs=[rows(D_MODEL), halo, tail] + [full(a) for a in consts],
        out_specs=[rows(w) for w in widths] + [tail],
        out_shape=outs + [sds((b_, SUBLANES, SSM_CONV_DIM), jnp.float32)],
        compiler_params=pltpu.CompilerParams(dimension_semantics=("parallel", "arbitrary"), **params),
        name="ssm_project_seq",
    )(x2d, x2d, cs8, *consts)
    return list(res[:5]) + [res[5][:, SUBLANES - (SSM_CONV - 1):, :]]


def _ssm_gate_norm(y, xs, z, dskip, normg):
    yg = (y + xs * dskip) * (z * jax.nn.sigmoid(z))
    gw = SSM_D_INNER // SSM_GROUPS
    outs = []
    for g in range(SSM_GROUPS):
        part = yg[:, g * gw:(g + 1) * gw]
        ms = jnp.mean(part * part, axis=-1, keepdims=True)
        outs.append(part * lax.rsqrt(ms + LN_EPS))
    return jnp.concatenate(outs, axis=1) * normg


def _ssm_chunk_kernel(xs_ref, bm_ref, cm_ref, dt_ref, z_ref, aneg_ref, dskip_ref, normg_ref,
                      yg_ref, h_out_ref, h_ref, yT_ref, xe_ref):
    c = pl.program_id(1)
    l = xs_ref.shape[0]
    hd = SSM_HEAD_DIM

    @pl.when(c == 0)
    def _():
        h_ref[...] = jnp.zeros_like(h_ref)

    dot = lambda u, w: jnp.dot(u, w, preferred_element_type=jnp.float32)
    dt = dt_ref[...]
    a = dt * aneg_ref[...]
    r_i = lax.broadcasted_iota(jnp.int32, (l, l), 0)
    c_i = lax.broadcasted_iota(jnp.int32, (l, l), 1)
    tril = jnp.where(r_i >= c_i, 1.0, 0.0)
    hi = lax.Precision.HIGHEST
    acum = jnp.dot(tril, a, precision=hi, preferred_element_type=jnp.float32)
    acum_t = jnp.dot(a.T, tril.T, precision=hi, preferred_element_type=jnp.float32)
    dt_t = dt.T
    to_end_t = jnp.exp(acum_t[:, l - 1:l] - acum_t)
    from_start_t = jnp.exp(acum_t)
    chunk_decay = jnp.exp(acum[l - 1:l, :])
    upper = r_i <= c_i
    xs = xs_ref[...]
    for g in range(SSM_GROUPS):
        bm = bm_ref[:, g * SSM_STATE:(g + 1) * SSM_STATE]
        cm_t = cm_ref[:, g * SSM_STATE:(g + 1) * SSM_STATE].astype(jnp.float32).T.astype(MXU_DTYPE)
        cb_t = dot(bm, cm_t)
        h_in = h_ref[g * SSM_HPG:(g + 1) * SSM_HPG].reshape(SSM_HPG * hd, SSM_STATE)
        y_off = dot(h_in.astype(MXU_DTYPE), cm_t)
        for e in range(SSM_HPG):
            h = g * SSM_HPG + e
            if h % 2 == 0:
                xs_pair_t = xs[:, h * hd:(h + 2) * hd].T
            xdt_t = xs_pair_t[(h % 2) * hd:(h % 2 + 1) * hd] * dt_t[h:h + 1, :]
            seg = jnp.exp(jnp.where(upper, acum_t[h:h + 1, :] - acum[:, h:h + 1], -jnp.inf))
            y_diag = dot(xdt_t.astype(MXU_DTYPE), (cb_t * seg).astype(MXU_DTYPE))
            yT_ref[h * hd:(h + 1) * hd, :] = y_diag + y_off[e * hd:(e + 1) * hd] * from_start_t[h:h + 1, :]
            xe_ref[e * hd:(e + 1) * hd, :] = (xdt_t * to_end_t[h:h + 1, :]).astype(xe_ref.dtype)
        states = dot(xe_ref[...], bm)
        for e in range(SSM_HPG):
            h = g * SSM_HPG + e
            h_ref[h] = h_ref[h] * chunk_decay[:, h:h + 1] + states[e * hd:(e + 1) * hd]
    y = jnp.concatenate([yT_ref[i * l:(i + 1) * l, :].T for i in range(SSM_D_INNER // l)], axis=1)
    yg_ref[...] = _ssm_gate_norm(y, xs, z_ref[...], dskip_ref[...], normg_ref[...]).astype(yg_ref.dtype)

    @pl.when(c == pl.num_programs(1) - 1)
    def _():
        h_out_ref[0] = h_ref[...]


def _ssm_head_lanes(p):
    return jnp.pad(p.astype(jnp.float32), (0, SSM_DT_LANES - SSM_HEADS)).reshape(1, SSM_DT_LANES)


def _ssm_chunk_scan(b_, t_, xs, bm, cm, dt, z, a_log, d_skip, norm_g):
    l = SSM_CHUNK
    nc = t_ // l
    rows = lambda w: pl.BlockSpec((l, w), lambda b, c: (b * nc + c, 0))
    vec = lambda w: pl.BlockSpec((1, w), lambda b, c: (0, 0))
    aneg = _ssm_head_lanes(-jnp.exp(a_log.astype(jnp.float32)))
    dskip = jnp.repeat(d_skip, SSM_HEAD_DIM).reshape(1, SSM_D_INNER)
    yg, h_new = pl.pallas_call(
        _ssm_chunk_kernel,
        grid=(b_, nc),
        in_specs=[rows(SSM_D_INNER), rows(SSM_BC_DIM), rows(SSM_BC_DIM), rows(SSM_DT_LANES), rows(SSM_D_INNER),
                  vec(SSM_DT_LANES), vec(SSM_D_INNER), vec(SSM_D_INNER)],
        out_specs=[rows(SSM_D_INNER),
                   pl.BlockSpec((1, SSM_HEADS, SSM_HEAD_DIM, SSM_STATE), lambda b, c: (b, 0, 0, 0))],
        out_shape=[jax.ShapeDtypeStruct((b_ * t_, SSM_D_INNER), MXU_DTYPE),
                   jax.ShapeDtypeStruct((b_, SSM_HEADS, SSM_HEAD_DIM, SSM_STATE), jnp.float32)],
        scratch_shapes=[pltpu.VMEM((SSM_HEADS, SSM_HEAD_DIM, SSM_STATE), jnp.float32),
                        pltpu.VMEM((SSM_D_INNER, l), jnp.float32),
                        pltpu.VMEM((SSM_HPG * SSM_HEAD_DIM, l), MXU_DTYPE)],
        compiler_params=pltpu.CompilerParams(
            dimension_semantics=("parallel", "arbitrary"),
            vmem_limit_bytes=V7X_VMEM_LIMIT_BYTES),
        name="ssm_chunk_scan",
    )(xs, bm, cm, dt, z, aneg, dskip, norm_g.reshape(1, SSM_D_INNER))
    return yg, h_new


def _ssm_step_kernel(h0_ref, xs_ref, dt_ref, an_ref, bm_ref, cm_ref, y_ref, h_ref):
    h0 = h0_ref[0]
    dt = dt_ref[0]
    decay = jnp.exp(dt * an_ref[...])
    xdt = xs_ref[0] * dt
    bm = bm_ref[0].astype(jnp.float32)
    cm = cm_ref[0].astype(jnp.float32)
    h_ref[0] = h0 * decay + xdt * bm
    cb = jnp.sum(cm * bm, axis=-1, keepdims=True)
    y_ref[0] = cb * xdt + jnp.sum(cm * h0, axis=-1, keepdims=True) * decay


def _ssm_step(state, xs, bm, cm, dt, a_log):
    b_ = state.shape[0]
    per_head = lambda u: jnp.repeat(u.reshape(b_, SSM_GROUPS, 1, SSM_STATE), SSM_HPG, axis=1)
    xs4 = xs.reshape(b_, SSM_HEADS, SSM_HEAD_DIM, 1)
    dt4 = dt[:, :SSM_HEADS].reshape(b_, SSM_HEADS, 1, 1)
    an = (-jnp.exp(a_log.astype(jnp.float32))).reshape(SSM_HEADS, 1, 1)
    blk = lambda a: pl.BlockSpec((1,) + a.shape[1:], lambda b: (b, 0, 0, 0))
    args = [state.astype(jnp.float32), xs4, dt4, an, per_head(bm), per_head(cm)]
    y4, h_new = pl.pallas_call(
        _ssm_step_kernel,
        grid=(b_,),
        in_specs=[blk(args[0]), blk(xs4), blk(dt4), pl.BlockSpec(an.shape, lambda b: (0, 0, 0)),
                  blk(args[4]), blk(args[5])],
        out_specs=[blk(xs4), blk(args[0])],
        out_shape=[jax.ShapeDtypeStruct(xs4.shape, jnp.float32), jax.ShapeDtypeStruct(state.shape, jnp.float32)],
        compiler_params=pltpu.CompilerParams(
            dimension_semantics=("parallel",),
            vmem_limit_bytes=V7X_VMEM_LIMIT_BYTES),
        name="ssm_step",
    )(*args)
    return y4.reshape(b_, SSM_D_INNER), h_new


def _ssm_gate_norm_kernel(y_ref, xs_ref, z_ref, dskip_ref, normg_ref, o_ref):
    o_ref[...] = _ssm_gate_norm(y_ref[...], xs_ref[...], z_ref[...], dskip_ref[...], normg_ref[...]).astype(o_ref.dtype)


def _ssm_gate_norm_rows(y, xs, z, d_skip, norm_g):
    full = lambda a: pl.BlockSpec(a.shape, lambda i: (0,) * a.ndim)
    args = [y, xs, z, jnp.repeat(d_skip, SSM_HEAD_DIM).reshape(1, SSM_D_INNER), norm_g.reshape(1, SSM_D_INNER)]
    return pl.pallas_call(
        _ssm_gate_norm_kernel,
        grid=(1,),
        in_specs=[full(a) for a in args],
        out_specs=full(y),
        out_shape=jax.ShapeDtypeStruct(y.shape, MXU_DTYPE),
        name="ssm_gate_norm",
    )(*args)


def _mamba2_mixer_pallas(x3d, conv_state, ssm_state, w_in, conv_w, conv_b, dt_bias, a_log, d_skip, norm_g):
    b_, t_, _ = x3d.shape
    z, xs, bm, cm, dt, conv_new = _ssm_project(x3d, conv_state, w_in, conv_w, conv_b, dt_bias)
    if t_ == 1:
        y, h_new = _ssm_step(ssm_state, xs, bm, cm, dt, a_log)
        yg = _ssm_gate_norm_rows(y, xs, z, d_skip, norm_g)
    else:
        yg, h_new = _ssm_chunk_scan(b_, t_, xs, bm, cm, dt, z, a_log, d_skip, norm_g)
    return yg, conv_new, h_new.astype(ssm_state.dtype)


PAGES_PER_STEP = 8


def _sortable_key(score):
    bits = pltpu.bitcast(score, jnp.int32)
    return jnp.where(bits >= 0, bits, bits ^ jnp.int32(0x7FFFFFFF))


def _decode_score_kernel(pt_ref, iq_ref, iw_ref, ikn_ref, *rest):
    idx_refs, (key_ref, knew_ref) = rest[:-2], rest[-2:]
    iq, iw = iq_ref[0], iw_ref[0]
    weigh = lambda sc: jnp.sum(iw * jnp.maximum(sc, 0.0), axis=0, keepdims=True)
    ik_t = jnp.concatenate([r[0] for r in idx_refs], axis=1).astype(MXU_DTYPE)
    key_ref[0] = _sortable_key(weigh(jnp.dot(iq, ik_t, preferred_element_type=jnp.float32)))

    @pl.when(pl.program_id(1) == 0)
    def _():
        sc_new = jnp.sum(iq.astype(jnp.float32) * ikn_ref[0].astype(jnp.float32), axis=1, keepdims=True)
        knew_ref[0] = jnp.broadcast_to(_sortable_key(weigh(sc_new)), knew_ref.shape[1:])


def _decode_select_kernel(keys_ref, knew_ref, thr_ref, last_ref, *, topk, col_bits):
    keys = keys_ref[...]
    key_new = knew_ref[:, 0:1]
    past = keys.shape[1]
    col = lax.broadcasted_iota(jnp.int32, keys.shape, 1)

    def count(pred_past, pred_new):
        hit = jnp.where(pred_past(keys, col), 1.0, 0.0)
        cnt = _tree_sum([hit[:, l * 128:(l + 1) * 128] for l in range(past // 128)])
        return jnp.sum(cnt, axis=1, keepdims=True) + jnp.where(pred_new(key_new), 1.0, 0.0)

    def at_least(cand):
        return count(lambda k, c: k >= cand, lambda k: k >= cand) >= float(topk)

    def two_bits(i, thr):
        hi = lax.shift_left(jnp.int32(1), jnp.int32(31) - 2 * i)
        lo = lax.shift_left(jnp.int32(1), jnp.int32(30) - 2 * i)
        c1, c2, c3 = thr ^ lo, thr ^ hi, thr ^ hi ^ lo
        return jnp.where(at_least(c3), c3, jnp.where(at_least(c2), c2, jnp.where(at_least(c1), c1, thr)))

    thr = lax.fori_loop(0, 16, two_bits, jnp.full(key_new.shape, INT32_MIN, jnp.int32))
    need = float(topk) - count(lambda k, c: k > thr, lambda k: k > thr)

    def col_body(i, last):
        cand = last | lax.shift_left(jnp.int32(1), jnp.int32(col_bits - 1) - i)
        ties = count(lambda k, c: (k == thr) & (c < cand), lambda k: (k == thr) & (jnp.int32(past) < cand))
        return jnp.where(ties < need, cand, last)

    n_tied = count(lambda k, c: k == thr, lambda k: k == thr)
    excess = jnp.max(jnp.where(n_tied > need, 1.0, 0.0), axis=0, keepdims=True)
    last_tie = lax.cond(excess[0, 0] > 0.0,
                        lambda: lax.fori_loop(0, col_bits, col_body, jnp.zeros(key_new.shape, jnp.int32)),
                        lambda: jnp.full(key_new.shape, 2 ** col_bits - 1, jnp.int32))
    thr_ref[...] = jnp.broadcast_to(thr, thr_ref.shape)
    last_ref[...] = jnp.broadcast_to(last_tie, last_ref.shape)


def _decode_attend_kernel(pt_ref, q_ref, kn_ref, vn_ref, key_ref, knew_ref, thr_ref, last_ref, *rest,
                          n_steps, pages):
    k_refs, v_refs = rest[:pages], rest[pages:2 * pages]
    o_ref, m_ref, l_ref, acc_ref = rest[2 * pages:]
    s = pl.program_id(1)
    nk = key_ref.shape[2]
    nt = (((1,), (1,)), ((), ()))
    thr, last_tie = thr_ref[0, :, 0:1], last_ref[0, :, 0:1]
    keep_mask = lambda key, col: (key > thr) | ((key == thr) & (col <= last_tie))
    gsz = ATT_HEADS // ATT_KV_HEADS
    q = q_ref[0]
    q_wide = jnp.concatenate([q] * ATT_KV_HEADS, axis=1)
    head_i = lax.broadcasted_iota(jnp.int32, q_wide.shape, 0)
    col_i = lax.broadcasted_iota(jnp.int32, q_wide.shape, 1)
    own_group = (col_i // ATT_HEAD_DIM) == (head_i // gsz)
    q_blk = jnp.where(own_group, q_wide, jnp.zeros_like(q_wide))

    def online_update(logits, weighted_values):
        m_old = m_ref[...]
        m_new = jnp.maximum(m_old, jnp.max(logits, axis=1, keepdims=True))
        p = jnp.exp(logits - m_new)
        alpha = jnp.exp(m_old - m_new)
        l_ref[...] = alpha * l_ref[...] + jnp.sum(p, axis=1, keepdims=True)
        acc_ref[...] = alpha * acc_ref[...] + weighted_values(p.astype(MXU_DTYPE))
        m_ref[...] = m_new

    @pl.when(s == 0)
    def _init():
        m_ref[...] = jnp.full(m_ref.shape, MASK_NEG, jnp.float32)
        l_ref[...] = jnp.zeros(l_ref.shape, jnp.float32)
        acc_ref[...] = jnp.zeros(acc_ref.shape, jnp.float32)

    col = lax.broadcasted_iota(jnp.int32, (1, nk), 1) + s * nk
    bias = jnp.where(keep_mask(key_ref[0], col), 0.0, MASK_NEG)
    k_t = jnp.concatenate([r[0] for r in k_refs], axis=1).astype(MXU_DTYPE)
    v_t = jnp.concatenate([r[0] for r in v_refs], axis=1).astype(MXU_DTYPE)
    online_update(jnp.dot(q_blk, k_t, preferred_element_type=jnp.float32) + bias,
                  lambda p: lax.dot_general(p, v_t, nt, preferred_element_type=jnp.float32))

    @pl.when(s == n_steps - 1)
    def _finish():
        keep_new = keep_mask(knew_ref[0, :, 0:1], jnp.int32(n_steps * nk))
        logit = jnp.sum(q_blk.astype(jnp.float32) * kn_ref[0].astype(jnp.float32), axis=1, keepdims=True)
        v_row = vn_ref[0].astype(jnp.float32)
        online_update(logit + jnp.where(keep_new, 0.0, MASK_NEG), lambda p: p.astype(jnp.float32) * v_row)
        out = jnp.where(own_group, acc_ref[...] / l_ref[...], 0.0)
        o = out[:, 0:ATT_HEAD_DIM]
        for g in range(1, ATT_KV_HEADS):
            o = o + out[:, g * ATT_HEAD_DIM:(g + 1) * ATT_HEAD_DIM]
        o_ref[0] = o.astype(o_ref.dtype)


def _dsa_decode_split(q, iq, iw, ik_new, k_new, v_new, cache_k, cache_v, cache_idx_k, page_table):
    b_, n_pages = page_table.shape
    n_pool, page = cache_k.shape[0], cache_k.shape[1]
    pages = PAGES_PER_STEP
    n_steps = n_pages // pages
    nk = pages * page
    past = n_pages * page
    ck = jnp.transpose(cache_k, (0, 2, 3, 1)).reshape(n_pool, ATT_KV_DIM, page)
    cv = jnp.transpose(cache_v, (0, 2, 3, 1)).reshape(n_pool, ATT_KV_DIM, page)
    cik = jnp.swapaxes(cache_idx_k, 1, 2)
    per_seq = lambda a: pl.BlockSpec((1,) + a.shape[1:], lambda b, s, pt: (b,) + (0,) * (a.ndim - 1))
    paged = lambda width, j: pl.BlockSpec((1, width, page), lambda b, s, pt: (pt[b, s * pages + j], 0, 0))
    key_blk = pl.BlockSpec((1, 1, nk), lambda b, s, pt: (b, 0, s))
    lane_blk = pl.BlockSpec((1, 1, 128), lambda b, s, pt: (b, 0, 0))
    params = pltpu.CompilerParams(dimension_semantics=("parallel", "arbitrary"),
                                  vmem_limit_bytes=V7X_VMEM_LIMIT_BYTES)

    score_in = [iq.reshape(b_, IDX_HEADS, IDX_DIM), iw[:, :IDX_HEADS].reshape(b_, IDX_HEADS, 1),
                ik_new.astype(MXU_DTYPE).reshape(b_, 1, IDX_DIM)]
    keys, key_new = pl.pallas_call(
        _decode_score_kernel,
        grid_spec=pltpu.PrefetchScalarGridSpec(
            num_scalar_prefetch=1, grid=(b_, n_steps),
            in_specs=[per_seq(a) for a in score_in] + [paged(IDX_DIM, j) for j in range(pages)],
            out_specs=[key_blk, lane_blk]),
        out_shape=[jax.ShapeDtypeStruct((b_, 1, past), jnp.int32), jax.ShapeDtypeStruct((b_, 1, 128), jnp.int32)],
        compiler_params=params,
        name="dsa_decode_score",
    )(page_table, *score_in, *([cik] * pages))

    whole = lambda shape: pl.BlockSpec(shape, lambda i: (0,) * len(shape))
    thr, last_tie = pl.pallas_call(
        functools.partial(_decode_select_kernel, topk=min(TOPK_MAX, (past + 1) // 4),
                          col_bits=max(1, past.bit_length())),
        grid=(1,),
        in_specs=[whole((b_, past)), whole((b_, 128))],
        out_specs=[whole((b_, 128)), whole((b_, 128))],
        out_shape=[jax.ShapeDtypeStruct((b_, 128), jnp.int32)] * 2,
        compiler_params=pltpu.CompilerParams(vmem_limit_bytes=V7X_VMEM_LIMIT_BYTES),
        name="dsa_decode_select",
    )(keys.reshape(b_, past), key_new.reshape(b_, 128))

    attend_in = [q.reshape(b_, ATT_HEADS, ATT_HEAD_DIM), k_new.astype(MXU_DTYPE).reshape(b_, 1, ATT_KV_DIM),
                 v_new.astype(MXU_DTYPE).reshape(b_, 1, ATT_KV_DIM)]
    o = pl.pallas_call(
        functools.partial(_decode_attend_kernel, n_steps=n_steps, pages=pages),
        grid_spec=pltpu.PrefetchScalarGridSpec(
            num_scalar_prefetch=1, grid=(b_, n_steps),
            in_specs=[per_seq(a) for a in attend_in] + [key_blk, lane_blk, lane_blk, lane_blk]
            + [paged(ATT_KV_DIM, j) for j in range(pages)] * 2,
            out_specs=pl.BlockSpec((1, ATT_HEADS, ATT_HEAD_DIM), lambda b, s, pt: (b, 0, 0)),
            scratch_shapes=[pltpu.VMEM((ATT_HEADS, 1), jnp.float32), pltpu.VMEM((ATT_HEADS, 1), jnp.float32),
                            pltpu.VMEM((ATT_HEADS, ATT_KV_DIM), jnp.float32)]),
        out_shape=jax.ShapeDtypeStruct((b_, ATT_HEADS, ATT_HEAD_DIM), MXU_DTYPE),
        compiler_params=params,
        name="dsa_decode_attend",
    )(page_table, *attend_in, keys, key_new, thr.reshape(b_, 1, 128), last_tie.reshape(b_, 1, 128),
      *([ck] * pages), *([cv] * pages))
    return o.reshape(b_, ATT_Q_DIM)


def _dsa_sample_pallas(x3d, cache_k, cache_v, cache_idx_k, page_table, w_in, kn_g, kn_b):
    b_, t_, _ = x3d.shape
    past = page_table.shape[1] * cache_k.shape[1]
    pos = jnp.full((b_,), past, jnp.int32)
    q, iq, v, _, iw, kT, _, ikT, _ = _dsa_project(x3d.reshape(1, b_, D_MODEL), pos, w_in, kn_g, kn_b)
    k = _untranspose_groups(kT)[0]
    ik = _untranspose_groups(ikT)[0]
    o = _dsa_decode_split(q, iq, iw, ik, k, v, cache_k, cache_v, cache_idx_k, page_table)
    kv4 = lambda u: u.reshape(b_, t_, ATT_KV_HEADS, ATT_HEAD_DIM)
    return o, kv4(k), kv4(v), ik.reshape(b_, t_, IDX_DIM)


def kernel(x_prompt, x_sample, state_ssm_conv, state_ssm, cache_k, cache_v, cache_idx_k, state_rwkv_shift, state_rwkv_wkv, page_table, p_prompt, p_sample, ln_g, ln_b, ffn_w_up, ffn_w_down, ple_w_p, ple_w_g, ple_b_g, gm_w_in, gm_ln_g, gm_ln_b, gm_ws, gm_bs, gm_w_out, ssm_w_in, ssm_conv_w, ssm_conv_b, ssm_dt_bias, ssm_a_log, ssm_d, ssm_norm_g, ssm_w_out, att_w_in, att_kn_g, att_kn_b, att_w_out, rw_mu, rw_w_r, rw_w_k, rw_w_v, rw_w_o, rw_w0, rw_w1, rw_w2, rw_a0, rw_a1, rw_a2, rw_g1, rw_g2, rw_k_k, rw_k_a, rw_r_k, rw_gn_g, rw_gn_b):
    bp, tp, _ = x_prompt.shape
    bs_, ts, _ = x_sample.shape
    bf = lambda w: w.astype(jnp.bfloat16)
    w_up_bf, w_down_bf = bf(ffn_w_up), bf(ffn_w_down)
    ple_wp_bf, ple_wg_bf = bf(ple_w_p), bf(ple_w_g)
    pp3 = p_prompt.reshape(DEPTH, bp * tp, PLE_DIM)
    ps3 = p_sample.reshape(DEPTH, bs_ * ts, PLE_DIM)

    yp = x_prompt.reshape(bp * tp, D_MODEL)
    ys = x_sample.reshape(bs_ * ts, D_MODEL)
    r3p = lambda t: t.reshape(bp, tp, -1)
    r3s = lambda t: t.reshape(bs_, ts, -1)
    f2 = lambda t: t.reshape(-1, t.shape[-1])

    for i in range(DEPTH):
        yp = _ffn_sub(yp, w_up_bf, w_down_bf, i, 0, ln_g[i, 0], ln_b[i, 0])
        ys = _ffn_sub(ys, w_up_bf, w_down_bf, i, 0, ln_g[i, 0], ln_b[i, 0])
        m = i % N_MIXERS
        if m == 0:
            gm_args = (gm_w_in, gm_ln_g, gm_ln_b, gm_ws, gm_bs, gm_w_out, ln_g[i, 1], ln_b[i, 1])
            yp, = _gmlp_block(yp, tp, *gm_args, False)
            ys, gm_v_s = _gmlp_block(ys, ts, *gm_args, True)
            gm_v_s = r3s(gm_v_s)
        elif m == 1:
            ssm_args = (ssm_w_in, ssm_conv_w, ssm_conv_b, ssm_dt_bias, ssm_a_log, ssm_d, ssm_norm_g)
            hp, conv_p, ssm_p = _mamba2_mixer_pallas(
                r3p(yp), jnp.zeros((bp, SSM_CONV - 1, SSM_CONV_DIM), yp.dtype),
                jnp.zeros((bp, SSM_HEADS, SSM_HEAD_DIM, SSM_STATE), yp.dtype), *ssm_args)
            hs, conv_s, ssm_s = _mamba2_mixer_pallas(r3s(ys), state_ssm_conv, state_ssm, *ssm_args)
            w_out = bf(ssm_w_out)
        elif m == 2:
            hp, k_p, v_p, ik_p = _dsa_prompt_pallas(r3p(yp), att_w_in, att_kn_g, att_kn_b)
            hs, k_s, v_s, ik_s = _dsa_sample_pallas(r3s(ys), cache_k, cache_v, cache_idx_k, page_table,
                                                    att_w_in, att_kn_g, att_kn_b)
            w_out = bf(att_w_out)
        else:
            rw_args = (rw_mu, rw_w_r, rw_w_k, rw_w_v, rw_w0, rw_w1, rw_w2, rw_a0, rw_a1, rw_a2,
                       rw_g1, rw_g2, rw_k_k, rw_k_a, rw_r_k, rw_gn_g, rw_gn_b)
            hp, gate_p, sh_p, wkv_p = _rwkv7_mixer_pallas(
                r3p(yp), jnp.zeros((bp, D_MODEL), yp.dtype),
                jnp.zeros((bp, RW_HEADS, RW_HEAD, RW_HEAD), yp.dtype), *rw_args)
            hs, gate_s, sh_s, wkv_s = _rwkv7_mixer_pallas(r3s(ys), state_rwkv_shift, state_rwkv_wkv, *rw_args)
            w_out = bf(rw_w_o)
        if m == 3:
            yp = _proj_gate_post_norm(yp, hp, gate_p, w_out, ln_g[i, 1], ln_b[i, 1])
            ys = _proj_gate_post_norm(ys, hs, gate_s, w_out, ln_g[i, 1], ln_b[i, 1])
        elif m != 0:
            yp = _proj_post_norm(yp, f2(hp), w_out, ln_g[i, 1], ln_b[i, 1])
            ys = _proj_post_norm(ys, f2(hs), w_out, ln_g[i, 1], ln_b[i, 1])
        yp = _ffn_sub(yp, w_up_bf, w_down_bf, i, 1, ln_g[i, 2], ln_b[i, 2], (pp3, ple_wp_bf, ple_wg_bf, ple_b_g))
        ys = _ffn_sub(ys, w_up_bf, w_down_bf, i, 1, ln_g[i, 2], ln_b[i, 2], (ps3, ple_wp_bf, ple_wg_bf, ple_b_g))

    return (r3p(yp), r3s(ys), gm_v_s, conv_p, ssm_p, conv_s, ssm_s, k_p, v_p, ik_p, k_s, v_s, ik_s,
            sh_p, wkv_p, sh_s, wkv_s)
```

```python
import functools

import jax
import jax.numpy as jnp
from jax import lax
from jax.experimental import pallas as pl
from jax.experimental.pallas import tpu as pltpu

D_MODEL = 1024
DEPTH = 4
N_MIXERS = 4
PLE_DIM = 256
D_FF = 2816
ALPHA = (2 * DEPTH) ** 0.25
LN_EPS = 1e-5

CHUNK = 128
GM_WIDTH = 2 * D_MODEL
GM_GROUPS = 8
GM_GROUP_DIM = GM_WIDTH // GM_GROUPS

SSM_D_INNER = 2 * D_MODEL
SSM_HEAD_DIM = 64
SSM_HEADS = SSM_D_INNER // SSM_HEAD_DIM
SSM_GROUPS = 4
SSM_HPG = SSM_HEADS // SSM_GROUPS
SSM_STATE = 128
SSM_CONV = 4
SSM_CONV_DIM = SSM_D_INNER + 2 * SSM_GROUPS * SSM_STATE
SSM_CHUNK = 128

ATT_HEADS = 16
ATT_KV_HEADS = 4
ATT_HEAD_DIM = D_MODEL // ATT_HEADS
ROPE_DIM = ATT_HEAD_DIM // 4
ROPE_THETA = 500000.0
IDX_HEADS = 8
IDX_DIM = 64
IDX_ROPE_DIM = IDX_DIM // 4
TOPK_MAX = 256
Q_BLOCK = 128
ATT_Q_DIM = ATT_HEADS * ATT_HEAD_DIM
ATT_KV_DIM = ATT_KV_HEADS * ATT_HEAD_DIM
ATT_IN_SPLITS = (ATT_Q_DIM, ATT_Q_DIM + ATT_KV_DIM, ATT_Q_DIM + 2 * ATT_KV_DIM,
                 ATT_Q_DIM + 2 * ATT_KV_DIM + IDX_HEADS * IDX_DIM,
                 ATT_Q_DIM + 2 * ATT_KV_DIM + IDX_HEADS * IDX_DIM + IDX_DIM)

RW_HEAD = 64
RW_HEADS = D_MODEL // RW_HEAD
RW_GN_EPS = 64e-5

V7X_VMEM_LIMIT_BYTES = 52 * 1024 * 1024
FF_TILE = D_FF // 2
ROW_TILE = 512


def _row_tile(m):
    return ROW_TILE if m % ROW_TILE == 0 else m


def _ln_rows(y, g, b):
    mu = jnp.mean(y, axis=-1, keepdims=True)
    yc = y - mu
    var = jnp.mean(yc * yc, axis=-1, keepdims=True)
    return yc * lax.rsqrt(var + LN_EPS) * g + b


FFN_ROW_TILE = 1024
FFN_ROW_PARTS = 2


def _ffn_kernel(x_ref, wu_ref, wd_ref, g_ref, b_ref, *rest, parts, with_ple):
    o_ref = rest[-1]
    rows = x_ref.shape[0] // parts
    for p in range(parts):
        x = x_ref[p * rows:(p + 1) * rows, :]
        xb = x.astype(MXU_DTYPE)
        acc = None
        for f in range(D_FF // FF_TILE):
            cols = slice(f * FF_TILE, (f + 1) * FF_TILE)
            gate = jnp.dot(xb, wu_ref[:, cols], preferred_element_type=jnp.float32)
            lin = jnp.dot(xb, wu_ref[:, D_FF + f * FF_TILE:D_FF + (f + 1) * FF_TILE],
                          preferred_element_type=jnp.float32)
            h = (gate * jax.nn.sigmoid(gate) * lin).astype(MXU_DTYPE)
            part = jnp.dot(h, wd_ref[cols, :], preferred_element_type=jnp.float32)
            acc = part if acc is None else acc + part
        y = _ln_rows(ALPHA * x + 0.5 * acc, g_ref[...], b_ref[...])
        if with_ple:
            p_ref, wp_ref, wg_ref, bg_ref = rest[:4]
            gate = jax.nn.sigmoid(
                jnp.dot(y.astype(MXU_DTYPE), wg_ref[...], preferred_element_type=jnp.float32) + bg_ref[...])
            emb = jnp.dot(p_ref[p * rows:(p + 1) * rows, :].astype(MXU_DTYPE), wp_ref[...],
                          preferred_element_type=jnp.float32)
            y = y + gate * emb
        o_ref[p * rows:(p + 1) * rows, :] = y


def _ffn_sub(x2d, w_up, w_down, layer, half, g, b, ple=None):
    m = x2d.shape[0]
    tm = FFN_ROW_TILE if m % FFN_ROW_TILE == 0 else m
    parts = FFN_ROW_PARTS if tm == FFN_ROW_TILE else 1
    resident = dict(pipeline_mode=pl.Buffered(1))
    vec = pl.BlockSpec((1, D_MODEL), lambda i: (0, 0))
    in_specs = [
        pl.BlockSpec((tm, D_MODEL), lambda i: (i, 0)),
        pl.BlockSpec((None, None, D_MODEL, 2 * D_FF), lambda i: (layer, half, 0, 0), **resident),
        pl.BlockSpec((None, None, D_FF, D_MODEL), lambda i: (layer, half, 0, 0), **resident),
        vec, vec]
    args = [x2d, w_up, w_down, g.reshape(1, D_MODEL), b.reshape(1, D_MODEL)]
    if ple is not None:
        p3d, w_p, w_g, b_g = ple
        in_specs += [pl.BlockSpec((None, tm, PLE_DIM), lambda i: (layer, i, 0)),
                     pl.BlockSpec((None, PLE_DIM, D_MODEL), lambda i: (layer, 0, 0), **resident),
                     pl.BlockSpec((None, D_MODEL, D_MODEL), lambda i: (layer, 0, 0), **resident),
                     pl.BlockSpec((None, 1, D_MODEL), lambda i: (layer, 0, 0))]
        args += [p3d, w_p, w_g, b_g.reshape(DEPTH, 1, D_MODEL)]
    return pl.pallas_call(
        functools.partial(_ffn_kernel, parts=parts, with_ple=ple is not None),
        grid=(m // tm,),
        in_specs=in_specs,
        out_specs=pl.BlockSpec((tm, D_MODEL), lambda i: (i, 0)),
        out_shape=jax.ShapeDtypeStruct((m, D_MODEL), jnp.float32),
        compiler_params=pltpu.CompilerParams(
            dimension_semantics=("parallel",),
            vmem_limit_bytes=V7X_VMEM_LIMIT_BYTES),
        name="ffn_ple" if ple is not None else "ffn_sub",
    )(*args)


def _proj_ln_kernel(x_ref, h_ref, w_ref, g_ref, b_ref, o_ref):
    y = ALPHA * x_ref[...] + jnp.dot(h_ref[...].astype(jnp.bfloat16), w_ref[...],
                                     preferred_element_type=jnp.float32)
    o_ref[...] = _ln_rows(y, g_ref[...], b_ref[...])


def _proj_post_norm(x2d, h2d, w_out, g, b):
    m = x2d.shape[0]
    k = h2d.shape[1]
    tm = _row_tile(m)
    return pl.pallas_call(
        _proj_ln_kernel,
        grid=(m // tm,),
        in_specs=[
            pl.BlockSpec((tm, D_MODEL), lambda i: (i, 0)),
            pl.BlockSpec((tm, k), lambda i: (i, 0)),
            pl.BlockSpec((k, D_MODEL), lambda i: (0, 0)),
            pl.BlockSpec((1, D_MODEL), lambda i: (0, 0)),
            pl.BlockSpec((1, D_MODEL), lambda i: (0, 0)),
        ],
        out_specs=pl.BlockSpec((tm, D_MODEL), lambda i: (i, 0)),
        out_shape=jax.ShapeDtypeStruct((m, D_MODEL), jnp.float32),
        compiler_params=pltpu.CompilerParams(
            dimension_semantics=("parallel",),
            vmem_limit_bytes=V7X_VMEM_LIMIT_BYTES),
        name="proj_post_norm",
    )(x2d, h2d, w_out, g.reshape(1, D_MODEL), b.reshape(1, D_MODEL))


MXU_DTYPE = jnp.bfloat16
KEY_GROUP = 512
INT32_MIN = -2 ** 31
MASK_NEG = -1e30


def _rope_lane_tables(pos, rot_dim, head_dim):
    half = rot_dim // 2
    inv = ROPE_THETA ** (-jnp.arange(half, dtype=jnp.float32) / half)
    ang = pos.astype(jnp.float32)[:, None] * inv[None, :]
    cos, sin = jnp.cos(ang), jnp.sin(ang)
    n = pos.shape[0]
    rest = head_dim - rot_dim
    c = jnp.concatenate([cos, cos, jnp.ones((n, rest), jnp.float32)], axis=1)
    s1 = jnp.concatenate([-sin, jnp.zeros((n, half + rest), jnp.float32)], axis=1)
    s2 = jnp.concatenate([jnp.zeros((n, half), jnp.float32), sin, jnp.zeros((n, rest), jnp.float32)], axis=1)
    reps = 128 // head_dim
    tile = lambda t: jnp.tile(t, (1, reps))
    return tile(c), tile(s1), tile(s2), cos.T, sin.T


def _rope_lanes(t, c, s1, s2, half):
    n = t.shape[1]
    reps = n // 128
    tl = lambda a: jnp.concatenate([a] * reps, axis=1)
    return t * tl(c) + pltpu.roll(t, n - half, 1) * tl(s1) + pltpu.roll(t, half, 1) * tl(s2)


def _rope_rows(t, cT, sT, head_dim, half):
    pieces = []
    for h in range(t.shape[0] // head_dim):
        x1 = t[h * head_dim:h * head_dim + half]
        x2 = t[h * head_dim + half:h * head_dim + 2 * half]
        pieces += [x1 * cT - x2 * sT, x2 * cT + x1 * sT, t[h * head_dim + 2 * half:(h + 1) * head_dim]]
    return jnp.concatenate(pieces, axis=0)


def _dsa_proj_kernel(x_ref, wq_ref, wiq_ref, wv_ref, wvx_ref, wiw_ref, wkT_ref, wikT_ref,
                     c_ref, s1_ref, s2_ref, cT_ref, sT_ref, kng_ref, knb_ref, one_ref,
                     q_ref, iq_ref, v_ref, vx_ref, iw_ref, kT_ref, kTb_ref, ikT_ref, ikTb_ref):
    xb = x_ref[...].astype(MXU_DTYPE)
    c, s1, s2 = c_ref[...], s1_ref[...], s2_ref[...]
    cT, sT = cT_ref[...], sT_ref[...]
    dot = lambda a, b: jnp.dot(a, b, preferred_element_type=jnp.float32)
    dot_t = lambda w, a: lax.dot_general(w, a, (((1,), (1,)), ((), ())), preferred_element_type=jnp.float32)

    q = _rope_lanes(dot(xb, wq_ref[...]), c, s1, s2, ROPE_DIM // 2)
    q_ref[...] = (q * (ATT_HEAD_DIM ** -0.5)).astype(q_ref.dtype)
    iq = _rope_lanes(dot(xb, wiq_ref[...]), c, s1, s2, IDX_ROPE_DIM // 2)
    iq_ref[...] = iq.astype(iq_ref.dtype)
    v_ref[...] = dot(xb, wv_ref[...])
    vx_ref[...] = (dot(xb, wvx_ref[...]) + one_ref[...]).astype(vx_ref.dtype)
    iw_ref[...] = dot(xb, wiw_ref[...]) * (IDX_HEADS ** -0.5 * IDX_DIM ** -0.5)

    kT = _rope_rows(dot_t(wkT_ref[...], xb), cT, sT, ATT_HEAD_DIM, ROPE_DIM // 2)
    kT_ref[0, 0] = kT
    kTb_ref[0, 0] = kT.astype(kTb_ref.dtype)
    ikT = dot_t(wikT_ref[...], xb)
    mu = jnp.mean(ikT, axis=0, keepdims=True)
    ikc = ikT - mu
    var = jnp.mean(ikc * ikc, axis=0, keepdims=True)
    ikT = ikc * lax.rsqrt(var + LN_EPS) * kng_ref[...] + knb_ref[...]
    ikT = _rope_rows(ikT, cT, sT, IDX_DIM, IDX_ROPE_DIM // 2)
    ikT_ref[0, 0] = ikT
    ikTb_ref[0, 0] = ikT.astype(ikTb_ref.dtype)


def _dsa_project(x3d, pos, w_in, kn_g, kn_b):
    b_, t_, _ = x3d.shape
    tk = KEY_GROUP if t_ % KEY_GROUP == 0 else t_
    ng = t_ // tk
    m = b_ * t_
    w_q, w_k, w_v, w_iq, w_ik, w_iw = jnp.split(w_in, list(ATT_IN_SPLITS), axis=1)
    cast = lambda w: w.astype(MXU_DTYPE)
    w_vx = jnp.pad(w_v.reshape(D_MODEL, ATT_KV_HEADS, ATT_HEAD_DIM),
                   ((0, 0), (0, 0), (0, 128 - ATT_HEAD_DIM))).reshape(D_MODEL, ATT_KV_HEADS * 128)
    one_col = jnp.tile((jnp.arange(128) == ATT_HEAD_DIM).astype(jnp.float32), ATT_KV_HEADS)[None, :]
    w_iw_pad = jnp.pad(w_iw, ((0, 0), (0, 128 - IDX_HEADS)))
    c, s1, s2, cT, sT = _rope_lane_tables(pos, ROPE_DIM, ATT_HEAD_DIM)
    full = lambda shape: pl.BlockSpec(shape, lambda b, i: (0,) * len(shape))
    rows = lambda n: pl.BlockSpec((tk, n), lambda b, i: (b * ng + i, 0))
    ptab = lambda n: pl.BlockSpec((tk, n), lambda b, i: (i, 0))
    grp = lambda n: pl.BlockSpec((1, 1, n, tk), lambda b, i: (b, i, 0, 0))
    sds = jax.ShapeDtypeStruct
    return pl.pallas_call(
        _dsa_proj_kernel,
        grid=(b_, ng),
        in_specs=[rows(D_MODEL), full((D_MODEL, ATT_Q_DIM)), full((D_MODEL, IDX_HEADS * IDX_DIM)),
                  full((D_MODEL, ATT_KV_DIM)), full((D_MODEL, ATT_KV_HEADS * 128)), full((D_MODEL, 128)),
                  full((ATT_KV_DIM, D_MODEL)), full((IDX_DIM, D_MODEL)),
                  ptab(128), ptab(128), ptab(128),
                  pl.BlockSpec((ROPE_DIM // 2, tk), lambda b, i: (0, i)),
                  pl.BlockSpec((ROPE_DIM // 2, tk), lambda b, i: (0, i)),
                  full((IDX_DIM, 1)), full((IDX_DIM, 1)), full((1, ATT_KV_HEADS * 128))],
        out_specs=[rows(ATT_Q_DIM), rows(IDX_HEADS * IDX_DIM), rows(ATT_KV_DIM), rows(ATT_KV_HEADS * 128),
                   rows(128), grp(ATT_KV_DIM), grp(ATT_KV_DIM), grp(IDX_DIM), grp(IDX_DIM)],
        out_shape=[sds((m, ATT_Q_DIM), MXU_DTYPE), sds((m, IDX_HEADS * IDX_DIM), MXU_DTYPE),
                   sds((m, ATT_KV_DIM), jnp.float32), sds((m, ATT_KV_HEADS * 128), MXU_DTYPE),
                   sds((m, 128), jnp.float32),
                   sds((b_, ng, ATT_KV_DIM, tk), jnp.float32), sds((b_, ng, ATT_KV_DIM, tk), MXU_DTYPE),
                   sds((b_, ng, IDX_DIM, tk), jnp.float32), sds((b_, ng, IDX_DIM, tk), MXU_DTYPE)],
        compiler_params=pltpu.CompilerParams(
            dimension_semantics=("parallel", "parallel"),
            vmem_limit_bytes=V7X_VMEM_LIMIT_BYTES),
        name="dsa_project",
    )(x3d.reshape(m, D_MODEL), cast(w_q), cast(w_iq), cast(w_v), cast(w_vx), cast(w_iw_pad),
      cast(w_k.T), cast(w_ik.T), c, s1, s2, cT, sT, kn_g.reshape(IDX_DIM, 1), kn_b.reshape(IDX_DIM, 1), one_col)


def _untranspose_groups(tg):
    b_, g_, r_, tk = tg.shape
    return jnp.transpose(tg, (0, 1, 3, 2)).reshape(b_, g_ * tk, r_)


def _dsa_proj_q_lanes_kernel(x_ref, wqT_ref, wiqT_ref, wiwT_ref, wk_ref, wv_ref, wvxT_ref, wik_ref,
                             c_ref, s1_ref, s2_ref, cT_ref, sT_ref, kng_ref, knb_ref, onerow_ref,
                             qT_ref, iqT_ref, iwT_ref, k_ref, khd_ref, v_ref, vxT_ref, ik_ref, ikb_ref):
    xb = x_ref[...].astype(MXU_DTYPE)
    tm = xb.shape[0]
    c, s1, s2 = c_ref[...], s1_ref[...], s2_ref[...]
    cT, sT = cT_ref[...], sT_ref[...]
    dot = lambda a, b: jnp.dot(a, b, preferred_element_type=jnp.float32)
    dot_t = lambda w, a: lax.dot_general(w, a, (((1,), (1,)), ((), ())), preferred_element_type=jnp.float32)

    qT = _rope_rows(dot_t(wqT_ref[...], xb), cT, sT, ATT_HEAD_DIM, ROPE_DIM // 2) * (ATT_HEAD_DIM ** -0.5)
    iqT = _rope_rows(dot_t(wiqT_ref[...], xb), cT, sT, IDX_DIM, IDX_ROPE_DIM // 2)
    iwT = dot_t(wiwT_ref[...], xb) * (IDX_HEADS ** -0.5 * IDX_DIM ** -0.5)
    for t in range(tm // Q_BLOCK):
        lanes = slice(t * Q_BLOCK, (t + 1) * Q_BLOCK)
        qT_ref[0, t] = qT[:, lanes].astype(qT_ref.dtype)
        iqT_ref[0, t] = iqT[:, lanes].astype(iqT_ref.dtype)
        iwT_ref[0, t] = iwT[:, lanes]

    k = _rope_lanes(dot(xb, wk_ref[...]), c, s1, s2, ROPE_DIM // 2)
    k_ref[...] = k
    for g in range(ATT_KV_HEADS):
        khd_ref[g] = k[:, g * ATT_HEAD_DIM:(g + 1) * ATT_HEAD_DIM].astype(khd_ref.dtype)
    v_ref[...] = dot(xb, wv_ref[...])
    vxT_ref[0, 0] = (dot_t(wvxT_ref[...], xb) + onerow_ref[...]).astype(vxT_ref.dtype)

    ik = dot(xb, wik_ref[...])
    real = lax.broadcasted_iota(jnp.int32, ik.shape, 1) < IDX_DIM
    mu = jnp.sum(ik, axis=-1, keepdims=True) * (1.0 / IDX_DIM)
    ikc = jnp.where(real, ik - mu, 0.0)
    var = jnp.sum(ikc * ikc, axis=-1, keepdims=True) * (1.0 / IDX_DIM)
    ikn = _rope_lanes(ikc * lax.rsqrt(var + LN_EPS) * kng_ref[...] + knb_ref[...], c, s1, s2, IDX_ROPE_DIM // 2)
    ik_ref[...] = ikn[:, :IDX_DIM]
    ikb_ref[...] = ikn[:, :IDX_DIM].astype(ikb_ref.dtype)


def _dsa_project_q_lanes(x3d, pos, w_in, kn_g, kn_b):
    b_, t_, _ = x3d.shape
    tk = KEY_GROUP
    ng = t_ // tk
    nq = tk // Q_BLOCK
    m = b_ * t_
    w_q, w_k, w_v, w_iq, w_ik, w_iw = jnp.split(w_in, list(ATT_IN_SPLITS), axis=1)
    cast = lambda w: w.astype(MXU_DTYPE)
    w_vxT = jnp.pad(w_v.T.reshape(ATT_KV_HEADS, ATT_HEAD_DIM, D_MODEL),
                    ((0, 0), (0, 128 - ATT_HEAD_DIM), (0, 0))).reshape(ATT_KV_HEADS * 128, D_MODEL)
    one_row = jnp.tile((jnp.arange(128) == ATT_HEAD_DIM).astype(jnp.float32), ATT_KV_HEADS)[:, None]
    pad_lanes = lambda a: jnp.pad(a, ((0, 0), (0, 128 - a.shape[1])))
    c, s1, s2, cT, sT = _rope_lane_tables(pos, ROPE_DIM, ATT_HEAD_DIM)
    full = lambda shape: pl.BlockSpec(shape, lambda b, i: (0,) * len(shape))
    rows = lambda n: pl.BlockSpec((tk, n), lambda b, i: (b * ng + i, 0))
    ptab = lambda n: pl.BlockSpec((tk, n), lambda b, i: (i, 0))
    qtile = lambda n: pl.BlockSpec((1, nq, n, Q_BLOCK), lambda b, i: (b, i, 0, 0))
    sds = jax.ShapeDtypeStruct
    return pl.pallas_call(
        _dsa_proj_q_lanes_kernel,
        grid=(b_, ng),
        in_specs=[rows(D_MODEL), full((ATT_Q_DIM, D_MODEL)), full((IDX_HEADS * IDX_DIM, D_MODEL)),
                  full((IDX_HEADS, D_MODEL)), full((D_MODEL, ATT_KV_DIM)), full((D_MODEL, ATT_KV_DIM)),
                  full((ATT_KV_HEADS * 128, D_MODEL)), full((D_MODEL, 128)),
                  ptab(128), ptab(128), ptab(128),
                  pl.BlockSpec((ROPE_DIM // 2, tk), lambda b, i: (0, i)),
                  pl.BlockSpec((ROPE_DIM // 2, tk), lambda b, i: (0, i)),
                  full((1, 128)), full((1, 128)), full((ATT_KV_HEADS * 128, 1))],
        out_specs=[qtile(ATT_Q_DIM), qtile(IDX_HEADS * IDX_DIM), qtile(IDX_HEADS),
                   rows(ATT_KV_DIM), pl.BlockSpec((ATT_KV_HEADS, tk, ATT_HEAD_DIM), lambda b, i: (0, b * ng + i, 0)),
                   rows(ATT_KV_DIM), pl.BlockSpec((1, 1, ATT_KV_HEADS * 128, tk), lambda b, i: (b, i, 0, 0)),
                   rows(IDX_DIM), rows(IDX_DIM)],
        out_shape=[sds((b_, t_ // Q_BLOCK, ATT_Q_DIM, Q_BLOCK), MXU_DTYPE),
                   sds((b_, t_ // Q_BLOCK, IDX_HEADS * IDX_DIM, Q_BLOCK), MXU_DTYPE),
                   sds((b_, t_ // Q_BLOCK, IDX_HEADS, Q_BLOCK), jnp.float32),
                   sds((m, ATT_KV_DIM), jnp.float32), sds((ATT_KV_HEADS, m, ATT_HEAD_DIM), MXU_DTYPE),
                   sds((m, ATT_KV_DIM), jnp.float32), sds((b_, ng, ATT_KV_HEADS * 128, tk), MXU_DTYPE),
                   sds((m, IDX_DIM), jnp.float32), sds((m, IDX_DIM), MXU_DTYPE)],
        compiler_params=pltpu.CompilerParams(
            dimension_semantics=("parallel", "parallel"),
            vmem_limit_bytes=V7X_VMEM_LIMIT_BYTES),
        name="dsa_project_q_lanes",
    )(x3d.reshape(m, D_MODEL), cast(w_q.T), cast(w_iq.T), cast(w_iw.T), cast(w_k), cast(w_v), cast(w_vxT),
      cast(pad_lanes(w_ik)), c, s1, s2, cT, sT, pad_lanes(kn_g.reshape(1, IDX_DIM)),
      pad_lanes(kn_b.reshape(1, IDX_DIM)), one_row)


def _tree_sum(parts):
    while len(parts) > 1:
        parts = [parts[i] + parts[i + 1] for i in range(0, len(parts) - 1, 2)] + (
            [parts[-1]] if len(parts) % 2 else [])
    return parts[0]


def _dsa_attend_q_lanes_kernel(iqT_ref, iwT_ref, ik_ref, qT_ref, k_ref, vxT_ref, o_ref,
                               key_ref, bias_ref, m_ref, acc_ref, s_ref, *, topk, col_bits):
    j = pl.program_id(1)
    tk, tq = key_ref.shape[1], key_ref.shape[2]
    n_groups = (j * tq + tq + tk - 1) // tk
    qpos = j * tq + lax.broadcasted_iota(jnp.int32, (tk, tq), 1)
    kpos0 = lax.broadcasted_iota(jnp.int32, (tk, tq), 0)
    dot = lambda a, b: jnp.dot(a, b, preferred_element_type=jnp.float32)

    def score_body(g, carry):
        start = pl.multiple_of(g * tk, tk)
        w_iq = jnp.concatenate([iqT_ref[0, 0, h * IDX_DIM:(h + 1) * IDX_DIM, :] for h in range(IDX_HEADS)], axis=1)
        s_all = dot(ik_ref[0, pl.ds(start, tk), :], w_iq)
        sc = _tree_sum([iwT_ref[0, 0, h:h + 1, :] * jnp.maximum(s_all[:, h * tq:(h + 1) * tq], 0.0)
                        for h in range(IDX_HEADS)])
        key_ref[g] = jnp.where(kpos0 + g * tk <= qpos, _sortable_key(sc), jnp.int32(INT32_MIN))
        return carry

    lax.fori_loop(0, n_groups, score_body, 0)

    def count_keys(pred):
        def body(g, part):
            hit = jnp.where(pred(key_ref[g], kpos0 + g * tk), 1.0, 0.0)
            return part + _tree_sum([hit[r * SUBLANES:(r + 1) * SUBLANES] for r in range(tk // SUBLANES)])
        part = lax.fori_loop(0, n_groups, body, jnp.zeros((SUBLANES, tq), jnp.float32))
        return jnp.sum(part, axis=0, keepdims=True)

    def bit_body(i, thr):
        cand = thr ^ lax.shift_left(jnp.int32(1), jnp.int32(31) - i)
        return jnp.where(count_keys(lambda k, kp: k >= cand) >= float(topk), cand, thr)

    thr = lax.fori_loop(0, 32, bit_body, jnp.full((1, tq), INT32_MIN, jnp.int32))

    need = float(topk) - count_keys(lambda k, kp: k > thr)

    def pos_body(i, last):
        cand = last | lax.shift_left(jnp.int32(1), jnp.int32(col_bits - 1) - i)
        return jnp.where(count_keys(lambda k, kp: (k == thr) & (kp < cand)) < need, cand, last)

    n_tied = count_keys(lambda k, kp: k == thr)
    excess = jnp.max(jnp.where(n_tied > need, 1.0, 0.0), axis=1, keepdims=True)
    last_tie = lax.cond(excess[0, 0] > 0.0,
                        lambda: lax.fori_loop(0, col_bits, pos_body, jnp.zeros((1, tq), jnp.int32)),
                        lambda: jnp.full((1, tq), 2 ** col_bits - 1, jnp.int32))

    m_ref[...] = jnp.full(m_ref.shape, MASK_NEG, jnp.float32)
    acc_ref[...] = jnp.zeros(acc_ref.shape, jnp.float32)
    gsz = ATT_HEADS // ATT_KV_HEADS

    def attend_body(g, carry):
        start = pl.multiple_of(g * tk, tk)
        key = key_ref[g]
        kpos = kpos0 + g * tk
        keep = (key > thr) | ((key == thr) & (kpos <= last_tie))
        bias_ref[...] = jnp.where(keep & (kpos <= qpos), 0.0, MASK_NEG)
        for kv in range(ATT_KV_HEADS):
            w_q = jnp.concatenate([qT_ref[0, 0, (kv * gsz + i) * ATT_HEAD_DIM:(kv * gsz + i + 1) * ATT_HEAD_DIM, :]
                                   for i in range(gsz)], axis=1)
            s_ref[kv] = dot(k_ref[kv, pl.ds(start, tk), :], w_q)
        for kv in range(ATT_KV_HEADS):
            s = s_ref[kv] + jnp.concatenate([bias_ref[...]] * gsz, axis=1)
            m_old = m_ref[kv]
            m_new = jnp.maximum(m_old, jnp.max(s, axis=0, keepdims=True))
            p = jnp.exp(s - m_new).astype(vxT_ref.dtype)
            pv = dot(vxT_ref[0, g, kv * 128:(kv + 1) * 128, :], p)
            acc_ref[kv] = jnp.exp(m_old - m_new) * acc_ref[kv] + pv
            m_ref[kv] = m_new
        return carry

    lax.fori_loop(0, n_groups, attend_body, 0)

    for h in range(ATT_HEADS):
        a = acc_ref[h // gsz, :, (h % gsz) * tq:(h % gsz + 1) * tq]
        o = (a / a[ATT_HEAD_DIM:ATT_HEAD_DIM + 1, :]).T
        o_ref[:, h * ATT_HEAD_DIM:(h + 1) * ATT_HEAD_DIM] = o[:, :ATT_HEAD_DIM].astype(o_ref.dtype)


def _dsa_attend_q_lanes(b_, t_, qT, iqT, iwT, ikb, khd, vxT):
    ng, tk = vxT.shape[1], vxT.shape[3]
    tq = Q_BLOCK
    nq = t_ // tq
    qtile = lambda n: pl.BlockSpec((1, 1, n, tq), lambda b, j: (b, j, 0, 0))
    return pl.pallas_call(
        functools.partial(_dsa_attend_q_lanes_kernel, topk=min(TOPK_MAX, t_ // 4),
                          col_bits=max(1, (t_ - 1).bit_length())),
        grid=(b_, nq),
        in_specs=[qtile(IDX_HEADS * IDX_DIM), qtile(IDX_HEADS),
                  pl.BlockSpec((1, t_, IDX_DIM), lambda b, j: (b, 0, 0)),
                  qtile(ATT_Q_DIM),
                  pl.BlockSpec((ATT_KV_HEADS, t_, ATT_HEAD_DIM), lambda b, j: (0, b, 0)),
                  pl.BlockSpec((1, ng, ATT_KV_HEADS * 128, tk), lambda b, j: (b, 0, 0, 0))],
        out_specs=pl.BlockSpec((tq, ATT_Q_DIM), lambda b, j: (b * nq + j, 0)),
        out_shape=jax.ShapeDtypeStruct((b_ * t_, ATT_Q_DIM), MXU_DTYPE),
        scratch_shapes=[pltpu.VMEM((ng, tk, tq), jnp.int32),
                        pltpu.VMEM((tk, tq), jnp.float32),
                        pltpu.VMEM((ATT_KV_HEADS, 1, tq * (ATT_HEADS // ATT_KV_HEADS)), jnp.float32),
                        pltpu.VMEM((ATT_KV_HEADS, 128, tq * (ATT_HEADS // ATT_KV_HEADS)), jnp.float32),
                        pltpu.VMEM((ATT_KV_HEADS, tk, tq * (ATT_HEADS // ATT_KV_HEADS)), jnp.float32)],
        compiler_params=pltpu.CompilerParams(
            dimension_semantics=("parallel", "arbitrary"),
            vmem_limit_bytes=V7X_VMEM_LIMIT_BYTES),
        name="dsa_attend_q_lanes",
    )(iqT, iwT, ikb.reshape(b_, t_, IDX_DIM), qT, khd, vxT)


def _dsa_prompt_pallas(x3d, w_in, kn_g, kn_b):
    b_, t_, _ = x3d.shape
    qT, iqT, iwT, k, khd, v, vxT, ik, ikb = _dsa_project_q_lanes(x3d, jnp.arange(t_), w_in, kn_g, kn_b)
    o = _dsa_attend_q_lanes(b_, t_, qT, iqT, iwT, ikb, khd, vxT)
    kv4 = lambda u: u.reshape(b_, t_, ATT_KV_HEADS, ATT_HEAD_DIM)
    return o, kv4(k), kv4(v), ik.reshape(b_, t_, IDX_DIM)


RW_ROW_TILE = 256


RW_PAIRS = RW_HEADS // 2
RW_PAIR_LANES = 2 * RW_HEAD


def _rwkv_project_rows(x, xp, mu_ref, wr_ref, wk_ref, wv_ref, w1_ref, w2_ref, a1_ref, a2_ref,
                       g1_ref, g2_ref, w0_ref, a0_ref):
    dx = xp - x
    mix = lambda c: (x + dx * mu_ref[c:c + 1, :]).astype(MXU_DTYPE)
    dot = lambda a, b: jnp.dot(a.astype(MXU_DTYPE), b, preferred_element_type=jnp.float32)
    r = dot(mix(0), wr_ref[...])
    lora_w = dot(jnp.tanh(dot(mix(1), w1_ref[...])), w2_ref[...])
    w_log = -jax.nn.softplus(-(w0_ref[...] + lora_w)) - 0.5
    d = jnp.exp(-jnp.exp(w_log))
    k = dot(mix(2), wk_ref[...])
    v = dot(mix(3), wv_ref[...])
    a = jax.nn.sigmoid(a0_ref[...] + dot(dot(mix(4), a1_ref[...]), a2_ref[...]))
    g = dot(jax.nn.sigmoid(dot(mix(5), g1_ref[...])), g2_ref[...])
    return r, d, k, v, a, g


def _rwkv_proj_step_kernel(x_ref, xp_ref, *refs):
    vals = _rwkv_project_rows(x_ref[...], xp_ref[...], *refs[:12])
    for ref, val in zip(refs[12:], vals):
        ref[...] = val


def _rwkv_proj_seq_kernel(x_ref, halo_ref, shift_ref, *refs):
    i = pl.program_id(1)
    x = x_ref[...]
    prev = jnp.where(i == 0, shift_ref[0], halo_ref[...])[SUBLANES - 1:SUBLANES, :]
    first = lax.broadcasted_iota(jnp.int32, (x.shape[0], 1), 0) == 0
    xp = jnp.where(first, prev, pltpu.roll(x, 1, 0))
    vals = _rwkv_project_rows(x, xp, *refs[:12])
    for ref, val in zip(refs[12:], vals):
        ref[...] = val


def _rwkv_consts(mu, w_r, w_k, w_v, w0, w1, w2, a0, a1, a2, g1, g2):
    cast = lambda w: w.astype(MXU_DTYPE)
    return [mu, cast(w_r), cast(w_k), cast(w_v), cast(w1), cast(w2), cast(a1), cast(a2), cast(g1), cast(g2),
            w0.reshape(1, D_MODEL), a0.reshape(1, D_MODEL)]


def _rwkv_project_step(x2d, xprev2d, *params):
    m = x2d.shape[0]
    consts = _rwkv_consts(*params)
    full = lambda a: pl.BlockSpec(a.shape, lambda i: (0,) * a.ndim)
    rows = pl.BlockSpec((m, D_MODEL), lambda i: (0, 0))
    return pl.pallas_call(
        _rwkv_proj_step_kernel,
        grid=(1,),
        in_specs=[rows, rows] + [full(a) for a in consts],
        out_specs=[rows] * 6,
        out_shape=[jax.ShapeDtypeStruct((m, D_MODEL), jnp.float32)] * 6,
        compiler_params=pltpu.CompilerParams(
            dimension_semantics=("arbitrary",),
            vmem_limit_bytes=V7X_VMEM_LIMIT_BYTES),
        name="rwkv_project_step",
    )(x2d, xprev2d, *consts)


def _rwkv_project_seq(x3d, shift, *params):
    b_, t_, _ = x3d.shape
    m = b_ * t_
    tm = RW_ROW_TILE
    nt = t_ // tm
    consts = _rwkv_consts(*params)
    full = lambda a: pl.BlockSpec(a.shape, lambda b, i: (0,) * a.ndim)
    rows = pl.BlockSpec((tm, D_MODEL), lambda b, i: (b * nt + i, 0))
    halo = pl.BlockSpec((SUBLANES, D_MODEL), lambda b, i: (jnp.maximum((b * nt + i) * (tm // SUBLANES) - 1, 0), 0))
    shift8 = jnp.pad(shift[:, None, :], ((0, 0), (SUBLANES - 1, 0), (0, 0)))
    x2d = x3d.reshape(m, D_MODEL)
    return pl.pallas_call(
        _rwkv_proj_seq_kernel,
        grid=(b_, nt),
        in_specs=[rows, halo, pl.BlockSpec((1, SUBLANES, D_MODEL), lambda b, i: (b, 0, 0))]
        + [full(a) for a in consts],
        out_specs=[rows] * 6,
        out_shape=[jax.ShapeDtypeStruct((m, D_MODEL), jnp.float32)] * 6,
        compiler_params=pltpu.CompilerParams(
            dimension_semantics=("parallel", "parallel"),
            vmem_limit_bytes=V7X_VMEM_LIMIT_BYTES),
        name="rwkv_project_seq",
    )(x2d, x2d, shift8, *consts)


RW_LANES = 128
RW_TIME_CHUNK = 64


def _rwkv_scan_kernel(r_ref, d_ref, k_ref, v_ref, a_ref, s0_ref, kk_ref, ka_ref, rk_ref, gg_ref, gb_ref,
                      z_ref, s_out_ref, s_ref, vec_ref):
    c = pl.program_id(1)
    n = RW_HEAD
    tc = r_ref.shape[1]
    low_half = lax.broadcasted_iota(jnp.int32, (n, RW_LANES), 1) < n

    @pl.when(c == 0)
    def _():
        s_ref[...] = s0_ref[...]

    def swap_layout(x):
        xt = jnp.concatenate([x, x], axis=0).T
        return jnp.where(low_half, xt[:n], xt[n:])

    def load_step(ref, t):
        rows = ref[:, t, :]
        return swap_layout(jnp.concatenate(
            [rows[:, p * RW_PAIR_LANES:(p + 1) * RW_PAIR_LANES] for p in range(RW_PAIRS)], axis=0))

    def store_step(ref, t, val):
        tile = swap_layout(val)
        ref[:, t, :] = jnp.concatenate(
            [tile[p * RW_SEQ_PER_TILE:(p + 1) * RW_SEQ_PER_TILE] for p in range(RW_PAIRS)], axis=1)

    def prepare(t, slot):
        r, k, a = load_step(r_ref, t), load_step(k_ref, t), load_step(a_ref, t)
        kkr = k * kk_ref[...]
        nrm = jnp.sqrt(jnp.sum(kkr * kkr, axis=0, keepdims=True))
        kk = kkr / jnp.maximum(nrm, 1e-12)
        vec_ref[slot, 0] = kk
        vec_ref[slot, 1] = load_step(d_ref, t)
        vec_ref[slot, 2] = kk * a
        vec_ref[slot, 3] = k * (1.0 + (a - 1.0) * ka_ref[...])
        vec_ref[slot, 4] = r
        vec_ref[slot, 5] = load_step(v_ref, t)

    def step(t, slot):
        row = lambda q, j: vec_ref[slot, q, j:j + 1, :]
        v = vec_ref[slot, 5]
        lanes = 4
        sa_parts = [s_ref[j] * row(0, j) for j in range(lanes)]
        for j in range(lanes, n):
            sa_parts[j % lanes] = sa_parts[j % lanes] + s_ref[j] * row(0, j)
        sa = _tree_sum(sa_parts)
        y_parts = []
        for j in range(n):
            sn = s_ref[j] * row(1, j) - sa * row(2, j) + v * row(3, j)
            s_ref[j] = sn
            if j < lanes:
                y_parts.append(sn * row(4, j))
            else:
                y_parts[j % lanes] = y_parts[j % lanes] + sn * row(4, j)
        y = _tree_sum(y_parts)
        mu = jnp.mean(y, axis=0, keepdims=True)
        yc = y - mu
        var = jnp.mean(yc * yc, axis=0, keepdims=True)
        bonus = jnp.sum(vec_ref[slot, 4] * vec_ref[slot, 3] * rk_ref[...], axis=0, keepdims=True)
        store_step(z_ref, t, yc * lax.rsqrt(var + RW_GN_EPS) * gg_ref[...] + gb_ref[...] + bonus * v)

    prepare(0, 0)
    if tc == 1:
        step(0, 0)
    else:
        def two_steps(i, carry):
            t = 2 * i
            prepare(t + 1, 1)
            step(t, 0)
            prepare(jnp.minimum(t + 2, tc - 1), 0)
            step(t + 1, 1)
            return carry

        lax.fori_loop(0, tc // 2, two_steps, 0)

    @pl.when(c == pl.num_programs(1) - 1)
    def _():
        s_out_ref[...] = s_ref[...]


RW_SEQ_PER_TILE = RW_LANES // RW_HEADS


def _rwkv_lane_heads():
    half = jnp.arange(2)[:, None, None]
    pair = jnp.arange(RW_PAIRS)[None, :, None]
    return jnp.broadcast_to(2 * pair + half, (2, RW_PAIRS, RW_SEQ_PER_TILE)).reshape(RW_LANES)


def _rwkv_scan(r, d, k, v, a, s0, k_k, k_a, r_k, gn_g, gn_b):
    b_, t_, _ = r.shape
    n = RW_HEAD
    tc = RW_TIME_CHUNK if t_ % RW_TIME_CHUNK == 0 else t_
    table = lambda p: p.reshape(RW_HEADS, n)[_rwkv_lane_heads()].T
    seq = pl.BlockSpec((RW_SEQ_PER_TILE, tc, D_MODEL), lambda l, c: (l, c, 0))
    state = pl.BlockSpec((n, n, RW_LANES), lambda l, c: (0, 0, l))
    tab = pl.BlockSpec((n, RW_LANES), lambda l, c: (0, 0))
    return pl.pallas_call(
        _rwkv_scan_kernel,
        grid=(b_ // RW_SEQ_PER_TILE, t_ // tc),
        in_specs=[seq] * 5 + [state] + [tab] * 5,
        out_specs=[seq, state],
        out_shape=[jax.ShapeDtypeStruct(r.shape, jnp.float32),
                   jax.ShapeDtypeStruct(s0.shape, jnp.float32)],
        scratch_shapes=[pltpu.VMEM((n, n, RW_LANES), jnp.float32),
                        pltpu.VMEM((2, 6, n, RW_LANES), jnp.float32)],
        compiler_params=pltpu.CompilerParams(
            dimension_semantics=("parallel", "arbitrary"),
            vmem_limit_bytes=V7X_VMEM_LIMIT_BYTES),
        name="rwkv_scan",
    )(r, d, k, v, a, s0, table(k_k), table(k_a), table(r_k), table(gn_g), table(gn_b))


def _rwkv_state_to_lanes(wkv):
    b_ = wkv.shape[0]
    w = wkv.astype(jnp.float32).reshape(b_ // RW_SEQ_PER_TILE, RW_SEQ_PER_TILE, RW_PAIRS, 2, RW_HEAD, RW_HEAD)
    return jnp.transpose(w, (5, 4, 0, 3, 2, 1)).reshape(RW_HEAD, RW_HEAD, b_ * RW_HEADS)


def _rwkv_state_from_lanes(s, b_):
    w = s.reshape(RW_HEAD, RW_HEAD, b_ // RW_SEQ_PER_TILE, 2, RW_PAIRS, RW_SEQ_PER_TILE)
    return jnp.transpose(w, (2, 5, 4, 3, 1, 0)).reshape(b_, RW_HEADS, RW_HEAD, RW_HEAD)


def _rwkv7_mixer_pallas(x3d, shift, wkv, mu, w_r, w_k, w_v, w0, w1, w2, a0, a1, a2, g1, g2,
                        k_k, k_a, r_k, gn_g, gn_b):
    b_, t_, _ = x3d.shape
    params = (mu, w_r, w_k, w_v, w0, w1, w2, a0, a1, a2, g1, g2)
    if t_ == 1:
        *seqs, g = _rwkv_project_step(x3d.reshape(b_, D_MODEL), shift, *params)
    else:
        *seqs, g = _rwkv_project_seq(x3d, shift, *params)
    seqs = [u.reshape(b_, t_, D_MODEL) for u in seqs]
    z, s = _rwkv_scan(*seqs, _rwkv_state_to_lanes(wkv), k_k, k_a, r_k, gn_g, gn_b)
    return z.reshape(b_ * t_, D_MODEL), g, x3d[:, -1], _rwkv_state_from_lanes(s, b_).astype(wkv.dtype)


def _proj_gate_ln_kernel(x_ref, h_ref, gate_ref, w_ref, g_ref, b_ref, o_ref):
    h = (h_ref[...] * gate_ref[...]).astype(MXU_DTYPE)
    y = ALPHA * x_ref[...] + jnp.dot(h, w_ref[...], preferred_element_type=jnp.float32)
    o_ref[...] = _ln_rows(y, g_ref[...], b_ref[...])


def _proj_gate_post_norm(x2d, h2d, gate2d, w_out, g, b):
    m = x2d.shape[0]
    tm = _row_tile(m)
    rows = pl.BlockSpec((tm, D_MODEL), lambda i: (i, 0))
    vec = pl.BlockSpec((1, D_MODEL), lambda i: (0, 0))
    return pl.pallas_call(
        _proj_gate_ln_kernel,
        grid=(m // tm,),
        in_specs=[rows, rows, rows, pl.BlockSpec((D_MODEL, D_MODEL), lambda i: (0, 0)), vec, vec],
        out_specs=rows,
        out_shape=jax.ShapeDtypeStruct((m, D_MODEL), jnp.float32),
        compiler_params=pltpu.CompilerParams(
            dimension_semantics=("parallel",),
            vmem_limit_bytes=V7X_VMEM_LIMIT_BYTES),
        name="proj_gate_post_norm",
    )(x2d, h2d, gate2d, w_out, g.reshape(1, D_MODEL), b.reshape(1, D_MODEL))


GM_ROW_TILE = 512
GM_ROW_PARTS = 2


def _gmlp_kernel(x_ref, win_ref, lng_ref, lnb_ref, mixw_ref, mixb_ref, wout_ref, g_ref, b_ref, *out_refs,
                 chunk_len, emit_v, parts):
    tp = x_ref.shape[0] // parts
    for part in range(parts):
        rows_p = slice(part * tp, (part + 1) * tp)
        x = x_ref[rows_p, :]
        h = jax.nn.gelu(jnp.dot(x.astype(MXU_DTYPE), win_ref[...], preferred_element_type=jnp.float32))
        u = h[:, :GM_WIDTH]
        v = _ln_rows(h[:, GM_WIDTH:], lng_ref[...], lnb_ref[...])
        if emit_v:
            out_refs[1][rows_p, :] = v
        if chunk_len == 1:
            gated = u * (v * mixw_ref[...] + mixb_ref[...])
        else:
            causal = (lax.broadcasted_iota(jnp.int32, (chunk_len, chunk_len), 0)
                      >= lax.broadcasted_iota(jnp.int32, (chunk_len, chunk_len), 1))
            vb = v.astype(MXU_DTYPE)
            cols = []
            for g in range(GM_GROUPS):
                w = jnp.where(causal, mixw_ref[g], 0.0).astype(MXU_DTYPE)
                bias = mixb_ref[:, g:g + 1]
                lanes = slice(g * GM_GROUP_DIM, (g + 1) * GM_GROUP_DIM)
                rows = [jnp.dot(w, vb[c * chunk_len:(c + 1) * chunk_len, lanes],
                                preferred_element_type=jnp.float32) + bias
                        for c in range(tp // chunk_len)]
                cols.append(jnp.concatenate(rows, axis=0))
            gated = u * jnp.concatenate(cols, axis=1)
        y = ALPHA * x + jnp.dot(gated.astype(MXU_DTYPE), wout_ref[...], preferred_element_type=jnp.float32)
        out_refs[0][rows_p, :] = _ln_rows(y, g_ref[...], b_ref[...])


def _gmlp_block(x2d, seq_len, w_in, ln_g, ln_b, ws, bs, w_out, g, b, emit_v):
    m = x2d.shape[0]
    chunk_len = min(seq_len, CHUNK)
    if chunk_len == 1:
        tm, parts = m, 1
        mixw = jnp.repeat(ws[:, 0, 0], GM_GROUP_DIM)[None, :]
        mixb = jnp.repeat(bs[:, 0], GM_GROUP_DIM)[None, :]
    else:
        tm, parts = GM_ROW_TILE, GM_ROW_PARTS
        mixw = ws[:, :chunk_len, :chunk_len]
        mixb = bs[:, :chunk_len].T
    full = lambda a: pl.BlockSpec(a.shape, lambda i: (0,) * a.ndim, pipeline_mode=pl.Buffered(1))
    rows = lambda n: pl.BlockSpec((tm, n), lambda i: (i, 0))
    consts = [w_in.astype(MXU_DTYPE), ln_g.reshape(1, GM_WIDTH), ln_b.reshape(1, GM_WIDTH), mixw, mixb,
              w_out.astype(MXU_DTYPE), g.reshape(1, D_MODEL), b.reshape(1, D_MODEL)]
    out_specs = [rows(D_MODEL)] + ([rows(GM_WIDTH)] if emit_v else [])
    out_shape = [jax.ShapeDtypeStruct((m, D_MODEL), jnp.float32)] + (
        [jax.ShapeDtypeStruct((m, GM_WIDTH), jnp.float32)] if emit_v else [])
    return pl.pallas_call(
        functools.partial(_gmlp_kernel, chunk_len=chunk_len, emit_v=emit_v, parts=parts),
        grid=(m // tm,),
        in_specs=[rows(D_MODEL)] + [full(a) for a in consts],
        out_specs=out_specs,
        out_shape=out_shape,
        compiler_params=pltpu.CompilerParams(
            dimension_semantics=("parallel",),
            vmem_limit_bytes=V7X_VMEM_LIMIT_BYTES),
        name="gmlp_block",
    )(x2d, *consts)


SSM_ROW_TILE = 256
SSM_BC_DIM = SSM_GROUPS * SSM_STATE
SSM_DT_LANES = 128
SUBLANES = 8


def _ssm_activate(xb, xbc, taps, wz_ref, wdt_ref, cw_ref, cb_ref, dtb_ref, z_ref, xs_ref, bm_ref, cm_ref, dt_ref):
    conv = cb_ref[...] + xbc * cw_ref[SSM_CONV - 1:SSM_CONV, :]
    for j in range(SSM_CONV - 1):
        conv = conv + taps[j] * cw_ref[j:j + 1, :]
    act = conv * jax.nn.sigmoid(conv)
    xs_ref[...] = act[:, :SSM_D_INNER]
    bm_ref[...] = act[:, SSM_D_INNER:SSM_D_INNER + SSM_BC_DIM].astype(bm_ref.dtype)
    cm_ref[...] = act[:, SSM_D_INNER + SSM_BC_DIM:].astype(cm_ref.dtype)
    z_ref[...] = jnp.dot(xb, wz_ref[...], preferred_element_type=jnp.float32)
    dt_ref[...] = jax.nn.softplus(jnp.dot(xb, wdt_ref[...], preferred_element_type=jnp.float32) + dtb_ref[...])


def _ssm_proj_seq_kernel(x_ref, halo_ref, cs_ref, wx_ref, wz_ref, wdt_ref, cw_ref, cb_ref, dtb_ref,
                         z_ref, xs_ref, bm_ref, cm_ref, dt_ref, tail_ref):
    i = pl.program_id(1)
    xb = x_ref[...].astype(MXU_DTYPE)
    xbc = jnp.dot(xb, wx_ref[...], preferred_element_type=jnp.float32)
    tm = xbc.shape[0]
    prev = jnp.dot(halo_ref[...].astype(MXU_DTYPE), wx_ref[...], preferred_element_type=jnp.float32)
    prev = jnp.where(i == 0, cs_ref[0], prev)
    row = lax.broadcasted_iota(jnp.int32, (SUBLANES, 1), 0)
    taps = []
    for j in range(SSM_CONV - 1):
        back = SSM_CONV - 1 - j
        rolled = pltpu.roll(xbc, back, 0)
        top = jnp.where(row < back, pltpu.roll(prev, back, 0), rolled[:SUBLANES])
        taps.append(jnp.concatenate([top, rolled[SUBLANES:]], axis=0))
    _ssm_activate(xb, xbc, taps, wz_ref, wdt_ref, cw_ref, cb_ref, dtb_ref, z_ref, xs_ref, bm_ref, cm_ref, dt_ref)
    tail_ref[0] = xbc[tm - SUBLANES:, :]


def _ssm_proj_step_kernel(x_ref, st_ref, wx_ref, wz_ref, wdt_ref, cw_ref, cb_ref, dtb_ref,
                          z_ref, xs_ref, bm_ref, cm_ref, dt_ref, st_out_ref):
    xb = x_ref[...].astype(MXU_DTYPE)
    xbc = jnp.dot(xb, wx_ref[...], preferred_element_type=jnp.float32)
    taps = [st_ref[j] for j in range(SSM_CONV - 1)]
    _ssm_activate(xb, xbc, taps, wz_ref, wdt_ref, cw_ref, cb_ref, dtb_ref, z_ref, xs_ref, bm_ref, cm_ref, dt_ref)
    for j in range(SSM_CONV - 2):
        st_out_ref[j] = st_ref[j + 1]
    st_out_ref[SSM_CONV - 2] = xbc


def _ssm_project(x3d, conv_state, w_in, conv_w, conv_b, dt_bias):
    b_, t_, _ = x3d.shape
    m = b_ * t_
    w_z, w_x, w_dt = jnp.split(w_in, [SSM_D_INNER, SSM_D_INNER + SSM_CONV_DIM], axis=1)
    cast = lambda w: w.astype(MXU_DTYPE)
    consts = [cast(w_x), cast(w_z), cast(jnp.pad(w_dt, ((0, 0), (0, SSM_DT_LANES - SSM_HEADS)))),
              conv_w, conv_b.reshape(1, SSM_CONV_DIM),
              jnp.pad(dt_bias, (0, SSM_DT_LANES - SSM_HEADS)).reshape(1, SSM_DT_LANES)]
    sds = jax.ShapeDtypeStruct
    outs = [sds((m, SSM_D_INNER), jnp.float32), sds((m, SSM_D_INNER), jnp.float32),
            sds((m, SSM_BC_DIM), MXU_DTYPE), sds((m, SSM_BC_DIM), MXU_DTYPE), sds((m, SSM_DT_LANES), jnp.float32)]
    widths = [SSM_D_INNER, SSM_D_INNER, SSM_BC_DIM, SSM_BC_DIM, SSM_DT_LANES]
    params = dict(vmem_limit_bytes=V7X_VMEM_LIMIT_BYTES)
    x2d = x3d.reshape(m, D_MODEL)
    if t_ == 1:
        full = lambda a: pl.BlockSpec(a.shape, lambda i: (0,) * a.ndim)
        st = jnp.transpose(conv_state, (1, 0, 2))
        res = pl.pallas_call(
            _ssm_proj_step_kernel,
            grid=(1,),
            in_specs=[full(x2d), full(st)] + [full(a) for a in consts],
            out_specs=[pl.BlockSpec((m, w), lambda i: (0, 0)) for w in widths] + [full(st)],
            out_shape=outs + [sds(st.shape, jnp.float32)],
            compiler_params=pltpu.CompilerParams(dimension_semantics=("arbitrary",), **params),
            name="ssm_project_step",
        )(x2d, st, *consts)
        return list(res[:5]) + [jnp.transpose(res[5], (1, 0, 2))]
    tm = SSM_ROW_TILE
    nt = t_ // tm
    full = lambda a: pl.BlockSpec(a.shape, lambda b, i: (0,) * a.ndim)
    rows = lambda w: pl.BlockSpec((tm, w), lambda b, i: (b * nt + i, 0))
    halo = pl.BlockSpec((SUBLANES, D_MODEL), lambda b, i: (jnp.maximum((b * nt + i) * (tm // SUBLANES) - 1, 0), 0))
    cs8 = jnp.pad(conv_state, ((0, 0), (SUBLANES - (SSM_CONV - 1), 0), (0, 0)))
    tail = pl.BlockSpec((1, SUBLANES, SSM_CONV_DIM), lambda b, i: (b, 0, 0))
    res = pl.pallas_call(
        _ssm_proj_seq_kernel,
        grid=(b_, nt),
        in_specs=[rows(D_MODEL), halo, tail] + [full(a) for a in consts],
        out_specs=[rows(w) for w in widths] + [tail],
        out_shape=outs + [sds((b_, SUBLANES, SSM_CONV_DIM), jnp.float32)],
        compiler_params=pltpu.CompilerParams(dimension_semantics=("parallel", "arbitrary"), **params),
        name="ssm_project_seq",
    )(x2d, x2d, cs8, *consts)
    return list(res[:5]) + [res[5][:, SUBLANES - (SSM_CONV - 1):, :]]


def _ssm_gate_norm(y, xs, z, dskip, normg):
    yg = (y + xs * dskip) * (z * jax.nn.sigmoid(z))
    gw = SSM_D_INNER // SSM_GROUPS
    outs = []
    for g in range(SSM_GROUPS):
        part = yg[:, g * gw:(g + 1) * gw]
        ms = jnp.mean(part * part, axis=-1, keepdims=True)
        outs.append(part * lax.rsqrt(ms + LN_EPS))
    return jnp.concatenate(outs, axis=1) * normg


def _ssm_chunk_kernel(xs_ref, bm_ref, cm_ref, dt_ref, z_ref, aneg_ref, dskip_ref, normg_ref,
                      yg_ref, h_out_ref, h_ref, yT_ref, xe_ref):
    c = pl.program_id(1)
    l = xs_ref.shape[0]
    hd = SSM_HEAD_DIM

    @pl.when(c == 0)
    def _():
        h_ref[...] = jnp.zeros_like(h_ref)

    dot = lambda u, w: jnp.dot(u, w, preferred_element_type=jnp.float32)
    dt = dt_ref[...]
    a = dt * aneg_ref[...]
    r_i = lax.broadcasted_iota(jnp.int32, (l, l), 0)
    c_i = lax.broadcasted_iota(jnp.int32, (l, l), 1)
    tril = jnp.where(r_i >= c_i, 1.0, 0.0)
    hi = lax.Precision.HIGHEST
    acum = jnp.dot(tril, a, precision=hi, preferred_element_type=jnp.float32)
    acum_t = jnp.dot(a.T, tril.T, precision=hi, preferred_element_type=jnp.float32)
    dt_t = dt.T
    to_end_t = jnp.exp(acum_t[:, l - 1:l] - acum_t)
    from_start_t = jnp.exp(acum_t)
    chunk_decay = jnp.exp(acum[l - 1:l, :])
    upper = r_i <= c_i
    xs = xs_ref[...]
    for g in range(SSM_GROUPS):
        bm = bm_ref[:, g * SSM_STATE:(g + 1) * SSM_STATE]
        cm_t = cm_ref[:, g * SSM_STATE:(g + 1) * SSM_STATE].astype(jnp.float32).T.astype(MXU_DTYPE)
        cb_t = dot(bm, cm_t)
        h_in = h_ref[g * SSM_HPG:(g + 1) * SSM_HPG].reshape(SSM_HPG * hd, SSM_STATE)
        y_off = dot(h_in.astype(MXU_DTYPE), cm_t)
        for e in range(SSM_HPG):
            h = g * SSM_HPG + e
            if h % 2 == 0:
                xs_pair_t = xs[:, h * hd:(h + 2) * hd].T
            xdt_t = xs_pair_t[(h % 2) * hd:(h % 2 + 1) * hd] * dt_t[h:h + 1, :]
            seg = jnp.exp(jnp.where(upper, acum_t[h:h + 1, :] - acum[:, h:h + 1], -jnp.inf))
            y_diag = dot(xdt_t.astype(MXU_DTYPE), (cb_t * seg).astype(MXU_DTYPE))
            yT_ref[h * hd:(h + 1) * hd, :] = y_diag + y_off[e * hd:(e + 1) * hd] * from_start_t[h:h + 1, :]
            xe_ref[e * hd:(e + 1) * hd, :] = (xdt_t * to_end_t[h:h + 1, :]).astype(xe_ref.dtype)
        states = dot(xe_ref[...], bm)
        for e in range(SSM_HPG):
            h = g * SSM_HPG + e
            h_ref[h] = h_ref[h] * chunk_decay[:, h:h + 1] + states[e * hd:(e + 1) * hd]
    y = jnp.concatenate([yT_ref[i * l:(i + 1) * l, :].T for i in range(SSM_D_INNER // l)], axis=1)
    yg_ref[...] = _ssm_gate_norm(y, xs, z_ref[...], dskip_ref[...], normg_ref[...]).astype(yg_ref.dtype)

    @pl.when(c == pl.num_programs(1) - 1)
    def _():
        h_out_ref[0] = h_ref[...]


def _ssm_head_lanes(p):
    return jnp.pad(p.astype(jnp.float32), (0, SSM_DT_LANES - SSM_HEADS)).reshape(1, SSM_DT_LANES)


def _ssm_chunk_scan(b_, t_, xs, bm, cm, dt, z, a_log, d_skip, norm_g):
    l = SSM_CHUNK
    nc = t_ // l
    rows = lambda w: pl.BlockSpec((l, w), lambda b, c: (b * nc + c, 0))
    vec = lambda w: pl.BlockSpec((1, w), lambda b, c: (0, 0))
    aneg = _ssm_head_lanes(-jnp.exp(a_log.astype(jnp.float32)))
    dskip = jnp.repeat(d_skip, SSM_HEAD_DIM).reshape(1, SSM_D_INNER)
    yg, h_new = pl.pallas_call(
        _ssm_chunk_kernel,
        grid=(b_, nc),
        in_specs=[rows(SSM_D_INNER), rows(SSM_BC_DIM), rows(SSM_BC_DIM), rows(SSM_DT_LANES), rows(SSM_D_INNER),
                  vec(SSM_DT_LANES), vec(SSM_D_INNER), vec(SSM_D_INNER)],
        out_specs=[rows(SSM_D_INNER),
                   pl.BlockSpec((1, SSM_HEADS, SSM_HEAD_DIM, SSM_STATE), lambda b, c: (b, 0, 0, 0))],
        out_shape=[jax.ShapeDtypeStruct((b_ * t_, SSM_D_INNER), MXU_DTYPE),
                   jax.ShapeDtypeStruct((b_, SSM_HEADS, SSM_HEAD_DIM, SSM_STATE), jnp.float32)],
        scratch_shapes=[pltpu.VMEM((SSM_HEADS, SSM_HEAD_DIM, SSM_STATE), jnp.float32),
                        pltpu.VMEM((SSM_D_INNER, l), jnp.float32),
                        pltpu.VMEM((SSM_HPG * SSM_HEAD_DIM, l), MXU_DTYPE)],
        compiler_params=pltpu.CompilerParams(
            dimension_semantics=("parallel", "arbitrary"),
            vmem_limit_bytes=V7X_VMEM_LIMIT_BYTES),
        name="ssm_chunk_scan",
    )(xs, bm, cm, dt, z, aneg, dskip, norm_g.reshape(1, SSM_D_INNER))
    return yg, h_new


def _ssm_step_kernel(h0_ref, xs_ref, dt_ref, an_ref, bm_ref, cm_ref, y_ref, h_ref):
    h0 = h0_ref[0]
    dt = dt_ref[0]
    decay = jnp.exp(dt * an_ref[...])
    xdt = xs_ref[0] * dt
    bm = bm_ref[0].astype(jnp.float32)
    cm = cm_ref[0].astype(jnp.float32)
    h_ref[0] = h0 * decay + xdt * bm
    cb = jnp.sum(cm * bm, axis=-1, keepdims=True)
    y_ref[0] = cb * xdt + jnp.sum(cm * h0, axis=-1, keepdims=True) * decay


def _ssm_step(state, xs, bm, cm, dt, a_log):
    b_ = state.shape[0]
    per_head = lambda u: jnp.repeat(u.reshape(b_, SSM_GROUPS, 1, SSM_STATE), SSM_HPG, axis=1)
    xs4 = xs.reshape(b_, SSM_HEADS, SSM_HEAD_DIM, 1)
    dt4 = dt[:, :SSM_HEADS].reshape(b_, SSM_HEADS, 1, 1)
    an = (-jnp.exp(a_log.astype(jnp.float32))).reshape(SSM_HEADS, 1, 1)
    blk = lambda a: pl.BlockSpec((1,) + a.shape[1:], lambda b: (b, 0, 0, 0))
    args = [state.astype(jnp.float32), xs4, dt4, an, per_head(bm), per_head(cm)]
    y4, h_new = pl.pallas_call(
        _ssm_step_kernel,
        grid=(b_,),
        in_specs=[blk(args[0]), blk(xs4), blk(dt4), pl.BlockSpec(an.shape, lambda b: (0, 0, 0)),
                  blk(args[4]), blk(args[5])],
        out_specs=[blk(xs4), blk(args[0])],
        out_shape=[jax.ShapeDtypeStruct(xs4.shape, jnp.float32), jax.ShapeDtypeStruct(state.shape, jnp.float32)],
        compiler_params=pltpu.CompilerParams(
            dimension_semantics=("parallel",),
            vmem_limit_bytes=V7X_VMEM_LIMIT_BYTES),
        name="ssm_step",
    )(*args)
    return y4.reshape(b_, SSM_D_INNER), h_new


def _ssm_gate_norm_kernel(y_ref, xs_ref, z_ref, dskip_ref, normg_ref, o_ref):
    o_ref[...] = _ssm_gate_norm(y_ref[...], xs_ref[...], z_ref[...], dskip_ref[...], normg_ref[...]).astype(o_ref.dtype)


def _ssm_gate_norm_rows(y, xs, z, d_skip, norm_g):
    full = lambda a: pl.BlockSpec(a.shape, lambda i: (0,) * a.ndim)
    args = [y, xs, z, jnp.repeat(d_skip, SSM_HEAD_DIM).reshape(1, SSM_D_INNER), norm_g.reshape(1, SSM_D_INNER)]
    return pl.pallas_call(
        _ssm_gate_norm_kernel,
        grid=(1,),
        in_specs=[full(a) for a in args],
        out_specs=full(y),
        out_shape=jax.ShapeDtypeStruct(y.shape, MXU_DTYPE),
        name="ssm_gate_norm",
    )(*args)


def _mamba2_mixer_pallas(x3d, conv_state, ssm_state, w_in, conv_w, conv_b, dt_bias, a_log, d_skip, norm_g):
    b_, t_, _ = x3d.shape
    z, xs, bm, cm, dt, conv_new = _ssm_project(x3d, conv_state, w_in, conv_w, conv_b, dt_bias)
    if t_ == 1:
        y, h_new = _ssm_step(ssm_state, xs, bm, cm, dt, a_log)
        yg = _ssm_gate_norm_rows(y, xs, z, d_skip, norm_g)
    else:
        yg, h_new = _ssm_chunk_scan(b_, t_, xs, bm, cm, dt, z, a_log, d_skip, norm_g)
    return yg, conv_new, h_new.astype(ssm_state.dtype)


PAGES_PER_STEP = 8


def _sortable_key(score):
    bits = pltpu.bitcast(score, jnp.int32)
    return jnp.where(bits >= 0, bits, bits ^ jnp.int32(0x7FFFFFFF))


def _decode_score_kernel(pt_ref, iq_ref, iw_ref, ikn_ref, *rest):
    idx_refs, (key_ref, knew_ref) = rest[:-2], rest[-2:]
    iq, iw = iq_ref[0], iw_ref[0]
    weigh = lambda sc: jnp.sum(iw * jnp.maximum(sc, 0.0), axis=0, keepdims=True)
    ik_t = jnp.concatenate([r[0] for r in idx_refs], axis=1).astype(MXU_DTYPE)
    key_ref[0] = _sortable_key(weigh(jnp.dot(iq, ik_t, preferred_element_type=jnp.float32)))

    @pl.when(pl.program_id(1) == 0)
    def _():
        sc_new = jnp.sum(iq.astype(jnp.float32) * ikn_ref[0].astype(jnp.float32), axis=1, keepdims=True)
        knew_ref[0] = jnp.broadcast_to(_sortable_key(weigh(sc_new)), knew_ref.shape[1:])


def _decode_select_kernel(keys_ref, knew_ref, thr_ref, last_ref, *, topk, col_bits):
    keys = keys_ref[...]
    key_new = knew_ref[:, 0:1]
    past = keys.shape[1]
    col = lax.broadcasted_iota(jnp.int32, keys.shape, 1)

    def count(pred_past, pred_new):
        hit = jnp.where(pred_past(keys, col), 1.0, 0.0)
        cnt = _tree_sum([hit[:, l * 128:(l + 1) * 128] for l in range(past // 128)])
        return jnp.sum(cnt, axis=1, keepdims=True) + jnp.where(pred_new(key_new), 1.0, 0.0)

    def at_least(cand):
        return count(lambda k, c: k >= cand, lambda k: k >= cand) >= float(topk)

    def two_bits(i, thr):
        hi = lax.shift_left(jnp.int32(1), jnp.int32(31) - 2 * i)
        lo = lax.shift_left(jnp.int32(1), jnp.int32(30) - 2 * i)
        c1, c2, c3 = thr ^ lo, thr ^ hi, thr ^ hi ^ lo
        return jnp.where(at_least(c3), c3, jnp.where(at_least(c2), c2, jnp.where(at_least(c1), c1, thr)))

    thr = lax.fori_loop(0, 16, two_bits, jnp.full(key_new.shape, INT32_MIN, jnp.int32))
    need = float(topk) - count(lambda k, c: k > thr, lambda k: k > thr)

    def col_body(i, last):
        cand = last | lax.shift_left(jnp.int32(1), jnp.int32(col_bits - 1) - i)
        ties = count(lambda k, c: (k == thr) & (c < cand), lambda k: (k == thr) & (jnp.int32(past) < cand))
        return jnp.where(ties < need, cand, last)

    n_tied = count(lambda k, c: k == thr, lambda k: k == thr)
    excess = jnp.max(jnp.where(n_tied > need, 1.0, 0.0), axis=0, keepdims=True)
    last_tie = lax.cond(excess[0, 0] > 0.0,
                        lambda: lax.fori_loop(0, col_bits, col_body, jnp.zeros(key_new.shape, jnp.int32)),
                        lambda: jnp.full(key_new.shape, 2 ** col_bits - 1, jnp.int32))
    thr_ref[...] = jnp.broadcast_to(thr, thr_ref.shape)
    last_ref[...] = jnp.broadcast_to(last_tie, last_ref.shape)


def _decode_attend_kernel(pt_ref, q_ref, kn_ref, vn_ref, key_ref, knew_ref, thr_ref, last_ref, *rest,
                          n_steps, pages):
    k_refs, v_refs = rest[:pages], rest[pages:2 * pages]
    o_ref, m_ref, l_ref, acc_ref = rest[2 * pages:]
    s = pl.program_id(1)
    nk = key_ref.shape[2]
    nt = (((1,), (1,)), ((), ()))
    thr, last_tie = thr_ref[0, :, 0:1], last_ref[0, :, 0:1]
    keep_mask = lambda key, col: (key > thr) | ((key == thr) & (col <= last_tie))
    gsz = ATT_HEADS // ATT_KV_HEADS
    q = q_ref[0]
    q_wide = jnp.concatenate([q] * ATT_KV_HEADS, axis=1)
    head_i = lax.broadcasted_iota(jnp.int32, q_wide.shape, 0)
    col_i = lax.broadcasted_iota(jnp.int32, q_wide.shape, 1)
    own_group = (col_i // ATT_HEAD_DIM) == (head_i // gsz)
    q_blk = jnp.where(own_group, q_wide, jnp.zeros_like(q_wide))

    def online_update(logits, weighted_values):
        m_old = m_ref[...]
        m_new = jnp.maximum(m_old, jnp.max(logits, axis=1, keepdims=True))
        p = jnp.exp(logits - m_new)
        alpha = jnp.exp(m_old - m_new)
        l_ref[...] = alpha * l_ref[...] + jnp.sum(p, axis=1, keepdims=True)
        acc_ref[...] = alpha * acc_ref[...] + weighted_values(p.astype(MXU_DTYPE))
        m_ref[...] = m_new

    @pl.when(s == 0)
    def _init():
        m_ref[...] = jnp.full(m_ref.shape, MASK_NEG, jnp.float32)
        l_ref[...] = jnp.zeros(l_ref.shape, jnp.float32)
        acc_ref[...] = jnp.zeros(acc_ref.shape, jnp.float32)

    col = lax.broadcasted_iota(jnp.int32, (1, nk), 1) + s * nk
    bias = jnp.where(keep_mask(key_ref[0], col), 0.0, MASK_NEG)
    k_t = jnp.concatenate([r[0] for r in k_refs], axis=1).astype(MXU_DTYPE)
    v_t = jnp.concatenate([r[0] for r in v_refs], axis=1).astype(MXU_DTYPE)
    online_update(jnp.dot(q_blk, k_t, preferred_element_type=jnp.float32) + bias,
                  lambda p: lax.dot_general(p, v_t, nt, preferred_element_type=jnp.float32))

    @pl.when(s == n_steps - 1)
    def _finish():
        keep_new = keep_mask(knew_ref[0, :, 0:1], jnp.int32(n_steps * nk))
        logit = jnp.sum(q_blk.astype(jnp.float32) * kn_ref[0].astype(jnp.float32), axis=1, keepdims=True)
        v_row = vn_ref[0].astype(jnp.float32)
        online_update(logit + jnp.where(keep_new, 0.0, MASK_NEG), lambda p: p.astype(jnp.float32) * v_row)
        out = jnp.where(own_group, acc_ref[...] / l_ref[...], 0.0)
        o = out[:, 0:ATT_HEAD_DIM]
        for g in range(1, ATT_KV_HEADS):
            o = o + out[:, g * ATT_HEAD_DIM:(g + 1) * ATT_HEAD_DIM]
        o_ref[0] = o.astype(o_ref.dtype)


def _dsa_decode_split(q, iq, iw, ik_new, k_new, v_new, cache_k, cache_v, cache_idx_k, page_table):
    b_, n_pages = page_table.shape
    n_pool, page = cache_k.shape[0], cache_k.shape[1]
    pages = PAGES_PER_STEP
    n_steps = n_pages // pages
    nk = pages * page
    past = n_pages * page
    ck = jnp.transpose(cache_k, (0, 2, 3, 1)).reshape(n_pool, ATT_KV_DIM, page)
    cv = jnp.transpose(cache_v, (0, 2, 3, 1)).reshape(n_pool, ATT_KV_DIM, page)
    cik = jnp.swapaxes(cache_idx_k, 1, 2)
    per_seq = lambda a: pl.BlockSpec((1,) + a.shape[1:], lambda b, s, pt: (b,) + (0,) * (a.ndim - 1))
    paged = lambda width, j: pl.BlockSpec((1, width, page), lambda b, s, pt: (pt[b, s * pages + j], 0, 0))
    key_blk = pl.BlockSpec((1, 1, nk), lambda b, s, pt: (b, 0, s))
    lane_blk = pl.BlockSpec((1, 1, 128), lambda b, s, pt: (b, 0, 0))
    params = pltpu.CompilerParams(dimension_semantics=("parallel", "arbitrary"),
                                  vmem_limit_bytes=V7X_VMEM_LIMIT_BYTES)

    score_in = [iq.reshape(b_, IDX_HEADS, IDX_DIM), iw[:, :IDX_HEADS].reshape(b_, IDX_HEADS, 1),
                ik_new.astype(MXU_DTYPE).reshape(b_, 1, IDX_DIM)]
    keys, key_new = pl.pallas_call(
        _decode_score_kernel,
        grid_spec=pltpu.PrefetchScalarGridSpec(
            num_scalar_prefetch=1, grid=(b_, n_steps),
            in_specs=[per_seq(a) for a in score_in] + [paged(IDX_DIM, j) for j in range(pages)],
            out_specs=[key_blk, lane_blk]),
        out_shape=[jax.ShapeDtypeStruct((b_, 1, past), jnp.int32), jax.ShapeDtypeStruct((b_, 1, 128), jnp.int32)],
        compiler_params=params,
        name="dsa_decode_score",
    )(page_table, *score_in, *([cik] * pages))

    whole = lambda shape: pl.BlockSpec(shape, lambda i: (0,) * len(shape))
    thr, last_tie = pl.pallas_call(
        functools.partial(_decode_select_kernel, topk=min(TOPK_MAX, (past + 1) // 4),
                          col_bits=max(1, past.bit_length())),
        grid=(1,),
        in_specs=[whole((b_, past)), whole((b_, 128))],
        out_specs=[whole((b_, 128)), whole((b_, 128))],
        out_shape=[jax.ShapeDtypeStruct((b_, 128), jnp.int32)] * 2,
        compiler_params=pltpu.CompilerParams(vmem_limit_bytes=V7X_VMEM_LIMIT_BYTES),
        name="dsa_decode_select",
    )(keys.reshape(b_, past), key_new.reshape(b_, 128))

    attend_in = [q.reshape(b_, ATT_HEADS, ATT_HEAD_DIM), k_new.astype(MXU_DTYPE).reshape(b_, 1, ATT_KV_DIM),
                 v_new.astype(MXU_DTYPE).reshape(b_, 1, ATT_KV_DIM)]
    o = pl.pallas_call(
        functools.partial(_decode_attend_kernel, n_steps=n_steps, pages=pages),
        grid_spec=pltpu.PrefetchScalarGridSpec(
            num_scalar_prefetch=1, grid=(b_, n_steps),
            in_specs=[per_seq(a) for a in attend_in] + [key_blk, lane_blk, lane_blk, lane_blk]
            + [paged(ATT_KV_DIM, j) for j in range(pages)] * 2,
            out_specs=pl.BlockSpec((1, ATT_HEADS, ATT_HEAD_DIM), lambda b, s, pt: (b, 0, 0)),
            scratch_shapes=[pltpu.VMEM((ATT_HEADS, 1), jnp.float32), pltpu.VMEM((ATT_HEADS, 1), jnp.float32),
                            pltpu.VMEM((ATT_HEADS, ATT_KV_DIM), jnp.float32)]),
        out_shape=jax.ShapeDtypeStruct((b_, ATT_HEADS, ATT_HEAD_DIM), MXU_DTYPE),
        compiler_params=params,
        name="dsa_decode_attend",
    )(page_table, *attend_in, keys, key_new, thr.reshape(b_, 1, 128), last_tie.reshape(b_, 1, 128),
      *([ck] * pages), *([cv] * pages))
    return o.reshape(b_, ATT_Q_DIM)


def _dsa_sample_pallas(x3d, cache_k, cache_v, cache_idx_k, page_table, w_in, kn_g, kn_b):
    b_, t_, _ = x3d.shape
    past = page_table.shape[1] * cache_k.shape[1]
    pos = jnp.full((b_,), past, jnp.int32)
    q, iq, v, _, iw, kT, _, ikT, _ = _dsa_project(x3d.reshape(1, b_, D_MODEL), pos, w_in, kn_g, kn_b)
    k = _untranspose_groups(kT)[0]
    ik = _untranspose_groups(ikT)[0]
    o = _dsa_decode_split(q, iq, iw, ik, k, v, cache_k, cache_v, cache_idx_k, page_table)
    kv4 = lambda u: u.reshape(b_, t_, ATT_KV_HEADS, ATT_HEAD_DIM)
    return o, kv4(k), kv4(v), ik.reshape(b_, t_, IDX_DIM)


def kernel(x_prompt, x_sample, state_ssm_conv, state_ssm, cache_k, cache_v, cache_idx_k, state_rwkv_shift, state_rwkv_wkv, page_table, p_prompt, p_sample, ln_g, ln_b, ffn_w_up, ffn_w_down, ple_w_p, ple_w_g, ple_b_g, gm_w_in, gm_ln_g, gm_ln_b, gm_ws, gm_bs, gm_w_out, ssm_w_in, ssm_conv_w, ssm_conv_b, ssm_dt_bias, ssm_a_log, ssm_d, ssm_norm_g, ssm_w_out, att_w_in, att_kn_g, att_kn_b, att_w_out, rw_mu, rw_w_r, rw_w_k, rw_w_v, rw_w_o, rw_w0, rw_w1, rw_w2, rw_a0, rw_a1, rw_a2, rw_g1, rw_g2, rw_k_k, rw_k_a, rw_r_k, rw_gn_g, rw_gn_b):
    bp, tp, _ = x_prompt.shape
    bs_, ts, _ = x_sample.shape
    bf = lambda w: w.astype(jnp.bfloat16)
    w_up_bf, w_down_bf = bf(ffn_w_up), bf(ffn_w_down)
    ple_wp_bf, ple_wg_bf = bf(ple_w_p), bf(ple_w_g)
    pp3 = p_prompt.reshape(DEPTH, bp * tp, PLE_DIM)
    ps3 = p_sample.reshape(DEPTH, bs_ * ts, PLE_DIM)

    yp = x_prompt.reshape(bp * tp, D_MODEL)
    ys = x_sample.reshape(bs_ * ts, D_MODEL)
    r3p = lambda t: t.reshape(bp, tp, -1)
    r3s = lambda t: t.reshape(bs_, ts, -1)
    f2 = lambda t: t.reshape(-1, t.shape[-1])

    for i in range(DEPTH):
        yp = _ffn_sub(yp, w_up_bf, w_down_bf, i, 0, ln_g[i, 0], ln_b[i, 0])
        ys = _ffn_sub(ys, w_up_bf, w_down_bf, i, 0, ln_g[i, 0], ln_b[i, 0])
        m = i % N_MIXERS
        if m == 0:
            gm_args = (gm_w_in, gm_ln_g, gm_ln_b, gm_ws, gm_bs, gm_w_out, ln_g[i, 1], ln_b[i, 1])
            yp, = _gmlp_block(yp, tp, *gm_args, False)
            ys, gm_v_s = _gmlp_block(ys, ts, *gm_args, True)
            gm_v_s = r3s(gm_v_s)
        elif m == 1:
            ssm_args = (ssm_w_in, ssm_conv_w, ssm_conv_b, ssm_dt_bias, ssm_a_log, ssm_d, ssm_norm_g)
            hp, conv_p, ssm_p = _mamba2_mixer_pallas(
                r3p(yp), jnp.zeros((bp, SSM_CONV - 1, SSM_CONV_DIM), yp.dtype),
                jnp.zeros((bp, SSM_HEADS, SSM_HEAD_DIM, SSM_STATE), yp.dtype), *ssm_args)
            hs, conv_s, ssm_s = _mamba2_mixer_pallas(r3s(ys), state_ssm_conv, state_ssm, *ssm_args)
            w_out = bf(ssm_w_out)
        elif m == 2:
            hp, k_p, v_p, ik_p = _dsa_prompt_pallas(r3p(yp), att_w_in, att_kn_g, att_kn_b)
            hs, k_s, v_s, ik_s = _dsa_sample_pallas(r3s(ys), cache_k, cache_v, cache_idx_k, page_table,
                                                    att_w_in, att_kn_g, att_kn_b)
            w_out = bf(att_w_out)
        else:
            rw_args = (rw_mu, rw_w_r, rw_w_k, rw_w_v, rw_w0, rw_w1, rw_w2, rw_a0, rw_a1, rw_a2,
                       rw_g1, rw_g2, rw_k_k, rw_k_a, rw_r_k, rw_gn_g, rw_gn_b)
            hp, gate_p, sh_p, wkv_p = _rwkv7_mixer_pallas(
                r3p(yp), jnp.zeros((bp, D_MODEL), yp.dtype),
                jnp.zeros((bp, RW_HEADS, RW_HEAD, RW_HEAD), yp.dtype), *rw_args)
            hs, gate_s, sh_s, wkv_s = _rwkv7_mixer_pallas(r3s(ys), state_rwkv_shift, state_rwkv_wkv, *rw_args)
            w_out = bf(rw_w_o)
        if m == 3:
            yp = _proj_gate_post_norm(yp, hp, gate_p, w_out, ln_g[i, 1], ln_b[i, 1])
            ys = _proj_gate_post_norm(ys, hs, gate_s, w_out, ln_g[i, 1], ln_b[i, 1])
        elif m != 0:
            yp = _proj_post_norm(yp, f2(hp), w_out, ln_g[i, 1], ln_b[i, 1])
            ys = _proj_post_norm(ys, f2(hs), w_out, ln_g[i, 1], ln_b[i, 1])
        yp = _ffn_sub(yp, w_up_bf, w_down_bf, i, 1, ln_g[i, 2], ln_b[i, 2], (pp3, ple_wp_bf, ple_wg_bf, ple_b_g))
        ys = _ffn_sub(ys, w_up_bf, w_down_bf, i, 1, ln_g[i, 2], ln_b[i, 2], (ps3, ple_wp_bf, ple_wg_bf, ple_b_g))

    return (r3p(yp), r3s(ys), gm_v_s, conv_p, ssm_p, conv_s, ssm_s, k_p, v_p, ik_p, k_s, v_s, ik_s,
            sh_p, wkv_p, sh_s, wkv_s)
```

```python
import functools

import jax
import jax.numpy as jnp
from jax import lax
from jax.experimental import pallas as pl
from jax.experimental.pallas import tpu as pltpu

D_MODEL = 1024
DEPTH = 4
N_MIXERS = 4
PLE_DIM = 256
D_FF = 2816
ALPHA = (2 * DEPTH) ** 0.25
LN_EPS = 1e-5

CHUNK = 128
GM_WIDTH = 2 * D_MODEL
GM_GROUPS = 8
GM_GROUP_DIM = GM_WIDTH // GM_GROUPS

SSM_D_INNER = 2 * D_MODEL
SSM_HEAD_DIM = 64
SSM_HEADS = SSM_D_INNER // SSM_HEAD_DIM
SSM_GROUPS = 4
SSM_HPG = SSM_HEADS // SSM_GROUPS
SSM_STATE = 128
SSM_CONV = 4
SSM_CONV_DIM = SSM_D_INNER + 2 * SSM_GROUPS * SSM_STATE
SSM_CHUNK = 128

ATT_HEADS = 16
ATT_KV_HEADS = 4
ATT_HEAD_DIM = D_MODEL // ATT_HEADS
ROPE_DIM = ATT_HEAD_DIM // 4
ROPE_THETA = 500000.0
IDX_HEADS = 8
IDX_DIM = 64
IDX_ROPE_DIM = IDX_DIM // 4
TOPK_MAX = 256
Q_BLOCK = 128
ATT_Q_DIM = ATT_HEADS * ATT_HEAD_DIM
ATT_KV_DIM = ATT_KV_HEADS * ATT_HEAD_DIM
ATT_IN_SPLITS = (ATT_Q_DIM, ATT_Q_DIM + ATT_KV_DIM, ATT_Q_DIM + 2 * ATT_KV_DIM,
                 ATT_Q_DIM + 2 * ATT_KV_DIM + IDX_HEADS * IDX_DIM,
                 ATT_Q_DIM + 2 * ATT_KV_DIM + IDX_HEADS * IDX_DIM + IDX_DIM)

RW_HEAD = 64
RW_HEADS = D_MODEL // RW_HEAD
RW_GN_EPS = 64e-5

V7X_VMEM_LIMIT_BYTES = 52 * 1024 * 1024
FF_TILE = D_FF // 2
ROW_TILE = 512


def _row_tile(m):
    return ROW_TILE if m % ROW_TILE == 0 else m


def _ln_rows(y, g, b):
    mu = jnp.mean(y, axis=-1, keepdims=True)
    yc = y - mu
    var = jnp.mean(yc * yc, axis=-1, keepdims=True)
    return yc * lax.rsqrt(var + LN_EPS) * g + b


FFN_ROW_TILE = 1024
FFN_ROW_PARTS = 2


def _ffn_kernel(x_ref, wu_ref, wd_ref, g_ref, b_ref, *rest, parts, with_ple):
    o_ref = rest[-1]
    rows = x_ref.shape[0] // parts
    for p in range(parts):
        x = x_ref[p * rows:(p + 1) * rows, :]
        xb = x.astype(MXU_DTYPE)
        acc = None
        for f in range(D_FF // FF_TILE):
            cols = slice(f * FF_TILE, (f + 1) * FF_TILE)
            gate = jnp.dot(xb, wu_ref[:, cols], preferred_element_type=jnp.float32)
            lin = jnp.dot(xb, wu_ref[:, D_FF + f * FF_TILE:D_FF + (f + 1) * FF_TILE],
                          preferred_element_type=jnp.float32)
            h = (gate * jax.nn.sigmoid(gate) * lin).astype(MXU_DTYPE)
            part = jnp.dot(h, wd_ref[cols, :], preferred_element_type=jnp.float32)
            acc = part if acc is None else acc + part
        y = _ln_rows(ALPHA * x + 0.5 * acc, g_ref[...], b_ref[...])
        if with_ple:
            p_ref, wp_ref, wg_ref, bg_ref = rest[:4]
            gate = jax.nn.sigmoid(
                jnp.dot(y.astype(MXU_DTYPE), wg_ref[...], preferred_element_type=jnp.float32) + bg_ref[...])
            emb = jnp.dot(p_ref[p * rows:(p + 1) * rows, :].astype(MXU_DTYPE), wp_ref[...],
                          preferred_element_type=jnp.float32)
            y = y + gate * emb
        o_ref[p * rows:(p + 1) * rows, :] = y


def _ffn_sub(x2d, w_up, w_down, layer, half, g, b, ple=None):
    m = x2d.shape[0]
    tm = FFN_ROW_TILE if m % FFN_ROW_TILE == 0 else m
    parts = FFN_ROW_PARTS if tm == FFN_ROW_TILE else 1
    resident = dict(pipeline_mode=pl.Buffered(1))
    vec = pl.BlockSpec((1, D_MODEL), lambda i: (0, 0))
    in_specs = [
        pl.BlockSpec((tm, D_MODEL), lambda i: (i, 0)),
        pl.BlockSpec((None, None, D_MODEL, 2 * D_FF), lambda i: (layer, half, 0, 0), **resident),
        pl.BlockSpec((None, None, D_FF, D_MODEL), lambda i: (layer, half, 0, 0), **resident),
        vec, vec]
    args = [x2d, w_up, w_down, g.reshape(1, D_MODEL), b.reshape(1, D_MODEL)]
    if ple is not None:
        p3d, w_p, w_g, b_g = ple
        in_specs += [pl.BlockSpec((None, tm, PLE_DIM), lambda i: (layer, i, 0)),
                     pl.BlockSpec((None, PLE_DIM, D_MODEL), lambda i: (layer, 0, 0), **resident),
                     pl.BlockSpec((None, D_MODEL, D_MODEL), lambda i: (layer, 0, 0), **resident),
                     pl.BlockSpec((None, 1, D_MODEL), lambda i: (layer, 0, 0))]
        args += [p3d, w_p, w_g, b_g.reshape(DEPTH, 1, D_MODEL)]
    return pl.pallas_call(
        functools.partial(_ffn_kernel, parts=parts, with_ple=ple is not None),
        grid=(m // tm,),
        in_specs=in_specs,
        out_specs=pl.BlockSpec((tm, D_MODEL), lambda i: (i, 0)),
        out_shape=jax.ShapeDtypeStruct((m, D_MODEL), jnp.float32),
        compiler_params=pltpu.CompilerParams(
            dimension_semantics=("parallel",),
            vmem_limit_bytes=V7X_VMEM_LIMIT_BYTES),
        name="ffn_ple" if ple is not None else "ffn_sub",
    )(*args)


def _proj_ln_kernel(x_ref, h_ref, w_ref, g_ref, b_ref, o_ref):
    y = ALPHA * x_ref[...] + jnp.dot(h_ref[...].astype(jnp.bfloat16), w_ref[...],
                                     preferred_element_type=jnp.float32)
    o_ref[...] = _ln_rows(y, g_ref[...], b_ref[...])


def _proj_post_norm(x2d, h2d, w_out, g, b):
    m = x2d.shape[0]
    k = h2d.shape[1]
    tm = _row_tile(m)
    return pl.pallas_call(
        _proj_ln_kernel,
        grid=(m // tm,),
        in_specs=[
            pl.BlockSpec((tm, D_MODEL), lambda i: (i, 0)),
            pl.BlockSpec((tm, k), lambda i: (i, 0)),
            pl.BlockSpec((k, D_MODEL), lambda i: (0, 0)),
            pl.BlockSpec((1, D_MODEL), lambda i: (0, 0)),
            pl.BlockSpec((1, D_MODEL), lambda i: (0, 0)),
        ],
        out_specs=pl.BlockSpec((tm, D_MODEL), lambda i: (i, 0)),
        out_shape=jax.ShapeDtypeStruct((m, D_MODEL), jnp.float32),
        compiler_params=pltpu.CompilerParams(
            dimension_semantics=("parallel",),
            vmem_limit_bytes=V7X_VMEM_LIMIT_BYTES),
        name="proj_post_norm",
    )(x2d, h2d, w_out, g.reshape(1, D_MODEL), b.reshape(1, D_MODEL))


MXU_DTYPE = jnp.bfloat16
KEY_GROUP = 512
INT32_MIN = -2 ** 31
MASK_NEG = -1e30


def _rope_lane_tables(pos, rot_dim, head_dim):
    half = rot_dim // 2
    inv = ROPE_THETA ** (-jnp.arange(half, dtype=jnp.float32) / half)
    ang = pos.astype(jnp.float32)[:, None] * inv[None, :]
    cos, sin = jnp.cos(ang), jnp.sin(ang)
    n = pos.shape[0]
    rest = head_dim - rot_dim
    c = jnp.concatenate([cos, cos, jnp.ones((n, rest), jnp.float32)], axis=1)
    s1 = jnp.concatenate([-sin, jnp.zeros((n, half + rest), jnp.float32)], axis=1)
    s2 = jnp.concatenate([jnp.zeros((n, half), jnp.float32), sin, jnp.zeros((n, rest), jnp.float32)], axis=1)
    reps = 128 // head_dim
    tile = lambda t: jnp.tile(t, (1, reps))
    return tile(c), tile(s1), tile(s2), cos.T, sin.T


def _rope_lanes(t, c, s1, s2, half):
    n = t.shape[1]
    reps = n // 128
    tl = lambda a: jnp.concatenate([a] * reps, axis=1)
    return t * tl(c) + pltpu.roll(t, n - half, 1) * tl(s1) + pltpu.roll(t, half, 1) * tl(s2)


def _rope_rows(t, cT, sT, head_dim, half):
    pieces = []
    for h in range(t.shape[0] // head_dim):
        x1 = t[h * head_dim:h * head_dim + half]
        x2 = t[h * head_dim + half:h * head_dim + 2 * half]
        pieces += [x1 * cT - x2 * sT, x2 * cT + x1 * sT, t[h * head_dim + 2 * half:(h + 1) * head_dim]]
    return jnp.concatenate(pieces, axis=0)


def _dsa_proj_kernel(x_ref, wq_ref, wiq_ref, wv_ref, wvx_ref, wiw_ref, wkT_ref, wikT_ref,
                     c_ref, s1_ref, s2_ref, cT_ref, sT_ref, kng_ref, knb_ref, one_ref,
                     q_ref, iq_ref, v_ref, vx_ref, iw_ref, kT_ref, kTb_ref, ikT_ref, ikTb_ref):
    xb = x_ref[...].astype(MXU_DTYPE)
    c, s1, s2 = c_ref[...], s1_ref[...], s2_ref[...]
    cT, sT = cT_ref[...], sT_ref[...]
    dot = lambda a, b: jnp.dot(a, b, preferred_element_type=jnp.float32)
    dot_t = lambda w, a: lax.dot_general(w, a, (((1,), (1,)), ((), ())), preferred_element_type=jnp.float32)

    q = _rope_lanes(dot(xb, wq_ref[...]), c, s1, s2, ROPE_DIM // 2)
    q_ref[...] = (q * (ATT_HEAD_DIM ** -0.5)).astype(q_ref.dtype)
    iq = _rope_lanes(dot(xb, wiq_ref[...]), c, s1, s2, IDX_ROPE_DIM // 2)
    iq_ref[...] = iq.astype(iq_ref.dtype)
    v_ref[...] = dot(xb, wv_ref[...])
    vx_ref[...] = (dot(xb, wvx_ref[...]) + one_ref[...]).astype(vx_ref.dtype)
    iw_ref[...] = dot(xb, wiw_ref[...]) * (IDX_HEADS ** -0.5 * IDX_DIM ** -0.5)

    kT = _rope_rows(dot_t(wkT_ref[...], xb), cT, sT, ATT_HEAD_DIM, ROPE_DIM // 2)
    kT_ref[0, 0] = kT
    kTb_ref[0, 0] = kT.astype(kTb_ref.dtype)
    ikT = dot_t(wikT_ref[...], xb)
    mu = jnp.mean(ikT, axis=0, keepdims=True)
    ikc = ikT - mu
    var = jnp.mean(ikc * ikc, axis=0, keepdims=True)
    ikT = ikc * lax.rsqrt(var + LN_EPS) * kng_ref[...] + knb_ref[...]
    ikT = _rope_rows(ikT, cT, sT, IDX_DIM, IDX_ROPE_DIM // 2)
    ikT_ref[0, 0] = ikT
    ikTb_ref[0, 0] = ikT.astype(ikTb_ref.dtype)


def _dsa_project(x3d, pos, w_in, kn_g, kn_b):
    b_, t_, _ = x3d.shape
    tk = KEY_GROUP if t_ % KEY_GROUP == 0 else t_
    ng = t_ // tk
    m = b_ * t_
    w_q, w_k, w_v, w_iq, w_ik, w_iw = jnp.split(w_in, list(ATT_IN_SPLITS), axis=1)
    cast = lambda w: w.astype(MXU_DTYPE)
    w_vx = jnp.pad(w_v.reshape(D_MODEL, ATT_KV_HEADS, ATT_HEAD_DIM),
                   ((0, 0), (0, 0), (0, 128 - ATT_HEAD_DIM))).reshape(D_MODEL, ATT_KV_HEADS * 128)
    one_col = jnp.tile((jnp.arange(128) == ATT_HEAD_DIM).astype(jnp.float32), ATT_KV_HEADS)[None, :]
    w_iw_pad = jnp.pad(w_iw, ((0, 0), (0, 128 - IDX_HEADS)))
    c, s1, s2, cT, sT = _rope_lane_tables(pos, ROPE_DIM, ATT_HEAD_DIM)
    full = lambda shape: pl.BlockSpec(shape, lambda b, i: (0,) * len(shape))
    rows = lambda n: pl.BlockSpec((tk, n), lambda b, i: (b * ng + i, 0))
    ptab = lambda n: pl.BlockSpec((tk, n), lambda b, i: (i, 0))
    grp = lambda n: pl.BlockSpec((1, 1, n, tk), lambda b, i: (b, i, 0, 0))
    sds = jax.ShapeDtypeStruct
    return pl.pallas_call(
        _dsa_proj_kernel,
        grid=(b_, ng),
        in_specs=[rows(D_MODEL), full((D_MODEL, ATT_Q_DIM)), full((D_MODEL, IDX_HEADS * IDX_DIM)),
                  full((D_MODEL, ATT_KV_DIM)), full((D_MODEL, ATT_KV_HEADS * 128)), full((D_MODEL, 128)),
                  full((ATT_KV_DIM, D_MODEL)), full((IDX_DIM, D_MODEL)),
                  ptab(128), ptab(128), ptab(128),
                  pl.BlockSpec((ROPE_DIM // 2, tk), lambda b, i: (0, i)),
                  pl.BlockSpec((ROPE_DIM // 2, tk), lambda b, i: (0, i)),
                  full((IDX_DIM, 1)), full((IDX_DIM, 1)), full((1, ATT_KV_HEADS * 128))],
        out_specs=[rows(ATT_Q_DIM), rows(IDX_HEADS * IDX_DIM), rows(ATT_KV_DIM), rows(ATT_KV_HEADS * 128),
                   rows(128), grp(ATT_KV_DIM), grp(ATT_KV_DIM), grp(IDX_DIM), grp(IDX_DIM)],
        out_shape=[sds((m, ATT_Q_DIM), MXU_DTYPE), sds((m, IDX_HEADS * IDX_DIM), MXU_DTYPE),
                   sds((m, ATT_KV_DIM), jnp.float32), sds((m, ATT_KV_HEADS * 128), MXU_DTYPE),
                   sds((m, 128), jnp.float32),
                   sds((b_, ng, ATT_KV_DIM, tk), jnp.float32), sds((b_, ng, ATT_KV_DIM, tk), MXU_DTYPE),
                   sds((b_, ng, IDX_DIM, tk), jnp.float32), sds((b_, ng, IDX_DIM, tk), MXU_DTYPE)],
        compiler_params=pltpu.CompilerParams(
            dimension_semantics=("parallel", "parallel"),
            vmem_limit_bytes=V7X_VMEM_LIMIT_BYTES),
        name="dsa_project",
    )(x3d.reshape(m, D_MODEL), cast(w_q), cast(w_iq), cast(w_v), cast(w_vx), cast(w_iw_pad),
      cast(w_k.T), cast(w_ik.T), c, s1, s2, cT, sT, kn_g.reshape(IDX_DIM, 1), kn_b.reshape(IDX_DIM, 1), one_col)


def _untranspose_groups(tg):
    b_, g_, r_, tk = tg.shape
    return jnp.transpose(tg, (0, 1, 3, 2)).reshape(b_, g_ * tk, r_)


def _dsa_proj_q_lanes_kernel(x_ref, wqT_ref, wiqT_ref, wiwT_ref, wk_ref, wv_ref, wvxT_ref, wik_ref,
                             c_ref, s1_ref, s2_ref, cT_ref, sT_ref, kng_ref, knb_ref, onerow_ref,
                             qT_ref, iqT_ref, iwT_ref, k_ref, khd_ref, v_ref, vxT_ref, ik_ref, ikb_ref):
    xb = x_ref[...].astype(MXU_DTYPE)
    tm = xb.shape[0]
    c, s1, s2 = c_ref[...], s1_ref[...], s2_ref[...]
    cT, sT = cT_ref[...], sT_ref[...]
    dot = lambda a, b: jnp.dot(a, b, preferred_element_type=jnp.float32)
    dot_t = lambda w, a: lax.dot_general(w, a, (((1,), (1,)), ((), ())), preferred_element_type=jnp.float32)

    qT = _rope_rows(dot_t(wqT_ref[...], xb), cT, sT, ATT_HEAD_DIM, ROPE_DIM // 2) * (ATT_HEAD_DIM ** -0.5)
    iqT = _rope_rows(dot_t(wiqT_ref[...], xb), cT, sT, IDX_DIM, IDX_ROPE_DIM // 2)
    iwT = dot_t(wiwT_ref[...], xb) * (IDX_HEADS ** -0.5 * IDX_DIM ** -0.5)
    for t in range(tm // Q_BLOCK):
        lanes = slice(t * Q_BLOCK, (t + 1) * Q_BLOCK)
        qT_ref[0, t] = qT[:, lanes].astype(qT_ref.dtype)
        iqT_ref[0, t] = iqT[:, lanes].astype(iqT_ref.dtype)
        iwT_ref[0, t] = iwT[:, lanes]

    k = _rope_lanes(dot(xb, wk_ref[...]), c, s1, s2, ROPE_DIM // 2)
    k_ref[...] = k
    for g in range(ATT_KV_HEADS):
        khd_ref[g] = k[:, g * ATT_HEAD_DIM:(g + 1) * ATT_HEAD_DIM].astype(khd_ref.dtype)
    v_ref[...] = dot(xb, wv_ref[...])
    vxT_ref[0, 0] = (dot_t(wvxT_ref[...], xb) + onerow_ref[...]).astype(vxT_ref.dtype)

    ik = dot(xb, wik_ref[...])
    real = lax.broadcasted_iota(jnp.int32, ik.shape, 1) < IDX_DIM
    mu = jnp.sum(ik, axis=-1, keepdims=True) * (1.0 / IDX_DIM)
    ikc = jnp.where(real, ik - mu, 0.0)
    var = jnp.sum(ikc * ikc, axis=-1, keepdims=True) * (1.0 / IDX_DIM)
    ikn = _rope_lanes(ikc * lax.rsqrt(var + LN_EPS) * kng_ref[...] + knb_ref[...], c, s1, s2, IDX_ROPE_DIM // 2)
    ik_ref[...] = ikn[:, :IDX_DIM]
    ikb_ref[...] = ikn[:, :IDX_DIM].astype(ikb_ref.dtype)


def _dsa_project_q_lanes(x3d, pos, w_in, kn_g, kn_b):
    b_, t_, _ = x3d.shape
    tk = KEY_GROUP
    ng = t_ // tk
    nq = tk // Q_BLOCK
    m = b_ * t_
    w_q, w_k, w_v, w_iq, w_ik, w_iw = jnp.split(w_in, list(ATT_IN_SPLITS), axis=1)
    cast = lambda w: w.astype(MXU_DTYPE)
    w_vxT = jnp.pad(w_v.T.reshape(ATT_KV_HEADS, ATT_HEAD_DIM, D_MODEL),
                    ((0, 0), (0, 128 - ATT_HEAD_DIM), (0, 0))).reshape(ATT_KV_HEADS * 128, D_MODEL)
    one_row = jnp.tile((jnp.arange(128) == ATT_HEAD_DIM).astype(jnp.float32), ATT_KV_HEADS)[:, None]
    pad_lanes = lambda a: jnp.pad(a, ((0, 0), (0, 128 - a.shape[1])))
    c, s1, s2, cT, sT = _rope_lane_tables(pos, ROPE_DIM, ATT_HEAD_DIM)
    full = lambda shape: pl.BlockSpec(shape, lambda b, i: (0,) * len(shape))
    rows = lambda n: pl.BlockSpec((tk, n), lambda b, i: (b * ng + i, 0))
    ptab = lambda n: pl.BlockSpec((tk, n), lambda b, i: (i, 0))
    qtile = lambda n: pl.BlockSpec((1, nq, n, Q_BLOCK), lambda b, i: (b, i, 0, 0))
    sds = jax.ShapeDtypeStruct
    return pl.pallas_call(
        _dsa_proj_q_lanes_kernel,
        grid=(b_, ng),
        in_specs=[rows(D_MODEL), full((ATT_Q_DIM, D_MODEL)), full((IDX_HEADS * IDX_DIM, D_MODEL)),
                  full((IDX_HEADS, D_MODEL)), full((D_MODEL, ATT_KV_DIM)), full((D_MODEL, ATT_KV_DIM)),
                  full((ATT_KV_HEADS * 128, D_MODEL)), full((D_MODEL, 128)),
                  ptab(128), ptab(128), ptab(128),
                  pl.BlockSpec((ROPE_DIM // 2, tk), lambda b, i: (0, i)),
                  pl.BlockSpec((ROPE_DIM // 2, tk), lambda b, i: (0, i)),
                  full((1, 128)), full((1, 128)), full((ATT_KV_HEADS * 128, 1))],
        out_specs=[qtile(ATT_Q_DIM), qtile(IDX_HEADS * IDX_DIM), qtile(IDX_HEADS),
                   rows(ATT_KV_DIM), pl.BlockSpec((ATT_KV_HEADS, tk, ATT_HEAD_DIM), lambda b, i: (0, b * ng + i, 0)),
                   rows(ATT_KV_DIM), pl.BlockSpec((1, 1, ATT_KV_HEADS * 128, tk), lambda b, i: (b, i, 0, 0)),
                   rows(IDX_DIM), rows(IDX_DIM)],
        out_shape=[sds((b_, t_ // Q_BLOCK, ATT_Q_DIM, Q_BLOCK), MXU_DTYPE),
                   sds((b_, t_ // Q_BLOCK, IDX_HEADS * IDX_DIM, Q_BLOCK), MXU_DTYPE),
                   sds((b_, t_ // Q_BLOCK, IDX_HEADS, Q_BLOCK), jnp.float32),
                   sds((m, ATT_KV_DIM), jnp.float32), sds((ATT_KV_HEADS, m, ATT_HEAD_DIM), MXU_DTYPE),
                   sds((m, ATT_KV_DIM), jnp.float32), sds((b_, ng, ATT_KV_HEADS * 128, tk), MXU_DTYPE),
                   sds((m, IDX_DIM), jnp.float32), sds((m, IDX_DIM), MXU_DTYPE)],
        compiler_params=pltpu.CompilerParams(
            dimension_semantics=("parallel", "parallel"),
            vmem_limit_bytes=V7X_VMEM_LIMIT_BYTES),
        name="dsa_project_q_lanes",
    )(x3d.reshape(m, D_MODEL), cast(w_q.T), cast(w_iq.T), cast(w_iw.T), cast(w_k), cast(w_v), cast(w_vxT),
      cast(pad_lanes(w_ik)), c, s1, s2, cT, sT, pad_lanes(kn_g.reshape(1, IDX_DIM)),
      pad_lanes(kn_b.reshape(1, IDX_DIM)), one_row)


def _tree_sum(parts):
    while len(parts) > 1:
        parts = [parts[i] + parts[i + 1] for i in range(0, len(parts) - 1, 2)] + (
            [parts[-1]] if len(parts) % 2 else [])
    return parts[0]


def _dsa_attend_q_lanes_kernel(iqT_ref, iwT_ref, ik_ref, qT_ref, k_ref, vxT_ref, o_ref,
                               key_ref, bias_ref, m_ref, acc_ref, s_ref, *, topk, col_bits):
    j = pl.program_id(1)
    tk, tq = key_ref.shape[1], key_ref.shape[2]
    n_groups = (j * tq + tq + tk - 1) // tk
    qpos = j * tq + lax.broadcasted_iota(jnp.int32, (tk, tq), 1)
    kpos0 = lax.broadcasted_iota(jnp.int32, (tk, tq), 0)
    dot = lambda a, b: jnp.dot(a, b, preferred_element_type=jnp.float32)

    def score_body(g, carry):
        start = pl.multiple_of(g * tk, tk)
        w_iq = jnp.concatenate([iqT_ref[0, 0, h * IDX_DIM:(h + 1) * IDX_DIM, :] for h in range(IDX_HEADS)], axis=1)
        s_all = dot(ik_ref[0, pl.ds(start, tk), :], w_iq)
        sc = _tree_sum([iwT_ref[0, 0, h:h + 1, :] * jnp.maximum(s_all[:, h * tq:(h + 1) * tq], 0.0)
                        for h in range(IDX_HEADS)])
        key_ref[g] = jnp.where(kpos0 + g * tk <= qpos, _sortable_key(sc), jnp.int32(INT32_MIN))
        return carry

    lax.fori_loop(0, n_groups, score_body, 0)

    def count_keys(pred):
        def body(g, part):
            hit = jnp.where(pred(key_ref[g], kpos0 + g * tk), 1.0, 0.0)
            return part + _tree_sum([hit[r * SUBLANES:(r + 1) * SUBLANES] for r in range(tk // SUBLANES)])
        part = lax.fori_loop(0, n_groups, body, jnp.zeros((SUBLANES, tq), jnp.float32))
        return jnp.sum(part, axis=0, keepdims=True)

    def bit_body(i, thr):
        cand = thr ^ lax.shift_left(jnp.int32(1), jnp.int32(31) - i)
        return jnp.where(count_keys(lambda k, kp: k >= cand) >= float(topk), cand, thr)

    thr = lax.fori_loop(0, 32, bit_body, jnp.full((1, tq), INT32_MIN, jnp.int32))

    need = float(topk) - count_keys(lambda k, kp: k > thr)

    def pos_body(i, last):
        cand = last | lax.shift_left(jnp.int32(1), jnp.int32(col_bits - 1) - i)
        return jnp.where(count_keys(lambda k, kp: (k == thr) & (kp < cand)) < need, cand, last)

    n_tied = count_keys(lambda k, kp: k == thr)
    excess = jnp.max(jnp.where(n_tied > need, 1.0, 0.0), axis=1, keepdims=True)
    last_tie = lax.cond(excess[0, 0] > 0.0,
                        lambda: lax.fori_loop(0, col_bits, pos_body, jnp.zeros((1, tq), jnp.int32)),
                        lambda: jnp.full((1, tq), 2 ** col_bits - 1, jnp.int32))

    m_ref[...] = jnp.full(m_ref.shape, MASK_NEG, jnp.float32)
    acc_ref[...] = jnp.zeros(acc_ref.shape, jnp.float32)
    gsz = ATT_HEADS // ATT_KV_HEADS

    def attend_body(g, carry):
        start = pl.multiple_of(g * tk, tk)
        key = key_ref[g]
        kpos = kpos0 + g * tk
        keep = (key > thr) | ((key == thr) & (kpos <= last_tie))
        bias_ref[...] = jnp.where(keep & (kpos <= qpos), 0.0, MASK_NEG)
        for kv in range(ATT_KV_HEADS):
            w_q = jnp.concatenate([qT_ref[0, 0, (kv * gsz + i) * ATT_HEAD_DIM:(kv * gsz + i + 1) * ATT_HEAD_DIM, :]
                                   for i in range(gsz)], axis=1)
            s_ref[kv] = dot(k_ref[kv, pl.ds(start, tk), :], w_q)
        for kv in range(ATT_KV_HEADS):
            s = s_ref[kv] + jnp.concatenate([bias_ref[...]] * gsz, axis=1)
            m_old = m_ref[kv]
            m_new = jnp.maximum(m_old, jnp.max(s, axis=0, keepdims=True))
            p = jnp.exp(s - m_new).astype(vxT_ref.dtype)
            pv = dot(vxT_ref[0, g, kv * 128:(kv + 1) * 128, :], p)
            acc_ref[kv] = jnp.exp(m_old - m_new) * acc_ref[kv] + pv
            m_ref[kv] = m_new
        return carry

    lax.fori_loop(0, n_groups, attend_body, 0)

    for h in range(ATT_HEADS):
        a = acc_ref[h // gsz, :, (h % gsz) * tq:(h % gsz + 1) * tq]
        o = (a / a[ATT_HEAD_DIM:ATT_HEAD_DIM + 1, :]).T
        o_ref[:, h * ATT_HEAD_DIM:(h + 1) * ATT_HEAD_DIM] = o[:, :ATT_HEAD_DIM].astype(o_ref.dtype)


def _dsa_attend_q_lanes(b_, t_, qT, iqT, iwT, ikb, khd, vxT):
    ng, tk = vxT.shape[1], vxT.shape[3]
    tq = Q_BLOCK
    nq = t_ // tq
    qtile = lambda n: pl.BlockSpec((1, 1, n, tq), lambda b, j: (b, j, 0, 0))
    return pl.pallas_call(
        functools.partial(_dsa_attend_q_lanes_kernel, topk=min(TOPK_MAX, t_ // 4),
                          col_bits=max(1, (t_ - 1).bit_length())),
        grid=(b_, nq),
        in_specs=[qtile(IDX_HEADS * IDX_DIM), qtile(IDX_HEADS),
                  pl.BlockSpec((1, t_, IDX_DIM), lambda b, j: (b, 0, 0)),
                  qtile(ATT_Q_DIM),
                  pl.BlockSpec((ATT_KV_HEADS, t_, ATT_HEAD_DIM), lambda b, j: (0, b, 0)),
                  pl.BlockSpec((1, ng, ATT_KV_HEADS * 128, tk), lambda b, j: (b, 0, 0, 0))],
        out_specs=pl.BlockSpec((tq, ATT_Q_DIM), lambda b, j: (b * nq + j, 0)),
        out_shape=jax.ShapeDtypeStruct((b_ * t_, ATT_Q_DIM), MXU_DTYPE),
        scratch_shapes=[pltpu.VMEM((ng, tk, tq), jnp.int32),
                        pltpu.VMEM((tk, tq), jnp.float32),
                        pltpu.VMEM((ATT_KV_HEADS, 1, tq * (ATT_HEADS // ATT_KV_HEADS)), jnp.float32),
                        pltpu.VMEM((ATT_KV_HEADS, 128, tq * (ATT_HEADS // ATT_KV_HEADS)), jnp.float32),
                        pltpu.VMEM((ATT_KV_HEADS, tk, tq * (ATT_HEADS // ATT_KV_HEADS)), jnp.float32)],
        compiler_params=pltpu.CompilerParams(
            dimension_semantics=("parallel", "arbitrary"),
            vmem_limit_bytes=V7X_VMEM_LIMIT_BYTES),
        name="dsa_attend_q_lanes",
    )(iqT, iwT, ikb.reshape(b_, t_, IDX_DIM), qT, khd, vxT)


def _dsa_prompt_pallas(x3d, w_in, kn_g, kn_b):
    b_, t_, _ = x3d.shape
    qT, iqT, iwT, k, khd, v, vxT, ik, ikb = _dsa_project_q_lanes(x3d, jnp.arange(t_), w_in, kn_g, kn_b)
    o = _dsa_attend_q_lanes(b_, t_, qT, iqT, iwT, ikb, khd, vxT)
    kv4 = lambda u: u.reshape(b_, t_, ATT_KV_HEADS, ATT_HEAD_DIM)
    return o, kv4(k), kv4(v), ik.reshape(b_, t_, IDX_DIM)


RW_ROW_TILE = 256


RW_PAIRS = RW_HEADS // 2
RW_PAIR_LANES = 2 * RW_HEAD


def _rwkv_project_rows(x, xp, mu_ref, wr_ref, wk_ref, wv_ref, w1_ref, w2_ref, a1_ref, a2_ref,
                       g1_ref, g2_ref, w0_ref, a0_ref):
    dx = xp - x
    mix = lambda c: (x + dx * mu_ref[c:c + 1, :]).astype(MXU_DTYPE)
    dot = lambda a, b: jnp.dot(a.astype(MXU_DTYPE), b, preferred_element_type=jnp.float32)
    r = dot(mix(0), wr_ref[...])
    lora_w = dot(jnp.tanh(dot(mix(1), w1_ref[...])), w2_ref[...])
    w_log = -jax.nn.softplus(-(w0_ref[...] + lora_w)) - 0.5
    d = jnp.exp(-jnp.exp(w_log))
    k = dot(mix(2), wk_ref[...])
    v = dot(mix(3), wv_ref[...])
    a = jax.nn.sigmoid(a0_ref[...] + dot(dot(mix(4), a1_ref[...]), a2_ref[...]))
    g = dot(jax.nn.sigmoid(dot(mix(5), g1_ref[...])), g2_ref[...])
    return r, d, k, v, a, g


def _rwkv_proj_step_kernel(x_ref, xp_ref, *refs):
    vals = _rwkv_project_rows(x_ref[...], xp_ref[...], *refs[:12])
    for ref, val in zip(refs[12:], vals):
        ref[...] = val


def _rwkv_proj_seq_kernel(x_ref, halo_ref, shift_ref, *refs):
    i = pl.program_id(1)
    x = x_ref[...]
    prev = jnp.where(i == 0, shift_ref[0], halo_ref[...])[SUBLANES - 1:SUBLANES, :]
    first = lax.broadcasted_iota(jnp.int32, (x.shape[0], 1), 0) == 0
    xp = jnp.where(first, prev, pltpu.roll(x, 1, 0))
    vals = _rwkv_project_rows(x, xp, *refs[:12])
    for ref, val in zip(refs[12:], vals):
        ref[...] = val


def _rwkv_consts(mu, w_r, w_k, w_v, w0, w1, w2, a0, a1, a2, g1, g2):
    cast = lambda w: w.astype(MXU_DTYPE)
    return [mu, cast(w_r), cast(w_k), cast(w_v), cast(w1), cast(w2), cast(a1), cast(a2), cast(g1), cast(g2),
            w0.reshape(1, D_MODEL), a0.reshape(1, D_MODEL)]


def _rwkv_project_step(x2d, xprev2d, *params):
    m = x2d.shape[0]
    consts = _rwkv_consts(*params)
    full = lambda a: pl.BlockSpec(a.shape, lambda i: (0,) * a.ndim)
    rows = pl.BlockSpec((m, D_MODEL), lambda i: (0, 0))
    return pl.pallas_call(
        _rwkv_proj_step_kernel,
        grid=(1,),
        in_specs=[rows, rows] + [full(a) for a in consts],
        out_specs=[rows] * 6,
        out_shape=[jax.ShapeDtypeStruct((m, D_MODEL), jnp.float32)] * 6,
        compiler_params=pltpu.CompilerParams(
            dimension_semantics=("arbitrary",),
            vmem_limit_bytes=V7X_VMEM_LIMIT_BYTES),
        name="rwkv_project_step",
    )(x2d, xprev2d, *consts)


def _rwkv_project_seq(x3d, shift, *params):
    b_, t_, _ = x3d.shape
    m = b_ * t_
    tm = RW_ROW_TILE
    nt = t_ // tm
    consts = _rwkv_consts(*params)
    full = lambda a: pl.BlockSpec(a.shape, lambda b, i: (0,) * a.ndim)
    rows = pl.BlockSpec((tm, D_MODEL), lambda b, i: (b * nt + i, 0))
    halo = pl.BlockSpec((SUBLANES, D_MODEL), lambda b, i: (jnp.maximum((b * nt + i) * (tm // SUBLANES) - 1, 0), 0))
    shift8 = jnp.pad(shift[:, None, :], ((0, 0), (SUBLANES - 1, 0), (0, 0)))
    x2d = x3d.reshape(m, D_MODEL)
    return pl.pallas_call(
        _rwkv_proj_seq_kernel,
        grid=(b_, nt),
        in_specs=[rows, halo, pl.BlockSpec((1, SUBLANES, D_MODEL), lambda b, i: (b, 0, 0))]
        + [full(a) for a in consts],
        out_specs=[rows] * 6,
        out_shape=[jax.ShapeDtypeStruct((m, D_MODEL), jnp.float32)] * 6,
        compiler_params=pltpu.CompilerParams(
            dimension_semantics=("parallel", "parallel"),
            vmem_limit_bytes=V7X_VMEM_LIMIT_BYTES),
        name="rwkv_project_seq",
    )(x2d, x2d, shift8, *consts)


RW_LANES = 128
RW_TIME_CHUNK = 64


def _rwkv_scan_kernel(r_ref, d_ref, k_ref, v_ref, a_ref, s0_ref, kk_ref, ka_ref, rk_ref, gg_ref, gb_ref,
                      z_ref, s_out_ref, s_ref, vec_ref):
    c = pl.program_id(1)
    n = RW_HEAD
    tc = r_ref.shape[1]
    low_half = lax.broadcasted_iota(jnp.int32, (n, RW_LANES), 1) < n

    @pl.when(c == 0)
    def _():
        s_ref[...] = s0_ref[...]

    def swap_layout(x):
        xt = jnp.concatenate([x, x], axis=0).T
        return jnp.where(low_half, xt[:n], xt[n:])

    def load_step(ref, t):
        rows = ref[:, t, :]
        return swap_layout(jnp.concatenate(
            [rows[:, p * RW_PAIR_LANES:(p + 1) * RW_PAIR_LANES] for p in range(RW_PAIRS)], axis=0))

    def store_step(ref, t, val):
        tile = swap_layout(val)
        ref[:, t, :] = jnp.concatenate(
            [tile[p * RW_SEQ_PER_TILE:(p + 1) * RW_SEQ_PER_TILE] for p in range(RW_PAIRS)], axis=1)

    def prepare(t, slot):
        r, k, a = load_step(r_ref, t), load_step(k_ref, t), load_step(a_ref, t)
        kkr = k * kk_ref[...]
        nrm = jnp.sqrt(jnp.sum(kkr * kkr, axis=0, keepdims=True))
        kk = kkr / jnp.maximum(nrm, 1e-12)
        vec_ref[slot, 0] = kk
        vec_ref[slot, 1] = load_step(d_ref, t)
        vec_ref[slot, 2] = kk * a
        vec_ref[slot, 3] = k * (1.0 + (a - 1.0) * ka_ref[...])
        vec_ref[slot, 4] = r
        vec_ref[slot, 5] = load_step(v_ref, t)

    def step(t, slot):
        row = lambda q, j: vec_ref[slot, q, j:j + 1, :]
        v = vec_ref[slot, 5]
        lanes = 4
        sa_parts = [s_ref[j] * row(0, j) for j in range(lanes)]
        for j in range(lanes, n):
            sa_parts[j % lanes] = sa_parts[j % lanes] + s_ref[j] * row(0, j)
        sa = _tree_sum(sa_parts)
        y_parts = []
        for j in range(n):
            sn = s_ref[j] * row(1, j) - sa * row(2, j) + v * row(3, j)
            s_ref[j] = sn
            if j < lanes:
                y_parts.append(sn * row(4, j))
            else:
                y_parts[j % lanes] = y_parts[j % lanes] + sn * row(4, j)
        y = _tree_sum(y_parts)
        mu = jnp.mean(y, axis=0, keepdims=True)
        yc = y - mu
        var = jnp.mean(yc * yc, axis=0, keepdims=True)
        bonus = jnp.sum(vec_ref[slot, 4] * vec_ref[slot, 3] * rk_ref[...], axis=0, keepdims=True)
        store_step(z_ref, t, yc * lax.rsqrt(var + RW_GN_EPS) * gg_ref[...] + gb_ref[...] + bonus * v)

    prepare(0, 0)
    if tc == 1:
        step(0, 0)
    else:
        def two_steps(i, carry):
            t = 2 * i
            prepare(t + 1, 1)
            step(t, 0)
            prepare(jnp.minimum(t + 2, tc - 1), 0)
            step(t + 1, 1)
            return carry

        lax.fori_loop(0, tc // 2, two_steps, 0)

    @pl.when(c == pl.num_programs(1) - 1)
    def _():
        s_out_ref[...] = s_ref[...]


RW_SEQ_PER_TILE = RW_LANES // RW_HEADS


def _rwkv_lane_heads():
    half = jnp.arange(2)[:, None, None]
    pair = jnp.arange(RW_PAIRS)[None, :, None]
    return jnp.broadcast_to(2 * pair + half, (2, RW_PAIRS, RW_SEQ_PER_TILE)).reshape(RW_LANES)


def _rwkv_scan(r, d, k, v, a, s0, k_k, k_a, r_k, gn_g, gn_b):
    b_, t_, _ = r.shape
    n = RW_HEAD
    tc = RW_TIME_CHUNK if t_ % RW_TIME_CHUNK == 0 else t_
    table = lambda p: p.reshape(RW_HEADS, n)[_rwkv_lane_heads()].T
    seq = pl.BlockSpec((RW_SEQ_PER_TILE, tc, D_MODEL), lambda l, c: (l, c, 0))
    state = pl.BlockSpec((n, n, RW_LANES), lambda l, c: (0, 0, l))
    tab = pl.BlockSpec((n, RW_LANES), lambda l, c: (0, 0))
    return pl.pallas_call(
        _rwkv_scan_kernel,
        grid=(b_ // RW_SEQ_PER_TILE, t_ // tc),
        in_specs=[seq] * 5 + [state] + [tab] * 5,
        out_specs=[seq, state],
        out_shape=[jax.ShapeDtypeStruct(r.shape, jnp.float32),
                   jax.ShapeDtypeStruct(s0.shape, jnp.float32)],
        scratch_shapes=[pltpu.VMEM((n, n, RW_LANES), jnp.float32),
                        pltpu.VMEM((2, 6, n, RW_LANES), jnp.float32)],
        compiler_params=pltpu.CompilerParams(
            dimension_semantics=("parallel", "arbitrary"),
            vmem_limit_bytes=V7X_VMEM_LIMIT_BYTES),
        name="rwkv_scan",
    )(r, d, k, v, a, s0, table(k_k), table(k_a), table(r_k), table(gn_g), table(gn_b))


def _rwkv_state_to_lanes(wkv):
    b_ = wkv.shape[0]
    w = wkv.astype(jnp.float32).reshape(b_ // RW_SEQ_PER_TILE, RW_SEQ_PER_TILE, RW_PAIRS, 2, RW_HEAD, RW_HEAD)
    return jnp.transpose(w, (5, 4, 0, 3, 2, 1)).reshape(RW_HEAD, RW_HEAD, b_ * RW_HEADS)


def _rwkv_state_from_lanes(s, b_):
    w = s.reshape(RW_HEAD, RW_HEAD, b_ // RW_SEQ_PER_TILE, 2, RW_PAIRS, RW_SEQ_PER_TILE)
    return jnp.transpose(w, (2, 5, 4, 3, 1, 0)).reshape(b_, RW_HEADS, RW_HEAD, RW_HEAD)


def _rwkv7_mixer_pallas(x3d, shift, wkv, mu, w_r, w_k, w_v, w0, w1, w2, a0, a1, a2, g1, g2,
                        k_k, k_a, r_k, gn_g, gn_b):
    b_, t_, _ = x3d.shape
    params = (mu, w_r, w_k, w_v, w0, w1, w2, a0, a1, a2, g1, g2)
    if t_ == 1:
        *seqs, g = _rwkv_project_step(x3d.reshape(b_, D_MODEL), shift, *params)
    else:
        *seqs, g = _rwkv_project_seq(x3d, shift, *params)
    seqs = [u.reshape(b_, t_, D_MODEL) for u in seqs]
    z, s = _rwkv_scan(*seqs, _rwkv_state_to_lanes(wkv), k_k, k_a, r_k, gn_g, gn_b)
    return z.reshape(b_ * t_, D_MODEL), g, x3d[:, -1], _rwkv_state_from_lanes(s, b_).astype(wkv.dtype)


def _proj_gate_ln_kernel(x_ref, h_ref, gate_ref, w_ref, g_ref, b_ref, o_ref):
    h = (h_ref[...] * gate_ref[...]).astype(MXU_DTYPE)
    y = ALPHA * x_ref[...] + jnp.dot(h, w_ref[...], preferred_element_type=jnp.float32)
    o_ref[...] = _ln_rows(y, g_ref[...], b_ref[...])


def _proj_gate_post_norm(x2d, h2d, gate2d, w_out, g, b):
    m = x2d.shape[0]
    tm = _row_tile(m)
    rows = pl.BlockSpec((tm, D_MODEL), lambda i: (i, 0))
    vec = pl.BlockSpec((1, D_MODEL), lambda i: (0, 0))
    return pl.pallas_call(
        _proj_gate_ln_kernel,
        grid=(m // tm,),
        in_specs=[rows, rows, rows, pl.BlockSpec((D_MODEL, D_MODEL), lambda i: (0, 0)), vec, vec],
        out_specs=rows,
        out_shape=jax.ShapeDtypeStruct((m, D_MODEL), jnp.float32),
        compiler_params=pltpu.CompilerParams(
            dimension_semantics=("parallel",),
            vmem_limit_bytes=V7X_VMEM_LIMIT_BYTES),
        name="proj_gate_post_norm",
    )(x2d, h2d, gate2d, w_out, g.reshape(1, D_MODEL), b.reshape(1, D_MODEL))


GM_ROW_TILE = 512
GM_ROW_PARTS = 2


def _gmlp_kernel(x_ref, win_ref, lng_ref, lnb_ref, mixw_ref, mixb_ref, wout_ref, g_ref, b_ref, *out_refs,
                 chunk_len, emit_v, parts):
    tp = x_ref.shape[0] // parts
    for part in range(parts):
        rows_p = slice(part * tp, (part + 1) * tp)
        x = x_ref[rows_p, :]
        h = jax.nn.gelu(jnp.dot(x.astype(MXU_DTYPE), win_ref[...], preferred_element_type=jnp.float32))
        u = h[:, :GM_WIDTH]
        v = _ln_rows(h[:, GM_WIDTH:], lng_ref[...], lnb_ref[...])
        if emit_v:
            out_refs[1][rows_p, :] = v
        if chunk_len == 1:
            gated = u * (v * mixw_ref[...] + mixb_ref[...])
        else:
            causal = (lax.broadcasted_iota(jnp.int32, (chunk_len, chunk_len), 0)
                      >= lax.broadcasted_iota(jnp.int32, (chunk_len, chunk_len), 1))
            vb = v.astype(MXU_DTYPE)
            cols = []
            for g in range(GM_GROUPS):
                w = jnp.where(causal, mixw_ref[g], 0.0).astype(MXU_DTYPE)
                bias = mixb_ref[:, g:g + 1]
                lanes = slice(g * GM_GROUP_DIM, (g + 1) * GM_GROUP_DIM)
                rows = [jnp.dot(w, vb[c * chunk_len:(c + 1) * chunk_len, lanes],
                                preferred_element_type=jnp.float32) + bias
                        for c in range(tp // chunk_len)]
                cols.append(jnp.concatenate(rows, axis=0))
            gated = u * jnp.concatenate(cols, axis=1)
        y = ALPHA * x + jnp.dot(gated.astype(MXU_DTYPE), wout_ref[...], preferred_element_type=jnp.float32)
        out_refs[0][rows_p, :] = _ln_rows(y, g_ref[...], b_ref[...])


def _gmlp_block(x2d, seq_len, w_in, ln_g, ln_b, ws, bs, w_out, g, b, emit_v):
    m = x2d.shape[0]
    chunk_len = min(seq_len, CHUNK)
    if chunk_len == 1:
        tm, parts = m, 1
        mixw = jnp.repeat(ws[:, 0, 0], GM_GROUP_DIM)[None, :]
        mixb = jnp.repeat(bs[:, 0], GM_GROUP_DIM)[None, :]
    else:
        tm, parts = GM_ROW_TILE, GM_ROW_PARTS
        mixw = ws[:, :chunk_len, :chunk_len]
        mixb = bs[:, :chunk_len].T
    full = lambda a: pl.BlockSpec(a.shape, lambda i: (0,) * a.ndim, pipeline_mode=pl.Buffered(1))
    rows = lambda n: pl.BlockSpec((tm, n), lambda i: (i, 0))
    consts = [w_in.astype(MXU_DTYPE), ln_g.reshape(1, GM_WIDTH), ln_b.reshape(1, GM_WIDTH), mixw, mixb,
              w_out.astype(MXU_DTYPE), g.reshape(1, D_MODEL), b.reshape(1, D_MODEL)]
    out_specs = [rows(D_MODEL)] + ([rows(GM_WIDTH)] if emit_v else [])
    out_shape = [jax.ShapeDtypeStruct((m, D_MODEL), jnp.float32)] + (
        [jax.ShapeDtypeStruct((m, GM_WIDTH), jnp.float32)] if emit_v else [])
    return pl.pallas_call(
        functools.partial(_gmlp_kernel, chunk_len=chunk_len, emit_v=emit_v, parts=parts),
        grid=(m // tm,),
        in_specs=[rows(D_MODEL)] + [full(a) for a in consts],
        out_specs=out_specs,
        out_shape=out_shape,
        compiler_params=pltpu.CompilerParams(
            dimension_semantics=("parallel",),
            vmem_limit_bytes=V7X_VMEM_LIMIT_BYTES),
        name="gmlp_block",
    )(x2d, *consts)


SSM_ROW_TILE = 256
SSM_ROW_PARTS = 2
SSM_BC_DIM = SSM_GROUPS * SSM_STATE
SSM_DT_LANES = 128
SUBLANES = 8


def _ssm_activate(xb, xbc, taps, wz_ref, wdt_ref, cw_ref, cb_ref, dtb_ref, z_ref, xs_ref, bm_ref, cm_ref, dt_ref):
    conv = cb_ref[...] + xbc * cw_ref[SSM_CONV - 1:SSM_CONV, :]
    for j in range(SSM_CONV - 1):
        conv = conv + taps[j] * cw_ref[j:j + 1, :]
    act = conv * jax.nn.sigmoid(conv)
    xs_ref[...] = act[:, :SSM_D_INNER]
    bm_ref[...] = act[:, SSM_D_INNER:SSM_D_INNER + SSM_BC_DIM].astype(bm_ref.dtype)
    cm_ref[...] = act[:, SSM_D_INNER + SSM_BC_DIM:].astype(cm_ref.dtype)
    z_ref[...] = jnp.dot(xb, wz_ref[...], preferred_element_type=jnp.float32)
    dt_ref[...] = jax.nn.softplus(jnp.dot(xb, wdt_ref[...], preferred_element_type=jnp.float32) + dtb_ref[...])


def _ssm_proj_seq_kernel(x_ref, halo_ref, cs_ref, wx_ref, wz_ref, wdt_ref, cw_ref, cb_ref, dtb_ref,
                         z_ref, xs_ref, bm_ref, cm_ref, dt_ref, tail_ref):
    i = pl.program_id(1)
    tm = x_ref.shape[0]
    tp = tm // SSM_ROW_PARTS
    prev = jnp.dot(halo_ref[...].astype(MXU_DTYPE), wx_ref[...], preferred_element_type=jnp.float32)
    prev = jnp.where(i == 0, cs_ref[0], prev)
    row = lax.broadcasted_iota(jnp.int32, (SUBLANES, 1), 0)
    for part in range(SSM_ROW_PARTS):
        rows_p = slice(part * tp, (part + 1) * tp)
        xb = x_ref[rows_p, :].astype(MXU_DTYPE)
        xbc = jnp.dot(xb, wx_ref[...], preferred_element_type=jnp.float32)
        taps = []
        for j in range(SSM_CONV - 1):
            back = SSM_CONV - 1 - j
            rolled = pltpu.roll(xbc, back, 0)
            top = jnp.where(row < back, pltpu.roll(prev, back, 0), rolled[:SUBLANES])
            taps.append(jnp.concatenate([top, rolled[SUBLANES:]], axis=0))
        _ssm_activate(xb, xbc, taps, wz_ref, wdt_ref, cw_ref, cb_ref, dtb_ref,
                      z_ref.at[rows_p, :], xs_ref.at[rows_p, :], bm_ref.at[rows_p, :], cm_ref.at[rows_p, :],
                      dt_ref.at[rows_p, :])
        prev = xbc[tp - SUBLANES:, :]
    tail_ref[0] = prev


def _ssm_proj_step_kernel(x_ref, st_ref, wx_ref, wz_ref, wdt_ref, cw_ref, cb_ref, dtb_ref,
                          z_ref, xs_ref, bm_ref, cm_ref, dt_ref, st_out_ref):
    xb = x_ref[...].astype(MXU_DTYPE)
    xbc = jnp.dot(xb, wx_ref[...], preferred_element_type=jnp.float32)
    taps = [st_ref[j] for j in range(SSM_CONV - 1)]
    _ssm_activate(xb, xbc, taps, wz_ref, wdt_ref, cw_ref, cb_ref, dtb_ref, z_ref, xs_ref, bm_ref, cm_ref, dt_ref)
    for j in range(SSM_CONV - 2):
        st_out_ref[j] = st_ref[j + 1]
    st_out_ref[SSM_CONV - 2] = xbc


def _ssm_project(x3d, conv_state, w_in, conv_w, conv_b, dt_bias):
    b_, t_, _ = x3d.shape
    m = b_ * t_
    w_z, w_x, w_dt = jnp.split(w_in, [SSM_D_INNER, SSM_D_INNER + SSM_CONV_DIM], axis=1)
    cast = lambda w: w.astype(MXU_DTYPE)
    consts = [cast(w_x), cast(w_z), cast(jnp.pad(w_dt, ((0, 0), (0, SSM_DT_LANES - SSM_HEADS)))),
              conv_w, conv_b.reshape(1, SSM_CONV_DIM),
              jnp.pad(dt_bias, (0, SSM_DT_LANES - SSM_HEADS)).reshape(1, SSM_DT_LANES)]
    sds = jax.ShapeDtypeStruct
    outs = [sds((m, SSM_D_INNER), jnp.float32), sds((m, SSM_D_INNER), jnp.float32),
            sds((m, SSM_BC_DIM), MXU_DTYPE), sds((m, SSM_BC_DIM), MXU_DTYPE), sds((m, SSM_DT_LANES), jnp.float32)]
    widths = [SSM_D_INNER, SSM_D_INNER, SSM_BC_DIM, SSM_BC_DIM, SSM_DT_LANES]
    params = dict(vmem_limit_bytes=V7X_VMEM_LIMIT_BYTES)
    x2d = x3d.reshape(m, D_MODEL)
    if t_ == 1:
        full = lambda a: pl.BlockSpec(a.shape, lambda i: (0,) * a.ndim)
        st = jnp.transpose(conv_state, (1, 0, 2))
        res = pl.pallas_call(
            _ssm_proj_step_kernel,
            grid=(1,),
            in_specs=[full(x2d), full(st)] + [full(a) for a in consts],
            out_specs=[pl.BlockSpec((m, w), lambda i: (0, 0)) for w in widths] + [full(st)],
            out_shape=outs + [sds(st.shape, jnp.float32)],
            compiler_params=pltpu.CompilerParams(dimension_semantics=("arbitrary",), **params),
            name="ssm_project_step",
        )(x2d, st, *consts)
        return list(res[:5]) + [jnp.transpose(res[5], (1, 0, 2))]
    tm = SSM_ROW_TILE
    nt = t_ // tm
    full = lambda a: pl.BlockSpec(a.shape, lambda b, i: (0,) * a.ndim)
    rows = lambda w: pl.BlockSpec((tm, w), lambda b, i: (b * nt + i, 0))
    halo = pl.BlockSpec((SUBLANES, D_MODEL), lambda b, i: (jnp.maximum((b * nt + i) * (tm // SUBLANES) - 1, 0), 0))
    cs8 = jnp.pad(conv_state, ((0, 0), (SUBLANES - (SSM_CONV - 1), 0), (0, 0)))
    tail = pl.BlockSpec((1, SUBLANES, SSM_CONV_DIM), lambda b, i: (b, 0, 0))
    res = pl.pallas_call(
        _ssm_proj_seq_kernel,
        grid=(b_, nt),
        in_specs=[rows(D_MODEL), halo, tail] + [full(a) for a in consts],
        out_specs=[rows(w) for w in widths] + [tail],
        out_shape=outs + [sds((b_, SUBLANES, SSM_CONV_DIM), jnp.float32)],
        compiler_params=pltpu.CompilerParams(dimension_semantics=("parallel", "arbitrary"), **params),
        name="ssm_project_seq",
    )(x2d, x2d, cs8, *consts)
    return list(res[:5]) + [res[5][:, SUBLANES - (SSM_CONV - 1):, :]]


def _ssm_gate_norm(y, xs, z, dskip, normg):
    yg = (y + xs * dskip) * (z * jax.nn.sigmoid(z))
    gw = SSM_D_INNER // SSM_GROUPS
    outs = []
    for g in range(SSM_GROUPS):
        part = yg[:, g * gw:(g + 1) * gw]
        ms = jnp.mean(part * part, axis=-1, keepdims=True)
        outs.append(part * lax.rsqrt(ms + LN_EPS))
    return jnp.concatenate(outs, axis=1) * normg


def _ssm_chunk_kernel(xs_ref, bm_ref, cm_ref, dt_ref, z_ref, aneg_ref, dskip_ref, normg_ref,
                      yg_ref, h_out_ref, h_ref, yT_ref, xe_ref):
    c = pl.program_id(1)
    l = xs_ref.shape[0]
    hd = SSM_HEAD_DIM

    @pl.when(c == 0)
    def _():
        h_ref[...] = jnp.zeros_like(h_ref)

    dot = lambda u, w: jnp.dot(u, w, preferred_element_type=jnp.float32)
    dt = dt_ref[...]
    a = dt * aneg_ref[...]
    r_i = lax.broadcasted_iota(jnp.int32, (l, l), 0)
    c_i = lax.broadcasted_iota(jnp.int32, (l, l), 1)
    tril = jnp.where(r_i >= c_i, 1.0, 0.0)
    hi = lax.Precision.HIGHEST
    acum = jnp.dot(tril, a, precision=hi, preferred_element_type=jnp.float32)
    acum_t = jnp.dot(a.T, tril.T, precision=hi, preferred_element_type=jnp.float32)
    dt_t = dt.T
    to_end_t = jnp.exp(acum_t[:, l - 1:l] - acum_t)
    from_start_t = jnp.exp(acum_t)
    chunk_decay = jnp.exp(acum[l - 1:l, :])
    upper = r_i <= c_i
    xs = xs_ref[...]
    for g in range(SSM_GROUPS):
        bm = bm_ref[:, g * SSM_STATE:(g + 1) * SSM_STATE]
        cm_t = cm_ref[:, g * SSM_STATE:(g + 1) * SSM_STATE].astype(jnp.float32).T.astype(MXU_DTYPE)
        cb_t = dot(bm, cm_t)
        h_in = h_ref[g * SSM_HPG:(g + 1) * SSM_HPG].reshape(SSM_HPG * hd, SSM_STATE)
        y_off = dot(h_in.astype(MXU_DTYPE), cm_t)
        for e in range(SSM_HPG):
            h = g * SSM_HPG + e
            if h % 2 == 0:
                xs_pair_t = xs[:, h * hd:(h + 2) * hd].T
            xdt_t = xs_pair_t[(h % 2) * hd:(h % 2 + 1) * hd] * dt_t[h:h + 1, :]
            seg = jnp.exp(jnp.where(upper, acum_t[h:h + 1, :] - acum[:, h:h + 1], -jnp.inf))
            y_diag = dot(xdt_t.astype(MXU_DTYPE), (cb_t * seg).astype(MXU_DTYPE))
            yT_ref[h * hd:(h + 1) * hd, :] = y_diag + y_off[e * hd:(e + 1) * hd] * from_start_t[h:h + 1, :]
            xe_ref[e * hd:(e + 1) * hd, :] = (xdt_t * to_end_t[h:h + 1, :]).astype(xe_ref.dtype)
        states = dot(xe_ref[...], bm)
        for e in range(SSM_HPG):
            h = g * SSM_HPG + e
            h_ref[h] = h_ref[h] * chunk_decay[:, h:h + 1] + states[e * hd:(e + 1) * hd]
    y = jnp.concatenate([yT_ref[i * l:(i + 1) * l, :].T for i in range(SSM_D_INNER // l)], axis=1)
    yg_ref[...] = _ssm_gate_norm(y, xs, z_ref[...], dskip_ref[...], normg_ref[...]).astype(yg_ref.dtype)

    @pl.when(c == pl.num_programs(1) - 1)
    def _():
        h_out_ref[0] = h_ref[...]


def _ssm_head_lanes(p):
    return jnp.pad(p.astype(jnp.float32), (0, SSM_DT_LANES - SSM_HEADS)).reshape(1, SSM_DT_LANES)


def _ssm_chunk_scan(b_, t_, xs, bm, cm, dt, z, a_log, d_skip, norm_g):
    l = SSM_CHUNK
    nc = t_ // l
    rows = lambda w: pl.BlockSpec((l, w), lambda b, c: (b * nc + c, 0))
    vec = lambda w: pl.BlockSpec((1, w), lambda b, c: (0, 0))
    aneg = _ssm_head_lanes(-jnp.exp(a_log.astype(jnp.float32)))
    dskip = jnp.repeat(d_skip, SSM_HEAD_DIM).reshape(1, SSM_D_INNER)
    yg, h_new = pl.pallas_call(
        _ssm_chunk_kernel,
        grid=(b_, nc),
        in_specs=[rows(SSM_D_INNER), rows(SSM_BC_DIM), rows(SSM_BC_DIM), rows(SSM_DT_LANES), rows(SSM_D_INNER),
                  vec(SSM_DT_LANES), vec(SSM_D_INNER), vec(SSM_D_INNER)],
        out_specs=[rows(SSM_D_INNER),
                   pl.BlockSpec((1, SSM_HEADS, SSM_HEAD_DIM, SSM_STATE), lambda b, c: (b, 0, 0, 0))],
        out_shape=[jax.ShapeDtypeStruct((b_ * t_, SSM_D_INNER), MXU_DTYPE),
                   jax.ShapeDtypeStruct((b_, SSM_HEADS, SSM_HEAD_DIM, SSM_STATE), jnp.float32)],
        scratch_shapes=[pltpu.VMEM((SSM_HEADS, SSM_HEAD_DIM, SSM_STATE), jnp.float32),
                        pltpu.VMEM((SSM_D_INNER, l), jnp.float32),
                        pltpu.VMEM((SSM_HPG * SSM_HEAD_DIM, l), MXU_DTYPE)],
        compiler_params=pltpu.CompilerParams(
            dimension_semantics=("parallel", "arbitrary"),
            vmem_limit_bytes=V7X_VMEM_LIMIT_BYTES),
        name="ssm_chunk_scan",
    )(xs, bm, cm, dt, z, aneg, dskip, norm_g.reshape(1, SSM_D_INNER))
    return yg, h_new


def _ssm_step_kernel(h0_ref, xs_ref, dt_ref, an_ref, bm_ref, cm_ref, y_ref, h_ref):
    h0 = h0_ref[0]
    dt = dt_ref[0]
    decay = jnp.exp(dt * an_ref[...])
    xdt = xs_ref[0] * dt
    bm = bm_ref[0].astype(jnp.float32)
    cm = cm_ref[0].astype(jnp.float32)
    h_ref[0] = h0 * decay + xdt * bm
    cb = jnp.sum(cm * bm, axis=-1, keepdims=True)
    y_ref[0] = cb * xdt + jnp.sum(cm * h0, axis=-1, keepdims=True) * decay


def _ssm_step(state, xs, bm, cm, dt, a_log):
    b_ = state.shape[0]
    per_head = lambda u: jnp.repeat(u.reshape(b_, SSM_GROUPS, 1, SSM_STATE), SSM_HPG, axis=1)
    xs4 = xs.reshape(b_, SSM_HEADS, SSM_HEAD_DIM, 1)
    dt4 = dt[:, :SSM_HEADS].reshape(b_, SSM_HEADS, 1, 1)
    an = (-jnp.exp(a_log.astype(jnp.float32))).reshape(SSM_HEADS, 1, 1)
    blk = lambda a: pl.BlockSpec((1,) + a.shape[1:], lambda b: (b, 0, 0, 0))
    args = [state.astype(jnp.float32), xs4, dt4, an, per_head(bm), per_head(cm)]
    y4, h_new = pl.pallas_call(
        _ssm_step_kernel,
        grid=(b_,),
        in_specs=[blk(args[0]), blk(xs4), blk(dt4), pl.BlockSpec(an.shape, lambda b: (0, 0, 0)),
                  blk(args[4]), blk(args[5])],
        out_specs=[blk(xs4), blk(args[0])],
        out_shape=[jax.ShapeDtypeStruct(xs4.shape, jnp.float32), jax.ShapeDtypeStruct(state.shape, jnp.float32)],
        compiler_params=pltpu.CompilerParams(
            dimension_semantics=("parallel",),
            vmem_limit_bytes=V7X_VMEM_LIMIT_BYTES),
        name="ssm_step",
    )(*args)
    return y4.reshape(b_, SSM_D_INNER), h_new


def _ssm_gate_norm_kernel(y_ref, xs_ref, z_ref, dskip_ref, normg_ref, o_ref):
    o_ref[...] = _ssm_gate_norm(y_ref[...], xs_ref[...], z_ref[...], dskip_ref[...], normg_ref[...]).astype(o_ref.dtype)


def _ssm_gate_norm_rows(y, xs, z, d_skip, norm_g):
    full = lambda a: pl.BlockSpec(a.shape, lambda i: (0,) * a.ndim)
    args = [y, xs, z, jnp.repeat(d_skip, SSM_HEAD_DIM).reshape(1, SSM_D_INNER), norm_g.reshape(1, SSM_D_INNER)]
    return pl.pallas_call(
        _ssm_gate_norm_kernel,
        grid=(1,),
        in_specs=[full(a) for a in args],
        out_specs=full(y),
        out_shape=jax.ShapeDtypeStruct(y.shape, MXU_DTYPE),
        name="ssm_gate_norm",
    )(*args)


def _mamba2_mixer_pallas(x3d, conv_state, ssm_state, w_in, conv_w, conv_b, dt_bias, a_log, d_skip, norm_g):
    b_, t_, _ = x3d.shape
    z, xs, bm, cm, dt, conv_new = _ssm_project(x3d, conv_state, w_in, conv_w, conv_b, dt_bias)
    if t_ == 1:
        y, h_new = _ssm_step(ssm_state, xs, bm, cm, dt, a_log)
        yg = _ssm_gate_norm_rows(y, xs, z, d_skip, norm_g)
    else:
        yg, h_new = _ssm_chunk_scan(b_, t_, xs, bm, cm, dt, z, a_log, d_skip, norm_g)
    return yg, conv_new, h_new.astype(ssm_state.dtype)


PAGES_PER_STEP = 8


def _sortable_key(score):
    bits = pltpu.bitcast(score, jnp.int32)
    return jnp.where(bits >= 0, bits, bits ^ jnp.int32(0x7FFFFFFF))


def _decode_score_kernel(pt_ref, iq_ref, iw_ref, ikn_ref, *rest):
    idx_refs, (key_ref, knew_ref) = rest[:-2], rest[-2:]
    iq, iw = iq_ref[0], iw_ref[0]
    weigh = lambda sc: jnp.sum(iw * jnp.maximum(sc, 0.0), axis=0, keepdims=True)
    ik_t = jnp.concatenate([r[0] for r in idx_refs], axis=1).astype(MXU_DTYPE)
    key_ref[0] = _sortable_key(weigh(jnp.dot(iq, ik_t, preferred_element_type=jnp.float32)))

    @pl.when(pl.program_id(1) == 0)
    def _():
        sc_new = jnp.sum(iq.astype(jnp.float32) * ikn_ref[0].astype(jnp.float32), axis=1, keepdims=True)
        knew_ref[0] = jnp.broadcast_to(_sortable_key(weigh(sc_new)), knew_ref.shape[1:])


def _decode_select_kernel(keys_ref, knew_ref, thr_ref, last_ref, *, topk, col_bits):
    keys = keys_ref[...]
    key_new = knew_ref[:, 0:1]
    past = keys.shape[1]
    col = lax.broadcasted_iota(jnp.int32, keys.shape, 1)

    def count(pred_past, pred_new):
        hit = jnp.where(pred_past(keys, col), 1.0, 0.0)
        cnt = _tree_sum([hit[:, l * 128:(l + 1) * 128] for l in range(past // 128)])
        return jnp.sum(cnt, axis=1, keepdims=True) + jnp.where(pred_new(key_new), 1.0, 0.0)

    def at_least(cand):
        return count(lambda k, c: k >= cand, lambda k: k >= cand) >= float(topk)

    def two_bits(i, thr):
        hi = lax.shift_left(jnp.int32(1), jnp.int32(31) - 2 * i)
        lo = lax.shift_left(jnp.int32(1), jnp.int32(30) - 2 * i)
        c1, c2, c3 = thr ^ lo, thr ^ hi, thr ^ hi ^ lo
        return jnp.where(at_least(c3), c3, jnp.where(at_least(c2), c2, jnp.where(at_least(c1), c1, thr)))

    thr = lax.fori_loop(0, 16, two_bits, jnp.full(key_new.shape, INT32_MIN, jnp.int32))
    need = float(topk) - count(lambda k, c: k > thr, lambda k: k > thr)

    def col_body(i, last):
        cand = last | lax.shift_left(jnp.int32(1), jnp.int32(col_bits - 1) - i)
        ties = count(lambda k, c: (k == thr) & (c < cand), lambda k: (k == thr) & (jnp.int32(past) < cand))
        return jnp.where(ties < need, cand, last)

    n_tied = count(lambda k, c: k == thr, lambda k: k == thr)
    excess = jnp.max(jnp.where(n_tied > need, 1.0, 0.0), axis=0, keepdims=True)
    last_tie = lax.cond(excess[0, 0] > 0.0,
                        lambda: lax.fori_loop(0, col_bits, col_body, jnp.zeros(key_new.shape, jnp.int32)),
                        lambda: jnp.full(key_new.shape, 2 ** col_bits - 1, jnp.int32))
    thr_ref[...] = jnp.broadcast_to(thr, thr_ref.shape)
    last_ref[...] = jnp.broadcast_to(last_tie, last_ref.shape)


def _decode_attend_kernel(pt_ref, q_ref, kn_ref, vn_ref, key_ref, knew_ref, thr_ref, last_ref, *rest,
                          n_steps, pages):
    k_refs, v_refs = rest[:pages], rest[pages:2 * pages]
    o_ref, m_ref, l_ref, acc_ref = rest[2 * pages:]
    s = pl.program_id(1)
    nk = key_ref.shape[2]
    nt = (((1,), (1,)), ((), ()))
    thr, last_tie = thr_ref[0, :, 0:1], last_ref[0, :, 0:1]
    keep_mask = lambda key, col: (key > thr) | ((key == thr) & (col <= last_tie))
    gsz = ATT_HEADS // ATT_KV_HEADS
    q = q_ref[0]
    q_wide = jnp.concatenate([q] * ATT_KV_HEADS, axis=1)
    head_i = lax.broadcasted_iota(jnp.int32, q_wide.shape, 0)
    col_i = lax.broadcasted_iota(jnp.int32, q_wide.shape, 1)
    own_group = (col_i // ATT_HEAD_DIM) == (head_i // gsz)
    q_blk = jnp.where(own_group, q_wide, jnp.zeros_like(q_wide))

    def online_update(logits, weighted_values):
        m_old = m_ref[...]
        m_new = jnp.maximum(m_old, jnp.max(logits, axis=1, keepdims=True))
        p = jnp.exp(logits - m_new)
        alpha = jnp.exp(m_old - m_new)
        l_ref[...] = alpha * l_ref[...] + jnp.sum(p, axis=1, keepdims=True)
        acc_ref[...] = alpha * acc_ref[...] + weighted_values(p.astype(MXU_DTYPE))
        m_ref[...] = m_new

    @pl.when(s == 0)
    def _init():
        m_ref[...] = jnp.full(m_ref.shape, MASK_NEG, jnp.float32)
        l_ref[...] = jnp.zeros(l_ref.shape, jnp.float32)
        acc_ref[...] = jnp.zeros(acc_ref.shape, jnp.float32)

    col = lax.broadcasted_iota(jnp.int32, (1, nk), 1) + s * nk
    bias = jnp.where(keep_mask(key_ref[0], col), 0.0, MASK_NEG)
    k_t = jnp.concatenate([r[0] for r in k_refs], axis=1).astype(MXU_DTYPE)
    v_t = jnp.concatenate([r[0] for r in v_refs], axis=1).astype(MXU_DTYPE)
    online_update(jnp.dot(q_blk, k_t, preferred_element_type=jnp.float32) + bias,
                  lambda p: lax.dot_general(p, v_t, nt, preferred_element_type=jnp.float32))

    @pl.when(s == n_steps - 1)
    def _finish():
        keep_new = keep_mask(knew_ref[0, :, 0:1], jnp.int32(n_steps * nk))
        logit = jnp.sum(q_blk.astype(jnp.float32) * kn_ref[0].astype(jnp.float32), axis=1, keepdims=True)
        v_row = vn_ref[0].astype(jnp.float32)
        online_update(logit + jnp.where(keep_new, 0.0, MASK_NEG), lambda p: p.astype(jnp.float32) * v_row)
        out = jnp.where(own_group, acc_ref[...] / l_ref[...], 0.0)
        o = out[:, 0:ATT_HEAD_DIM]
        for g in range(1, ATT_KV_HEADS):
            o = o + out[:, g * ATT_HEAD_DIM:(g + 1) * ATT_HEAD_DIM]
        o_ref[0] = o.astype(o_ref.dtype)


def _dsa_decode_split(q, iq, iw, ik_new, k_new, v_new, cache_k, cache_v, cache_idx_k, page_table):
    b_, n_pages = page_table.shape
    n_pool, page = cache_k.shape[0], cache_k.shape[1]
    pages = PAGES_PER_STEP
    n_steps = n_pages // pages
    nk = pages * page
    past = n_pages * page
    ck = jnp.transpose(cache_k, (0, 2, 3, 1)).reshape(n_pool, ATT_KV_DIM, page)
    cv = jnp.transpose(cache_v, (0, 2, 3, 1)).reshape(n_pool, ATT_KV_DIM, page)
    cik = jnp.swapaxes(cache_idx_k, 1, 2)
    per_seq = lambda a: pl.BlockSpec((1,) + a.shape[1:], lambda b, s, pt: (b,) + (0,) * (a.ndim - 1))
    paged = lambda width, j: pl.BlockSpec((1, width, page), lambda b, s, pt: (pt[b, s * pages + j], 0, 0))
    key_blk = pl.BlockSpec((1, 1, nk), lambda b, s, pt: (b, 0, s))
    lane_blk = pl.BlockSpec((1, 1, 128), lambda b, s, pt: (b, 0, 0))
    params = pltpu.CompilerParams(dimension_semantics=("parallel", "arbitrary"),
                                  vmem_limit_bytes=V7X_VMEM_LIMIT_BYTES)

    score_in = [iq.reshape(b_, IDX_HEADS, IDX_DIM), iw[:, :IDX_HEADS].reshape(b_, IDX_HEADS, 1),
                ik_new.astype(MXU_DTYPE).reshape(b_, 1, IDX_DIM)]
    keys, key_new = pl.pallas_call(
        _decode_score_kernel,
        grid_spec=pltpu.PrefetchScalarGridSpec(
            num_scalar_prefetch=1, grid=(b_, n_steps),
            in_specs=[per_seq(a) for a in score_in] + [paged(IDX_DIM, j) for j in range(pages)],
            out_specs=[key_blk, lane_blk]),
        out_shape=[jax.ShapeDtypeStruct((b_, 1, past), jnp.int32), jax.ShapeDtypeStruct((b_, 1, 128), jnp.int32)],
        compiler_params=params,
        name="dsa_decode_score",
    )(page_table, *score_in, *([cik] * pages))

    whole = lambda shape: pl.BlockSpec(shape, lambda i: (0,) * len(shape))
    thr, last_tie = pl.pallas_call(
        functools.partial(_decode_select_kernel, topk=min(TOPK_MAX, (past + 1) // 4),
                          col_bits=max(1, past.bit_length())),
        grid=(1,),
        in_specs=[whole((b_, past)), whole((b_, 128))],
        out_specs=[whole((b_, 128)), whole((b_, 128))],
        out_shape=[jax.ShapeDtypeStruct((b_, 128), jnp.int32)] * 2,
        compiler_params=pltpu.CompilerParams(vmem_limit_bytes=V7X_VMEM_LIMIT_BYTES),
        name="dsa_decode_select",
    )(keys.reshape(b_, past), key_new.reshape(b_, 128))

    attend_in = [q.reshape(b_, ATT_HEADS, ATT_HEAD_DIM), k_new.astype(MXU_DTYPE).reshape(b_, 1, ATT_KV_DIM),
                 v_new.astype(MXU_DTYPE).reshape(b_, 1, ATT_KV_DIM)]
    o = pl.pallas_call(
        functools.partial(_decode_attend_kernel, n_steps=n_steps, pages=pages),
        grid_spec=pltpu.PrefetchScalarGridSpec(
            num_scalar_prefetch=1, grid=(b_, n_steps),
            in_specs=[per_seq(a) for a in attend_in] + [key_blk, lane_blk, lane_blk, lane_blk]
            + [paged(ATT_KV_DIM, j) for j in range(pages)] * 2,
            out_specs=pl.BlockSpec((1, ATT_HEADS, ATT_HEAD_DIM), lambda b, s, pt: (b, 0, 0)),
            scratch_shapes=[pltpu.VMEM((ATT_HEADS, 1), jnp.float32), pltpu.VMEM((ATT_HEADS, 1), jnp.float32),
                            pltpu.VMEM((ATT_HEADS, ATT_KV_DIM), jnp.float32)]),
        out_shape=jax.ShapeDtypeStruct((b_, ATT_HEADS, ATT_HEAD_DIM), MXU_DTYPE),
        compiler_params=params,
        name="dsa_decode_attend",
    )(page_table, *attend_in, keys, key_new, thr.reshape(b_, 1, 128), last_tie.reshape(b_, 1, 128),
      *([ck] * pages), *([cv] * pages))
    return o.reshape(b_, ATT_Q_DIM)


def _dsa_sample_pallas(x3d, cache_k, cache_v, cache_idx_k, page_table, w_in, kn_g, kn_b):
    b_, t_, _ = x3d.shape
    past = page_table.shape[1] * cache_k.shape[1]
    pos = jnp.full((b_,), past, jnp.int32)
    q, iq, v, _, iw, kT, _, ikT, _ = _dsa_project(x3d.reshape(1, b_, D_MODEL), pos, w_in, kn_g, kn_b)
    k = _untranspose_groups(kT)[0]
    ik = _untranspose_groups(ikT)[0]
    o = _dsa_decode_split(q, iq, iw, ik, k, v, cache_k, cache_v, cache_idx_k, page_table)
    kv4 = lambda u: u.reshape(b_, t_, ATT_KV_HEADS, ATT_HEAD_DIM)
    return o, kv4(k), kv4(v), ik.reshape(b_, t_, IDX_DIM)


def kernel(x_prompt, x_sample, state_ssm_conv, state_ssm, cache_k, cache_v, cache_idx_k, state_rwkv_shift, state_rwkv_wkv, page_table, p_prompt, p_sample, ln_g, ln_b, ffn_w_up, ffn_w_down, ple_w_p, ple_w_g, ple_b_g, gm_w_in, gm_ln_g, gm_ln_b, gm_ws, gm_bs, gm_w_out, ssm_w_in, ssm_conv_w, ssm_conv_b, ssm_dt_bias, ssm_a_log, ssm_d, ssm_norm_g, ssm_w_out, att_w_in, att_kn_g, att_kn_b, att_w_out, rw_mu, rw_w_r, rw_w_k, rw_w_v, rw_w_o, rw_w0, rw_w1, rw_w2, rw_a0, rw_a1, rw_a2, rw_g1, rw_g2, rw_k_k, rw_k_a, rw_r_k, rw_gn_g, rw_gn_b):
    bp, tp, _ = x_prompt.shape
    bs_, ts, _ = x_sample.shape
    bf = lambda w: w.astype(jnp.bfloat16)
    w_up_bf, w_down_bf = bf(ffn_w_up), bf(ffn_w_down)
    ple_wp_bf, ple_wg_bf = bf(ple_w_p), bf(ple_w_g)
    pp3 = p_prompt.reshape(DEPTH, bp * tp, PLE_DIM)
    ps3 = p_sample.reshape(DEPTH, bs_ * ts, PLE_DIM)

    yp = x_prompt.reshape(bp * tp, D_MODEL)
    ys = x_sample.reshape(bs_ * ts, D_MODEL)
    r3p = lambda t: t.reshape(bp, tp, -1)
    r3s = lambda t: t.reshape(bs_, ts, -1)
    f2 = lambda t: t.reshape(-1, t.shape[-1])

    for i in range(DEPTH):
        yp = _ffn_sub(yp, w_up_bf, w_down_bf, i, 0, ln_g[i, 0], ln_b[i, 0])
        ys = _ffn_sub(ys, w_up_bf, w_down_bf, i, 0, ln_g[i, 0], ln_b[i, 0])
        m = i % N_MIXERS
        if m == 0:
            gm_args = (gm_w_in, gm_ln_g, gm_ln_b, gm_ws, gm_bs, gm_w_out, ln_g[i, 1], ln_b[i, 1])
            yp, = _gmlp_block(yp, tp, *gm_args, False)
            ys, gm_v_s = _gmlp_block(ys, ts, *gm_args, True)
            gm_v_s = r3s(gm_v_s)
        elif m == 1:
            ssm_args = (ssm_w_in, ssm_conv_w, ssm_conv_b, ssm_dt_bias, ssm_a_log, ssm_d, ssm_norm_g)
            hp, conv_p, ssm_p = _mamba2_mixer_pallas(
                r3p(yp), jnp.zeros((bp, SSM_CONV - 1, SSM_CONV_DIM), yp.dtype),
                jnp.zeros((bp, SSM_HEADS, SSM_HEAD_DIM, SSM_STATE), yp.dtype), *ssm_args)
            hs, conv_s, ssm_s = _mamba2_mixer_pallas(r3s(ys), state_ssm_conv, state_ssm, *ssm_args)
            w_out = bf(ssm_w_out)
        elif m == 2:
            hp, k_p, v_p, ik_p = _dsa_prompt_pallas(r3p(yp), att_w_in, att_kn_g, att_kn_b)
            hs, k_s, v_s, ik_s = _dsa_sample_pallas(r3s(ys), cache_k, cache_v, cache_idx_k, page_table,
                                                    att_w_in, att_kn_g, att_kn_b)
            w_out = bf(att_w_out)
        else:
            rw_args = (rw_mu, rw_w_r, rw_w_k, rw_w_v, rw_w0, rw_w1, rw_w2, rw_a0, rw_a1, rw_a2,
                       rw_g1, rw_g2, rw_k_k, rw_k_a, rw_r_k, rw_gn_g, rw_gn_b)
            hp, gate_p, sh_p, wkv_p = _rwkv7_mixer_pallas(
                r3p(yp), jnp.zeros((bp, D_MODEL), yp.dtype),
                jnp.zeros((bp, RW_HEADS, RW_HEAD, RW_HEAD), yp.dtype), *rw_args)
            hs, gate_s, sh_s, wkv_s = _rwkv7_mixer_pallas(r3s(ys), state_rwkv_shift, state_rwkv_wkv, *rw_args)
            w_out = bf(rw_w_o)
        if m == 3:
            yp = _proj_gate_post_norm(yp, hp, gate_p, w_out, ln_g[i, 1], ln_b[i, 1])
            ys = _proj_gate_post_norm(ys, hs, gate_s, w_out, ln_g[i, 1], ln_b[i, 1])
        elif m != 0:
            yp = _proj_post_norm(yp, f2(hp), w_out, ln_g[i, 1], ln_b[i, 1])
            ys = _proj_post_norm(ys, f2(hs), w_out, ln_g[i, 1], ln_b[i, 1])
        yp = _ffn_sub(yp, w_up_bf, w_down_bf, i, 1, ln_g[i, 2], ln_b[i, 2], (pp3, ple_wp_bf, ple_wg_bf, ple_b_g))
        ys = _ffn_sub(ys, w_up_bf, w_down_bf, i, 1, ln_g[i, 2], ln_b[i, 2], (ps3, ple_wp_bf, ple_wg_bf, ple_b_g))

    return (r3p(yp), r3s(ys), gm_v_s, conv_p, ssm_p, conv_s, ssm_s, k_p, v_p, ik_p, k_s, v_s, ik_s,
            sh_p, wkv_p, sh_s, wkv_s)
```

```python
import functools

import jax
import jax.numpy as jnp
from jax import lax
from jax.experimental import pallas as pl
from jax.experimental.pallas import tpu as pltpu

D_MODEL = 1024
DEPTH = 4
N_MIXERS = 4
PLE_DIM = 256
D_FF = 2816
ALPHA = (2 * DEPTH) ** 0.25
LN_EPS = 1e-5

CHUNK = 128
GM_WIDTH = 2 * D_MODEL
GM_GROUPS = 8
GM_GROUP_DIM = GM_WIDTH // GM_GROUPS

SSM_D_INNER = 2 * D_MODEL
SSM_HEAD_DIM = 64
SSM_HEADS = SSM_D_INNER // SSM_HEAD_DIM
SSM_GROUPS = 4
SSM_HPG = SSM_HEADS // SSM_GROUPS
SSM_STATE = 128
SSM_CONV = 4
SSM_CONV_DIM = SSM_D_INNER + 2 * SSM_GROUPS * SSM_STATE
SSM_CHUNK = 128

ATT_HEADS = 16
ATT_KV_HEADS = 4
ATT_HEAD_DIM = D_MODEL // ATT_HEADS
ROPE_DIM = ATT_HEAD_DIM // 4
ROPE_THETA = 500000.0
IDX_HEADS = 8
IDX_DIM = 64
IDX_ROPE_DIM = IDX_DIM // 4
TOPK_MAX = 256
Q_BLOCK = 128
ATT_Q_DIM = ATT_HEADS * ATT_HEAD_DIM
ATT_KV_DIM = ATT_KV_HEADS * ATT_HEAD_DIM
ATT_IN_SPLITS = (ATT_Q_DIM, ATT_Q_DIM + ATT_KV_DIM, ATT_Q_DIM + 2 * ATT_KV_DIM,
                 ATT_Q_DIM + 2 * ATT_KV_DIM + IDX_HEADS * IDX_DIM,
                 ATT_Q_DIM + 2 * ATT_KV_DIM + IDX_HEADS * IDX_DIM + IDX_DIM)

RW_HEAD = 64
RW_HEADS = D_MODEL // RW_HEAD
RW_GN_EPS = 64e-5

V7X_VMEM_LIMIT_BYTES = 52 * 1024 * 1024
LANES = 128
SUBLANES = 8
FF_TILE = D_FF // 2
ROW_TILE = 512


def _row_tile(m):
    return ROW_TILE if m % ROW_TILE == 0 else m


def _ln_rows(y, g, b):
    mu = jnp.mean(y, axis=-1, keepdims=True)
    yc = y - mu
    var = jnp.mean(yc * yc, axis=-1, keepdims=True)
    return yc * lax.rsqrt(var + LN_EPS) * g + b


FFN_ROW_TILE = 1024
FFN_ROW_PARTS = 4


def _ffn_kernel(x_ref, wu_ref, wd_ref, g_ref, b_ref, *rest, parts, with_ple):
    o_ref = rest[-1]
    rows = x_ref.shape[0] // parts
    for p in range(parts):
        x = x_ref[p * rows:(p + 1) * rows, :]
        xb = x.astype(MXU_DTYPE)
        acc = None
        for f in range(D_FF // FF_TILE):
            cols = slice(f * FF_TILE, (f + 1) * FF_TILE)
            gate = jnp.dot(xb, wu_ref[:, cols], preferred_element_type=jnp.float32)
            lin = jnp.dot(xb, wu_ref[:, D_FF + f * FF_TILE:D_FF + (f + 1) * FF_TILE],
                          preferred_element_type=jnp.float32)
            h = (gate * jax.nn.sigmoid(gate) * lin).astype(MXU_DTYPE)
            part = jnp.dot(h, wd_ref[cols, :], preferred_element_type=jnp.float32)
            acc = part if acc is None else acc + part
        y = _ln_rows(ALPHA * x + 0.5 * acc, g_ref[...], b_ref[...])
        if with_ple:
            p_ref, wp_ref, wg_ref, bg_ref = rest[:4]
            gate = jax.nn.sigmoid(
                jnp.dot(y.astype(MXU_DTYPE), wg_ref[...], preferred_element_type=jnp.float32) + bg_ref[...])
            emb = jnp.dot(p_ref[p * rows:(p + 1) * rows, :].astype(MXU_DTYPE), wp_ref[...],
                          preferred_element_type=jnp.float32)
            y = y + gate * emb
        o_ref[p * rows:(p + 1) * rows, :] = y


def _ffn_sub(x2d, w_up, w_down, layer, half, g, b, ple=None):
    m = x2d.shape[0]
    tm = FFN_ROW_TILE if m % FFN_ROW_TILE == 0 else m
    parts = FFN_ROW_PARTS if tm == FFN_ROW_TILE else 1
    resident = dict(pipeline_mode=pl.Buffered(1))
    vec = pl.BlockSpec((1, D_MODEL), lambda i: (0, 0))
    in_specs = [
        pl.BlockSpec((tm, D_MODEL), lambda i: (i, 0)),
        pl.BlockSpec((None, None, D_MODEL, 2 * D_FF), lambda i: (layer, half, 0, 0), **resident),
        pl.BlockSpec((None, None, D_FF, D_MODEL), lambda i: (layer, half, 0, 0), **resident),
        vec, vec]
    args = [x2d, w_up, w_down, g.reshape(1, D_MODEL), b.reshape(1, D_MODEL)]
    if ple is not None:
        p3d, w_p, w_g, b_g = ple
        in_specs += [pl.BlockSpec((None, tm, PLE_DIM), lambda i: (layer, i, 0)),
                     pl.BlockSpec((None, PLE_DIM, D_MODEL), lambda i: (layer, 0, 0), **resident),
                     pl.BlockSpec((None, D_MODEL, D_MODEL), lambda i: (layer, 0, 0), **resident),
                     pl.BlockSpec((None, 1, D_MODEL), lambda i: (layer, 0, 0))]
        args += [p3d, w_p, w_g, b_g.reshape(DEPTH, 1, D_MODEL)]
    return pl.pallas_call(
        functools.partial(_ffn_kernel, parts=parts, with_ple=ple is not None),
        grid=(m // tm,),
        in_specs=in_specs,
        out_specs=pl.BlockSpec((tm, D_MODEL), lambda i: (i, 0)),
        out_shape=jax.ShapeDtypeStruct((m, D_MODEL), jnp.float32),
        compiler_params=pltpu.CompilerParams(
            dimension_semantics=("parallel",),
            vmem_limit_bytes=V7X_VMEM_LIMIT_BYTES),
        name="ffn_ple" if ple is not None else "ffn_sub",
    )(*args)


def _proj_ln_kernel(x_ref, h_ref, w_ref, g_ref, b_ref, o_ref):
    y = ALPHA * x_ref[...] + jnp.dot(h_ref[...].astype(jnp.bfloat16), w_ref[...],
                                     preferred_element_type=jnp.float32)
    o_ref[...] = _ln_rows(y, g_ref[...], b_ref[...])


def _proj_post_norm(x2d, h2d, w_out, g, b):
    m = x2d.shape[0]
    k = h2d.shape[1]
    tm = _row_tile(m)
    return pl.pallas_call(
        _proj_ln_kernel,
        grid=(m // tm,),
        in_specs=[
            pl.BlockSpec((tm, D_MODEL), lambda i: (i, 0)),
            pl.BlockSpec((tm, k), lambda i: (i, 0)),
            pl.BlockSpec((k, D_MODEL), lambda i: (0, 0)),
            pl.BlockSpec((1, D_MODEL), lambda i: (0, 0)),
            pl.BlockSpec((1, D_MODEL), lambda i: (0, 0)),
        ],
        out_specs=pl.BlockSpec((tm, D_MODEL), lambda i: (i, 0)),
        out_shape=jax.ShapeDtypeStruct((m, D_MODEL), jnp.float32),
        compiler_params=pltpu.CompilerParams(
            dimension_semantics=("parallel",),
            vmem_limit_bytes=V7X_VMEM_LIMIT_BYTES),
        name="proj_post_norm",
    )(x2d, h2d, w_out, g.reshape(1, D_MODEL), b.reshape(1, D_MODEL))


MXU_DTYPE = jnp.bfloat16
KEY_GROUP = 512
INT32_MIN = -2 ** 31
MASK_NEG = -1e30


def _rope_lane_tables(pos, rot_dim, head_dim):
    half = rot_dim // 2
    inv = ROPE_THETA ** (-jnp.arange(half, dtype=jnp.float32) / half)
    ang = pos.astype(jnp.float32)[:, None] * inv[None, :]
    cos, sin = jnp.cos(ang), jnp.sin(ang)
    n = pos.shape[0]
    rest = head_dim - rot_dim
    c = jnp.concatenate([cos, cos, jnp.ones((n, rest), jnp.float32)], axis=1)
    s1 = jnp.concatenate([-sin, jnp.zeros((n, half + rest), jnp.float32)], axis=1)
    s2 = jnp.concatenate([jnp.zeros((n, half), jnp.float32), sin, jnp.zeros((n, rest), jnp.float32)], axis=1)
    reps = LANES // head_dim
    tile = lambda t: jnp.tile(t, (1, reps))
    return tile(c), tile(s1), tile(s2), cos.T, sin.T


def _rope_lanes(t, c, s1, s2, half):
    n = t.shape[1]
    reps = n // LANES
    tl = lambda a: jnp.concatenate([a] * reps, axis=1)
    return t * tl(c) + pltpu.roll(t, n - half, 1) * tl(s1) + pltpu.roll(t, half, 1) * tl(s2)


def _rope_rows(t, cT, sT, head_dim, half):
    pieces = []
    for h in range(t.shape[0] // head_dim):
        x1 = t[h * head_dim:h * head_dim + half]
        x2 = t[h * head_dim + half:h * head_dim + 2 * half]
        pieces += [x1 * cT - x2 * sT, x2 * cT + x1 * sT, t[h * head_dim + 2 * half:(h + 1) * head_dim]]
    return jnp.concatenate(pieces, axis=0)


def _dsa_proj_kernel(x_ref, wq_ref, wiq_ref, wv_ref, wvx_ref, wiw_ref, wkT_ref, wikT_ref,
                     c_ref, s1_ref, s2_ref, cT_ref, sT_ref, kng_ref, knb_ref, one_ref,
                     q_ref, iq_ref, v_ref, vx_ref, iw_ref, kT_ref, kTb_ref, ikT_ref, ikTb_ref):
    xb = x_ref[...].astype(MXU_DTYPE)
    c, s1, s2 = c_ref[...], s1_ref[...], s2_ref[...]
    cT, sT = cT_ref[...], sT_ref[...]
    dot = lambda a, b: jnp.dot(a, b, preferred_element_type=jnp.float32)
    dot_t = lambda w, a: lax.dot_general(w, a, (((1,), (1,)), ((), ())), preferred_element_type=jnp.float32)

    q = _rope_lanes(dot(xb, wq_ref[...]), c, s1, s2, ROPE_DIM // 2)
    q_ref[...] = (q * (ATT_HEAD_DIM ** -0.5)).astype(q_ref.dtype)
    iq = _rope_lanes(dot(xb, wiq_ref[...]), c, s1, s2, IDX_ROPE_DIM // 2)
    iq_ref[...] = iq.astype(iq_ref.dtype)
    v_ref[...] = dot(xb, wv_ref[...])
    vx_ref[...] = (dot(xb, wvx_ref[...]) + one_ref[...]).astype(vx_ref.dtype)
    iw_ref[...] = dot(xb, wiw_ref[...]) * (IDX_HEADS ** -0.5 * IDX_DIM ** -0.5)

    kT = _rope_rows(dot_t(wkT_ref[...], xb), cT, sT, ATT_HEAD_DIM, ROPE_DIM // 2)
    kT_ref[0, 0] = kT
    kTb_ref[0, 0] = kT.astype(kTb_ref.dtype)
    ikT = dot_t(wikT_ref[...], xb)
    mu = jnp.mean(ikT, axis=0, keepdims=True)
    ikc = ikT - mu
    var = jnp.mean(ikc * ikc, axis=0, keepdims=True)
    ikT = ikc * lax.rsqrt(var + LN_EPS) * kng_ref[...] + knb_ref[...]
    ikT = _rope_rows(ikT, cT, sT, IDX_DIM, IDX_ROPE_DIM // 2)
    ikT_ref[0, 0] = ikT
    ikTb_ref[0, 0] = ikT.astype(ikTb_ref.dtype)


def _dsa_project(x3d, pos, w_in, kn_g, kn_b):
    b_, t_, _ = x3d.shape
    tk = KEY_GROUP if t_ % KEY_GROUP == 0 else t_
    ng = t_ // tk
    m = b_ * t_
    w_q, w_k, w_v, w_iq, w_ik, w_iw = jnp.split(w_in, list(ATT_IN_SPLITS), axis=1)
    cast = lambda w: w.astype(MXU_DTYPE)
    w_vx = jnp.pad(w_v.reshape(D_MODEL, ATT_KV_HEADS, ATT_HEAD_DIM),
                   ((0, 0), (0, 0), (0, LANES - ATT_HEAD_DIM))).reshape(D_MODEL, ATT_KV_HEADS * LANES)
    one_col = jnp.tile((jnp.arange(LANES) == ATT_HEAD_DIM).astype(jnp.float32), ATT_KV_HEADS)[None, :]
    w_iw_pad = jnp.pad(w_iw, ((0, 0), (0, LANES - IDX_HEADS)))
    c, s1, s2, cT, sT = _rope_lane_tables(pos, ROPE_DIM, ATT_HEAD_DIM)
    full = lambda shape: pl.BlockSpec(shape, lambda b, i: (0,) * len(shape))
    rows = lambda n: pl.BlockSpec((tk, n), lambda b, i: (b * ng + i, 0))
    ptab = lambda n: pl.BlockSpec((tk, n), lambda b, i: (i, 0))
    grp = lambda n: pl.BlockSpec((1, 1, n, tk), lambda b, i: (b, i, 0, 0))
    sds = jax.ShapeDtypeStruct
    return pl.pallas_call(
        _dsa_proj_kernel,
        grid=(b_, ng),
        in_specs=[rows(D_MODEL), full((D_MODEL, ATT_Q_DIM)), full((D_MODEL, IDX_HEADS * IDX_DIM)),
                  full((D_MODEL, ATT_KV_DIM)), full((D_MODEL, ATT_KV_HEADS * LANES)), full((D_MODEL, LANES)),
                  full((ATT_KV_DIM, D_MODEL)), full((IDX_DIM, D_MODEL)),
                  ptab(LANES), ptab(LANES), ptab(LANES),
                  pl.BlockSpec((ROPE_DIM // 2, tk), lambda b, i: (0, i)),
                  pl.BlockSpec((ROPE_DIM // 2, tk), lambda b, i: (0, i)),
                  full((IDX_DIM, 1)), full((IDX_DIM, 1)), full((1, ATT_KV_HEADS * LANES))],
        out_specs=[rows(ATT_Q_DIM), rows(IDX_HEADS * IDX_DIM), rows(ATT_KV_DIM), rows(ATT_KV_HEADS * LANES),
                   rows(LANES), grp(ATT_KV_DIM), grp(ATT_KV_DIM), grp(IDX_DIM), grp(IDX_DIM)],
        out_shape=[sds((m, ATT_Q_DIM), MXU_DTYPE), sds((m, IDX_HEADS * IDX_DIM), MXU_DTYPE),
                   sds((m, ATT_KV_DIM), jnp.float32), sds((m, ATT_KV_HEADS * LANES), MXU_DTYPE),
                   sds((m, LANES), jnp.float32),
                   sds((b_, ng, ATT_KV_DIM, tk), jnp.float32), sds((b_, ng, ATT_KV_DIM, tk), MXU_DTYPE),
                   sds((b_, ng, IDX_DIM, tk), jnp.float32), sds((b_, ng, IDX_DIM, tk), MXU_DTYPE)],
        compiler_params=pltpu.CompilerParams(
            dimension_semantics=("parallel", "parallel"),
            vmem_limit_bytes=V7X_VMEM_LIMIT_BYTES),
        name="dsa_project",
    )(x3d.reshape(m, D_MODEL), cast(w_q), cast(w_iq), cast(w_v), cast(w_vx), cast(w_iw_pad),
      cast(w_k.T), cast(w_ik.T), c, s1, s2, cT, sT, kn_g.reshape(IDX_DIM, 1), kn_b.reshape(IDX_DIM, 1), one_col)


def _untranspose_groups(tg):
    b_, g_, r_, tk = tg.shape
    return jnp.transpose(tg, (0, 1, 3, 2)).reshape(b_, g_ * tk, r_)


def _dsa_proj_q_lanes_kernel(x_ref, wqT_ref, wiqT_ref, wiwT_ref, wk_ref, wv_ref, wvxT_ref, wik_ref,
                             c_ref, s1_ref, s2_ref, cT_ref, sT_ref, kng_ref, knb_ref, onerow_ref,
                             qT_ref, iqT_ref, iwT_ref, k_ref, khd_ref, v_ref, vxT_ref, ik_ref, ikb_ref):
    xb = x_ref[...].astype(MXU_DTYPE)
    tm = xb.shape[0]
    c, s1, s2 = c_ref[...], s1_ref[...], s2_ref[...]
    cT, sT = cT_ref[...], sT_ref[...]
    dot = lambda a, b: jnp.dot(a, b, preferred_element_type=jnp.float32)
    dot_t = lambda w, a: lax.dot_general(w, a, (((1,), (1,)), ((), ())), preferred_element_type=jnp.float32)

    qT = _rope_rows(dot_t(wqT_ref[...], xb), cT, sT, ATT_HEAD_DIM, ROPE_DIM // 2) * (ATT_HEAD_DIM ** -0.5)
    iqT = _rope_rows(dot_t(wiqT_ref[...], xb), cT, sT, IDX_DIM, IDX_ROPE_DIM // 2)
    iwT = dot_t(wiwT_ref[...], xb) * (IDX_HEADS ** -0.5 * IDX_DIM ** -0.5)
    for t in range(tm // Q_BLOCK):
        lanes = slice(t * Q_BLOCK, (t + 1) * Q_BLOCK)
        qT_ref[0, t] = qT[:, lanes].astype(qT_ref.dtype)
        iqT_ref[0, t] = iqT[:, lanes].astype(iqT_ref.dtype)
        iwT_ref[0, t] = iwT[:, lanes]

    k = _rope_lanes(dot(xb, wk_ref[...]), c, s1, s2, ROPE_DIM // 2)
    k_ref[...] = k
    for g in range(ATT_KV_HEADS):
        khd_ref[g] = k[:, g * ATT_HEAD_DIM:(g + 1) * ATT_HEAD_DIM].astype(khd_ref.dtype)
    v_ref[...] = dot(xb, wv_ref[...])
    vxT_ref[0, 0] = (dot_t(wvxT_ref[...], xb) + onerow_ref[...]).astype(vxT_ref.dtype)

    ik = dot(xb, wik_ref[...])
    real = lax.broadcasted_iota(jnp.int32, ik.shape, 1) < IDX_DIM
    mu = jnp.sum(ik, axis=-1, keepdims=True) * (1.0 / IDX_DIM)
    ikc = jnp.where(real, ik - mu, 0.0)
    var = jnp.sum(ikc * ikc, axis=-1, keepdims=True) * (1.0 / IDX_DIM)
    ikn = _rope_lanes(ikc * lax.rsqrt(var + LN_EPS) * kng_ref[...] + knb_ref[...], c, s1, s2, IDX_ROPE_DIM // 2)
    ik_ref[...] = ikn[:, :IDX_DIM]
    ikb_ref[...] = ikn[:, :IDX_DIM].astype(ikb_ref.dtype)


def _dsa_project_q_lanes(x3d, pos, w_in, kn_g, kn_b):
    b_, t_, _ = x3d.shape
    tk = KEY_GROUP
    ng = t_ // tk
    nq = tk // Q_BLOCK
    m = b_ * t_
    w_q, w_k, w_v, w_iq, w_ik, w_iw = jnp.split(w_in, list(ATT_IN_SPLITS), axis=1)
    cast = lambda w: w.astype(MXU_DTYPE)
    w_vxT = jnp.pad(w_v.T.reshape(ATT_KV_HEADS, ATT_HEAD_DIM, D_MODEL),
                    ((0, 0), (0, LANES - ATT_HEAD_DIM), (0, 0))).reshape(ATT_KV_HEADS * LANES, D_MODEL)
    one_row = jnp.tile((jnp.arange(LANES) == ATT_HEAD_DIM).astype(jnp.float32), ATT_KV_HEADS)[:, None]
    pad_lanes = lambda a: jnp.pad(a, ((0, 0), (0, LANES - a.shape[1])))
    c, s1, s2, cT, sT = _rope_lane_tables(pos, ROPE_DIM, ATT_HEAD_DIM)
    full = lambda shape: pl.BlockSpec(shape, lambda b, i: (0,) * len(shape))
    rows = lambda n: pl.BlockSpec((tk, n), lambda b, i: (b * ng + i, 0))
    ptab = lambda n: pl.BlockSpec((tk, n), lambda b, i: (i, 0))
    qtile = lambda n: pl.BlockSpec((1, nq, n, Q_BLOCK), lambda b, i: (b, i, 0, 0))
    sds = jax.ShapeDtypeStruct
    return pl.pallas_call(
        _dsa_proj_q_lanes_kernel,
        grid=(b_, ng),
        in_specs=[rows(D_MODEL), full((ATT_Q_DIM, D_MODEL)), full((IDX_HEADS * IDX_DIM, D_MODEL)),
                  full((IDX_HEADS, D_MODEL)), full((D_MODEL, ATT_KV_DIM)), full((D_MODEL, ATT_KV_DIM)),
                  full((ATT_KV_HEADS * LANES, D_MODEL)), full((D_MODEL, LANES)),
                  ptab(LANES), ptab(LANES), ptab(LANES),
                  pl.BlockSpec((ROPE_DIM // 2, tk), lambda b, i: (0, i)),
                  pl.BlockSpec((ROPE_DIM // 2, tk), lambda b, i: (0, i)),
                  full((1, LANES)), full((1, LANES)), full((ATT_KV_HEADS * LANES, 1))],
        out_specs=[qtile(ATT_Q_DIM), qtile(IDX_HEADS * IDX_DIM), qtile(IDX_HEADS),
                   rows(ATT_KV_DIM), pl.BlockSpec((ATT_KV_HEADS, tk, ATT_HEAD_DIM), lambda b, i: (0, b * ng + i, 0)),
                   rows(ATT_KV_DIM), pl.BlockSpec((1, 1, ATT_KV_HEADS * LANES, tk), lambda b, i: (b, i, 0, 0)),
                   rows(IDX_DIM), rows(IDX_DIM)],
        out_shape=[sds((b_, t_ // Q_BLOCK, ATT_Q_DIM, Q_BLOCK), MXU_DTYPE),
                   sds((b_, t_ // Q_BLOCK, IDX_HEADS * IDX_DIM, Q_BLOCK), MXU_DTYPE),
                   sds((b_, t_ // Q_BLOCK, IDX_HEADS, Q_BLOCK), jnp.float32),
                   sds((m, ATT_KV_DIM), jnp.float32), sds((ATT_KV_HEADS, m, ATT_HEAD_DIM), MXU_DTYPE),
                   sds((m, ATT_KV_DIM), jnp.float32), sds((b_, ng, ATT_KV_HEADS * LANES, tk), MXU_DTYPE),
                   sds((m, IDX_DIM), jnp.float32), sds((m, IDX_DIM), MXU_DTYPE)],
        compiler_params=pltpu.CompilerParams(
            dimension_semantics=("parallel", "parallel"),
            vmem_limit_bytes=V7X_VMEM_LIMIT_BYTES),
        name="dsa_project_q_lanes",
    )(x3d.reshape(m, D_MODEL), cast(w_q.T), cast(w_iq.T), cast(w_iw.T), cast(w_k), cast(w_v), cast(w_vxT),
      cast(pad_lanes(w_ik)), c, s1, s2, cT, sT, pad_lanes(kn_g.reshape(1, IDX_DIM)),
      pad_lanes(kn_b.reshape(1, IDX_DIM)), one_row)


def _tree_sum(parts):
    while len(parts) > 1:
        parts = [parts[i] + parts[i + 1] for i in range(0, len(parts) - 1, 2)] + (
            [parts[-1]] if len(parts) % 2 else [])
    return parts[0]


def _dsa_attend_q_lanes_kernel(iqT_ref, iwT_ref, ik_ref, qT_ref, k_ref, vxT_ref, o_ref,
                               key_ref, bias_ref, m_ref, acc_ref, s_ref, *, topk, col_bits):
    j = pl.program_id(1)
    tk, tq = key_ref.shape[1], key_ref.shape[2]
    n_groups = (j * tq + tq + tk - 1) // tk
    qpos = j * tq + lax.broadcasted_iota(jnp.int32, (tk, tq), 1)
    kpos0 = lax.broadcasted_iota(jnp.int32, (tk, tq), 0)
    dot = lambda a, b: jnp.dot(a, b, preferred_element_type=jnp.float32)

    def score_body(g, carry):
        start = pl.multiple_of(g * tk, tk)
        w_iq = jnp.concatenate([iqT_ref[0, 0, h * IDX_DIM:(h + 1) * IDX_DIM, :] for h in range(IDX_HEADS)], axis=1)
        s_all = dot(ik_ref[0, pl.ds(start, tk), :], w_iq)
        sc = _tree_sum([iwT_ref[0, 0, h:h + 1, :] * jnp.maximum(s_all[:, h * tq:(h + 1) * tq], 0.0)
                        for h in range(IDX_HEADS)])
        key_ref[g] = jnp.where(kpos0 + g * tk <= qpos, _sortable_key(sc), jnp.int32(INT32_MIN))
        return carry

    lax.fori_loop(0, n_groups, score_body, 0)

    def count_keys(pred):
        def body(g, part):
            hit = jnp.where(pred(key_ref[g], kpos0 + g * tk), 1.0, 0.0)
            return part + _tree_sum([hit[r * SUBLANES:(r + 1) * SUBLANES] for r in range(tk // SUBLANES)])
        part = lax.fori_loop(0, n_groups, body, jnp.zeros((SUBLANES, tq), jnp.float32))
        return jnp.sum(part, axis=0, keepdims=True)

    def bit_body(i, thr):
        cand = thr ^ lax.shift_left(jnp.int32(1), jnp.int32(31) - i)
        return jnp.where(count_keys(lambda k, kp: k >= cand) >= float(topk), cand, thr)

    thr = lax.fori_loop(0, 32, bit_body, jnp.full((1, tq), INT32_MIN, jnp.int32))

    need = float(topk) - count_keys(lambda k, kp: k > thr)

    def pos_body(i, last):
        cand = last | lax.shift_left(jnp.int32(1), jnp.int32(col_bits - 1) - i)
        return jnp.where(count_keys(lambda k, kp: (k == thr) & (kp < cand)) < need, cand, last)

    n_tied = count_keys(lambda k, kp: k == thr)
    excess = jnp.max(jnp.where(n_tied > need, 1.0, 0.0), axis=1, keepdims=True)
    last_tie = lax.cond(excess[0, 0] > 0.0,
                        lambda: lax.fori_loop(0, col_bits, pos_body, jnp.zeros((1, tq), jnp.int32)),
                        lambda: jnp.full((1, tq), 2 ** col_bits - 1, jnp.int32))

    m_ref[...] = jnp.full(m_ref.shape, MASK_NEG, jnp.float32)
    acc_ref[...] = jnp.zeros(acc_ref.shape, jnp.float32)
    gsz = ATT_HEADS // ATT_KV_HEADS

    def attend_body(g, carry):
        start = pl.multiple_of(g * tk, tk)
        key = key_ref[g]
        kpos = kpos0 + g * tk
        keep = (key > thr) | ((key == thr) & (kpos <= last_tie))
        bias_ref[...] = jnp.where(keep & (kpos <= qpos), 0.0, MASK_NEG)
        for kv in range(ATT_KV_HEADS):
            w_q = jnp.concatenate([qT_ref[0, 0, (kv * gsz + i) * ATT_HEAD_DIM:(kv * gsz + i + 1) * ATT_HEAD_DIM, :]
                                   for i in range(gsz)], axis=1)
            s_ref[kv] = dot(k_ref[kv, pl.ds(start, tk), :], w_q)
        for kv in range(ATT_KV_HEADS):
            s = s_ref[kv] + jnp.concatenate([bias_ref[...]] * gsz, axis=1)
            m_old = m_ref[kv]
            m_new = jnp.maximum(m_old, jnp.max(s, axis=0, keepdims=True))
            p = jnp.exp(s - m_new).astype(vxT_ref.dtype)
            pv = dot(vxT_ref[0, g, kv * LANES:(kv + 1) * LANES, :], p)
            acc_ref[kv] = jnp.exp(m_old - m_new) * acc_ref[kv] + pv
            m_ref[kv] = m_new
        return carry

    lax.fori_loop(0, n_groups, attend_body, 0)

    for h in range(ATT_HEADS):
        a = acc_ref[h // gsz, :, (h % gsz) * tq:(h % gsz + 1) * tq]
        o = (a / a[ATT_HEAD_DIM:ATT_HEAD_DIM + 1, :]).T
        o_ref[:, h * ATT_HEAD_DIM:(h + 1) * ATT_HEAD_DIM] = o[:, :ATT_HEAD_DIM].astype(o_ref.dtype)


def _dsa_attend_q_lanes(b_, t_, qT, iqT, iwT, ikb, khd, vxT):
    ng, tk = vxT.shape[1], vxT.shape[3]
    tq = Q_BLOCK
    nq = t_ // tq
    qtile = lambda n: pl.BlockSpec((1, 1, n, tq), lambda b, j: (b, j, 0, 0))
    return pl.pallas_call(
        functools.partial(_dsa_attend_q_lanes_kernel, topk=min(TOPK_MAX, t_ // 4),
                          col_bits=max(1, (t_ - 1).bit_length())),
        grid=(b_, nq),
        in_specs=[qtile(IDX_HEADS * IDX_DIM), qtile(IDX_HEADS),
                  pl.BlockSpec((1, t_, IDX_DIM), lambda b, j: (b, 0, 0)),
                  qtile(ATT_Q_DIM),
                  pl.BlockSpec((ATT_KV_HEADS, t_, ATT_HEAD_DIM), lambda b, j: (0, b, 0)),
                  pl.BlockSpec((1, ng, ATT_KV_HEADS * LANES, tk), lambda b, j: (b, 0, 0, 0))],
        out_specs=pl.BlockSpec((tq, ATT_Q_DIM), lambda b, j: (b * nq + j, 0)),
        out_shape=jax.ShapeDtypeStruct((b_ * t_, ATT_Q_DIM), MXU_DTYPE),
        scratch_shapes=[pltpu.VMEM((ng, tk, tq), jnp.int32),
                        pltpu.VMEM((tk, tq), jnp.float32),
                        pltpu.VMEM((ATT_KV_HEADS, 1, tq * (ATT_HEADS // ATT_KV_HEADS)), jnp.float32),
                        pltpu.VMEM((ATT_KV_HEADS, LANES, tq * (ATT_HEADS // ATT_KV_HEADS)), jnp.float32),
                        pltpu.VMEM((ATT_KV_HEADS, tk, tq * (ATT_HEADS // ATT_KV_HEADS)), jnp.float32)],
        compiler_params=pltpu.CompilerParams(
            dimension_semantics=("parallel", "arbitrary"),
            vmem_limit_bytes=V7X_VMEM_LIMIT_BYTES),
        name="dsa_attend_q_lanes",
    )(iqT, iwT, ikb.reshape(b_, t_, IDX_DIM), qT, khd, vxT)


def _dsa_prompt_pallas(x3d, w_in, kn_g, kn_b):
    b_, t_, _ = x3d.shape
    qT, iqT, iwT, k, khd, v, vxT, ik, ikb = _dsa_project_q_lanes(x3d, jnp.arange(t_), w_in, kn_g, kn_b)
    o = _dsa_attend_q_lanes(b_, t_, qT, iqT, iwT, ikb, khd, vxT)
    kv4 = lambda u: u.reshape(b_, t_, ATT_KV_HEADS, ATT_HEAD_DIM)
    return o, kv4(k), kv4(v), ik.reshape(b_, t_, IDX_DIM)


RW_ROW_TILE = 256


RW_PAIRS = RW_HEADS // 2
RW_PAIR_LANES = 2 * RW_HEAD


def _rwkv_project_rows(x, xp, mu_ref, wr_ref, wk_ref, wv_ref, w1_ref, w2_ref, a1_ref, a2_ref,
                       g1_ref, g2_ref, w0_ref, a0_ref):
    dx = xp - x
    mix = lambda c: (x + dx * mu_ref[c:c + 1, :]).astype(MXU_DTYPE)
    dot = lambda a, b: jnp.dot(a.astype(MXU_DTYPE), b, preferred_element_type=jnp.float32)
    r = dot(mix(0), wr_ref[...])
    lora_w = dot(jnp.tanh(dot(mix(1), w1_ref[...])), w2_ref[...])
    w_log = -jax.nn.softplus(-(w0_ref[...] + lora_w)) - 0.5
    d = jnp.exp(-jnp.exp(w_log))
    k = dot(mix(2), wk_ref[...])
    v = dot(mix(3), wv_ref[...])
    a = jax.nn.sigmoid(a0_ref[...] + dot(dot(mix(4), a1_ref[...]), a2_ref[...]))
    g = dot(jax.nn.sigmoid(dot(mix(5), g1_ref[...])), g2_ref[...])
    return r, d, k, v, a, g


def _rwkv_proj_step_kernel(x_ref, xp_ref, *refs):
    vals = _rwkv_project_rows(x_ref[...], xp_ref[...], *refs[:12])
    for ref, val in zip(refs[12:], vals):
        ref[...] = val


def _rwkv_proj_seq_kernel(x_ref, halo_ref, shift_ref, *refs):
    i = pl.program_id(1)
    x = x_ref[...]
    prev = jnp.where(i == 0, shift_ref[0], halo_ref[...])[SUBLANES - 1:SUBLANES, :]
    first = lax.broadcasted_iota(jnp.int32, (x.shape[0], 1), 0) == 0
    xp = jnp.where(first, prev, pltpu.roll(x, 1, 0))
    vals = _rwkv_project_rows(x, xp, *refs[:12])
    for ref, val in zip(refs[12:], vals):
        ref[...] = val


def _rwkv_consts(mu, w_r, w_k, w_v, w0, w1, w2, a0, a1, a2, g1, g2):
    cast = lambda w: w.astype(MXU_DTYPE)
    return [mu, cast(w_r), cast(w_k), cast(w_v), cast(w1), cast(w2), cast(a1), cast(a2), cast(g1), cast(g2),
            w0.reshape(1, D_MODEL), a0.reshape(1, D_MODEL)]


def _rwkv_project_step(x2d, xprev2d, *params):
    m = x2d.shape[0]
    consts = _rwkv_consts(*params)
    full = lambda a: pl.BlockSpec(a.shape, lambda i: (0,) * a.ndim)
    rows = pl.BlockSpec((m, D_MODEL), lambda i: (0, 0))
    return pl.pallas_call(
        _rwkv_proj_step_kernel,
        grid=(1,),
        in_specs=[rows, rows] + [full(a) for a in consts],
        out_specs=[rows] * 6,
        out_shape=[jax.ShapeDtypeStruct((m, D_MODEL), jnp.float32)] * 6,
        compiler_params=pltpu.CompilerParams(
            dimension_semantics=("arbitrary",),
            vmem_limit_bytes=V7X_VMEM_LIMIT_BYTES),
        name="rwkv_project_step",
    )(x2d, xprev2d, *consts)


def _rwkv_project_seq(x3d, shift, *params):
    b_, t_, _ = x3d.shape
    m = b_ * t_
    tm = RW_ROW_TILE
    nt = t_ // tm
    consts = _rwkv_consts(*params)
    full = lambda a: pl.BlockSpec(a.shape, lambda b, i: (0,) * a.ndim)
    rows = pl.BlockSpec((tm, D_MODEL), lambda b, i: (b * nt + i, 0))
    halo = pl.BlockSpec((SUBLANES, D_MODEL), lambda b, i: (jnp.maximum((b * nt + i) * (tm // SUBLANES) - 1, 0), 0))
    shift8 = jnp.pad(shift[:, None, :], ((0, 0), (SUBLANES - 1, 0), (0, 0)))
    x2d = x3d.reshape(m, D_MODEL)
    return pl.pallas_call(
        _rwkv_proj_seq_kernel,
        grid=(b_, nt),
        in_specs=[rows, halo, pl.BlockSpec((1, SUBLANES, D_MODEL), lambda b, i: (b, 0, 0))]
        + [full(a) for a in consts],
        out_specs=[rows] * 6,
        out_shape=[jax.ShapeDtypeStruct((m, D_MODEL), jnp.float32)] * 6,
        compiler_params=pltpu.CompilerParams(
            dimension_semantics=("parallel", "parallel"),
            vmem_limit_bytes=V7X_VMEM_LIMIT_BYTES),
        name="rwkv_project_seq",
    )(x2d, x2d, shift8, *consts)


RW_LANES = LANES
RW_TIME_CHUNK = 64


def _rwkv_scan_kernel(r_ref, d_ref, k_ref, v_ref, a_ref, s0_ref, kk_ref, ka_ref, rk_ref, gg_ref, gb_ref,
                      z_ref, s_out_ref, s_ref, vec_ref):
    c = pl.program_id(1)
    n = RW_HEAD
    tc = r_ref.shape[1]
    low_half = lax.broadcasted_iota(jnp.int32, (n, RW_LANES), 1) < n

    @pl.when(c == 0)
    def _():
        s_ref[...] = s0_ref[...]

    def swap_layout(x):
        xt = jnp.concatenate([x, x], axis=0).T
        return jnp.where(low_half, xt[:n], xt[n:])

    def load_step(ref, t):
        rows = ref[:, t, :]
        return swap_layout(jnp.concatenate(
            [rows[:, p * RW_PAIR_LANES:(p + 1) * RW_PAIR_LANES] for p in range(RW_PAIRS)], axis=0))

    def store_step(ref, t, val):
        tile = swap_layout(val)
        ref[:, t, :] = jnp.concatenate(
            [tile[p * RW_SEQ_PER_TILE:(p + 1) * RW_SEQ_PER_TILE] for p in range(RW_PAIRS)], axis=1)

    def prepare(t, slot):
        r, k, a = load_step(r_ref, t), load_step(k_ref, t), load_step(a_ref, t)
        kkr = k * kk_ref[...]
        nrm = jnp.sqrt(jnp.sum(kkr * kkr, axis=0, keepdims=True))
        kk = kkr / jnp.maximum(nrm, 1e-12)
        vec_ref[slot, 0] = kk
        vec_ref[slot, 1] = load_step(d_ref, t)
        vec_ref[slot, 2] = kk * a
        vec_ref[slot, 3] = k * (1.0 + (a - 1.0) * ka_ref[...])
        vec_ref[slot, 4] = r
        vec_ref[slot, 5] = load_step(v_ref, t)

    def step(t, slot):
        row = lambda q, j: vec_ref[slot, q, j:j + 1, :]
        v = vec_ref[slot, 5]
        lanes = 4
        sa_parts = [s_ref[j] * row(0, j) for j in range(lanes)]
        for j in range(lanes, n):
            sa_parts[j % lanes] = sa_parts[j % lanes] + s_ref[j] * row(0, j)
        sa = _tree_sum(sa_parts)
        y_parts = []
        for j in range(n):
            sn = s_ref[j] * row(1, j) - sa * row(2, j) + v * row(3, j)
            s_ref[j] = sn
            if j < lanes:
                y_parts.append(sn * row(4, j))
            else:
                y_parts[j % lanes] = y_parts[j % lanes] + sn * row(4, j)
        y = _tree_sum(y_parts)
        mu = jnp.mean(y, axis=0, keepdims=True)
        yc = y - mu
        var = jnp.mean(yc * yc, axis=0, keepdims=True)
        bonus = jnp.sum(vec_ref[slot, 4] * vec_ref[slot, 3] * rk_ref[...], axis=0, keepdims=True)
        store_step(z_ref, t, yc * lax.rsqrt(var + RW_GN_EPS) * gg_ref[...] + gb_ref[...] + bonus * v)

    prepare(0, 0)
    if tc == 1:
        step(0, 0)
    else:
        def two_steps(i, carry):
            t = 2 * i
            prepare(t + 1, 1)
            step(t, 0)
            prepare(jnp.minimum(t + 2, tc - 1), 0)
            step(t + 1, 1)
            return carry

        lax.fori_loop(0, tc // 2, two_steps, 0)

    @pl.when(c == pl.num_programs(1) - 1)
    def _():
        s_out_ref[...] = s_ref[...]


RW_SEQ_PER_TILE = RW_LANES // RW_HEADS


def _rwkv_lane_heads():
    half = jnp.arange(2)[:, None, None]
    pair = jnp.arange(RW_PAIRS)[None, :, None]
    return jnp.broadcast_to(2 * pair + half, (2, RW_PAIRS, RW_SEQ_PER_TILE)).reshape(RW_LANES)


def _rwkv_scan(r, d, k, v, a, s0, k_k, k_a, r_k, gn_g, gn_b):
    b_, t_, _ = r.shape
    n = RW_HEAD
    tc = RW_TIME_CHUNK if t_ % RW_TIME_CHUNK == 0 else t_
    table = lambda p: p.reshape(RW_HEADS, n)[_rwkv_lane_heads()].T
    seq = pl.BlockSpec((RW_SEQ_PER_TILE, tc, D_MODEL), lambda l, c: (l, c, 0))
    state = pl.BlockSpec((n, n, RW_LANES), lambda l, c: (0, 0, l))
    tab = pl.BlockSpec((n, RW_LANES), lambda l, c: (0, 0))
    return pl.pallas_call(
        _rwkv_scan_kernel,
        grid=(b_ // RW_SEQ_PER_TILE, t_ // tc),
        in_specs=[seq] * 5 + [state] + [tab] * 5,
        out_specs=[seq, state],
        out_shape=[jax.ShapeDtypeStruct(r.shape, jnp.float32),
                   jax.ShapeDtypeStruct(s0.shape, jnp.float32)],
        scratch_shapes=[pltpu.VMEM((n, n, RW_LANES), jnp.float32),
                        pltpu.VMEM((2, 6, n, RW_LANES), jnp.float32)],
        compiler_params=pltpu.CompilerParams(
            dimension_semantics=("parallel", "arbitrary"),
            vmem_limit_bytes=V7X_VMEM_LIMIT_BYTES),
        name="rwkv_scan",
    )(r, d, k, v, a, s0, table(k_k), table(k_a), table(r_k), table(gn_g), table(gn_b))


def _rwkv_state_to_lanes(wkv):
    b_ = wkv.shape[0]
    w = wkv.astype(jnp.float32).reshape(b_ // RW_SEQ_PER_TILE, RW_SEQ_PER_TILE, RW_PAIRS, 2, RW_HEAD, RW_HEAD)
    return jnp.transpose(w, (5, 4, 0, 3, 2, 1)).reshape(RW_HEAD, RW_HEAD, b_ * RW_HEADS)


def _rwkv_state_from_lanes(s, b_):
    w = s.reshape(RW_HEAD, RW_HEAD, b_ // RW_SEQ_PER_TILE, 2, RW_PAIRS, RW_SEQ_PER_TILE)
    return jnp.transpose(w, (2, 5, 4, 3, 1, 0)).reshape(b_, RW_HEADS, RW_HEAD, RW_HEAD)


def _rwkv7_mixer_pallas(x3d, shift, wkv, mu, w_r, w_k, w_v, w0, w1, w2, a0, a1, a2, g1, g2,
                        k_k, k_a, r_k, gn_g, gn_b):
    b_, t_, _ = x3d.shape
    params = (mu, w_r, w_k, w_v, w0, w1, w2, a0, a1, a2, g1, g2)
    if t_ == 1:
        *seqs, g = _rwkv_project_step(x3d.reshape(b_, D_MODEL), shift, *params)
    else:
        *seqs, g = _rwkv_project_seq(x3d, shift, *params)
    seqs = [u.reshape(b_, t_, D_MODEL) for u in seqs]
    z, s = _rwkv_scan(*seqs, _rwkv_state_to_lanes(wkv), k_k, k_a, r_k, gn_g, gn_b)
    return z.reshape(b_ * t_, D_MODEL), g, x3d[:, -1], _rwkv_state_from_lanes(s, b_).astype(wkv.dtype)


def _proj_gate_ln_kernel(x_ref, h_ref, gate_ref, w_ref, g_ref, b_ref, o_ref):
    h = (h_ref[...] * gate_ref[...]).astype(MXU_DTYPE)
    y = ALPHA * x_ref[...] + jnp.dot(h, w_ref[...], preferred_element_type=jnp.float32)
    o_ref[...] = _ln_rows(y, g_ref[...], b_ref[...])


def _proj_gate_post_norm(x2d, h2d, gate2d, w_out, g, b):
    m = x2d.shape[0]
    tm = _row_tile(m)
    rows = pl.BlockSpec((tm, D_MODEL), lambda i: (i, 0))
    vec = pl.BlockSpec((1, D_MODEL), lambda i: (0, 0))
    return pl.pallas_call(
        _proj_gate_ln_kernel,
        grid=(m // tm,),
        in_specs=[rows, rows, rows, pl.BlockSpec((D_MODEL, D_MODEL), lambda i: (0, 0)), vec, vec],
        out_specs=rows,
        out_shape=jax.ShapeDtypeStruct((m, D_MODEL), jnp.float32),
        compiler_params=pltpu.CompilerParams(
            dimension_semantics=("parallel",),
            vmem_limit_bytes=V7X_VMEM_LIMIT_BYTES),
        name="proj_gate_post_norm",
    )(x2d, h2d, gate2d, w_out, g.reshape(1, D_MODEL), b.reshape(1, D_MODEL))


GM_ROW_TILE = 512
GM_ROW_PARTS = 2


def _gmlp_kernel(x_ref, win_ref, lng_ref, lnb_ref, mixw_ref, mixb_ref, wout_ref, g_ref, b_ref, *out_refs,
                 chunk_len, emit_v, parts):
    tp = x_ref.shape[0] // parts
    for part in range(parts):
        rows_p = slice(part * tp, (part + 1) * tp)
        x = x_ref[rows_p, :]
        h = jax.nn.gelu(jnp.dot(x.astype(MXU_DTYPE), win_ref[...], preferred_element_type=jnp.float32))
        u = h[:, :GM_WIDTH]
        v = _ln_rows(h[:, GM_WIDTH:], lng_ref[...], lnb_ref[...])
        if emit_v:
            out_refs[1][rows_p, :] = v
        if chunk_len == 1:
            gated = u * (v * mixw_ref[...] + mixb_ref[...])
        else:
            causal = (lax.broadcasted_iota(jnp.int32, (chunk_len, chunk_len), 0)
                      >= lax.broadcasted_iota(jnp.int32, (chunk_len, chunk_len), 1))
            vb = v.astype(MXU_DTYPE)
            cols = []
            for g in range(GM_GROUPS):
                w = jnp.where(causal, mixw_ref[g], 0.0).astype(MXU_DTYPE)
                bias = mixb_ref[:, g:g + 1]
                lanes = slice(g * GM_GROUP_DIM, (g + 1) * GM_GROUP_DIM)
                rows = [jnp.dot(w, vb[c * chunk_len:(c + 1) * chunk_len, lanes],
                                preferred_element_type=jnp.float32) + bias
                        for c in range(tp // chunk_len)]
                cols.append(jnp.concatenate(rows, axis=0))
            gated = u * jnp.concatenate(cols, axis=1)
        y = ALPHA * x + jnp.dot(gated.astype(MXU_DTYPE), wout_ref[...], preferred_element_type=jnp.float32)
        out_refs[0][rows_p, :] = _ln_rows(y, g_ref[...], b_ref[...])


def _gmlp_block(x2d, seq_len, w_in, ln_g, ln_b, ws, bs, w_out, g, b, emit_v):
    m = x2d.shape[0]
    chunk_len = min(seq_len, CHUNK)
    if chunk_len == 1:
        tm, parts = m, 1
        mixw = jnp.repeat(ws[:, 0, 0], GM_GROUP_DIM)[None, :]
        mixb = jnp.repeat(bs[:, 0], GM_GROUP_DIM)[None, :]
    else:
        tm, parts = GM_ROW_TILE, GM_ROW_PARTS
        mixw = ws[:, :chunk_len, :chunk_len]
        mixb = bs[:, :chunk_len].T
    full = lambda a: pl.BlockSpec(a.shape, lambda i: (0,) * a.ndim, pipeline_mode=pl.Buffered(1))
    rows = lambda n: pl.BlockSpec((tm, n), lambda i: (i, 0))
    consts = [w_in.astype(MXU_DTYPE), ln_g.reshape(1, GM_WIDTH), ln_b.reshape(1, GM_WIDTH), mixw, mixb,
              w_out.astype(MXU_DTYPE), g.reshape(1, D_MODEL), b.reshape(1, D_MODEL)]
    out_specs = [rows(D_MODEL)] + ([rows(GM_WIDTH)] if emit_v else [])
    out_shape = [jax.ShapeDtypeStruct((m, D_MODEL), jnp.float32)] + (
        [jax.ShapeDtypeStruct((m, GM_WIDTH), jnp.float32)] if emit_v else [])
    return pl.pallas_call(
        functools.partial(_gmlp_kernel, chunk_len=chunk_len, emit_v=emit_v, parts=parts),
        grid=(m // tm,),
        in_specs=[rows(D_MODEL)] + [full(a) for a in consts],
        out_specs=out_specs,
        out_shape=out_shape,
        compiler_params=pltpu.CompilerParams(
            dimension_semantics=("parallel",),
            vmem_limit_bytes=V7X_VMEM_LIMIT_BYTES),
        name="gmlp_block",
    )(x2d, *consts)


SSM_ROW_TILE = 256
SSM_ROW_PARTS = 2
SSM_BC_DIM = SSM_GROUPS * SSM_STATE
SSM_DT_LANES = LANES


def _ssm_activate(xb, xbc, taps, wz_ref, wdt_ref, cw_ref, cb_ref, dtb_ref, z_ref, xs_ref, bm_ref, cm_ref, dt_ref):
    conv = cb_ref[...] + xbc * cw_ref[SSM_CONV - 1:SSM_CONV, :]
    for j in range(SSM_CONV - 1):
        conv = conv + taps[j] * cw_ref[j:j + 1, :]
    act = conv * jax.nn.sigmoid(conv)
    xs_ref[...] = act[:, :SSM_D_INNER]
    bm_ref[...] = act[:, SSM_D_INNER:SSM_D_INNER + SSM_BC_DIM].astype(bm_ref.dtype)
    cm_ref[...] = act[:, SSM_D_INNER + SSM_BC_DIM:].astype(cm_ref.dtype)
    z_ref[...] = jnp.dot(xb, wz_ref[...], preferred_element_type=jnp.float32)
    dt_ref[...] = jax.nn.softplus(jnp.dot(xb, wdt_ref[...], preferred_element_type=jnp.float32) + dtb_ref[...])


def _ssm_proj_seq_kernel(x_ref, halo_ref, cs_ref, wx_ref, wz_ref, wdt_ref, cw_ref, cb_ref, dtb_ref,
                         z_ref, xs_ref, bm_ref, cm_ref, dt_ref, tail_ref):
    i = pl.program_id(1)
    tm = x_ref.shape[0]
    tp = tm // SSM_ROW_PARTS
    prev = jnp.dot(halo_ref[...].astype(MXU_DTYPE), wx_ref[...], preferred_element_type=jnp.float32)
    prev = jnp.where(i == 0, cs_ref[0], prev)
    row = lax.broadcasted_iota(jnp.int32, (SUBLANES, 1), 0)
    for part in range(SSM_ROW_PARTS):
        rows_p = slice(part * tp, (part + 1) * tp)
        xb = x_ref[rows_p, :].astype(MXU_DTYPE)
        xbc = jnp.dot(xb, wx_ref[...], preferred_element_type=jnp.float32)
        taps = []
        for j in range(SSM_CONV - 1):
            back = SSM_CONV - 1 - j
            rolled = pltpu.roll(xbc, back, 0)
            top = jnp.where(row < back, pltpu.roll(prev, back, 0), rolled[:SUBLANES])
            taps.append(jnp.concatenate([top, rolled[SUBLANES:]], axis=0))
        _ssm_activate(xb, xbc, taps, wz_ref, wdt_ref, cw_ref, cb_ref, dtb_ref,
                      z_ref.at[rows_p, :], xs_ref.at[rows_p, :], bm_ref.at[rows_p, :], cm_ref.at[rows_p, :],
                      dt_ref.at[rows_p, :])
        prev = xbc[tp - SUBLANES:, :]
    tail_ref[0] = prev


def _ssm_proj_step_kernel(x_ref, st_ref, wx_ref, wz_ref, wdt_ref, cw_ref, cb_ref, dtb_ref,
                          z_ref, xs_ref, bm_ref, cm_ref, dt_ref, st_out_ref):
    xb = x_ref[...].astype(MXU_DTYPE)
    xbc = jnp.dot(xb, wx_ref[...], preferred_element_type=jnp.float32)
    taps = [st_ref[j] for j in range(SSM_CONV - 1)]
    _ssm_activate(xb, xbc, taps, wz_ref, wdt_ref, cw_ref, cb_ref, dtb_ref, z_ref, xs_ref, bm_ref, cm_ref, dt_ref)
    for j in range(SSM_CONV - 2):
        st_out_ref[j] = st_ref[j + 1]
    st_out_ref[SSM_CONV - 2] = xbc


def _ssm_project(x3d, conv_state, w_in, conv_w, conv_b, dt_bias):
    b_, t_, _ = x3d.shape
    m = b_ * t_
    w_z, w_x, w_dt = jnp.split(w_in, [SSM_D_INNER, SSM_D_INNER + SSM_CONV_DIM], axis=1)
    cast = lambda w: w.astype(MXU_DTYPE)
    consts = [cast(w_x), cast(w_z), cast(jnp.pad(w_dt, ((0, 0), (0, SSM_DT_LANES - SSM_HEADS)))),
              conv_w, conv_b.reshape(1, SSM_CONV_DIM),
              jnp.pad(dt_bias, (0, SSM_DT_LANES - SSM_HEADS)).reshape(1, SSM_DT_LANES)]
    sds = jax.ShapeDtypeStruct
    outs = [sds((m, SSM_D_INNER), jnp.float32), sds((m, SSM_D_INNER), jnp.float32),
            sds((m, SSM_BC_DIM), MXU_DTYPE), sds((m, SSM_BC_DIM), MXU_DTYPE), sds((m, SSM_DT_LANES), jnp.float32)]
    widths = [SSM_D_INNER, SSM_D_INNER, SSM_BC_DIM, SSM_BC_DIM, SSM_DT_LANES]
    params = dict(vmem_limit_bytes=V7X_VMEM_LIMIT_BYTES)
    x2d = x3d.reshape(m, D_MODEL)
    if t_ == 1:
        full = lambda a: pl.BlockSpec(a.shape, lambda i: (0,) * a.ndim)
        st = jnp.transpose(conv_state, (1, 0, 2))
        res = pl.pallas_call(
            _ssm_proj_step_kernel,
            grid=(1,),
            in_specs=[full(x2d), full(st)] + [full(a) for a in consts],
            out_specs=[pl.BlockSpec((m, w), lambda i: (0, 0)) for w in widths] + [full(st)],
            out_shape=outs + [sds(st.shape, jnp.float32)],
            compiler_params=pltpu.CompilerParams(dimension_semantics=("arbitrary",), **params),
            name="ssm_project_step",
        )(x2d, st, *consts)
        return list(res[:5]) + [jnp.transpose(res[5], (1, 0, 2))]
    tm = SSM_ROW_TILE
    nt = t_ // tm
    full = lambda a: pl.BlockSpec(a.shape, lambda b, i: (0,) * a.ndim)
    rows = lambda w: pl.BlockSpec((tm, w), lambda b, i: (b * nt + i, 0))
    halo = pl.BlockSpec((SUBLANES, D_MODEL), lambda b, i: (jnp.maximum((b * nt + i) * (tm // SUBLANES) - 1, 0), 0))
    cs8 = jnp.pad(conv_state, ((0, 0), (SUBLANES - (SSM_CONV - 1), 0), (0, 0)))
    tail = pl.BlockSpec((1, SUBLANES, SSM_CONV_DIM), lambda b, i: (b, 0, 0))
    res = pl.pallas_call(
        _ssm_proj_seq_kernel,
        grid=(b_, nt),
        in_specs=[rows(D_MODEL), halo, tail] + [full(a) for a in consts],
        out_specs=[rows(w) for w in widths] + [tail],
        out_shape=outs + [sds((b_, SUBLANES, SSM_CONV_DIM), jnp.float32)],
        compiler_params=pltpu.CompilerParams(dimension_semantics=("parallel", "arbitrary"), **params),
        name="ssm_project_seq",
    )(x2d, x2d, cs8, *consts)
    return list(res[:5]) + [res[5][:, SUBLANES - (SSM_CONV - 1):, :]]


def _ssm_gate_norm(y, xs, z, dskip, normg):
    yg = (y + xs * dskip) * (z * jax.nn.sigmoid(z))
    gw = SSM_D_INNER // SSM_GROUPS
    outs = []
    for g in range(SSM_GROUPS):
        part = yg[:, g * gw:(g + 1) * gw]
        ms = jnp.mean(part * part, axis=-1, keepdims=True)
        outs.append(part * lax.rsqrt(ms + LN_EPS))
    return jnp.concatenate(outs, axis=1) * normg


def _ssm_chunk_kernel(xs_ref, bm_ref, cm_ref, dt_ref, z_ref, aneg_ref, dskip_ref, normg_ref,
                      yg_ref, h_out_ref, h_ref, yT_ref, xe_ref):
    c = pl.program_id(1)
    l = xs_ref.shape[0]
    hd = SSM_HEAD_DIM

    @pl.when(c == 0)
    def _():
        h_ref[...] = jnp.zeros_like(h_ref)

    dot = lambda u, w: jnp.dot(u, w, preferred_element_type=jnp.float32)
    dt = dt_ref[...]
    a = dt * aneg_ref[...]
    r_i = lax.broadcasted_iota(jnp.int32, (l, l), 0)
    c_i = lax.broadcasted_iota(jnp.int32, (l, l), 1)
    tril = jnp.where(r_i >= c_i, 1.0, 0.0)
    hi = lax.Precision.HIGHEST
    acum = jnp.dot(tril, a, precision=hi, preferred_element_type=jnp.float32)
    acum_t = jnp.dot(a.T, tril.T, precision=hi, preferred_element_type=jnp.float32)
    dt_t = dt.T
    to_end_t = jnp.exp(acum_t[:, l - 1:l] - acum_t)
    from_start_t = jnp.exp(acum_t)
    chunk_decay = jnp.exp(acum[l - 1:l, :])
    upper = r_i <= c_i
    xs = xs_ref[...]
    for g in range(SSM_GROUPS):
        bm = bm_ref[:, g * SSM_STATE:(g + 1) * SSM_STATE]
        cm_t = cm_ref[:, g * SSM_STATE:(g + 1) * SSM_STATE].astype(jnp.float32).T.astype(MXU_DTYPE)
        cb_t = dot(bm, cm_t)
        h_in = h_ref[g * SSM_HPG:(g + 1) * SSM_HPG].reshape(SSM_HPG * hd, SSM_STATE)
        y_off = dot(h_in.astype(MXU_DTYPE), cm_t)
        for e in range(SSM_HPG):
            h = g * SSM_HPG + e
            if h % 2 == 0:
                xs_pair_t = xs[:, h * hd:(h + 2) * hd].T
            xdt_t = xs_pair_t[(h % 2) * hd:(h % 2 + 1) * hd] * dt_t[h:h + 1, :]
            seg = jnp.exp(jnp.where(upper, acum_t[h:h + 1, :] - acum[:, h:h + 1], -jnp.inf))
            y_diag = dot(xdt_t.astype(MXU_DTYPE), (cb_t * seg).astype(MXU_DTYPE))
            yT_ref[h * hd:(h + 1) * hd, :] = y_diag + y_off[e * hd:(e + 1) * hd] * from_start_t[h:h + 1, :]
            xe_ref[e * hd:(e + 1) * hd, :] = (xdt_t * to_end_t[h:h + 1, :]).astype(xe_ref.dtype)
        states = dot(xe_ref[...], bm)
        for e in range(SSM_HPG):
            h = g * SSM_HPG + e
            h_ref[h] = h_ref[h] * chunk_decay[:, h:h + 1] + states[e * hd:(e + 1) * hd]
    y = jnp.concatenate([yT_ref[i * l:(i + 1) * l, :].T for i in range(SSM_D_INNER // l)], axis=1)
    yg_ref[...] = _ssm_gate_norm(y, xs, z_ref[...], dskip_ref[...], normg_ref[...]).astype(yg_ref.dtype)

    @pl.when(c == pl.num_programs(1) - 1)
    def _():
        h_out_ref[0] = h_ref[...]


def _ssm_head_lanes(p):
    return jnp.pad(p.astype(jnp.float32), (0, SSM_DT_LANES - SSM_HEADS)).reshape(1, SSM_DT_LANES)


def _ssm_chunk_scan(b_, t_, xs, bm, cm, dt, z, a_log, d_skip, norm_g):
    l = SSM_CHUNK
    nc = t_ // l
    rows = lambda w: pl.BlockSpec((l, w), lambda b, c: (b * nc + c, 0))
    vec = lambda w: pl.BlockSpec((1, w), lambda b, c: (0, 0))
    aneg = _ssm_head_lanes(-jnp.exp(a_log.astype(jnp.float32)))
    dskip = jnp.repeat(d_skip, SSM_HEAD_DIM).reshape(1, SSM_D_INNER)
    yg, h_new = pl.pallas_call(
        _ssm_chunk_kernel,
        grid=(b_, nc),
        in_specs=[rows(SSM_D_INNER), rows(SSM_BC_DIM), rows(SSM_BC_DIM), rows(SSM_DT_LANES), rows(SSM_D_INNER),
                  vec(SSM_DT_LANES), vec(SSM_D_INNER), vec(SSM_D_INNER)],
        out_specs=[rows(SSM_D_INNER),
                   pl.BlockSpec((1, SSM_HEADS, SSM_HEAD_DIM, SSM_STATE), lambda b, c: (b, 0, 0, 0))],
        out_shape=[jax.ShapeDtypeStruct((b_ * t_, SSM_D_INNER), MXU_DTYPE),
                   jax.ShapeDtypeStruct((b_, SSM_HEADS, SSM_HEAD_DIM, SSM_STATE), jnp.float32)],
        scratch_shapes=[pltpu.VMEM((SSM_HEADS, SSM_HEAD_DIM, SSM_STATE), jnp.float32),
                        pltpu.VMEM((SSM_D_INNER, l), jnp.float32),
                        pltpu.VMEM((SSM_HPG * SSM_HEAD_DIM, l), MXU_DTYPE)],
        compiler_params=pltpu.CompilerParams(
            dimension_semantics=("parallel", "arbitrary"),
            vmem_limit_bytes=V7X_VMEM_LIMIT_BYTES),
        name="ssm_chunk_scan",
    )(xs, bm, cm, dt, z, aneg, dskip, norm_g.reshape(1, SSM_D_INNER))
    return yg, h_new


def _ssm_step_kernel(h0_ref, xs_ref, dt_ref, an_ref, bm_ref, cm_ref, y_ref, h_ref):
    h0 = h0_ref[0]
    dt = dt_ref[0]
    decay = jnp.exp(dt * an_ref[...])
    xdt = xs_ref[0] * dt
    bm = bm_ref[0].astype(jnp.float32)
    cm = cm_ref[0].astype(jnp.float32)
    h_ref[0] = h0 * decay + xdt * bm
    cb = jnp.sum(cm * bm, axis=-1, keepdims=True)
    y_ref[0] = cb * xdt + jnp.sum(cm * h0, axis=-1, keepdims=True) * decay


def _ssm_step(state, xs, bm, cm, dt, a_log):
    b_ = state.shape[0]
    per_head = lambda u: jnp.repeat(u.reshape(b_, SSM_GROUPS, 1, SSM_STATE), SSM_HPG, axis=1)
    xs4 = xs.reshape(b_, SSM_HEADS, SSM_HEAD_DIM, 1)
    dt4 = dt[:, :SSM_HEADS].reshape(b_, SSM_HEADS, 1, 1)
    an = (-jnp.exp(a_log.astype(jnp.float32))).reshape(SSM_HEADS, 1, 1)
    blk = lambda a: pl.BlockSpec((1,) + a.shape[1:], lambda b: (b, 0, 0, 0))
    args = [state.astype(jnp.float32), xs4, dt4, an, per_head(bm), per_head(cm)]
    y4, h_new = pl.pallas_call(
        _ssm_step_kernel,
        grid=(b_,),
        in_specs=[blk(args[0]), blk(xs4), blk(dt4), pl.BlockSpec(an.shape, lambda b: (0, 0, 0)),
                  blk(args[4]), blk(args[5])],
        out_specs=[blk(xs4), blk(args[0])],
        out_shape=[jax.ShapeDtypeStruct(xs4.shape, jnp.float32), jax.ShapeDtypeStruct(state.shape, jnp.float32)],
        compiler_params=pltpu.CompilerParams(
            dimension_semantics=("parallel",),
            vmem_limit_bytes=V7X_VMEM_LIMIT_BYTES),
        name="ssm_step",
    )(*args)
    return y4.reshape(b_, SSM_D_INNER), h_new


def _ssm_gate_norm_kernel(y_ref, xs_ref, z_ref, dskip_ref, normg_ref, o_ref):
    o_ref[...] = _ssm_gate_norm(y_ref[...], xs_ref[...], z_ref[...], dskip_ref[...], normg_ref[...]).astype(o_ref.dtype)


def _ssm_gate_norm_rows(y, xs, z, d_skip, norm_g):
    full = lambda a: pl.BlockSpec(a.shape, lambda i: (0,) * a.ndim)
    args = [y, xs, z, jnp.repeat(d_skip, SSM_HEAD_DIM).reshape(1, SSM_D_INNER), norm_g.reshape(1, SSM_D_INNER)]
    return pl.pallas_call(
        _ssm_gate_norm_kernel,
        grid=(1,),
        in_specs=[full(a) for a in args],
        out_specs=full(y),
        out_shape=jax.ShapeDtypeStruct(y.shape, MXU_DTYPE),
        name="ssm_gate_norm",
    )(*args)


def _mamba2_mixer_pallas(x3d, conv_state, ssm_state, w_in, conv_w, conv_b, dt_bias, a_log, d_skip, norm_g):
    b_, t_, _ = x3d.shape
    z, xs, bm, cm, dt, conv_new = _ssm_project(x3d, conv_state, w_in, conv_w, conv_b, dt_bias)
    if t_ == 1:
        y, h_new = _ssm_step(ssm_state, xs, bm, cm, dt, a_log)
        yg = _ssm_gate_norm_rows(y, xs, z, d_skip, norm_g)
    else:
        yg, h_new = _ssm_chunk_scan(b_, t_, xs, bm, cm, dt, z, a_log, d_skip, norm_g)
    return yg, conv_new, h_new.astype(ssm_state.dtype)


PAGES_PER_STEP = 8


def _sortable_key(score):
    bits = pltpu.bitcast(score, jnp.int32)
    return jnp.where(bits >= 0, bits, bits ^ jnp.int32(0x7FFFFFFF))


def _decode_score_kernel(pt_ref, iq_ref, iw_ref, ikn_ref, *rest):
    idx_refs, (key_ref, knew_ref) = rest[:-2], rest[-2:]
    iq, iw = iq_ref[0], iw_ref[0]
    weigh = lambda sc: jnp.sum(iw * jnp.maximum(sc, 0.0), axis=0, keepdims=True)
    ik_t = jnp.concatenate([r[0] for r in idx_refs], axis=1).astype(MXU_DTYPE)
    key_ref[0] = _sortable_key(weigh(jnp.dot(iq, ik_t, preferred_element_type=jnp.float32)))

    @pl.when(pl.program_id(1) == 0)
    def _():
        sc_new = jnp.sum(iq.astype(jnp.float32) * ikn_ref[0].astype(jnp.float32), axis=1, keepdims=True)
        knew_ref[0] = jnp.broadcast_to(_sortable_key(weigh(sc_new)), knew_ref.shape[1:])


def _decode_select_kernel(keys_ref, knew_ref, thr_ref, last_ref, *, topk, col_bits):
    keys = keys_ref[...]
    key_new = knew_ref[:, 0:1]
    past = keys.shape[1]
    col = lax.broadcasted_iota(jnp.int32, keys.shape, 1)

    def count(pred_past, pred_new):
        hit = jnp.where(pred_past(keys, col), 1.0, 0.0)
        cnt = _tree_sum([hit[:, l * LANES:(l + 1) * LANES] for l in range(past // LANES)])
        return jnp.sum(cnt, axis=1, keepdims=True) + jnp.where(pred_new(key_new), 1.0, 0.0)

    def at_least(cand):
        return count(lambda k, c: k >= cand, lambda k: k >= cand) >= float(topk)

    def two_bits(i, thr):
        hi = lax.shift_left(jnp.int32(1), jnp.int32(31) - 2 * i)
        lo = lax.shift_left(jnp.int32(1), jnp.int32(30) - 2 * i)
        c1, c2, c3 = thr ^ lo, thr ^ hi, thr ^ hi ^ lo
        return jnp.where(at_least(c3), c3, jnp.where(at_least(c2), c2, jnp.where(at_least(c1), c1, thr)))

    thr = lax.fori_loop(0, 16, two_bits, jnp.full(key_new.shape, INT32_MIN, jnp.int32))
    need = float(topk) - count(lambda k, c: k > thr, lambda k: k > thr)

    def col_body(i, last):
        cand = last | lax.shift_left(jnp.int32(1), jnp.int32(col_bits - 1) - i)
        ties = count(lambda k, c: (k == thr) & (c < cand), lambda k: (k == thr) & (jnp.int32(past) < cand))
        return jnp.where(ties < need, cand, last)

    n_tied = count(lambda k, c: k == thr, lambda k: k == thr)
    excess = jnp.max(jnp.where(n_tied > need, 1.0, 0.0), axis=0, keepdims=True)
    last_tie = lax.cond(excess[0, 0] > 0.0,
                        lambda: lax.fori_loop(0, col_bits, col_body, jnp.zeros(key_new.shape, jnp.int32)),
                        lambda: jnp.full(key_new.shape, 2 ** col_bits - 1, jnp.int32))
    thr_ref[...] = jnp.broadcast_to(thr, thr_ref.shape)
    last_ref[...] = jnp.broadcast_to(last_tie, last_ref.shape)


def _decode_attend_kernel(pt_ref, q_ref, kn_ref, vn_ref, key_ref, knew_ref, thr_ref, last_ref, *rest,
                          n_steps, pages):
    k_refs, v_refs = rest[:pages], rest[pages:2 * pages]
    o_ref, m_ref, l_ref, acc_ref = rest[2 * pages:]
    s = pl.program_id(1)
    nk = key_ref.shape[2]
    nt = (((1,), (1,)), ((), ()))
    thr, last_tie = thr_ref[0, :, 0:1], last_ref[0, :, 0:1]
    keep_mask = lambda key, col: (key > thr) | ((key == thr) & (col <= last_tie))
    gsz = ATT_HEADS // ATT_KV_HEADS
    q = q_ref[0]
    q_wide = jnp.concatenate([q] * ATT_KV_HEADS, axis=1)
    head_i = lax.broadcasted_iota(jnp.int32, q_wide.shape, 0)
    col_i = lax.broadcasted_iota(jnp.int32, q_wide.shape, 1)
    own_group = (col_i // ATT_HEAD_DIM) == (head_i // gsz)
    q_blk = jnp.where(own_group, q_wide, jnp.zeros_like(q_wide))

    def online_update(logits, weighted_values):
        m_old = m_ref[...]
        m_new = jnp.maximum(m_old, jnp.max(logits, axis=1, keepdims=True))
        p = jnp.exp(logits - m_new)
        alpha = jnp.exp(m_old - m_new)
        l_ref[...] = alpha * l_ref[...] + jnp.sum(p, axis=1, keepdims=True)
        acc_ref[...] = alpha * acc_ref[...] + weighted_values(p.astype(MXU_DTYPE))
        m_ref[...] = m_new

    @pl.when(s == 0)
    def _init():
        m_ref[...] = jnp.full(m_ref.shape, MASK_NEG, jnp.float32)
        l_ref[...] = jnp.zeros(l_ref.shape, jnp.float32)
        acc_ref[...] = jnp.zeros(acc_ref.shape, jnp.float32)

    col = lax.broadcasted_iota(jnp.int32, (1, nk), 1) + s * nk
    bias = jnp.where(keep_mask(key_ref[0], col), 0.0, MASK_NEG)
    k_t = jnp.concatenate([r[0] for r in k_refs], axis=1).astype(MXU_DTYPE)
    v_t = jnp.concatenate([r[0] for r in v_refs], axis=1).astype(MXU_DTYPE)
    online_update(jnp.dot(q_blk, k_t, preferred_element_type=jnp.float32) + bias,
                  lambda p: lax.dot_general(p, v_t, nt, preferred_element_type=jnp.float32))

    @pl.when(s == n_steps - 1)
    def _finish():
        keep_new = keep_mask(knew_ref[0, :, 0:1], jnp.int32(n_steps * nk))
        logit = jnp.sum(q_blk.astype(jnp.float32) * kn_ref[0].astype(jnp.float32), axis=1, keepdims=True)
        v_row = vn_ref[0].astype(jnp.float32)
        online_update(logit + jnp.where(keep_new, 0.0, MASK_NEG), lambda p: p.astype(jnp.float32) * v_row)
        out = jnp.where(own_group, acc_ref[...] / l_ref[...], 0.0)
        o = out[:, 0:ATT_HEAD_DIM]
        for g in range(1, ATT_KV_HEADS):
            o = o + out[:, g * ATT_HEAD_DIM:(g + 1) * ATT_HEAD_DIM]
        o_ref[0] = o.astype(o_ref.dtype)


def _dsa_decode_split(q, iq, iw, ik_new, k_new, v_new, cache_k, cache_v, cache_idx_k, page_table):
    b_, n_pages = page_table.shape
    n_pool, page = cache_k.shape[0], cache_k.shape[1]
    pages = PAGES_PER_STEP
    n_steps = n_pages // pages
    nk = pages * page
    past = n_pages * page
    ck = jnp.transpose(cache_k, (0, 2, 3, 1)).reshape(n_pool, ATT_KV_DIM, page)
    cv = jnp.transpose(cache_v, (0, 2, 3, 1)).reshape(n_pool, ATT_KV_DIM, page)
    cik = jnp.swapaxes(cache_idx_k, 1, 2)
    per_seq = lambda a: pl.BlockSpec((1,) + a.shape[1:], lambda b, s, pt: (b,) + (0,) * (a.ndim - 1))
    paged = lambda width, j: pl.BlockSpec((1, width, page), lambda b, s, pt: (pt[b, s * pages + j], 0, 0))
    key_blk = pl.BlockSpec((1, 1, nk), lambda b, s, pt: (b, 0, s))
    lane_blk = pl.BlockSpec((1, 1, LANES), lambda b, s, pt: (b, 0, 0))
    params = pltpu.CompilerParams(dimension_semantics=("parallel", "arbitrary"),
                                  vmem_limit_bytes=V7X_VMEM_LIMIT_BYTES)

    score_in = [iq.reshape(b_, IDX_HEADS, IDX_DIM), iw[:, :IDX_HEADS].reshape(b_, IDX_HEADS, 1),
                ik_new.astype(MXU_DTYPE).reshape(b_, 1, IDX_DIM)]
    keys, key_new = pl.pallas_call(
        _decode_score_kernel,
        grid_spec=pltpu.PrefetchScalarGridSpec(
            num_scalar_prefetch=1, grid=(b_, n_steps),
            in_specs=[per_seq(a) for a in score_in] + [paged(IDX_DIM, j) for j in range(pages)],
            out_specs=[key_blk, lane_blk]),
        out_shape=[jax.ShapeDtypeStruct((b_, 1, past), jnp.int32), jax.ShapeDtypeStruct((b_, 1, LANES), jnp.int32)],
        compiler_params=params,
        name="dsa_decode_score",
    )(page_table, *score_in, *([cik] * pages))

    whole = lambda shape: pl.BlockSpec(shape, lambda i: (0,) * len(shape))
    thr, last_tie = pl.pallas_call(
        functools.partial(_decode_select_kernel, topk=min(TOPK_MAX, (past + 1) // 4),
                          col_bits=max(1, past.bit_length())),
        grid=(1,),
        in_specs=[whole((b_, past)), whole((b_, LANES))],
        out_specs=[whole((b_, LANES)), whole((b_, LANES))],
        out_shape=[jax.ShapeDtypeStruct((b_, LANES), jnp.int32)] * 2,
        compiler_params=pltpu.CompilerParams(vmem_limit_bytes=V7X_VMEM_LIMIT_BYTES),
        name="dsa_decode_select",
    )(keys.reshape(b_, past), key_new.reshape(b_, LANES))

    attend_in = [q.reshape(b_, ATT_HEADS, ATT_HEAD_DIM), k_new.astype(MXU_DTYPE).reshape(b_, 1, ATT_KV_DIM),
                 v_new.astype(MXU_DTYPE).reshape(b_, 1, ATT_KV_DIM)]
    o = pl.pallas_call(
        functools.partial(_decode_attend_kernel, n_steps=n_steps, pages=pages),
        grid_spec=pltpu.PrefetchScalarGridSpec(
            num_scalar_prefetch=1, grid=(b_, n_steps),
            in_specs=[per_seq(a) for a in attend_in] + [key_blk, lane_blk, lane_blk, lane_blk]
            + [paged(ATT_KV_DIM, j) for j in range(pages)] * 2,
            out_specs=pl.BlockSpec((1, ATT_HEADS, ATT_HEAD_DIM), lambda b, s, pt: (b, 0, 0)),
            scratch_shapes=[pltpu.VMEM((ATT_HEADS, 1), jnp.float32), pltpu.VMEM((ATT_HEADS, 1), jnp.float32),
                            pltpu.VMEM((ATT_HEADS, ATT_KV_DIM), jnp.float32)]),
        out_shape=jax.ShapeDtypeStruct((b_, ATT_HEADS, ATT_HEAD_DIM), MXU_DTYPE),
        compiler_params=params,
        name="dsa_decode_attend",
    )(page_table, *attend_in, keys, key_new, thr.reshape(b_, 1, LANES), last_tie.reshape(b_, 1, LANES),
      *([ck] * pages), *([cv] * pages))
    return o.reshape(b_, ATT_Q_DIM)


def _dsa_sample_pallas(x3d, cache_k, cache_v, cache_idx_k, page_table, w_in, kn_g, kn_b):
    b_, t_, _ = x3d.shape
    past = page_table.shape[1] * cache_k.shape[1]
    pos = jnp.full((b_,), past, jnp.int32)
    q, iq, v, _, iw, kT, _, ikT, _ = _dsa_project(x3d.reshape(1, b_, D_MODEL), pos, w_in, kn_g, kn_b)
    k = _untranspose_groups(kT)[0]
    ik = _untranspose_groups(ikT)[0]
    o = _dsa_decode_split(q, iq, iw, ik, k, v, cache_k, cache_v, cache_idx_k, page_table)
    kv4 = lambda u: u.reshape(b_, t_, ATT_KV_HEADS, ATT_HEAD_DIM)
    return o, kv4(k), kv4(v), ik.reshape(b_, t_, IDX_DIM)


def kernel(x_prompt, x_sample, state_ssm_conv, state_ssm, cache_k, cache_v, cache_idx_k, state_rwkv_shift, state_rwkv_wkv, page_table, p_prompt, p_sample, ln_g, ln_b, ffn_w_up, ffn_w_down, ple_w_p, ple_w_g, ple_b_g, gm_w_in, gm_ln_g, gm_ln_b, gm_ws, gm_bs, gm_w_out, ssm_w_in, ssm_conv_w, ssm_conv_b, ssm_dt_bias, ssm_a_log, ssm_d, ssm_norm_g, ssm_w_out, att_w_in, att_kn_g, att_kn_b, att_w_out, rw_mu, rw_w_r, rw_w_k, rw_w_v, rw_w_o, rw_w0, rw_w1, rw_w2, rw_a0, rw_a1, rw_a2, rw_g1, rw_g2, rw_k_k, rw_k_a, rw_r_k, rw_gn_g, rw_gn_b):
    bp, tp, _ = x_prompt.shape
    bs_, ts, _ = x_sample.shape
    bf = lambda w: w.astype(jnp.bfloat16)
    w_up_bf, w_down_bf = bf(ffn_w_up), bf(ffn_w_down)
    ple_wp_bf, ple_wg_bf = bf(ple_w_p), bf(ple_w_g)
    pp3 = p_prompt.reshape(DEPTH, bp * tp, PLE_DIM)
    ps3 = p_sample.reshape(DEPTH, bs_ * ts, PLE_DIM)

    yp = x_prompt.reshape(bp * tp, D_MODEL)
    ys = x_sample.reshape(bs_ * ts, D_MODEL)
    r3p = lambda t: t.reshape(bp, tp, -1)
    r3s = lambda t: t.reshape(bs_, ts, -1)
    f2 = lambda t: t.reshape(-1, t.shape[-1])

    for i in range(DEPTH):
        yp = _ffn_sub(yp, w_up_bf, w_down_bf, i, 0, ln_g[i, 0], ln_b[i, 0])
        ys = _ffn_sub(ys, w_up_bf, w_down_bf, i, 0, ln_g[i, 0], ln_b[i, 0])
        m = i % N_MIXERS
        if m == 0:
            gm_args = (gm_w_in, gm_ln_g, gm_ln_b, gm_ws, gm_bs, gm_w_out, ln_g[i, 1], ln_b[i, 1])
            yp, = _gmlp_block(yp, tp, *gm_args, False)
            ys, gm_v_s = _gmlp_block(ys, ts, *gm_args, True)
            gm_v_s = r3s(gm_v_s)
        elif m == 1:
            ssm_args = (ssm_w_in, ssm_conv_w, ssm_conv_b, ssm_dt_bias, ssm_a_log, ssm_d, ssm_norm_g)
            hp, conv_p, ssm_p = _mamba2_mixer_pallas(
                r3p(yp), jnp.zeros((bp, SSM_CONV - 1, SSM_CONV_DIM), yp.dtype),
                jnp.zeros((bp, SSM_HEADS, SSM_HEAD_DIM, SSM_STATE), yp.dtype), *ssm_args)
            hs, conv_s, ssm_s = _mamba2_mixer_pallas(r3s(ys), state_ssm_conv, state_ssm, *ssm_args)
            w_out = bf(ssm_w_out)
        elif m == 2:
            hp, k_p, v_p, ik_p = _dsa_prompt_pallas(r3p(yp), att_w_in, att_kn_g, att_kn_b)
            hs, k_s, v_s, ik_s = _dsa_sample_pallas(r3s(ys), cache_k, cache_v, cache_idx_k, page_table,
                                                    att_w_in, att_kn_g, att_kn_b)
            w_out = bf(att_w_out)
        else:
            rw_args = (rw_mu, rw_w_r, rw_w_k, rw_w_v, rw_w0, rw_w1, rw_w2, rw_a0, rw_a1, rw_a2,
                       rw_g1, rw_g2, rw_k_k, rw_k_a, rw_r_k, rw_gn_g, rw_gn_b)
            hp, gate_p, sh_p, wkv_p = _rwkv7_mixer_pallas(
                r3p(yp), jnp.zeros((bp, D_MODEL), yp.dtype),
                jnp.zeros((bp, RW_HEADS, RW_HEAD, RW_HEAD), yp.dtype), *rw_args)
            hs, gate_s, sh_s, wkv_s = _rwkv7_mixer_pallas(r3s(ys), state_rwkv_shift, state_rwkv_wkv, *rw_args)
            w_out = bf(rw_w_o)
        if m == 3:
            yp = _proj_gate_post_norm(yp, hp, gate_p, w_out, ln_g[i, 1], ln_b[i, 1])
            ys = _proj_gate_post_norm(ys, hs, gate_s, w_out, ln_g[i, 1], ln_b[i, 1])
        elif m != 0:
            yp = _proj_post_norm(yp, f2(hp), w_out, ln_g[i, 1], ln_b[i, 1])
            ys = _proj_post_norm(ys, f2(hs), w_out, ln_g[i, 1], ln_b[i, 1])
        yp = _ffn_sub(yp, w_up_bf, w_down_bf, i, 1, ln_g[i, 2], ln_b[i, 2], (pp3, ple_wp_bf, ple_wg_bf, ple_b_g))
        ys = _ffn_sub(ys, w_up_bf, w_down_bf, i, 1, ln_g[i, 2], ln_b[i, 2], (ps3, ple_wp_bf, ple_wg_bf, ple_b_g))

    return (r3p(yp), r3s(ys), gm_v_s, conv_p, ssm_p, conv_s, ssm_s, k_p, v_p, ik_p, k_s, v_s, ik_s,
            sh_p, wkv_p, sh_s, wkv_s)
```

```python
import functools

import jax
import jax.numpy as jnp
from jax import lax
from jax.experimental import pallas as pl
from jax.experimental.pallas import tpu as pltpu

D_MODEL = 1024
DEPTH = 4
N_MIXERS = 4
PLE_DIM = 256
D_FF = 2816
ALPHA = (2 * DEPTH) ** 0.25
LN_EPS = 1e-5

CHUNK = 128
GM_WIDTH = 2 * D_MODEL
GM_GROUPS = 8
GM_GROUP_DIM = GM_WIDTH // GM_GROUPS

SSM_D_INNER = 2 * D_MODEL
SSM_HEAD_DIM = 64
SSM_HEADS = SSM_D_INNER // SSM_HEAD_DIM
SSM_GROUPS = 4
SSM_HPG = SSM_HEADS // SSM_GROUPS
SSM_STATE = 128
SSM_CONV = 4
SSM_CONV_DIM = SSM_D_INNER + 2 * SSM_GROUPS * SSM_STATE
SSM_CHUNK = 128

ATT_HEADS = 16
ATT_KV_HEADS = 4
ATT_HEAD_DIM = D_MODEL // ATT_HEADS
ROPE_DIM = ATT_HEAD_DIM // 4
ROPE_THETA = 500000.0
IDX_HEADS = 8
IDX_DIM = 64
IDX_ROPE_DIM = IDX_DIM // 4
TOPK_MAX = 256
Q_BLOCK = 128
ATT_Q_DIM = ATT_HEADS * ATT_HEAD_DIM
ATT_KV_DIM = ATT_KV_HEADS * ATT_HEAD_DIM
ATT_IN_SPLITS = (ATT_Q_DIM, ATT_Q_DIM + ATT_KV_DIM, ATT_Q_DIM + 2 * ATT_KV_DIM,
                 ATT_Q_DIM + 2 * ATT_KV_DIM + IDX_HEADS * IDX_DIM,
                 ATT_Q_DIM + 2 * ATT_KV_DIM + IDX_HEADS * IDX_DIM + IDX_DIM)

RW_HEAD = 64
RW_HEADS = D_MODEL // RW_HEAD
RW_GN_EPS = 64e-5

V7X_VMEM_LIMIT_BYTES = 52 * 1024 * 1024
LANES = 128
SUBLANES = 8
FF_TILE = D_FF // 2
ROW_TILE = 512


def _row_tile(m):
    return ROW_TILE if m % ROW_TILE == 0 else m


def _ln_rows(y, g, b):
    mu = jnp.mean(y, axis=-1, keepdims=True)
    yc = y - mu
    var = jnp.mean(yc * yc, axis=-1, keepdims=True)
    return yc * lax.rsqrt(var + LN_EPS) * g + b


FFN_ROW_TILE = 1024
FFN_ROW_PARTS = 4
FFN_PLE_ROW_PARTS = 2


def _ffn_kernel(x_ref, wu_ref, wd_ref, g_ref, b_ref, *rest, parts, with_ple):
    o_ref = rest[-1]
    rows = x_ref.shape[0] // parts
    for p in range(parts):
        x = x_ref[p * rows:(p + 1) * rows, :]
        xb = x.astype(MXU_DTYPE)
        acc = None
        for f in range(D_FF // FF_TILE):
            cols = slice(f * FF_TILE, (f + 1) * FF_TILE)
            gate = jnp.dot(xb, wu_ref[:, cols], preferred_element_type=jnp.float32)
            lin = jnp.dot(xb, wu_ref[:, D_FF + f * FF_TILE:D_FF + (f + 1) * FF_TILE],
                          preferred_element_type=jnp.float32)
            h = (gate * jax.nn.sigmoid(gate) * lin).astype(MXU_DTYPE)
            part = jnp.dot(h, wd_ref[cols, :], preferred_element_type=jnp.float32)
            acc = part if acc is None else acc + part
        y = _ln_rows(ALPHA * x + 0.5 * acc, g_ref[...], b_ref[...])
        if with_ple:
            p_ref, wp_ref, wg_ref, bg_ref = rest[:4]
            gate = jax.nn.sigmoid(
                jnp.dot(y.astype(MXU_DTYPE), wg_ref[...], preferred_element_type=jnp.float32) + bg_ref[...])
            emb = jnp.dot(p_ref[p * rows:(p + 1) * rows, :].astype(MXU_DTYPE), wp_ref[...],
                          preferred_element_type=jnp.float32)
            y = y + gate * emb
        o_ref[p * rows:(p + 1) * rows, :] = y


def _ffn_sub(x2d, w_up, w_down, layer, half, g, b, ple=None):
    m = x2d.shape[0]
    tm = FFN_ROW_TILE if m % FFN_ROW_TILE == 0 else m
    parts = 1 if tm != FFN_ROW_TILE else (FFN_ROW_PARTS if ple is None else FFN_PLE_ROW_PARTS)
    resident = dict(pipeline_mode=pl.Buffered(1))
    vec = pl.BlockSpec((1, D_MODEL), lambda i: (0, 0))
    in_specs = [
        pl.BlockSpec((tm, D_MODEL), lambda i: (i, 0)),
        pl.BlockSpec((None, None, D_MODEL, 2 * D_FF), lambda i: (layer, half, 0, 0), **resident),
        pl.BlockSpec((None, None, D_FF, D_MODEL), lambda i: (layer, half, 0, 0), **resident),
        vec, vec]
    args = [x2d, w_up, w_down, g.reshape(1, D_MODEL), b.reshape(1, D_MODEL)]
    if ple is not None:
        p3d, w_p, w_g, b_g = ple
        in_specs += [pl.BlockSpec((None, tm, PLE_DIM), lambda i: (layer, i, 0)),
                     pl.BlockSpec((None, PLE_DIM, D_MODEL), lambda i: (layer, 0, 0), **resident),
                     pl.BlockSpec((None, D_MODEL, D_MODEL), lambda i: (layer, 0, 0), **resident),
                     pl.BlockSpec((None, 1, D_MODEL), lambda i: (layer, 0, 0))]
        args += [p3d, w_p, w_g, b_g.reshape(DEPTH, 1, D_MODEL)]
    return pl.pallas_call(
        functools.partial(_ffn_kernel, parts=parts, with_ple=ple is not None),
        grid=(m // tm,),
        in_specs=in_specs,
        out_specs=pl.BlockSpec((tm, D_MODEL), lambda i: (i, 0)),
        out_shape=jax.ShapeDtypeStruct((m, D_MODEL), jnp.float32),
        compiler_params=pltpu.CompilerParams(
            dimension_semantics=("parallel",),
            vmem_limit_bytes=V7X_VMEM_LIMIT_BYTES),
        name="ffn_ple" if ple is not None else "ffn_sub",
    )(*args)


def _proj_ln_kernel(x_ref, h_ref, w_ref, g_ref, b_ref, o_ref):
    y = ALPHA * x_ref[...] + jnp.dot(h_ref[...].astype(jnp.bfloat16), w_ref[...],
                                     preferred_element_type=jnp.float32)
    o_ref[...] = _ln_rows(y, g_ref[...], b_ref[...])


def _proj_post_norm(x2d, h2d, w_out, g, b):
    m = x2d.shape[0]
    k = h2d.shape[1]
    tm = _row_tile(m)
    return pl.pallas_call(
        _proj_ln_kernel,
        grid=(m // tm,),
        in_specs=[
            pl.BlockSpec((tm, D_MODEL), lambda i: (i, 0)),
            pl.BlockSpec((tm, k), lambda i: (i, 0)),
            pl.BlockSpec((k, D_MODEL), lambda i: (0, 0)),
            pl.BlockSpec((1, D_MODEL), lambda i: (0, 0)),
            pl.BlockSpec((1, D_MODEL), lambda i: (0, 0)),
        ],
        out_specs=pl.BlockSpec((tm, D_MODEL), lambda i: (i, 0)),
        out_shape=jax.ShapeDtypeStruct((m, D_MODEL), jnp.float32),
        compiler_params=pltpu.CompilerParams(
            dimension_semantics=("parallel",),
            vmem_limit_bytes=V7X_VMEM_LIMIT_BYTES),
        name="proj_post_norm",
    )(x2d, h2d, w_out, g.reshape(1, D_MODEL), b.reshape(1, D_MODEL))


MXU_DTYPE = jnp.bfloat16
KEY_GROUP = 512
INT32_MIN = -2 ** 31
MASK_NEG = -1e30


def _rope_lane_tables(pos, rot_dim, head_dim):
    half = rot_dim // 2
    inv = ROPE_THETA ** (-jnp.arange(half, dtype=jnp.float32) / half)
    ang = pos.astype(jnp.float32)[:, None] * inv[None, :]
    cos, sin = jnp.cos(ang), jnp.sin(ang)
    n = pos.shape[0]
    rest = head_dim - rot_dim
    c = jnp.concatenate([cos, cos, jnp.ones((n, rest), jnp.float32)], axis=1)
    s1 = jnp.concatenate([-sin, jnp.zeros((n, half + rest), jnp.float32)], axis=1)
    s2 = jnp.concatenate([jnp.zeros((n, half), jnp.float32), sin, jnp.zeros((n, rest), jnp.float32)], axis=1)
    reps = LANES // head_dim
    tile = lambda t: jnp.tile(t, (1, reps))
    return tile(c), tile(s1), tile(s2), cos.T, sin.T


def _rope_lanes(t, c, s1, s2, half):
    n = t.shape[1]
    reps = n // LANES
    tl = lambda a: jnp.concatenate([a] * reps, axis=1)
    return t * tl(c) + pltpu.roll(t, n - half, 1) * tl(s1) + pltpu.roll(t, half, 1) * tl(s2)


def _rope_rows(t, cT, sT, head_dim, half):
    pieces = []
    for h in range(t.shape[0] // head_dim):
        x1 = t[h * head_dim:h * head_dim + half]
        x2 = t[h * head_dim + half:h * head_dim + 2 * half]
        pieces += [x1 * cT - x2 * sT, x2 * cT + x1 * sT, t[h * head_dim + 2 * half:(h + 1) * head_dim]]
    return jnp.concatenate(pieces, axis=0)


def _dsa_proj_kernel(x_ref, wq_ref, wiq_ref, wv_ref, wvx_ref, wiw_ref, wkT_ref, wikT_ref,
                     c_ref, s1_ref, s2_ref, cT_ref, sT_ref, kng_ref, knb_ref, one_ref,
                     q_ref, iq_ref, v_ref, vx_ref, iw_ref, kT_ref, kTb_ref, ikT_ref, ikTb_ref):
    xb = x_ref[...].astype(MXU_DTYPE)
    c, s1, s2 = c_ref[...], s1_ref[...], s2_ref[...]
    cT, sT = cT_ref[...], sT_ref[...]
    dot = lambda a, b: jnp.dot(a, b, preferred_element_type=jnp.float32)
    dot_t = lambda w, a: lax.dot_general(w, a, (((1,), (1,)), ((), ())), preferred_element_type=jnp.float32)

    q = _rope_lanes(dot(xb, wq_ref[...]), c, s1, s2, ROPE_DIM // 2)
    q_ref[...] = (q * (ATT_HEAD_DIM ** -0.5)).astype(q_ref.dtype)
    iq = _rope_lanes(dot(xb, wiq_ref[...]), c, s1, s2, IDX_ROPE_DIM // 2)
    iq_ref[...] = iq.astype(iq_ref.dtype)
    v_ref[...] = dot(xb, wv_ref[...])
    vx_ref[...] = (dot(xb, wvx_ref[...]) + one_ref[...]).astype(vx_ref.dtype)
    iw_ref[...] = dot(xb, wiw_ref[...]) * (IDX_HEADS ** -0.5 * IDX_DIM ** -0.5)

    kT = _rope_rows(dot_t(wkT_ref[...], xb), cT, sT, ATT_HEAD_DIM, ROPE_DIM // 2)
    kT_ref[0, 0] = kT
    kTb_ref[0, 0] = kT.astype(kTb_ref.dtype)
    ikT = dot_t(wikT_ref[...], xb)
    mu = jnp.mean(ikT, axis=0, keepdims=True)
    ikc = ikT - mu
    var = jnp.mean(ikc * ikc, axis=0, keepdims=True)
    ikT = ikc * lax.rsqrt(var + LN_EPS) * kng_ref[...] + knb_ref[...]
    ikT = _rope_rows(ikT, cT, sT, IDX_DIM, IDX_ROPE_DIM // 2)
    ikT_ref[0, 0] = ikT
    ikTb_ref[0, 0] = ikT.astype(ikTb_ref.dtype)


def _dsa_project(x3d, pos, w_in, kn_g, kn_b):
    b_, t_, _ = x3d.shape
    tk = KEY_GROUP if t_ % KEY_GROUP == 0 else t_
    ng = t_ // tk
    m = b_ * t_
    w_q, w_k, w_v, w_iq, w_ik, w_iw = jnp.split(w_in, list(ATT_IN_SPLITS), axis=1)
    cast = lambda w: w.astype(MXU_DTYPE)
    w_vx = jnp.pad(w_v.reshape(D_MODEL, ATT_KV_HEADS, ATT_HEAD_DIM),
                   ((0, 0), (0, 0), (0, LANES - ATT_HEAD_DIM))).reshape(D_MODEL, ATT_KV_HEADS * LANES)
    one_col = jnp.tile((jnp.arange(LANES) == ATT_HEAD_DIM).astype(jnp.float32), ATT_KV_HEADS)[None, :]
    w_iw_pad = jnp.pad(w_iw, ((0, 0), (0, LANES - IDX_HEADS)))
    c, s1, s2, cT, sT = _rope_lane_tables(pos, ROPE_DIM, ATT_HEAD_DIM)
    full = lambda shape: pl.BlockSpec(shape, lambda b, i: (0,) * len(shape))
    rows = lambda n: pl.BlockSpec((tk, n), lambda b, i: (b * ng + i, 0))
    ptab = lambda n: pl.BlockSpec((tk, n), lambda b, i: (i, 0))
    grp = lambda n: pl.BlockSpec((1, 1, n, tk), lambda b, i: (b, i, 0, 0))
    sds = jax.ShapeDtypeStruct
    return pl.pallas_call(
        _dsa_proj_kernel,
        grid=(b_, ng),
        in_specs=[rows(D_MODEL), full((D_MODEL, ATT_Q_DIM)), full((D_MODEL, IDX_HEADS * IDX_DIM)),
                  full((D_MODEL, ATT_KV_DIM)), full((D_MODEL, ATT_KV_HEADS * LANES)), full((D_MODEL, LANES)),
                  full((ATT_KV_DIM, D_MODEL)), full((IDX_DIM, D_MODEL)),
                  ptab(LANES), ptab(LANES), ptab(LANES),
                  pl.BlockSpec((ROPE_DIM // 2, tk), lambda b, i: (0, i)),
                  pl.BlockSpec((ROPE_DIM // 2, tk), lambda b, i: (0, i)),
                  full((IDX_DIM, 1)), full((IDX_DIM, 1)), full((1, ATT_KV_HEADS * LANES))],
        out_specs=[rows(ATT_Q_DIM), rows(IDX_HEADS * IDX_DIM), rows(ATT_KV_DIM), rows(ATT_KV_HEADS * LANES),
                   rows(LANES), grp(ATT_KV_DIM), grp(ATT_KV_DIM), grp(IDX_DIM), grp(IDX_DIM)],
        out_shape=[sds((m, ATT_Q_DIM), MXU_DTYPE), sds((m, IDX_HEADS * IDX_DIM), MXU_DTYPE),
                   sds((m, ATT_KV_DIM), jnp.float32), sds((m, ATT_KV_HEADS * LANES), MXU_DTYPE),
                   sds((m, LANES), jnp.float32),
                   sds((b_, ng, ATT_KV_DIM, tk), jnp.float32), sds((b_, ng, ATT_KV_DIM, tk), MXU_DTYPE),
                   sds((b_, ng, IDX_DIM, tk), jnp.float32), sds((b_, ng, IDX_DIM, tk), MXU_DTYPE)],
        compiler_params=pltpu.CompilerParams(
            dimension_semantics=("parallel", "parallel"),
            vmem_limit_bytes=V7X_VMEM_LIMIT_BYTES),
        name="dsa_project",
    )(x3d.reshape(m, D_MODEL), cast(w_q), cast(w_iq), cast(w_v), cast(w_vx), cast(w_iw_pad),
      cast(w_k.T), cast(w_ik.T), c, s1, s2, cT, sT, kn_g.reshape(IDX_DIM, 1), kn_b.reshape(IDX_DIM, 1), one_col)


def _untranspose_groups(tg):
    b_, g_, r_, tk = tg.shape
    return jnp.transpose(tg, (0, 1, 3, 2)).reshape(b_, g_ * tk, r_)


def _dsa_proj_q_lanes_kernel(x_ref, wqT_ref, wiqT_ref, wiwT_ref, wk_ref, wv_ref, wvxT_ref, wik_ref,
                             c_ref, s1_ref, s2_ref, cT_ref, sT_ref, kng_ref, knb_ref, onerow_ref,
                             qT_ref, iqT_ref, iwT_ref, k_ref, khd_ref, v_ref, vxT_ref, ik_ref, ikb_ref):
    xb = x_ref[...].astype(MXU_DTYPE)
    tm = xb.shape[0]
    c, s1, s2 = c_ref[...], s1_ref[...], s2_ref[...]
    cT, sT = cT_ref[...], sT_ref[...]
    dot = lambda a, b: jnp.dot(a, b, preferred_element_type=jnp.float32)
    dot_t = lambda w, a: lax.dot_general(w, a, (((1,), (1,)), ((), ())), preferred_element_type=jnp.float32)

    qT = _rope_rows(dot_t(wqT_ref[...], xb), cT, sT, ATT_HEAD_DIM, ROPE_DIM // 2) * (ATT_HEAD_DIM ** -0.5)
    iqT = _rope_rows(dot_t(wiqT_ref[...], xb), cT, sT, IDX_DIM, IDX_ROPE_DIM // 2)
    iwT = dot_t(wiwT_ref[...], xb) * (IDX_HEADS ** -0.5 * IDX_DIM ** -0.5)
    for t in range(tm // Q_BLOCK):
        lanes = slice(t * Q_BLOCK, (t + 1) * Q_BLOCK)
        qT_ref[0, t] = qT[:, lanes].astype(qT_ref.dtype)
        iqT_ref[0, t] = iqT[:, lanes].astype(iqT_ref.dtype)
        iwT_ref[0, t] = iwT[:, lanes]

    k = _rope_lanes(dot(xb, wk_ref[...]), c, s1, s2, ROPE_DIM // 2)
    k_ref[...] = k
    for g in range(ATT_KV_HEADS):
        khd_ref[g] = k[:, g * ATT_HEAD_DIM:(g + 1) * ATT_HEAD_DIM].astype(khd_ref.dtype)
    v_ref[...] = dot(xb, wv_ref[...])
    vxT_ref[0, 0] = (dot_t(wvxT_ref[...], xb) + onerow_ref[...]).astype(vxT_ref.dtype)

    ik = dot(xb, wik_ref[...])
    real = lax.broadcasted_iota(jnp.int32, ik.shape, 1) < IDX_DIM
    mu = jnp.sum(ik, axis=-1, keepdims=True) * (1.0 / IDX_DIM)
    ikc = jnp.where(real, ik - mu, 0.0)
    var = jnp.sum(ikc * ikc, axis=-1, keepdims=True) * (1.0 / IDX_DIM)
    ikn = _rope_lanes(ikc * lax.rsqrt(var + LN_EPS) * kng_ref[...] + knb_ref[...], c, s1, s2, IDX_ROPE_DIM // 2)
    ik_ref[...] = ikn[:, :IDX_DIM]
    ikb_ref[...] = ikn[:, :IDX_DIM].astype(ikb_ref.dtype)


def _dsa_project_q_lanes(x3d, pos, w_in, kn_g, kn_b):
    b_, t_, _ = x3d.shape
    tk = KEY_GROUP
    ng = t_ // tk
    nq = tk // Q_BLOCK
    m = b_ * t_
    w_q, w_k, w_v, w_iq, w_ik, w_iw = jnp.split(w_in, list(ATT_IN_SPLITS), axis=1)
    cast = lambda w: w.astype(MXU_DTYPE)
    w_vxT = jnp.pad(w_v.T.reshape(ATT_KV_HEADS, ATT_HEAD_DIM, D_MODEL),
                    ((0, 0), (0, LANES - ATT_HEAD_DIM), (0, 0))).reshape(ATT_KV_HEADS * LANES, D_MODEL)
    one_row = jnp.tile((jnp.arange(LANES) == ATT_HEAD_DIM).astype(jnp.float32), ATT_KV_HEADS)[:, None]
    pad_lanes = lambda a: jnp.pad(a, ((0, 0), (0, LANES - a.shape[1])))
    c, s1, s2, cT, sT = _rope_lane_tables(pos, ROPE_DIM, ATT_HEAD_DIM)
    full = lambda shape: pl.BlockSpec(shape, lambda b, i: (0,) * len(shape))
    rows = lambda n: pl.BlockSpec((tk, n), lambda b, i: (b * ng + i, 0))
    ptab = lambda n: pl.BlockSpec((tk, n), lambda b, i: (i, 0))
    qtile = lambda n: pl.BlockSpec((1, nq, n, Q_BLOCK), lambda b, i: (b, i, 0, 0))
    sds = jax.ShapeDtypeStruct
    return pl.pallas_call(
        _dsa_proj_q_lanes_kernel,
        grid=(b_, ng),
        in_specs=[rows(D_MODEL), full((ATT_Q_DIM, D_MODEL)), full((IDX_HEADS * IDX_DIM, D_MODEL)),
                  full((IDX_HEADS, D_MODEL)), full((D_MODEL, ATT_KV_DIM)), full((D_MODEL, ATT_KV_DIM)),
                  full((ATT_KV_HEADS * LANES, D_MODEL)), full((D_MODEL, LANES)),
                  ptab(LANES), ptab(LANES), ptab(LANES),
                  pl.BlockSpec((ROPE_DIM // 2, tk), lambda b, i: (0, i)),
                  pl.BlockSpec((ROPE_DIM // 2, tk), lambda b, i: (0, i)),
                  full((1, LANES)), full((1, LANES)), full((ATT_KV_HEADS * LANES, 1))],
        out_specs=[qtile(ATT_Q_DIM), qtile(IDX_HEADS * IDX_DIM), qtile(IDX_HEADS),
                   rows(ATT_KV_DIM), pl.BlockSpec((ATT_KV_HEADS, tk, ATT_HEAD_DIM), lambda b, i: (0, b * ng + i, 0)),
                   rows(ATT_KV_DIM), pl.BlockSpec((1, 1, ATT_KV_HEADS * LANES, tk), lambda b, i: (b, i, 0, 0)),
                   rows(IDX_DIM), rows(IDX_DIM)],
        out_shape=[sds((b_, t_ // Q_BLOCK, ATT_Q_DIM, Q_BLOCK), MXU_DTYPE),
                   sds((b_, t_ // Q_BLOCK, IDX_HEADS * IDX_DIM, Q_BLOCK), MXU_DTYPE),
                   sds((b_, t_ // Q_BLOCK, IDX_HEADS, Q_BLOCK), jnp.float32),
                   sds((m, ATT_KV_DIM), jnp.float32), sds((ATT_KV_HEADS, m, ATT_HEAD_DIM), MXU_DTYPE),
                   sds((m, ATT_KV_DIM), jnp.float32), sds((b_, ng, ATT_KV_HEADS * LANES, tk), MXU_DTYPE),
                   sds((m, IDX_DIM), jnp.float32), sds((m, IDX_DIM), MXU_DTYPE)],
        compiler_params=pltpu.CompilerParams(
            dimension_semantics=("parallel", "parallel"),
            vmem_limit_bytes=V7X_VMEM_LIMIT_BYTES),
        name="dsa_project_q_lanes",
    )(x3d.reshape(m, D_MODEL), cast(w_q.T), cast(w_iq.T), cast(w_iw.T), cast(w_k), cast(w_v), cast(w_vxT),
      cast(pad_lanes(w_ik)), c, s1, s2, cT, sT, pad_lanes(kn_g.reshape(1, IDX_DIM)),
      pad_lanes(kn_b.reshape(1, IDX_DIM)), one_row)


def _tree_sum(parts):
    while len(parts) > 1:
        parts = [parts[i] + parts[i + 1] for i in range(0, len(parts) - 1, 2)] + (
            [parts[-1]] if len(parts) % 2 else [])
    return parts[0]


def _dsa_attend_q_lanes_kernel(iqT_ref, iwT_ref, ik_ref, qT_ref, k_ref, vxT_ref, o_ref,
                               key_ref, bias_ref, m_ref, acc_ref, s_ref, *, topk, col_bits):
    j = pl.program_id(1)
    tk, tq = key_ref.shape[1], key_ref.shape[2]
    n_groups = (j * tq + tq + tk - 1) // tk
    qpos = j * tq + lax.broadcasted_iota(jnp.int32, (tk, tq), 1)
    kpos0 = lax.broadcasted_iota(jnp.int32, (tk, tq), 0)
    dot = lambda a, b: jnp.dot(a, b, preferred_element_type=jnp.float32)

    def score_body(g, carry):
        start = pl.multiple_of(g * tk, tk)
        w_iq = jnp.concatenate([iqT_ref[0, 0, h * IDX_DIM:(h + 1) * IDX_DIM, :] for h in range(IDX_HEADS)], axis=1)
        s_all = dot(ik_ref[0, pl.ds(start, tk), :], w_iq)
        sc = _tree_sum([iwT_ref[0, 0, h:h + 1, :] * jnp.maximum(s_all[:, h * tq:(h + 1) * tq], 0.0)
                        for h in range(IDX_HEADS)])
        key_ref[g] = jnp.where(kpos0 + g * tk <= qpos, _sortable_key(sc), jnp.int32(INT32_MIN))
        return carry

    lax.fori_loop(0, n_groups, score_body, 0)

    def count_keys(pred):
        def body(g, part):
            hit = jnp.where(pred(key_ref[g], kpos0 + g * tk), 1.0, 0.0)
            return part + _tree_sum([hit[r * SUBLANES:(r + 1) * SUBLANES] for r in range(tk // SUBLANES)])
        part = lax.fori_loop(0, n_groups, body, jnp.zeros((SUBLANES, tq), jnp.float32))
        return jnp.sum(part, axis=0, keepdims=True)

    def bit_body(i, thr):
        cand = thr ^ lax.shift_left(jnp.int32(1), jnp.int32(31) - i)
        return jnp.where(count_keys(lambda k, kp: k >= cand) >= float(topk), cand, thr)

    thr = lax.fori_loop(0, 32, bit_body, jnp.full((1, tq), INT32_MIN, jnp.int32))

    need = float(topk) - count_keys(lambda k, kp: k > thr)

    def pos_body(i, last):
        cand = last | lax.shift_left(jnp.int32(1), jnp.int32(col_bits - 1) - i)
        return jnp.where(count_keys(lambda k, kp: (k == thr) & (kp < cand)) < need, cand, last)

    n_tied = count_keys(lambda k, kp: k == thr)
    excess = jnp.max(jnp.where(n_tied > need, 1.0, 0.0), axis=1, keepdims=True)
    last_tie = lax.cond(excess[0, 0] > 0.0,
                        lambda: lax.fori_loop(0, col_bits, pos_body, jnp.zeros((1, tq), jnp.int32)),
                        lambda: jnp.full((1, tq), 2 ** col_bits - 1, jnp.int32))

    m_ref[...] = jnp.full(m_ref.shape, MASK_NEG, jnp.float32)
    acc_ref[...] = jnp.zeros(acc_ref.shape, jnp.float32)
    gsz = ATT_HEADS // ATT_KV_HEADS

    def attend_body(g, carry):
        start = pl.multiple_of(g * tk, tk)
        key = key_ref[g]
        kpos = kpos0 + g * tk
        keep = (key > thr) | ((key == thr) & (kpos <= last_tie))
        bias_ref[...] = jnp.where(keep & (kpos <= qpos), 0.0, MASK_NEG)
        for kv in range(ATT_KV_HEADS):
            w_q = jnp.concatenate([qT_ref[0, 0, (kv * gsz + i) * ATT_HEAD_DIM:(kv * gsz + i + 1) * ATT_HEAD_DIM, :]
                                   for i in range(gsz)], axis=1)
            s_ref[kv] = dot(k_ref[kv, pl.ds(start, tk), :], w_q)
        for kv in range(ATT_KV_HEADS):
            s = s_ref[kv] + jnp.concatenate([bias_ref[...]] * gsz, axis=1)
            m_old = m_ref[kv]
            m_new = jnp.maximum(m_old, jnp.max(s, axis=0, keepdims=True))
            p = jnp.exp(s - m_new).astype(vxT_ref.dtype)
            pv = dot(vxT_ref[0, g, kv * LANES:(kv + 1) * LANES, :], p)
            acc_ref[kv] = jnp.exp(m_old - m_new) * acc_ref[kv] + pv
            m_ref[kv] = m_new
        return carry

    lax.fori_loop(0, n_groups, attend_body, 0)

    for h in range(ATT_HEADS):
        a = acc_ref[h // gsz, :, (h % gsz) * tq:(h % gsz + 1) * tq]
        o = (a / a[ATT_HEAD_DIM:ATT_HEAD_DIM + 1, :]).T
        o_ref[:, h * ATT_HEAD_DIM:(h + 1) * ATT_HEAD_DIM] = o[:, :ATT_HEAD_DIM].astype(o_ref.dtype)


def _dsa_attend_q_lanes(b_, t_, qT, iqT, iwT, ikb, khd, vxT):
    ng, tk = vxT.shape[1], vxT.shape[3]
    tq = Q_BLOCK
    nq = t_ // tq
    qtile = lambda n: pl.BlockSpec((1, 1, n, tq), lambda b, j: (b, j, 0, 0))
    return pl.pallas_call(
        functools.partial(_dsa_attend_q_lanes_kernel, topk=min(TOPK_MAX, t_ // 4),
                          col_bits=max(1, (t_ - 1).bit_length())),
        grid=(b_, nq),
        in_specs=[qtile(IDX_HEADS * IDX_DIM), qtile(IDX_HEADS),
                  pl.BlockSpec((1, t_, IDX_DIM), lambda b, j: (b, 0, 0)),
                  qtile(ATT_Q_DIM),
                  pl.BlockSpec((ATT_KV_HEADS, t_, ATT_HEAD_DIM), lambda b, j: (0, b, 0)),
                  pl.BlockSpec((1, ng, ATT_KV_HEADS * LANES, tk), lambda b, j: (b, 0, 0, 0))],
        out_specs=pl.BlockSpec((tq, ATT_Q_DIM), lambda b, j: (b * nq + j, 0)),
        out_shape=jax.ShapeDtypeStruct((b_ * t_, ATT_Q_DIM), MXU_DTYPE),
        scratch_shapes=[pltpu.VMEM((ng, tk, tq), jnp.int32),
                        pltpu.VMEM((tk, tq), jnp.float32),
                        pltpu.VMEM((ATT_KV_HEADS, 1, tq * (ATT_HEADS // ATT_KV_HEADS)), jnp.float32),
                        pltpu.VMEM((ATT_KV_HEADS, LANES, tq * (ATT_HEADS // ATT_KV_HEADS)), jnp.float32),
                        pltpu.VMEM((ATT_KV_HEADS, tk, tq * (ATT_HEADS // ATT_KV_HEADS)), jnp.float32)],
        compiler_params=pltpu.CompilerParams(
            dimension_semantics=("parallel", "arbitrary"),
            vmem_limit_bytes=V7X_VMEM_LIMIT_BYTES),
        name="dsa_attend_q_lanes",
    )(iqT, iwT, ikb.reshape(b_, t_, IDX_DIM), qT, khd, vxT)


def _dsa_prompt_pallas(x3d, w_in, kn_g, kn_b):
    b_, t_, _ = x3d.shape
    qT, iqT, iwT, k, khd, v, vxT, ik, ikb = _dsa_project_q_lanes(x3d, jnp.arange(t_), w_in, kn_g, kn_b)
    o = _dsa_attend_q_lanes(b_, t_, qT, iqT, iwT, ikb, khd, vxT)
    kv4 = lambda u: u.reshape(b_, t_, ATT_KV_HEADS, ATT_HEAD_DIM)
    return o, kv4(k), kv4(v), ik.reshape(b_, t_, IDX_DIM)


RW_ROW_TILE = 256


RW_PAIRS = RW_HEADS // 2
RW_PAIR_LANES = 2 * RW_HEAD


def _rwkv_project_rows(x, xp, mu_ref, wr_ref, wk_ref, wv_ref, w1_ref, w2_ref, a1_ref, a2_ref,
                       g1_ref, g2_ref, w0_ref, a0_ref):
    dx = xp - x
    mix = lambda c: (x + dx * mu_ref[c:c + 1, :]).astype(MXU_DTYPE)
    dot = lambda a, b: jnp.dot(a.astype(MXU_DTYPE), b, preferred_element_type=jnp.float32)
    r = dot(mix(0), wr_ref[...])
    lora_w = dot(jnp.tanh(dot(mix(1), w1_ref[...])), w2_ref[...])
    w_log = -jax.nn.softplus(-(w0_ref[...] + lora_w)) - 0.5
    d = jnp.exp(-jnp.exp(w_log))
    k = dot(mix(2), wk_ref[...])
    v = dot(mix(3), wv_ref[...])
    a = jax.nn.sigmoid(a0_ref[...] + dot(dot(mix(4), a1_ref[...]), a2_ref[...]))
    g = dot(jax.nn.sigmoid(dot(mix(5), g1_ref[...])), g2_ref[...])
    return r, d, k, v, a, g


def _rwkv_proj_step_kernel(x_ref, xp_ref, *refs):
    vals = _rwkv_project_rows(x_ref[...], xp_ref[...], *refs[:12])
    for ref, val in zip(refs[12:], vals):
        ref[...] = val


def _rwkv_proj_seq_kernel(x_ref, halo_ref, shift_ref, *refs):
    i = pl.program_id(1)
    x = x_ref[...]
    prev = jnp.where(i == 0, shift_ref[0], halo_ref[...])[SUBLANES - 1:SUBLANES, :]
    first = lax.broadcasted_iota(jnp.int32, (x.shape[0], 1), 0) == 0
    xp = jnp.where(first, prev, pltpu.roll(x, 1, 0))
    vals = _rwkv_project_rows(x, xp, *refs[:12])
    for ref, val in zip(refs[12:], vals):
        ref[...] = val


def _rwkv_consts(mu, w_r, w_k, w_v, w0, w1, w2, a0, a1, a2, g1, g2):
    cast = lambda w: w.astype(MXU_DTYPE)
    return [mu, cast(w_r), cast(w_k), cast(w_v), cast(w1), cast(w2), cast(a1), cast(a2), cast(g1), cast(g2),
            w0.reshape(1, D_MODEL), a0.reshape(1, D_MODEL)]


def _rwkv_project_step(x2d, xprev2d, *params):
    m = x2d.shape[0]
    consts = _rwkv_consts(*params)
    full = lambda a: pl.BlockSpec(a.shape, lambda i: (0,) * a.ndim)
    rows = pl.BlockSpec((m, D_MODEL), lambda i: (0, 0))
    return pl.pallas_call(
        _rwkv_proj_step_kernel,
        grid=(1,),
        in_specs=[rows, rows] + [full(a) for a in consts],
        out_specs=[rows] * 6,
        out_shape=[jax.ShapeDtypeStruct((m, D_MODEL), jnp.float32)] * 6,
        compiler_params=pltpu.CompilerParams(
            dimension_semantics=("arbitrary",),
            vmem_limit_bytes=V7X_VMEM_LIMIT_BYTES),
        name="rwkv_project_step",
    )(x2d, xprev2d, *consts)


def _rwkv_project_seq(x3d, shift, *params):
    b_, t_, _ = x3d.shape
    m = b_ * t_
    tm = RW_ROW_TILE
    nt = t_ // tm
    consts = _rwkv_consts(*params)
    full = lambda a: pl.BlockSpec(a.shape, lambda b, i: (0,) * a.ndim)
    rows = pl.BlockSpec((tm, D_MODEL), lambda b, i: (b * nt + i, 0))
    halo = pl.BlockSpec((SUBLANES, D_MODEL), lambda b, i: (jnp.maximum((b * nt + i) * (tm // SUBLANES) - 1, 0), 0))
    shift8 = jnp.pad(shift[:, None, :], ((0, 0), (SUBLANES - 1, 0), (0, 0)))
    x2d = x3d.reshape(m, D_MODEL)
    return pl.pallas_call(
        _rwkv_proj_seq_kernel,
        grid=(b_, nt),
        in_specs=[rows, halo, pl.BlockSpec((1, SUBLANES, D_MODEL), lambda b, i: (b, 0, 0))]
        + [full(a) for a in consts],
        out_specs=[rows] * 6,
        out_shape=[jax.ShapeDtypeStruct((m, D_MODEL), jnp.float32)] * 6,
        compiler_params=pltpu.CompilerParams(
            dimension_semantics=("parallel", "parallel"),
            vmem_limit_bytes=V7X_VMEM_LIMIT_BYTES),
        name="rwkv_project_seq",
    )(x2d, x2d, shift8, *consts)


RW_LANES = LANES
RW_TIME_CHUNK = 64


def _rwkv_scan_kernel(r_ref, d_ref, k_ref, v_ref, a_ref, s0_ref, kk_ref, ka_ref, rk_ref, gg_ref, gb_ref,
                      z_ref, s_out_ref, s_ref, vec_ref):
    c = pl.program_id(1)
    n = RW_HEAD
    tc = r_ref.shape[1]
    low_half = lax.broadcasted_iota(jnp.int32, (n, RW_LANES), 1) < n

    @pl.when(c == 0)
    def _():
        s_ref[...] = s0_ref[...]

    def swap_layout(x):
        xt = jnp.concatenate([x, x], axis=0).T
        return jnp.where(low_half, xt[:n], xt[n:])

    def load_step(ref, t):
        rows = ref[:, t, :]
        return swap_layout(jnp.concatenate(
            [rows[:, p * RW_PAIR_LANES:(p + 1) * RW_PAIR_LANES] for p in range(RW_PAIRS)], axis=0))

    def store_step(ref, t, val):
        tile = swap_layout(val)
        ref[:, t, :] = jnp.concatenate(
            [tile[p * RW_SEQ_PER_TILE:(p + 1) * RW_SEQ_PER_TILE] for p in range(RW_PAIRS)], axis=1)

    def prepare(t, slot):
        r, k, a = load_step(r_ref, t), load_step(k_ref, t), load_step(a_ref, t)
        kkr = k * kk_ref[...]
        nrm = jnp.sqrt(jnp.sum(kkr * kkr, axis=0, keepdims=True))
        kk = kkr / jnp.maximum(nrm, 1e-12)
        vec_ref[slot, 0] = kk
        vec_ref[slot, 1] = load_step(d_ref, t)
        vec_ref[slot, 2] = kk * a
        vec_ref[slot, 3] = k * (1.0 + (a - 1.0) * ka_ref[...])
        vec_ref[slot, 4] = r
        vec_ref[slot, 5] = load_step(v_ref, t)

    def step(t, slot):
        row = lambda q, j: vec_ref[slot, q, j:j + 1, :]
        v = vec_ref[slot, 5]
        lanes = 4
        sa_parts = [s_ref[j] * row(0, j) for j in range(lanes)]
        for j in range(lanes, n):
            sa_parts[j % lanes] = sa_parts[j % lanes] + s_ref[j] * row(0, j)
        sa = _tree_sum(sa_parts)
        y_parts = []
        for j in range(n):
            sn = s_ref[j] * row(1, j) - sa * row(2, j) + v * row(3, j)
            s_ref[j] = sn
            if j < lanes:
                y_parts.append(sn * row(4, j))
            else:
                y_parts[j % lanes] = y_parts[j % lanes] + sn * row(4, j)
        y = _tree_sum(y_parts)
        mu = jnp.mean(y, axis=0, keepdims=True)
        yc = y - mu
        var = jnp.mean(yc * yc, axis=0, keepdims=True)
        bonus = jnp.sum(vec_ref[slot, 4] * vec_ref[slot, 3] * rk_ref[...], axis=0, keepdims=True)
        store_step(z_ref, t, yc * lax.rsqrt(var + RW_GN_EPS) * gg_ref[...] + gb_ref[...] + bonus * v)

    prepare(0, 0)
    if tc == 1:
        step(0, 0)
    else:
        def two_steps(i, carry):
            t = 2 * i
            prepare(t + 1, 1)
            step(t, 0)
            prepare(jnp.minimum(t + 2, tc - 1), 0)
            step(t + 1, 1)
            return carry

        lax.fori_loop(0, tc // 2, two_steps, 0)

    @pl.when(c == pl.num_programs(1) - 1)
    def _():
        s_out_ref[...] = s_ref[...]


RW_SEQ_PER_TILE = RW_LANES // RW_HEADS


def _rwkv_lane_heads():
    half = jnp.arange(2)[:, None, None]
    pair = jnp.arange(RW_PAIRS)[None, :, None]
    return jnp.broadcast_to(2 * pair + half, (2, RW_PAIRS, RW_SEQ_PER_TILE)).reshape(RW_LANES)


def _rwkv_scan(r, d, k, v, a, s0, k_k, k_a, r_k, gn_g, gn_b):
    b_, t_, _ = r.shape
    n = RW_HEAD
    tc = RW_TIME_CHUNK if t_ % RW_TIME_CHUNK == 0 else t_
    table = lambda p: p.reshape(RW_HEADS, n)[_rwkv_lane_heads()].T
    seq = pl.BlockSpec((RW_SEQ_PER_TILE, tc, D_MODEL), lambda l, c: (l, c, 0))
    state = pl.BlockSpec((n, n, RW_LANES), lambda l, c: (0, 0, l))
    tab = pl.BlockSpec((n, RW_LANES), lambda l, c: (0, 0))
    return pl.pallas_call(
        _rwkv_scan_kernel,
        grid=(b_ // RW_SEQ_PER_TILE, t_ // tc),
        in_specs=[seq] * 5 + [state] + [tab] * 5,
        out_specs=[seq, state],
        out_shape=[jax.ShapeDtypeStruct(r.shape, jnp.float32),
                   jax.ShapeDtypeStruct(s0.shape, jnp.float32)],
        scratch_shapes=[pltpu.VMEM((n, n, RW_LANES), jnp.float32),
                        pltpu.VMEM((2, 6, n, RW_LANES), jnp.float32)],
        compiler_params=pltpu.CompilerParams(
            dimension_semantics=("parallel", "arbitrary"),
            vmem_limit_bytes=V7X_VMEM_LIMIT_BYTES),
        name="rwkv_scan",
    )(r, d, k, v, a, s0, table(k_k), table(k_a), table(r_k), table(gn_g), table(gn_b))


def _rwkv_state_to_lanes(wkv):
    b_ = wkv.shape[0]
    w = wkv.astype(jnp.float32).reshape(b_ // RW_SEQ_PER_TILE, RW_SEQ_PER_TILE, RW_PAIRS, 2, RW_HEAD, RW_HEAD)
    return jnp.transpose(w, (5, 4, 0, 3, 2, 1)).reshape(RW_HEAD, RW_HEAD, b_ * RW_HEADS)


def _rwkv_state_from_lanes(s, b_):
    w = s.reshape(RW_HEAD, RW_HEAD, b_ // RW_SEQ_PER_TILE, 2, RW_PAIRS, RW_SEQ_PER_TILE)
    return jnp.transpose(w, (2, 5, 4, 3, 1, 0)).reshape(b_, RW_HEADS, RW_HEAD, RW_HEAD)


def _rwkv7_mixer_pallas(x3d, shift, wkv, mu, w_r, w_k, w_v, w0, w1, w2, a0, a1, a2, g1, g2,
                        k_k, k_a, r_k, gn_g, gn_b):
    b_, t_, _ = x3d.shape
    params = (mu, w_r, w_k, w_v, w0, w1, w2, a0, a1, a2, g1, g2)
    if t_ == 1:
        *seqs, g = _rwkv_project_step(x3d.reshape(b_, D_MODEL), shift, *params)
    else:
        *seqs, g = _rwkv_project_seq(x3d, shift, *params)
    seqs = [u.reshape(b_, t_, D_MODEL) for u in seqs]
    z, s = _rwkv_scan(*seqs, _rwkv_state_to_lanes(wkv), k_k, k_a, r_k, gn_g, gn_b)
    return z.reshape(b_ * t_, D_MODEL), g, x3d[:, -1], _rwkv_state_from_lanes(s, b_).astype(wkv.dtype)


def _proj_gate_ln_kernel(x_ref, h_ref, gate_ref, w_ref, g_ref, b_ref, o_ref):
    h = (h_ref[...] * gate_ref[...]).astype(MXU_DTYPE)
    y = ALPHA * x_ref[...] + jnp.dot(h, w_ref[...], preferred_element_type=jnp.float32)
    o_ref[...] = _ln_rows(y, g_ref[...], b_ref[...])


def _proj_gate_post_norm(x2d, h2d, gate2d, w_out, g, b):
    m = x2d.shape[0]
    tm = _row_tile(m)
    rows = pl.BlockSpec((tm, D_MODEL), lambda i: (i, 0))
    vec = pl.BlockSpec((1, D_MODEL), lambda i: (0, 0))
    return pl.pallas_call(
        _proj_gate_ln_kernel,
        grid=(m // tm,),
        in_specs=[rows, rows, rows, pl.BlockSpec((D_MODEL, D_MODEL), lambda i: (0, 0)), vec, vec],
        out_specs=rows,
        out_shape=jax.ShapeDtypeStruct((m, D_MODEL), jnp.float32),
        compiler_params=pltpu.CompilerParams(
            dimension_semantics=("parallel",),
            vmem_limit_bytes=V7X_VMEM_LIMIT_BYTES),
        name="proj_gate_post_norm",
    )(x2d, h2d, gate2d, w_out, g.reshape(1, D_MODEL), b.reshape(1, D_MODEL))


GM_ROW_TILE = 512
GM_ROW_PARTS = 2


def _gmlp_kernel(x_ref, win_ref, lng_ref, lnb_ref, mixw_ref, mixb_ref, wout_ref, g_ref, b_ref, *out_refs,
                 chunk_len, emit_v, parts):
    tp = x_ref.shape[0] // parts
    for part in range(parts):
        rows_p = slice(part * tp, (part + 1) * tp)
        x = x_ref[rows_p, :]
        h = jax.nn.gelu(jnp.dot(x.astype(MXU_DTYPE), win_ref[...], preferred_element_type=jnp.float32))
        u = h[:, :GM_WIDTH]
        v = _ln_rows(h[:, GM_WIDTH:], lng_ref[...], lnb_ref[...])
        if emit_v:
            out_refs[1][rows_p, :] = v
        if chunk_len == 1:
            gated = u * (v * mixw_ref[...] + mixb_ref[...])
        else:
            causal = (lax.broadcasted_iota(jnp.int32, (chunk_len, chunk_len), 0)
                      >= lax.broadcasted_iota(jnp.int32, (chunk_len, chunk_len), 1))
            vb = v.astype(MXU_DTYPE)
            cols = []
            for g in range(GM_GROUPS):
                w = jnp.where(causal, mixw_ref[g], 0.0).astype(MXU_DTYPE)
                bias = mixb_ref[:, g:g + 1]
                lanes = slice(g * GM_GROUP_DIM, (g + 1) * GM_GROUP_DIM)
                rows = [jnp.dot(w, vb[c * chunk_len:(c + 1) * chunk_len, lanes],
                                preferred_element_type=jnp.float32) + bias
                        for c in range(tp // chunk_len)]
                cols.append(jnp.concatenate(rows, axis=0))
            gated = u * jnp.concatenate(cols, axis=1)
        y = ALPHA * x + jnp.dot(gated.astype(MXU_DTYPE), wout_ref[...], preferred_element_type=jnp.float32)
        out_refs[0][rows_p, :] = _ln_rows(y, g_ref[...], b_ref[...])


def _gmlp_block(x2d, seq_len, w_in, ln_g, ln_b, ws, bs, w_out, g, b, emit_v):
    m = x2d.shape[0]
    chunk_len = min(seq_len, CHUNK)
    if chunk_len == 1:
        tm, parts = m, 1
        mixw = jnp.repeat(ws[:, 0, 0], GM_GROUP_DIM)[None, :]
        mixb = jnp.repeat(bs[:, 0], GM_GROUP_DIM)[None, :]
    else:
        tm, parts = GM_ROW_TILE, GM_ROW_PARTS
        mixw = ws[:, :chunk_len, :chunk_len]
        mixb = bs[:, :chunk_len].T
    full = lambda a: pl.BlockSpec(a.shape, lambda i: (0,) * a.ndim, pipeline_mode=pl.Buffered(1))
    rows = lambda n: pl.BlockSpec((tm, n), lambda i: (i, 0))
    consts = [w_in.astype(MXU_DTYPE), ln_g.reshape(1, GM_WIDTH), ln_b.reshape(1, GM_WIDTH), mixw, mixb,
              w_out.astype(MXU_DTYPE), g.reshape(1, D_MODEL), b.reshape(1, D_MODEL)]
    out_specs = [rows(D_MODEL)] + ([rows(GM_WIDTH)] if emit_v else [])
    out_shape = [jax.ShapeDtypeStruct((m, D_MODEL), jnp.float32)] + (
        [jax.ShapeDtypeStruct((m, GM_WIDTH), jnp.float32)] if emit_v else [])
    return pl.pallas_call(
        functools.partial(_gmlp_kernel, chunk_len=chunk_len, emit_v=emit_v, parts=parts),
        grid=(m // tm,),
        in_specs=[rows(D_MODEL)] + [full(a) for a in consts],
        out_specs=out_specs,
        out_shape=out_shape,
        compiler_params=pltpu.CompilerParams(
            dimension_semantics=("parallel",),
            vmem_limit_bytes=V7X_VMEM_LIMIT_BYTES),
        name="gmlp_block",
    )(x2d, *consts)


SSM_ROW_TILE = 256
SSM_ROW_PARTS = 2
SSM_BC_DIM = SSM_GROUPS * SSM_STATE
SSM_DT_LANES = LANES


def _ssm_activate(xb, xbc, taps, wz_ref, wdt_ref, cw_ref, cb_ref, dtb_ref, z_ref, xs_ref, bm_ref, cm_ref, dt_ref):
    conv = cb_ref[...] + xbc * cw_ref[SSM_CONV - 1:SSM_CONV, :]
    for j in range(SSM_CONV - 1):
        conv = conv + taps[j] * cw_ref[j:j + 1, :]
    act = conv * jax.nn.sigmoid(conv)
    xs_ref[...] = act[:, :SSM_D_INNER]
    bm_ref[...] = act[:, SSM_D_INNER:SSM_D_INNER + SSM_BC_DIM].astype(bm_ref.dtype)
    cm_ref[...] = act[:, SSM_D_INNER + SSM_BC_DIM:].astype(cm_ref.dtype)
    z_ref[...] = jnp.dot(xb, wz_ref[...], preferred_element_type=jnp.float32)
    dt_ref[...] = jax.nn.softplus(jnp.dot(xb, wdt_ref[...], preferred_element_type=jnp.float32) + dtb_ref[...])


def _ssm_proj_seq_kernel(x_ref, halo_ref, cs_ref, wx_ref, wz_ref, wdt_ref, cw_ref, cb_ref, dtb_ref,
                         z_ref, xs_ref, bm_ref, cm_ref, dt_ref, tail_ref):
    i = pl.program_id(1)
    tm = x_ref.shape[0]
    tp = tm // SSM_ROW_PARTS
    prev = jnp.dot(halo_ref[...].astype(MXU_DTYPE), wx_ref[...], preferred_element_type=jnp.float32)
    prev = jnp.where(i == 0, cs_ref[0], prev)
    row = lax.broadcasted_iota(jnp.int32, (SUBLANES, 1), 0)
    for part in range(SSM_ROW_PARTS):
        rows_p = slice(part * tp, (part + 1) * tp)
        xb = x_ref[rows_p, :].astype(MXU_DTYPE)
        xbc = jnp.dot(xb, wx_ref[...], preferred_element_type=jnp.float32)
        taps = []
        for j in range(SSM_CONV - 1):
            back = SSM_CONV - 1 - j
            rolled = pltpu.roll(xbc, back, 0)
            top = jnp.where(row < back, pltpu.roll(prev, back, 0), rolled[:SUBLANES])
            taps.append(jnp.concatenate([top, rolled[SUBLANES:]], axis=0))
        _ssm_activate(xb, xbc, taps, wz_ref, wdt_ref, cw_ref, cb_ref, dtb_ref,
                      z_ref.at[rows_p, :], xs_ref.at[rows_p, :], bm_ref.at[rows_p, :], cm_ref.at[rows_p, :],
                      dt_ref.at[rows_p, :])
        prev = xbc[tp - SUBLANES:, :]
    tail_ref[0] = prev


def _ssm_proj_step_kernel(x_ref, st_ref, wx_ref, wz_ref, wdt_ref, cw_ref, cb_ref, dtb_ref,
                          z_ref, xs_ref, bm_ref, cm_ref, dt_ref, st_out_ref):
    xb = x_ref[...].astype(MXU_DTYPE)
    xbc = jnp.dot(xb, wx_ref[...], preferred_element_type=jnp.float32)
    taps = [st_ref[j] for j in range(SSM_CONV - 1)]
    _ssm_activate(xb, xbc, taps, wz_ref, wdt_ref, cw_ref, cb_ref, dtb_ref, z_ref, xs_ref, bm_ref, cm_ref, dt_ref)
    for j in range(SSM_CONV - 2):
        st_out_ref[j] = st_ref[j + 1]
    st_out_ref[SSM_CONV - 2] = xbc


def _ssm_project(x3d, conv_state, w_in, conv_w, conv_b, dt_bias):
    b_, t_, _ = x3d.shape
    m = b_ * t_
    w_z, w_x, w_dt = jnp.split(w_in, [SSM_D_INNER, SSM_D_INNER + SSM_CONV_DIM], axis=1)
    cast = lambda w: w.astype(MXU_DTYPE)
    consts = [cast(w_x), cast(w_z), cast(jnp.pad(w_dt, ((0, 0), (0, SSM_DT_LANES - SSM_HEADS)))),
              conv_w, conv_b.reshape(1, SSM_CONV_DIM),
              jnp.pad(dt_bias, (0, SSM_DT_LANES - SSM_HEADS)).reshape(1, SSM_DT_LANES)]
    sds = jax.ShapeDtypeStruct
    outs = [sds((m, SSM_D_INNER), jnp.float32), sds((m, SSM_D_INNER), jnp.float32),
            sds((m, SSM_BC_DIM), MXU_DTYPE), sds((m, SSM_BC_DIM), MXU_DTYPE), sds((m, SSM_DT_LANES), jnp.float32)]
    widths = [SSM_D_INNER, SSM_D_INNER, SSM_BC_DIM, SSM_BC_DIM, SSM_DT_LANES]
    params = dict(vmem_limit_bytes=V7X_VMEM_LIMIT_BYTES)
    x2d = x3d.reshape(m, D_MODEL)
    if t_ == 1:
        full = lambda a: pl.BlockSpec(a.shape, lambda i: (0,) * a.ndim)
        st = jnp.transpose(conv_state, (1, 0, 2))
        res = pl.pallas_call(
            _ssm_proj_step_kernel,
            grid=(1,),
            in_specs=[full(x2d), full(st)] + [full(a) for a in consts],
            out_specs=[pl.BlockSpec((m, w), lambda i: (0, 0)) for w in widths] + [full(st)],
            out_shape=outs + [sds(st.shape, jnp.float32)],
            compiler_params=pltpu.CompilerParams(dimension_semantics=("arbitrary",), **params),
            name="ssm_project_step",
        )(x2d, st, *consts)
        return list(res[:5]) + [jnp.transpose(res[5], (1, 0, 2))]
    tm = SSM_ROW_TILE
    nt = t_ // tm
    full = lambda a: pl.BlockSpec(a.shape, lambda b, i: (0,) * a.ndim)
    rows = lambda w: pl.BlockSpec((tm, w), lambda b, i: (b * nt + i, 0))
    halo = pl.BlockSpec((SUBLANES, D_MODEL), lambda b, i: (jnp.maximum((b * nt + i) * (tm // SUBLANES) - 1, 0), 0))
    cs8 = jnp.pad(conv_state, ((0, 0), (SUBLANES - (SSM_CONV - 1), 0), (0, 0)))
    tail = pl.BlockSpec((1, SUBLANES, SSM_CONV_DIM), lambda b, i: (b, 0, 0))
    res = pl.pallas_call(
        _ssm_proj_seq_kernel,
        grid=(b_, nt),
        in_specs=[rows(D_MODEL), halo, tail] + [full(a) for a in consts],
        out_specs=[rows(w) for w in widths] + [tail],
        out_shape=outs + [sds((b_, SUBLANES, SSM_CONV_DIM), jnp.float32)],
        compiler_params=pltpu.CompilerParams(dimension_semantics=("parallel", "arbitrary"), **params),
        name="ssm_project_seq",
    )(x2d, x2d, cs8, *consts)
    return list(res[:5]) + [res[5][:, SUBLANES - (SSM_CONV - 1):, :]]


def _ssm_gate_norm(y, xs, z, dskip, normg):
    yg = (y + xs * dskip) * (z * jax.nn.sigmoid(z))
    gw = SSM_D_INNER // SSM_GROUPS
    outs = []
    for g in range(SSM_GROUPS):
        part = yg[:, g * gw:(g + 1) * gw]
        ms = jnp.mean(part * part, axis=-1, keepdims=True)
        outs.append(part * lax.rsqrt(ms + LN_EPS))
    return jnp.concatenate(outs, axis=1) * normg


def _ssm_chunk_kernel(xs_ref, bm_ref, cm_ref, dt_ref, z_ref, aneg_ref, dskip_ref, normg_ref,
                      yg_ref, h_out_ref, h_ref, yT_ref, xe_ref):
    c = pl.program_id(1)
    l = xs_ref.shape[0]
    hd = SSM_HEAD_DIM

    @pl.when(c == 0)
    def _():
        h_ref[...] = jnp.zeros_like(h_ref)

    dot = lambda u, w: jnp.dot(u, w, preferred_element_type=jnp.float32)
    dt = dt_ref[...]
    a = dt * aneg_ref[...]
    r_i = lax.broadcasted_iota(jnp.int32, (l, l), 0)
    c_i = lax.broadcasted_iota(jnp.int32, (l, l), 1)
    tril = jnp.where(r_i >= c_i, 1.0, 0.0)
    hi = lax.Precision.HIGHEST
    acum = jnp.dot(tril, a, precision=hi, preferred_element_type=jnp.float32)
    acum_t = jnp.dot(a.T, tril.T, precision=hi, preferred_element_type=jnp.float32)
    dt_t = dt.T
    to_end_t = jnp.exp(acum_t[:, l - 1:l] - acum_t)
    from_start_t = jnp.exp(acum_t)
    chunk_decay = jnp.exp(acum[l - 1:l, :])
    upper = r_i <= c_i
    xs = xs_ref[...]
    for g in range(SSM_GROUPS):
        bm = bm_ref[:, g * SSM_STATE:(g + 1) * SSM_STATE]
        cm_t = cm_ref[:, g * SSM_STATE:(g + 1) * SSM_STATE].astype(jnp.float32).T.astype(MXU_DTYPE)
        cb_t = dot(bm, cm_t)
        h_in = h_ref[g * SSM_HPG:(g + 1) * SSM_HPG].reshape(SSM_HPG * hd, SSM_STATE)
        y_off = dot(h_in.astype(MXU_DTYPE), cm_t)
        for e in range(SSM_HPG):
            h = g * SSM_HPG + e
            if h % 2 == 0:
                xs_pair_t = xs[:, h * hd:(h + 2) * hd].T
            xdt_t = xs_pair_t[(h % 2) * hd:(h % 2 + 1) * hd] * dt_t[h:h + 1, :]
            seg = jnp.exp(jnp.where(upper, acum_t[h:h + 1, :] - acum[:, h:h + 1], -jnp.inf))
            y_diag = dot(xdt_t.astype(MXU_DTYPE), (cb_t * seg).astype(MXU_DTYPE))
            yT_ref[h * hd:(h + 1) * hd, :] = y_diag + y_off[e * hd:(e + 1) * hd] * from_start_t[h:h + 1, :]
            xe_ref[e * hd:(e + 1) * hd, :] = (xdt_t * to_end_t[h:h + 1, :]).astype(xe_ref.dtype)
        states = dot(xe_ref[...], bm)
        for e in range(SSM_HPG):
            h = g * SSM_HPG + e
            h_ref[h] = h_ref[h] * chunk_decay[:, h:h + 1] + states[e * hd:(e + 1) * hd]
    y = jnp.concatenate([yT_ref[i * l:(i + 1) * l, :].T for i in range(SSM_D_INNER // l)], axis=1)
    yg_ref[...] = _ssm_gate_norm(y, xs, z_ref[...], dskip_ref[...], normg_ref[...]).astype(yg_ref.dtype)

    @pl.when(c == pl.num_programs(1) - 1)
    def _():
        h_out_ref[0] = h_ref[...]


def _ssm_head_lanes(p):
    return jnp.pad(p.astype(jnp.float32), (0, SSM_DT_LANES - SSM_HEADS)).reshape(1, SSM_DT_LANES)


def _ssm_chunk_scan(b_, t_, xs, bm, cm, dt, z, a_log, d_skip, norm_g):
    l = SSM_CHUNK
    nc = t_ // l
    rows = lambda w: pl.BlockSpec((l, w), lambda b, c: (b * nc + c, 0))
    vec = lambda w: pl.BlockSpec((1, w), lambda b, c: (0, 0))
    aneg = _ssm_head_lanes(-jnp.exp(a_log.astype(jnp.float32)))
    dskip = jnp.repeat(d_skip, SSM_HEAD_DIM).reshape(1, SSM_D_INNER)
    yg, h_new = pl.pallas_call(
        _ssm_chunk_kernel,
        grid=(b_, nc),
        in_specs=[rows(SSM_D_INNER), rows(SSM_BC_DIM), rows(SSM_BC_DIM), rows(SSM_DT_LANES), rows(SSM_D_INNER),
                  vec(SSM_DT_LANES), vec(SSM_D_INNER), vec(SSM_D_INNER)],
        out_specs=[rows(SSM_D_INNER),
                   pl.BlockSpec((1, SSM_HEADS, SSM_HEAD_DIM, SSM_STATE), lambda b, c: (b, 0, 0, 0))],
        out_shape=[jax.ShapeDtypeStruct((b_ * t_, SSM_D_INNER), MXU_DTYPE),
                   jax.ShapeDtypeStruct((b_, SSM_HEADS, SSM_HEAD_DIM, SSM_STATE), jnp.float32)],
        scratch_shapes=[pltpu.VMEM((SSM_HEADS, SSM_HEAD_DIM, SSM_STATE), jnp.float32),
                        pltpu.VMEM((SSM_D_INNER, l), jnp.float32),
                        pltpu.VMEM((SSM_HPG * SSM_HEAD_DIM, l), MXU_DTYPE)],
        compiler_params=pltpu.CompilerParams(
            dimension_semantics=("parallel", "arbitrary"),
            vmem_limit_bytes=V7X_VMEM_LIMIT_BYTES),
        name="ssm_chunk_scan",
    )(xs, bm, cm, dt, z, aneg, dskip, norm_g.reshape(1, SSM_D_INNER))
    return yg, h_new


def _ssm_step_kernel(h0_ref, xs_ref, dt_ref, an_ref, bm_ref, cm_ref, y_ref, h_ref):
    h0 = h0_ref[0]
    dt = dt_ref[0]
    decay = jnp.exp(dt * an_ref[...])
    xdt = xs_ref[0] * dt
    bm = bm_ref[0].astype(jnp.float32)
    cm = cm_ref[0].astype(jnp.float32)
    h_ref[0] = h0 * decay + xdt * bm
    cb = jnp.sum(cm * bm, axis=-1, keepdims=True)
    y_ref[0] = cb * xdt + jnp.sum(cm * h0, axis=-1, keepdims=True) * decay


def _ssm_step(state, xs, bm, cm, dt, a_log):
    b_ = state.shape[0]
    per_head = lambda u: jnp.repeat(u.reshape(b_, SSM_GROUPS, 1, SSM_STATE), SSM_HPG, axis=1)
    xs4 = xs.reshape(b_, SSM_HEADS, SSM_HEAD_DIM, 1)
    dt4 = dt[:, :SSM_HEADS].reshape(b_, SSM_HEADS, 1, 1)
    an = (-jnp.exp(a_log.astype(jnp.float32))).reshape(SSM_HEADS, 1, 1)
    blk = lambda a: pl.BlockSpec((1,) + a.shape[1:], lambda b: (b, 0, 0, 0))
    args = [state.astype(jnp.float32), xs4, dt4, an, per_head(bm), per_head(cm)]
    y4, h_new = pl.pallas_call(
        _ssm_step_kernel,
        grid=(b_,),
        in_specs=[blk(args[0]), blk(xs4), blk(dt4), pl.BlockSpec(an.shape, lambda b: (0, 0, 0)),
                  blk(args[4]), blk(args[5])],
        out_specs=[blk(xs4), blk(args[0])],
        out_shape=[jax.ShapeDtypeStruct(xs4.shape, jnp.float32), jax.ShapeDtypeStruct(state.shape, jnp.float32)],
        compiler_params=pltpu.CompilerParams(
            dimension_semantics=("parallel",),
            vmem_limit_bytes=V7X_VMEM_LIMIT_BYTES),
        name="ssm_step",
    )(*args)
    return y4.reshape(b_, SSM_D_INNER), h_new


def _ssm_gate_norm_kernel(y_ref, xs_ref, z_ref, dskip_ref, normg_ref, o_ref):
    o_ref[...] = _ssm_gate_norm(y_ref[...], xs_ref[...], z_ref[...], dskip_ref[...], normg_ref[...]).astype(o_ref.dtype)


def _ssm_gate_norm_rows(y, xs, z, d_skip, norm_g):
    full = lambda a: pl.BlockSpec(a.shape, lambda i: (0,) * a.ndim)
    args = [y, xs, z, jnp.repeat(d_skip, SSM_HEAD_DIM).reshape(1, SSM_D_INNER), norm_g.reshape(1, SSM_D_INNER)]
    return pl.pallas_call(
        _ssm_gate_norm_kernel,
        grid=(1,),
        in_specs=[full(a) for a in args],
        out_specs=full(y),
        out_shape=jax.ShapeDtypeStruct(y.shape, MXU_DTYPE),
        name="ssm_gate_norm",
    )(*args)


def _mamba2_mixer_pallas(x3d, conv_state, ssm_state, w_in, conv_w, conv_b, dt_bias, a_log, d_skip, norm_g):
    b_, t_, _ = x3d.shape
    z, xs, bm, cm, dt, conv_new = _ssm_project(x3d, conv_state, w_in, conv_w, conv_b, dt_bias)
    if t_ == 1:
        y, h_new = _ssm_step(ssm_state, xs, bm, cm, dt, a_log)
        yg = _ssm_gate_norm_rows(y, xs, z, d_skip, norm_g)
    else:
        yg, h_new = _ssm_chunk_scan(b_, t_, xs, bm, cm, dt, z, a_log, d_skip, norm_g)
    return yg, conv_new, h_new.astype(ssm_state.dtype)


PAGES_PER_STEP = 8


def _sortable_key(score):
    bits = pltpu.bitcast(score, jnp.int32)
    return jnp.where(bits >= 0, bits, bits ^ jnp.int32(0x7FFFFFFF))


def _decode_score_kernel(pt_ref, iq_ref, iw_ref, ikn_ref, *rest):
    idx_refs, (key_ref, knew_ref) = rest[:-2], rest[-2:]
    iq, iw = iq_ref[0], iw_ref[0]
    weigh = lambda sc: jnp.sum(iw * jnp.maximum(sc, 0.0), axis=0, keepdims=True)
    ik_t = jnp.concatenate([r[0] for r in idx_refs], axis=1).astype(MXU_DTYPE)
    key_ref[0] = _sortable_key(weigh(jnp.dot(iq, ik_t, preferred_element_type=jnp.float32)))

    @pl.when(pl.program_id(1) == 0)
    def _():
        sc_new = jnp.sum(iq.astype(jnp.float32) * ikn_ref[0].astype(jnp.float32), axis=1, keepdims=True)
        knew_ref[0] = jnp.broadcast_to(_sortable_key(weigh(sc_new)), knew_ref.shape[1:])


def _decode_select_kernel(keys_ref, knew_ref, thr_ref, last_ref, *, topk, col_bits):
    keys = keys_ref[...]
    key_new = knew_ref[:, 0:1]
    past = keys.shape[1]
    col = lax.broadcasted_iota(jnp.int32, keys.shape, 1)

    def count(pred_past, pred_new):
        hit = jnp.where(pred_past(keys, col), 1.0, 0.0)
        cnt = _tree_sum([hit[:, l * LANES:(l + 1) * LANES] for l in range(past // LANES)])
        return jnp.sum(cnt, axis=1, keepdims=True) + jnp.where(pred_new(key_new), 1.0, 0.0)

    def at_least(cand):
        return count(lambda k, c: k >= cand, lambda k: k >= cand) >= float(topk)

    def two_bits(i, thr):
        hi = lax.shift_left(jnp.int32(1), jnp.int32(31) - 2 * i)
        lo = lax.shift_left(jnp.int32(1), jnp.int32(30) - 2 * i)
        c1, c2, c3 = thr ^ lo, thr ^ hi, thr ^ hi ^ lo
        return jnp.where(at_least(c3), c3, jnp.where(at_least(c2), c2, jnp.where(at_least(c1), c1, thr)))

    thr = lax.fori_loop(0, 16, two_bits, jnp.full(key_new.shape, INT32_MIN, jnp.int32))
    need = float(topk) - count(lambda k, c: k > thr, lambda k: k > thr)

    def col_body(i, last):
        cand = last | lax.shift_left(jnp.int32(1), jnp.int32(col_bits - 1) - i)
        ties = count(lambda k, c: (k == thr) & (c < cand), lambda k: (k == thr) & (jnp.int32(past) < cand))
        return jnp.where(ties < need, cand, last)

    n_tied = count(lambda k, c: k == thr, lambda k: k == thr)
    excess = jnp.max(jnp.where(n_tied > need, 1.0, 0.0), axis=0, keepdims=True)
    last_tie = lax.cond(excess[0, 0] > 0.0,
                        lambda: lax.fori_loop(0, col_bits, col_body, jnp.zeros(key_new.shape, jnp.int32)),
                        lambda: jnp.full(key_new.shape, 2 ** col_bits - 1, jnp.int32))
    thr_ref[...] = jnp.broadcast_to(thr, thr_ref.shape)
    last_ref[...] = jnp.broadcast_to(last_tie, last_ref.shape)


def _decode_attend_kernel(pt_ref, q_ref, kn_ref, vn_ref, key_ref, knew_ref, thr_ref, last_ref, *rest,
                          n_steps, pages):
    k_refs, v_refs = rest[:pages], rest[pages:2 * pages]
    o_ref, m_ref, l_ref, acc_ref = rest[2 * pages:]
    s = pl.program_id(1)
    nk = key_ref.shape[2]
    nt = (((1,), (1,)), ((), ()))
    thr, last_tie = thr_ref[0, :, 0:1], last_ref[0, :, 0:1]
    keep_mask = lambda key, col: (key > thr) | ((key == thr) & (col <= last_tie))
    gsz = ATT_HEADS // ATT_KV_HEADS
    q = q_ref[0]
    q_wide = jnp.concatenate([q] * ATT_KV_HEADS, axis=1)
    head_i = lax.broadcasted_iota(jnp.int32, q_wide.shape, 0)
    col_i = lax.broadcasted_iota(jnp.int32, q_wide.shape, 1)
    own_group = (col_i // ATT_HEAD_DIM) == (head_i // gsz)
    q_blk = jnp.where(own_group, q_wide, jnp.zeros_like(q_wide))

    def online_update(logits, weighted_values):
        m_old = m_ref[...]
        m_new = jnp.maximum(m_old, jnp.max(logits, axis=1, keepdims=True))
        p = jnp.exp(logits - m_new)
        alpha = jnp.exp(m_old - m_new)
        l_ref[...] = alpha * l_ref[...] + jnp.sum(p, axis=1, keepdims=True)
        acc_ref[...] = alpha * acc_ref[...] + weighted_values(p.astype(MXU_DTYPE))
        m_ref[...] = m_new

    @pl.when(s == 0)
    def _init():
        m_ref[...] = jnp.full(m_ref.shape, MASK_NEG, jnp.float32)
        l_ref[...] = jnp.zeros(l_ref.shape, jnp.float32)
        acc_ref[...] = jnp.zeros(acc_ref.shape, jnp.float32)

    col = lax.broadcasted_iota(jnp.int32, (1, nk), 1) + s * nk
    bias = jnp.where(keep_mask(key_ref[0], col), 0.0, MASK_NEG)
    k_t = jnp.concatenate([r[0] for r in k_refs], axis=1).astype(MXU_DTYPE)
    v_t = jnp.concatenate([r[0] for r in v_refs], axis=1).astype(MXU_DTYPE)
    online_update(jnp.dot(q_blk, k_t, preferred_element_type=jnp.float32) + bias,
                  lambda p: lax.dot_general(p, v_t, nt, preferred_element_type=jnp.float32))

    @pl.when(s == n_steps - 1)
    def _finish():
        keep_new = keep_mask(knew_ref[0, :, 0:1], jnp.int32(n_steps * nk))
        logit = jnp.sum(q_blk.astype(jnp.float32) * kn_ref[0].astype(jnp.float32), axis=1, keepdims=True)
        v_row = vn_ref[0].astype(jnp.float32)
        online_update(logit + jnp.where(keep_new, 0.0, MASK_NEG), lambda p: p.astype(jnp.float32) * v_row)
        out = jnp.where(own_group, acc_ref[...] / l_ref[...], 0.0)
        o = out[:, 0:ATT_HEAD_DIM]
        for g in range(1, ATT_KV_HEADS):
            o = o + out[:, g * ATT_HEAD_DIM:(g + 1) * ATT_HEAD_DIM]
        o_ref[0] = o.astype(o_ref.dtype)


def _dsa_decode_split(q, iq, iw, ik_new, k_new, v_new, cache_k, cache_v, cache_idx_k, page_table):
    b_, n_pages = page_table.shape
    n_pool, page = cache_k.shape[0], cache_k.shape[1]
    pages = PAGES_PER_STEP
    n_steps = n_pages // pages
    nk = pages * page
    past = n_pages * page
    ck = jnp.transpose(cache_k, (0, 2, 3, 1)).reshape(n_pool, ATT_KV_DIM, page)
    cv = jnp.transpose(cache_v, (0, 2, 3, 1)).reshape(n_pool, ATT_KV_DIM, page)
    cik = jnp.swapaxes(cache_idx_k, 1, 2)
    per_seq = lambda a: pl.BlockSpec((1,) + a.shape[1:], lambda b, s, pt: (b,) + (0,) * (a.ndim - 1))
    paged = lambda width, j: pl.BlockSpec((1, width, page), lambda b, s, pt: (pt[b, s * pages + j], 0, 0))
    key_blk = pl.BlockSpec((1, 1, nk), lambda b, s, pt: (b, 0, s))
    lane_blk = pl.BlockSpec((1, 1, LANES), lambda b, s, pt: (b, 0, 0))
    params = pltpu.CompilerParams(dimension_semantics=("parallel", "arbitrary"),
                                  vmem_limit_bytes=V7X_VMEM_LIMIT_BYTES)

    score_in = [iq.reshape(b_, IDX_HEADS, IDX_DIM), iw[:, :IDX_HEADS].reshape(b_, IDX_HEADS, 1),
                ik_new.astype(MXU_DTYPE).reshape(b_, 1, IDX_DIM)]
    keys, key_new = pl.pallas_call(
        _decode_score_kernel,
        grid_spec=pltpu.PrefetchScalarGridSpec(
            num_scalar_prefetch=1, grid=(b_, n_steps),
            in_specs=[per_seq(a) for a in score_in] + [paged(IDX_DIM, j) for j in range(pages)],
            out_specs=[key_blk, lane_blk]),
        out_shape=[jax.ShapeDtypeStruct((b_, 1, past), jnp.int32), jax.ShapeDtypeStruct((b_, 1, LANES), jnp.int32)],
        compiler_params=params,
        name="dsa_decode_score",
    )(page_table, *score_in, *([cik] * pages))

    whole = lambda shape: pl.BlockSpec(shape, lambda i: (0,) * len(shape))
    thr, last_tie = pl.pallas_call(
        functools.partial(_decode_select_kernel, topk=min(TOPK_MAX, (past + 1) // 4),
                          col_bits=max(1, past.bit_length())),
        grid=(1,),
        in_specs=[whole((b_, past)), whole((b_, LANES))],
        out_specs=[whole((b_, LANES)), whole((b_, LANES))],
        out_shape=[jax.ShapeDtypeStruct((b_, LANES), jnp.int32)] * 2,
        compiler_params=pltpu.CompilerParams(vmem_limit_bytes=V7X_VMEM_LIMIT_BYTES),
        name="dsa_decode_select",
    )(keys.reshape(b_, past), key_new.reshape(b_, LANES))

    attend_in = [q.reshape(b_, ATT_HEADS, ATT_HEAD_DIM), k_new.astype(MXU_DTYPE).reshape(b_, 1, ATT_KV_DIM),
                 v_new.astype(MXU_DTYPE).reshape(b_, 1, ATT_KV_DIM)]
    o = pl.pallas_call(
        functools.partial(_decode_attend_kernel, n_steps=n_steps, pages=pages),
        grid_spec=pltpu.PrefetchScalarGridSpec(
            num_scalar_prefetch=1, grid=(b_, n_steps),
            in_specs=[per_seq(a) for a in attend_in] + [key_blk, lane_blk, lane_blk, lane_blk]
            + [paged(ATT_KV_DIM, j) for j in range(pages)] * 2,
            out_specs=pl.BlockSpec((1, ATT_HEADS, ATT_HEAD_DIM), lambda b, s, pt: (b, 0, 0)),
            scratch_shapes=[pltpu.VMEM((ATT_HEADS, 1), jnp.float32), pltpu.VMEM((ATT_HEADS, 1), jnp.float32),
                            pltpu.VMEM((ATT_HEADS, ATT_KV_DIM), jnp.float32)]),
        out_shape=jax.ShapeDtypeStruct((b_, ATT_HEADS, ATT_HEAD_DIM), MXU_DTYPE),
        compiler_params=params,
        name="dsa_decode_attend",
    )(page_table, *attend_in, keys, key_new, thr.reshape(b_, 1, LANES), last_tie.reshape(b_, 1, LANES),
      *([ck] * pages), *([cv] * pages))
    return o.reshape(b_, ATT_Q_DIM)


def _dsa_sample_pallas(x3d, cache_k, cache_v, cache_idx_k, page_table, w_in, kn_g, kn_b):
    b_, t_, _ = x3d.shape
    past = page_table.shape[1] * cache_k.shape[1]
    pos = jnp.full((b_,), past, jnp.int32)
    q, iq, v, _, iw, kT, _, ikT, _ = _dsa_project(x3d.reshape(1, b_, D_MODEL), pos, w_in, kn_g, kn_b)
    k = _untranspose_groups(kT)[0]
    ik = _untranspose_groups(ikT)[0]
    o = _dsa_decode_split(q, iq, iw, ik, k, v, cache_k, cache_v, cache_idx_k, page_table)
    kv4 = lambda u: u.reshape(b_, t_, ATT_KV_HEADS, ATT_HEAD_DIM)
    return o, kv4(k), kv4(v), ik.reshape(b_, t_, IDX_DIM)


def kernel(x_prompt, x_sample, state_ssm_conv, state_ssm, cache_k, cache_v, cache_idx_k, state_rwkv_shift, state_rwkv_wkv, page_table, p_prompt, p_sample, ln_g, ln_b, ffn_w_up, ffn_w_down, ple_w_p, ple_w_g, ple_b_g, gm_w_in, gm_ln_g, gm_ln_b, gm_ws, gm_bs, gm_w_out, ssm_w_in, ssm_conv_w, ssm_conv_b, ssm_dt_bias, ssm_a_log, ssm_d, ssm_norm_g, ssm_w_out, att_w_in, att_kn_g, att_kn_b, att_w_out, rw_mu, rw_w_r, rw_w_k, rw_w_v, rw_w_o, rw_w0, rw_w1, rw_w2, rw_a0, rw_a1, rw_a2, rw_g1, rw_g2, rw_k_k, rw_k_a, rw_r_k, rw_gn_g, rw_gn_b):
    bp, tp, _ = x_prompt.shape
    bs_, ts, _ = x_sample.shape
    bf = lambda w: w.astype(jnp.bfloat16)
    w_up_bf, w_down_bf = bf(ffn_w_up), bf(ffn_w_down)
    ple_wp_bf, ple_wg_bf = bf(ple_w_p), bf(ple_w_g)
    pp3 = p_prompt.reshape(DEPTH, bp * tp, PLE_DIM)
    ps3 = p_sample.reshape(DEPTH, bs_ * ts, PLE_DIM)

    yp = x_prompt.reshape(bp * tp, D_MODEL)
    ys = x_sample.reshape(bs_ * ts, D_MODEL)
    r3p = lambda t: t.reshape(bp, tp, -1)
    r3s = lambda t: t.reshape(bs_, ts, -1)
    f2 = lambda t: t.reshape(-1, t.shape[-1])

    for i in range(DEPTH):
        yp = _ffn_sub(yp, w_up_bf, w_down_bf, i, 0, ln_g[i, 0], ln_b[i, 0])
        ys = _ffn_sub(ys, w_up_bf, w_down_bf, i, 0, ln_g[i, 0], ln_b[i, 0])
        m = i % N_MIXERS
        if m == 0:
            gm_args = (gm_w_in, gm_ln_g, gm_ln_b, gm_ws, gm_bs, gm_w_out, ln_g[i, 1], ln_b[i, 1])
            yp, = _gmlp_block(yp, tp, *gm_args, False)
            ys, gm_v_s = _gmlp_block(ys, ts, *gm_args, True)
            gm_v_s = r3s(gm_v_s)
        elif m == 1:
            ssm_args = (ssm_w_in, ssm_conv_w, ssm_conv_b, ssm_dt_bias, ssm_a_log, ssm_d, ssm_norm_g)
            hp, conv_p, ssm_p = _mamba2_mixer_pallas(
                r3p(yp), jnp.zeros((bp, SSM_CONV - 1, SSM_CONV_DIM), yp.dtype),
                jnp.zeros((bp, SSM_HEADS, SSM_HEAD_DIM, SSM_STATE), yp.dtype), *ssm_args)
            hs, conv_s, ssm_s = _mamba2_mixer_pallas(r3s(ys), state_ssm_conv, state_ssm, *ssm_args)
            w_out = bf(ssm_w_out)
        elif m == 2:
            hp, k_p, v_p, ik_p = _dsa_prompt_pallas(r3p(yp), att_w_in, att_kn_g, att_kn_b)
            hs, k_s, v_s, ik_s = _dsa_sample_pallas(r3s(ys), cache_k, cache_v, cache_idx_k, page_table,
                                                    att_w_in, att_kn_g, att_kn_b)
            w_out = bf(att_w_out)
        else:
            rw_args = (rw_mu, rw_w_r, rw_w_k, rw_w_v, rw_w0, rw_w1, rw_w2, rw_a0, rw_a1, rw_a2,
                       rw_g1, rw_g2, rw_k_k, rw_k_a, rw_r_k, rw_gn_g, rw_gn_b)
            hp, gate_p, sh_p, wkv_p = _rwkv7_mixer_pallas(
                r3p(yp), jnp.zeros((bp, D_MODEL), yp.dtype),
                jnp.zeros((bp, RW_HEADS, RW_HEAD, RW_HEAD), yp.dtype), *rw_args)
            hs, gate_s, sh_s, wkv_s = _rwkv7_mixer_pallas(r3s(ys), state_rwkv_shift, state_rwkv_wkv, *rw_args)
            w_out = bf(rw_w_o)
        if m == 3:
            yp = _proj_gate_post_norm(yp, hp, gate_p, w_out, ln_g[i, 1], ln_b[i, 1])
            ys = _proj_gate_post_norm(ys, hs, gate_s, w_out, ln_g[i, 1], ln_b[i, 1])
        elif m != 0:
            yp = _proj_post_norm(yp, f2(hp), w_out, ln_g[i, 1], ln_b[i, 1])
            ys = _proj_post_norm(ys, f2(hs), w_out, ln_g[i, 1], ln_b[i, 1])
        yp = _ffn_sub(yp, w_up_bf, w_down_bf, i, 1, ln_g[i, 2], ln_b[i, 2], (pp3, ple_wp_bf, ple_wg_bf, ple_b_g))
        ys = _ffn_sub(ys, w_up_bf, w_down_bf, i, 1, ln_g[i, 2], ln_b[i, 2], (ps3, ple_wp_bf, ple_wg_bf, ple_b_g))

    return (r3p(yp), r3s(ys), gm_v_s, conv_p, ssm_p, conv_s, ssm_s, k_p, v_p, ik_p, k_s, v_s, ik_s,
            sh_p, wkv_p, sh_s, wkv_s)
```

```python
import functools

import jax
import jax.numpy as jnp
from jax import lax
from jax.experimental import pallas as pl
from jax.experimental.pallas import tpu as pltpu

D_MODEL = 1024
DEPTH = 4
N_MIXERS = 4
PLE_DIM = 256
D_FF = 2816
ALPHA = (2 * DEPTH) ** 0.25
LN_EPS = 1e-5

CHUNK = 128
GM_WIDTH = 2 * D_MODEL
GM_GROUPS = 8
GM_GROUP_DIM = GM_WIDTH // GM_GROUPS

SSM_D_INNER = 2 * D_MODEL
SSM_HEAD_DIM = 64
SSM_HEADS = SSM_D_INNER // SSM_HEAD_DIM
SSM_GROUPS = 4
SSM_HPG = SSM_HEADS // SSM_GROUPS
SSM_STATE = 128
SSM_CONV = 4
SSM_CONV_DIM = SSM_D_INNER + 2 * SSM_GROUPS * SSM_STATE
SSM_CHUNK = 128

ATT_HEADS = 16
ATT_KV_HEADS = 4
ATT_HEAD_DIM = D_MODEL // ATT_HEADS
ROPE_DIM = ATT_HEAD_DIM // 4
ROPE_THETA = 500000.0
IDX_HEADS = 8
IDX_DIM = 64
IDX_ROPE_DIM = IDX_DIM // 4
TOPK_MAX = 256
Q_BLOCK = 128
ATT_Q_DIM = ATT_HEADS * ATT_HEAD_DIM
ATT_KV_DIM = ATT_KV_HEADS * ATT_HEAD_DIM
ATT_IN_SPLITS = (ATT_Q_DIM, ATT_Q_DIM + ATT_KV_DIM, ATT_Q_DIM + 2 * ATT_KV_DIM,
                 ATT_Q_DIM + 2 * ATT_KV_DIM + IDX_HEADS * IDX_DIM,
                 ATT_Q_DIM + 2 * ATT_KV_DIM + IDX_HEADS * IDX_DIM + IDX_DIM)

RW_HEAD = 64
RW_HEADS = D_MODEL // RW_HEAD
RW_GN_EPS = 64e-5

V7X_VMEM_LIMIT_BYTES = 52 * 1024 * 1024
LANES = 128
SUBLANES = 8
FF_TILE = D_FF // 2
ROW_TILE = 512


def _row_tile(m):
    return ROW_TILE if m % ROW_TILE == 0 else m


def _ln_rows(y, g, b):
    mu = jnp.mean(y, axis=-1, keepdims=True)
    yc = y - mu
    var = jnp.mean(yc * yc, axis=-1, keepdims=True)
    return yc * lax.rsqrt(var + LN_EPS) * g + b


FFN_ROW_TILE = 1024
FFN_ROW_PARTS = 4
FFN_PLE_ROW_PARTS = 2


def _ffn_kernel(x_ref, wu_ref, wd_ref, g_ref, b_ref, *rest, parts, with_ple):
    o_ref = rest[-1]
    rows = x_ref.shape[0] // parts
    for p in range(parts):
        x = x_ref[p * rows:(p + 1) * rows, :]
        xb = x.astype(MXU_DTYPE)
        acc = None
        for f in range(D_FF // FF_TILE):
            cols = slice(f * FF_TILE, (f + 1) * FF_TILE)
            gate = jnp.dot(xb, wu_ref[:, cols], preferred_element_type=jnp.float32)
            lin = jnp.dot(xb, wu_ref[:, D_FF + f * FF_TILE:D_FF + (f + 1) * FF_TILE],
                          preferred_element_type=jnp.float32)
            h = (gate * jax.nn.sigmoid(gate) * lin).astype(MXU_DTYPE)
            part = jnp.dot(h, wd_ref[cols, :], preferred_element_type=jnp.float32)
            acc = part if acc is None else acc + part
        y = _ln_rows(ALPHA * x + 0.5 * acc, g_ref[...], b_ref[...])
        if with_ple:
            p_ref, wp_ref, wg_ref, bg_ref = rest[:4]
            gate = jax.nn.sigmoid(
                jnp.dot(y.astype(MXU_DTYPE), wg_ref[...], preferred_element_type=jnp.float32) + bg_ref[...])
            emb = jnp.dot(p_ref[p * rows:(p + 1) * rows, :].astype(MXU_DTYPE), wp_ref[...],
                          preferred_element_type=jnp.float32)
            y = y + gate * emb
        o_ref[p * rows:(p + 1) * rows, :] = y


def _ffn_sub(x2d, w_up, w_down, layer, half, g, b, ple=None):
    m = x2d.shape[0]
    tm = FFN_ROW_TILE if m % FFN_ROW_TILE == 0 else m
    parts = 1 if tm != FFN_ROW_TILE else (FFN_ROW_PARTS if ple is None else FFN_PLE_ROW_PARTS)
    resident = dict(pipeline_mode=pl.Buffered(1))
    vec = pl.BlockSpec((1, D_MODEL), lambda i: (0, 0))
    in_specs = [
        pl.BlockSpec((tm, D_MODEL), lambda i: (i, 0)),
        pl.BlockSpec((None, None, D_MODEL, 2 * D_FF), lambda i: (layer, half, 0, 0), **resident),
        pl.BlockSpec((None, None, D_FF, D_MODEL), lambda i: (layer, half, 0, 0), **resident),
        vec, vec]
    args = [x2d, w_up, w_down, g.reshape(1, D_MODEL), b.reshape(1, D_MODEL)]
    if ple is not None:
        p3d, w_p, w_g, b_g = ple
        in_specs += [pl.BlockSpec((None, tm, PLE_DIM), lambda i: (layer, i, 0)),
                     pl.BlockSpec((None, PLE_DIM, D_MODEL), lambda i: (layer, 0, 0), **resident),
                     pl.BlockSpec((None, D_MODEL, D_MODEL), lambda i: (layer, 0, 0), **resident),
                     pl.BlockSpec((None, 1, D_MODEL), lambda i: (layer, 0, 0))]
        args += [p3d, w_p, w_g, b_g.reshape(DEPTH, 1, D_MODEL)]
    return pl.pallas_call(
        functools.partial(_ffn_kernel, parts=parts, with_ple=ple is not None),
        grid=(m // tm,),
        in_specs=in_specs,
        out_specs=pl.BlockSpec((tm, D_MODEL), lambda i: (i, 0)),
        out_shape=jax.ShapeDtypeStruct((m, D_MODEL), jnp.float32),
        compiler_params=pltpu.CompilerParams(
            dimension_semantics=("parallel",),
            vmem_limit_bytes=V7X_VMEM_LIMIT_BYTES),
        name="ffn_ple" if ple is not None else "ffn_sub",
    )(*args)


def _proj_ln_kernel(x_ref, h_ref, w_ref, g_ref, b_ref, o_ref):
    y = ALPHA * x_ref[...] + jnp.dot(h_ref[...].astype(jnp.bfloat16), w_ref[...],
                                     preferred_element_type=jnp.float32)
    o_ref[...] = _ln_rows(y, g_ref[...], b_ref[...])


def _proj_post_norm(x2d, h2d, w_out, g, b):
    m = x2d.shape[0]
    k = h2d.shape[1]
    tm = _row_tile(m)
    return pl.pallas_call(
        _proj_ln_kernel,
        grid=(m // tm,),
        in_specs=[
            pl.BlockSpec((tm, D_MODEL), lambda i: (i, 0)),
            pl.BlockSpec((tm, k), lambda i: (i, 0)),
            pl.BlockSpec((k, D_MODEL), lambda i: (0, 0)),
            pl.BlockSpec((1, D_MODEL), lambda i: (0, 0)),
            pl.BlockSpec((1, D_MODEL), lambda i: (0, 0)),
        ],
        out_specs=pl.BlockSpec((tm, D_MODEL), lambda i: (i, 0)),
        out_shape=jax.ShapeDtypeStruct((m, D_MODEL), jnp.float32),
        compiler_params=pltpu.CompilerParams(
            dimension_semantics=("parallel",),
            vmem_limit_bytes=V7X_VMEM_LIMIT_BYTES),
        name="proj_post_norm",
    )(x2d, h2d, w_out, g.reshape(1, D_MODEL), b.reshape(1, D_MODEL))


MXU_DTYPE = jnp.bfloat16
KEY_GROUP = 512
INT32_MIN = -2 ** 31
MASK_NEG = -1e30


def _rope_lane_tables(pos, rot_dim, head_dim):
    half = rot_dim // 2
    inv = ROPE_THETA ** (-jnp.arange(half, dtype=jnp.float32) / half)
    ang = pos.astype(jnp.float32)[:, None] * inv[None, :]
    cos, sin = jnp.cos(ang), jnp.sin(ang)
    n = pos.shape[0]
    rest = head_dim - rot_dim
    c = jnp.concatenate([cos, cos, jnp.ones((n, rest), jnp.float32)], axis=1)
    s1 = jnp.concatenate([-sin, jnp.zeros((n, half + rest), jnp.float32)], axis=1)
    s2 = jnp.concatenate([jnp.zeros((n, half), jnp.float32), sin, jnp.zeros((n, rest), jnp.float32)], axis=1)
    reps = LANES // head_dim
    tile = lambda t: jnp.tile(t, (1, reps))
    return tile(c), tile(s1), tile(s2), cos.T, sin.T


def _rope_lanes(t, c, s1, s2, half):
    n = t.shape[1]
    reps = n // LANES
    tl = lambda a: jnp.concatenate([a] * reps, axis=1)
    return t * tl(c) + pltpu.roll(t, n - half, 1) * tl(s1) + pltpu.roll(t, half, 1) * tl(s2)


def _rope_rows(t, cT, sT, head_dim, half):
    pieces = []
    for h in range(t.shape[0] // head_dim):
        x1 = t[h * head_dim:h * head_dim + half]
        x2 = t[h * head_dim + half:h * head_dim + 2 * half]
        pieces += [x1 * cT - x2 * sT, x2 * cT + x1 * sT, t[h * head_dim + 2 * half:(h + 1) * head_dim]]
    return jnp.concatenate(pieces, axis=0)


def _dsa_proj_kernel(x_ref, wq_ref, wiq_ref, wv_ref, wvx_ref, wiw_ref, wkT_ref, wikT_ref,
                     c_ref, s1_ref, s2_ref, cT_ref, sT_ref, kng_ref, knb_ref, one_ref,
                     q_ref, iq_ref, v_ref, vx_ref, iw_ref, kT_ref, kTb_ref, ikT_ref, ikTb_ref):
    xb = x_ref[...].astype(MXU_DTYPE)
    c, s1, s2 = c_ref[...], s1_ref[...], s2_ref[...]
    cT, sT = cT_ref[...], sT_ref[...]
    dot = lambda a, b: jnp.dot(a, b, preferred_element_type=jnp.float32)
    dot_t = lambda w, a: lax.dot_general(w, a, (((1,), (1,)), ((), ())), preferred_element_type=jnp.float32)

    q = _rope_lanes(dot(xb, wq_ref[...]), c, s1, s2, ROPE_DIM // 2)
    q_ref[...] = (q * (ATT_HEAD_DIM ** -0.5)).astype(q_ref.dtype)
    iq = _rope_lanes(dot(xb, wiq_ref[...]), c, s1, s2, IDX_ROPE_DIM // 2)
    iq_ref[...] = iq.astype(iq_ref.dtype)
    v_ref[...] = dot(xb, wv_ref[...])
    vx_ref[...] = (dot(xb, wvx_ref[...]) + one_ref[...]).astype(vx_ref.dtype)
    iw_ref[...] = dot(xb, wiw_ref[...]) * (IDX_HEADS ** -0.5 * IDX_DIM ** -0.5)

    kT = _rope_rows(dot_t(wkT_ref[...], xb), cT, sT, ATT_HEAD_DIM, ROPE_DIM // 2)
    kT_ref[0, 0] = kT
    kTb_ref[0, 0] = kT.astype(kTb_ref.dtype)
    ikT = dot_t(wikT_ref[...], xb)
    mu = jnp.mean(ikT, axis=0, keepdims=True)
    ikc = ikT - mu
    var = jnp.mean(ikc * ikc, axis=0, keepdims=True)
    ikT = ikc * lax.rsqrt(var + LN_EPS) * kng_ref[...] + knb_ref[...]
    ikT = _rope_rows(ikT, cT, sT, IDX_DIM, IDX_ROPE_DIM // 2)
    ikT_ref[0, 0] = ikT
    ikTb_ref[0, 0] = ikT.astype(ikTb_ref.dtype)


def _dsa_project(x3d, pos, w_in, kn_g, kn_b):
    b_, t_, _ = x3d.shape
    tk = KEY_GROUP if t_ % KEY_GROUP == 0 else t_
    ng = t_ // tk
    m = b_ * t_
    w_q, w_k, w_v, w_iq, w_ik, w_iw = jnp.split(w_in, list(ATT_IN_SPLITS), axis=1)
    cast = lambda w: w.astype(MXU_DTYPE)
    w_vx = jnp.pad(w_v.reshape(D_MODEL, ATT_KV_HEADS, ATT_HEAD_DIM),
                   ((0, 0), (0, 0), (0, LANES - ATT_HEAD_DIM))).reshape(D_MODEL, ATT_KV_HEADS * LANES)
    one_col = jnp.tile((jnp.arange(LANES) == ATT_HEAD_DIM).astype(jnp.float32), ATT_KV_HEADS)[None, :]
    w_iw_pad = jnp.pad(w_iw, ((0, 0), (0, LANES - IDX_HEADS)))
    c, s1, s2, cT, sT = _rope_lane_tables(pos, ROPE_DIM, ATT_HEAD_DIM)
    full = lambda shape: pl.BlockSpec(shape, lambda b, i: (0,) * len(shape))
    rows = lambda n: pl.BlockSpec((tk, n), lambda b, i: (b * ng + i, 0))
    ptab = lambda n: pl.BlockSpec((tk, n), lambda b, i: (i, 0))
    grp = lambda n: pl.BlockSpec((1, 1, n, tk), lambda b, i: (b, i, 0, 0))
    sds = jax.ShapeDtypeStruct
    return pl.pallas_call(
        _dsa_proj_kernel,
        grid=(b_, ng),
        in_specs=[rows(D_MODEL), full((D_MODEL, ATT_Q_DIM)), full((D_MODEL, IDX_HEADS * IDX_DIM)),
                  full((D_MODEL, ATT_KV_DIM)), full((D_MODEL, ATT_KV_HEADS * LANES)), full((D_MODEL, LANES)),
                  full((ATT_KV_DIM, D_MODEL)), full((IDX_DIM, D_MODEL)),
                  ptab(LANES), ptab(LANES), ptab(LANES),
                  pl.BlockSpec((ROPE_DIM // 2, tk), lambda b, i: (0, i)),
                  pl.BlockSpec((ROPE_DIM // 2, tk), lambda b, i: (0, i)),
                  full((IDX_DIM, 1)), full((IDX_DIM, 1)), full((1, ATT_KV_HEADS * LANES))],
        out_specs=[rows(ATT_Q_DIM), rows(IDX_HEADS * IDX_DIM), rows(ATT_KV_DIM), rows(ATT_KV_HEADS * LANES),
                   rows(LANES), grp(ATT_KV_DIM), grp(ATT_KV_DIM), grp(IDX_DIM), grp(IDX_DIM)],
        out_shape=[sds((m, ATT_Q_DIM), MXU_DTYPE), sds((m, IDX_HEADS * IDX_DIM), MXU_DTYPE),
                   sds((m, ATT_KV_DIM), jnp.float32), sds((m, ATT_KV_HEADS * LANES), MXU_DTYPE),
                   sds((m, LANES), jnp.float32),
                   sds((b_, ng, ATT_KV_DIM, tk), jnp.float32), sds((b_, ng, ATT_KV_DIM, tk), MXU_DTYPE),
                   sds((b_, ng, IDX_DIM, tk), jnp.float32), sds((b_, ng, IDX_DIM, tk), MXU_DTYPE)],
        compiler_params=pltpu.CompilerParams(
            dimension_semantics=("parallel", "parallel"),
            vmem_limit_bytes=V7X_VMEM_LIMIT_BYTES),
        name="dsa_project",
    )(x3d.reshape(m, D_MODEL), cast(w_q), cast(w_iq), cast(w_v), cast(w_vx), cast(w_iw_pad),
      cast(w_k.T), cast(w_ik.T), c, s1, s2, cT, sT, kn_g.reshape(IDX_DIM, 1), kn_b.reshape(IDX_DIM, 1), one_col)


def _untranspose_groups(tg):
    b_, g_, r_, tk = tg.shape
    return jnp.transpose(tg, (0, 1, 3, 2)).reshape(b_, g_ * tk, r_)


def _dsa_proj_q_lanes_kernel(x_ref, wqT_ref, wiqT_ref, wiwT_ref, wk_ref, wv_ref, wvxT_ref, wik_ref,
                             c_ref, s1_ref, s2_ref, cT_ref, sT_ref, kng_ref, knb_ref, onerow_ref,
                             qT_ref, iqT_ref, iwT_ref, k_ref, khd_ref, v_ref, vxT_ref, ik_ref, ikb_ref):
    xb = x_ref[...].astype(MXU_DTYPE)
    tm = xb.shape[0]
    c, s1, s2 = c_ref[...], s1_ref[...], s2_ref[...]
    cT, sT = cT_ref[...], sT_ref[...]
    dot = lambda a, b: jnp.dot(a, b, preferred_element_type=jnp.float32)
    dot_t = lambda w, a: lax.dot_general(w, a, (((1,), (1,)), ((), ())), preferred_element_type=jnp.float32)

    qT = _rope_rows(dot_t(wqT_ref[...], xb), cT, sT, ATT_HEAD_DIM, ROPE_DIM // 2) * (ATT_HEAD_DIM ** -0.5)
    iqT = _rope_rows(dot_t(wiqT_ref[...], xb), cT, sT, IDX_DIM, IDX_ROPE_DIM // 2)
    iwT = dot_t(wiwT_ref[...], xb) * (IDX_HEADS ** -0.5 * IDX_DIM ** -0.5)
    for t in range(tm // Q_BLOCK):
        lanes = slice(t * Q_BLOCK, (t + 1) * Q_BLOCK)
        qT_ref[0, t] = qT[:, lanes].astype(qT_ref.dtype)
        iqT_ref[0, t] = iqT[:, lanes].astype(iqT_ref.dtype)
        iwT_ref[0, t] = iwT[:, lanes]

    k = _rope_lanes(dot(xb, wk_ref[...]), c, s1, s2, ROPE_DIM // 2)
    k_ref[...] = k
    for g in range(ATT_KV_HEADS):
        khd_ref[g] = k[:, g * ATT_HEAD_DIM:(g + 1) * ATT_HEAD_DIM].astype(khd_ref.dtype)
    v_ref[...] = dot(xb, wv_ref[...])
    vxT_ref[0, 0] = (dot_t(wvxT_ref[...], xb) + onerow_ref[...]).astype(vxT_ref.dtype)

    ik = dot(xb, wik_ref[...])
    real = lax.broadcasted_iota(jnp.int32, ik.shape, 1) < IDX_DIM
    mu = jnp.sum(ik, axis=-1, keepdims=True) * (1.0 / IDX_DIM)
    ikc = jnp.where(real, ik - mu, 0.0)
    var = jnp.sum(ikc * ikc, axis=-1, keepdims=True) * (1.0 / IDX_DIM)
    ikn = _rope_lanes(ikc * lax.rsqrt(var + LN_EPS) * kng_ref[...] + knb_ref[...], c, s1, s2, IDX_ROPE_DIM // 2)
    ik_ref[...] = ikn[:, :IDX_DIM]
    ikb_ref[...] = ikn[:, :IDX_DIM].astype(ikb_ref.dtype)


def _dsa_project_q_lanes(x3d, pos, w_in, kn_g, kn_b):
    b_, t_, _ = x3d.shape
    tk = KEY_GROUP
    ng = t_ // tk
    nq = tk // Q_BLOCK
    m = b_ * t_
    w_q, w_k, w_v, w_iq, w_ik, w_iw = jnp.split(w_in, list(ATT_IN_SPLITS), axis=1)
    cast = lambda w: w.astype(MXU_DTYPE)
    w_vxT = jnp.pad(w_v.T.reshape(ATT_KV_HEADS, ATT_HEAD_DIM, D_MODEL),
                    ((0, 0), (0, LANES - ATT_HEAD_DIM), (0, 0))).reshape(ATT_KV_HEADS * LANES, D_MODEL)
    one_row = jnp.tile((jnp.arange(LANES) == ATT_HEAD_DIM).astype(jnp.float32), ATT_KV_HEADS)[:, None]
    pad_lanes = lambda a: jnp.pad(a, ((0, 0), (0, LANES - a.shape[1])))
    c, s1, s2, cT, sT = _rope_lane_tables(pos, ROPE_DIM, ATT_HEAD_DIM)
    full = lambda shape: pl.BlockSpec(shape, lambda b, i: (0,) * len(shape))
    rows = lambda n: pl.BlockSpec((tk, n), lambda b, i: (b * ng + i, 0))
    ptab = lambda n: pl.BlockSpec((tk, n), lambda b, i: (i, 0))
    qtile = lambda n: pl.BlockSpec((1, nq, n, Q_BLOCK), lambda b, i: (b, i, 0, 0))
    sds = jax.ShapeDtypeStruct
    return pl.pallas_call(
        _dsa_proj_q_lanes_kernel,
        grid=(b_, ng),
        in_specs=[rows(D_MODEL), full((ATT_Q_DIM, D_MODEL)), full((IDX_HEADS * IDX_DIM, D_MODEL)),
                  full((IDX_HEADS, D_MODEL)), full((D_MODEL, ATT_KV_DIM)), full((D_MODEL, ATT_KV_DIM)),
                  full((ATT_KV_HEADS * LANES, D_MODEL)), full((D_MODEL, LANES)),
                  ptab(LANES), ptab(LANES), ptab(LANES),
                  pl.BlockSpec((ROPE_DIM // 2, tk), lambda b, i: (0, i)),
                  pl.BlockSpec((ROPE_DIM // 2, tk), lambda b, i: (0, i)),
                  full((1, LANES)), full((1, LANES)), full((ATT_KV_HEADS * LANES, 1))],
        out_specs=[qtile(ATT_Q_DIM), qtile(IDX_HEADS * IDX_DIM), qtile(IDX_HEADS),
                   rows(ATT_KV_DIM), pl.BlockSpec((ATT_KV_HEADS, tk, ATT_HEAD_DIM), lambda b, i: (0, b * ng + i, 0)),
                   rows(ATT_KV_DIM), pl.BlockSpec((1, 1, ATT_KV_HEADS * LANES, tk), lambda b, i: (b, i, 0, 0)),
                   rows(IDX_DIM), rows(IDX_DIM)],
        out_shape=[sds((b_, t_ // Q_BLOCK, ATT_Q_DIM, Q_BLOCK), MXU_DTYPE),
                   sds((b_, t_ // Q_BLOCK, IDX_HEADS * IDX_DIM, Q_BLOCK), MXU_DTYPE),
                   sds((b_, t_ // Q_BLOCK, IDX_HEADS, Q_BLOCK), jnp.float32),
                   sds((m, ATT_KV_DIM), jnp.float32), sds((ATT_KV_HEADS, m, ATT_HEAD_DIM), MXU_DTYPE),
                   sds((m, ATT_KV_DIM), jnp.float32), sds((b_, ng, ATT_KV_HEADS * LANES, tk), MXU_DTYPE),
                   sds((m, IDX_DIM), jnp.float32), sds((m, IDX_DIM), MXU_DTYPE)],
        compiler_params=pltpu.CompilerParams(
            dimension_semantics=("parallel", "parallel"),
            vmem_limit_bytes=V7X_VMEM_LIMIT_BYTES),
        name="dsa_project_q_lanes",
    )(x3d.reshape(m, D_MODEL), cast(w_q.T), cast(w_iq.T), cast(w_iw.T), cast(w_k), cast(w_v), cast(w_vxT),
      cast(pad_lanes(w_ik)), c, s1, s2, cT, sT, pad_lanes(kn_g.reshape(1, IDX_DIM)),
      pad_lanes(kn_b.reshape(1, IDX_DIM)), one_row)


def _tree_sum(parts):
    while len(parts) > 1:
        parts = [parts[i] + parts[i + 1] for i in range(0, len(parts) - 1, 2)] + (
            [parts[-1]] if len(parts) % 2 else [])
    return parts[0]


def _dsa_attend_q_lanes_kernel(iqT_ref, iwT_ref, ik_ref, qT_ref, k_ref, vxT_ref, o_ref,
                               key_ref, bias_ref, m_ref, acc_ref, s_ref, *, topk, col_bits):
    j = pl.program_id(1)
    tk, tq = key_ref.shape[1], key_ref.shape[2]
    n_groups = (j * tq + tq + tk - 1) // tk
    qpos = j * tq + lax.broadcasted_iota(jnp.int32, (tk, tq), 1)
    kpos0 = lax.broadcasted_iota(jnp.int32, (tk, tq), 0)
    dot = lambda a, b: jnp.dot(a, b, preferred_element_type=jnp.float32)

    def score_body(g, carry):
        start = pl.multiple_of(g * tk, tk)
        w_iq = jnp.concatenate([iqT_ref[0, 0, h * IDX_DIM:(h + 1) * IDX_DIM, :] for h in range(IDX_HEADS)], axis=1)
        s_all = dot(ik_ref[0, pl.ds(start, tk), :], w_iq)
        sc = _tree_sum([iwT_ref[0, 0, h:h + 1, :] * jnp.maximum(s_all[:, h * tq:(h + 1) * tq], 0.0)
                        for h in range(IDX_HEADS)])
        key_ref[g] = jnp.where(kpos0 + g * tk <= qpos, _sortable_key(sc), jnp.int32(INT32_MIN))
        return carry

    lax.fori_loop(0, n_groups, score_body, 0)

    def count_keys(pred):
        def body(g, part):
            hit = jnp.where(pred(key_ref[g], kpos0 + g * tk), 1.0, 0.0)
            return part + _tree_sum([hit[r * SUBLANES:(r + 1) * SUBLANES] for r in range(tk // SUBLANES)])
        part = lax.fori_loop(0, n_groups, body, jnp.zeros((SUBLANES, tq), jnp.float32))
        return jnp.sum(part, axis=0, keepdims=True)

    def bit_body(i, thr):
        cand = thr ^ lax.shift_left(jnp.int32(1), jnp.int32(31) - i)
        return jnp.where(count_keys(lambda k, kp: k >= cand) >= float(topk), cand, thr)

    thr = lax.fori_loop(0, 32, bit_body, jnp.full((1, tq), INT32_MIN, jnp.int32))

    need = float(topk) - count_keys(lambda k, kp: k > thr)

    def pos_body(i, last):
        cand = last | lax.shift_left(jnp.int32(1), jnp.int32(col_bits - 1) - i)
        return jnp.where(count_keys(lambda k, kp: (k == thr) & (kp < cand)) < need, cand, last)

    n_tied = count_keys(lambda k, kp: k == thr)
    excess = jnp.max(jnp.where(n_tied > need, 1.0, 0.0), axis=1, keepdims=True)
    last_tie = lax.cond(excess[0, 0] > 0.0,
                        lambda: lax.fori_loop(0, col_bits, pos_body, jnp.zeros((1, tq), jnp.int32)),
                        lambda: jnp.full((1, tq), 2 ** col_bits - 1, jnp.int32))

    m_ref[...] = jnp.full(m_ref.shape, MASK_NEG, jnp.float32)
    acc_ref[...] = jnp.zeros(acc_ref.shape, jnp.float32)
    gsz = ATT_HEADS // ATT_KV_HEADS

    def attend_body(g, carry):
        start = pl.multiple_of(g * tk, tk)
        key = key_ref[g]
        kpos = kpos0 + g * tk
        keep = (key > thr) | ((key == thr) & (kpos <= last_tie))
        bias_ref[...] = jnp.where(keep & (kpos <= qpos), 0.0, MASK_NEG)
        for kv in range(ATT_KV_HEADS):
            w_q = jnp.concatenate([qT_ref[0, 0, (kv * gsz + i) * ATT_HEAD_DIM:(kv * gsz + i + 1) * ATT_HEAD_DIM, :]
                                   for i in range(gsz)], axis=1)
            s_ref[kv] = dot(k_ref[kv, pl.ds(start, tk), :], w_q)
        for kv in range(ATT_KV_HEADS):
            s = s_ref[kv] + jnp.concatenate([bias_ref[...]] * gsz, axis=1)
            m_old = m_ref[kv]
            m_new = jnp.maximum(m_old, jnp.max(s, axis=0, keepdims=True))
            p = jnp.exp(s - m_new).astype(vxT_ref.dtype)
            pv = dot(vxT_ref[0, g, kv * LANES:(kv + 1) * LANES, :], p)
            acc_ref[kv] = jnp.exp(m_old - m_new) * acc_ref[kv] + pv
            m_ref[kv] = m_new
        return carry

    lax.fori_loop(0, n_groups, attend_body, 0)

    for h in range(ATT_HEADS):
        a = acc_ref[h // gsz, :, (h % gsz) * tq:(h % gsz + 1) * tq]
        o = (a / a[ATT_HEAD_DIM:ATT_HEAD_DIM + 1, :]).T
        o_ref[:, h * ATT_HEAD_DIM:(h + 1) * ATT_HEAD_DIM] = o[:, :ATT_HEAD_DIM].astype(o_ref.dtype)


def _dsa_attend_q_lanes(b_, t_, qT, iqT, iwT, ikb, khd, vxT):
    ng, tk = vxT.shape[1], vxT.shape[3]
    tq = Q_BLOCK
    nq = t_ // tq
    qtile = lambda n: pl.BlockSpec((1, 1, n, tq), lambda b, j: (b, j, 0, 0))
    return pl.pallas_call(
        functools.partial(_dsa_attend_q_lanes_kernel, topk=min(TOPK_MAX, t_ // 4),
                          col_bits=max(1, (t_ - 1).bit_length())),
        grid=(b_, nq),
        in_specs=[qtile(IDX_HEADS * IDX_DIM), qtile(IDX_HEADS),
                  pl.BlockSpec((1, t_, IDX_DIM), lambda b, j: (b, 0, 0)),
                  qtile(ATT_Q_DIM),
                  pl.BlockSpec((ATT_KV_HEADS, t_, ATT_HEAD_DIM), lambda b, j: (0, b, 0)),
                  pl.BlockSpec((1, ng, ATT_KV_HEADS * LANES, tk), lambda b, j: (b, 0, 0, 0))],
        out_specs=pl.BlockSpec((tq, ATT_Q_DIM), lambda b, j: (b * nq + j, 0)),
        out_shape=jax.ShapeDtypeStruct((b_ * t_, ATT_Q_DIM), MXU_DTYPE),
        scratch_shapes=[pltpu.VMEM((ng, tk, tq), jnp.int32),
                        pltpu.VMEM((tk, tq), jnp.float32),
                        pltpu.VMEM((ATT_KV_HEADS, 1, tq * (ATT_HEADS // ATT_KV_HEADS)), jnp.float32),
                        pltpu.VMEM((ATT_KV_HEADS, LANES, tq * (ATT_HEADS // ATT_KV_HEADS)), jnp.float32),
                        pltpu.VMEM((ATT_KV_HEADS, tk, tq * (ATT_HEADS // ATT_KV_HEADS)), jnp.float32)],
        compiler_params=pltpu.CompilerParams(
            dimension_semantics=("parallel", "arbitrary"),
            vmem_limit_bytes=V7X_VMEM_LIMIT_BYTES),
        name="dsa_attend_q_lanes",
    )(iqT, iwT, ikb.reshape(b_, t_, IDX_DIM), qT, khd, vxT)


def _dsa_prompt_pallas(x3d, w_in, kn_g, kn_b):
    b_, t_, _ = x3d.shape
    qT, iqT, iwT, k, khd, v, vxT, ik, ikb = _dsa_project_q_lanes(x3d, jnp.arange(t_), w_in, kn_g, kn_b)
    o = _dsa_attend_q_lanes(b_, t_, qT, iqT, iwT, ikb, khd, vxT)
    kv4 = lambda u: u.reshape(b_, t_, ATT_KV_HEADS, ATT_HEAD_DIM)
    return o, kv4(k), kv4(v), ik.reshape(b_, t_, IDX_DIM)


RW_ROW_TILE = 256


RW_PAIRS = RW_HEADS // 2
RW_PAIR_LANES = 2 * RW_HEAD


def _rwkv_project_rows(x, xp, mu_ref, wr_ref, wk_ref, wv_ref, w1_ref, w2_ref, a1_ref, a2_ref,
                       g1_ref, g2_ref, w0_ref, a0_ref):
    dx = xp - x
    mix = lambda c: (x + dx * mu_ref[c:c + 1, :]).astype(MXU_DTYPE)
    dot = lambda a, b: jnp.dot(a.astype(MXU_DTYPE), b, preferred_element_type=jnp.float32)
    r = dot(mix(0), wr_ref[...])
    lora_w = dot(jnp.tanh(dot(mix(1), w1_ref[...])), w2_ref[...])
    w_log = -jax.nn.softplus(-(w0_ref[...] + lora_w)) - 0.5
    d = jnp.exp(-jnp.exp(w_log))
    k = dot(mix(2), wk_ref[...])
    v = dot(mix(3), wv_ref[...])
    a = jax.nn.sigmoid(a0_ref[...] + dot(dot(mix(4), a1_ref[...]), a2_ref[...]))
    g = dot(jax.nn.sigmoid(dot(mix(5), g1_ref[...])), g2_ref[...])
    return r, d, k, v, a, g


def _rwkv_proj_step_kernel(x_ref, xp_ref, *refs):
    vals = _rwkv_project_rows(x_ref[...], xp_ref[...], *refs[:12])
    for ref, val in zip(refs[12:], vals):
        ref[...] = val


def _rwkv_proj_seq_kernel(x_ref, halo_ref, shift_ref, *refs):
    i = pl.program_id(1)
    x = x_ref[...]
    prev = jnp.where(i == 0, shift_ref[0], halo_ref[...])[SUBLANES - 1:SUBLANES, :]
    first = lax.broadcasted_iota(jnp.int32, (x.shape[0], 1), 0) == 0
    xp = jnp.where(first, prev, pltpu.roll(x, 1, 0))
    vals = _rwkv_project_rows(x, xp, *refs[:12])
    for ref, val in zip(refs[12:], vals):
        ref[...] = val


def _rwkv_consts(mu, w_r, w_k, w_v, w0, w1, w2, a0, a1, a2, g1, g2):
    cast = lambda w: w.astype(MXU_DTYPE)
    return [mu, cast(w_r), cast(w_k), cast(w_v), cast(w1), cast(w2), cast(a1), cast(a2), cast(g1), cast(g2),
            w0.reshape(1, D_MODEL), a0.reshape(1, D_MODEL)]


def _rwkv_project_step(x2d, xprev2d, *params):
    m = x2d.shape[0]
    consts = _rwkv_consts(*params)
    full = lambda a: pl.BlockSpec(a.shape, lambda i: (0,) * a.ndim)
    rows = pl.BlockSpec((m, D_MODEL), lambda i: (0, 0))
    return pl.pallas_call(
        _rwkv_proj_step_kernel,
        grid=(1,),
        in_specs=[rows, rows] + [full(a) for a in consts],
        out_specs=[rows] * 6,
        out_shape=[jax.ShapeDtypeStruct((m, D_MODEL), jnp.float32)] * 6,
        compiler_params=pltpu.CompilerParams(
            dimension_semantics=("arbitrary",),
            vmem_limit_bytes=V7X_VMEM_LIMIT_BYTES),
        name="rwkv_project_step",
    )(x2d, xprev2d, *consts)


def _rwkv_project_seq(x3d, shift, *params):
    b_, t_, _ = x3d.shape
    m = b_ * t_
    tm = RW_ROW_TILE
    nt = t_ // tm
    consts = _rwkv_consts(*params)
    full = lambda a: pl.BlockSpec(a.shape, lambda b, i: (0,) * a.ndim)
    rows = pl.BlockSpec((tm, D_MODEL), lambda b, i: (b * nt + i, 0))
    halo = pl.BlockSpec((SUBLANES, D_MODEL), lambda b, i: (jnp.maximum((b * nt + i) * (tm // SUBLANES) - 1, 0), 0))
    shift8 = jnp.pad(shift[:, None, :], ((0, 0), (SUBLANES - 1, 0), (0, 0)))
    x2d = x3d.reshape(m, D_MODEL)
    return pl.pallas_call(
        _rwkv_proj_seq_kernel,
        grid=(b_, nt),
        in_specs=[rows, halo, pl.BlockSpec((1, SUBLANES, D_MODEL), lambda b, i: (b, 0, 0))]
        + [full(a) for a in consts],
        out_specs=[rows] * 6,
        out_shape=[jax.ShapeDtypeStruct((m, D_MODEL), jnp.float32)] * 6,
        compiler_params=pltpu.CompilerParams(
            dimension_semantics=("parallel", "parallel"),
            vmem_limit_bytes=V7X_VMEM_LIMIT_BYTES),
        name="rwkv_project_seq",
    )(x2d, x2d, shift8, *consts)


RW_LANES = LANES
RW_TIME_CHUNK = 64


def _rwkv_scan_kernel(r_ref, d_ref, k_ref, v_ref, a_ref, s0_ref, kk_ref, ka_ref, rk_ref, gg_ref, gb_ref,
                      z_ref, s_out_ref, s_ref, vec_ref):
    c = pl.program_id(1)
    n = RW_HEAD
    tc = r_ref.shape[1]
    low_half = lax.broadcasted_iota(jnp.int32, (n, RW_LANES), 1) < n

    @pl.when(c == 0)
    def _():
        s_ref[...] = s0_ref[...]

    def swap_layout(x):
        xt = jnp.concatenate([x, x], axis=0).T
        return jnp.where(low_half, xt[:n], xt[n:])

    def load_step(ref, t):
        rows = ref[:, t, :]
        return swap_layout(jnp.concatenate(
            [rows[:, p * RW_PAIR_LANES:(p + 1) * RW_PAIR_LANES] for p in range(RW_PAIRS)], axis=0))

    def store_step(ref, t, val):
        tile = swap_layout(val)
        ref[:, t, :] = jnp.concatenate(
            [tile[p * RW_SEQ_PER_TILE:(p + 1) * RW_SEQ_PER_TILE] for p in range(RW_PAIRS)], axis=1)

    def prepare(t, slot):
        r, k, a = load_step(r_ref, t), load_step(k_ref, t), load_step(a_ref, t)
        kkr = k * kk_ref[...]
        nrm = jnp.sqrt(jnp.sum(kkr * kkr, axis=0, keepdims=True))
        kk = kkr / jnp.maximum(nrm, 1e-12)
        vec_ref[slot, 0] = kk
        vec_ref[slot, 1] = load_step(d_ref, t)
        vec_ref[slot, 2] = kk * a
        vec_ref[slot, 3] = k * (1.0 + (a - 1.0) * ka_ref[...])
        vec_ref[slot, 4] = r
        vec_ref[slot, 5] = load_step(v_ref, t)

    def step(t, slot):
        row = lambda q, j: vec_ref[slot, q, j:j + 1, :]
        v = vec_ref[slot, 5]
        lanes = 4
        sa_parts = [s_ref[j] * row(0, j) for j in range(lanes)]
        for j in range(lanes, n):
            sa_parts[j % lanes] = sa_parts[j % lanes] + s_ref[j] * row(0, j)
        sa = _tree_sum(sa_parts)
        y_parts = []
        for j in range(n):
            sn = s_ref[j] * row(1, j) - sa * row(2, j) + v * row(3, j)
            s_ref[j] = sn
            if j < lanes:
                y_parts.append(sn * row(4, j))
            else:
                y_parts[j % lanes] = y_parts[j % lanes] + sn * row(4, j)
        y = _tree_sum(y_parts)
        mu = jnp.mean(y, axis=0, keepdims=True)
        yc = y - mu
        var = jnp.mean(yc * yc, axis=0, keepdims=True)
        bonus = jnp.sum(vec_ref[slot, 4] * vec_ref[slot, 3] * rk_ref[...], axis=0, keepdims=True)
        store_step(z_ref, t, yc * lax.rsqrt(var + RW_GN_EPS) * gg_ref[...] + gb_ref[...] + bonus * v)

    prepare(0, 0)
    if tc == 1:
        step(0, 0)
    else:
        def two_steps(i, carry):
            t = 2 * i
            prepare(t + 1, 1)
            step(t, 0)
            prepare(jnp.minimum(t + 2, tc - 1), 0)
            step(t + 1, 1)
            return carry

        lax.fori_loop(0, tc // 2, two_steps, 0)

    @pl.when(c == pl.num_programs(1) - 1)
    def _():
        s_out_ref[...] = s_ref[...]


RW_SEQ_PER_TILE = RW_LANES // RW_HEADS


def _rwkv_lane_heads():
    half = jnp.arange(2)[:, None, None]
    pair = jnp.arange(RW_PAIRS)[None, :, None]
    return jnp.broadcast_to(2 * pair + half, (2, RW_PAIRS, RW_SEQ_PER_TILE)).reshape(RW_LANES)


def _rwkv_scan(r, d, k, v, a, s0, k_k, k_a, r_k, gn_g, gn_b):
    b_, t_, _ = r.shape
    n = RW_HEAD
    tc = RW_TIME_CHUNK if t_ % RW_TIME_CHUNK == 0 else t_
    table = lambda p: p.reshape(RW_HEADS, n)[_rwkv_lane_heads()].T
    seq = pl.BlockSpec((RW_SEQ_PER_TILE, tc, D_MODEL), lambda l, c: (l, c, 0))
    state = pl.BlockSpec((n, n, RW_LANES), lambda l, c: (0, 0, l))
    tab = pl.BlockSpec((n, RW_LANES), lambda l, c: (0, 0))
    return pl.pallas_call(
        _rwkv_scan_kernel,
        grid=(b_ // RW_SEQ_PER_TILE, t_ // tc),
        in_specs=[seq] * 5 + [state] + [tab] * 5,
        out_specs=[seq, state],
        out_shape=[jax.ShapeDtypeStruct(r.shape, jnp.float32),
                   jax.ShapeDtypeStruct(s0.shape, jnp.float32)],
        scratch_shapes=[pltpu.VMEM((n, n, RW_LANES), jnp.float32),
                        pltpu.VMEM((2, 6, n, RW_LANES), jnp.float32)],
        compiler_params=pltpu.CompilerParams(
            dimension_semantics=("parallel", "arbitrary"),
            vmem_limit_bytes=V7X_VMEM_LIMIT_BYTES),
        name="rwkv_scan",
    )(r, d, k, v, a, s0, table(k_k), table(k_a), table(r_k), table(gn_g), table(gn_b))


def _rwkv_state_to_lanes(wkv):
    b_ = wkv.shape[0]
    w = wkv.astype(jnp.float32).reshape(b_ // RW_SEQ_PER_TILE, RW_SEQ_PER_TILE, RW_PAIRS, 2, RW_HEAD, RW_HEAD)
    return jnp.transpose(w, (5, 4, 0, 3, 2, 1)).reshape(RW_HEAD, RW_HEAD, b_ * RW_HEADS)


def _rwkv_state_from_lanes(s, b_):
    w = s.reshape(RW_HEAD, RW_HEAD, b_ // RW_SEQ_PER_TILE, 2, RW_PAIRS, RW_SEQ_PER_TILE)
    return jnp.transpose(w, (2, 5, 4, 3, 1, 0)).reshape(b_, RW_HEADS, RW_HEAD, RW_HEAD)


def _rwkv7_mixer_pallas(x3d, shift, wkv, mu, w_r, w_k, w_v, w0, w1, w2, a0, a1, a2, g1, g2,
                        k_k, k_a, r_k, gn_g, gn_b):
    b_, t_, _ = x3d.shape
    params = (mu, w_r, w_k, w_v, w0, w1, w2, a0, a1, a2, g1, g2)
    if t_ == 1:
        *seqs, g = _rwkv_project_step(x3d.reshape(b_, D_MODEL), shift, *params)
    else:
        *seqs, g = _rwkv_project_seq(x3d, shift, *params)
    seqs = [u.reshape(b_, t_, D_MODEL) for u in seqs]
    z, s = _rwkv_scan(*seqs, _rwkv_state_to_lanes(wkv), k_k, k_a, r_k, gn_g, gn_b)
    return z.reshape(b_ * t_, D_MODEL), g, x3d[:, -1], _rwkv_state_from_lanes(s, b_).astype(wkv.dtype)


def _proj_gate_ln_kernel(x_ref, h_ref, gate_ref, w_ref, g_ref, b_ref, o_ref):
    h = (h_ref[...] * gate_ref[...]).astype(MXU_DTYPE)
    y = ALPHA * x_ref[...] + jnp.dot(h, w_ref[...], preferred_element_type=jnp.float32)
    o_ref[...] = _ln_rows(y, g_ref[...], b_ref[...])


def _proj_gate_post_norm(x2d, h2d, gate2d, w_out, g, b):
    m = x2d.shape[0]
    tm = _row_tile(m)
    rows = pl.BlockSpec((tm, D_MODEL), lambda i: (i, 0))
    vec = pl.BlockSpec((1, D_MODEL), lambda i: (0, 0))
    return pl.pallas_call(
        _proj_gate_ln_kernel,
        grid=(m // tm,),
        in_specs=[rows, rows, rows, pl.BlockSpec((D_MODEL, D_MODEL), lambda i: (0, 0)), vec, vec],
        out_specs=rows,
        out_shape=jax.ShapeDtypeStruct((m, D_MODEL), jnp.float32),
        compiler_params=pltpu.CompilerParams(
            dimension_semantics=("parallel",),
            vmem_limit_bytes=V7X_VMEM_LIMIT_BYTES),
        name="proj_gate_post_norm",
    )(x2d, h2d, gate2d, w_out, g.reshape(1, D_MODEL), b.reshape(1, D_MODEL))


GM_ROW_TILE = 512
GM_ROW_PARTS = 2


def _gmlp_kernel(x_ref, win_ref, lng_ref, lnb_ref, mixw_ref, mixb_ref, wout_ref, g_ref, b_ref, *out_refs,
                 chunk_len, emit_v, parts):
    tp = x_ref.shape[0] // parts
    for part in range(parts):
        rows_p = slice(part * tp, (part + 1) * tp)
        x = x_ref[rows_p, :]
        h = jax.nn.gelu(jnp.dot(x.astype(MXU_DTYPE), win_ref[...], preferred_element_type=jnp.float32))
        u = h[:, :GM_WIDTH]
        v = _ln_rows(h[:, GM_WIDTH:], lng_ref[...], lnb_ref[...])
        if emit_v:
            out_refs[1][rows_p, :] = v
        if chunk_len == 1:
            gated = u * (v * mixw_ref[...] + mixb_ref[...])
        else:
            causal = (lax.broadcasted_iota(jnp.int32, (chunk_len, chunk_len), 0)
                      >= lax.broadcasted_iota(jnp.int32, (chunk_len, chunk_len), 1))
            vb = v.astype(MXU_DTYPE)
            cols = []
            for g in range(GM_GROUPS):
                w = jnp.where(causal, mixw_ref[g], 0.0).astype(MXU_DTYPE)
                bias = mixb_ref[:, g:g + 1]
                lanes = slice(g * GM_GROUP_DIM, (g + 1) * GM_GROUP_DIM)
                rows = [jnp.dot(w, vb[c * chunk_len:(c + 1) * chunk_len, lanes],
                                preferred_element_type=jnp.float32) + bias
                        for c in range(tp // chunk_len)]
                cols.append(jnp.concatenate(rows, axis=0))
            gated = u * jnp.concatenate(cols, axis=1)
        y = ALPHA * x + jnp.dot(gated.astype(MXU_DTYPE), wout_ref[...], preferred_element_type=jnp.float32)
        out_refs[0][rows_p, :] = _ln_rows(y, g_ref[...], b_ref[...])


def _gmlp_block(x2d, seq_len, w_in, ln_g, ln_b, ws, bs, w_out, g, b, emit_v):
    m = x2d.shape[0]
    chunk_len = min(seq_len, CHUNK)
    if chunk_len == 1:
        tm, parts = m, 1
        mixw = jnp.repeat(ws[:, 0, 0], GM_GROUP_DIM)[None, :]
        mixb = jnp.repeat(bs[:, 0], GM_GROUP_DIM)[None, :]
    else:
        tm, parts = GM_ROW_TILE, GM_ROW_PARTS
        mixw = ws[:, :chunk_len, :chunk_len]
        mixb = bs[:, :chunk_len].T
    full = lambda a: pl.BlockSpec(a.shape, lambda i: (0,) * a.ndim, pipeline_mode=pl.Buffered(1))
    rows = lambda n: pl.BlockSpec((tm, n), lambda i: (i, 0))
    consts = [w_in.astype(MXU_DTYPE), ln_g.reshape(1, GM_WIDTH), ln_b.reshape(1, GM_WIDTH), mixw, mixb,
              w_out.astype(MXU_DTYPE), g.reshape(1, D_MODEL), b.reshape(1, D_MODEL)]
    out_specs = [rows(D_MODEL)] + ([rows(GM_WIDTH)] if emit_v else [])
    out_shape = [jax.ShapeDtypeStruct((m, D_MODEL), jnp.float32)] + (
        [jax.ShapeDtypeStruct((m, GM_WIDTH), jnp.float32)] if emit_v else [])
    return pl.pallas_call(
        functools.partial(_gmlp_kernel, chunk_len=chunk_len, emit_v=emit_v, parts=parts),
        grid=(m // tm,),
        in_specs=[rows(D_MODEL)] + [full(a) for a in consts],
        out_specs=out_specs,
        out_shape=out_shape,
        compiler_params=pltpu.CompilerParams(
            dimension_semantics=("parallel",),
            vmem_limit_bytes=V7X_VMEM_LIMIT_BYTES),
        name="gmlp_block",
    )(x2d, *consts)


SSM_ROW_TILE = 256
SSM_ROW_PARTS = 2
SSM_BC_DIM = SSM_GROUPS * SSM_STATE
SSM_DT_LANES = LANES


def _ssm_activate(xb, xbc, taps, wz_ref, wdt_ref, cw_ref, cb_ref, dtb_ref, z_ref, xs_ref, bm_ref, cm_ref, dt_ref):
    conv = cb_ref[...] + xbc * cw_ref[SSM_CONV - 1:SSM_CONV, :]
    for j in range(SSM_CONV - 1):
        conv = conv + taps[j] * cw_ref[j:j + 1, :]
    act = conv * jax.nn.sigmoid(conv)
    xs_ref[...] = act[:, :SSM_D_INNER]
    bm_ref[...] = act[:, SSM_D_INNER:SSM_D_INNER + SSM_BC_DIM].astype(bm_ref.dtype)
    cm_ref[...] = act[:, SSM_D_INNER + SSM_BC_DIM:].astype(cm_ref.dtype)
    z_ref[...] = jnp.dot(xb, wz_ref[...], preferred_element_type=jnp.float32)
    dt_ref[...] = jax.nn.softplus(jnp.dot(xb, wdt_ref[...], preferred_element_type=jnp.float32) + dtb_ref[...])


def _ssm_proj_seq_kernel(x_ref, halo_ref, cs_ref, wx_ref, wz_ref, wdt_ref, cw_ref, cb_ref, dtb_ref,
                         z_ref, xs_ref, bm_ref, cm_ref, dt_ref, tail_ref):
    i = pl.program_id(1)
    tm = x_ref.shape[0]
    tp = tm // SSM_ROW_PARTS
    prev = jnp.dot(halo_ref[...].astype(MXU_DTYPE), wx_ref[...], preferred_element_type=jnp.float32)
    prev = jnp.where(i == 0, cs_ref[0], prev)
    row = lax.broadcasted_iota(jnp.int32, (SUBLANES, 1), 0)
    for part in range(SSM_ROW_PARTS):
        rows_p = slice(part * tp, (part + 1) * tp)
        xb = x_ref[rows_p, :].astype(MXU_DTYPE)
        xbc = jnp.dot(xb, wx_ref[...], preferred_element_type=jnp.float32)
        taps = []
        for j in range(SSM_CONV - 1):
            back = SSM_CONV - 1 - j
            rolled = pltpu.roll(xbc, back, 0)
            top = jnp.where(row < back, pltpu.roll(prev, back, 0), rolled[:SUBLANES])
            taps.append(jnp.concatenate([top, rolled[SUBLANES:]], axis=0))
        _ssm_activate(xb, xbc, taps, wz_ref, wdt_ref, cw_ref, cb_ref, dtb_ref,
                      z_ref.at[rows_p, :], xs_ref.at[rows_p, :], bm_ref.at[rows_p, :], cm_ref.at[rows_p, :],
                      dt_ref.at[rows_p, :])
        prev = xbc[tp - SUBLANES:, :]
    tail_ref[0] = prev


def _ssm_proj_step_kernel(x_ref, st_ref, wx_ref, wz_ref, wdt_ref, cw_ref, cb_ref, dtb_ref,
                          z_ref, xs_ref, bm_ref, cm_ref, dt_ref, st_out_ref):
    xb = x_ref[...].astype(MXU_DTYPE)
    xbc = jnp.dot(xb, wx_ref[...], preferred_element_type=jnp.float32)
    taps = [st_ref[j] for j in range(SSM_CONV - 1)]
    _ssm_activate(xb, xbc, taps, wz_ref, wdt_ref, cw_ref, cb_ref, dtb_ref, z_ref, xs_ref, bm_ref, cm_ref, dt_ref)
    for j in range(SSM_CONV - 2):
        st_out_ref[j] = st_ref[j + 1]
    st_out_ref[SSM_CONV - 2] = xbc


def _ssm_project(x3d, conv_state, w_in, conv_w, conv_b, dt_bias):
    b_, t_, _ = x3d.shape
    m = b_ * t_
    w_z, w_x, w_dt = jnp.split(w_in, [SSM_D_INNER, SSM_D_INNER + SSM_CONV_DIM], axis=1)
    cast = lambda w: w.astype(MXU_DTYPE)
    consts = [cast(w_x), cast(w_z), cast(jnp.pad(w_dt, ((0, 0), (0, SSM_DT_LANES - SSM_HEADS)))),
              conv_w, conv_b.reshape(1, SSM_CONV_DIM),
              jnp.pad(dt_bias, (0, SSM_DT_LANES - SSM_HEADS)).reshape(1, SSM_DT_LANES)]
    sds = jax.ShapeDtypeStruct
    outs = [sds((m, SSM_D_INNER), jnp.float32), sds((m, SSM_D_INNER), jnp.float32),
            sds((m, SSM_BC_DIM), MXU_DTYPE), sds((m, SSM_BC_DIM), MXU_DTYPE), sds((m, SSM_DT_LANES), jnp.float32)]
    widths = [SSM_D_INNER, SSM_D_INNER, SSM_BC_DIM, SSM_BC_DIM, SSM_DT_LANES]
    params = dict(vmem_limit_bytes=V7X_VMEM_LIMIT_BYTES)
    x2d = x3d.reshape(m, D_MODEL)
    if t_ == 1:
        full = lambda a: pl.BlockSpec(a.shape, lambda i: (0,) * a.ndim)
        st = jnp.transpose(conv_state, (1, 0, 2))
        res = pl.pallas_call(
            _ssm_proj_step_kernel,
            grid=(1,),
            in_specs=[full(x2d), full(st)] + [full(a) for a in consts],
            out_specs=[pl.BlockSpec((m, w), lambda i: (0, 0)) for w in widths] + [full(st)],
            out_shape=outs + [sds(st.shape, jnp.float32)],
            compiler_params=pltpu.CompilerParams(dimension_semantics=("arbitrary",), **params),
            name="ssm_project_step",
        )(x2d, st, *consts)
        return list(res[:5]) + [jnp.transpose(res[5], (1, 0, 2))]
    tm = SSM_ROW_TILE
    nt = t_ // tm
    full = lambda a: pl.BlockSpec(a.shape, lambda b, i: (0,) * a.ndim)
    rows = lambda w: pl.BlockSpec((tm, w), lambda b, i: (b * nt + i, 0))
    halo = pl.BlockSpec((SUBLANES, D_MODEL), lambda b, i: (jnp.maximum((b * nt + i) * (tm // SUBLANES) - 1, 0), 0))
    cs8 = jnp.pad(conv_state, ((0, 0), (SUBLANES - (SSM_CONV - 1), 0), (0, 0)))
    tail = pl.BlockSpec((1, SUBLANES, SSM_CONV_DIM), lambda b, i: (b, 0, 0))
    res = pl.pallas_call(
        _ssm_proj_seq_kernel,
        grid=(b_, nt),
        in_specs=[rows(D_MODEL), halo, tail] + [full(a) for a in consts],
        out_specs=[rows(w) for w in widths] + [tail],
        out_shape=outs + [sds((b_, SUBLANES, SSM_CONV_DIM), jnp.float32)],
        compiler_params=pltpu.CompilerParams(dimension_semantics=("parallel", "arbitrary"), **params),
        name="ssm_project_seq",
    )(x2d, x2d, cs8, *consts)
    return list(res[:5]) + [res[5][:, SUBLANES - (SSM_CONV - 1):, :]]


def _ssm_gate_norm(y, xs, z, dskip, normg):
    yg = (y + xs * dskip) * (z * jax.nn.sigmoid(z))
    gw = SSM_D_INNER // SSM_GROUPS
    outs = []
    for g in range(SSM_GROUPS):
        part = yg[:, g * gw:(g + 1) * gw]
        ms = jnp.mean(part * part, axis=-1, keepdims=True)
        outs.append(part * lax.rsqrt(ms + LN_EPS))
    return jnp.concatenate(outs, axis=1) * normg


def _ssm_chunk_kernel(xs_ref, bm_ref, cm_ref, dt_ref, z_ref, aneg_ref, dskip_ref, normg_ref,
                      yg_ref, h_out_ref, h_ref, yT_ref, xe_ref):
    c = pl.program_id(1)
    l = xs_ref.shape[0]
    hd = SSM_HEAD_DIM

    @pl.when(c == 0)
    def _():
        h_ref[...] = jnp.zeros_like(h_ref)

    dot = lambda u, w: jnp.dot(u, w, preferred_element_type=jnp.float32)
    dt = dt_ref[...]
    a = dt * aneg_ref[...]
    r_i = lax.broadcasted_iota(jnp.int32, (l, l), 0)
    c_i = lax.broadcasted_iota(jnp.int32, (l, l), 1)
    tril = jnp.where(r_i >= c_i, 1.0, 0.0)
    hi = lax.Precision.HIGHEST
    acum = jnp.dot(tril, a, precision=hi, preferred_element_type=jnp.float32)
    acum_t = jnp.dot(a.T, tril.T, precision=hi, preferred_element_type=jnp.float32)
    dt_t = dt.T
    to_end_t = jnp.exp(acum_t[:, l - 1:l] - acum_t)
    from_start_t = jnp.exp(acum_t)
    chunk_decay = jnp.exp(acum[l - 1:l, :])
    upper = r_i <= c_i
    xs = xs_ref[...]
    for g in range(SSM_GROUPS):
        bm = bm_ref[:, g * SSM_STATE:(g + 1) * SSM_STATE]
        cm_t = cm_ref[:, g * SSM_STATE:(g + 1) * SSM_STATE].astype(jnp.float32).T.astype(MXU_DTYPE)
        cb_t = dot(bm, cm_t)
        h_in = h_ref[g * SSM_HPG:(g + 1) * SSM_HPG].reshape(SSM_HPG * hd, SSM_STATE)
        y_off = dot(h_in.astype(MXU_DTYPE), cm_t)
        for e in range(SSM_HPG):
            h = g * SSM_HPG + e
            if h % 2 == 0:
                xs_pair_t = xs[:, h * hd:(h + 2) * hd].T
            xdt_t = xs_pair_t[(h % 2) * hd:(h % 2 + 1) * hd] * dt_t[h:h + 1, :]
            seg = jnp.exp(jnp.where(upper, acum_t[h:h + 1, :] - acum[:, h:h + 1], -jnp.inf))
            y_diag = dot(xdt_t.astype(MXU_DTYPE), (cb_t * seg).astype(MXU_DTYPE))
            yT_ref[h * hd:(h + 1) * hd, :] = y_diag + y_off[e * hd:(e + 1) * hd] * from_start_t[h:h + 1, :]
            xe_ref[e * hd:(e + 1) * hd, :] = (xdt_t * to_end_t[h:h + 1, :]).astype(xe_ref.dtype)
        states = dot(xe_ref[...], bm)
        for e in range(SSM_HPG):
            h = g * SSM_HPG + e
            h_ref[h] = h_ref[h] * chunk_decay[:, h:h + 1] + states[e * hd:(e + 1) * hd]
    y = jnp.concatenate([yT_ref[i * l:(i + 1) * l, :].T for i in range(SSM_D_INNER // l)], axis=1)
    yg_ref[...] = _ssm_gate_norm(y, xs, z_ref[...], dskip_ref[...], normg_ref[...]).astype(yg_ref.dtype)

    @pl.when(c == pl.num_programs(1) - 1)
    def _():
        h_out_ref[0] = h_ref[...]


def _ssm_head_lanes(p):
    return jnp.pad(p.astype(jnp.float32), (0, SSM_DT_LANES - SSM_HEADS)).reshape(1, SSM_DT_LANES)


def _ssm_chunk_scan(b_, t_, xs, bm, cm, dt, z, a_log, d_skip, norm_g):
    l = SSM_CHUNK
    nc = t_ // l
    rows = lambda w: pl.BlockSpec((l, w), lambda b, c: (b * nc + c, 0))
    vec = lambda w: pl.BlockSpec((1, w), lambda b, c: (0, 0))
    aneg = _ssm_head_lanes(-jnp.exp(a_log.astype(jnp.float32)))
    dskip = jnp.repeat(d_skip, SSM_HEAD_DIM).reshape(1, SSM_D_INNER)
    yg, h_new = pl.pallas_call(
        _ssm_chunk_kernel,
        grid=(b_, nc),
        in_specs=[rows(SSM_D_INNER), rows(SSM_BC_DIM), rows(SSM_BC_DIM), rows(SSM_DT_LANES), rows(SSM_D_INNER),
                  vec(SSM_DT_LANES), vec(SSM_D_INNER), vec(SSM_D_INNER)],
        out_specs=[rows(SSM_D_INNER),
                   pl.BlockSpec((1, SSM_HEADS, SSM_HEAD_DIM, SSM_STATE), lambda b, c: (b, 0, 0, 0))],
        out_shape=[jax.ShapeDtypeStruct((b_ * t_, SSM_D_INNER), MXU_DTYPE),
                   jax.ShapeDtypeStruct((b_, SSM_HEADS, SSM_HEAD_DIM, SSM_STATE), jnp.float32)],
        scratch_shapes=[pltpu.VMEM((SSM_HEADS, SSM_HEAD_DIM, SSM_STATE), jnp.float32),
                        pltpu.VMEM((SSM_D_INNER, l), jnp.float32),
                        pltpu.VMEM((SSM_HPG * SSM_HEAD_DIM, l), MXU_DTYPE)],
        compiler_params=pltpu.CompilerParams(
            dimension_semantics=("parallel", "arbitrary"),
            vmem_limit_bytes=V7X_VMEM_LIMIT_BYTES),
        name="ssm_chunk_scan",
    )(xs, bm, cm, dt, z, aneg, dskip, norm_g.reshape(1, SSM_D_INNER))
    return yg, h_new


def _ssm_step_kernel(h0_ref, xs_ref, dt_ref, an_ref, bm_ref, cm_ref, y_ref, h_ref):
    h0 = h0_ref[0]
    dt = dt_ref[0]
    decay = jnp.exp(dt * an_ref[...])
    xdt = xs_ref[0] * dt
    bm = bm_ref[0].astype(jnp.float32)
    cm = cm_ref[0].astype(jnp.float32)
    h_ref[0] = h0 * decay + xdt * bm
    cb = jnp.sum(cm * bm, axis=-1, keepdims=True)
    y_ref[0] = cb * xdt + jnp.sum(cm * h0, axis=-1, keepdims=True) * decay


def _ssm_step(state, xs, bm, cm, dt, a_log):
    b_ = state.shape[0]
    per_head = lambda u: jnp.repeat(u.reshape(b_, SSM_GROUPS, 1, SSM_STATE), SSM_HPG, axis=1)
    xs4 = xs.reshape(b_, SSM_HEADS, SSM_HEAD_DIM, 1)
    dt4 = dt[:, :SSM_HEADS].reshape(b_, SSM_HEADS, 1, 1)
    an = (-jnp.exp(a_log.astype(jnp.float32))).reshape(SSM_HEADS, 1, 1)
    blk = lambda a: pl.BlockSpec((1,) + a.shape[1:], lambda b: (b, 0, 0, 0))
    args = [state.astype(jnp.float32), xs4, dt4, an, per_head(bm), per_head(cm)]
    y4, h_new = pl.pallas_call(
        _ssm_step_kernel,
        grid=(b_,),
        in_specs=[blk(args[0]), blk(xs4), blk(dt4), pl.BlockSpec(an.shape, lambda b: (0, 0, 0)),
                  blk(args[4]), blk(args[5])],
        out_specs=[blk(xs4), blk(args[0])],
        out_shape=[jax.ShapeDtypeStruct(xs4.shape, jnp.float32), jax.ShapeDtypeStruct(state.shape, jnp.float32)],
        compiler_params=pltpu.CompilerParams(
            dimension_semantics=("parallel",),
            vmem_limit_bytes=V7X_VMEM_LIMIT_BYTES),
        name="ssm_step",
    )(*args)
    return y4.reshape(b_, SSM_D_INNER), h_new


def _ssm_gate_norm_kernel(y_ref, xs_ref, z_ref, dskip_ref, normg_ref, o_ref):
    o_ref[...] = _ssm_gate_norm(y_ref[...], xs_ref[...], z_ref[...], dskip_ref[...], normg_ref[...]).astype(o_ref.dtype)


def _ssm_gate_norm_rows(y, xs, z, d_skip, norm_g):
    full = lambda a: pl.BlockSpec(a.shape, lambda i: (0,) * a.ndim)
    args = [y, xs, z, jnp.repeat(d_skip, SSM_HEAD_DIM).reshape(1, SSM_D_INNER), norm_g.reshape(1, SSM_D_INNER)]
    return pl.pallas_call(
        _ssm_gate_norm_kernel,
        grid=(1,),
        in_specs=[full(a) for a in args],
        out_specs=full(y),
        out_shape=jax.ShapeDtypeStruct(y.shape, MXU_DTYPE),
        name="ssm_gate_norm",
    )(*args)


def _mamba2_mixer_pallas(x3d, conv_state, ssm_state, w_in, conv_w, conv_b, dt_bias, a_log, d_skip, norm_g):
    b_, t_, _ = x3d.shape
    z, xs, bm, cm, dt, conv_new = _ssm_project(x3d, conv_state, w_in, conv_w, conv_b, dt_bias)
    if t_ == 1:
        y, h_new = _ssm_step(ssm_state, xs, bm, cm, dt, a_log)
        yg = _ssm_gate_norm_rows(y, xs, z, d_skip, norm_g)
    else:
        yg, h_new = _ssm_chunk_scan(b_, t_, xs, bm, cm, dt, z, a_log, d_skip, norm_g)
    return yg, conv_new, h_new.astype(ssm_state.dtype)


PAGES_PER_STEP = 16


def _sortable_key(score):
    bits = pltpu.bitcast(score, jnp.int32)
    return jnp.where(bits >= 0, bits, bits ^ jnp.int32(0x7FFFFFFF))


def _decode_score_kernel(pt_ref, iq_ref, iw_ref, ikn_ref, *rest):
    idx_refs, (key_ref, knew_ref) = rest[:-2], rest[-2:]
    iq, iw = iq_ref[0], iw_ref[0]
    weigh = lambda sc: jnp.sum(iw * jnp.maximum(sc, 0.0), axis=0, keepdims=True)
    ik_t = jnp.concatenate([r[0] for r in idx_refs], axis=1).astype(MXU_DTYPE)
    key_ref[0] = _sortable_key(weigh(jnp.dot(iq, ik_t, preferred_element_type=jnp.float32)))

    @pl.when(pl.program_id(1) == 0)
    def _():
        sc_new = jnp.sum(iq.astype(jnp.float32) * ikn_ref[0].astype(jnp.float32), axis=1, keepdims=True)
        knew_ref[0] = jnp.broadcast_to(_sortable_key(weigh(sc_new)), knew_ref.shape[1:])


def _decode_select_kernel(keys_ref, knew_ref, thr_ref, last_ref, *, topk, col_bits):
    keys = keys_ref[...]
    key_new = knew_ref[:, 0:1]
    past = keys.shape[1]
    col = lax.broadcasted_iota(jnp.int32, keys.shape, 1)

    def count(pred_past, pred_new):
        hit = jnp.where(pred_past(keys, col), 1.0, 0.0)
        cnt = _tree_sum([hit[:, l * LANES:(l + 1) * LANES] for l in range(past // LANES)])
        return jnp.sum(cnt, axis=1, keepdims=True) + jnp.where(pred_new(key_new), 1.0, 0.0)

    def at_least(cand):
        return count(lambda k, c: k >= cand, lambda k: k >= cand) >= float(topk)

    def two_bits(i, thr):
        hi = lax.shift_left(jnp.int32(1), jnp.int32(31) - 2 * i)
        lo = lax.shift_left(jnp.int32(1), jnp.int32(30) - 2 * i)
        c1, c2, c3 = thr ^ lo, thr ^ hi, thr ^ hi ^ lo
        return jnp.where(at_least(c3), c3, jnp.where(at_least(c2), c2, jnp.where(at_least(c1), c1, thr)))

    thr = lax.fori_loop(0, 16, two_bits, jnp.full(key_new.shape, INT32_MIN, jnp.int32))
    need = float(topk) - count(lambda k, c: k > thr, lambda k: k > thr)

    def col_body(i, last):
        cand = last | lax.shift_left(jnp.int32(1), jnp.int32(col_bits - 1) - i)
        ties = count(lambda k, c: (k == thr) & (c < cand), lambda k: (k == thr) & (jnp.int32(past) < cand))
        return jnp.where(ties < need, cand, last)

    n_tied = count(lambda k, c: k == thr, lambda k: k == thr)
    excess = jnp.max(jnp.where(n_tied > need, 1.0, 0.0), axis=0, keepdims=True)
    last_tie = lax.cond(excess[0, 0] > 0.0,
                        lambda: lax.fori_loop(0, col_bits, col_body, jnp.zeros(key_new.shape, jnp.int32)),
                        lambda: jnp.full(key_new.shape, 2 ** col_bits - 1, jnp.int32))
    thr_ref[...] = jnp.broadcast_to(thr, thr_ref.shape)
    last_ref[...] = jnp.broadcast_to(last_tie, last_ref.shape)


def _decode_attend_kernel(pt_ref, q_ref, kn_ref, vn_ref, key_ref, knew_ref, thr_ref, last_ref, *rest,
                          n_steps, pages):
    k_refs, v_refs = rest[:pages], rest[pages:2 * pages]
    o_ref, m_ref, l_ref, acc_ref = rest[2 * pages:]
    s = pl.program_id(1)
    nk = key_ref.shape[2]
    nt = (((1,), (1,)), ((), ()))
    thr, last_tie = thr_ref[0, :, 0:1], last_ref[0, :, 0:1]
    keep_mask = lambda key, col: (key > thr) | ((key == thr) & (col <= last_tie))
    gsz = ATT_HEADS // ATT_KV_HEADS
    q = q_ref[0]
    q_wide = jnp.concatenate([q] * ATT_KV_HEADS, axis=1)
    head_i = lax.broadcasted_iota(jnp.int32, q_wide.shape, 0)
    col_i = lax.broadcasted_iota(jnp.int32, q_wide.shape, 1)
    own_group = (col_i // ATT_HEAD_DIM) == (head_i // gsz)
    q_blk = jnp.where(own_group, q_wide, jnp.zeros_like(q_wide))

    def online_update(logits, weighted_values):
        m_old = m_ref[...]
        m_new = jnp.maximum(m_old, jnp.max(logits, axis=1, keepdims=True))
        p = jnp.exp(logits - m_new)
        alpha = jnp.exp(m_old - m_new)
        l_ref[...] = alpha * l_ref[...] + jnp.sum(p, axis=1, keepdims=True)
        acc_ref[...] = alpha * acc_ref[...] + weighted_values(p.astype(MXU_DTYPE))
        m_ref[...] = m_new

    @pl.when(s == 0)
    def _init():
        m_ref[...] = jnp.full(m_ref.shape, MASK_NEG, jnp.float32)
        l_ref[...] = jnp.zeros(l_ref.shape, jnp.float32)
        acc_ref[...] = jnp.zeros(acc_ref.shape, jnp.float32)

    col = lax.broadcasted_iota(jnp.int32, (1, nk), 1) + s * nk
    bias = jnp.where(keep_mask(key_ref[0], col), 0.0, MASK_NEG)
    k_t = jnp.concatenate([r[0] for r in k_refs], axis=1).astype(MXU_DTYPE)
    v_t = jnp.concatenate([r[0] for r in v_refs], axis=1).astype(MXU_DTYPE)
    online_update(jnp.dot(q_blk, k_t, preferred_element_type=jnp.float32) + bias,
                  lambda p: lax.dot_general(p, v_t, nt, preferred_element_type=jnp.float32))

    @pl.when(s == n_steps - 1)
    def _finish():
        keep_new = keep_mask(knew_ref[0, :, 0:1], jnp.int32(n_steps * nk))
        logit = jnp.sum(q_blk.astype(jnp.float32) * kn_ref[0].astype(jnp.float32), axis=1, keepdims=True)
        v_row = vn_ref[0].astype(jnp.float32)
        online_update(logit + jnp.where(keep_new, 0.0, MASK_NEG), lambda p: p.astype(jnp.float32) * v_row)
        out = jnp.where(own_group, acc_ref[...] / l_ref[...], 0.0)
        o = out[:, 0:ATT_HEAD_DIM]
        for g in range(1, ATT_KV_HEADS):
            o = o + out[:, g * ATT_HEAD_DIM:(g + 1) * ATT_HEAD_DIM]
        o_ref[0] = o.astype(o_ref.dtype)


def _dsa_decode_split(q, iq, iw, ik_new, k_new, v_new, cache_k, cache_v, cache_idx_k, page_table):
    b_, n_pages = page_table.shape
    n_pool, page = cache_k.shape[0], cache_k.shape[1]
    pages = PAGES_PER_STEP
    n_steps = n_pages // pages
    nk = pages * page
    past = n_pages * page
    ck = jnp.transpose(cache_k, (0, 2, 3, 1)).reshape(n_pool, ATT_KV_DIM, page)
    cv = jnp.transpose(cache_v, (0, 2, 3, 1)).reshape(n_pool, ATT_KV_DIM, page)
    cik = jnp.swapaxes(cache_idx_k, 1, 2)
    per_seq = lambda a: pl.BlockSpec((1,) + a.shape[1:], lambda b, s, pt: (b,) + (0,) * (a.ndim - 1))
    paged = lambda width, j: pl.BlockSpec((1, width, page), lambda b, s, pt: (pt[b, s * pages + j], 0, 0))
    key_blk = pl.BlockSpec((1, 1, nk), lambda b, s, pt: (b, 0, s))
    lane_blk = pl.BlockSpec((1, 1, LANES), lambda b, s, pt: (b, 0, 0))
    params = pltpu.CompilerParams(dimension_semantics=("parallel", "arbitrary"),
                                  vmem_limit_bytes=V7X_VMEM_LIMIT_BYTES)

    score_in = [iq.reshape(b_, IDX_HEADS, IDX_DIM), iw[:, :IDX_HEADS].reshape(b_, IDX_HEADS, 1),
                ik_new.astype(MXU_DTYPE).reshape(b_, 1, IDX_DIM)]
    keys, key_new = pl.pallas_call(
        _decode_score_kernel,
        grid_spec=pltpu.PrefetchScalarGridSpec(
            num_scalar_prefetch=1, grid=(b_, n_steps),
            in_specs=[per_seq(a) for a in score_in] + [paged(IDX_DIM, j) for j in range(pages)],
            out_specs=[key_blk, lane_blk]),
        out_shape=[jax.ShapeDtypeStruct((b_, 1, past), jnp.int32), jax.ShapeDtypeStruct((b_, 1, LANES), jnp.int32)],
        compiler_params=params,
        name="dsa_decode_score",
    )(page_table, *score_in, *([cik] * pages))

    whole = lambda shape: pl.BlockSpec(shape, lambda i: (0,) * len(shape))
    thr, last_tie = pl.pallas_call(
        functools.partial(_decode_select_kernel, topk=min(TOPK_MAX, (past + 1) // 4),
                          col_bits=max(1, past.bit_length())),
        grid=(1,),
        in_specs=[whole((b_, past)), whole((b_, LANES))],
        out_specs=[whole((b_, LANES)), whole((b_, LANES))],
        out_shape=[jax.ShapeDtypeStruct((b_, LANES), jnp.int32)] * 2,
        compiler_params=pltpu.CompilerParams(vmem_limit_bytes=V7X_VMEM_LIMIT_BYTES),
        name="dsa_decode_select",
    )(keys.reshape(b_, past), key_new.reshape(b_, LANES))

    attend_in = [q.reshape(b_, ATT_HEADS, ATT_HEAD_DIM), k_new.astype(MXU_DTYPE).reshape(b_, 1, ATT_KV_DIM),
                 v_new.astype(MXU_DTYPE).reshape(b_, 1, ATT_KV_DIM)]
    o = pl.pallas_call(
        functools.partial(_decode_attend_kernel, n_steps=n_steps, pages=pages),
        grid_spec=pltpu.PrefetchScalarGridSpec(
            num_scalar_prefetch=1, grid=(b_, n_steps),
            in_specs=[per_seq(a) for a in attend_in] + [key_blk, lane_blk, lane_blk, lane_blk]
            + [paged(ATT_KV_DIM, j) for j in range(pages)] * 2,
            out_specs=pl.BlockSpec((1, ATT_HEADS, ATT_HEAD_DIM), lambda b, s, pt: (b, 0, 0)),
            scratch_shapes=[pltpu.VMEM((ATT_HEADS, 1), jnp.float32), pltpu.VMEM((ATT_HEADS, 1), jnp.float32),
                            pltpu.VMEM((ATT_HEADS, ATT_KV_DIM), jnp.float32)]),
        out_shape=jax.ShapeDtypeStruct((b_, ATT_HEADS, ATT_HEAD_DIM), MXU_DTYPE),
        compiler_params=params,
        name="dsa_decode_attend",
    )(page_table, *attend_in, keys, key_new, thr.reshape(b_, 1, LANES), last_tie.reshape(b_, 1, LANES),
      *([ck] * pages), *([cv] * pages))
    return o.reshape(b_, ATT_Q_DIM)


def _dsa_sample_pallas(x3d, cache_k, cache_v, cache_idx_k, page_table, w_in, kn_g, kn_b):
    b_, t_, _ = x3d.shape
    past = page_table.shape[1] * cache_k.shape[1]
    pos = jnp.full((b_,), past, jnp.int32)
    q, iq, v, _, iw, kT, _, ikT, _ = _dsa_project(x3d.reshape(1, b_, D_MODEL), pos, w_in, kn_g, kn_b)
    k = _untranspose_groups(kT)[0]
    ik = _untranspose_groups(ikT)[0]
    o = _dsa_decode_split(q, iq, iw, ik, k, v, cache_k, cache_v, cache_idx_k, page_table)
    kv4 = lambda u: u.reshape(b_, t_, ATT_KV_HEADS, ATT_HEAD_DIM)
    return o, kv4(k), kv4(v), ik.reshape(b_, t_, IDX_DIM)


def kernel(x_prompt, x_sample, state_ssm_conv, state_ssm, cache_k, cache_v, cache_idx_k, state_rwkv_shift, state_rwkv_wkv, page_table, p_prompt, p_sample, ln_g, ln_b, ffn_w_up, ffn_w_down, ple_w_p, ple_w_g, ple_b_g, gm_w_in, gm_ln_g, gm_ln_b, gm_ws, gm_bs, gm_w_out, ssm_w_in, ssm_conv_w, ssm_conv_b, ssm_dt_bias, ssm_a_log, ssm_d, ssm_norm_g, ssm_w_out, att_w_in, att_kn_g, att_kn_b, att_w_out, rw_mu, rw_w_r, rw_w_k, rw_w_v, rw_w_o, rw_w0, rw_w1, rw_w2, rw_a0, rw_a1, rw_a2, rw_g1, rw_g2, rw_k_k, rw_k_a, rw_r_k, rw_gn_g, rw_gn_b):
    bp, tp, _ = x_prompt.shape
    bs_, ts, _ = x_sample.shape
    bf = lambda w: w.astype(jnp.bfloat16)
    w_up_bf, w_down_bf = bf(ffn_w_up), bf(ffn_w_down)
    ple_wp_bf, ple_wg_bf = bf(ple_w_p), bf(ple_w_g)
    pp3 = p_prompt.reshape(DEPTH, bp * tp, PLE_DIM)
    ps3 = p_sample.reshape(DEPTH, bs_ * ts, PLE_DIM)

    yp = x_prompt.reshape(bp * tp, D_MODEL)
    ys = x_sample.reshape(bs_ * ts, D_MODEL)
    r3p = lambda t: t.reshape(bp, tp, -1)
    r3s = lambda t: t.reshape(bs_, ts, -1)
    f2 = lambda t: t.reshape(-1, t.shape[-1])

    for i in range(DEPTH):
        yp = _ffn_sub(yp, w_up_bf, w_down_bf, i, 0, ln_g[i, 0], ln_b[i, 0])
        ys = _ffn_sub(ys, w_up_bf, w_down_bf, i, 0, ln_g[i, 0], ln_b[i, 0])
        m = i % N_MIXERS
        if m == 0:
            gm_args = (gm_w_in, gm_ln_g, gm_ln_b, gm_ws, gm_bs, gm_w_out, ln_g[i, 1], ln_b[i, 1])
            yp, = _gmlp_block(yp, tp, *gm_args, False)
            ys, gm_v_s = _gmlp_block(ys, ts, *gm_args, True)
            gm_v_s = r3s(gm_v_s)
        elif m == 1:
            ssm_args = (ssm_w_in, ssm_conv_w, ssm_conv_b, ssm_dt_bias, ssm_a_log, ssm_d, ssm_norm_g)
            hp, conv_p, ssm_p = _mamba2_mixer_pallas(
                r3p(yp), jnp.zeros((bp, SSM_CONV - 1, SSM_CONV_DIM), yp.dtype),
                jnp.zeros((bp, SSM_HEADS, SSM_HEAD_DIM, SSM_STATE), yp.dtype), *ssm_args)
            hs, conv_s, ssm_s = _mamba2_mixer_pallas(r3s(ys), state_ssm_conv, state_ssm, *ssm_args)
            w_out = bf(ssm_w_out)
        elif m == 2:
            hp, k_p, v_p, ik_p = _dsa_prompt_pallas(r3p(yp), att_w_in, att_kn_g, att_kn_b)
            hs, k_s, v_s, ik_s = _dsa_sample_pallas(r3s(ys), cache_k, cache_v, cache_idx_k, page_table,
                                                    att_w_in, att_kn_g, att_kn_b)
            w_out = bf(att_w_out)
        else:
            rw_args = (rw_mu, rw_w_r, rw_w_k, rw_w_v, rw_w0, rw_w1, rw_w2, rw_a0, rw_a1, rw_a2,
                       rw_g1, rw_g2, rw_k_k, rw_k_a, rw_r_k, rw_gn_g, rw_gn_b)
            hp, gate_p, sh_p, wkv_p = _rwkv7_mixer_pallas(
                r3p(yp), jnp.zeros((bp, D_MODEL), yp.dtype),
                jnp.zeros((bp, RW_HEADS, RW_HEAD, RW_HEAD), yp.dtype), *rw_args)
            hs, gate_s, sh_s, wkv_s = _rwkv7_mixer_pallas(r3s(ys), state_rwkv_shift, state_rwkv_wkv, *rw_args)
            w_out = bf(rw_w_o)
        if m == 3:
            yp = _proj_gate_post_norm(yp, hp, gate_p, w_out, ln_g[i, 1], ln_b[i, 1])
            ys = _proj_gate_post_norm(ys, hs, gate_s, w_out, ln_g[i, 1], ln_b[i, 1])
        elif m != 0:
            yp = _proj_post_norm(yp, f2(hp), w_out, ln_g[i, 1], ln_b[i, 1])
            ys = _proj_post_norm(ys, f2(hs), w_out, ln_g[i, 1], ln_b[i, 1])
        yp = _ffn_sub(yp, w_up_bf, w_down_bf, i, 1, ln_g[i, 2], ln_b[i, 2], (pp3, ple_wp_bf, ple_wg_bf, ple_b_g))
        ys = _ffn_sub(ys, w_up_bf, w_down_bf, i, 1, ln_g[i, 2], ln_b[i, 2], (ps3, ple_wp_bf, ple_wg_bf, ple_b_g))

    return (r3p(yp), r3s(ys), gm_v_s, conv_p, ssm_p, conv_s, ssm_s, k_p, v_p, ik_p, k_s, v_s, ik_s,
            sh_p, wkv_p, sh_s, wkv_s)
```

```python
import functools

import jax
import jax.numpy as jnp
from jax import lax
from jax.experimental import pallas as pl
from jax.experimental.pallas import tpu as pltpu

D_MODEL = 1024
DEPTH = 4
N_MIXERS = 4
PLE_DIM = 256
D_FF = 2816
ALPHA = (2 * DEPTH) ** 0.25
LN_EPS = 1e-5

CHUNK = 128
GM_WIDTH = 2 * D_MODEL
GM_GROUPS = 8
GM_GROUP_DIM = GM_WIDTH // GM_GROUPS

SSM_D_INNER = 2 * D_MODEL
SSM_HEAD_DIM = 64
SSM_HEADS = SSM_D_INNER // SSM_HEAD_DIM
SSM_GROUPS = 4
SSM_HPG = SSM_HEADS // SSM_GROUPS
SSM_STATE = 128
SSM_CONV = 4
SSM_CONV_DIM = SSM_D_INNER + 2 * SSM_GROUPS * SSM_STATE
SSM_CHUNK = 128

ATT_HEADS = 16
ATT_KV_HEADS = 4
ATT_HEAD_DIM = D_MODEL // ATT_HEADS
ROPE_DIM = ATT_HEAD_DIM // 4
ROPE_THETA = 500000.0
IDX_HEADS = 8
IDX_DIM = 64
IDX_ROPE_DIM = IDX_DIM // 4
TOPK_MAX = 256
Q_BLOCK = 128
ATT_Q_DIM = ATT_HEADS * ATT_HEAD_DIM
ATT_KV_DIM = ATT_KV_HEADS * ATT_HEAD_DIM
ATT_IN_SPLITS = (ATT_Q_DIM, ATT_Q_DIM + ATT_KV_DIM, ATT_Q_DIM + 2 * ATT_KV_DIM,
                 ATT_Q_DIM + 2 * ATT_KV_DIM + IDX_HEADS * IDX_DIM,
                 ATT_Q_DIM + 2 * ATT_KV_DIM + IDX_HEADS * IDX_DIM + IDX_DIM)

RW_HEAD = 64
RW_HEADS = D_MODEL // RW_HEAD
RW_GN_EPS = 64e-5

V7X_VMEM_LIMIT_BYTES = 52 * 1024 * 1024
LANES = 128
SUBLANES = 8
FF_TILE = D_FF // 2
ROW_TILE = 512


def _row_tile(m):
    return ROW_TILE if m % ROW_TILE == 0 else m


def _ln_rows(y, g, b):
    mu = jnp.mean(y, axis=-1, keepdims=True)
    yc = y - mu
    var = jnp.mean(yc * yc, axis=-1, keepdims=True)
    return yc * lax.rsqrt(var + LN_EPS) * g + b


FFN_ROW_TILE = 1024
FFN_ROW_PARTS = 4
FFN_PLE_ROW_PARTS = 2


def _ffn_kernel(x_ref, wu_ref, wd_ref, g_ref, b_ref, *rest, parts, with_ple):
    o_ref = rest[-1]
    rows = x_ref.shape[0] // parts
    for p in range(parts):
        x = x_ref[p * rows:(p + 1) * rows, :]
        xb = x.astype(MXU_DTYPE)
        acc = None
        for f in range(D_FF // FF_TILE):
            cols = slice(f * FF_TILE, (f + 1) * FF_TILE)
            gate = jnp.dot(xb, wu_ref[:, cols], preferred_element_type=jnp.float32)
            lin = jnp.dot(xb, wu_ref[:, D_FF + f * FF_TILE:D_FF + (f + 1) * FF_TILE],
                          preferred_element_type=jnp.float32)
            h = (gate * jax.nn.sigmoid(gate) * lin).astype(MXU_DTYPE)
            part = jnp.dot(h, wd_ref[cols, :], preferred_element_type=jnp.float32)
            acc = part if acc is None else acc + part
        y = _ln_rows(ALPHA * x + 0.5 * acc, g_ref[...], b_ref[...])
        if with_ple:
            p_ref, wp_ref, wg_ref, bg_ref = rest[:4]
            gate = jax.nn.sigmoid(
                jnp.dot(y.astype(MXU_DTYPE), wg_ref[...], preferred_element_type=jnp.float32) + bg_ref[...])
            emb = jnp.dot(p_ref[p * rows:(p + 1) * rows, :].astype(MXU_DTYPE), wp_ref[...],
                          preferred_element_type=jnp.float32)
            y = y + gate * emb
        o_ref[p * rows:(p + 1) * rows, :] = y


def _ffn_sub(x2d, w_up, w_down, layer, half, g, b, ple=None):
    m = x2d.shape[0]
    tm = FFN_ROW_TILE if m % FFN_ROW_TILE == 0 else m
    parts = 1 if tm != FFN_ROW_TILE else (FFN_ROW_PARTS if ple is None else FFN_PLE_ROW_PARTS)
    resident = dict(pipeline_mode=pl.Buffered(1))
    vec = pl.BlockSpec((1, D_MODEL), lambda i: (0, 0))
    in_specs = [
        pl.BlockSpec((tm, D_MODEL), lambda i: (i, 0)),
        pl.BlockSpec((None, None, D_MODEL, 2 * D_FF), lambda i: (layer, half, 0, 0), **resident),
        pl.BlockSpec((None, None, D_FF, D_MODEL), lambda i: (layer, half, 0, 0), **resident),
        vec, vec]
    args = [x2d, w_up, w_down, g.reshape(1, D_MODEL), b.reshape(1, D_MODEL)]
    if ple is not None:
        p3d, w_p, w_g, b_g = ple
        in_specs += [pl.BlockSpec((None, tm, PLE_DIM), lambda i: (layer, i, 0)),
                     pl.BlockSpec((None, PLE_DIM, D_MODEL), lambda i: (layer, 0, 0), **resident),
                     pl.BlockSpec((None, D_MODEL, D_MODEL), lambda i: (layer, 0, 0), **resident),
                     pl.BlockSpec((None, 1, D_MODEL), lambda i: (layer, 0, 0))]
        args += [p3d, w_p, w_g, b_g.reshape(DEPTH, 1, D_MODEL)]
    return pl.pallas_call(
        functools.partial(_ffn_kernel, parts=parts, with_ple=ple is not None),
        grid=(m // tm,),
        in_specs=in_specs,
        out_specs=pl.BlockSpec((tm, D_MODEL), lambda i: (i, 0)),
        out_shape=jax.ShapeDtypeStruct((m, D_MODEL), jnp.float32),
        compiler_params=pltpu.CompilerParams(
            dimension_semantics=("parallel",),
            vmem_limit_bytes=V7X_VMEM_LIMIT_BYTES),
        name="ffn_ple" if ple is not None else "ffn_sub",
    )(*args)


def _proj_ln_kernel(x_ref, h_ref, w_ref, g_ref, b_ref, o_ref):
    y = ALPHA * x_ref[...] + jnp.dot(h_ref[...].astype(jnp.bfloat16), w_ref[...],
                                     preferred_element_type=jnp.float32)
    o_ref[...] = _ln_rows(y, g_ref[...], b_ref[...])


def _proj_post_norm(x2d, h2d, w_out, g, b):
    m = x2d.shape[0]
    k = h2d.shape[1]
    tm = _row_tile(m)
    return pl.pallas_call(
        _proj_ln_kernel,
        grid=(m // tm,),
        in_specs=[
            pl.BlockSpec((tm, D_MODEL), lambda i: (i, 0)),
            pl.BlockSpec((tm, k), lambda i: (i, 0)),
            pl.BlockSpec((k, D_MODEL), lambda i: (0, 0)),
            pl.BlockSpec((1, D_MODEL), lambda i: (0, 0)),
            pl.BlockSpec((1, D_MODEL), lambda i: (0, 0)),
        ],
        out_specs=pl.BlockSpec((tm, D_MODEL), lambda i: (i, 0)),
        out_shape=jax.ShapeDtypeStruct((m, D_MODEL), jnp.float32),
        compiler_params=pltpu.CompilerParams(
            dimension_semantics=("parallel",),
            vmem_limit_bytes=V7X_VMEM_LIMIT_BYTES),
        name="proj_post_norm",
    )(x2d, h2d, w_out, g.reshape(1, D_MODEL), b.reshape(1, D_MODEL))


MXU_DTYPE = jnp.bfloat16
KEY_GROUP = 512
INT32_MIN = -2 ** 31
MASK_NEG = -1e30


def _rope_lane_tables(pos, rot_dim, head_dim):
    half = rot_dim // 2
    inv = ROPE_THETA ** (-jnp.arange(half, dtype=jnp.float32) / half)
    ang = pos.astype(jnp.float32)[:, None] * inv[None, :]
    cos, sin = jnp.cos(ang), jnp.sin(ang)
    n = pos.shape[0]
    rest = head_dim - rot_dim
    c = jnp.concatenate([cos, cos, jnp.ones((n, rest), jnp.float32)], axis=1)
    s1 = jnp.concatenate([-sin, jnp.zeros((n, half + rest), jnp.float32)], axis=1)
    s2 = jnp.concatenate([jnp.zeros((n, half), jnp.float32), sin, jnp.zeros((n, rest), jnp.float32)], axis=1)
    reps = LANES // head_dim
    tile = lambda t: jnp.tile(t, (1, reps))
    return tile(c), tile(s1), tile(s2), cos.T, sin.T


def _rope_lanes(t, c, s1, s2, half):
    n = t.shape[1]
    reps = n // LANES
    tl = lambda a: jnp.concatenate([a] * reps, axis=1)
    return t * tl(c) + pltpu.roll(t, n - half, 1) * tl(s1) + pltpu.roll(t, half, 1) * tl(s2)


def _rope_rows(t, cT, sT, head_dim, half):
    pieces = []
    for h in range(t.shape[0] // head_dim):
        x1 = t[h * head_dim:h * head_dim + half]
        x2 = t[h * head_dim + half:h * head_dim + 2 * half]
        pieces += [x1 * cT - x2 * sT, x2 * cT + x1 * sT, t[h * head_dim + 2 * half:(h + 1) * head_dim]]
    return jnp.concatenate(pieces, axis=0)


def _dsa_proj_kernel(x_ref, wq_ref, wiq_ref, wv_ref, wvx_ref, wiw_ref, wkT_ref, wikT_ref,
                     c_ref, s1_ref, s2_ref, cT_ref, sT_ref, kng_ref, knb_ref, one_ref,
                     q_ref, iq_ref, v_ref, vx_ref, iw_ref, kT_ref, kTb_ref, ikT_ref, ikTb_ref):
    xb = x_ref[...].astype(MXU_DTYPE)
    c, s1, s2 = c_ref[...], s1_ref[...], s2_ref[...]
    cT, sT = cT_ref[...], sT_ref[...]
    dot = lambda a, b: jnp.dot(a, b, preferred_element_type=jnp.float32)
    dot_t = lambda w, a: lax.dot_general(w, a, (((1,), (1,)), ((), ())), preferred_element_type=jnp.float32)

    q = _rope_lanes(dot(xb, wq_ref[...]), c, s1, s2, ROPE_DIM // 2)
    q_ref[...] = (q * (ATT_HEAD_DIM ** -0.5)).astype(q_ref.dtype)
    iq = _rope_lanes(dot(xb, wiq_ref[...]), c, s1, s2, IDX_ROPE_DIM // 2)
    iq_ref[...] = iq.astype(iq_ref.dtype)
    v_ref[...] = dot(xb, wv_ref[...])
    vx_ref[...] = (dot(xb, wvx_ref[...]) + one_ref[...]).astype(vx_ref.dtype)
    iw_ref[...] = dot(xb, wiw_ref[...]) * (IDX_HEADS ** -0.5 * IDX_DIM ** -0.5)

    kT = _rope_rows(dot_t(wkT_ref[...], xb), cT, sT, ATT_HEAD_DIM, ROPE_DIM // 2)
    kT_ref[0, 0] = kT
    kTb_ref[0, 0] = kT.astype(kTb_ref.dtype)
    ikT = dot_t(wikT_ref[...], xb)
    mu = jnp.mean(ikT, axis=0, keepdims=True)
    ikc = ikT - mu
    var = jnp.mean(ikc * ikc, axis=0, keepdims=True)
    ikT = ikc * lax.rsqrt(var + LN_EPS) * kng_ref[...] + knb_ref[...]
    ikT = _rope_rows(ikT, cT, sT, IDX_DIM, IDX_ROPE_DIM // 2)
    ikT_ref[0, 0] = ikT
    ikTb_ref[0, 0] = ikT.astype(ikTb_ref.dtype)


def _dsa_project(x3d, pos, w_in, kn_g, kn_b):
    b_, t_, _ = x3d.shape
    tk = KEY_GROUP if t_ % KEY_GROUP == 0 else t_
    ng = t_ // tk
    m = b_ * t_
    w_q, w_k, w_v, w_iq, w_ik, w_iw = jnp.split(w_in, list(ATT_IN_SPLITS), axis=1)
    cast = lambda w: w.astype(MXU_DTYPE)
    w_vx = jnp.pad(w_v.reshape(D_MODEL, ATT_KV_HEADS, ATT_HEAD_DIM),
                   ((0, 0), (0, 0), (0, LANES - ATT_HEAD_DIM))).reshape(D_MODEL, ATT_KV_HEADS * LANES)
    one_col = jnp.tile((jnp.arange(LANES) == ATT_HEAD_DIM).astype(jnp.float32), ATT_KV_HEADS)[None, :]
    w_iw_pad = jnp.pad(w_iw, ((0, 0), (0, LANES - IDX_HEADS)))
    c, s1, s2, cT, sT = _rope_lane_tables(pos, ROPE_DIM, ATT_HEAD_DIM)
    full = lambda shape: pl.BlockSpec(shape, lambda b, i: (0,) * len(shape))
    rows = lambda n: pl.BlockSpec((tk, n), lambda b, i: (b * ng + i, 0))
    ptab = lambda n: pl.BlockSpec((tk, n), lambda b, i: (i, 0))
    grp = lambda n: pl.BlockSpec((1, 1, n, tk), lambda b, i: (b, i, 0, 0))
    sds = jax.ShapeDtypeStruct
    return pl.pallas_call(
        _dsa_proj_kernel,
        grid=(b_, ng),
        in_specs=[rows(D_MODEL), full((D_MODEL, ATT_Q_DIM)), full((D_MODEL, IDX_HEADS * IDX_DIM)),
                  full((D_MODEL, ATT_KV_DIM)), full((D_MODEL, ATT_KV_HEADS * LANES)), full((D_MODEL, LANES)),
                  full((ATT_KV_DIM, D_MODEL)), full((IDX_DIM, D_MODEL)),
                  ptab(LANES), ptab(LANES), ptab(LANES),
                  pl.BlockSpec((ROPE_DIM // 2, tk), lambda b, i: (0, i)),
                  pl.BlockSpec((ROPE_DIM // 2, tk), lambda b, i: (0, i)),
                  full((IDX_DIM, 1)), full((IDX_DIM, 1)), full((1, ATT_KV_HEADS * LANES))],
        out_specs=[rows(ATT_Q_DIM), rows(IDX_HEADS * IDX_DIM), rows(ATT_KV_DIM), rows(ATT_KV_HEADS * LANES),
                   rows(LANES), grp(ATT_KV_DIM), grp(ATT_KV_DIM), grp(IDX_DIM), grp(IDX_DIM)],
        out_shape=[sds((m, ATT_Q_DIM), MXU_DTYPE), sds((m, IDX_HEADS * IDX_DIM), MXU_DTYPE),
                   sds((m, ATT_KV_DIM), jnp.float32), sds((m, ATT_KV_HEADS * LANES), MXU_DTYPE),
                   sds((m, LANES), jnp.float32),
                   sds((b_, ng, ATT_KV_DIM, tk), jnp.float32), sds((b_, ng, ATT_KV_DIM, tk), MXU_DTYPE),
                   sds((b_, ng, IDX_DIM, tk), jnp.float32), sds((b_, ng, IDX_DIM, tk), MXU_DTYPE)],
        compiler_params=pltpu.CompilerParams(
            dimension_semantics=("parallel", "parallel"),
            vmem_limit_bytes=V7X_VMEM_LIMIT_BYTES),
        name="dsa_project",
    )(x3d.reshape(m, D_MODEL), cast(w_q), cast(w_iq), cast(w_v), cast(w_vx), cast(w_iw_pad),
      cast(w_k.T), cast(w_ik.T), c, s1, s2, cT, sT, kn_g.reshape(IDX_DIM, 1), kn_b.reshape(IDX_DIM, 1), one_col)


def _untranspose_groups(tg):
    b_, g_, r_, tk = tg.shape
    return jnp.transpose(tg, (0, 1, 3, 2)).reshape(b_, g_ * tk, r_)


def _dsa_proj_q_lanes_kernel(x_ref, wqT_ref, wiqT_ref, wiwT_ref, wk_ref, wv_ref, wvxT_ref, wik_ref,
                             c_ref, s1_ref, s2_ref, cT_ref, sT_ref, kng_ref, knb_ref, onerow_ref,
                             qT_ref, iqT_ref, iwT_ref, k_ref, khd_ref, v_ref, vxT_ref, ik_ref, ikb_ref):
    xb = x_ref[...].astype(MXU_DTYPE)
    tm = xb.shape[0]
    c, s1, s2 = c_ref[...], s1_ref[...], s2_ref[...]
    cT, sT = cT_ref[...], sT_ref[...]
    dot = lambda a, b: jnp.dot(a, b, preferred_element_type=jnp.float32)
    dot_t = lambda w, a: lax.dot_general(w, a, (((1,), (1,)), ((), ())), preferred_element_type=jnp.float32)

    qT = _rope_rows(dot_t(wqT_ref[...], xb), cT, sT, ATT_HEAD_DIM, ROPE_DIM // 2) * (ATT_HEAD_DIM ** -0.5)
    iqT = _rope_rows(dot_t(wiqT_ref[...], xb), cT, sT, IDX_DIM, IDX_ROPE_DIM // 2)
    iwT = dot_t(wiwT_ref[...], xb) * (IDX_HEADS ** -0.5 * IDX_DIM ** -0.5)
    for t in range(tm // Q_BLOCK):
        lanes = slice(t * Q_BLOCK, (t + 1) * Q_BLOCK)
        qT_ref[0, t] = qT[:, lanes].astype(qT_ref.dtype)
        iqT_ref[0, t] = iqT[:, lanes].astype(iqT_ref.dtype)
        iwT_ref[0, t] = iwT[:, lanes]

    k = _rope_lanes(dot(xb, wk_ref[...]), c, s1, s2, ROPE_DIM // 2)
    k_ref[...] = k
    for g in range(ATT_KV_HEADS):
        khd_ref[g] = k[:, g * ATT_HEAD_DIM:(g + 1) * ATT_HEAD_DIM].astype(khd_ref.dtype)
    v_ref[...] = dot(xb, wv_ref[...])
    vxT_ref[0, 0] = (dot_t(wvxT_ref[...], xb) + onerow_ref[...]).astype(vxT_ref.dtype)

    ik = dot(xb, wik_ref[...])
    real = lax.broadcasted_iota(jnp.int32, ik.shape, 1) < IDX_DIM
    mu = jnp.sum(ik, axis=-1, keepdims=True) * (1.0 / IDX_DIM)
    ikc = jnp.where(real, ik - mu, 0.0)
    var = jnp.sum(ikc * ikc, axis=-1, keepdims=True) * (1.0 / IDX_DIM)
    ikn = _rope_lanes(ikc * lax.rsqrt(var + LN_EPS) * kng_ref[...] + knb_ref[...], c, s1, s2, IDX_ROPE_DIM // 2)
    ik_ref[...] = ikn[:, :IDX_DIM]
    ikb_ref[...] = ikn[:, :IDX_DIM].astype(ikb_ref.dtype)


def _dsa_project_q_lanes(x3d, pos, w_in, kn_g, kn_b):
    b_, t_, _ = x3d.shape
    tk = KEY_GROUP
    ng = t_ // tk
    nq = tk // Q_BLOCK
    m = b_ * t_
    w_q, w_k, w_v, w_iq, w_ik, w_iw = jnp.split(w_in, list(ATT_IN_SPLITS), axis=1)
    cast = lambda w: w.astype(MXU_DTYPE)
    w_vxT = jnp.pad(w_v.T.reshape(ATT_KV_HEADS, ATT_HEAD_DIM, D_MODEL),
                    ((0, 0), (0, LANES - ATT_HEAD_DIM), (0, 0))).reshape(ATT_KV_HEADS * LANES, D_MODEL)
    one_row = jnp.tile((jnp.arange(LANES) == ATT_HEAD_DIM).astype(jnp.float32), ATT_KV_HEADS)[:, None]
    pad_lanes = lambda a: jnp.pad(a, ((0, 0), (0, LANES - a.shape[1])))
    c, s1, s2, cT, sT = _rope_lane_tables(pos, ROPE_DIM, ATT_HEAD_DIM)
    full = lambda shape: pl.BlockSpec(shape, lambda b, i: (0,) * len(shape))
    rows = lambda n: pl.BlockSpec((tk, n), lambda b, i: (b * ng + i, 0))
    ptab = lambda n: pl.BlockSpec((tk, n), lambda b, i: (i, 0))
    qtile = lambda n: pl.BlockSpec((1, nq, n, Q_BLOCK), lambda b, i: (b, i, 0, 0))
    sds = jax.ShapeDtypeStruct
    return pl.pallas_call(
        _dsa_proj_q_lanes_kernel,
        grid=(b_, ng),
        in_specs=[rows(D_MODEL), full((ATT_Q_DIM, D_MODEL)), full((IDX_HEADS * IDX_DIM, D_MODEL)),
                  full((IDX_HEADS, D_MODEL)), full((D_MODEL, ATT_KV_DIM)), full((D_MODEL, ATT_KV_DIM)),
                  full((ATT_KV_HEADS * LANES, D_MODEL)), full((D_MODEL, LANES)),
                  ptab(LANES), ptab(LANES), ptab(LANES),
                  pl.BlockSpec((ROPE_DIM // 2, tk), lambda b, i: (0, i)),
                  pl.BlockSpec((ROPE_DIM // 2, tk), lambda b, i: (0, i)),
                  full((1, LANES)), full((1, LANES)), full((ATT_KV_HEADS * LANES, 1))],
        out_specs=[qtile(ATT_Q_DIM), qtile(IDX_HEADS * IDX_DIM), qtile(IDX_HEADS),
                   rows(ATT_KV_DIM), pl.BlockSpec((ATT_KV_HEADS, tk, ATT_HEAD_DIM), lambda b, i: (0, b * ng + i, 0)),
                   rows(ATT_KV_DIM), pl.BlockSpec((1, 1, ATT_KV_HEADS * LANES, tk), lambda b, i: (b, i, 0, 0)),
                   rows(IDX_DIM), rows(IDX_DIM)],
        out_shape=[sds((b_, t_ // Q_BLOCK, ATT_Q_DIM, Q_BLOCK), MXU_DTYPE),
                   sds((b_, t_ // Q_BLOCK, IDX_HEADS * IDX_DIM, Q_BLOCK), MXU_DTYPE),
                   sds((b_, t_ // Q_BLOCK, IDX_HEADS, Q_BLOCK), jnp.float32),
                   sds((m, ATT_KV_DIM), jnp.float32), sds((ATT_KV_HEADS, m, ATT_HEAD_DIM), MXU_DTYPE),
                   sds((m, ATT_KV_DIM), jnp.float32), sds((b_, ng, ATT_KV_HEADS * LANES, tk), MXU_DTYPE),
                   sds((m, IDX_DIM), jnp.float32), sds((m, IDX_DIM), MXU_DTYPE)],
        compiler_params=pltpu.CompilerParams(
            dimension_semantics=("parallel", "parallel"),
            vmem_limit_bytes=V7X_VMEM_LIMIT_BYTES),
        name="dsa_project_q_lanes",
    )(x3d.reshape(m, D_MODEL), cast(w_q.T), cast(w_iq.T), cast(w_iw.T), cast(w_k), cast(w_v), cast(w_vxT),
      cast(pad_lanes(w_ik)), c, s1, s2, cT, sT, pad_lanes(kn_g.reshape(1, IDX_DIM)),
      pad_lanes(kn_b.reshape(1, IDX_DIM)), one_row)


def _tree_sum(parts):
    while len(parts) > 1:
        parts = [parts[i] + parts[i + 1] for i in range(0, len(parts) - 1, 2)] + (
            [parts[-1]] if len(parts) % 2 else [])
    return parts[0]


def _dsa_attend_q_lanes_kernel(iqT_ref, iwT_ref, ik_ref, qT_ref, k_ref, vxT_ref, o_ref,
                               key_ref, bias_ref, m_ref, acc_ref, s_ref, *, topk, col_bits):
    j = pl.program_id(1)
    tk, tq = key_ref.shape[1], key_ref.shape[2]
    n_groups = (j * tq + tq + tk - 1) // tk
    qpos = j * tq + lax.broadcasted_iota(jnp.int32, (tk, tq), 1)
    kpos0 = lax.broadcasted_iota(jnp.int32, (tk, tq), 0)
    dot = lambda a, b: jnp.dot(a, b, preferred_element_type=jnp.float32)

    def score_body(g, carry):
        start = pl.multiple_of(g * tk, tk)
        w_iq = jnp.concatenate([iqT_ref[0, 0, h * IDX_DIM:(h + 1) * IDX_DIM, :] for h in range(IDX_HEADS)], axis=1)
        s_all = dot(ik_ref[0, pl.ds(start, tk), :], w_iq)
        sc = _tree_sum([iwT_ref[0, 0, h:h + 1, :] * jnp.maximum(s_all[:, h * tq:(h + 1) * tq], 0.0)
                        for h in range(IDX_HEADS)])
        key_ref[g] = jnp.where(kpos0 + g * tk <= qpos, _sortable_key(sc), jnp.int32(INT32_MIN))
        return carry

    lax.fori_loop(0, n_groups, score_body, 0)

    def count_keys(pred):
        def body(g, part):
            hit = jnp.where(pred(key_ref[g], kpos0 + g * tk), 1.0, 0.0)
            return part + _tree_sum([hit[r * SUBLANES:(r + 1) * SUBLANES] for r in range(tk // SUBLANES)])
        part = lax.fori_loop(0, n_groups, body, jnp.zeros((SUBLANES, tq), jnp.float32))
        return jnp.sum(part, axis=0, keepdims=True)

    def bit_body(i, thr):
        cand = thr ^ lax.shift_left(jnp.int32(1), jnp.int32(31) - i)
        return jnp.where(count_keys(lambda k, kp: k >= cand) >= float(topk), cand, thr)

    thr = lax.fori_loop(0, 32, bit_body, jnp.full((1, tq), INT32_MIN, jnp.int32))

    need = float(topk) - count_keys(lambda k, kp: k > thr)

    def pos_body(i, last):
        cand = last | lax.shift_left(jnp.int32(1), jnp.int32(col_bits - 1) - i)
        return jnp.where(count_keys(lambda k, kp: (k == thr) & (kp < cand)) < need, cand, last)

    n_tied = count_keys(lambda k, kp: k == thr)
    excess = jnp.max(jnp.where(n_tied > need, 1.0, 0.0), axis=1, keepdims=True)
    last_tie = lax.cond(excess[0, 0] > 0.0,
                        lambda: lax.fori_loop(0, col_bits, pos_body, jnp.zeros((1, tq), jnp.int32)),
                        lambda: jnp.full((1, tq), 2 ** col_bits - 1, jnp.int32))

    m_ref[...] = jnp.full(m_ref.shape, MASK_NEG, jnp.float32)
    acc_ref[...] = jnp.zeros(acc_ref.shape, jnp.float32)
    gsz = ATT_HEADS // ATT_KV_HEADS

    def attend_body(g, carry):
        start = pl.multiple_of(g * tk, tk)
        key = key_ref[g]
        kpos = kpos0 + g * tk
        keep = (key > thr) | ((key == thr) & (kpos <= last_tie))
        bias_ref[...] = jnp.where(keep & (kpos <= qpos), 0.0, MASK_NEG)
        for kv in range(ATT_KV_HEADS):
            w_q = jnp.concatenate([qT_ref[0, 0, (kv * gsz + i) * ATT_HEAD_DIM:(kv * gsz + i + 1) * ATT_HEAD_DIM, :]
                                   for i in range(gsz)], axis=1)
            s_ref[kv] = dot(k_ref[kv, pl.ds(start, tk), :], w_q)
        for kv in range(ATT_KV_HEADS):
            s = s_ref[kv] + jnp.concatenate([bias_ref[...]] * gsz, axis=1)
            m_old = m_ref[kv]
            m_new = jnp.maximum(m_old, jnp.max(s, axis=0, keepdims=True))
            p = jnp.exp(s - m_new).astype(vxT_ref.dtype)
            pv = dot(vxT_ref[0, g, kv * LANES:(kv + 1) * LANES, :], p)
            acc_ref[kv] = jnp.exp(m_old - m_new) * acc_ref[kv] + pv
            m_ref[kv] = m_new
        return carry

    lax.fori_loop(0, n_groups, attend_body, 0)

    for h in range(ATT_HEADS):
        a = acc_ref[h // gsz, :, (h % gsz) * tq:(h % gsz + 1) * tq]
        o = (a / a[ATT_HEAD_DIM:ATT_HEAD_DIM + 1, :]).T
        o_ref[:, h * ATT_HEAD_DIM:(h + 1) * ATT_HEAD_DIM] = o[:, :ATT_HEAD_DIM].astype(o_ref.dtype)


def _dsa_attend_q_lanes(b_, t_, qT, iqT, iwT, ikb, khd, vxT):
    ng, tk = vxT.shape[1], vxT.shape[3]
    tq = Q_BLOCK
    nq = t_ // tq
    qtile = lambda n: pl.BlockSpec((1, 1, n, tq), lambda b, j: (b, j, 0, 0))
    return pl.pallas_call(
        functools.partial(_dsa_attend_q_lanes_kernel, topk=min(TOPK_MAX, t_ // 4),
                          col_bits=max(1, (t_ - 1).bit_length())),
        grid=(b_, nq),
        in_specs=[qtile(IDX_HEADS * IDX_DIM), qtile(IDX_HEADS),
                  pl.BlockSpec((1, t_, IDX_DIM), lambda b, j: (b, 0, 0)),
                  qtile(ATT_Q_DIM),
                  pl.BlockSpec((ATT_KV_HEADS, t_, ATT_HEAD_DIM), lambda b, j: (0, b, 0)),
                  pl.BlockSpec((1, ng, ATT_KV_HEADS * LANES, tk), lambda b, j: (b, 0, 0, 0))],
        out_specs=pl.BlockSpec((tq, ATT_Q_DIM), lambda b, j: (b * nq + j, 0)),
        out_shape=jax.ShapeDtypeStruct((b_ * t_, ATT_Q_DIM), MXU_DTYPE),
        scratch_shapes=[pltpu.VMEM((ng, tk, tq), jnp.int32),
                        pltpu.VMEM((tk, tq), jnp.float32),
                        pltpu.VMEM((ATT_KV_HEADS, 1, tq * (ATT_HEADS // ATT_KV_HEADS)), jnp.float32),
                        pltpu.VMEM((ATT_KV_HEADS, LANES, tq * (ATT_HEADS // ATT_KV_HEADS)), jnp.float32),
                        pltpu.VMEM((ATT_KV_HEADS, tk, tq * (ATT_HEADS // ATT_KV_HEADS)), jnp.float32)],
        compiler_params=pltpu.CompilerParams(
            dimension_semantics=("parallel", "arbitrary"),
            vmem_limit_bytes=V7X_VMEM_LIMIT_BYTES),
        name="dsa_attend_q_lanes",
    )(iqT, iwT, ikb.reshape(b_, t_, IDX_DIM), qT, khd, vxT)


def _dsa_prompt_pallas(x3d, w_in, kn_g, kn_b):
    b_, t_, _ = x3d.shape
    qT, iqT, iwT, k, khd, v, vxT, ik, ikb = _dsa_project_q_lanes(x3d, jnp.arange(t_), w_in, kn_g, kn_b)
    o = _dsa_attend_q_lanes(b_, t_, qT, iqT, iwT, ikb, khd, vxT)
    kv4 = lambda u: u.reshape(b_, t_, ATT_KV_HEADS, ATT_HEAD_DIM)
    return o, kv4(k), kv4(v), ik.reshape(b_, t_, IDX_DIM)


RW_ROW_TILE = 256


RW_PAIRS = RW_HEADS // 2
RW_PAIR_LANES = 2 * RW_HEAD


def _rwkv_project_rows(x, xp, mu_ref, wr_ref, wk_ref, wv_ref, w1_ref, w2_ref, a1_ref, a2_ref,
                       g1_ref, g2_ref, w0_ref, a0_ref):
    dx = xp - x
    mix = lambda c: (x + dx * mu_ref[c:c + 1, :]).astype(MXU_DTYPE)
    dot = lambda a, b: jnp.dot(a.astype(MXU_DTYPE), b, preferred_element_type=jnp.float32)
    r = dot(mix(0), wr_ref[...])
    lora_w = dot(jnp.tanh(dot(mix(1), w1_ref[...])), w2_ref[...])
    w_log = -jax.nn.softplus(-(w0_ref[...] + lora_w)) - 0.5
    d = jnp.exp(-jnp.exp(w_log))
    k = dot(mix(2), wk_ref[...])
    v = dot(mix(3), wv_ref[...])
    a = jax.nn.sigmoid(a0_ref[...] + dot(dot(mix(4), a1_ref[...]), a2_ref[...]))
    g = dot(jax.nn.sigmoid(dot(mix(5), g1_ref[...])), g2_ref[...])
    return r, d, k, v, a, g


def _rwkv_proj_step_kernel(x_ref, xp_ref, *refs):
    vals = _rwkv_project_rows(x_ref[...], xp_ref[...], *refs[:12])
    for ref, val in zip(refs[12:], vals):
        ref[...] = val


def _rwkv_proj_seq_kernel(x_ref, halo_ref, shift_ref, *refs):
    i = pl.program_id(1)
    x = x_ref[...]
    prev = jnp.where(i == 0, shift_ref[0], halo_ref[...])[SUBLANES - 1:SUBLANES, :]
    first = lax.broadcasted_iota(jnp.int32, (x.shape[0], 1), 0) == 0
    xp = jnp.where(first, prev, pltpu.roll(x, 1, 0))
    vals = _rwkv_project_rows(x, xp, *refs[:12])
    for ref, val in zip(refs[12:], vals):
        ref[...] = val


def _rwkv_consts(mu, w_r, w_k, w_v, w0, w1, w2, a0, a1, a2, g1, g2):
    cast = lambda w: w.astype(MXU_DTYPE)
    return [mu, cast(w_r), cast(w_k), cast(w_v), cast(w1), cast(w2), cast(a1), cast(a2), cast(g1), cast(g2),
            w0.reshape(1, D_MODEL), a0.reshape(1, D_MODEL)]


def _rwkv_project_step(x2d, xprev2d, *params):
    m = x2d.shape[0]
    consts = _rwkv_consts(*params)
    full = lambda a: pl.BlockSpec(a.shape, lambda i: (0,) * a.ndim)
    rows = pl.BlockSpec((m, D_MODEL), lambda i: (0, 0))
    return pl.pallas_call(
        _rwkv_proj_step_kernel,
        grid=(1,),
        in_specs=[rows, rows] + [full(a) for a in consts],
        out_specs=[rows] * 6,
        out_shape=[jax.ShapeDtypeStruct((m, D_MODEL), jnp.float32)] * 6,
        compiler_params=pltpu.CompilerParams(
            dimension_semantics=("arbitrary",),
            vmem_limit_bytes=V7X_VMEM_LIMIT_BYTES),
        name="rwkv_project_step",
    )(x2d, xprev2d, *consts)


def _rwkv_project_seq(x3d, shift, *params):
    b_, t_, _ = x3d.shape
    m = b_ * t_
    tm = RW_ROW_TILE
    nt = t_ // tm
    consts = _rwkv_consts(*params)
    full = lambda a: pl.BlockSpec(a.shape, lambda b, i: (0,) * a.ndim)
    rows = pl.BlockSpec((tm, D_MODEL), lambda b, i: (b * nt + i, 0))
    halo = pl.BlockSpec((SUBLANES, D_MODEL), lambda b, i: (jnp.maximum((b * nt + i) * (tm // SUBLANES) - 1, 0), 0))
    shift8 = jnp.pad(shift[:, None, :], ((0, 0), (SUBLANES - 1, 0), (0, 0)))
    x2d = x3d.reshape(m, D_MODEL)
    return pl.pallas_call(
        _rwkv_proj_seq_kernel,
        grid=(b_, nt),
        in_specs=[rows, halo, pl.BlockSpec((1, SUBLANES, D_MODEL), lambda b, i: (b, 0, 0))]
        + [full(a) for a in consts],
        out_specs=[rows] * 6,
        out_shape=[jax.ShapeDtypeStruct((m, D_MODEL), jnp.float32)] * 6,
        compiler_params=pltpu.CompilerParams(
            dimension_semantics=("parallel", "parallel"),
            vmem_limit_bytes=V7X_VMEM_LIMIT_BYTES),
        name="rwkv_project_seq",
    )(x2d, x2d, shift8, *consts)


RW_LANES = LANES
RW_TIME_CHUNK = 64


def _rwkv_scan_kernel(r_ref, d_ref, k_ref, v_ref, a_ref, s0_ref, kk_ref, ka_ref, rk_ref, gg_ref, gb_ref,
                      z_ref, s_out_ref, s_ref, vec_ref):
    c = pl.program_id(1)
    n = RW_HEAD
    tc = r_ref.shape[1]
    low_half = lax.broadcasted_iota(jnp.int32, (n, RW_LANES), 1) < n

    @pl.when(c == 0)
    def _():
        s_ref[...] = s0_ref[...]

    def swap_layout(x):
        xt = jnp.concatenate([x, x], axis=0).T
        return jnp.where(low_half, xt[:n], xt[n:])

    def load_step(ref, t):
        rows = ref[:, t, :]
        return swap_layout(jnp.concatenate(
            [rows[:, p * RW_PAIR_LANES:(p + 1) * RW_PAIR_LANES] for p in range(RW_PAIRS)], axis=0))

    def store_step(ref, t, val):
        tile = swap_layout(val)
        ref[:, t, :] = jnp.concatenate(
            [tile[p * RW_SEQ_PER_TILE:(p + 1) * RW_SEQ_PER_TILE] for p in range(RW_PAIRS)], axis=1)

    def prepare(t, slot):
        r, k, a = load_step(r_ref, t), load_step(k_ref, t), load_step(a_ref, t)
        kkr = k * kk_ref[...]
        nrm = jnp.sqrt(jnp.sum(kkr * kkr, axis=0, keepdims=True))
        kk = kkr / jnp.maximum(nrm, 1e-12)
        vec_ref[slot, 0] = kk
        vec_ref[slot, 1] = load_step(d_ref, t)
        vec_ref[slot, 2] = kk * a
        vec_ref[slot, 3] = k * (1.0 + (a - 1.0) * ka_ref[...])
        vec_ref[slot, 4] = r
        vec_ref[slot, 5] = load_step(v_ref, t)

    def step(t, slot):
        row = lambda q, j: vec_ref[slot, q, j:j + 1, :]
        v = vec_ref[slot, 5]
        lanes = 4
        sa_parts = [s_ref[j] * row(0, j) for j in range(lanes)]
        for j in range(lanes, n):
            sa_parts[j % lanes] = sa_parts[j % lanes] + s_ref[j] * row(0, j)
        sa = _tree_sum(sa_parts)
        y_parts = []
        for j in range(n):
            sn = s_ref[j] * row(1, j) - sa * row(2, j) + v * row(3, j)
            s_ref[j] = sn
            if j < lanes:
                y_parts.append(sn * row(4, j))
            else:
                y_parts[j % lanes] = y_parts[j % lanes] + sn * row(4, j)
        y = _tree_sum(y_parts)
        mu = jnp.mean(y, axis=0, keepdims=True)
        yc = y - mu
        var = jnp.mean(yc * yc, axis=0, keepdims=True)
        bonus = jnp.sum(vec_ref[slot, 4] * vec_ref[slot, 3] * rk_ref[...], axis=0, keepdims=True)
        store_step(z_ref, t, yc * lax.rsqrt(var + RW_GN_EPS) * gg_ref[...] + gb_ref[...] + bonus * v)

    prepare(0, 0)
    if tc == 1:
        step(0, 0)
    else:
        def two_steps(i, carry):
            t = 2 * i
            prepare(t + 1, 1)
            step(t, 0)
            prepare(jnp.minimum(t + 2, tc - 1), 0)
            step(t + 1, 1)
            return carry

        lax.fori_loop(0, tc // 2, two_steps, 0)

    @pl.when(c == pl.num_programs(1) - 1)
    def _():
        s_out_ref[...] = s_ref[...]


RW_SEQ_PER_TILE = RW_LANES // RW_HEADS


def _rwkv_lane_heads():
    half = jnp.arange(2)[:, None, None]
    pair = jnp.arange(RW_PAIRS)[None, :, None]
    return jnp.broadcast_to(2 * pair + half, (2, RW_PAIRS, RW_SEQ_PER_TILE)).reshape(RW_LANES)


def _rwkv_scan(r, d, k, v, a, s0, k_k, k_a, r_k, gn_g, gn_b):
    b_, t_, _ = r.shape
    n = RW_HEAD
    tc = RW_TIME_CHUNK if t_ % RW_TIME_CHUNK == 0 else t_
    table = lambda p: p.reshape(RW_HEADS, n)[_rwkv_lane_heads()].T
    seq = pl.BlockSpec((RW_SEQ_PER_TILE, tc, D_MODEL), lambda l, c: (l, c, 0))
    state = pl.BlockSpec((n, n, RW_LANES), lambda l, c: (0, 0, l))
    tab = pl.BlockSpec((n, RW_LANES), lambda l, c: (0, 0))
    return pl.pallas_call(
        _rwkv_scan_kernel,
        grid=(b_ // RW_SEQ_PER_TILE, t_ // tc),
        in_specs=[seq] * 5 + [state] + [tab] * 5,
        out_specs=[seq, state],
        out_shape=[jax.ShapeDtypeStruct(r.shape, jnp.float32),
                   jax.ShapeDtypeStruct(s0.shape, jnp.float32)],
        scratch_shapes=[pltpu.VMEM((n, n, RW_LANES), jnp.float32),
                        pltpu.VMEM((2, 6, n, RW_LANES), jnp.float32)],
        compiler_params=pltpu.CompilerParams(
            dimension_semantics=("parallel", "arbitrary"),
            vmem_limit_bytes=V7X_VMEM_LIMIT_BYTES),
        name="rwkv_scan",
    )(r, d, k, v, a, s0, table(k_k), table(k_a), table(r_k), table(gn_g), table(gn_b))


def _rwkv_state_to_lanes(wkv):
    b_ = wkv.shape[0]
    w = wkv.astype(jnp.float32).reshape(b_ // RW_SEQ_PER_TILE, RW_SEQ_PER_TILE, RW_PAIRS, 2, RW_HEAD, RW_HEAD)
    return jnp.transpose(w, (5, 4, 0, 3, 2, 1)).reshape(RW_HEAD, RW_HEAD, b_ * RW_HEADS)


def _rwkv_state_from_lanes(s, b_):
    w = s.reshape(RW_HEAD, RW_HEAD, b_ // RW_SEQ_PER_TILE, 2, RW_PAIRS, RW_SEQ_PER_TILE)
    return jnp.transpose(w, (2, 5, 4, 3, 1, 0)).reshape(b_, RW_HEADS, RW_HEAD, RW_HEAD)


def _rwkv7_mixer_pallas(x3d, shift, wkv, mu, w_r, w_k, w_v, w0, w1, w2, a0, a1, a2, g1, g2,
                        k_k, k_a, r_k, gn_g, gn_b):
    b_, t_, _ = x3d.shape
    params = (mu, w_r, w_k, w_v, w0, w1, w2, a0, a1, a2, g1, g2)
    if t_ == 1:
        *seqs, g = _rwkv_project_step(x3d.reshape(b_, D_MODEL), shift, *params)
    else:
        *seqs, g = _rwkv_project_seq(x3d, shift, *params)
    seqs = [u.reshape(b_, t_, D_MODEL) for u in seqs]
    z, s = _rwkv_scan(*seqs, _rwkv_state_to_lanes(wkv), k_k, k_a, r_k, gn_g, gn_b)
    return z.reshape(b_ * t_, D_MODEL), g, x3d[:, -1], _rwkv_state_from_lanes(s, b_).astype(wkv.dtype)


def _proj_gate_ln_kernel(x_ref, h_ref, gate_ref, w_ref, g_ref, b_ref, o_ref):
    h = (h_ref[...] * gate_ref[...]).astype(MXU_DTYPE)
    y = ALPHA * x_ref[...] + jnp.dot(h, w_ref[...], preferred_element_type=jnp.float32)
    o_ref[...] = _ln_rows(y, g_ref[...], b_ref[...])


def _proj_gate_post_norm(x2d, h2d, gate2d, w_out, g, b):
    m = x2d.shape[0]
    tm = _row_tile(m)
    rows = pl.BlockSpec((tm, D_MODEL), lambda i: (i, 0))
    vec = pl.BlockSpec((1, D_MODEL), lambda i: (0, 0))
    return pl.pallas_call(
        _proj_gate_ln_kernel,
        grid=(m // tm,),
        in_specs=[rows, rows, rows, pl.BlockSpec((D_MODEL, D_MODEL), lambda i: (0, 0)), vec, vec],
        out_specs=rows,
        out_shape=jax.ShapeDtypeStruct((m, D_MODEL), jnp.float32),
        compiler_params=pltpu.CompilerParams(
            dimension_semantics=("parallel",),
            vmem_limit_bytes=V7X_VMEM_LIMIT_BYTES),
        name="proj_gate_post_norm",
    )(x2d, h2d, gate2d, w_out, g.reshape(1, D_MODEL), b.reshape(1, D_MODEL))


GM_ROW_TILE = 512
GM_ROW_PARTS = 2


def _gmlp_kernel(x_ref, win_ref, lng_ref, lnb_ref, mixw_ref, mixb_ref, wout_ref, g_ref, b_ref, *out_refs,
                 chunk_len, emit_v, parts):
    tp = x_ref.shape[0] // parts
    for part in range(parts):
        rows_p = slice(part * tp, (part + 1) * tp)
        x = x_ref[rows_p, :]
        h = jax.nn.gelu(jnp.dot(x.astype(MXU_DTYPE), win_ref[...], preferred_element_type=jnp.float32))
        u = h[:, :GM_WIDTH]
        v = _ln_rows(h[:, GM_WIDTH:], lng_ref[...], lnb_ref[...])
        if emit_v:
            out_refs[1][rows_p, :] = v
        if chunk_len == 1:
            gated = u * (v * mixw_ref[...] + mixb_ref[...])
        else:
            causal = (lax.broadcasted_iota(jnp.int32, (chunk_len, chunk_len), 0)
                      >= lax.broadcasted_iota(jnp.int32, (chunk_len, chunk_len), 1))
            vb = v.astype(MXU_DTYPE)
            cols = []
            for g in range(GM_GROUPS):
                w = jnp.where(causal, mixw_ref[g], 0.0).astype(MXU_DTYPE)
                bias = mixb_ref[:, g:g + 1]
                lanes = slice(g * GM_GROUP_DIM, (g + 1) * GM_GROUP_DIM)
                rows = [jnp.dot(w, vb[c * chunk_len:(c + 1) * chunk_len, lanes],
                                preferred_element_type=jnp.float32) + bias
                        for c in range(tp // chunk_len)]
                cols.append(jnp.concatenate(rows, axis=0))
            gated = u * jnp.concatenate(cols, axis=1)
        y = ALPHA * x + jnp.dot(gated.astype(MXU_DTYPE), wout_ref[...], preferred_element_type=jnp.float32)
        out_refs[0][rows_p, :] = _ln_rows(y, g_ref[...], b_ref[...])


def _gmlp_block(x2d, seq_len, w_in, ln_g, ln_b, ws, bs, w_out, g, b, emit_v):
    m = x2d.shape[0]
    chunk_len = min(seq_len, CHUNK)
    if chunk_len == 1:
        tm, parts = m, 1
        mixw = jnp.repeat(ws[:, 0, 0], GM_GROUP_DIM)[None, :]
        mixb = jnp.repeat(bs[:, 0], GM_GROUP_DIM)[None, :]
    else:
        tm, parts = GM_ROW_TILE, GM_ROW_PARTS
        mixw = ws[:, :chunk_len, :chunk_len]
        mixb = bs[:, :chunk_len].T
    full = lambda a: pl.BlockSpec(a.shape, lambda i: (0,) * a.ndim, pipeline_mode=pl.Buffered(1))
    rows = lambda n: pl.BlockSpec((tm, n), lambda i: (i, 0))
    consts = [w_in.astype(MXU_DTYPE), ln_g.reshape(1, GM_WIDTH), ln_b.reshape(1, GM_WIDTH), mixw, mixb,
              w_out.astype(MXU_DTYPE), g.reshape(1, D_MODEL), b.reshape(1, D_MODEL)]
    out_specs = [rows(D_MODEL)] + ([rows(GM_WIDTH)] if emit_v else [])
    out_shape = [jax.ShapeDtypeStruct((m, D_MODEL), jnp.float32)] + (
        [jax.ShapeDtypeStruct((m, GM_WIDTH), jnp.float32)] if emit_v else [])
    return pl.pallas_call(
        functools.partial(_gmlp_kernel, chunk_len=chunk_len, emit_v=emit_v, parts=parts),
        grid=(m // tm,),
        in_specs=[rows(D_MODEL)] + [full(a) for a in consts],
        out_specs=out_specs,
        out_shape=out_shape,
        compiler_params=pltpu.CompilerParams(
            dimension_semantics=("parallel",),
            vmem_limit_bytes=V7X_VMEM_LIMIT_BYTES),
        name="gmlp_block",
    )(x2d, *consts)


SSM_ROW_TILE = 256
SSM_ROW_PARTS = 2
SSM_BC_DIM = SSM_GROUPS * SSM_STATE
SSM_DT_LANES = LANES


def _ssm_activate(xb, xbc, taps, wz_ref, wdt_ref, cw_ref, cb_ref, dtb_ref, z_ref, xs_ref, bm_ref, cm_ref, dt_ref):
    conv = cb_ref[...] + xbc * cw_ref[SSM_CONV - 1:SSM_CONV, :]
    for j in range(SSM_CONV - 1):
        conv = conv + taps[j] * cw_ref[j:j + 1, :]
    act = conv * jax.nn.sigmoid(conv)
    xs_ref[...] = act[:, :SSM_D_INNER]
    bm_ref[...] = act[:, SSM_D_INNER:SSM_D_INNER + SSM_BC_DIM].astype(bm_ref.dtype)
    cm_ref[...] = act[:, SSM_D_INNER + SSM_BC_DIM:].astype(cm_ref.dtype)
    z_ref[...] = jnp.dot(xb, wz_ref[...], preferred_element_type=jnp.float32)
    dt_ref[...] = jax.nn.softplus(jnp.dot(xb, wdt_ref[...], preferred_element_type=jnp.float32) + dtb_ref[...])


def _ssm_proj_seq_kernel(x_ref, halo_ref, cs_ref, wx_ref, wz_ref, wdt_ref, cw_ref, cb_ref, dtb_ref,
                         z_ref, xs_ref, bm_ref, cm_ref, dt_ref, tail_ref):
    i = pl.program_id(1)
    tm = x_ref.shape[0]
    tp = tm // SSM_ROW_PARTS
    prev = jnp.dot(halo_ref[...].astype(MXU_DTYPE), wx_ref[...], preferred_element_type=jnp.float32)
    prev = jnp.where(i == 0, cs_ref[0], prev)
    row = lax.broadcasted_iota(jnp.int32, (SUBLANES, 1), 0)
    for part in range(SSM_ROW_PARTS):
        rows_p = slice(part * tp, (part + 1) * tp)
        xb = x_ref[rows_p, :].astype(MXU_DTYPE)
        xbc = jnp.dot(xb, wx_ref[...], preferred_element_type=jnp.float32)
        taps = []
        for j in range(SSM_CONV - 1):
            back = SSM_CONV - 1 - j
            rolled = pltpu.roll(xbc, back, 0)
            top = jnp.where(row < back, pltpu.roll(prev, back, 0), rolled[:SUBLANES])
            taps.append(jnp.concatenate([top, rolled[SUBLANES:]], axis=0))
        _ssm_activate(xb, xbc, taps, wz_ref, wdt_ref, cw_ref, cb_ref, dtb_ref,
                      z_ref.at[rows_p, :], xs_ref.at[rows_p, :], bm_ref.at[rows_p, :], cm_ref.at[rows_p, :],
                      dt_ref.at[rows_p, :])
        prev = xbc[tp - SUBLANES:, :]
    tail_ref[0] = prev


def _ssm_proj_step_kernel(x_ref, st_ref, wx_ref, wz_ref, wdt_ref, cw_ref, cb_ref, dtb_ref,
                          z_ref, xs_ref, bm_ref, cm_ref, dt_ref, st_out_ref):
    xb = x_ref[...].astype(MXU_DTYPE)
    xbc = jnp.dot(xb, wx_ref[...], preferred_element_type=jnp.float32)
    taps = [st_ref[j] for j in range(SSM_CONV - 1)]
    _ssm_activate(xb, xbc, taps, wz_ref, wdt_ref, cw_ref, cb_ref, dtb_ref, z_ref, xs_ref, bm_ref, cm_ref, dt_ref)
    for j in range(SSM_CONV - 2):
        st_out_ref[j] = st_ref[j + 1]
    st_out_ref[SSM_CONV - 2] = xbc


def _ssm_project(x3d, conv_state, w_in, conv_w, conv_b, dt_bias):
    b_, t_, _ = x3d.shape
    m = b_ * t_
    w_z, w_x, w_dt = jnp.split(w_in, [SSM_D_INNER, SSM_D_INNER + SSM_CONV_DIM], axis=1)
    cast = lambda w: w.astype(MXU_DTYPE)
    consts = [cast(w_x), cast(w_z), cast(jnp.pad(w_dt, ((0, 0), (0, SSM_DT_LANES - SSM_HEADS)))),
              conv_w, conv_b.reshape(1, SSM_CONV_DIM),
              jnp.pad(dt_bias, (0, SSM_DT_LANES - SSM_HEADS)).reshape(1, SSM_DT_LANES)]
    sds = jax.ShapeDtypeStruct
    outs = [sds((m, SSM_D_INNER), jnp.float32), sds((m, SSM_D_INNER), jnp.float32),
            sds((m, SSM_BC_DIM), MXU_DTYPE), sds((m, SSM_BC_DIM), MXU_DTYPE), sds((m, SSM_DT_LANES), jnp.float32)]
    widths = [SSM_D_INNER, SSM_D_INNER, SSM_BC_DIM, SSM_BC_DIM, SSM_DT_LANES]
    params = dict(vmem_limit_bytes=V7X_VMEM_LIMIT_BYTES)
    x2d = x3d.reshape(m, D_MODEL)
    if t_ == 1:
        full = lambda a: pl.BlockSpec(a.shape, lambda i: (0,) * a.ndim)
        st = jnp.transpose(conv_state, (1, 0, 2))
        res = pl.pallas_call(
            _ssm_proj_step_kernel,
            grid=(1,),
            in_specs=[full(x2d), full(st)] + [full(a) for a in consts],
            out_specs=[pl.BlockSpec((m, w), lambda i: (0, 0)) for w in widths] + [full(st)],
            out_shape=outs + [sds(st.shape, jnp.float32)],
            compiler_params=pltpu.CompilerParams(dimension_semantics=("arbitrary",), **params),
            name="ssm_project_step",
        )(x2d, st, *consts)
        return list(res[:5]) + [jnp.transpose(res[5], (1, 0, 2))]
    tm = SSM_ROW_TILE
    nt = t_ // tm
    full = lambda a: pl.BlockSpec(a.shape, lambda b, i: (0,) * a.ndim)
    rows = lambda w: pl.BlockSpec((tm, w), lambda b, i: (b * nt + i, 0))
    halo = pl.BlockSpec((SUBLANES, D_MODEL), lambda b, i: (jnp.maximum((b * nt + i) * (tm // SUBLANES) - 1, 0), 0))
    cs8 = jnp.pad(conv_state, ((0, 0), (SUBLANES - (SSM_CONV - 1), 0), (0, 0)))
    tail = pl.BlockSpec((1, SUBLANES, SSM_CONV_DIM), lambda b, i: (b, 0, 0))
    res = pl.pallas_call(
        _ssm_proj_seq_kernel,
        grid=(b_, nt),
        in_specs=[rows(D_MODEL), halo, tail] + [full(a) for a in consts],
        out_specs=[rows(w) for w in widths] + [tail],
        out_shape=outs + [sds((b_, SUBLANES, SSM_CONV_DIM), jnp.float32)],
        compiler_params=pltpu.CompilerParams(dimension_semantics=("parallel", "arbitrary"), **params),
        name="ssm_project_seq",
    )(x2d, x2d, cs8, *consts)
    return list(res[:5]) + [res[5][:, SUBLANES - (SSM_CONV - 1):, :]]


def _ssm_gate_norm(y, xs, z, dskip, normg):
    yg = (y + xs * dskip) * (z * jax.nn.sigmoid(z))
    gw = SSM_D_INNER // SSM_GROUPS
    outs = []
    for g in range(SSM_GROUPS):
        part = yg[:, g * gw:(g + 1) * gw]
        ms = jnp.mean(part * part, axis=-1, keepdims=True)
        outs.append(part * lax.rsqrt(ms + LN_EPS))
    return jnp.concatenate(outs, axis=1) * normg


def _ssm_chunk_kernel(xs_ref, bm_ref, cm_ref, dt_ref, z_ref, aneg_ref, dskip_ref, normg_ref,
                      yg_ref, h_out_ref, h_ref, yT_ref, xe_ref):
    c = pl.program_id(1)
    l = xs_ref.shape[0]
    hd = SSM_HEAD_DIM

    @pl.when(c == 0)
    def _():
        h_ref[...] = jnp.zeros_like(h_ref)

    dot = lambda u, w: jnp.dot(u, w, preferred_element_type=jnp.float32)
    dt = dt_ref[...]
    a = dt * aneg_ref[...]
    r_i = lax.broadcasted_iota(jnp.int32, (l, l), 0)
    c_i = lax.broadcasted_iota(jnp.int32, (l, l), 1)
    tril = jnp.where(r_i >= c_i, 1.0, 0.0)
    hi = lax.Precision.HIGHEST
    acum = jnp.dot(tril, a, precision=hi, preferred_element_type=jnp.float32)
    acum_t = jnp.dot(a.T, tril.T, precision=hi, preferred_element_type=jnp.float32)
    dt_t = dt.T
    to_end_t = jnp.exp(acum_t[:, l - 1:l] - acum_t)
    from_start_t = jnp.exp(acum_t)
    chunk_decay = jnp.exp(acum[l - 1:l, :])
    upper = r_i <= c_i
    xs = xs_ref[...]
    for g in range(SSM_GROUPS):
        bm = bm_ref[:, g * SSM_STATE:(g + 1) * SSM_STATE]
        cm_t = cm_ref[:, g * SSM_STATE:(g + 1) * SSM_STATE].astype(jnp.float32).T.astype(MXU_DTYPE)
        cb_t = dot(bm, cm_t)
        h_in = h_ref[g * SSM_HPG:(g + 1) * SSM_HPG].reshape(SSM_HPG * hd, SSM_STATE)
        y_off = dot(h_in.astype(MXU_DTYPE), cm_t)
        for e in range(SSM_HPG):
            h = g * SSM_HPG + e
            if h % 2 == 0:
                xs_pair_t = xs[:, h * hd:(h + 2) * hd].T
            xdt_t = xs_pair_t[(h % 2) * hd:(h % 2 + 1) * hd] * dt_t[h:h + 1, :]
            seg = jnp.exp(jnp.where(upper, acum_t[h:h + 1, :] - acum[:, h:h + 1], -jnp.inf))
            y_diag = dot(xdt_t.astype(MXU_DTYPE), (cb_t * seg).astype(MXU_DTYPE))
            yT_ref[h * hd:(h + 1) * hd, :] = y_diag + y_off[e * hd:(e + 1) * hd] * from_start_t[h:h + 1, :]
            xe_ref[e * hd:(e + 1) * hd, :] = (xdt_t * to_end_t[h:h + 1, :]).astype(xe_ref.dtype)
        states = dot(xe_ref[...], bm)
        for e in range(SSM_HPG):
            h = g * SSM_HPG + e
            h_ref[h] = h_ref[h] * chunk_decay[:, h:h + 1] + states[e * hd:(e + 1) * hd]
    y = jnp.concatenate([yT_ref[i * l:(i + 1) * l, :].T for i in range(SSM_D_INNER // l)], axis=1)
    yg_ref[...] = _ssm_gate_norm(y, xs, z_ref[...], dskip_ref[...], normg_ref[...]).astype(yg_ref.dtype)

    @pl.when(c == pl.num_programs(1) - 1)
    def _():
        h_out_ref[0] = h_ref[...]


def _ssm_head_lanes(p):
    return jnp.pad(p.astype(jnp.float32), (0, SSM_DT_LANES - SSM_HEADS)).reshape(1, SSM_DT_LANES)


def _ssm_chunk_scan(b_, t_, xs, bm, cm, dt, z, a_log, d_skip, norm_g):
    l = SSM_CHUNK
    nc = t_ // l
    rows = lambda w: pl.BlockSpec((l, w), lambda b, c: (b * nc + c, 0))
    vec = lambda w: pl.BlockSpec((1, w), lambda b, c: (0, 0))
    aneg = _ssm_head_lanes(-jnp.exp(a_log.astype(jnp.float32)))
    dskip = jnp.repeat(d_skip, SSM_HEAD_DIM).reshape(1, SSM_D_INNER)
    yg, h_new = pl.pallas_call(
        _ssm_chunk_kernel,
        grid=(b_, nc),
        in_specs=[rows(SSM_D_INNER), rows(SSM_BC_DIM), rows(SSM_BC_DIM), rows(SSM_DT_LANES), rows(SSM_D_INNER),
                  vec(SSM_DT_LANES), vec(SSM_D_INNER), vec(SSM_D_INNER)],
        out_specs=[rows(SSM_D_INNER),
                   pl.BlockSpec((1, SSM_HEADS, SSM_HEAD_DIM, SSM_STATE), lambda b, c: (b, 0, 0, 0))],
        out_shape=[jax.ShapeDtypeStruct((b_ * t_, SSM_D_INNER), MXU_DTYPE),
                   jax.ShapeDtypeStruct((b_, SSM_HEADS, SSM_HEAD_DIM, SSM_STATE), jnp.float32)],
        scratch_shapes=[pltpu.VMEM((SSM_HEADS, SSM_HEAD_DIM, SSM_STATE), jnp.float32),
                        pltpu.VMEM((SSM_D_INNER, l), jnp.float32),
                        pltpu.VMEM((SSM_HPG * SSM_HEAD_DIM, l), MXU_DTYPE)],
        compiler_params=pltpu.CompilerParams(
            dimension_semantics=("parallel", "arbitrary"),
            vmem_limit_bytes=V7X_VMEM_LIMIT_BYTES),
        name="ssm_chunk_scan",
    )(xs, bm, cm, dt, z, aneg, dskip, norm_g.reshape(1, SSM_D_INNER))
    return yg, h_new


def _ssm_step_kernel(h0_ref, xs_ref, dt_ref, an_ref, bm_ref, cm_ref, y_ref, h_ref):
    h0 = h0_ref[0]
    dt = dt_ref[0]
    decay = jnp.exp(dt * an_ref[...])
    xdt = xs_ref[0] * dt
    bm = bm_ref[0].astype(jnp.float32)
    cm = cm_ref[0].astype(jnp.float32)
    h_ref[0] = h0 * decay + xdt * bm
    cb = jnp.sum(cm * bm, axis=-1, keepdims=True)
    y_ref[0] = cb * xdt + jnp.sum(cm * h0, axis=-1, keepdims=True) * decay


def _ssm_step(state, xs, bm, cm, dt, a_log):
    b_ = state.shape[0]
    per_head = lambda u: jnp.repeat(u.reshape(b_, SSM_GROUPS, 1, SSM_STATE), SSM_HPG, axis=1)
    xs4 = xs.reshape(b_, SSM_HEADS, SSM_HEAD_DIM, 1)
    dt4 = dt[:, :SSM_HEADS].reshape(b_, SSM_HEADS, 1, 1)
    an = (-jnp.exp(a_log.astype(jnp.float32))).reshape(SSM_HEADS, 1, 1)
    blk = lambda a: pl.BlockSpec((1,) + a.shape[1:], lambda b: (b, 0, 0, 0))
    args = [state.astype(jnp.float32), xs4, dt4, an, per_head(bm), per_head(cm)]
    y4, h_new = pl.pallas_call(
        _ssm_step_kernel,
        grid=(b_,),
        in_specs=[blk(args[0]), blk(xs4), blk(dt4), pl.BlockSpec(an.shape, lambda b: (0, 0, 0)),
                  blk(args[4]), blk(args[5])],
        out_specs=[blk(xs4), blk(args[0])],
        out_shape=[jax.ShapeDtypeStruct(xs4.shape, jnp.float32), jax.ShapeDtypeStruct(state.shape, jnp.float32)],
        compiler_params=pltpu.CompilerParams(
            dimension_semantics=("parallel",),
            vmem_limit_bytes=V7X_VMEM_LIMIT_BYTES),
        name="ssm_step",
    )(*args)
    return y4.reshape(b_, SSM_D_INNER), h_new


def _ssm_gate_norm_kernel(y_ref, xs_ref, z_ref, dskip_ref, normg_ref, o_ref):
    o_ref[...] = _ssm_gate_norm(y_ref[...], xs_ref[...], z_ref[...], dskip_ref[...], normg_ref[...]).astype(o_ref.dtype)


def _ssm_gate_norm_rows(y, xs, z, d_skip, norm_g):
    full = lambda a: pl.BlockSpec(a.shape, lambda i: (0,) * a.ndim)
    args = [y, xs, z, jnp.repeat(d_skip, SSM_HEAD_DIM).reshape(1, SSM_D_INNER), norm_g.reshape(1, SSM_D_INNER)]
    return pl.pallas_call(
        _ssm_gate_norm_kernel,
        grid=(1,),
        in_specs=[full(a) for a in args],
        out_specs=full(y),
        out_shape=jax.ShapeDtypeStruct(y.shape, MXU_DTYPE),
        name="ssm_gate_norm",
    )(*args)


def _mamba2_mixer_pallas(x3d, conv_state, ssm_state, w_in, conv_w, conv_b, dt_bias, a_log, d_skip, norm_g):
    b_, t_, _ = x3d.shape
    z, xs, bm, cm, dt, conv_new = _ssm_project(x3d, conv_state, w_in, conv_w, conv_b, dt_bias)
    if t_ == 1:
        y, h_new = _ssm_step(ssm_state, xs, bm, cm, dt, a_log)
        yg = _ssm_gate_norm_rows(y, xs, z, d_skip, norm_g)
    else:
        yg, h_new = _ssm_chunk_scan(b_, t_, xs, bm, cm, dt, z, a_log, d_skip, norm_g)
    return yg, conv_new, h_new.astype(ssm_state.dtype)


PAGES_PER_STEP = 32


def _sortable_key(score):
    bits = pltpu.bitcast(score, jnp.int32)
    return jnp.where(bits >= 0, bits, bits ^ jnp.int32(0x7FFFFFFF))


def _decode_score_kernel(pt_ref, iq_ref, iw_ref, ikn_ref, *rest):
    idx_refs, (key_ref, knew_ref) = rest[:-2], rest[-2:]
    iq, iw = iq_ref[0], iw_ref[0]
    weigh = lambda sc: jnp.sum(iw * jnp.maximum(sc, 0.0), axis=0, keepdims=True)
    ik_t = jnp.concatenate([r[0] for r in idx_refs], axis=1).astype(MXU_DTYPE)
    key_ref[0] = _sortable_key(weigh(jnp.dot(iq, ik_t, preferred_element_type=jnp.float32)))

    @pl.when(pl.program_id(1) == 0)
    def _():
        sc_new = jnp.sum(iq.astype(jnp.float32) * ikn_ref[0].astype(jnp.float32), axis=1, keepdims=True)
        knew_ref[0] = jnp.broadcast_to(_sortable_key(weigh(sc_new)), knew_ref.shape[1:])


def _decode_select_kernel(keys_ref, knew_ref, thr_ref, last_ref, *, topk, col_bits):
    keys = keys_ref[...]
    key_new = knew_ref[:, 0:1]
    past = keys.shape[1]
    col = lax.broadcasted_iota(jnp.int32, keys.shape, 1)

    def count(pred_past, pred_new):
        hit = jnp.where(pred_past(keys, col), 1.0, 0.0)
        cnt = _tree_sum([hit[:, l * LANES:(l + 1) * LANES] for l in range(past // LANES)])
        return jnp.sum(cnt, axis=1, keepdims=True) + jnp.where(pred_new(key_new), 1.0, 0.0)

    def at_least(cand):
        return count(lambda k, c: k >= cand, lambda k: k >= cand) >= float(topk)

    def two_bits(i, thr):
        hi = lax.shift_left(jnp.int32(1), jnp.int32(31) - 2 * i)
        lo = lax.shift_left(jnp.int32(1), jnp.int32(30) - 2 * i)
        c1, c2, c3 = thr ^ lo, thr ^ hi, thr ^ hi ^ lo
        return jnp.where(at_least(c3), c3, jnp.where(at_least(c2), c2, jnp.where(at_least(c1), c1, thr)))

    thr = lax.fori_loop(0, 16, two_bits, jnp.full(key_new.shape, INT32_MIN, jnp.int32))
    need = float(topk) - count(lambda k, c: k > thr, lambda k: k > thr)

    def col_body(i, last):
        cand = last | lax.shift_left(jnp.int32(1), jnp.int32(col_bits - 1) - i)
        ties = count(lambda k, c: (k == thr) & (c < cand), lambda k: (k == thr) & (jnp.int32(past) < cand))
        return jnp.where(ties < need, cand, last)

    n_tied = count(lambda k, c: k == thr, lambda k: k == thr)
    excess = jnp.max(jnp.where(n_tied > need, 1.0, 0.0), axis=0, keepdims=True)
    last_tie = lax.cond(excess[0, 0] > 0.0,
                        lambda: lax.fori_loop(0, col_bits, col_body, jnp.zeros(key_new.shape, jnp.int32)),
                        lambda: jnp.full(key_new.shape, 2 ** col_bits - 1, jnp.int32))
    thr_ref[...] = jnp.broadcast_to(thr, thr_ref.shape)
    last_ref[...] = jnp.broadcast_to(last_tie, last_ref.shape)


def _decode_attend_kernel(pt_ref, q_ref, kn_ref, vn_ref, key_ref, knew_ref, thr_ref, last_ref, *rest,
                          n_steps, pages):
    k_refs, v_refs = rest[:pages], rest[pages:2 * pages]
    o_ref, m_ref, l_ref, acc_ref = rest[2 * pages:]
    s = pl.program_id(1)
    nk = key_ref.shape[2]
    nt = (((1,), (1,)), ((), ()))
    thr, last_tie = thr_ref[0, :, 0:1], last_ref[0, :, 0:1]
    keep_mask = lambda key, col: (key > thr) | ((key == thr) & (col <= last_tie))
    gsz = ATT_HEADS // ATT_KV_HEADS
    q = q_ref[0]
    q_wide = jnp.concatenate([q] * ATT_KV_HEADS, axis=1)
    head_i = lax.broadcasted_iota(jnp.int32, q_wide.shape, 0)
    col_i = lax.broadcasted_iota(jnp.int32, q_wide.shape, 1)
    own_group = (col_i // ATT_HEAD_DIM) == (head_i // gsz)
    q_blk = jnp.where(own_group, q_wide, jnp.zeros_like(q_wide))

    def online_update(logits, weighted_values):
        m_old = m_ref[...]
        m_new = jnp.maximum(m_old, jnp.max(logits, axis=1, keepdims=True))
        p = jnp.exp(logits - m_new)
        alpha = jnp.exp(m_old - m_new)
        l_ref[...] = alpha * l_ref[...] + jnp.sum(p, axis=1, keepdims=True)
        acc_ref[...] = alpha * acc_ref[...] + weighted_values(p.astype(MXU_DTYPE))
        m_ref[...] = m_new

    @pl.when(s == 0)
    def _init():
        m_ref[...] = jnp.full(m_ref.shape, MASK_NEG, jnp.float32)
        l_ref[...] = jnp.zeros(l_ref.shape, jnp.float32)
        acc_ref[...] = jnp.zeros(acc_ref.shape, jnp.float32)

    col = lax.broadcasted_iota(jnp.int32, (1, nk), 1) + s * nk
    bias = jnp.where(keep_mask(key_ref[0], col), 0.0, MASK_NEG)
    k_t = jnp.concatenate([r[0] for r in k_refs], axis=1).astype(MXU_DTYPE)
    v_t = jnp.concatenate([r[0] for r in v_refs], axis=1).astype(MXU_DTYPE)
    online_update(jnp.dot(q_blk, k_t, preferred_element_type=jnp.float32) + bias,
                  lambda p: lax.dot_general(p, v_t, nt, preferred_element_type=jnp.float32))

    @pl.when(s == n_steps - 1)
    def _finish():
        keep_new = keep_mask(knew_ref[0, :, 0:1], jnp.int32(n_steps * nk))
        logit = jnp.sum(q_blk.astype(jnp.float32) * kn_ref[0].astype(jnp.float32), axis=1, keepdims=True)
        v_row = vn_ref[0].astype(jnp.float32)
        online_update(logit + jnp.where(keep_new, 0.0, MASK_NEG), lambda p: p.astype(jnp.float32) * v_row)
        out = jnp.where(own_group, acc_ref[...] / l_ref[...], 0.0)
        o = out[:, 0:ATT_HEAD_DIM]
        for g in range(1, ATT_KV_HEADS):
            o = o + out[:, g * ATT_HEAD_DIM:(g + 1) * ATT_HEAD_DIM]
        o_ref[0] = o.astype(o_ref.dtype)


def _dsa_decode_split(q, iq, iw, ik_new, k_new, v_new, cache_k, cache_v, cache_idx_k, page_table):
    b_, n_pages = page_table.shape
    n_pool, page = cache_k.shape[0], cache_k.shape[1]
    pages = PAGES_PER_STEP
    n_steps = n_pages // pages
    nk = pages * page
    past = n_pages * page
    ck = jnp.transpose(cache_k, (0, 2, 3, 1)).reshape(n_pool, ATT_KV_DIM, page)
    cv = jnp.transpose(cache_v, (0, 2, 3, 1)).reshape(n_pool, ATT_KV_DIM, page)
    cik = jnp.swapaxes(cache_idx_k, 1, 2)
    per_seq = lambda a: pl.BlockSpec((1,) + a.shape[1:], lambda b, s, pt: (b,) + (0,) * (a.ndim - 1))
    paged = lambda width, j: pl.BlockSpec((1, width, page), lambda b, s, pt: (pt[b, s * pages + j], 0, 0))
    key_blk = pl.BlockSpec((1, 1, nk), lambda b, s, pt: (b, 0, s))
    lane_blk = pl.BlockSpec((1, 1, LANES), lambda b, s, pt: (b, 0, 0))
    params = pltpu.CompilerParams(dimension_semantics=("parallel", "arbitrary"),
                                  vmem_limit_bytes=V7X_VMEM_LIMIT_BYTES)

    score_in = [iq.reshape(b_, IDX_HEADS, IDX_DIM), iw[:, :IDX_HEADS].reshape(b_, IDX_HEADS, 1),
                ik_new.astype(MXU_DTYPE).reshape(b_, 1, IDX_DIM)]
    keys, key_new = pl.pallas_call(
        _decode_score_kernel,
        grid_spec=pltpu.PrefetchScalarGridSpec(
            num_scalar_prefetch=1, grid=(b_, n_steps),
            in_specs=[per_seq(a) for a in score_in] + [paged(IDX_DIM, j) for j in range(pages)],
            out_specs=[key_blk, lane_blk]),
        out_shape=[jax.ShapeDtypeStruct((b_, 1, past), jnp.int32), jax.ShapeDtypeStruct((b_, 1, LANES), jnp.int32)],
        compiler_params=params,
        name="dsa_decode_score",
    )(page_table, *score_in, *([cik] * pages))

    whole = lambda shape: pl.BlockSpec(shape, lambda i: (0,) * len(shape))
    thr, last_tie = pl.pallas_call(
        functools.partial(_decode_select_kernel, topk=min(TOPK_MAX, (past + 1) // 4),
                          col_bits=max(1, past.bit_length())),
        grid=(1,),
        in_specs=[whole((b_, past)), whole((b_, LANES))],
        out_specs=[whole((b_, LANES)), whole((b_, LANES))],
        out_shape=[jax.ShapeDtypeStruct((b_, LANES), jnp.int32)] * 2,
        compiler_params=pltpu.CompilerParams(vmem_limit_bytes=V7X_VMEM_LIMIT_BYTES),
        name="dsa_decode_select",
    )(keys.reshape(b_, past), key_new.reshape(b_, LANES))

    attend_in = [q.reshape(b_, ATT_HEADS, ATT_HEAD_DIM), k_new.astype(MXU_DTYPE).reshape(b_, 1, ATT_KV_DIM),
                 v_new.astype(MXU_DTYPE).reshape(b_, 1, ATT_KV_DIM)]
    o = pl.pallas_call(
        functools.partial(_decode_attend_kernel, n_steps=n_steps, pages=pages),
        grid_spec=pltpu.PrefetchScalarGridSpec(
            num_scalar_prefetch=1, grid=(b_, n_steps),
            in_specs=[per_seq(a) for a in attend_in] + [key_blk, lane_blk, lane_blk, lane_blk]
            + [paged(ATT_KV_DIM, j) for j in range(pages)] * 2,
            out_specs=pl.BlockSpec((1, ATT_HEADS, ATT_HEAD_DIM), lambda b, s, pt: (b, 0, 0)),
            scratch_shapes=[pltpu.VMEM((ATT_HEADS, 1), jnp.float32), pltpu.VMEM((ATT_HEADS, 1), jnp.float32),
                            pltpu.VMEM((ATT_HEADS, ATT_KV_DIM), jnp.float32)]),
        out_shape=jax.ShapeDtypeStruct((b_, ATT_HEADS, ATT_HEAD_DIM), MXU_DTYPE),
        compiler_params=params,
        name="dsa_decode_attend",
    )(page_table, *attend_in, keys, key_new, thr.reshape(b_, 1, LANES), last_tie.reshape(b_, 1, LANES),
      *([ck] * pages), *([cv] * pages))
    return o.reshape(b_, ATT_Q_DIM)


def _dsa_sample_pallas(x3d, cache_k, cache_v, cache_idx_k, page_table, w_in, kn_g, kn_b):
    b_, t_, _ = x3d.shape
    past = page_table.shape[1] * cache_k.shape[1]
    pos = jnp.full((b_,), past, jnp.int32)
    q, iq, v, _, iw, kT, _, ikT, _ = _dsa_project(x3d.reshape(1, b_, D_MODEL), pos, w_in, kn_g, kn_b)
    k = _untranspose_groups(kT)[0]
    ik = _untranspose_groups(ikT)[0]
    o = _dsa_decode_split(q, iq, iw, ik, k, v, cache_k, cache_v, cache_idx_k, page_table)
    kv4 = lambda u: u.reshape(b_, t_, ATT_KV_HEADS, ATT_HEAD_DIM)
    return o, kv4(k), kv4(v), ik.reshape(b_, t_, IDX_DIM)


def kernel(x_prompt, x_sample, state_ssm_conv, state_ssm, cache_k, cache_v, cache_idx_k, state_rwkv_shift, state_rwkv_wkv, page_table, p_prompt, p_sample, ln_g, ln_b, ffn_w_up, ffn_w_down, ple_w_p, ple_w_g, ple_b_g, gm_w_in, gm_ln_g, gm_ln_b, gm_ws, gm_bs, gm_w_out, ssm_w_in, ssm_conv_w, ssm_conv_b, ssm_dt_bias, ssm_a_log, ssm_d, ssm_norm_g, ssm_w_out, att_w_in, att_kn_g, att_kn_b, att_w_out, rw_mu, rw_w_r, rw_w_k, rw_w_v, rw_w_o, rw_w0, rw_w1, rw_w2, rw_a0, rw_a1, rw_a2, rw_g1, rw_g2, rw_k_k, rw_k_a, rw_r_k, rw_gn_g, rw_gn_b):
    bp, tp, _ = x_prompt.shape
    bs_, ts, _ = x_sample.shape
    bf = lambda w: w.astype(jnp.bfloat16)
    w_up_bf, w_down_bf = bf(ffn_w_up), bf(ffn_w_down)
    ple_wp_bf, ple_wg_bf = bf(ple_w_p), bf(ple_w_g)
    pp3 = p_prompt.reshape(DEPTH, bp * tp, PLE_DIM)
    ps3 = p_sample.reshape(DEPTH, bs_ * ts, PLE_DIM)

    yp = x_prompt.reshape(bp * tp, D_MODEL)
    ys = x_sample.reshape(bs_ * ts, D_MODEL)
    r3p = lambda t: t.reshape(bp, tp, -1)
    r3s = lambda t: t.reshape(bs_, ts, -1)
    f2 = lambda t: t.reshape(-1, t.shape[-1])

    for i in range(DEPTH):
        yp = _ffn_sub(yp, w_up_bf, w_down_bf, i, 0, ln_g[i, 0], ln_b[i, 0])
        ys = _ffn_sub(ys, w_up_bf, w_down_bf, i, 0, ln_g[i, 0], ln_b[i, 0])
        m = i % N_MIXERS
        if m == 0:
            gm_args = (gm_w_in, gm_ln_g, gm_ln_b, gm_ws, gm_bs, gm_w_out, ln_g[i, 1], ln_b[i, 1])
            yp, = _gmlp_block(yp, tp, *gm_args, False)
            ys, gm_v_s = _gmlp_block(ys, ts, *gm_args, True)
            gm_v_s = r3s(gm_v_s)
        elif m == 1:
            ssm_args = (ssm_w_in, ssm_conv_w, ssm_conv_b, ssm_dt_bias, ssm_a_log, ssm_d, ssm_norm_g)
            hp, conv_p, ssm_p = _mamba2_mixer_pallas(
                r3p(yp), jnp.zeros((bp, SSM_CONV - 1, SSM_CONV_DIM), yp.dtype),
                jnp.zeros((bp, SSM_HEADS, SSM_HEAD_DIM, SSM_STATE), yp.dtype), *ssm_args)
            hs, conv_s, ssm_s = _mamba2_mixer_pallas(r3s(ys), state_ssm_conv, state_ssm, *ssm_args)
            w_out = bf(ssm_w_out)
        elif m == 2:
            hp, k_p, v_p, ik_p = _dsa_prompt_pallas(r3p(yp), att_w_in, att_kn_g, att_kn_b)
            hs, k_s, v_s, ik_s = _dsa_sample_pallas(r3s(ys), cache_k, cache_v, cache_idx_k, page_table,
                                                    att_w_in, att_kn_g, att_kn_b)
            w_out = bf(att_w_out)
        else:
            rw_args = (rw_mu, rw_w_r, rw_w_k, rw_w_v, rw_w0, rw_w1, rw_w2, rw_a0, rw_a1, rw_a2,
                       rw_g1, rw_g2, rw_k_k, rw_k_a, rw_r_k, rw_gn_g, rw_gn_b)
            hp, gate_p, sh_p, wkv_p = _rwkv7_mixer_pallas(
                r3p(yp), jnp.zeros((bp, D_MODEL), yp.dtype),
                jnp.zeros((bp, RW_HEADS, RW_HEAD, RW_HEAD), yp.dtype), *rw_args)
            hs, gate_s, sh_s, wkv_s = _rwkv7_mixer_pallas(r3s(ys), state_rwkv_shift, state_rwkv_wkv, *rw_args)
            w_out = bf(rw_w_o)
        if m == 3:
            yp = _proj_gate_post_norm(yp, hp, gate_p, w_out, ln_g[i, 1], ln_b[i, 1])
            ys = _proj_gate_post_norm(ys, hs, gate_s, w_out, ln_g[i, 1], ln_b[i, 1])
        elif m != 0:
            yp = _proj_post_norm(yp, f2(hp), w_out, ln_g[i, 1], ln_b[i, 1])
            ys = _proj_post_norm(ys, f2(hs), w_out, ln_g[i, 1], ln_b[i, 1])
        yp = _ffn_sub(yp, w_up_bf, w_down_bf, i, 1, ln_g[i, 2], ln_b[i, 2], (pp3, ple_wp_bf, ple_wg_bf, ple_b_g))
        ys = _ffn_sub(ys, w_up_bf, w_down_bf, i, 1, ln_g[i, 2], ln_b[i, 2], (ps3, ple_wp_bf, ple_wg_bf, ple_b_g))

    return (r3p(yp), r3s(ys), gm_v_s, conv_p, ssm_p, conv_s, ssm_s, k_p, v_p, ik_p, k_s, v_s, ik_s,
            sh_p, wkv_p, sh_s, wkv_s)
```
